```python
import jax, jax.numpy as jnp
from jax import lax
import numpy as np

D_MODEL = 1024
BATCH = 2
SEQ = 8192
DEPTH = 1
DEC_BATCH = 128
DEC_SEQ = 4
PAST_LEN = 16384
PAGE_SIZE = 128

NH_M = 4
DHK_M = 128
DHV_M = 256
DQK_M = NH_M * DHK_M
DV_M = NH_M * DHV_M
CHUNK_M = 128
NH_A = 16
NKV_A = 4
HD_A = 64
WINDOW = 128
DQ_A = NH_A * HD_A
DKV_A = NKV_A * HD_A
D_FF = 2816
CONV_W = 3
N_MOD = 6
ALPHA = (2 * DEPTH) ** 0.25
BETA = (8 * DEPTH) ** -0.25
LN_EPS = 1e-5
IN_SPLITS = (DQK_M, DQK_M, DV_M, NH_M, NH_M, DV_M, DQ_A, DKV_A, DKV_A, D_MODEL, D_MODEL)
D_IN = DQK_M * 2 + DV_M * 2 + NH_M * 2 + DQ_A + DKV_A * 2 + D_MODEL * 2

kernel_name = 'hybrid_mlstm_swa_convffn_step'


def _ln(x, g=None, b=None):
    xf = x.astype(jnp.float32)
    mu = jnp.mean(xf, axis=-1, keepdims=True)
    var = jnp.mean(jnp.square(xf - mu), axis=-1, keepdims=True)
    y = (xf - mu) * lax.rsqrt(var + LN_EPS)
    if g is not None:
        y = y * g.astype(jnp.float32) + b.astype(jnp.float32)
    return y.astype(x.dtype)


def _split(z, sizes):
    out, off = [], 0
    for s in sizes:
        out.append(z[..., off:off + s])
        off += s
    return out


def _alibi_slopes():
    return jnp.exp2(-8.0 * jnp.arange(1, NH_A + 1, dtype=jnp.float32) / NH_A)


def _to_chunks(a, nc, t):
    b = a.shape[0]
    a = a.reshape((b, nc, t) + a.shape[2:])
    return jnp.transpose(a, (1, 0, 3, 2) + tuple(range(4, a.ndim)))


def _mlstm(q, k, v, ig, lf, C0, n0, m0):
    B, L = q.shape[:2]
    T = CHUNK_M if L % CHUNK_M == 0 else L
    NC = L // T
    f32 = jnp.float32
    qc = _to_chunks(q.astype(f32) * DHK_M ** -0.5, NC, T)
    kc = _to_chunks(k.astype(f32), NC, T)
    vc = _to_chunks(v.astype(f32), NC, T)
    ic = _to_chunks(ig, NC, T)
    fc = _to_chunks(lf, NC, T)
    causal = jnp.tril(jnp.ones((T, T), dtype=bool))

    def step(carry, inp):
        C, n, m = carry
        qx, kx, vx, ix, fx = inp
        b = jnp.cumsum(fx, axis=-1)
        dmat = b[..., :, None] - b[..., None, :] + ix[..., None, :]
        dmat = jnp.where(causal, dmat, -jnp.inf)
        inter = b + m[..., None]
        mt = jnp.maximum(inter, jnp.max(dmat, axis=-1))
        smat = jnp.einsum('bhtd,bhsd->bhts', qx, kx) * jnp.exp(dmat - mt[..., None])
        a_in = jnp.exp(inter - mt)
        num = jnp.einsum('bhts,bhsv->bhtv', smat, vx) + a_in[..., None] * jnp.einsum('bhvd,bhtd->bhtv', C, qx)
        den = jnp.sum(smat, axis=-1) + a_in * jnp.einsum('bhd,bhtd->bht', n, qx)
        h = num / jnp.maximum(jnp.abs(den), jnp.exp(-mt))[..., None]
        bT = b[..., -1]
        wk = bT[..., None] - b + ix
        m_new = jnp.maximum(bT + m, jnp.max(wk, axis=-1))
        decay = jnp.exp(bT + m - m_new)
        ws = jnp.exp(wk - m_new[..., None])
        C_new = decay[..., None, None] * C + jnp.einsum('bhs,bhsv,bhsd->bhvd', ws, vx, kx)
        n_new = decay[..., None] * n + jnp.einsum('bhs,bhsd->bhd', ws, kx)
        return (C_new, n_new, m_new), h

    (C, n, m), h = lax.scan(step, (C0.astype(f32), n0.astype(f32), m0.astype(f32)), (qc, kc, vc, ic, fc))
    h = jnp.transpose(h, (1, 0, 3, 2, 4)).reshape(B, L, NH_M, DHV_M)
    return h, C, n, m


def _swa(q, k, v, kbuf, vbuf, pos0, sinks):
    B, L = q.shape[:2]
    W = kbuf.shape[1]
    Qb = WINDOW if L % WINDOW == 0 else L
    NB = L // Qb
    G = NH_A // NKV_A
    f32 = jnp.float32
    kx = jnp.concatenate([kbuf.astype(k.dtype), k], axis=1)
    vx = jnp.concatenate([vbuf.astype(v.dtype), v], axis=1)
    idx = jnp.arange(NB)[:, None] * Qb + jnp.arange(W + Qb)[None, :]
    kb = kx[:, idx].astype(f32)
    vb = vx[:, idx].astype(f32)
    qb = q.reshape(B, NB, Qb, NKV_A, G, HD_A).astype(f32) * HD_A ** -0.5
    qpos = pos0 + jnp.arange(L).reshape(NB, Qb)
    kpos = pos0 - W + idx
    delta = qpos[:, :, None] - kpos[:, None, :]
    valid = (delta >= 0) & (delta < WINDOW) & (kpos[:, None, :] >= 0)
    slopes = _alibi_slopes().reshape(NKV_A, G)[:, :, None, None]
    s = jnp.einsum('bnqkgd,bnskd->bnkgqs', qb, kb)
    s = s - slopes * delta[:, None, None].astype(f32)
    s = jnp.where(valid[:, None, None], s, -jnp.inf)
    sink = sinks.astype(f32).reshape(NKV_A, G)[:, :, None, None]
    mx = jnp.maximum(jnp.max(s, axis=-1, keepdims=True), sink)
    p = jnp.exp(s - mx)
    den = jnp.sum(p, axis=-1, keepdims=True) + jnp.exp(sink - mx)
    o = jnp.einsum('bnkgqs,bnskd->bnqkgd', p / den, vb).reshape(B, L, NH_A * HD_A)
    return o, kx[:, -W:], vx[:, -W:]


def _conv_ffn(h, cbuf, w_up, b_up, conv_w, conv_b, w_down, b_down):
    L = h.shape[1]
    u = h @ w_up + b_up
    ux = jnp.concatenate([cbuf.astype(u.dtype), u], axis=1)
    y = conv_b + ux[:, 0:L] * conv_w[0]
    for j in range(1, CONV_W):
        y = y + ux[:, j:j + L] * conv_w[j]
    a, g = y[..., :D_FF], y[..., D_FF:]
    out = (jax.nn.gelu(a) * g) @ w_down + b_down
    return out, ux[:, L:]


def _layer(x, c, pos0, C0, n0, m0, kbuf, vbuf, cbuf,
           w_ada, b_ada, w_in, b_in, mlstm_norm_w, attn_sinks,
           w_branch_m, w_branch_a, w_out, ln1_g, ln1_b,
           w_up, b_up, conv_w, conv_b, w_down, b_down, ln2_g, ln2_b):
    B, L, _ = x.shape
    f32 = jnp.float32
    mod = (jax.nn.silu(c) @ w_ada + b_ada)[:, None, :]
    sh1, sc1, g1, sh2, sc2, g2 = jnp.split(mod, N_MOD, axis=-1)
    h = _ln(x) * (1.0 + sc1) + sh1
    z = h @ w_in + b_in
    qm, km, vm, ig, fg, og, qa, ka, va, gm, ga = _split(z, IN_SPLITS)
    hm, C, n, m = _mlstm(qm.reshape(B, L, NH_M, DHK_M), km.reshape(B, L, NH_M, DHK_M),
                         vm.reshape(B, L, NH_M, DHV_M), ig.astype(f32),
                         jax.nn.log_sigmoid(fg.astype(f32)), C0, n0, m0)
    hm = _ln(hm) * mlstm_norm_w.astype(f32).reshape(NH_M, DHV_M)
    hm = (hm.reshape(B, L, DV_M) * jax.nn.sigmoid(og.astype(f32))).astype(x.dtype)
    ha, k_new, v_new = _swa(qa.reshape(B, L, NH_A, HD_A), ka.reshape(B, L, NKV_A, HD_A),
                            va.reshape(B, L, NKV_A, HD_A), kbuf, vbuf, pos0, attn_sinks)
    merged = (jax.nn.sigmoid(gm) * (hm @ w_branch_m)
              + jax.nn.sigmoid(ga) * (ha.astype(x.dtype) @ w_branch_a))
    x = _ln(ALPHA * x + g1 * (merged @ w_out), ln1_g, ln1_b)
    h2 = _ln(x) * (1.0 + sc2) + sh2
    f, cbuf_new = _conv_ffn(h2, cbuf, w_up, b_up, conv_w, conv_b, w_down, b_down)
    x = _ln(ALPHA * x + g2 * f, ln2_g, ln2_b)
    dt = x.dtype
    return x, (C.astype(dt), n.astype(dt), m.astype(dt), k_new, v_new, cbuf_new)


def setup_inputs(seed: int = 0) -> dict:
    key = jax.random.key(seed)
    ks = iter(jax.random.split(key, 40))

    def nrm(shape, scale=1.0):
        return jax.random.normal(next(ks), shape, jnp.float32) * scale

    D = D_MODEL
    F2 = 2 * D_FF
    WB = min(WINDOW, PAST_LEN)
    f_off = 2 * DQK_M + DV_M + NH_M
    b_in = nrm((DEPTH, D_IN), 0.02).at[:, f_off:f_off + NH_M].add(3.0)
    return {
        'x_prompt': nrm((BATCH, SEQ, D)),
        'x_sample': nrm((DEC_BATCH, DEC_SEQ, D)),
        'c_prompt': nrm((BATCH, D)),
        'c_sample': nrm((DEC_BATCH, D)),
        'state_mlstm_C': nrm((DEPTH, DEC_BATCH, NH_M, DHV_M, DHK_M), 0.1),
        'state_mlstm_n': jnp.abs(nrm((DEPTH, DEC_BATCH, NH_M, DHK_M))),
        'state_mlstm_m': nrm((DEPTH, DEC_BATCH, NH_M)),
        'cache_k_win': nrm((DEPTH, DEC_BATCH, WB, NKV_A, HD_A)),
        'cache_v_win': nrm((DEPTH, DEC_BATCH, WB, NKV_A, HD_A)),
        'state_ffn_conv': nrm((DEPTH, DEC_BATCH, CONV_W - 1, F2), 0.5),
        'w_ada': nrm((DEPTH, D, N_MOD * D), D ** -0.5),
        'b_ada': nrm((DEPTH, N_MOD * D), 0.02),
        'w_in': nrm((DEPTH, D, D_IN), D ** -0.5),
        'b_in': b_in,
        'mlstm_norm_w': 1.0 + nrm((DEPTH, DV_M), 0.02),
        'attn_sinks': nrm((DEPTH, NH_A)),
        'w_branch_m': nrm((DEPTH, DV_M, D), BETA * DV_M ** -0.5),
        'w_branch_a': nrm((DEPTH, DQ_A, D), BETA * DQ_A ** -0.5),
        'w_out': nrm((DEPTH, D, D), BETA * D ** -0.5),
        'ln1_g': 1.0 + nrm((DEPTH, D), 0.02),
        'ln1_b': nrm((DEPTH, D), 0.02),
        'w_up': nrm((DEPTH, D, F2), D ** -0.5),
        'b_up': nrm((DEPTH, F2), 0.02),
        'conv_w': nrm((DEPTH, CONV_W, F2), CONV_W ** -0.5),
        'conv_b': nrm((DEPTH, F2), 0.02),
        'w_down': nrm((DEPTH, D_FF, D), BETA * D_FF ** -0.5),
        'b_down': nrm((DEPTH, D), 0.02),
        'ln2_g': 1.0 + nrm((DEPTH, D), 0.02),
        'ln2_b': nrm((DEPTH, D), 0.02),
    }


def reference(x_prompt, x_sample, c_prompt, c_sample, state_mlstm_C, state_mlstm_n, state_mlstm_m,
              cache_k_win, cache_v_win, state_ffn_conv, w_ada, b_ada, w_in, b_in, mlstm_norm_w,
              attn_sinks, w_branch_m, w_branch_a, w_out, ln1_g, ln1_b, w_up, b_up, conv_w, conv_b,
              w_down, b_down, ln2_g, ln2_b):
    params = (w_ada, b_ada, w_in, b_in, mlstm_norm_w, attn_sinks, w_branch_m, w_branch_a, w_out,
              ln1_g, ln1_b, w_up, b_up, conv_w, conv_b, w_down, b_down, ln2_g, ln2_b)
    Bp = x_prompt.shape[0]
    dt = x_prompt.dtype
    yp, ys = x_prompt, x_sample
    new_p, new_s = [], []
    for l in range(DEPTH):
        wl = [w[l] for w in params]
        yp, sp = _layer(yp, c_prompt, 0,
                        jnp.zeros((Bp, NH_M, DHV_M, DHK_M), dt), jnp.zeros((Bp, NH_M, DHK_M), dt),
                        jnp.zeros((Bp, NH_M), dt), jnp.zeros((Bp, WINDOW, NKV_A, HD_A), dt),
                        jnp.zeros((Bp, WINDOW, NKV_A, HD_A), dt), jnp.zeros((Bp, CONV_W - 1, 2 * D_FF), dt),
                        *wl)
        ys, ss = _layer(ys, c_sample, PAST_LEN, state_mlstm_C[l], state_mlstm_n[l], state_mlstm_m[l],
                        cache_k_win[l], cache_v_win[l], state_ffn_conv[l], *wl)
        new_p.append(sp)
        new_s.append(ss)
    p_C, p_n, p_m, p_k, p_v, p_conv = [jnp.stack(a) for a in zip(*new_p)]
    s_C, s_n, s_m, s_k, s_v, s_conv = [jnp.stack(a) for a in zip(*new_s)]
    return (yp, ys, p_C, p_n, p_m, p_k, p_v, p_conv, s_C, s_n, s_m, s_k, s_v, s_conv)
```

```python
import functools

import jax
import jax.numpy as jnp
from jax import lax
from jax.experimental import pallas as pl
from jax.experimental.pallas import tpu as pltpu

F32 = jnp.float32
BF16 = jnp.bfloat16

D_MODEL = 1024
NH_M, DHK_M, DHV_M = 4, 128, 256
DQK_M, DV_M = NH_M * DHK_M, NH_M * DHV_M
NH_A, NKV_A, HD_A = 16, 4, 64
GROUP_A = NH_A // NKV_A
WINDOW = 128
DQ_A, DKV_A = NH_A * HD_A, NKV_A * HD_A
D_FF = 2816
CONV_W = 3
N_MOD = 6
LN_EPS = 1e-5
CHUNK = 128
NEG = -1e30
LANES = 128
SUBLANES = 8
FF_CHUNK = 256
VMEM_LIMIT = 56 * 1024 * 1024


def _ln(x):
    mu = jnp.mean(x, axis=-1, keepdims=True)
    xc = x - mu
    var = jnp.mean(xc * xc, axis=-1, keepdims=True)
    return xc * lax.rsqrt(var + LN_EPS)


def _dot(a, b):
    return jnp.dot(a, b, preferred_element_type=F32)


def _dot_nt(a, b):
    return lax.dot_general(a, b, (((1,), (1,)), ((), ())), preferred_element_type=F32)


def _dot_tn(a, b):
    return lax.dot_general(a, b, (((0,), (0,)), ((), ())), preferred_element_type=F32)


def _const_spec(shape):
    nd = len(shape)
    return pl.BlockSpec(shape, lambda *_: (0,) * nd, pipeline_mode=pl.Buffered(1))


def _params(n_grid):
    return pltpu.CompilerParams(dimension_semantics=("arbitrary",) * n_grid,
                                vmem_limit_bytes=VMEM_LIMIT)


def _ada_body(c_ref, w_ref, b_ref, o_ref):
    c = c_ref[...]
    s = (c * jax.nn.sigmoid(c)).astype(BF16)
    o_ref[...] = _dot(s, w_ref[...].astype(BF16)) + b_ref[...]


def _ada(c, w_ada, b_ada):
    rows = c.shape[0]
    n_out = w_ada.shape[1]
    bn = 512
    return pl.pallas_call(
        _ada_body,
        out_shape=jax.ShapeDtypeStruct((rows, n_out), F32),
        grid=(n_out // bn,),
        in_specs=[pl.BlockSpec((rows, D_MODEL), lambda j: (0, 0)),
                  pl.BlockSpec((D_MODEL, bn), lambda j: (0, j)),
                  pl.BlockSpec((1, bn), lambda j: (0, j))],
        out_specs=pl.BlockSpec((rows, bn), lambda j: (0, j)),
        compiler_params=_params(1),
        name="ada",
    )(c, w_ada, b_ada.reshape(1, n_out))


_O_QM, _O_KM, _O_VM = 0, DQK_M, 2 * DQK_M
_O_QA = _O_VM + DV_M
_O_KA = _O_QA + DQ_A
_O_VA = _O_KA + DKV_A
_O_G = _O_VA + DKV_A
N_IN1 = _O_G + LANES


def _inproj_body(x_ref, sh_ref, sc_ref, w_ref, b_ref,
                 qk_ref, v_ref, g_ref, qa_ref, ka_ref, va_ref):
    h = (_ln(x_ref[...]) * (1.0 + sc_ref[...]) + sh_ref[...]).astype(BF16)

    def proj(off, n):
        return _dot(h, w_ref[:, off:off + n]) + b_ref[:, off:off + n]

    qk_ref[:, :DQK_M] = (proj(_O_QM, DQK_M) * DHK_M ** -0.5).astype(qk_ref.dtype)
    qk_ref[:, DQK_M:] = proj(_O_KM, DQK_M).astype(qk_ref.dtype)
    v_ref[...] = proj(_O_VM, DV_M).astype(v_ref.dtype)
    qa_ref[...] = (proj(_O_QA, DQ_A) * HD_A ** -0.5).astype(qa_ref.dtype)
    ka_ref[...] = proj(_O_KA, DKV_A)
    va_ref[...] = proj(_O_VA, DKV_A)
    g_ref[...] = proj(_O_G, LANES)


def _tok_spec(tm, n):
    return pl.BlockSpec((None, tm, n), lambda b, i: (b, i, 0))


def _mod_spec(rows, tm):
    if rows == 1:
        return pl.BlockSpec((None, 1, D_MODEL), lambda b, i: (b, 0, 0))
    return pl.BlockSpec((None, tm, D_MODEL), lambda b, i: (b, i, 0))


def _inproj(x, sh, sc, w1, b1, tm, act_dtype):
    bsz, seq, _ = x.shape
    mrows = sh.shape[1]
    outs = [(2 * DQK_M, act_dtype), (DV_M, act_dtype), (LANES, F32),
            (DQ_A, act_dtype), (DKV_A, F32), (DKV_A, F32)]
    return pl.pallas_call(
        _inproj_body,
        out_shape=[jax.ShapeDtypeStruct((bsz, seq, n), dt) for n, dt in outs],
        grid=(bsz, seq // tm),
        in_specs=[_tok_spec(tm, D_MODEL), _mod_spec(mrows, tm), _mod_spec(mrows, tm),
                  _const_spec((D_MODEL, N_IN1)), _const_spec((1, N_IN1))],
        out_specs=[_tok_spec(tm, n) for n, _ in outs],
        compiler_params=_params(2),
        name="inproj",
    )(x, sh, sc, w1, b1)


def _zero_pads_once(pads):
    @pl.when((pl.program_id(0) == 0) & (pl.program_id(1) == 0))
    def _():
        for p in pads:
            p[...] = jnp.zeros(p.shape, p.dtype)


def _pad_rows(ref, scr, t_in):
    scr[pl.ds(0, t_in), :] = ref[...].astype(scr.dtype)
    return scr[...]


def _mlstm_body(t_in, qk_ref, v_ref, g_ref, c0_ref, n0_ref, m0_ref, nw_ref,
                h_ref, c_ref, n_ref, m_ref, *pads):
    @pl.when(pl.program_id(1) == 0)
    def _():
        c_ref[...] = c0_ref[...]
        n_ref[...] = n0_ref[...]
        m_ref[...] = m0_ref[...]

    if t_in == CHUNK:
        qk, v, g = qk_ref[...], v_ref[...], g_ref[...]
    else:
        _zero_pads_once(pads)
        qk = _pad_rows(qk_ref, pads[0], t_in)
        v = _pad_rows(v_ref, pads[1], t_in)
        g = _pad_rows(g_ref, pads[2], t_in)
    qk = qk.astype(BF16)
    v = v.astype(BF16)

    row = lax.broadcasted_iota(jnp.int32, (CHUNK, LANES), 0)
    col = lax.broadcasted_iota(jnp.int32, (CHUNK, LANES), 1)
    lf = jax.nn.log_sigmoid(g)
    ig = g
    if t_in != CHUNK:
        lf = jnp.where(row < t_in, lf, 0.0)
        ig = jnp.where(row < t_in, ig, NEG)
    b = lf
    d = 1
    while d < CHUNK:
        b = b + jnp.where(row >= d, pltpu.roll(b, d, axis=0), 0.0)
        d *= 2
    gates = jnp.where(col < NH_M, ig, b)
    gates_t = gates.T
    causal = col <= row

    m_all = m_ref[...]
    lane1 = lax.broadcasted_iota(jnp.int32, (1, LANES), 1)
    outs = []
    for hd in range(NH_M):
        i_col, b_col = gates[:, hd:hd + 1], gates[:, NH_M + hd:NH_M + hd + 1]
        i_row, b_row = gates_t[hd:hd + 1, :], gates_t[NH_M + hd:NH_M + hd + 1, :]
        m_prev = m_all[:, hd:hd + 1]
        q = qk[:, hd * DHK_M:(hd + 1) * DHK_M]
        k = qk[:, DQK_M + hd * DHK_M:DQK_M + (hd + 1) * DHK_M]
        vh = v[:, hd * DHV_M:(hd + 1) * DHV_M]
        c_old = c_ref[hd]
        n_old = n_ref[hd:hd + 1, :]

        dmat = jnp.where(causal, b_col - b_row + i_row, NEG)
        inter = b_col + m_prev
        mt = jnp.maximum(inter, jnp.max(dmat, axis=-1, keepdims=True))
        smat = _dot_nt(q, k) * jnp.exp(dmat - mt)
        a_in = jnp.exp(inter - mt)
        num = _dot(smat.astype(BF16), vh) + a_in * _dot_nt(q, c_old.astype(BF16))
        qn = jnp.sum(q.astype(F32) * n_old.astype(BF16).astype(F32), axis=-1, keepdims=True)
        den = jnp.sum(smat, axis=-1, keepdims=True) + a_in * qn
        hh = num / jnp.maximum(jnp.abs(den), jnp.exp(-mt))
        outs.append(_ln(hh) * nw_ref[:, hd * DHV_M:(hd + 1) * DHV_M])

        b_last = b_row[:, CHUNK - 1:CHUNK]
        wk_col = b_last - b_col + i_col
        wk_row = b_last - b_row + i_row
        m_new = jnp.maximum(b_last + m_prev, jnp.max(wk_row, axis=-1, keepdims=True))
        decay = jnp.exp(b_last + m_prev - m_new)
        kw = k.astype(F32) * jnp.exp(wk_col - m_new)
        c_ref[hd] = decay * c_old + _dot_tn(vh, kw.astype(BF16))
        n_ref[hd:hd + 1, :] = decay * n_old + jnp.sum(kw, axis=0, keepdims=True)
        m_all = jnp.where(lane1 == hd, m_new, m_all)
    m_ref[...] = m_all
    hn = jnp.concatenate(outs, axis=1)
    h_ref[...] = hn if t_in == CHUNK else hn[:t_in]


def _mlstm(qk, v, g, c0, n0, m0, norm_w, t_in):
    bsz, seq, _ = qk.shape
    nc = seq // t_in
    blk = lambda n: pl.BlockSpec((None, t_in, n), lambda b, c: (b, c, 0))
    st_c = pl.BlockSpec((None, NH_M, DHV_M, DHK_M), lambda b, c: (b, 0, 0, 0))
    st_n = pl.BlockSpec((None, NH_M, DHK_M), lambda b, c: (b, 0, 0))
    st_m = pl.BlockSpec((None, 1, LANES), lambda b, c: (b, 0, 0))
    pads = []
    if t_in != CHUNK:
        pads = [pltpu.VMEM((CHUNK, 2 * DQK_M), F32), pltpu.VMEM((CHUNK, DV_M), F32),
                pltpu.VMEM((CHUNK, LANES), F32)]
    return pl.pallas_call(
        functools.partial(_mlstm_body, t_in),
        out_shape=[jax.ShapeDtypeStruct((bsz, seq, DV_M), F32),
                   jax.ShapeDtypeStruct((bsz, NH_M, DHV_M, DHK_M), F32),
                   jax.ShapeDtypeStruct((bsz, NH_M, DHK_M), F32),
                   jax.ShapeDtypeStruct((bsz, 1, LANES), F32)],
        grid=(bsz, nc),
        in_specs=[blk(2 * DQK_M), blk(DV_M), blk(LANES), st_c, st_n, st_m,
                  pl.BlockSpec((1, DV_M), lambda b, c: (0, 0))],
        out_specs=[blk(DV_M), st_c, st_n, st_m],
        scratch_shapes=pads,
        compiler_params=_params(2),
        name="mlstm",
    )(qk, v, g, c0, n0, m0, norm_w)


def _swa_body(t_in, pos0, prm_ref, q_ref, kc_ref, vc_ref, kp_ref, vp_ref, o_ref, *pads):
    blk = pl.program_id(1)
    if t_in == CHUNK:
        q, kc, vc = q_ref[...], kc_ref[...], vc_ref[...]
    else:
        _zero_pads_once(pads)
        q = _pad_rows(q_ref, pads[0], t_in)
        kc = _pad_rows(kc_ref, pads[1], t_in)
        vc = _pad_rows(vc_ref, pads[2], t_in)
    q, kc, vc = q.astype(BF16), kc.astype(BF16), vc.astype(BF16)
    kp, vp = kp_ref[...].astype(BF16), vp_ref[...].astype(BF16)

    row = lax.broadcasted_iota(jnp.int32, (CHUNK, WINDOW), 0)
    col = lax.broadcasted_iota(jnp.int32, (CHUNK, WINDOW), 1)
    ok_c = (col <= row) & (col < t_in)
    d_c = (row - col).astype(F32)
    ok_p = (col > row) & (pos0 + blk * t_in - WINDOW + col >= 0)
    d_p = (row - col + WINDOW).astype(F32)

    outs = []
    for hd in range(NH_A):
        kv = hd // GROUP_A
        slope, sink = prm_ref[0, hd], prm_ref[1, hd]
        qh = q[:, hd * HD_A:(hd + 1) * HD_A]
        ksl = slice(kv * HD_A, (kv + 1) * HD_A)
        s_c = jnp.where(ok_c, _dot_nt(qh, kc[:, ksl]) - slope * d_c, NEG)
        s_p = jnp.where(ok_p, _dot_nt(qh, kp[:, ksl]) - slope * d_p, NEG)
        mx = jnp.maximum(jnp.maximum(jnp.max(s_c, axis=-1, keepdims=True),
                                     jnp.max(s_p, axis=-1, keepdims=True)), sink)
        p_c = jnp.exp(s_c - mx)
        p_p = jnp.exp(s_p - mx)
        den = (jnp.sum(p_c, axis=-1, keepdims=True) + jnp.sum(p_p, axis=-1, keepdims=True)
               + jnp.exp(sink - mx))
        inv = 1.0 / den
        outs.append(_dot((p_c * inv).astype(BF16), vc[:, ksl])
                    + _dot((p_p * inv).astype(BF16), vp[:, ksl]))
    o = jnp.concatenate(outs, axis=1).astype(o_ref.dtype)
    o_ref[...] = o if t_in == CHUNK else o[:t_in]


def _swa(prm, q, k, v, k_prev, v_prev, t_in, pos0, prev_is_cache, out_dtype):
    bsz, seq, _ = q.shape
    nb = seq // t_in
    cur = lambda n: pl.BlockSpec((None, t_in, n), lambda b, i: (b, i, 0))
    if prev_is_cache:
        prev = pl.BlockSpec((None, WINDOW, DKV_A), lambda b, i: (b, 0, 0))
    else:
        prev = pl.BlockSpec((None, WINDOW, DKV_A), lambda b, i: (b, jnp.maximum(i - 1, 0), 0))
    pads = []
    if t_in != CHUNK:
        pads = [pltpu.VMEM((CHUNK, DQ_A), F32), pltpu.VMEM((CHUNK, DKV_A), F32),
                pltpu.VMEM((CHUNK, DKV_A), F32)]
    return pl.pallas_call(
        functools.partial(_swa_body, t_in, pos0),
        out_shape=jax.ShapeDtypeStruct((bsz, seq, DQ_A), out_dtype),
        grid=(bsz, nb),
        in_specs=[pl.BlockSpec(memory_space=pltpu.SMEM), cur(DQ_A), cur(DKV_A), cur(DKV_A),
                  prev, prev],
        out_specs=cur(DQ_A),
        scratch_shapes=pads,
        compiler_params=_params(2),
        name="swa",
    )(prm, q, k, v, k_prev, v_prev)


def _merge_body(alpha, x_ref, sh_ref, sc_ref, g1_ref, hm_ref, ha_ref, wg_ref, bg_ref,
                wbm_ref, wba_ref, wo_ref, lg_ref, lb_ref, o_ref):
    x = x_ref[...]
    h = (_ln(x) * (1.0 + sc_ref[...]) + sh_ref[...]).astype(BF16)

    def gate(j):
        z = _dot(h, wg_ref[:, j * D_MODEL:(j + 1) * D_MODEL]) + bg_ref[:, j * D_MODEL:(j + 1) * D_MODEL]
        return jax.nn.sigmoid(z)

    hm = (hm_ref[...] * gate(0)).astype(BF16)
    merged = gate(1) * _dot(hm, wbm_ref[...])
    merged = merged + gate(2) * _dot(ha_ref[...].astype(BF16), wba_ref[...])
    mo = _dot(merged.astype(BF16), wo_ref[...])
    o_ref[...] = _ln(alpha * x + g1_ref[...] * mo) * lg_ref[...] + lb_ref[...]


def _merge(x, sh, sc, g1, hm, ha, wg, bg, wbm, wba, wo, lg, lb, tm, alpha):
    bsz, seq, _ = x.shape
    mrows = sh.shape[1]
    sq = _const_spec((D_MODEL, D_MODEL))
    vec = _const_spec((1, D_MODEL))
    return pl.pallas_call(
        functools.partial(_merge_body, alpha),
        out_shape=jax.ShapeDtypeStruct((bsz, seq, D_MODEL), F32),
        grid=(bsz, seq // tm),
        in_specs=[_tok_spec(tm, D_MODEL), _mod_spec(mrows, tm), _mod_spec(mrows, tm),
                  _mod_spec(mrows, tm), _tok_spec(tm, DV_M), _tok_spec(tm, DQ_A),
                  _const_spec((D_MODEL, 3 * D_MODEL)), _const_spec((1, 3 * D_MODEL)),
                  sq, sq, sq, vec, vec],
        out_specs=_tok_spec(tm, D_MODEL),
        compiler_params=_params(2),
        name="merge",
    )(x, sh, sc, g1, hm, ha, wg, bg, wbm, wba, wo, lg, lb)


def _ffn_body(alpha, tm, stride, halo, x_ref, sh_ref, sc_ref, g2_ref, cb0_ref, wup_ref, bup_ref,
              cw_ref, cbias_ref, wdn_ref, bdn_ref, lg_ref, lb_ref, o_ref, cs_ref, ubuf):
    @pl.when(pl.program_id(1) == 0)
    def _():
        cs_ref[...] = cb0_ref[...]

    x = x_ref[...]
    h = (_ln(x) * (1.0 + sc_ref[...]) + sh_ref[...]).astype(BF16)
    acc = jnp.zeros((tm, D_MODEL), F32)
    for c in range(D_FF // FF_CHUNK):
        ys = []
        for half in range(2):
            off = half * D_FF + c * FF_CHUNK
            cols = slice(off, off + FF_CHUNK)
            u = _dot(h, wup_ref[:, cols]) + bup_ref[:, cols]
            ubuf[pl.ds(0, halo), :] = cs_ref[:, cols]
            ubuf[pl.ds(halo, tm), :] = u
            cs_ref[:, cols] = ubuf[pl.ds(tm, halo), :]
            y = cbias_ref[:, cols]
            for j in range(CONV_W):
                y = y + ubuf[pl.ds(halo - (CONV_W - 1 - j) * stride, tm), :] * cw_ref[j:j + 1, cols]
            ys.append(y)
        act = (jax.nn.gelu(ys[0]) * ys[1]).astype(BF16)
        acc = acc + _dot(act, wdn_ref[c * FF_CHUNK:(c + 1) * FF_CHUNK, :])
    f = acc + bdn_ref[...]
    o_ref[...] = _ln(alpha * x + g2_ref[...] * f) * lg_ref[...] + lb_ref[...]


def _ffn(x, sh, sc, g2, cb0, wup, bup, cw, cbias, wdn, bdn, lg, lb, tm, stride, alpha):
    bsz, seq, _ = x.shape
    mrows = sh.shape[1]
    halo = cb0.shape[1]
    vec = _const_spec((1, D_MODEL))
    cs = pl.BlockSpec((None, halo, 2 * D_FF), lambda b, i: (b, 0, 0))
    return pl.pallas_call(
        functools.partial(_ffn_body, alpha, tm, stride, halo),
        out_shape=[jax.ShapeDtypeStruct((bsz, seq, D_MODEL), F32),
                   jax.ShapeDtypeStruct((bsz, halo, 2 * D_FF), F32)],
        grid=(bsz, seq // tm),
        in_specs=[_tok_spec(tm, D_MODEL), _mod_spec(mrows, tm), _mod_spec(mrows, tm),
                  _mod_spec(mrows, tm), cs,
                  _const_spec((D_MODEL, 2 * D_FF)), _const_spec((1, 2 * D_FF)),
                  _const_spec((CONV_W, 2 * D_FF)), _const_spec((1, 2 * D_FF)),
                  _const_spec((D_FF, D_MODEL)), vec, vec, vec],
        out_specs=[_tok_spec(tm, D_MODEL), cs],
        scratch_shapes=[pltpu.VMEM((halo + tm, FF_CHUNK), F32)],
        compiler_params=_params(2),
        name="ffn",
    )(x, sh, sc, g2, cb0, wup, bup, cw, cbias, wdn, bdn, lg, lb)


def _layer(x, mods, state, prm, w, t_in, pos0, tm, alpha):
    sh1, sc1, g1, sh2, sc2, g2 = mods
    c0, n0, m0, k_win, v_win, cb0 = state
    act_dtype = BF16 if t_in == CHUNK else F32
    qk, v, g, qa, ka, va = _inproj(x, sh1, sc1, w["w1"], w["b1"], tm, act_dtype)

    bsz, seq, _ = x.shape
    nseq = bsz * seq // t_in
    per_seq = lambda a: a.reshape(nseq, t_in, a.shape[-1]) if t_in != CHUNK else a
    hm, c_new, n_new, m_new = _mlstm(per_seq(qk), per_seq(v), per_seq(g), c0, n0, m0,
                                     w["norm_w"], t_in)
    if k_win is None:
        ha = _swa(prm, qa, ka, va, ka, va, t_in, pos0, False, BF16)
    else:
        ha = _swa(prm, per_seq(qa), per_seq(ka), per_seq(va), k_win, v_win, t_in, pos0, True, F32)
    hm = hm.reshape(x.shape[0], seq, DV_M)
    ha = ha.reshape(x.shape[0], seq, DQ_A)
    x1 = _merge(x, sh1, sc1, g1, hm, ha, w["wg"], w["bg"], w["wbm"], w["wba"], w["wo"],
                w["ln1_g"], w["ln1_b"], tm, alpha)
    return x1, (c_new, n_new, m_new, ka, va)


def _prep_weights(w_in, b_in, mlstm_norm_w, w_branch_m, w_branch_a, w_out, ln1_g, ln1_b,
                  w_up, b_up, conv_w, conv_b, w_down, b_down, ln2_g, ln2_b):
    o_gate = 2 * DQK_M + DV_M
    o_og = o_gate + 2 * NH_M
    o_qa = o_og + DV_M
    o_gm = o_qa + DQ_A + 2 * DKV_A
    pad = jnp.zeros((D_MODEL, LANES - 2 * NH_M), w_in.dtype)
    w1 = jnp.concatenate([w_in[:, :o_gate], w_in[:, o_qa:o_gm], w_in[:, o_gate:o_og], pad], axis=1)
    b1 = jnp.concatenate([b_in[:o_gate], b_in[o_qa:o_gm], b_in[o_gate:o_og],
                          jnp.zeros((LANES - 2 * NH_M,), b_in.dtype)])
    wg = jnp.concatenate([w_in[:, o_og:o_qa], w_in[:, o_gm:]], axis=1)
    bg = jnp.concatenate([b_in[o_og:o_qa], b_in[o_gm:]])
    row = lambda a: a.reshape(1, -1)
    return dict(
        w1=w1.astype(BF16), b1=row(b1), wg=wg.astype(BF16), bg=row(bg),
        norm_w=row(mlstm_norm_w), wbm=w_branch_m.astype(BF16), wba=w_branch_a.astype(BF16),
        wo=w_out.astype(BF16), ln1_g=row(ln1_g), ln1_b=row(ln1_b),
        wup=w_up.astype(BF16), bup=row(b_up), cw=conv_w, cbias=row(conv_b),
        wdn=w_down.astype(BF16), bdn=row(b_down), ln2_g=row(ln2_g), ln2_b=row(ln2_b))


def kernel(x_prompt, x_sample, c_prompt, c_sample, state_mlstm_C, state_mlstm_n, state_mlstm_m,
           cache_k_win, cache_v_win, state_ffn_conv, w_ada, b_ada, w_in, b_in, mlstm_norm_w,
           attn_sinks, w_branch_m, w_branch_a, w_out, ln1_g, ln1_b, w_up, b_up, conv_w, conv_b,
           w_down, b_down, ln2_g, ln2_b):
    depth = w_in.shape[0]
    bp, lp, _ = x_prompt.shape
    bs, ls, _ = x_sample.shape
    past_len = 16384
    alpha = (2 * depth) ** 0.25
    dt = x_prompt.dtype
    slopes = jnp.exp2(-8.0 * jnp.arange(1, NH_A + 1, dtype=F32) / NH_A)
    tm_p = 512

    yp = x_prompt
    ys = x_sample.reshape(1, bs * ls, D_MODEL)
    new_p, new_s = [], []
    n_c = bp + bs
    c_rows = -(-n_c // SUBLANES) * SUBLANES
    c_all = jnp.concatenate([c_sample, c_prompt, jnp.zeros((c_rows - n_c, D_MODEL), dt)], axis=0)
    for l in range(depth):
        w = _prep_weights(w_in[l], b_in[l], mlstm_norm_w[l], w_branch_m[l], w_branch_a[l], w_out[l],
                          ln1_g[l], ln1_b[l], w_up[l], b_up[l], conv_w[l], conv_b[l], w_down[l],
                          b_down[l], ln2_g[l], ln2_b[l])
        prm = jnp.stack([slopes, attn_sinks[l].astype(F32)])
        mod = _ada(c_all, w_ada[l], b_ada[l])
        mod_s = mod[:bs].reshape(bs, N_MOD, D_MODEL)
        mod_p = mod[bs:bs + bp].reshape(bp, N_MOD, D_MODEL)
        mods_p = [mod_p[:, j:j + 1] for j in range(N_MOD)]
        mods_s_seq = [jnp.repeat(mod_s[:, j], ls, axis=0)[None] for j in range(N_MOD)]
        mods_s_pos = [jnp.tile(mod_s[:, j], (ls, 1))[None] for j in range(N_MOD)]

        zero_state = (jnp.zeros((bp, NH_M, DHV_M, DHK_M), dt), jnp.zeros((bp, NH_M, DHK_M), dt),
                      jnp.zeros((bp, 1, LANES), dt), None, None, None)
        x1p, (cp, np_, mp, kap, vap) = _layer(yp, mods_p, zero_state, prm, w, CHUNK, 0, tm_p, alpha)
        halo_p = SUBLANES
        yp, csp = _ffn(x1p, mods_p[3], mods_p[4], mods_p[5],
                       jnp.zeros((bp, halo_p, 2 * D_FF), dt),
                       w["wup"], w["bup"], w["cw"], w["cbias"], w["wdn"], w["bdn"],
                       w["ln2_g"], w["ln2_b"], tm_p, 1, alpha)
        kx = kap.reshape(bp, lp, NKV_A, HD_A)
        vx = vap.reshape(bp, lp, NKV_A, HD_A)
        new_p.append((cp, np_, mp[:, 0, :NH_M], kx[:, -WINDOW:], vx[:, -WINDOW:],
                      csp[:, halo_p - (CONV_W - 1):]))

        m0 = jnp.pad(state_mlstm_m[l], ((0, 0), (0, LANES - NH_M)))[:, None, :]
        k_win = cache_k_win[l].reshape(bs, -1, DKV_A)
        v_win = cache_v_win[l].reshape(bs, -1, DKV_A)
        state_s = (state_mlstm_C[l], state_mlstm_n[l], m0, k_win, v_win, None)
        x1s, (cs_, ns, ms, kas, vas) = _layer(ys, mods_s_seq[:3] + [None] * 3, state_s, prm, w,
                                              ls, past_len, bs * ls, alpha)
        to_pos = lambda a: a.reshape(bs, -1, a.shape[-1]).transpose(1, 0, 2).reshape(1, -1, a.shape[-1])
        halo_s = (CONV_W - 1) * bs
        y_pos, css = _ffn(to_pos(x1s[0]), mods_s_pos[3], mods_s_pos[4], mods_s_pos[5],
                          to_pos(state_ffn_conv[l]), w["wup"], w["bup"], w["cw"], w["cbias"],
                          w["wdn"], w["bdn"], w["ln2_g"], w["ln2_b"], bs * ls, bs, alpha)
        ys = y_pos.reshape(ls, bs, D_MODEL).transpose(1, 0, 2).reshape(1, bs * ls, D_MODEL)
        kx = jnp.concatenate([cache_k_win[l], kas.reshape(bs, ls, NKV_A, HD_A)], axis=1)
        vx = jnp.concatenate([cache_v_win[l], vas.reshape(bs, ls, NKV_A, HD_A)], axis=1)
        conv_s = css.reshape(CONV_W - 1, bs, 2 * D_FF).transpose(1, 0, 2)
        new_s.append((cs_, ns, ms[:, 0, :NH_M], kx[:, -WINDOW:], vx[:, -WINDOW:], conv_s))

    p_state = [jnp.stack(a) for a in zip(*new_p)]
    s_state = [jnp.stack(a) for a in zip(*new_s)]
    return (yp, ys.reshape(bs, ls, D_MODEL), *p_state, *s_state)
```

```python
import functools

import jax
import jax.numpy as jnp
from jax import lax
from jax.experimental import pallas as pl
from jax.experimental.pallas import tpu as pltpu

F32 = jnp.float32
BF16 = jnp.bfloat16

D_MODEL = 1024
NH_M, DHK_M, DHV_M = 4, 128, 256
DQK_M, DV_M = NH_M * DHK_M, NH_M * DHV_M
NH_A, NKV_A, HD_A = 16, 4, 64
GROUP_A = NH_A // NKV_A
WINDOW = 128
DQ_A, DKV_A = NH_A * HD_A, NKV_A * HD_A
D_FF = 2816
CONV_W = 3
N_MOD = 6
LN_EPS = 1e-5
CHUNK = 128
NEG = -1e30
LANES = 128
SUBLANES = 8
FF_CHUNK = 256
VMEM_LIMIT = 56 * 1024 * 1024
SEQS_PER_STEP = 4


def _ln(x):
    mu = jnp.mean(x, axis=-1, keepdims=True)
    xc = x - mu
    var = jnp.mean(xc * xc, axis=-1, keepdims=True)
    return xc * lax.rsqrt(var + LN_EPS)


def _dot(a, b):
    return jnp.dot(a, b, preferred_element_type=F32)


def _dot_nt(a, b):
    return lax.dot_general(a, b, (((1,), (1,)), ((), ())), preferred_element_type=F32)


def _dot_tn(a, b):
    return lax.dot_general(a, b, (((0,), (0,)), ((), ())), preferred_element_type=F32)


def _const_spec(shape):
    nd = len(shape)
    return pl.BlockSpec(shape, lambda *_: (0,) * nd, pipeline_mode=pl.Buffered(1))


def _params(n_grid):
    return pltpu.CompilerParams(dimension_semantics=("arbitrary",) * n_grid,
                                vmem_limit_bytes=VMEM_LIMIT)


def _ada_body(c_ref, w_ref, b_ref, o_ref):
    c = c_ref[...]
    s = (c * jax.nn.sigmoid(c)).astype(BF16)
    o_ref[...] = _dot(s, w_ref[...].astype(BF16)) + b_ref[...]


def _ada(c, w_ada, b_ada):
    rows = c.shape[0]
    n_out = w_ada.shape[1]
    bn = 512
    return pl.pallas_call(
        _ada_body,
        out_shape=jax.ShapeDtypeStruct((rows, n_out), F32),
        grid=(n_out // bn,),
        in_specs=[pl.BlockSpec((rows, D_MODEL), lambda j: (0, 0)),
                  pl.BlockSpec((D_MODEL, bn), lambda j: (0, j)),
                  pl.BlockSpec((1, bn), lambda j: (0, j))],
        out_specs=pl.BlockSpec((rows, bn), lambda j: (0, j)),
        compiler_params=_params(1),
        name="ada",
    )(c, w_ada, b_ada.reshape(1, n_out))


_O_QM, _O_KM, _O_VM = 0, DQK_M, 2 * DQK_M
_O_QA = _O_VM + DV_M
_O_KA = _O_QA + DQ_A
_O_VA = _O_KA + DKV_A
_O_G = _O_VA + DKV_A
N_IN1 = _O_G + LANES


def _inproj_body(x_ref, sh_ref, sc_ref, w_ref, b_ref,
                 qk_ref, v_ref, g_ref, qa_ref, ka_ref, va_ref):
    h = (_ln(x_ref[...]) * (1.0 + sc_ref[...]) + sh_ref[...]).astype(BF16)

    def proj(off, n):
        return _dot(h, w_ref[:, off:off + n]) + b_ref[:, off:off + n]

    qk_ref[:, :DQK_M] = (proj(_O_QM, DQK_M) * DHK_M ** -0.5).astype(qk_ref.dtype)
    qk_ref[:, DQK_M:] = proj(_O_KM, DQK_M).astype(qk_ref.dtype)
    v_ref[...] = proj(_O_VM, DV_M).astype(v_ref.dtype)
    qa_ref[...] = (proj(_O_QA, DQ_A) * HD_A ** -0.5).astype(qa_ref.dtype)
    ka_ref[...] = proj(_O_KA, DKV_A)
    va_ref[...] = proj(_O_VA, DKV_A)
    g_ref[...] = proj(_O_G, LANES)


def _tok_spec(tm, n):
    return pl.BlockSpec((None, tm, n), lambda b, i: (b, i, 0))


def _mod_spec(rows, tm):
    if rows == 1:
        return pl.BlockSpec((None, 1, D_MODEL), lambda b, i: (b, 0, 0))
    return pl.BlockSpec((None, tm, D_MODEL), lambda b, i: (b, i, 0))


def _inproj(x, sh, sc, w1, b1, tm, act_dtype):
    bsz, seq, _ = x.shape
    mrows = sh.shape[1]
    outs = [(2 * DQK_M, act_dtype), (DV_M, act_dtype), (LANES, F32),
            (DQ_A, act_dtype), (DKV_A, F32), (DKV_A, F32)]
    return pl.pallas_call(
        _inproj_body,
        out_shape=[jax.ShapeDtypeStruct((bsz, seq, n), dt) for n, dt in outs],
        grid=(bsz, seq // tm),
        in_specs=[_tok_spec(tm, D_MODEL), _mod_spec(mrows, tm), _mod_spec(mrows, tm),
                  _const_spec((D_MODEL, N_IN1)), _const_spec((1, N_IN1))],
        out_specs=[_tok_spec(tm, n) for n, _ in outs],
        compiler_params=_params(2),
        name="inproj",
    )(x, sh, sc, w1, b1)


def _zero_once(pads):
    @pl.when((pl.program_id(0) == 0) & (pl.program_id(1) == 0))
    def _():
        for p in pads:
            p[...] = jnp.zeros(p.shape, p.dtype)


def _scan_rows(x, op, rows):
    row = lax.broadcasted_iota(jnp.int32, x.shape, 0)
    d = 1
    while d < rows:
        shifted = pltpu.roll(x, d, axis=0)
        x = jnp.where(row >= d, op(x, shifted), x)
        d *= 2
    return x


def _mlstm_body(t_in, t, nb, qk_ref, v_ref, g_ref, c0_ref, n0_ref, m0_ref, nw_ref,
                h_ref, c_ref, n_ref, m_ref, *pads):
    @pl.when(pl.program_id(1) == 0)
    def _():
        c_ref[...] = c0_ref[...]
        n_ref[...] = n0_ref[...]
        m_ref[...] = m0_ref[...]

    if t_in != t:
        _zero_once(pads)
        for src, dst in zip((qk_ref, v_ref, g_ref), pads):
            dst[:, pl.ds(0, t_in), :] = src[...].astype(dst.dtype)
        qk_src, v_src, g_src = pads
    else:
        qk_src, v_src, g_src = qk_ref, v_ref, g_ref

    row = lax.broadcasted_iota(jnp.int32, (t, LANES), 0)
    lane1 = lax.broadcasted_iota(jnp.int32, (1, LANES), 1)
    r2 = lax.broadcasted_iota(jnp.int32, (t, t), 0)
    c2 = lax.broadcasted_iota(jnp.int32, (t, t), 1)
    causal = c2 <= r2

    seqs = []
    for s in range(nb):
        g = g_src[s]
        lf = jax.nn.log_sigmoid(g)
        ig = g
        if t_in != t:
            lf = jnp.where(row < t_in, lf, 0.0)
            ig = jnp.where(row < t_in, ig, NEG)
        b = pltpu.roll(_scan_rows(lf, jnp.add, t), LANES - NH_M, axis=1)
        gg = ig - b
        gmx = _scan_rows(gg, jnp.maximum, t)
        m_prev = m_ref[s]
        a = b + m_prev
        mt = jnp.maximum(a, b + gmx)
        b_last, gmx_last = b[t - 1:t, :], gmx[t - 1:t, :]
        m_new = jnp.maximum(b_last + m_prev, b_last + gmx_last)
        seqs.append(dict(
            a_in=jnp.exp(a - mt), eb=b - mt, lowb=jnp.exp(-mt), gt=gg.T,
            ws=jnp.exp(b_last + gg - m_new), decay=jnp.exp(b_last + m_prev - m_new)))
        m_ref[s] = jnp.where(lane1 < NH_M, m_new, 0.0)

    units = [(s, hd) for s in range(nb) for hd in range(NH_M)]
    nu = len(units)
    col = lambda name, s, hd: seqs[s][name][:, hd:hd + 1]
    q = [qk_src[s, :, hd * DHK_M:(hd + 1) * DHK_M].astype(BF16) for s, hd in units]
    k = [qk_src[s, :, DQK_M + hd * DHK_M:DQK_M + (hd + 1) * DHK_M].astype(BF16) for s, hd in units]
    v = [v_src[s, :, hd * DHV_M:(hd + 1) * DHV_M].astype(BF16) for s, hd in units]
    c_old = [c_ref[s, hd] for s, hd in units]
    n_old = [n_ref[s, hd:hd + 1, :] for s, hd in units]

    sqk = [_dot_nt(q[u], k[u]) for u in range(nu)]
    dexp = [jnp.exp(jnp.where(causal, col("eb", s, hd) + seqs[s]["gt"][hd:hd + 1, :], NEG))
            for s, hd in units]
    smat = [sqk[u] * dexp[u] for u in range(nu)]
    intra = [_dot(smat[u].astype(BF16), v[u]) for u in range(nu)]
    inter = [_dot_nt(q[u], c_old[u].astype(BF16)) for u in range(nu)]
    qn = [jnp.sum(q[u].astype(F32) * n_old[u].astype(BF16).astype(F32), axis=-1, keepdims=True)
          for u in range(nu)]
    den = [jnp.sum(smat[u], axis=-1, keepdims=True) + col("a_in", s, hd) * qn[u]
           for u, (s, hd) in enumerate(units)]
    hh = [(intra[u] + col("a_in", s, hd) * inter[u])
          / jnp.maximum(jnp.abs(den[u]), col("lowb", s, hd)) for u, (s, hd) in enumerate(units)]
    hn = [_ln(hh[u]) * nw_ref[:, hd * DHV_M:(hd + 1) * DHV_M] for u, (s, hd) in enumerate(units)]
    for u, (s, hd) in enumerate(units):
        h_ref[s, :, hd * DHV_M:(hd + 1) * DHV_M] = hn[u] if t_in == t else hn[u][:t_in]

    kw = [k[u].astype(F32) * col("ws", s, hd) for u, (s, hd) in enumerate(units)]
    upd = [_dot_tn(v[u], kw[u].astype(BF16)) for u in range(nu)]
    for u, (s, hd) in enumerate(units):
        dec = seqs[s]["decay"][:, hd:hd + 1]
        c_ref[s, hd] = dec * c_old[u] + upd[u]
        n_ref[s, hd:hd + 1, :] = dec * n_old[u] + jnp.sum(kw[u], axis=0, keepdims=True)


def _mlstm(qk, v, g, c0, n0, m0, norm_w, t_in, nb):
    nseq, seq, _ = qk.shape
    t = CHUNK if t_in == CHUNK else SUBLANES
    blk = lambda n: pl.BlockSpec((nb, t_in, n), lambda o, c: (o, c, 0))
    st_c = pl.BlockSpec((nb, NH_M, DHV_M, DHK_M), lambda o, c: (o, 0, 0, 0))
    st_n = pl.BlockSpec((nb, NH_M, DHK_M), lambda o, c: (o, 0, 0))
    st_m = pl.BlockSpec((nb, 1, LANES), lambda o, c: (o, 0, 0))
    pads = []
    if t_in != t:
        pads = [pltpu.VMEM((nb, t, 2 * DQK_M), F32), pltpu.VMEM((nb, t, DV_M), F32),
                pltpu.VMEM((nb, t, LANES), F32)]
    return pl.pallas_call(
        functools.partial(_mlstm_body, t_in, t, nb),
        out_shape=[jax.ShapeDtypeStruct((nseq, seq, DV_M), F32),
                   jax.ShapeDtypeStruct((nseq, NH_M, DHV_M, DHK_M), F32),
                   jax.ShapeDtypeStruct((nseq, NH_M, DHK_M), F32),
                   jax.ShapeDtypeStruct((nseq, 1, LANES), F32)],
        grid=(nseq // nb, seq // t_in),
        in_specs=[blk(2 * DQK_M), blk(DV_M), blk(LANES), st_c, st_n, st_m,
                  pl.BlockSpec((1, DV_M), lambda o, c: (0, 0))],
        out_specs=[blk(DV_M), st_c, st_n, st_m],
        scratch_shapes=pads,
        compiler_params=_params(2),
        name="mlstm",
    )(qk, v, g, c0, n0, m0, norm_w)


def _lo_hi(x):
    lo = lax.broadcasted_iota(jnp.int32, x.shape, 1) < HD_A
    xs = pltpu.roll(x, HD_A, axis=1)
    z = jnp.zeros_like(x)
    even = (jnp.where(lo, x, z).astype(BF16), jnp.where(lo, z, xs).astype(BF16))
    odd = (jnp.where(lo, xs, z).astype(BF16), jnp.where(lo, z, x).astype(BF16))
    return even, odd


def _swa_body(t_in, t, nb, first_block_has_no_past, prm_ref, q_ref, kc_ref, vc_ref, kp_ref, vp_ref,
              o_ref, tbl, *pads):
    row = lax.broadcasted_iota(jnp.int32, (t, WINDOW), 0)
    col = lax.broadcasted_iota(jnp.int32, (t, WINDOW), 1)
    tri = col <= row

    @pl.when((pl.program_id(0) == 0) & (pl.program_id(1) == 0))
    def _():
        dist = jnp.where(tri, row - col, row - col + WINDOW).astype(F32)
        for hd in range(NH_A):
            bias = prm_ref[0, hd] * dist
            tbl[0, pl.ds(hd * t, t), :] = bias
            if first_block_has_no_past:
                tbl[1, pl.ds(hd * t, t), :] = bias + jnp.where(tri, 0.0, -NEG)
        for p in pads:
            p[...] = jnp.zeros(p.shape, p.dtype)

    which = 0
    if first_block_has_no_past:
        which = jnp.where(pl.program_id(1) == 0, 1, 0)
    if t_in != t:
        pads[0][:, pl.ds(0, t_in), :] = q_ref[...].astype(F32)
        q_src = pads[0]
    else:
        q_src = q_ref
    if t_in != WINDOW:
        pads[1][:, pl.ds(0, t_in), :] = kc_ref[...]
        pads[2][:, pl.ds(0, t_in), :] = vc_ref[...]
        kc_src, vc_src = pads[1], pads[2]
    else:
        kc_src, vc_src = kc_ref, vc_ref
    sink_col = jnp.concatenate([jnp.full((t, 1), prm_ref[1, hd], F32) for hd in range(NH_A)], axis=0)

    for s in range(nb):
        rhs_k, rhs_v = [None] * NKV_A, [None] * NKV_A
        for c in range(NKV_A // 2):
            sl = slice(c * LANES, (c + 1) * LANES)
            parts = [_lo_hi(src[s, :, sl]) for src in (kc_src, kp_ref, vc_src, vp_ref)]
            for par in range(2):
                kv = 2 * c + par
                rhs_k[kv] = jnp.concatenate([*parts[0][par], *parts[1][par]], axis=0)
                rhs_v[kv] = jnp.concatenate([*parts[2][par], *parts[3][par]], axis=0)
        sg = [_dot_nt(jnp.concatenate([q_src[s, :, (2 * kv) * LANES:(2 * kv + 1) * LANES],
                                       q_src[s, :, (2 * kv + 1) * LANES:(2 * kv + 2) * LANES]],
                                      axis=0).astype(BF16), rhs_k[kv]) for kv in range(NKV_A)]
        tiles = []
        for hd in range(NH_A):
            kv, a, par = hd // GROUP_A, (hd % GROUP_A) // 2, hd % 2
            rows = slice(a * t, (a + 1) * t)
            tiles.append(jnp.where(tri, sg[kv][rows, par * WINDOW:(par + 1) * WINDOW],
                                   sg[kv][rows, (2 + par) * WINDOW:(3 + par) * WINDOW]))
        sc = jnp.concatenate(tiles, axis=0) - tbl[which]
        mx = jnp.maximum(jnp.max(sc, axis=-1, keepdims=True), sink_col)
        p = jnp.exp(sc - mx)
        den = jnp.sum(p, axis=-1, keepdims=True) + jnp.exp(sink_col - mx)
        pn = p * (1.0 / den)
        zero = jnp.zeros((t, WINDOW), F32)
        outs = []
        for kv in range(NKV_A):
            lhs_rows = []
            for a in range(2):
                h0 = kv * GROUP_A + 2 * a
                pe, po = pn[h0 * t:(h0 + 1) * t], pn[(h0 + 1) * t:(h0 + 2) * t]
                lhs_rows.append(jnp.concatenate(
                    [jnp.where(tri, pe, zero), jnp.where(tri, po, zero),
                     jnp.where(tri, zero, pe), jnp.where(tri, zero, po)], axis=1).astype(BF16))
            og = _dot(jnp.concatenate(lhs_rows, axis=0), rhs_v[kv])
            outs += [og[:t], og[t:]]
        o = jnp.concatenate(outs, axis=1).astype(o_ref.dtype)
        o_ref[s] = o if t_in == t else o[:t_in]


def _swa(prm, q, k, v, k_prev, v_prev, t_in, nb, first_block_has_no_past, prev_is_cache, out_dtype):
    nseq, seq, _ = q.shape
    t = CHUNK if t_in == CHUNK else SUBLANES
    assert t_in == WINDOW or seq == t_in
    cur = lambda n: pl.BlockSpec((nb, t_in, n), lambda o, i: (o, i, 0))
    if prev_is_cache:
        prev = pl.BlockSpec((nb, WINDOW, DKV_A), lambda o, i: (o, 0, 0))
    else:
        prev = pl.BlockSpec((nb, WINDOW, DKV_A), lambda o, i: (o, jnp.maximum(i - 1, 0), 0))
    pads = []
    if t_in != t:
        pads = [pltpu.VMEM((nb, t, DQ_A), F32), pltpu.VMEM((nb, WINDOW, DKV_A), F32),
                pltpu.VMEM((nb, WINDOW, DKV_A), F32)]
    ntab = 2 if first_block_has_no_past else 1
    return pl.pallas_call(
        functools.partial(_swa_body, t_in, t, nb, first_block_has_no_past),
        out_shape=jax.ShapeDtypeStruct((nseq, seq, DQ_A), out_dtype),
        grid=(nseq // nb, seq // t_in),
        in_specs=[pl.BlockSpec(memory_space=pltpu.SMEM), cur(DQ_A), cur(DKV_A), cur(DKV_A),
                  prev, prev],
        out_specs=cur(DQ_A),
        scratch_shapes=[pltpu.VMEM((ntab, NH_A * t, WINDOW), F32)] + pads,
        compiler_params=_params(2),
        name="swa",
    )(prm, q, k, v, k_prev, v_prev)


def _merge_body(alpha, x_ref, sh_ref, sc_ref, g1_ref, hm_ref, ha_ref, wg_ref, bg_ref,
                wbm_ref, wba_ref, wo_ref, lg_ref, lb_ref, o_ref):
    x = x_ref[...]
    h = (_ln(x) * (1.0 + sc_ref[...]) + sh_ref[...]).astype(BF16)

    def gate(j):
        z = _dot(h, wg_ref[:, j * D_MODEL:(j + 1) * D_MODEL]) + bg_ref[:, j * D_MODEL:(j + 1) * D_MODEL]
        return jax.nn.sigmoid(z)

    hm = (hm_ref[...] * gate(0)).astype(BF16)
    merged = gate(1) * _dot(hm, wbm_ref[...])
    merged = merged + gate(2) * _dot(ha_ref[...].astype(BF16), wba_ref[...])
    mo = _dot(merged.astype(BF16), wo_ref[...])
    o_ref[...] = _ln(alpha * x + g1_ref[...] * mo) * lg_ref[...] + lb_ref[...]


def _merge(x, sh, sc, g1, hm, ha, wg, bg, wbm, wba, wo, lg, lb, tm, alpha):
    bsz, seq, _ = x.shape
    mrows = sh.shape[1]
    sq = _const_spec((D_MODEL, D_MODEL))
    vec = _const_spec((1, D_MODEL))
    return pl.pallas_call(
        functools.partial(_merge_body, alpha),
        out_shape=jax.ShapeDtypeStruct((bsz, seq, D_MODEL), F32),
        grid=(bsz, seq // tm),
        in_specs=[_tok_spec(tm, D_MODEL), _mod_spec(mrows, tm), _mod_spec(mrows, tm),
                  _mod_spec(mrows, tm), _tok_spec(tm, DV_M), _tok_spec(tm, DQ_A),
                  _const_spec((D_MODEL, 3 * D_MODEL)), _const_spec((1, 3 * D_MODEL)),
                  sq, sq, sq, vec, vec],
        out_specs=_tok_spec(tm, D_MODEL),
        compiler_params=_params(2),
        name="merge",
    )(x, sh, sc, g1, hm, ha, wg, bg, wbm, wba, wo, lg, lb)


def _ffn_body(alpha, tm, stride, halo, x_ref, sh_ref, sc_ref, g2_ref, cb0_ref, wup_ref, bup_ref,
              cw_ref, cbias_ref, wdn_ref, bdn_ref, lg_ref, lb_ref, o_ref, cs_ref, ubuf):
    @pl.when(pl.program_id(1) == 0)
    def _():
        cs_ref[...] = cb0_ref[...]

    x = x_ref[...]
    h = (_ln(x) * (1.0 + sc_ref[...]) + sh_ref[...]).astype(BF16)
    acc = jnp.zeros((tm, D_MODEL), F32)
    for c in range(D_FF // FF_CHUNK):
        ys = []
        for half in range(2):
            off = half * D_FF + c * FF_CHUNK
            cols = slice(off, off + FF_CHUNK)
            u = _dot(h, wup_ref[:, cols]) + bup_ref[:, cols]
            ubuf[pl.ds(0, halo), :] = cs_ref[:, cols]
            ubuf[pl.ds(halo, tm), :] = u
            cs_ref[:, cols] = ubuf[pl.ds(tm, halo), :]
            y = cbias_ref[:, cols]
            for j in range(CONV_W):
                y = y + ubuf[pl.ds(halo - (CONV_W - 1 - j) * stride, tm), :] * cw_ref[j:j + 1, cols]
            ys.append(y)
        act = (jax.nn.gelu(ys[0]) * ys[1]).astype(BF16)
        acc = acc + _dot(act, wdn_ref[c * FF_CHUNK:(c + 1) * FF_CHUNK, :])
    f = acc + bdn_ref[...]
    o_ref[...] = _ln(alpha * x + g2_ref[...] * f) * lg_ref[...] + lb_ref[...]


def _ffn(x, sh, sc, g2, cb0, wup, bup, cw, cbias, wdn, bdn, lg, lb, tm, stride, alpha):
    bsz, seq, _ = x.shape
    mrows = sh.shape[1]
    halo = cb0.shape[1]
    vec = _const_spec((1, D_MODEL))
    cs = pl.BlockSpec((None, halo, 2 * D_FF), lambda b, i: (b, 0, 0))
    return pl.pallas_call(
        functools.partial(_ffn_body, alpha, tm, stride, halo),
        out_shape=[jax.ShapeDtypeStruct((bsz, seq, D_MODEL), F32),
                   jax.ShapeDtypeStruct((bsz, halo, 2 * D_FF), F32)],
        grid=(bsz, seq // tm),
        in_specs=[_tok_spec(tm, D_MODEL), _mod_spec(mrows, tm), _mod_spec(mrows, tm),
                  _mod_spec(mrows, tm), cs,
                  _const_spec((D_MODEL, 2 * D_FF)), _const_spec((1, 2 * D_FF)),
                  _const_spec((CONV_W, 2 * D_FF)), _const_spec((1, 2 * D_FF)),
                  _const_spec((D_FF, D_MODEL)), vec, vec, vec],
        out_specs=[_tok_spec(tm, D_MODEL), cs],
        scratch_shapes=[pltpu.VMEM((halo + tm, FF_CHUNK), F32)],
        compiler_params=_params(2),
        name="ffn",
    )(x, sh, sc, g2, cb0, wup, bup, cw, cbias, wdn, bdn, lg, lb)


def _layer(x, mods, state, prm, w, t_in, pos0, tm, alpha):
    sh1, sc1, g1, sh2, sc2, g2 = mods
    c0, n0, m0, k_win, v_win, cb0 = state
    act_dtype = BF16 if t_in == CHUNK else F32
    qk, v, g, qa, ka, va = _inproj(x, sh1, sc1, w["w1"], w["b1"], tm, act_dtype)

    bsz, seq, _ = x.shape
    nseq = bsz * seq // t_in
    per_seq = lambda a: a.reshape(nseq, t_in, a.shape[-1]) if t_in != CHUNK else a
    nb = bsz if t_in == CHUNK else SEQS_PER_STEP
    hm, c_new, n_new, m_new = _mlstm(per_seq(qk), per_seq(v), per_seq(g), c0, n0, m0,
                                     w["norm_w"], t_in, nb)
    if k_win is None:
        assert pos0 == 0
        ha = _swa(prm, qa, ka, va, ka, va, t_in, 1, True, False, BF16)
    else:
        assert pos0 >= WINDOW
        ha = _swa(prm, per_seq(qa), per_seq(ka), per_seq(va), k_win, v_win, t_in, SEQS_PER_STEP,
                  False, True, F32)
    hm = hm.reshape(x.shape[0], seq, DV_M)
    ha = ha.reshape(x.shape[0], seq, DQ_A)
    x1 = _merge(x, sh1, sc1, g1, hm, ha, w["wg"], w["bg"], w["wbm"], w["wba"], w["wo"],
                w["ln1_g"], w["ln1_b"], tm, alpha)
    return x1, (c_new, n_new, m_new, ka, va)


def _prep_weights(w_in, b_in, mlstm_norm_w, w_branch_m, w_branch_a, w_out, ln1_g, ln1_b,
                  w_up, b_up, conv_w, conv_b, w_down, b_down, ln2_g, ln2_b):
    o_gate = 2 * DQK_M + DV_M
    o_og = o_gate + 2 * NH_M
    o_qa = o_og + DV_M
    o_gm = o_qa + DQ_A + 2 * DKV_A
    pad = jnp.zeros((D_MODEL, LANES - 2 * NH_M), w_in.dtype)
    w1 = jnp.concatenate([w_in[:, :o_gate], w_in[:, o_qa:o_gm], w_in[:, o_gate:o_og], pad], axis=1)
    b1 = jnp.concatenate([b_in[:o_gate], b_in[o_qa:o_gm], b_in[o_gate:o_og],
                          jnp.zeros((LANES - 2 * NH_M,), b_in.dtype)])
    wg = jnp.concatenate([w_in[:, o_og:o_qa], w_in[:, o_gm:]], axis=1)
    bg = jnp.concatenate([b_in[o_og:o_qa], b_in[o_gm:]])
    row = lambda a: a.reshape(1, -1)
    return dict(
        w1=w1.astype(BF16), b1=row(b1), wg=wg.astype(BF16), bg=row(bg),
        norm_w=row(mlstm_norm_w), wbm=w_branch_m.astype(BF16), wba=w_branch_a.astype(BF16),
        wo=w_out.astype(BF16), ln1_g=row(ln1_g), ln1_b=row(ln1_b),
        wup=w_up.astype(BF16), bup=row(b_up), cw=conv_w, cbias=row(conv_b),
        wdn=w_down.astype(BF16), bdn=row(b_down), ln2_g=row(ln2_g), ln2_b=row(ln2_b))


def kernel(x_prompt, x_sample, c_prompt, c_sample, state_mlstm_C, state_mlstm_n, state_mlstm_m,
           cache_k_win, cache_v_win, state_ffn_conv, w_ada, b_ada, w_in, b_in, mlstm_norm_w,
           attn_sinks, w_branch_m, w_branch_a, w_out, ln1_g, ln1_b, w_up, b_up, conv_w, conv_b,
           w_down, b_down, ln2_g, ln2_b):
    depth = w_in.shape[0]
    bp, lp, _ = x_prompt.shape
    bs, ls, _ = x_sample.shape
    past_len = 16384
    alpha = (2 * depth) ** 0.25
    dt = x_prompt.dtype
    slopes = jnp.exp2(-8.0 * jnp.arange(1, NH_A + 1, dtype=F32) / NH_A)
    tm_p = 512

    yp = x_prompt
    ys = x_sample.reshape(1, bs * ls, D_MODEL)
    new_p, new_s = [], []
    n_c = bp + bs
    c_rows = -(-n_c // SUBLANES) * SUBLANES
    c_all = jnp.concatenate([c_sample, c_prompt, jnp.zeros((c_rows - n_c, D_MODEL), dt)], axis=0)
    for l in range(depth):
        w = _prep_weights(w_in[l], b_in[l], mlstm_norm_w[l], w_branch_m[l], w_branch_a[l], w_out[l],
                          ln1_g[l], ln1_b[l], w_up[l], b_up[l], conv_w[l], conv_b[l], w_down[l],
                          b_down[l], ln2_g[l], ln2_b[l])
        prm = jnp.stack([slopes, attn_sinks[l].astype(F32)])
        mod = _ada(c_all, w_ada[l], b_ada[l])
        mod_s = mod[:bs].reshape(bs, N_MOD, D_MODEL)
        mod_p = mod[bs:bs + bp].reshape(bp, N_MOD, D_MODEL)
        mods_p = [mod_p[:, j:j + 1] for j in range(N_MOD)]
        mods_s_seq = [jnp.repeat(mod_s[:, j], ls, axis=0)[None] for j in range(N_MOD)]
        mods_s_pos = [jnp.tile(mod_s[:, j], (ls, 1))[None] for j in range(N_MOD)]

        zero_state = (jnp.zeros((bp, NH_M, DHV_M, DHK_M), dt), jnp.zeros((bp, NH_M, DHK_M), dt),
                      jnp.zeros((bp, 1, LANES), dt), None, None, None)
        x1p, (cp, np_, mp, kap, vap) = _layer(yp, mods_p, zero_state, prm, w, CHUNK, 0, tm_p, alpha)
        halo_p = SUBLANES
        yp, csp = _ffn(x1p, mods_p[3], mods_p[4], mods_p[5],
                       jnp.zeros((bp, halo_p, 2 * D_FF), dt),
                       w["wup"], w["bup"], w["cw"], w["cbias"], w["wdn"], w["bdn"],
                       w["ln2_g"], w["ln2_b"], tm_p, 1, alpha)
        kx = kap.reshape(bp, lp, NKV_A, HD_A)
        vx = vap.reshape(bp, lp, NKV_A, HD_A)
        new_p.append((cp, np_, mp[:, 0, :NH_M], kx[:, -WINDOW:], vx[:, -WINDOW:],
                      csp[:, halo_p - (CONV_W - 1):]))

        m0 = jnp.pad(state_mlstm_m[l], ((0, 0), (0, LANES - NH_M)))[:, None, :]
        k_win = cache_k_win[l].reshape(bs, -1, DKV_A)
        v_win = cache_v_win[l].reshape(bs, -1, DKV_A)
        state_s = (state_mlstm_C[l], state_mlstm_n[l], m0, k_win, v_win, None)
        x1s, (cs_, ns, ms, kas, vas) = _layer(ys, mods_s_seq[:3] + [None] * 3, state_s, prm, w,
                                              ls, past_len, bs * ls, alpha)
        to_pos = lambda a: a.reshape(bs, -1, a.shape[-1]).transpose(1, 0, 2).reshape(1, -1, a.shape[-1])
        y_pos, css = _ffn(to_pos(x1s[0]), mods_s_pos[3], mods_s_pos[4], mods_s_pos[5],
                          to_pos(state_ffn_conv[l]), w["wup"], w["bup"], w["cw"], w["cbias"],
                          w["wdn"], w["bdn"], w["ln2_g"], w["ln2_b"], bs * ls, bs, alpha)
        ys = y_pos.reshape(ls, bs, D_MODEL).transpose(1, 0, 2).reshape(1, bs * ls, D_MODEL)
        kx = jnp.concatenate([cache_k_win[l], kas.reshape(bs, ls, NKV_A, HD_A)], axis=1)
        vx = jnp.concatenate([cache_v_win[l], vas.reshape(bs, ls, NKV_A, HD_A)], axis=1)
        conv_s = css.reshape(CONV_W - 1, bs, 2 * D_FF).transpose(1, 0, 2)
        new_s.append((cs_, ns, ms[:, 0, :NH_M], kx[:, -WINDOW:], vx[:, -WINDOW:], conv_s))

    p_state = [jnp.stack(a) for a in zip(*new_p)]
    s_state = [jnp.stack(a) for a in zip(*new_s)]
    return (yp, ys.reshape(bs, ls, D_MODEL), *p_state, *s_state)
```

```python
import functools

import jax
import jax.numpy as jnp
from jax import lax
from jax.experimental import pallas as pl
from jax.experimental.pallas import tpu as pltpu

F32 = jnp.float32
BF16 = jnp.bfloat16

D_MODEL = 1024
NH_M, DHK_M, DHV_M = 4, 128, 256
DQK_M, DV_M = NH_M * DHK_M, NH_M * DHV_M
NH_A, NKV_A, HD_A = 16, 4, 64
GROUP_A = NH_A // NKV_A
WINDOW = 128
DQ_A, DKV_A = NH_A * HD_A, NKV_A * HD_A
D_FF = 2816
CONV_W = 3
N_MOD = 6
LN_EPS = 1e-5
CHUNK = 128
NEG = -1e30
LANES = 128
SUBLANES = 8
FF_CHUNK = 256
VMEM_LIMIT = 56 * 1024 * 1024
SEQS_PER_STEP = 4


def _ln(x):
    mu = jnp.mean(x, axis=-1, keepdims=True)
    xc = x - mu
    var = jnp.mean(xc * xc, axis=-1, keepdims=True)
    return xc * lax.rsqrt(var + LN_EPS)


def _dot(a, b):
    return jnp.dot(a, b, preferred_element_type=F32)


def _dot_nt(a, b):
    return lax.dot_general(a, b, (((1,), (1,)), ((), ())), preferred_element_type=F32)


def _dot_tn(a, b):
    return lax.dot_general(a, b, (((0,), (0,)), ((), ())), preferred_element_type=F32)


def _const_spec(shape):
    nd = len(shape)
    return pl.BlockSpec(shape, lambda *_: (0,) * nd, pipeline_mode=pl.Buffered(1))


def _params(n_grid):
    return pltpu.CompilerParams(dimension_semantics=("arbitrary",) * n_grid,
                                vmem_limit_bytes=VMEM_LIMIT)


def _first_step():
    return (pl.program_id(0) == 0) & (pl.program_id(1) == 0)


def _ada_body(c_ref, w_ref, b_ref, o_ref):
    c = c_ref[...]
    s = (c * jax.nn.sigmoid(c)).astype(BF16)
    o_ref[...] = _dot(s, w_ref[...].astype(BF16)) + b_ref[...]


def _ada(c, w_ada, b_ada):
    rows = c.shape[0]
    n_out = w_ada.shape[1]
    bn = 512
    return pl.pallas_call(
        _ada_body,
        out_shape=jax.ShapeDtypeStruct((rows, n_out), F32),
        grid=(n_out // bn,),
        in_specs=[pl.BlockSpec((rows, D_MODEL), lambda j: (0, 0)),
                  pl.BlockSpec((D_MODEL, bn), lambda j: (0, j)),
                  pl.BlockSpec((1, bn), lambda j: (0, j))],
        out_specs=pl.BlockSpec((rows, bn), lambda j: (0, j)),
        compiler_params=_params(1),
        name="ada",
    )(c, w_ada, b_ada.reshape(1, n_out))


def _inproj_body(prompt, x_ref, sh_ref, sc_ref, wm_ref, bm_ref, wg_ref, bg_ref, wqa_ref, bqa_ref,
                 wkk_ref, bkk_ref, wv_ref, bv_ref, qk_ref, v_ref, g_ref, qa_ref, *rest):
    h = (_ln(x_ref[...]) * (1.0 + sc_ref[...]) + sh_ref[...]).astype(BF16)

    def proj(w_ref, b_ref, lo, n):
        return _dot(h, w_ref[:, lo:lo + n]) + b_ref[:, lo:lo + n]

    qk_ref[:, :DQK_M] = (proj(wm_ref, bm_ref, 0, DQK_M) * DHK_M ** -0.5).astype(qk_ref.dtype)
    qk_ref[:, DQK_M:] = proj(wm_ref, bm_ref, DQK_M, DQK_M).astype(qk_ref.dtype)
    v_ref[...] = proj(wm_ref, bm_ref, 2 * DQK_M, DV_M).astype(v_ref.dtype)
    g_ref[...] = proj(wg_ref, bg_ref, 0, LANES)
    qa_ref[...] = (proj(wqa_ref, bqa_ref, 0, DQ_A) * HD_A ** -0.5).astype(qa_ref.dtype)
    kk = proj(wkk_ref, bkk_ref, 0, 2 * DKV_A)
    if prompt:
        kk_ref, ka_ref, vat_ref = rest
        kk_ref[...] = kk.astype(BF16)
        ka_ref[...] = kk[:, :DKV_A]
        vat_ref[...] = _dot_nt(wv_ref[...], h) + bv_ref[...]
    else:
        ka_ref, va_ref = rest
        ka_ref[...] = kk[:, :DKV_A]
        va_ref[...] = _dot(h, wv_ref[...]) + bv_ref[...]


def _tok_spec(tm, n):
    return pl.BlockSpec((None, tm, n), lambda b, i: (b, i, 0))


def _mod_spec(rows, tm):
    if rows == 1:
        return pl.BlockSpec((None, 1, D_MODEL), lambda b, i: (b, 0, 0))
    return pl.BlockSpec((None, tm, D_MODEL), lambda b, i: (b, i, 0))


def _inproj(x, sh, sc, w, tm, prompt):
    bsz, seq, _ = x.shape
    mrows = sh.shape[1]
    act = BF16 if prompt else F32
    outs = [((bsz, seq, 2 * DQK_M), act, _tok_spec(tm, 2 * DQK_M)),
            ((bsz, seq, DV_M), act, _tok_spec(tm, DV_M)),
            ((bsz, seq, LANES), F32, _tok_spec(tm, LANES)),
            ((bsz, seq, DQ_A), act, _tok_spec(tm, DQ_A))]
    if prompt:
        wv, bv = w["wv_t"], w["bv_col"]
        outs += [((bsz, seq, 2 * DKV_A), BF16, _tok_spec(tm, 2 * DKV_A)),
                 ((bsz, seq, DKV_A), F32, _tok_spec(tm, DKV_A)),
                 ((bsz, DKV_A, seq), F32, pl.BlockSpec((None, DKV_A, tm), lambda b, i: (b, 0, i)))]
    else:
        wv, bv = w["wv"], w["bv_row"]
        outs += [((bsz, seq, DKV_A), F32, _tok_spec(tm, DKV_A)),
                 ((bsz, seq, DKV_A), F32, _tok_spec(tm, DKV_A))]
    weights = [w["wm"], w["bm"], w["wg"], w["bg"], w["wqa"], w["bqa"], w["wkk"], w["bkk"], wv, bv]
    return pl.pallas_call(
        functools.partial(_inproj_body, prompt),
        out_shape=[jax.ShapeDtypeStruct(s, dt) for s, dt, _ in outs],
        grid=(bsz, seq // tm),
        in_specs=[_tok_spec(tm, D_MODEL), _mod_spec(mrows, tm), _mod_spec(mrows, tm)]
                 + [_const_spec(a.shape) for a in weights],
        out_specs=[spec for _, _, spec in outs],
        compiler_params=_params(2),
        name="inproj",
    )(x, sh, sc, *weights)


def _scan_rows(x, op, rows):
    row = lax.broadcasted_iota(jnp.int32, x.shape, 0)
    d = 1
    while d < rows:
        shifted = pltpu.roll(x, d, axis=0)
        x = jnp.where(row >= d, op(x, shifted), x)
        d *= 2
    return x


def _mlstm_body(t_in, t, nb, qk_ref, v_ref, g_ref, c0_ref, n0_ref, m0_ref, nw_ref,
                h_ref, c_ref, n_ref, m_ref, *pads):
    @pl.when(pl.program_id(1) == 0)
    def _():
        c_ref[...] = c0_ref[...]
        n_ref[...] = n0_ref[...]
        m_ref[...] = m0_ref[...]

    if t_in != t:
        @pl.when(_first_step())
        def _():
            for p in pads:
                p[...] = jnp.zeros(p.shape, p.dtype)
        for src, dst in zip((qk_ref, v_ref, g_ref), pads):
            dst[:, pl.ds(0, t_in), :] = src[...].astype(dst.dtype)
        qk_src, v_src, g_src = pads
    else:
        qk_src, v_src, g_src = qk_ref, v_ref, g_ref

    row = lax.broadcasted_iota(jnp.int32, (t, LANES), 0)
    lane1 = lax.broadcasted_iota(jnp.int32, (1, LANES), 1)
    r2 = lax.broadcasted_iota(jnp.int32, (t, t), 0)
    c2 = lax.broadcasted_iota(jnp.int32, (t, t), 1)
    causal = c2 <= r2

    seqs = []
    for s in range(nb):
        g = g_src[s]
        lf = jax.nn.log_sigmoid(g)
        ig = g
        if t_in != t:
            lf = jnp.where(row < t_in, lf, 0.0)
            ig = jnp.where(row < t_in, ig, NEG)
        b = pltpu.roll(_scan_rows(lf, jnp.add, t), LANES - NH_M, axis=1)
        gg = ig - b
        gmx = _scan_rows(gg, jnp.maximum, t)
        m_prev = m_ref[s]
        a = b + m_prev
        mt = jnp.maximum(a, b + gmx)
        b_last, gmx_last = b[t - 1:t, :], gmx[t - 1:t, :]
        m_new = jnp.maximum(b_last + m_prev, b_last + gmx_last)
        seqs.append(dict(
            a_in=jnp.exp(a - mt), eb=b - mt, lowb=jnp.exp(-mt), gt=gg.T,
            ws=jnp.exp(b_last + gg - m_new), decay=jnp.exp(b_last + m_prev - m_new)))
        m_ref[s] = jnp.where(lane1 < NH_M, m_new, 0.0)

    units = [(s, hd) for s in range(nb) for hd in range(NH_M)]
    nu = len(units)
    col = lambda name, s, hd: seqs[s][name][:, hd:hd + 1]
    q = [qk_src[s, :, hd * DHK_M:(hd + 1) * DHK_M].astype(BF16) for s, hd in units]
    k = [qk_src[s, :, DQK_M + hd * DHK_M:DQK_M + (hd + 1) * DHK_M].astype(BF16) for s, hd in units]
    v = [v_src[s, :, hd * DHV_M:(hd + 1) * DHV_M].astype(BF16) for s, hd in units]
    c_old = [c_ref[s, hd] for s, hd in units]
    n_old = [n_ref[s, hd:hd + 1, :] for s, hd in units]

    sqk = [_dot_nt(q[u], k[u]) for u in range(nu)]
    dexp = [jnp.exp(jnp.where(causal, col("eb", s, hd) + seqs[s]["gt"][hd:hd + 1, :], NEG))
            for s, hd in units]
    smat = [sqk[u] * dexp[u] for u in range(nu)]
    intra = [_dot(smat[u].astype(BF16), v[u]) for u in range(nu)]
    inter = [_dot_nt(q[u], c_old[u].astype(BF16)) for u in range(nu)]
    qn = [jnp.sum(q[u].astype(F32) * n_old[u].astype(BF16).astype(F32), axis=-1, keepdims=True)
          for u in range(nu)]
    den = [jnp.sum(smat[u], axis=-1, keepdims=True) + col("a_in", s, hd) * qn[u]
           for u, (s, hd) in enumerate(units)]
    hh = [(intra[u] + col("a_in", s, hd) * inter[u])
          / jnp.maximum(jnp.abs(den[u]), col("lowb", s, hd)) for u, (s, hd) in enumerate(units)]
    hn = [_ln(hh[u]) * nw_ref[:, hd * DHV_M:(hd + 1) * DHV_M] for u, (s, hd) in enumerate(units)]
    for u, (s, hd) in enumerate(units):
        h_ref[s, :, hd * DHV_M:(hd + 1) * DHV_M] = hn[u] if t_in == t else hn[u][:t_in]

    kw = [k[u].astype(F32) * col("ws", s, hd) for u, (s, hd) in enumerate(units)]
    upd = [_dot_tn(v[u], kw[u].astype(BF16)) for u in range(nu)]
    for u, (s, hd) in enumerate(units):
        dec = seqs[s]["decay"][:, hd:hd + 1]
        c_ref[s, hd] = dec * c_old[u] + upd[u]
        n_ref[s, hd:hd + 1, :] = dec * n_old[u] + jnp.sum(kw[u], axis=0, keepdims=True)


def _mlstm(qk, v, g, c0, n0, m0, norm_w, t_in, nb):
    nseq, seq, _ = qk.shape
    t = CHUNK if t_in == CHUNK else SUBLANES
    blk = lambda n: pl.BlockSpec((nb, t_in, n), lambda o, c: (o, c, 0))
    st_c = pl.BlockSpec((nb, NH_M, DHV_M, DHK_M), lambda o, c: (o, 0, 0, 0))
    st_n = pl.BlockSpec((nb, NH_M, DHK_M), lambda o, c: (o, 0, 0))
    st_m = pl.BlockSpec((nb, 1, LANES), lambda o, c: (o, 0, 0))
    pads = []
    if t_in != t:
        pads = [pltpu.VMEM((nb, t, 2 * DQK_M), F32), pltpu.VMEM((nb, t, DV_M), F32),
                pltpu.VMEM((nb, t, LANES), F32)]
    return pl.pallas_call(
        functools.partial(_mlstm_body, t_in, t, nb),
        out_shape=[jax.ShapeDtypeStruct((nseq, seq, DV_M), F32),
                   jax.ShapeDtypeStruct((nseq, NH_M, DHV_M, DHK_M), F32),
                   jax.ShapeDtypeStruct((nseq, NH_M, DHK_M), F32),
                   jax.ShapeDtypeStruct((nseq, 1, LANES), F32)],
        grid=(nseq // nb, seq // t_in),
        in_specs=[blk(2 * DQK_M), blk(DV_M), blk(LANES), st_c, st_n, st_m,
                  pl.BlockSpec((1, DV_M), lambda o, c: (0, 0))],
        out_specs=[blk(DV_M), st_c, st_n, st_m],
        scratch_shapes=pads,
        compiler_params=_params(2),
        name="mlstm",
    )(qk, v, g, c0, n0, m0, norm_w)


def _swa_prompt_body(prm_ref, q_ref, kc_ref, kp_ref, vc_ref, vp_ref, o_ref, tbl):
    row = lax.broadcasted_iota(jnp.int32, (WINDOW, WINDOW), 0)
    col = lax.broadcasted_iota(jnp.int32, (WINDOW, WINDOW), 1)
    tri = row <= col
    lo = col < HD_A

    @pl.when(_first_step())
    def _():
        dist = jnp.where(tri, col - row, col - row + WINDOW).astype(F32)
        for hd in range(NH_A):
            bias = prm_ref[0, hd] * dist
            tbl[0, hd] = bias
            tbl[1, hd] = bias + jnp.where(tri, 0.0, -NEG)

    which = jnp.where(pl.program_id(1) == 0, 1, 0)
    zb = jnp.zeros((WINDOW, LANES), BF16)

    def placed(k_ref, kv):
        c, par = kv // 2, kv % 2
        own = k_ref[:, c * LANES:(c + 1) * LANES]
        swp = k_ref[:, DKV_A + c * LANES:DKV_A + (c + 1) * LANES]
        if par == 0:
            return jnp.where(lo, own, zb), jnp.where(lo, zb, swp)
        return jnp.where(lo, swp, zb), jnp.where(lo, zb, own)

    st = []
    for kv in range(NKV_A):
        lhs = jnp.concatenate([*placed(kc_ref, kv), *placed(kp_ref, kv)], axis=0)
        qg = jnp.concatenate([q_ref[:, (2 * kv) * LANES:(2 * kv + 1) * LANES],
                              q_ref[:, (2 * kv + 1) * LANES:(2 * kv + 2) * LANES]], axis=0)
        st.append(_dot_nt(lhs, qg))
    tiles = []
    for hd in range(NH_A):
        kv, a, par = hd // GROUP_A, (hd % GROUP_A) // 2, hd % 2
        cols = slice(a * WINDOW, (a + 1) * WINDOW)
        tiles.append(jnp.where(tri, st[kv][par * WINDOW:(par + 1) * WINDOW, cols],
                               st[kv][(2 + par) * WINDOW:(3 + par) * WINDOW, cols]))
    sc = jnp.concatenate(tiles, axis=0).reshape(NH_A, WINDOW, WINDOW) - tbl[which]
    sink = jnp.concatenate([jnp.full((1, 1, WINDOW), prm_ref[1, hd], F32) for hd in range(NH_A)], axis=0)
    mx = jnp.maximum(jnp.max(sc, axis=1, keepdims=True), sink)
    p = jnp.exp(sc - mx)
    den = jnp.sum(p, axis=1, keepdims=True) + jnp.exp(sink - mx)
    pn = p * (1.0 / den)

    zero = jnp.zeros((WINDOW, WINDOW), F32)
    z64 = jnp.zeros((HD_A, WINDOW), BF16)
    outs = []
    for kv in range(NKV_A):
        cols = []
        for a in range(2):
            pe, po = pn[kv * GROUP_A + 2 * a], pn[kv * GROUP_A + 2 * a + 1]
            cols.append(jnp.concatenate([jnp.where(tri, pe, zero), jnp.where(tri, po, zero),
                                         jnp.where(tri, zero, pe), jnp.where(tri, zero, po)],
                                        axis=0).astype(BF16))
        pt = jnp.concatenate(cols, axis=1)
        vc = vc_ref[kv * HD_A:(kv + 1) * HD_A, :].astype(BF16)
        vp = vp_ref[kv * HD_A:(kv + 1) * HD_A, :].astype(BF16)
        vt = jnp.concatenate([jnp.concatenate([vc, z64], axis=0), jnp.concatenate([z64, vc], axis=0),
                              jnp.concatenate([vp, z64], axis=0), jnp.concatenate([z64, vp], axis=0)],
                             axis=1)
        ot = _dot(vt, pt)
        outs += [ot[:, :WINDOW].T, ot[:, WINDOW:].T]
    o_ref[...] = jnp.concatenate(outs, axis=1).astype(o_ref.dtype)


def _swa_prompt(prm, q, kk, vat):
    bsz, seq, _ = q.shape
    prev = lambda i: jnp.maximum(i - 1, 0)
    return pl.pallas_call(
        _swa_prompt_body,
        out_shape=jax.ShapeDtypeStruct((bsz, seq, DQ_A), BF16),
        grid=(bsz, seq // WINDOW),
        in_specs=[pl.BlockSpec(memory_space=pltpu.SMEM),
                  pl.BlockSpec((None, WINDOW, DQ_A), lambda b, i: (b, i, 0)),
                  pl.BlockSpec((None, WINDOW, 2 * DKV_A), lambda b, i: (b, i, 0)),
                  pl.BlockSpec((None, WINDOW, 2 * DKV_A), lambda b, i: (b, prev(i), 0)),
                  pl.BlockSpec((None, DKV_A, WINDOW), lambda b, i: (b, 0, i)),
                  pl.BlockSpec((None, DKV_A, WINDOW), lambda b, i: (b, 0, prev(i)))],
        out_specs=pl.BlockSpec((None, WINDOW, DQ_A), lambda b, i: (b, i, 0)),
        scratch_shapes=[pltpu.VMEM((2, NH_A, WINDOW, WINDOW), F32)],
        compiler_params=_params(2),
        name="swa_prompt",
    )(prm, q, kk, kk, vat, vat)


def _half_mask(shape, half):
    lane = lax.broadcasted_iota(jnp.int32, shape, 1)
    return lane < HD_A if half == 0 else lane >= HD_A


def _swa_sample_body(t_in, nb, prm_ref, q_ref, kn_ref, vn_ref, kct_ref, vct_ref,
                     o_ref, kco_ref, vco_ref, tbl, q8, kn_pad, vn_pad):
    tq = SUBLANES
    rows = NH_A * tq
    row = lax.broadcasted_iota(jnp.int32, (rows, WINDOW), 0)
    col = lax.broadcasted_iota(jnp.int32, (rows, WINDOW), 1)
    tri = col <= (row & (tq - 1))

    @pl.when(_first_step())
    def _():
        r8 = lax.broadcasted_iota(jnp.int32, (tq, WINDOW), 0)
        c8 = lax.broadcasted_iota(jnp.int32, (tq, WINDOW), 1)
        dist = jnp.where(c8 <= r8, r8 - c8, r8 - c8 + WINDOW).astype(F32)
        for hd in range(NH_A):
            tbl[pl.ds(hd * tq, tq), :] = prm_ref[0, hd] * dist
        for p in (q8, kn_pad, vn_pad):
            p[...] = jnp.zeros(p.shape, p.dtype)

    q8[:, pl.ds(0, t_in), :] = q_ref[...]
    kn_pad[:, pl.ds(0, t_in), :] = kn_ref[...]
    vn_pad[:, pl.ds(0, t_in), :] = vn_ref[...]
    sink_col = jnp.concatenate([jnp.full((tq, 1), prm_ref[1, hd], F32) for hd in range(NH_A)], axis=0)
    z8 = jnp.zeros((tq, LANES), F32)
    n_chunk = NKV_A // 2
    heads_per_chunk = NH_A // n_chunk

    def place(piece, src_half, dst_half):
        if src_half != dst_half:
            piece = pltpu.roll(piece, HD_A, axis=1)
        return jnp.where(_half_mask(piece.shape, dst_half), piece, z8)

    qexp = []
    for s in range(nb):
        per_c = []
        for c in range(n_chunk):
            pieces = []
            for hl in range(heads_per_chunk):
                hd = c * heads_per_chunk + hl
                pieces.append(place(q8[s, :, (hd // 2) * LANES:(hd // 2 + 1) * LANES],
                                    hd % 2, hl // GROUP_A))
            per_c.append(jnp.concatenate(pieces, axis=0).astype(BF16))
        qexp.append(per_c)
    csl = lambda c: slice(c * LANES, (c + 1) * LANES)
    s_prev = [[_dot(qexp[s][c], kct_ref[s, csl(c), :].astype(BF16)) for c in range(n_chunk)]
              for s in range(nb)]
    s_cur = [[_dot_nt(qexp[s][c], kn_pad[s, :, csl(c)].astype(BF16)) for c in range(n_chunk)]
             for s in range(nb)]
    sc = [jnp.where(tri, jnp.concatenate(s_cur[s], axis=0), jnp.concatenate(s_prev[s], axis=0)) - tbl[...]
          for s in range(nb)]
    mx = [jnp.maximum(jnp.max(sc[s], axis=-1, keepdims=True), sink_col) for s in range(nb)]
    p = [jnp.exp(sc[s] - mx[s]) for s in range(nb)]
    den = [jnp.sum(p[s], axis=-1, keepdims=True) + jnp.exp(sink_col - mx[s]) for s in range(nb)]
    pn = [p[s] * (1.0 / den[s]) for s in range(nb)]
    zero = jnp.zeros((rows, WINDOW), F32)
    pc = [jnp.where(tri, pn[s], zero).astype(BF16) for s in range(nb)]
    pp = [jnp.where(tri, zero, pn[s]).astype(BF16) for s in range(nb)]
    half_rows = heads_per_chunk * tq
    oc = [[_dot_nt(pp[s][c * half_rows:(c + 1) * half_rows], vct_ref[s, csl(c), :].astype(BF16))
           + _dot(pc[s][c * half_rows:(c + 1) * half_rows], vn_pad[s, :, csl(c)].astype(BF16))
           for c in range(n_chunk)] for s in range(nb)]
    for s in range(nb):
        chunks = []
        for pch in range(NH_A // 2):
            acc = None
            for hd in (2 * pch, 2 * pch + 1):
                c, hl = hd // heads_per_chunk, hd % heads_per_chunk
                piece = place(oc[s][c][hl * tq:(hl + 1) * tq, :], hl // GROUP_A, hd % 2)
                acc = piece if acc is None else acc + piece
            chunks.append(acc)
        o_ref[s] = jnp.concatenate(chunks, axis=1)[:t_in]

    lane = lax.broadcasted_iota(jnp.int32, (DKV_A, WINDOW), 1)
    for new_pad, old_ref, out_ref in ((kn_pad, kct_ref, kco_ref), (vn_pad, vct_ref, vco_ref)):
        for s in range(nb):
            merged = jnp.where(lane < t_in, new_pad[s].T, old_ref[s])
            out_ref[s] = pltpu.roll(merged, WINDOW - t_in, axis=1)


def _swa_sample(prm, q, kn, vn, kct, vct, nb):
    nseq, t_in, _ = q.shape
    assert t_in <= SUBLANES
    cur = lambda n: pl.BlockSpec((nb, t_in, n), lambda o, i: (o, 0, 0))
    win = pl.BlockSpec((nb, DKV_A, WINDOW), lambda o, i: (o, 0, 0))
    return pl.pallas_call(
        functools.partial(_swa_sample_body, t_in, nb),
        out_shape=[jax.ShapeDtypeStruct((nseq, t_in, DQ_A), F32),
                   jax.ShapeDtypeStruct((nseq, DKV_A, WINDOW), F32),
                   jax.ShapeDtypeStruct((nseq, DKV_A, WINDOW), F32)],
        grid=(nseq // nb, 1),
        in_specs=[pl.BlockSpec(memory_space=pltpu.SMEM), cur(DQ_A), cur(DKV_A), cur(DKV_A), win, win],
        out_specs=[cur(DQ_A), win, win],
        scratch_shapes=[pltpu.VMEM((NH_A * SUBLANES, WINDOW), F32),
                        pltpu.VMEM((nb, SUBLANES, DQ_A), F32),
                        pltpu.VMEM((nb, WINDOW, DKV_A), F32),
                        pltpu.VMEM((nb, WINDOW, DKV_A), F32)],
        compiler_params=_params(2),
        name="swa_sample",
    )(prm, q, kn, vn, kct, vct)


def _merge_body(alpha, x_ref, sh_ref, sc_ref, g1_ref, hm_ref, ha_ref, wog_ref, bog_ref,
                wgg_ref, bgg_ref, wbm_ref, wba_ref, wo_ref, lg_ref, lb_ref, o_ref):
    x = x_ref[...]
    h = (_ln(x) * (1.0 + sc_ref[...]) + sh_ref[...]).astype(BF16)

    def branch_gate(j):
        cols = slice(j * D_MODEL, (j + 1) * D_MODEL)
        return jax.nn.sigmoid(_dot(h, wgg_ref[:, cols]) + bgg_ref[:, cols])

    hm = (hm_ref[...] * jax.nn.sigmoid(_dot(h, wog_ref[...]) + bog_ref[...])).astype(BF16)
    merged = branch_gate(0) * _dot(hm, wbm_ref[...])
    merged = merged + branch_gate(1) * _dot(ha_ref[...].astype(BF16), wba_ref[...])
    mo = _dot(merged.astype(BF16), wo_ref[...])
    o_ref[...] = _ln(alpha * x + g1_ref[...] * mo) * lg_ref[...] + lb_ref[...]


def _merge(x, sh, sc, g1, hm, ha, w, tm, alpha):
    bsz, seq, _ = x.shape
    mrows = sh.shape[1]
    weights = [w["wog"], w["bog"], w["wgg"], w["bgg"], w["wbm"], w["wba"], w["wo"],
               w["ln1_g"], w["ln1_b"]]
    return pl.pallas_call(
        functools.partial(_merge_body, alpha),
        out_shape=jax.ShapeDtypeStruct((bsz, seq, D_MODEL), F32),
        grid=(bsz, seq // tm),
        in_specs=[_tok_spec(tm, D_MODEL), _mod_spec(mrows, tm), _mod_spec(mrows, tm),
                  _mod_spec(mrows, tm), _tok_spec(tm, DV_M), _tok_spec(tm, DQ_A)]
                 + [_const_spec(a.shape) for a in weights],
        out_specs=_tok_spec(tm, D_MODEL),
        compiler_params=_params(2),
        name="merge",
    )(x, sh, sc, g1, hm, ha, *weights)


def _ffn_body(alpha, tm, stride, halo, x_ref, sh_ref, sc_ref, g2_ref, cb0_ref, wup_ref, bup_ref,
              cw_ref, cbias_ref, wdn_ref, bdn_ref, lg_ref, lb_ref, o_ref, cs_ref, ubuf):
    @pl.when(pl.program_id(1) == 0)
    def _():
        cs_ref[...] = cb0_ref[...]

    x = x_ref[...]
    h = (_ln(x) * (1.0 + sc_ref[...]) + sh_ref[...]).astype(BF16)
    n_chunks = D_FF // FF_CHUNK

    def up(c):
        cols = [slice(half * D_FF + c * FF_CHUNK, half * D_FF + (c + 1) * FF_CHUNK) for half in range(2)]
        return [(cs, _dot(h, wup_ref[:, cs]) + bup_ref[:, cs]) for cs in cols]

    acc = jnp.zeros((tm, D_MODEL), F32)
    ahead = up(0)
    for c in range(n_chunks):
        now = ahead
        if c + 1 < n_chunks:
            ahead = up(c + 1)
        ys = []
        for cols, u in now:
            ubuf[pl.ds(0, halo), :] = cs_ref[:, cols]
            ubuf[pl.ds(halo, tm), :] = u
            cs_ref[:, cols] = ubuf[pl.ds(tm, halo), :]
            y = cbias_ref[:, cols] + u * cw_ref[CONV_W - 1:CONV_W, cols]
            for j in range(CONV_W - 1):
                y = y + ubuf[pl.ds(halo - (CONV_W - 1 - j) * stride, tm), :] * cw_ref[j:j + 1, cols]
            ys.append(y)
        act = (jax.nn.gelu(ys[0]) * ys[1]).astype(BF16)
        acc = acc + _dot(act, wdn_ref[c * FF_CHUNK:(c + 1) * FF_CHUNK, :])
    f = acc + bdn_ref[...]
    o_ref[...] = _ln(alpha * x + g2_ref[...] * f) * lg_ref[...] + lb_ref[...]


def _ffn(x, sh, sc, g2, cb0, w, tm, stride, alpha):
    bsz, seq, _ = x.shape
    mrows = sh.shape[1]
    halo = cb0.shape[1]
    cs = pl.BlockSpec((None, halo, 2 * D_FF), lambda b, i: (b, 0, 0))
    weights = [w["wup"], w["bup"], w["cw"], w["cbias"], w["wdn"], w["bdn"], w["ln2_g"], w["ln2_b"]]
    return pl.pallas_call(
        functools.partial(_ffn_body, alpha, tm, stride, halo),
        out_shape=[jax.ShapeDtypeStruct((bsz, seq, D_MODEL), F32),
                   jax.ShapeDtypeStruct((bsz, halo, 2 * D_FF), F32)],
        grid=(bsz, seq // tm),
        in_specs=[_tok_spec(tm, D_MODEL), _mod_spec(mrows, tm), _mod_spec(mrows, tm),
                  _mod_spec(mrows, tm), cs] + [_const_spec(a.shape) for a in weights],
        out_specs=[_tok_spec(tm, D_MODEL), cs],
        scratch_shapes=[pltpu.VMEM((halo + tm, FF_CHUNK), F32)],
        compiler_params=_params(2),
        name="ffn",
    )(x, sh, sc, g2, cb0, *weights)


def _prep_weights(w_in, b_in, mlstm_norm_w, w_branch_m, w_branch_a, w_out, ln1_g, ln1_b,
                  w_up, b_up, conv_w, conv_b, w_down, b_down, ln2_g, ln2_b):
    o_gate = 2 * DQK_M + DV_M
    o_og = o_gate + 2 * NH_M
    o_qa = o_og + DV_M
    o_ka = o_qa + DQ_A
    o_va = o_ka + DKV_A
    o_gm = o_va + DKV_A
    row = lambda a: a.reshape(1, -1)
    swap = lambda a: a.reshape(a.shape[:-1] + (DKV_A // LANES, 2, HD_A))[..., ::-1, :].reshape(a.shape)
    gate_pad = LANES - 2 * NH_M
    w_k, b_k = w_in[:, o_ka:o_va], b_in[o_ka:o_va]
    w_v, b_v = w_in[:, o_va:o_gm], b_in[o_va:o_gm]
    return dict(
        wm=w_in[:, :o_gate].astype(BF16), bm=row(b_in[:o_gate]),
        wg=jnp.pad(w_in[:, o_gate:o_og], ((0, 0), (0, gate_pad))).astype(BF16),
        bg=row(jnp.pad(b_in[o_gate:o_og], (0, gate_pad))),
        wqa=w_in[:, o_qa:o_ka].astype(BF16), bqa=row(b_in[o_qa:o_ka]),
        wkk=jnp.concatenate([w_k, swap(w_k)], axis=1).astype(BF16),
        bkk=row(jnp.concatenate([b_k, swap(b_k)])),
        wv=w_v.astype(BF16), bv_row=row(b_v), wv_t=w_v.T.astype(BF16), bv_col=b_v.reshape(-1, 1),
        wog=w_in[:, o_og:o_qa].astype(BF16), bog=row(b_in[o_og:o_qa]),
        wgg=w_in[:, o_gm:].astype(BF16), bgg=row(b_in[o_gm:]),
        norm_w=row(mlstm_norm_w), wbm=w_branch_m.astype(BF16), wba=w_branch_a.astype(BF16),
        wo=w_out.astype(BF16), ln1_g=row(ln1_g), ln1_b=row(ln1_b),
        wup=w_up.astype(BF16), bup=row(b_up), cw=conv_w, cbias=row(conv_b),
        wdn=w_down.astype(BF16), bdn=row(b_down), ln2_g=row(ln2_g), ln2_b=row(ln2_b))


def kernel(x_prompt, x_sample, c_prompt, c_sample, state_mlstm_C, state_mlstm_n, state_mlstm_m,
           cache_k_win, cache_v_win, state_ffn_conv, w_ada, b_ada, w_in, b_in, mlstm_norm_w,
           attn_sinks, w_branch_m, w_branch_a, w_out, ln1_g, ln1_b, w_up, b_up, conv_w, conv_b,
           w_down, b_down, ln2_g, ln2_b):
    depth = w_in.shape[0]
    bp, lp, _ = x_prompt.shape
    bs, ls, _ = x_sample.shape
    assert cache_k_win.shape[2] == WINDOW
    alpha = (2 * depth) ** 0.25
    dt = x_prompt.dtype
    slopes = jnp.exp2(-8.0 * jnp.arange(1, NH_A + 1, dtype=F32) / NH_A)
    tm_p = 512
    ns = bs * ls

    yp = x_prompt
    ys = x_sample.reshape(1, ns, D_MODEL)
    new_p, new_s = [], []
    n_c = bp + bs
    c_rows = -(-n_c // SUBLANES) * SUBLANES
    c_all = jnp.concatenate([c_sample, c_prompt, jnp.zeros((c_rows - n_c, D_MODEL), dt)], axis=0)
    for l in range(depth):
        w = _prep_weights(w_in[l], b_in[l], mlstm_norm_w[l], w_branch_m[l], w_branch_a[l], w_out[l],
                          ln1_g[l], ln1_b[l], w_up[l], b_up[l], conv_w[l], conv_b[l], w_down[l],
                          b_down[l], ln2_g[l], ln2_b[l])
        prm = jnp.stack([slopes, attn_sinks[l].astype(F32)])
        mod = _ada(c_all, w_ada[l], b_ada[l])
        mod_s = mod[:bs].reshape(bs, N_MOD, D_MODEL)
        mod_p = mod[bs:bs + bp].reshape(bp, N_MOD, D_MODEL)
        mp_ = [mod_p[:, j:j + 1] for j in range(N_MOD)]
        ms_seq = [jnp.repeat(mod_s[:, j], ls, axis=0)[None] for j in range(3)]
        ms_pos = [jnp.tile(mod_s[:, j], (ls, 1))[None] for j in range(3, N_MOD)]

        qk, v, g, qa, kk, ka, vat = _inproj(yp, mp_[0], mp_[1], w, tm_p, True)
        hm, cp, np_, mp = _mlstm(qk, v, g, jnp.zeros((bp, NH_M, DHV_M, DHK_M), dt),
                                 jnp.zeros((bp, NH_M, DHK_M), dt), jnp.zeros((bp, 1, LANES), dt),
                                 w["norm_w"], CHUNK, bp)
        ha = _swa_prompt(prm, qa, kk, vat)
        x1p = _merge(yp, mp_[0], mp_[1], mp_[2], hm, ha, w, tm_p, alpha)
        halo_p = SUBLANES
        yp, csp = _ffn(x1p, mp_[3], mp_[4], mp_[5], jnp.zeros((bp, halo_p, 2 * D_FF), dt),
                       w, tm_p, 1, alpha)
        p_k = ka[:, lp - WINDOW:].reshape(bp, WINDOW, NKV_A, HD_A)
        p_v = vat[:, :, lp - WINDOW:].reshape(bp, NKV_A, HD_A, WINDOW).transpose(0, 3, 1, 2)
        new_p.append((cp, np_, mp[:, 0, :NH_M], p_k, p_v, csp[:, halo_p - (CONV_W - 1):]))

        qk, v, g, qa, kn, vn = _inproj(ys, ms_seq[0], ms_seq[1], w, ns, False)
        per_seq = lambda a: a.reshape(bs, ls, a.shape[-1])
        m0 = jnp.pad(state_mlstm_m[l], ((0, 0), (0, LANES - NH_M)))[:, None, :]
        hm, cs_, ns_, ms = _mlstm(per_seq(qk), per_seq(v), per_seq(g), state_mlstm_C[l],
                                  state_mlstm_n[l], m0, w["norm_w"], ls, SEQS_PER_STEP)
        to_t = lambda a: a.transpose(0, 2, 3, 1).reshape(bs, DKV_A, WINDOW)
        from_t = lambda a: a.reshape(bs, NKV_A, HD_A, WINDOW).transpose(0, 3, 1, 2)
        ha, kct, vct = _swa_sample(prm, per_seq(qa), per_seq(kn), per_seq(vn),
                                   to_t(cache_k_win[l]), to_t(cache_v_win[l]), SEQS_PER_STEP)
        x1s = _merge(ys, ms_seq[0], ms_seq[1], ms_seq[2], hm.reshape(1, ns, DV_M),
                     ha.reshape(1, ns, DQ_A), w, ns, alpha)
        to_pos = lambda a: a.reshape(bs, -1, a.shape[-1]).transpose(1, 0, 2).reshape(1, -1, a.shape[-1])
        y_pos, css = _ffn(to_pos(x1s[0]), ms_pos[0], ms_pos[1], ms_pos[2], to_pos(state_ffn_conv[l]),
                          w, ns, bs, alpha)
        ys = y_pos.reshape(ls, bs, D_MODEL).transpose(1, 0, 2).reshape(1, ns, D_MODEL)
        conv_s = css.reshape(CONV_W - 1, bs, 2 * D_FF).transpose(1, 0, 2)
        new_s.append((cs_, ns_, ms[:, 0, :NH_M], from_t(kct), from_t(vct), conv_s))

    p_state = [jnp.stack(a) for a in zip(*new_p)]
    s_state = [jnp.stack(a) for a in zip(*new_s)]
    return (yp, ys.reshape(bs, ls, D_MODEL), *p_state, *s_state)
```

```python
import functools

import jax
import jax.numpy as jnp
from jax import lax
from jax.experimental import pallas as pl
from jax.experimental.pallas import tpu as pltpu

F32 = jnp.float32
BF16 = jnp.bfloat16

D_MODEL = 1024
NH_M, DHK_M, DHV_M = 4, 128, 256
DQK_M, DV_M = NH_M * DHK_M, NH_M * DHV_M
NH_A, NKV_A, HD_A = 16, 4, 64
GROUP_A = NH_A // NKV_A
WINDOW = 128
DQ_A, DKV_A = NH_A * HD_A, NKV_A * HD_A
D_FF = 2816
CONV_W = 3
N_MOD = 6
LN_EPS = 1e-5
CHUNK = 128
NEG = -1e30
LANES = 128
SUBLANES = 8
VMEM_LIMIT = 56 * 1024 * 1024
SEQS_PER_STEP = 4


def _ln(x):
    mu = jnp.mean(x, axis=-1, keepdims=True)
    xc = x - mu
    var = jnp.mean(xc * xc, axis=-1, keepdims=True)
    return xc * lax.rsqrt(var + LN_EPS)


def _dot(a, b):
    return jnp.dot(a, b, preferred_element_type=F32)


def _dot_nt(a, b):
    return lax.dot_general(a, b, (((1,), (1,)), ((), ())), preferred_element_type=F32)


def _dot_tn(a, b):
    return lax.dot_general(a, b, (((0,), (0,)), ((), ())), preferred_element_type=F32)


def _const_spec(shape):
    nd = len(shape)
    return pl.BlockSpec(shape, lambda *_: (0,) * nd, pipeline_mode=pl.Buffered(1))


def _params(n_grid):
    return pltpu.CompilerParams(dimension_semantics=("arbitrary",) * n_grid,
                                vmem_limit_bytes=VMEM_LIMIT)


def _first_step():
    return (pl.program_id(0) == 0) & (pl.program_id(1) == 0)


def _ada_body(c_ref, w_ref, b_ref, o_ref):
    c = c_ref[...]
    s = (c * jax.nn.sigmoid(c)).astype(BF16)
    o_ref[...] = _dot(s, w_ref[...].astype(BF16)) + b_ref[...]


def _ada(c, w_ada, b_ada):
    rows = c.shape[0]
    n_out = w_ada.shape[1]
    bn = 512
    return pl.pallas_call(
        _ada_body,
        out_shape=jax.ShapeDtypeStruct((rows, n_out), F32),
        grid=(n_out // bn,),
        in_specs=[pl.BlockSpec((rows, D_MODEL), lambda j: (0, 0)),
                  pl.BlockSpec((D_MODEL, bn), lambda j: (0, j)),
                  pl.BlockSpec((1, bn), lambda j: (0, j))],
        out_specs=pl.BlockSpec((rows, bn), lambda j: (0, j)),
        compiler_params=_params(1),
        name="ada",
    )(c, w_ada, b_ada.reshape(1, n_out))


def _inproj_body(prompt, x_ref, sh_ref, sc_ref, wm_ref, bm_ref, wg_ref, bg_ref, wqa_ref, bqa_ref,
                 wkk_ref, bkk_ref, wv_ref, bv_ref, qk_ref, v_ref, g_ref, qa_ref, *rest):
    h = (_ln(x_ref[...]) * (1.0 + sc_ref[...]) + sh_ref[...]).astype(BF16)

    def proj(w_ref, b_ref, lo, n):
        return _dot(h, w_ref[:, lo:lo + n]) + b_ref[:, lo:lo + n]

    qk_ref[:, :DQK_M] = (proj(wm_ref, bm_ref, 0, DQK_M) * DHK_M ** -0.5).astype(qk_ref.dtype)
    qk_ref[:, DQK_M:] = proj(wm_ref, bm_ref, DQK_M, DQK_M).astype(qk_ref.dtype)
    v_ref[...] = proj(wm_ref, bm_ref, 2 * DQK_M, DV_M).astype(v_ref.dtype)
    g_ref[...] = proj(wg_ref, bg_ref, 0, LANES)
    qa_ref[...] = (proj(wqa_ref, bqa_ref, 0, DQ_A) * HD_A ** -0.5).astype(qa_ref.dtype)
    kk = proj(wkk_ref, bkk_ref, 0, 2 * DKV_A)
    if prompt:
        kk_ref, ka_ref, vat_ref = rest
        kk_ref[...] = kk.astype(BF16)
        ka_ref[...] = kk[:, :DKV_A]
        vat_ref[...] = _dot_nt(wv_ref[...], h) + bv_ref[...]
    else:
        ka_ref, va_ref = rest
        ka_ref[...] = kk[:, :DKV_A]
        va_ref[...] = _dot(h, wv_ref[...]) + bv_ref[...]


def _tok_spec(tm, n):
    return pl.BlockSpec((None, tm, n), lambda b, i: (b, i, 0))


def _mod_spec(rows, tm):
    if rows == 1:
        return pl.BlockSpec((None, 1, D_MODEL), lambda b, i: (b, 0, 0))
    return pl.BlockSpec((None, tm, D_MODEL), lambda b, i: (b, i, 0))


def _inproj(x, sh, sc, w, tm, prompt):
    bsz, seq, _ = x.shape
    mrows = sh.shape[1]
    act = BF16 if prompt else F32
    outs = [((bsz, seq, 2 * DQK_M), act, _tok_spec(tm, 2 * DQK_M)),
            ((bsz, seq, DV_M), act, _tok_spec(tm, DV_M)),
            ((bsz, seq, LANES), F32, _tok_spec(tm, LANES)),
            ((bsz, seq, DQ_A), act, _tok_spec(tm, DQ_A))]
    if prompt:
        wv, bv = w["wv_t"], w["bv_col"]
        outs += [((bsz, seq, 2 * DKV_A), BF16, _tok_spec(tm, 2 * DKV_A)),
                 ((bsz, seq, DKV_A), F32, _tok_spec(tm, DKV_A)),
                 ((bsz, DKV_A, seq), F32, pl.BlockSpec((None, DKV_A, tm), lambda b, i: (b, 0, i)))]
    else:
        wv, bv = w["wv"], w["bv_row"]
        outs += [((bsz, seq, DKV_A), F32, _tok_spec(tm, DKV_A)),
                 ((bsz, seq, DKV_A), F32, _tok_spec(tm, DKV_A))]
    weights = [w["wm"], w["bm"], w["wg"], w["bg"], w["wqa"], w["bqa"], w["wkk"], w["bkk"], wv, bv]
    return pl.pallas_call(
        functools.partial(_inproj_body, prompt),
        out_shape=[jax.ShapeDtypeStruct(s, dt) for s, dt, _ in outs],
        grid=(bsz, seq // tm),
        in_specs=[_tok_spec(tm, D_MODEL), _mod_spec(mrows, tm), _mod_spec(mrows, tm)]
                 + [_const_spec(a.shape) for a in weights],
        out_specs=[spec for _, _, spec in outs],
        compiler_params=_params(2),
        name="inproj",
    )(x, sh, sc, *weights)


def _scan_rows(x, op, rows):
    row = lax.broadcasted_iota(jnp.int32, x.shape, 0)
    d = 1
    while d < rows:
        shifted = pltpu.roll(x, d, axis=0)
        x = jnp.where(row >= d, op(x, shifted), x)
        d *= 2
    return x


def _mlstm_body(t_in, t, nb, qk_ref, v_ref, g_ref, c0_ref, n0_ref, m0_ref, nw_ref,
                h_ref, c_ref, n_ref, m_ref, *pads):
    @pl.when(pl.program_id(1) == 0)
    def _():
        c_ref[...] = c0_ref[...]
        n_ref[...] = n0_ref[...]
        m_ref[...] = m0_ref[...]

    if t_in != t:
        @pl.when(_first_step())
        def _():
            for p in pads:
                p[...] = jnp.zeros(p.shape, p.dtype)
        for src, dst in zip((qk_ref, v_ref, g_ref), pads):
            dst[:, pl.ds(0, t_in), :] = src[...].astype(dst.dtype)
        qk_src, v_src, g_src = pads
    else:
        qk_src, v_src, g_src = qk_ref, v_ref, g_ref

    row = lax.broadcasted_iota(jnp.int32, (t, LANES), 0)
    lane1 = lax.broadcasted_iota(jnp.int32, (1, LANES), 1)
    r2 = lax.broadcasted_iota(jnp.int32, (t, t), 0)
    c2 = lax.broadcasted_iota(jnp.int32, (t, t), 1)
    causal = c2 <= r2

    seqs = []
    for s in range(nb):
        g = g_src[s]
        lf = jax.nn.log_sigmoid(g)
        ig = g
        if t_in != t:
            lf = jnp.where(row < t_in, lf, 0.0)
            ig = jnp.where(row < t_in, ig, NEG)
        b = pltpu.roll(_scan_rows(lf, jnp.add, t), LANES - NH_M, axis=1)
        gg = ig - b
        gmx = _scan_rows(gg, jnp.maximum, t)
        m_prev = m_ref[s]
        a = b + m_prev
        mt = jnp.maximum(a, b + gmx)
        b_last, gmx_last = b[t - 1:t, :], gmx[t - 1:t, :]
        m_new = jnp.maximum(b_last + m_prev, b_last + gmx_last)
        seqs.append(dict(
            a_in=jnp.exp(a - mt), eb=b - mt, lowb=jnp.exp(-mt), gt=gg.T,
            ws=jnp.exp(b_last + gg - m_new), decay=jnp.exp(b_last + m_prev - m_new)))
        m_ref[s] = jnp.where(lane1 < NH_M, m_new, 0.0)

    units = [(s, hd) for s in range(nb) for hd in range(NH_M)]
    nu = len(units)
    col = lambda name, s, hd: seqs[s][name][:, hd:hd + 1]
    q = [qk_src[s, :, hd * DHK_M:(hd + 1) * DHK_M].astype(BF16) for s, hd in units]
    k = [qk_src[s, :, DQK_M + hd * DHK_M:DQK_M + (hd + 1) * DHK_M].astype(BF16) for s, hd in units]
    v = [v_src[s, :, hd * DHV_M:(hd + 1) * DHV_M].astype(BF16) for s, hd in units]
    c_old = [c_ref[s, hd] for s, hd in units]
    n_old = [n_ref[s, hd:hd + 1, :] for s, hd in units]

    sqk = [_dot_nt(q[u], k[u]) for u in range(nu)]
    dexp = [jnp.exp(jnp.where(causal, col("eb", s, hd) + seqs[s]["gt"][hd:hd + 1, :], NEG))
            for s, hd in units]
    smat = [sqk[u] * dexp[u] for u in range(nu)]
    intra = [_dot(smat[u].astype(BF16), v[u]) for u in range(nu)]
    inter = [_dot_nt(q[u], c_old[u].astype(BF16)) for u in range(nu)]
    qn = [jnp.sum(q[u].astype(F32) * n_old[u].astype(BF16).astype(F32), axis=-1, keepdims=True)
          for u in range(nu)]
    den = [jnp.sum(smat[u], axis=-1, keepdims=True) + col("a_in", s, hd) * qn[u]
           for u, (s, hd) in enumerate(units)]
    hh = [(intra[u] + col("a_in", s, hd) * inter[u])
          / jnp.maximum(jnp.abs(den[u]), col("lowb", s, hd)) for u, (s, hd) in enumerate(units)]
    hn = [_ln(hh[u]) * nw_ref[:, hd * DHV_M:(hd + 1) * DHV_M] for u, (s, hd) in enumerate(units)]
    for u, (s, hd) in enumerate(units):
        h_ref[s, :, hd * DHV_M:(hd + 1) * DHV_M] = hn[u] if t_in == t else hn[u][:t_in]

    kw = [k[u].astype(F32) * col("ws", s, hd) for u, (s, hd) in enumerate(units)]
    upd = [_dot_tn(v[u], kw[u].astype(BF16)) for u in range(nu)]
    for u, (s, hd) in enumerate(units):
        dec = seqs[s]["decay"][:, hd:hd + 1]
        c_ref[s, hd] = dec * c_old[u] + upd[u]
        n_ref[s, hd:hd + 1, :] = dec * n_old[u] + jnp.sum(kw[u], axis=0, keepdims=True)


def _mlstm(qk, v, g, c0, n0, m0, norm_w, t_in, nb):
    nseq, seq, _ = qk.shape
    t = CHUNK if t_in == CHUNK else SUBLANES
    blk = lambda n: pl.BlockSpec((nb, t_in, n), lambda o, c: (o, c, 0))
    st_c = pl.BlockSpec((nb, NH_M, DHV_M, DHK_M), lambda o, c: (o, 0, 0, 0))
    st_n = pl.BlockSpec((nb, NH_M, DHK_M), lambda o, c: (o, 0, 0))
    st_m = pl.BlockSpec((nb, 1, LANES), lambda o, c: (o, 0, 0))
    pads = []
    if t_in != t:
        pads = [pltpu.VMEM((nb, t, 2 * DQK_M), F32), pltpu.VMEM((nb, t, DV_M), F32),
                pltpu.VMEM((nb, t, LANES), F32)]
    return pl.pallas_call(
        functools.partial(_mlstm_body, t_in, t, nb),
        out_shape=[jax.ShapeDtypeStruct((nseq, seq, DV_M), F32),
                   jax.ShapeDtypeStruct((nseq, NH_M, DHV_M, DHK_M), F32),
                   jax.ShapeDtypeStruct((nseq, NH_M, DHK_M), F32),
                   jax.ShapeDtypeStruct((nseq, 1, LANES), F32)],
        grid=(nseq // nb, seq // t_in),
        in_specs=[blk(2 * DQK_M), blk(DV_M), blk(LANES), st_c, st_n, st_m,
                  pl.BlockSpec((1, DV_M), lambda o, c: (0, 0))],
        out_specs=[blk(DV_M), st_c, st_n, st_m],
        scratch_shapes=pads,
        compiler_params=_params(2),
        name="mlstm",
    )(qk, v, g, c0, n0, m0, norm_w)


def _swa_prompt_body(prm_ref, q_ref, kc_ref, kp_ref, vc_ref, vp_ref, o_ref, tbl):
    row = lax.broadcasted_iota(jnp.int32, (WINDOW, WINDOW), 0)
    col = lax.broadcasted_iota(jnp.int32, (WINDOW, WINDOW), 1)
    tri = row <= col
    lo = col < HD_A

    @pl.when(_first_step())
    def _():
        dist = jnp.where(tri, col - row, col - row + WINDOW).astype(F32)
        for hd in range(NH_A):
            bias = prm_ref[0, hd] * dist
            tbl[0, hd] = bias
            tbl[1, hd] = bias + jnp.where(tri, 0.0, -NEG)

    which = jnp.where(pl.program_id(1) == 0, 1, 0)
    zb = jnp.zeros((WINDOW, LANES), BF16)

    def placed(k_ref, kv):
        c, par = kv // 2, kv % 2
        own = k_ref[:, c * LANES:(c + 1) * LANES]
        swp = k_ref[:, DKV_A + c * LANES:DKV_A + (c + 1) * LANES]
        if par == 0:
            return jnp.where(lo, own, zb), jnp.where(lo, zb, swp)
        return jnp.where(lo, swp, zb), jnp.where(lo, zb, own)

    st = []
    for kv in range(NKV_A):
        lhs = jnp.concatenate([*placed(kc_ref, kv), *placed(kp_ref, kv)], axis=0)
        qg = jnp.concatenate([q_ref[:, (2 * kv) * LANES:(2 * kv + 1) * LANES],
                              q_ref[:, (2 * kv + 1) * LANES:(2 * kv + 2) * LANES]], axis=0)
        st.append(_dot_nt(lhs, qg))
    tiles = []
    for hd in range(NH_A):
        kv, a, par = hd // GROUP_A, (hd % GROUP_A) // 2, hd % 2
        cols = slice(a * WINDOW, (a + 1) * WINDOW)
        tiles.append(jnp.where(tri, st[kv][par * WINDOW:(par + 1) * WINDOW, cols],
                               st[kv][(2 + par) * WINDOW:(3 + par) * WINDOW, cols]))
    sc = jnp.concatenate(tiles, axis=0).reshape(NH_A, WINDOW, WINDOW) - tbl[which]
    sink = jnp.concatenate([jnp.full((1, 1, WINDOW), prm_ref[1, hd], F32) for hd in range(NH_A)], axis=0)
    mx = jnp.maximum(jnp.max(sc, axis=1, keepdims=True), sink)
    p = jnp.exp(sc - mx)
    den = jnp.sum(p, axis=1, keepdims=True) + jnp.exp(sink - mx)
    pn = p * (1.0 / den)

    zero = jnp.zeros((WINDOW, WINDOW), F32)
    z64 = jnp.zeros((HD_A, WINDOW), BF16)
    outs = []
    for kv in range(NKV_A):
        cols = []
        for a in range(2):
            pe, po = pn[kv * GROUP_A + 2 * a], pn[kv * GROUP_A + 2 * a + 1]
            cols.append(jnp.concatenate([jnp.where(tri, pe, zero), jnp.where(tri, po, zero),
                                         jnp.where(tri, zero, pe), jnp.where(tri, zero, po)],
                                        axis=0).astype(BF16))
        pt = jnp.concatenate(cols, axis=1)
        vc = vc_ref[kv * HD_A:(kv + 1) * HD_A, :].astype(BF16)
        vp = vp_ref[kv * HD_A:(kv + 1) * HD_A, :].astype(BF16)
        vt = jnp.concatenate([jnp.concatenate([vc, z64], axis=0), jnp.concatenate([z64, vc], axis=0),
                              jnp.concatenate([vp, z64], axis=0), jnp.concatenate([z64, vp], axis=0)],
                             axis=1)
        ot = _dot(vt, pt)
        outs += [ot[:, :WINDOW].T, ot[:, WINDOW:].T]
    o_ref[...] = jnp.concatenate(outs, axis=1).astype(o_ref.dtype)


def _swa_prompt(prm, q, kk, vat):
    bsz, seq, _ = q.shape
    prev = lambda i: jnp.maximum(i - 1, 0)
    return pl.pallas_call(
        _swa_prompt_body,
        out_shape=jax.ShapeDtypeStruct((bsz, seq, DQ_A), BF16),
        grid=(bsz, seq // WINDOW),
        in_specs=[pl.BlockSpec(memory_space=pltpu.SMEM),
                  pl.BlockSpec((None, WINDOW, DQ_A), lambda b, i: (b, i, 0)),
                  pl.BlockSpec((None, WINDOW, 2 * DKV_A), lambda b, i: (b, i, 0)),
                  pl.BlockSpec((None, WINDOW, 2 * DKV_A), lambda b, i: (b, prev(i), 0)),
                  pl.BlockSpec((None, DKV_A, WINDOW), lambda b, i: (b, 0, i)),
                  pl.BlockSpec((None, DKV_A, WINDOW), lambda b, i: (b, 0, prev(i)))],
        out_specs=pl.BlockSpec((None, WINDOW, DQ_A), lambda b, i: (b, i, 0)),
        scratch_shapes=[pltpu.VMEM((2, NH_A, WINDOW, WINDOW), F32)],
        compiler_params=_params(2),
        name="swa_prompt",
    )(prm, q, kk, kk, vat, vat)


def _half_mask(shape, half):
    lane = lax.broadcasted_iota(jnp.int32, shape, 1)
    return lane < HD_A if half == 0 else lane >= HD_A


def _swa_sample_body(t_in, nb, prm_ref, q_ref, kn_ref, vn_ref, kct_ref, vct_ref,
                     o_ref, kco_ref, vco_ref, tbl, q8, kn_pad, vn_pad):
    tq = SUBLANES
    rows = NH_A * tq
    row = lax.broadcasted_iota(jnp.int32, (rows, WINDOW), 0)
    col = lax.broadcasted_iota(jnp.int32, (rows, WINDOW), 1)
    tri = col <= (row & (tq - 1))

    @pl.when(_first_step())
    def _():
        r8 = lax.broadcasted_iota(jnp.int32, (tq, WINDOW), 0)
        c8 = lax.broadcasted_iota(jnp.int32, (tq, WINDOW), 1)
        dist = jnp.where(c8 <= r8, r8 - c8, r8 - c8 + WINDOW).astype(F32)
        for hd in range(NH_A):
            tbl[pl.ds(hd * tq, tq), :] = prm_ref[0, hd] * dist
        for p in (q8, kn_pad, vn_pad):
            p[...] = jnp.zeros(p.shape, p.dtype)

    q8[:, pl.ds(0, t_in), :] = q_ref[...]
    kn_pad[:, pl.ds(0, t_in), :] = kn_ref[...]
    vn_pad[:, pl.ds(0, t_in), :] = vn_ref[...]
    sink_col = jnp.concatenate([jnp.full((tq, 1), prm_ref[1, hd], F32) for hd in range(NH_A)], axis=0)
    z8 = jnp.zeros((tq, LANES), F32)
    n_chunk = NKV_A // 2
    heads_per_chunk = NH_A // n_chunk

    def place(piece, src_half, dst_half):
        if src_half != dst_half:
            piece = pltpu.roll(piece, HD_A, axis=1)
        return jnp.where(_half_mask(piece.shape, dst_half), piece, z8)

    qexp = []
    for s in range(nb):
        per_c = []
        for c in range(n_chunk):
            pieces = []
            for hl in range(heads_per_chunk):
                hd = c * heads_per_chunk + hl
                pieces.append(place(q8[s, :, (hd // 2) * LANES:(hd // 2 + 1) * LANES],
                                    hd % 2, hl // GROUP_A))
            per_c.append(jnp.concatenate(pieces, axis=0).astype(BF16))
        qexp.append(per_c)
    csl = lambda c: slice(c * LANES, (c + 1) * LANES)
    s_prev = [[_dot(qexp[s][c], kct_ref[s, csl(c), :].astype(BF16)) for c in range(n_chunk)]
              for s in range(nb)]
    s_cur = [[_dot_nt(qexp[s][c], kn_pad[s, :, csl(c)].astype(BF16)) for c in range(n_chunk)]
             for s in range(nb)]
    sc = [jnp.where(tri, jnp.concatenate(s_cur[s], axis=0), jnp.concatenate(s_prev[s], axis=0)) - tbl[...]
          for s in range(nb)]
    mx = [jnp.maximum(jnp.max(sc[s], axis=-1, keepdims=True), sink_col) for s in range(nb)]
    p = [jnp.exp(sc[s] - mx[s]) for s in range(nb)]
    den = [jnp.sum(p[s], axis=-1, keepdims=True) + jnp.exp(sink_col - mx[s]) for s in range(nb)]
    pn = [p[s] * (1.0 / den[s]) for s in range(nb)]
    zero = jnp.zeros((rows, WINDOW), F32)
    pc = [jnp.where(tri, pn[s], zero).astype(BF16) for s in range(nb)]
    pp = [jnp.where(tri, zero, pn[s]).astype(BF16) for s in range(nb)]
    half_rows = heads_per_chunk * tq
    oc = [[_dot_nt(pp[s][c * half_rows:(c + 1) * half_rows], vct_ref[s, csl(c), :].astype(BF16))
           + _dot(pc[s][c * half_rows:(c + 1) * half_rows], vn_pad[s, :, csl(c)].astype(BF16))
           for c in range(n_chunk)] for s in range(nb)]
    for s in range(nb):
        chunks = []
        for pch in range(NH_A // 2):
            acc = None
            for hd in (2 * pch, 2 * pch + 1):
                c, hl = hd // heads_per_chunk, hd % heads_per_chunk
                piece = place(oc[s][c][hl * tq:(hl + 1) * tq, :], hl // GROUP_A, hd % 2)
                acc = piece if acc is None else acc + piece
            chunks.append(acc)
        o_ref[s] = jnp.concatenate(chunks, axis=1)[:t_in]

    lane = lax.broadcasted_iota(jnp.int32, (DKV_A, WINDOW), 1)
    for new_pad, old_ref, out_ref in ((kn_pad, kct_ref, kco_ref), (vn_pad, vct_ref, vco_ref)):
        for s in range(nb):
            merged = jnp.where(lane < t_in, new_pad[s].T, old_ref[s])
            out_ref[s] = pltpu.roll(merged, WINDOW - t_in, axis=1)


def _swa_sample(prm, q, kn, vn, kct, vct, nb):
    nseq, t_in, _ = q.shape
    assert t_in <= SUBLANES
    cur = lambda n: pl.BlockSpec((nb, t_in, n), lambda o, i: (o, 0, 0))
    win = pl.BlockSpec((nb, DKV_A, WINDOW), lambda o, i: (o, 0, 0))
    return pl.pallas_call(
        functools.partial(_swa_sample_body, t_in, nb),
        out_shape=[jax.ShapeDtypeStruct((nseq, t_in, DQ_A), F32),
                   jax.ShapeDtypeStruct((nseq, DKV_A, WINDOW), F32),
                   jax.ShapeDtypeStruct((nseq, DKV_A, WINDOW), F32)],
        grid=(nseq // nb, 1),
        in_specs=[pl.BlockSpec(memory_space=pltpu.SMEM), cur(DQ_A), cur(DKV_A), cur(DKV_A), win, win],
        out_specs=[cur(DQ_A), win, win],
        scratch_shapes=[pltpu.VMEM((NH_A * SUBLANES, WINDOW), F32),
                        pltpu.VMEM((nb, SUBLANES, DQ_A), F32),
                        pltpu.VMEM((nb, WINDOW, DKV_A), F32),
                        pltpu.VMEM((nb, WINDOW, DKV_A), F32)],
        compiler_params=_params(2),
        name="swa_sample",
    )(prm, q, kn, vn, kct, vct)


def _merge_body(alpha, x_ref, sh_ref, sc_ref, g1_ref, hm_ref, ha_ref, wog_ref, bog_ref,
                wgg_ref, bgg_ref, wbm_ref, wba_ref, wo_ref, lg_ref, lb_ref, o_ref):
    x = x_ref[...]
    h = (_ln(x) * (1.0 + sc_ref[...]) + sh_ref[...]).astype(BF16)

    def branch_gate(j):
        cols = slice(j * D_MODEL, (j + 1) * D_MODEL)
        return jax.nn.sigmoid(_dot(h, wgg_ref[:, cols]) + bgg_ref[:, cols])

    hm = (hm_ref[...] * jax.nn.sigmoid(_dot(h, wog_ref[...]) + bog_ref[...])).astype(BF16)
    merged = branch_gate(0) * _dot(hm, wbm_ref[...])
    merged = merged + branch_gate(1) * _dot(ha_ref[...].astype(BF16), wba_ref[...])
    mo = _dot(merged.astype(BF16), wo_ref[...])
    o_ref[...] = _ln(alpha * x + g1_ref[...] * mo) * lg_ref[...] + lb_ref[...]


def _merge(x, sh, sc, g1, hm, ha, w, tm, alpha):
    bsz, seq, _ = x.shape
    mrows = sh.shape[1]
    weights = [w["wog"], w["bog"], w["wgg"], w["bgg"], w["wbm"], w["wba"], w["wo"],
               w["ln1_g"], w["ln1_b"]]
    return pl.pallas_call(
        functools.partial(_merge_body, alpha),
        out_shape=jax.ShapeDtypeStruct((bsz, seq, D_MODEL), F32),
        grid=(bsz, seq // tm),
        in_specs=[_tok_spec(tm, D_MODEL), _mod_spec(mrows, tm), _mod_spec(mrows, tm),
                  _mod_spec(mrows, tm), _tok_spec(tm, DV_M), _tok_spec(tm, DQ_A)]
                 + [_const_spec(a.shape) for a in weights],
        out_specs=_tok_spec(tm, D_MODEL),
        compiler_params=_params(2),
        name="merge",
    )(x, sh, sc, g1, hm, ha, *weights)


def _ffn_body(alpha, tm, stride, halo, x_ref, sh_ref, sc_ref, g2_ref, cb0_ref, wup_ref, bup_ref,
              cw_ref, cbias_ref, wdn_ref, bdn_ref, lg_ref, lb_ref, o_ref, cs_ref, ubuf, act):
    @pl.when(pl.program_id(1) == 0)
    def _():
        cs_ref[...] = cb0_ref[...]

    x = x_ref[...]
    h = (_ln(x) * (1.0 + sc_ref[...]) + sh_ref[...]).astype(BF16)
    ys = []
    for half in range(2):
        cols = slice(half * D_FF, (half + 1) * D_FF)
        u = _dot(h, wup_ref[:, cols]) + bup_ref[:, cols]
        ubuf[pl.ds(0, halo), :] = cs_ref[:, cols]
        ubuf[pl.ds(halo, tm), :] = u
        cs_ref[:, cols] = ubuf[pl.ds(tm, halo), :]
        y = cbias_ref[:, cols] + u * cw_ref[CONV_W - 1:CONV_W, cols]
        for j in range(CONV_W - 1):
            y = y + ubuf[pl.ds(halo - (CONV_W - 1 - j) * stride, tm), :] * cw_ref[j:j + 1, cols]
        ys.append(y)
    act[...] = (jax.nn.gelu(ys[0]) * ys[1]).astype(BF16)
    f = _dot(act[...], wdn_ref[...]) + bdn_ref[...]
    o_ref[...] = _ln(alpha * x + g2_ref[...] * f) * lg_ref[...] + lb_ref[...]


def _ffn(x, sh, sc, g2, cb0, w, tm, stride, alpha):
    bsz, seq, _ = x.shape
    mrows = sh.shape[1]
    halo = cb0.shape[1]
    cs = pl.BlockSpec((None, halo, 2 * D_FF), lambda b, i: (b, 0, 0))
    weights = [w["wup"], w["bup"], w["cw"], w["cbias"], w["wdn"], w["bdn"], w["ln2_g"], w["ln2_b"]]
    return pl.pallas_call(
        functools.partial(_ffn_body, alpha, tm, stride, halo),
        out_shape=[jax.ShapeDtypeStruct((bsz, seq, D_MODEL), F32),
                   jax.ShapeDtypeStruct((bsz, halo, 2 * D_FF), F32)],
        grid=(bsz, seq // tm),
        in_specs=[_tok_spec(tm, D_MODEL), _mod_spec(mrows, tm), _mod_spec(mrows, tm),
                  _mod_spec(mrows, tm), cs] + [_const_spec(a.shape) for a in weights],
        out_specs=[_tok_spec(tm, D_MODEL), cs],
        scratch_shapes=[pltpu.VMEM((halo + tm, D_FF), F32), pltpu.VMEM((tm, D_FF), BF16)],
        compiler_params=_params(2),
        name="ffn",
    )(x, sh, sc, g2, cb0, *weights)


def _prep_weights(w_in, b_in, mlstm_norm_w, w_branch_m, w_branch_a, w_out, ln1_g, ln1_b,
                  w_up, b_up, conv_w, conv_b, w_down, b_down, ln2_g, ln2_b):
    o_gate = 2 * DQK_M + DV_M
    o_og = o_gate + 2 * NH_M
    o_qa = o_og + DV_M
    o_ka = o_qa + DQ_A
    o_va = o_ka + DKV_A
    o_gm = o_va + DKV_A
    row = lambda a: a.reshape(1, -1)
    swap = lambda a: a.reshape(a.shape[:-1] + (DKV_A // LANES, 2, HD_A))[..., ::-1, :].reshape(a.shape)
    gate_pad = LANES - 2 * NH_M
    w_k, b_k = w_in[:, o_ka:o_va], b_in[o_ka:o_va]
    w_v, b_v = w_in[:, o_va:o_gm], b_in[o_va:o_gm]
    return dict(
        wm=w_in[:, :o_gate].astype(BF16), bm=row(b_in[:o_gate]),
        wg=jnp.pad(w_in[:, o_gate:o_og], ((0, 0), (0, gate_pad))).astype(BF16),
        bg=row(jnp.pad(b_in[o_gate:o_og], (0, gate_pad))),
        wqa=w_in[:, o_qa:o_ka].astype(BF16), bqa=row(b_in[o_qa:o_ka]),
        wkk=jnp.concatenate([w_k, swap(w_k)], axis=1).astype(BF16),
        bkk=row(jnp.concatenate([b_k, swap(b_k)])),
        wv=w_v.astype(BF16), bv_row=row(b_v), wv_t=w_v.T.astype(BF16), bv_col=b_v.reshape(-1, 1),
        wog=w_in[:, o_og:o_qa].astype(BF16), bog=row(b_in[o_og:o_qa]),
        wgg=w_in[:, o_gm:].astype(BF16), bgg=row(b_in[o_gm:]),
        norm_w=row(mlstm_norm_w), wbm=w_branch_m.astype(BF16), wba=w_branch_a.astype(BF16),
        wo=w_out.astype(BF16), ln1_g=row(ln1_g), ln1_b=row(ln1_b),
        wup=w_up.astype(BF16), bup=row(b_up), cw=conv_w, cbias=row(conv_b),
        wdn=w_down.astype(BF16), bdn=row(b_down), ln2_g=row(ln2_g), ln2_b=row(ln2_b))


def kernel(x_prompt, x_sample, c_prompt, c_sample, state_mlstm_C, state_mlstm_n, state_mlstm_m,
           cache_k_win, cache_v_win, state_ffn_conv, w_ada, b_ada, w_in, b_in, mlstm_norm_w,
           attn_sinks, w_branch_m, w_branch_a, w_out, ln1_g, ln1_b, w_up, b_up, conv_w, conv_b,
           w_down, b_down, ln2_g, ln2_b):
    depth = w_in.shape[0]
    bp, lp, _ = x_prompt.shape
    bs, ls, _ = x_sample.shape
    assert cache_k_win.shape[2] == WINDOW
    alpha = (2 * depth) ** 0.25
    dt = x_prompt.dtype
    slopes = jnp.exp2(-8.0 * jnp.arange(1, NH_A + 1, dtype=F32) / NH_A)
    tm_p = 512
    ns = bs * ls

    yp = x_prompt
    ys = x_sample.reshape(1, ns, D_MODEL)
    new_p, new_s = [], []
    n_c = bp + bs
    c_rows = -(-n_c // SUBLANES) * SUBLANES
    c_all = jnp.concatenate([c_sample, c_prompt, jnp.zeros((c_rows - n_c, D_MODEL), dt)], axis=0)
    for l in range(depth):
        w = _prep_weights(w_in[l], b_in[l], mlstm_norm_w[l], w_branch_m[l], w_branch_a[l], w_out[l],
                          ln1_g[l], ln1_b[l], w_up[l], b_up[l], conv_w[l], conv_b[l], w_down[l],
                          b_down[l], ln2_g[l], ln2_b[l])
        prm = jnp.stack([slopes, attn_sinks[l].astype(F32)])
        mod = _ada(c_all, w_ada[l], b_ada[l])
        mod_s = mod[:bs].reshape(bs, N_MOD, D_MODEL)
        mod_p = mod[bs:bs + bp].reshape(bp, N_MOD, D_MODEL)
        mp_ = [mod_p[:, j:j + 1] for j in range(N_MOD)]
        ms_seq = [jnp.repeat(mod_s[:, j], ls, axis=0)[None] for j in range(3)]
        ms_pos = [jnp.tile(mod_s[:, j], (ls, 1))[None] for j in range(3, N_MOD)]

        qk, v, g, qa, kk, ka, vat = _inproj(yp, mp_[0], mp_[1], w, tm_p, True)
        hm, cp, np_, mp = _mlstm(qk, v, g, jnp.zeros((bp, NH_M, DHV_M, DHK_M), dt),
                                 jnp.zeros((bp, NH_M, DHK_M), dt), jnp.zeros((bp, 1, LANES), dt),
                                 w["norm_w"], CHUNK, bp)
        ha = _swa_prompt(prm, qa, kk, vat)
        x1p = _merge(yp, mp_[0], mp_[1], mp_[2], hm, ha, w, tm_p, alpha)
        halo_p = SUBLANES
        yp, csp = _ffn(x1p, mp_[3], mp_[4], mp_[5], jnp.zeros((bp, halo_p, 2 * D_FF), dt),
                       w, tm_p, 1, alpha)
        p_k = ka[:, lp - WINDOW:].reshape(bp, WINDOW, NKV_A, HD_A)
        p_v = vat[:, :, lp - WINDOW:].reshape(bp, NKV_A, HD_A, WINDOW).transpose(0, 3, 1, 2)
        new_p.append((cp, np_, mp[:, 0, :NH_M], p_k, p_v, csp[:, halo_p - (CONV_W - 1):]))

        qk, v, g, qa, kn, vn = _inproj(ys, ms_seq[0], ms_seq[1], w, ns, False)
        per_seq = lambda a: a.reshape(bs, ls, a.shape[-1])
        m0 = jnp.pad(state_mlstm_m[l], ((0, 0), (0, LANES - NH_M)))[:, None, :]
        hm, cs_, ns_, ms = _mlstm(per_seq(qk), per_seq(v), per_seq(g), state_mlstm_C[l],
                                  state_mlstm_n[l], m0, w["norm_w"], ls, SEQS_PER_STEP)
        to_t = lambda a: a.transpose(0, 2, 3, 1).reshape(bs, DKV_A, WINDOW)
        from_t = lambda a: a.reshape(bs, NKV_A, HD_A, WINDOW).transpose(0, 3, 1, 2)
        ha, kct, vct = _swa_sample(prm, per_seq(qa), per_seq(kn), per_seq(vn),
                                   to_t(cache_k_win[l]), to_t(cache_v_win[l]), SEQS_PER_STEP)
        x1s = _merge(ys, ms_seq[0], ms_seq[1], ms_seq[2], hm.reshape(1, ns, DV_M),
                     ha.reshape(1, ns, DQ_A), w, ns, alpha)
        to_pos = lambda a: a.reshape(bs, -1, a.shape[-1]).transpose(1, 0, 2).reshape(1, -1, a.shape[-1])
        y_pos, css = _ffn(to_pos(x1s[0]), ms_pos[0], ms_pos[1], ms_pos[2], to_pos(state_ffn_conv[l]),
                          w, ns, bs, alpha)
        ys = y_pos.reshape(ls, bs, D_MODEL).transpose(1, 0, 2).reshape(1, ns, D_MODEL)
        conv_s = css.reshape(CONV_W - 1, bs, 2 * D_FF).transpose(1, 0, 2)
        new_s.append((cs_, ns_, ms[:, 0, :NH_M], from_t(kct), from_t(vct), conv_s))

    p_state = [jnp.stack(a) for a in zip(*new_p)]
    s_state = [jnp.stack(a) for a in zip(*new_s)]
    return (yp, ys.reshape(bs, ls, D_MODEL), *p_state, *s_state)
```

```python
import functools

import jax
import jax.numpy as jnp
from jax import lax
from jax.experimental import pallas as pl
from jax.experimental.pallas import tpu as pltpu

F32 = jnp.float32
BF16 = jnp.bfloat16

D_MODEL = 1024
NH_M, DHK_M, DHV_M = 4, 128, 256
DQK_M, DV_M = NH_M * DHK_M, NH_M * DHV_M
NH_A, NKV_A, HD_A = 16, 4, 64
GROUP_A = NH_A // NKV_A
WINDOW = 128
DQ_A, DKV_A = NH_A * HD_A, NKV_A * HD_A
D_FF = 2816
CONV_W = 3
N_MOD = 6
LN_EPS = 1e-5
CHUNK = 128
NEG = -1e30
LANES = 128
SUBLANES = 8
VMEM_LIMIT = 56 * 1024 * 1024
SEQS_PER_STEP = 4


def _ln(x):
    mu = jnp.mean(x, axis=-1, keepdims=True)
    xc = x - mu
    var = jnp.mean(xc * xc, axis=-1, keepdims=True)
    return xc * lax.rsqrt(var + LN_EPS)


def _dot(a, b):
    return jnp.dot(a, b, preferred_element_type=F32)


def _dot_nt(a, b):
    return lax.dot_general(a, b, (((1,), (1,)), ((), ())), preferred_element_type=F32)


def _dot_tn(a, b):
    return lax.dot_general(a, b, (((0,), (0,)), ((), ())), preferred_element_type=F32)


def _const_spec(shape):
    nd = len(shape)
    return pl.BlockSpec(shape, lambda *_: (0,) * nd, pipeline_mode=pl.Buffered(1))


def _params(n_grid):
    return pltpu.CompilerParams(dimension_semantics=("arbitrary",) * n_grid,
                                vmem_limit_bytes=VMEM_LIMIT)


def _first_step():
    return (pl.program_id(0) == 0) & (pl.program_id(1) == 0)


def _ada_body(c_ref, w_ref, b_ref, o_ref):
    c = c_ref[...]
    s = (c * jax.nn.sigmoid(c)).astype(BF16)
    o_ref[...] = _dot(s, w_ref[...].astype(BF16)) + b_ref[...]


def _ada(c, w_ada, b_ada):
    rows = c.shape[0]
    n_out = w_ada.shape[1]
    bn = 512
    return pl.pallas_call(
        _ada_body,
        out_shape=jax.ShapeDtypeStruct((rows, n_out), F32),
        grid=(n_out // bn,),
        in_specs=[pl.BlockSpec((rows, D_MODEL), lambda j: (0, 0)),
                  pl.BlockSpec((D_MODEL, bn), lambda j: (0, j)),
                  pl.BlockSpec((1, bn), lambda j: (0, j))],
        out_specs=pl.BlockSpec((rows, bn), lambda j: (0, j)),
        compiler_params=_params(1),
        name="ada",
    )(c, w_ada, b_ada.reshape(1, n_out))


def _inproj_body(prompt, x_ref, sh_ref, sc_ref, wqk_ref, bqk_ref, wvm_ref, bvm_ref, wg_ref, bg_ref,
                 wqa_ref, bqa_ref, wkk_ref, bkk_ref, wv_ref, bv_ref, qk_ref, v_ref, g_ref, qa_ref, *rest):
    h = (_ln(x_ref[...]) * (1.0 + sc_ref[...]) + sh_ref[...]).astype(BF16)

    def proj(w_ref, b_ref, lo, n):
        return _dot(h, w_ref[:, lo:lo + n]) + b_ref[:, lo:lo + n]

    qk_ref[:, :DQK_M] = (proj(wqk_ref, bqk_ref, 0, DQK_M) * DHK_M ** -0.5).astype(qk_ref.dtype)
    qk_ref[:, DQK_M:] = proj(wqk_ref, bqk_ref, DQK_M, DQK_M).astype(qk_ref.dtype)
    if prompt:
        v_ref[...] = (_dot_nt(wvm_ref[...], h) + bvm_ref[...]).astype(v_ref.dtype)
    else:
        v_ref[...] = proj(wvm_ref, bvm_ref, 0, DV_M).astype(v_ref.dtype)
    g_ref[...] = proj(wg_ref, bg_ref, 0, LANES)
    qa_ref[...] = (proj(wqa_ref, bqa_ref, 0, DQ_A) * HD_A ** -0.5).astype(qa_ref.dtype)
    kk = proj(wkk_ref, bkk_ref, 0, 2 * DKV_A)
    if prompt:
        kk_ref, ka_ref, vat_ref = rest
        kk_ref[...] = kk.astype(BF16)
        ka_ref[...] = kk[:, :DKV_A]
        vat_ref[...] = _dot_nt(wv_ref[...], h) + bv_ref[...]
    else:
        ka_ref, va_ref = rest
        ka_ref[...] = kk[:, :DKV_A]
        va_ref[...] = _dot(h, wv_ref[...]) + bv_ref[...]


def _tok_spec(tm, n):
    return pl.BlockSpec((None, tm, n), lambda b, i: (b, i, 0))


def _mod_spec(rows, tm):
    if rows == 1:
        return pl.BlockSpec((None, 1, D_MODEL), lambda b, i: (b, 0, 0))
    return pl.BlockSpec((None, tm, D_MODEL), lambda b, i: (b, i, 0))


def _inproj(x, sh, sc, w, tm, prompt):
    bsz, seq, _ = x.shape
    mrows = sh.shape[1]
    act = BF16 if prompt else F32
    t_spec = lambda n: pl.BlockSpec((None, n, tm), lambda b, i: (b, 0, i))
    outs = [((bsz, seq, 2 * DQK_M), act, _tok_spec(tm, 2 * DQK_M)),
            ((bsz, DV_M, seq), act, t_spec(DV_M)) if prompt else ((bsz, seq, DV_M), act, _tok_spec(tm, DV_M)),
            ((bsz, seq, LANES), F32, _tok_spec(tm, LANES)),
            ((bsz, seq, DQ_A), act, _tok_spec(tm, DQ_A))]
    if prompt:
        wvm, bvm, wv, bv = w["wvm_t"], w["bvm_col"], w["wv_t"], w["bv_col"]
        outs += [((bsz, seq, 2 * DKV_A), BF16, _tok_spec(tm, 2 * DKV_A)),
                 ((bsz, seq, DKV_A), F32, _tok_spec(tm, DKV_A)),
                 ((bsz, DKV_A, seq), F32, t_spec(DKV_A))]
    else:
        wvm, bvm, wv, bv = w["wvm"], w["bvm_row"], w["wv"], w["bv_row"]
        outs += [((bsz, seq, DKV_A), F32, _tok_spec(tm, DKV_A)),
                 ((bsz, seq, DKV_A), F32, _tok_spec(tm, DKV_A))]
    weights = [w["wqk"], w["bqk"], wvm, bvm, w["wg"], w["bg"], w["wqa"], w["bqa"], w["wkk"], w["bkk"],
               wv, bv]
    return pl.pallas_call(
        functools.partial(_inproj_body, prompt),
        out_shape=[jax.ShapeDtypeStruct(s, dt) for s, dt, _ in outs],
        grid=(bsz, seq // tm),
        in_specs=[_tok_spec(tm, D_MODEL), _mod_spec(mrows, tm), _mod_spec(mrows, tm)]
                 + [_const_spec(a.shape) for a in weights],
        out_specs=[spec for _, _, spec in outs],
        compiler_params=_params(2),
        name="inproj",
    )(x, sh, sc, *weights)


def _scan_rows(x, op, rows):
    row = lax.broadcasted_iota(jnp.int32, x.shape, 0)
    d = 1
    while d < rows:
        shifted = pltpu.roll(x, d, axis=0)
        x = jnp.where(row >= d, op(x, shifted), x)
        d *= 2
    return x


def _mlstm_sample_body(t_in, nb, qk_ref, v_ref, g_ref, c0_ref, n0_ref, m0_ref, nw_ref,
                       h_ref, c_ref, n_ref, m_ref, *pads):
    t = SUBLANES
    c_ref[...] = c0_ref[...]
    n_ref[...] = n0_ref[...]
    m_ref[...] = m0_ref[...]

    @pl.when(_first_step())
    def _():
        for p in pads:
            p[...] = jnp.zeros(p.shape, p.dtype)
    for src, dst in zip((qk_ref, v_ref, g_ref), pads):
        dst[:, pl.ds(0, t_in), :] = src[...]
    qk_src, v_src, g_src = pads

    row = lax.broadcasted_iota(jnp.int32, (t, LANES), 0)
    lane = lax.broadcasted_iota(jnp.int32, (t, LANES), 1)
    lane1 = lax.broadcasted_iota(jnp.int32, (1, LANES), 1)
    r2 = lax.broadcasted_iota(jnp.int32, (t, t), 0)
    c2 = lax.broadcasted_iota(jnp.int32, (t, t), 1)
    causal = c2 <= r2
    pad_gate = jnp.where(lane < NH_M, NEG, -NEG)

    seqs = []
    for s in range(nb):
        ga = _gate_algebra(jnp.where(row < t_in, g_src[s], pad_gate), m_ref[s], t)
        ga["gt"] = ga["gg"].T
        m_ref[s] = jnp.where(lane1 < NH_M, ga["m_new"], 0.0)
        seqs.append(ga)

    units = [(s, hd) for s in range(nb) for hd in range(NH_M)]
    nu = len(units)
    col = lambda name, s, hd: seqs[s][name][:, hd:hd + 1]
    q = [qk_src[s, :, hd * DHK_M:(hd + 1) * DHK_M].astype(BF16) for s, hd in units]
    k = [qk_src[s, :, DQK_M + hd * DHK_M:DQK_M + (hd + 1) * DHK_M].astype(BF16) for s, hd in units]
    v = [v_src[s, :, hd * DHV_M:(hd + 1) * DHV_M].astype(BF16) for s, hd in units]
    c_old = [c_ref[s, hd] for s, hd in units]
    n_old = [n_ref[s, hd:hd + 1, :] for s, hd in units]

    sqk = [_dot_nt(q[u], k[u]) for u in range(nu)]
    dexp = [jnp.exp(jnp.where(causal, col("eb", s, hd) + seqs[s]["gt"][hd:hd + 1, :], NEG))
            for s, hd in units]
    smat = [sqk[u] * dexp[u] for u in range(nu)]
    intra = [_dot(smat[u].astype(BF16), v[u]) for u in range(nu)]
    inter = [_dot_nt(q[u], c_old[u].astype(BF16)) for u in range(nu)]
    qn = [jnp.sum(q[u].astype(F32) * n_old[u].astype(BF16).astype(F32), axis=-1, keepdims=True)
          for u in range(nu)]
    den = [jnp.sum(smat[u], axis=-1, keepdims=True) + col("a_in", s, hd) * qn[u]
           for u, (s, hd) in enumerate(units)]
    hh = [(intra[u] + col("a_in", s, hd) * inter[u])
          / jnp.maximum(jnp.abs(den[u]), col("lowb", s, hd)) for u, (s, hd) in enumerate(units)]
    hn = [_ln(hh[u]) * nw_ref[:, hd * DHV_M:(hd + 1) * DHV_M] for u, (s, hd) in enumerate(units)]
    for u, (s, hd) in enumerate(units):
        h_ref[s, :, hd * DHV_M:(hd + 1) * DHV_M] = hn[u][:t_in]

    kw = [k[u].astype(F32) * col("ws", s, hd) for u, (s, hd) in enumerate(units)]
    upd = [_dot_tn(v[u], kw[u].astype(BF16)) for u in range(nu)]
    for u, (s, hd) in enumerate(units):
        dec = seqs[s]["decay"][:, hd:hd + 1]
        c_ref[s, hd] = dec * c_old[u] + upd[u]
        n_ref[s, hd:hd + 1, :] = dec * n_old[u] + jnp.sum(kw[u], axis=0, keepdims=True)


def _mlstm_sample(qk, v, g, c0, n0, m0, norm_w, nb):
    nseq, t_in, _ = qk.shape
    assert t_in <= SUBLANES
    blk = lambda n: pl.BlockSpec((nb, t_in, n), lambda o, c: (o, 0, 0))
    st_c = pl.BlockSpec((nb, NH_M, DHV_M, DHK_M), lambda o, c: (o, 0, 0, 0))
    st_n = pl.BlockSpec((nb, NH_M, DHK_M), lambda o, c: (o, 0, 0))
    st_m = pl.BlockSpec((nb, 1, LANES), lambda o, c: (o, 0, 0))
    return pl.pallas_call(
        functools.partial(_mlstm_sample_body, t_in, nb),
        out_shape=[jax.ShapeDtypeStruct((nseq, t_in, DV_M), F32),
                   jax.ShapeDtypeStruct((nseq, NH_M, DHV_M, DHK_M), F32),
                   jax.ShapeDtypeStruct((nseq, NH_M, DHK_M), F32),
                   jax.ShapeDtypeStruct((nseq, 1, LANES), F32)],
        grid=(nseq // nb, 1),
        in_specs=[blk(2 * DQK_M), blk(DV_M), blk(LANES), st_c, st_n, st_m,
                  pl.BlockSpec((1, DV_M), lambda o, c: (0, 0))],
        out_specs=[blk(DV_M), st_c, st_n, st_m],
        scratch_shapes=[pltpu.VMEM((nb, SUBLANES, 2 * DQK_M), F32), pltpu.VMEM((nb, SUBLANES, DV_M), F32),
                        pltpu.VMEM((nb, SUBLANES, LANES), F32)],
        compiler_params=_params(2),
        name="mlstm_sample",
    )(qk, v, g, c0, n0, m0, norm_w)


def _gate_algebra(g, m_prev, t):
    b = pltpu.roll(_scan_rows(jax.nn.log_sigmoid(g), jnp.add, t), LANES - NH_M, axis=1)
    gg = g - b
    gmx = _scan_rows(gg, jnp.maximum, t)
    a = b + m_prev
    mt = jnp.maximum(a, b + gmx)
    b_last, gmx_last = b[t - 1:t, :], gmx[t - 1:t, :]
    m_new = jnp.maximum(b_last + m_prev, b_last + gmx_last)
    return dict(gg=gg, eb=b - mt, a_in=jnp.exp(a - mt), lowb=jnp.exp(-mt),
                ws=jnp.exp(b_last + gg - m_new), decay=jnp.exp(b_last + m_prev - m_new), m_new=m_new)


def _mlstm_prompt_body(nb, qk_ref, vt_ref, g_ref, c0_ref, n0_ref, m0_ref, nwc_ref,
                       h_ref, c_ref, n_ref, m_ref, nwb):
    t = CHUNK

    @pl.when(pl.program_id(1) == 0)
    def _():
        c_ref[...] = c0_ref[...]
        n_ref[...] = n0_ref[...]
        m_ref[...] = m0_ref[...]

    @pl.when(_first_step())
    def _():
        for hd in range(NH_M):
            nwb[hd] = jnp.broadcast_to(nwc_ref[hd * DHV_M:(hd + 1) * DHV_M, :], (DHV_M, t))

    lane1 = lax.broadcasted_iota(jnp.int32, (1, LANES), 1)
    r2 = lax.broadcasted_iota(jnp.int32, (t, t), 0)
    c2 = lax.broadcasted_iota(jnp.int32, (t, t), 1)
    causal = r2 <= c2

    seqs = []
    for s in range(nb):
        ga = _gate_algebra(g_ref[s], m_ref[s], t)
        m_ref[s] = jnp.where(lane1 < NH_M, ga["m_new"], 0.0)
        for name in ("eb", "a_in", "lowb", "ws"):
            ga[name + "_t"] = ga[name].T
        seqs.append(ga)

    units = [(s, hd) for s in range(nb) for hd in range(NH_M)]
    nu = len(units)
    rowv = lambda name, s, hd: seqs[s][name + "_t"][hd:hd + 1, :]
    q = [qk_ref[s, :, hd * DHK_M:(hd + 1) * DHK_M] for s, hd in units]
    k = [qk_ref[s, :, DQK_M + hd * DHK_M:DQK_M + (hd + 1) * DHK_M] for s, hd in units]
    vt = [vt_ref[s, hd * DHV_M:(hd + 1) * DHV_M, :] for s, hd in units]
    c_old = [c_ref[s, hd] for s, hd in units]
    n_old = [n_ref[s, hd:hd + 1, :] for s, hd in units]

    skq = [_dot_nt(k[u], q[u]) for u in range(nu)]
    dexp = [jnp.exp(jnp.where(causal, seqs[s]["gg"][:, hd:hd + 1] + rowv("eb", s, hd), NEG))
            for s, hd in units]
    smat = [skq[u] * dexp[u] for u in range(nu)]
    intra = [_dot(vt[u], smat[u].astype(BF16)) for u in range(nu)]
    inter = [_dot_nt(c_old[u].astype(BF16), q[u]) for u in range(nu)]
    qn = [_dot_nt(jnp.broadcast_to(n_old[u], (SUBLANES, DHK_M)).astype(BF16), q[u])[0:1, :]
          for u in range(nu)]
    den = [jnp.sum(smat[u], axis=0, keepdims=True) + rowv("a_in", s, hd) * qn[u]
           for u, (s, hd) in enumerate(units)]
    inv = [1.0 / jnp.maximum(jnp.abs(den[u]), rowv("lowb", s, hd)) for u, (s, hd) in enumerate(units)]
    hh = [(intra[u] + rowv("a_in", s, hd) * inter[u]) * inv[u] for u, (s, hd) in enumerate(units)]
    mu = [jnp.mean(hh[u], axis=0, keepdims=True) for u in range(nu)]
    xc = [hh[u] - mu[u] for u in range(nu)]
    var = [jnp.mean(xc[u] * xc[u], axis=0, keepdims=True) for u in range(nu)]
    hn = [xc[u] * lax.rsqrt(var[u] + LN_EPS) * nwb[hd] for u, (s, hd) in enumerate(units)]
    for u, (s, hd) in enumerate(units):
        h_ref[s, :, hd * DHV_M:(hd + 1) * DHV_M] = hn[u].T

    vw = [(vt[u].astype(F32) * rowv("ws", s, hd)).astype(BF16) for u, (s, hd) in enumerate(units)]
    upd = [_dot(vw[u], k[u]) for u in range(nu)]
    nupd = [_dot(jnp.broadcast_to(rowv("ws", s, hd), (SUBLANES, t)).astype(BF16), k[u])[0:1, :]
            for u, (s, hd) in enumerate(units)]
    for u, (s, hd) in enumerate(units):
        dec = seqs[s]["decay"][:, hd:hd + 1]
        c_ref[s, hd] = dec * c_old[u] + upd[u]
        n_ref[s, hd:hd + 1, :] = dec * n_old[u] + nupd[u]


def _mlstm_prompt(qk, vt, g, c0, n0, m0, norm_w_col):
    nb, seq, _ = qk.shape
    tok = lambda n: pl.BlockSpec((nb, CHUNK, n), lambda o, c: (0, c, 0))
    st_c = pl.BlockSpec((nb, NH_M, DHV_M, DHK_M), lambda o, c: (0, 0, 0, 0))
    st_n = pl.BlockSpec((nb, NH_M, DHK_M), lambda o, c: (0, 0, 0))
    st_m = pl.BlockSpec((nb, 1, LANES), lambda o, c: (0, 0, 0))
    return pl.pallas_call(
        functools.partial(_mlstm_prompt_body, nb),
        out_shape=[jax.ShapeDtypeStruct((nb, seq, DV_M), F32),
                   jax.ShapeDtypeStruct((nb, NH_M, DHV_M, DHK_M), F32),
                   jax.ShapeDtypeStruct((nb, NH_M, DHK_M), F32),
                   jax.ShapeDtypeStruct((nb, 1, LANES), F32)],
        grid=(1, seq // CHUNK),
        in_specs=[tok(2 * DQK_M), pl.BlockSpec((nb, DV_M, CHUNK), lambda o, c: (0, 0, c)), tok(LANES),
                  st_c, st_n, st_m, pl.BlockSpec((DV_M, 1), lambda o, c: (0, 0))],
        out_specs=[tok(DV_M), st_c, st_n, st_m],
        scratch_shapes=[pltpu.VMEM((NH_M, DHV_M, CHUNK), F32)],
        compiler_params=_params(2),
        name="mlstm_prompt",
    )(qk, vt, g, c0, n0, m0, norm_w_col)


def _swa_prompt_body(prm_ref, q_ref, kc_ref, kp_ref, vc_ref, vp_ref, o_ref, tbl):
    row = lax.broadcasted_iota(jnp.int32, (WINDOW, WINDOW), 0)
    col = lax.broadcasted_iota(jnp.int32, (WINDOW, WINDOW), 1)
    tri = row <= col
    lo = col < HD_A

    @pl.when(_first_step())
    def _():
        dist = jnp.where(tri, col - row, col - row + WINDOW).astype(F32)
        for hd in range(NH_A):
            bias = prm_ref[0, hd] * dist
            tbl[0, hd] = bias
            tbl[1, hd] = bias + jnp.where(tri, 0.0, -NEG)

    which = jnp.where(pl.program_id(1) == 0, 1, 0)
    zb = jnp.zeros((WINDOW, LANES), BF16)

    def placed(k_ref, kv):
        c, par = kv // 2, kv % 2
        own = k_ref[:, c * LANES:(c + 1) * LANES]
        swp = k_ref[:, DKV_A + c * LANES:DKV_A + (c + 1) * LANES]
        if par == 0:
            return jnp.where(lo, own, zb), jnp.where(lo, zb, swp)
        return jnp.where(lo, swp, zb), jnp.where(lo, zb, own)

    st = []
    for kv in range(NKV_A):
        lhs = jnp.concatenate([*placed(kc_ref, kv), *placed(kp_ref, kv)], axis=0)
        qg = jnp.concatenate([q_ref[:, (2 * kv) * LANES:(2 * kv + 1) * LANES],
                              q_ref[:, (2 * kv + 1) * LANES:(2 * kv + 2) * LANES]], axis=0)
        st.append(_dot_nt(lhs, qg))
    tiles = []
    for hd in range(NH_A):
        kv, a, par = hd // GROUP_A, (hd % GROUP_A) // 2, hd % 2
        cols = slice(a * WINDOW, (a + 1) * WINDOW)
        tiles.append(jnp.where(tri, st[kv][par * WINDOW:(par + 1) * WINDOW, cols],
                               st[kv][(2 + par) * WINDOW:(3 + par) * WINDOW, cols]))
    sc = jnp.concatenate(tiles, axis=0).reshape(NH_A, WINDOW, WINDOW) - tbl[which]
    sink = jnp.concatenate([jnp.full((1, 1, WINDOW), prm_ref[1, hd], F32) for hd in range(NH_A)], axis=0)
    mx = jnp.maximum(jnp.max(sc, axis=1, keepdims=True), sink)
    p = jnp.exp(sc - mx)
    den = jnp.sum(p, axis=1, keepdims=True) + jnp.exp(sink - mx)
    pn = p * (1.0 / den)

    zero = jnp.zeros((WINDOW, WINDOW), F32)
    z64 = jnp.zeros((HD_A, WINDOW), BF16)
    outs = []
    for kv in range(NKV_A):
        cols = []
        for a in range(2):
            pe, po = pn[kv * GROUP_A + 2 * a], pn[kv * GROUP_A + 2 * a + 1]
            cols.append(jnp.concatenate([jnp.where(tri, pe, zero), jnp.where(tri, po, zero),
                                         jnp.where(tri, zero, pe), jnp.where(tri, zero, po)],
                                        axis=0).astype(BF16))
        pt = jnp.concatenate(cols, axis=1)
        vc = vc_ref[kv * HD_A:(kv + 1) * HD_A, :].astype(BF16)
        vp = vp_ref[kv * HD_A:(kv + 1) * HD_A, :].astype(BF16)
        vt = jnp.concatenate([jnp.concatenate([vc, z64], axis=0), jnp.concatenate([z64, vc], axis=0),
                              jnp.concatenate([vp, z64], axis=0), jnp.concatenate([z64, vp], axis=0)],
                             axis=1)
        ot = _dot(vt, pt)
        outs += [ot[:, :WINDOW].T, ot[:, WINDOW:].T]
    o_ref[...] = jnp.concatenate(outs, axis=1).astype(o_ref.dtype)


def _swa_prompt(prm, q, kk, vat):
    bsz, seq, _ = q.shape
    prev = lambda i: jnp.maximum(i - 1, 0)
    return pl.pallas_call(
        _swa_prompt_body,
        out_shape=jax.ShapeDtypeStruct((bsz, seq, DQ_A), BF16),
        grid=(bsz, seq // WINDOW),
        in_specs=[pl.BlockSpec(memory_space=pltpu.SMEM),
                  pl.BlockSpec((None, WINDOW, DQ_A), lambda b, i: (b, i, 0)),
                  pl.BlockSpec((None, WINDOW, 2 * DKV_A), lambda b, i: (b, i, 0)),
                  pl.BlockSpec((None, WINDOW, 2 * DKV_A), lambda b, i: (b, prev(i), 0)),
                  pl.BlockSpec((None, DKV_A, WINDOW), lambda b, i: (b, 0, i)),
                  pl.BlockSpec((None, DKV_A, WINDOW), lambda b, i: (b, 0, prev(i)))],
        out_specs=pl.BlockSpec((None, WINDOW, DQ_A), lambda b, i: (b, i, 0)),
        scratch_shapes=[pltpu.VMEM((2, NH_A, WINDOW, WINDOW), F32)],
        compiler_params=_params(2),
        name="swa_prompt",
    )(prm, q, kk, kk, vat, vat)


def _half_mask(shape, half):
    lane = lax.broadcasted_iota(jnp.int32, shape, 1)
    return lane < HD_A if half == 0 else lane >= HD_A


def _swa_sample_body(t_in, nb, prm_ref, q_ref, kn_ref, vn_ref, kct_ref, vct_ref,
                     o_ref, kco_ref, vco_ref, tbl, q8, kn_pad, vn_pad):
    tq = SUBLANES
    rows = NH_A * tq
    row = lax.broadcasted_iota(jnp.int32, (rows, WINDOW), 0)
    col = lax.broadcasted_iota(jnp.int32, (rows, WINDOW), 1)
    tri = col <= (row & (tq - 1))

    @pl.when(_first_step())
    def _():
        r8 = lax.broadcasted_iota(jnp.int32, (tq, WINDOW), 0)
        c8 = lax.broadcasted_iota(jnp.int32, (tq, WINDOW), 1)
        dist = jnp.where(c8 <= r8, r8 - c8, r8 - c8 + WINDOW).astype(F32)
        for hd in range(NH_A):
            tbl[pl.ds(hd * tq, tq), :] = prm_ref[0, hd] * dist
        for p in (q8, kn_pad, vn_pad):
            p[...] = jnp.zeros(p.shape, p.dtype)

    q8[:, pl.ds(0, t_in), :] = q_ref[...]
    kn_pad[:, pl.ds(0, t_in), :] = kn_ref[...]
    vn_pad[:, pl.ds(0, t_in), :] = vn_ref[...]
    sink_col = jnp.concatenate([jnp.full((tq, 1), prm_ref[1, hd], F32) for hd in range(NH_A)], axis=0)
    z8 = jnp.zeros((tq, LANES), F32)
    n_chunk = NKV_A // 2
    heads_per_chunk = NH_A // n_chunk

    def place(piece, src_half, dst_half):
        if src_half != dst_half:
            piece = pltpu.roll(piece, HD_A, axis=1)
        return jnp.where(_half_mask(piece.shape, dst_half), piece, z8)

    qexp = []
    for s in range(nb):
        per_c = []
        for c in range(n_chunk):
            pieces = []
            for hl in range(heads_per_chunk):
                hd = c * heads_per_chunk + hl
                pieces.append(place(q8[s, :, (hd // 2) * LANES:(hd // 2 + 1) * LANES],
                                    hd % 2, hl // GROUP_A))
            per_c.append(jnp.concatenate(pieces, axis=0).astype(BF16))
        qexp.append(per_c)
    csl = lambda c: slice(c * LANES, (c + 1) * LANES)
    s_prev = [[_dot(qexp[s][c], kct_ref[s, csl(c), :].astype(BF16)) for c in range(n_chunk)]
              for s in range(nb)]
    s_cur = [[_dot_nt(qexp[s][c], kn_pad[s, :, csl(c)].astype(BF16)) for c in range(n_chunk)]
             for s in range(nb)]
    sc = [jnp.where(tri, jnp.concatenate(s_cur[s], axis=0), jnp.concatenate(s_prev[s], axis=0)) - tbl[...]
          for s in range(nb)]
    mx = [jnp.maximum(jnp.max(sc[s], axis=-1, keepdims=True), sink_col) for s in range(nb)]
    p = [jnp.exp(sc[s] - mx[s]) for s in range(nb)]
    den = [jnp.sum(p[s], axis=-1, keepdims=True) + jnp.exp(sink_col - mx[s]) for s in range(nb)]
    pn = [p[s] * (1.0 / den[s]) for s in range(nb)]
    zero = jnp.zeros((rows, WINDOW), F32)
    pc = [jnp.where(tri, pn[s], zero).astype(BF16) for s in range(nb)]
    pp = [jnp.where(tri, zero, pn[s]).astype(BF16) for s in range(nb)]
    half_rows = heads_per_chunk * tq
    oc = [[_dot_nt(pp[s][c * half_rows:(c + 1) * half_rows], vct_ref[s, csl(c), :].astype(BF16))
           + _dot(pc[s][c * half_rows:(c + 1) * half_rows], vn_pad[s, :, csl(c)].astype(BF16))
           for c in range(n_chunk)] for s in range(nb)]
    for s in range(nb):
        chunks = []
        for pch in range(NH_A // 2):
            acc = None
            for hd in (2 * pch, 2 * pch + 1):
                c, hl = hd // heads_per_chunk, hd % heads_per_chunk
                piece = place(oc[s][c][hl * tq:(hl + 1) * tq, :], hl // GROUP_A, hd % 2)
                acc = piece if acc is None else acc + piece
            chunks.append(acc)
        o_ref[s] = jnp.concatenate(chunks, axis=1)[:t_in]

    lane = lax.broadcasted_iota(jnp.int32, (DKV_A, WINDOW), 1)
    for new_pad, old_ref, out_ref in ((kn_pad, kct_ref, kco_ref), (vn_pad, vct_ref, vco_ref)):
        for s in range(nb):
            merged = jnp.where(lane < t_in, new_pad[s].T, old_ref[s])
            out_ref[s] = pltpu.roll(merged, WINDOW - t_in, axis=1)


def _swa_sample(prm, q, kn, vn, kct, vct, nb):
    nseq, t_in, _ = q.shape
    assert t_in <= SUBLANES
    cur = lambda n: pl.BlockSpec((nb, t_in, n), lambda o, i: (o, 0, 0))
    win = pl.BlockSpec((nb, DKV_A, WINDOW), lambda o, i: (o, 0, 0))
    return pl.pallas_call(
        functools.partial(_swa_sample_body, t_in, nb),
        out_shape=[jax.ShapeDtypeStruct((nseq, t_in, DQ_A), F32),
                   jax.ShapeDtypeStruct((nseq, DKV_A, WINDOW), F32),
                   jax.ShapeDtypeStruct((nseq, DKV_A, WINDOW), F32)],
        grid=(nseq // nb, 1),
        in_specs=[pl.BlockSpec(memory_space=pltpu.SMEM), cur(DQ_A), cur(DKV_A), cur(DKV_A), win, win],
        out_specs=[cur(DQ_A), win, win],
        scratch_shapes=[pltpu.VMEM((NH_A * SUBLANES, WINDOW), F32),
                        pltpu.VMEM((nb, SUBLANES, DQ_A), F32),
                        pltpu.VMEM((nb, WINDOW, DKV_A), F32),
                        pltpu.VMEM((nb, WINDOW, DKV_A), F32)],
        compiler_params=_params(2),
        name="swa_sample",
    )(prm, q, kn, vn, kct, vct)


def _merge_body(alpha, x_ref, sh_ref, sc_ref, g1_ref, hm_ref, ha_ref, wog_ref, bog_ref,
                wgg_ref, bgg_ref, wbm_ref, wba_ref, wo_ref, lg_ref, lb_ref, o_ref):
    x = x_ref[...]
    h = (_ln(x) * (1.0 + sc_ref[...]) + sh_ref[...]).astype(BF16)

    def branch_gate(j):
        cols = slice(j * D_MODEL, (j + 1) * D_MODEL)
        return jax.nn.sigmoid(_dot(h, wgg_ref[:, cols]) + bgg_ref[:, cols])

    hm = (hm_ref[...] * jax.nn.sigmoid(_dot(h, wog_ref[...]) + bog_ref[...])).astype(BF16)
    merged = branch_gate(0) * _dot(hm, wbm_ref[...])
    merged = merged + branch_gate(1) * _dot(ha_ref[...].astype(BF16), wba_ref[...])
    mo = _dot(merged.astype(BF16), wo_ref[...])
    o_ref[...] = _ln(alpha * x + g1_ref[...] * mo) * lg_ref[...] + lb_ref[...]


def _merge(x, sh, sc, g1, hm, ha, w, tm, alpha):
    bsz, seq, _ = x.shape
    mrows = sh.shape[1]
    weights = [w["wog"], w["bog"], w["wgg"], w["bgg"], w["wbm"], w["wba"], w["wo"],
               w["ln1_g"], w["ln1_b"]]
    return pl.pallas_call(
        functools.partial(_merge_body, alpha),
        out_shape=jax.ShapeDtypeStruct((bsz, seq, D_MODEL), F32),
        grid=(bsz, seq // tm),
        in_specs=[_tok_spec(tm, D_MODEL), _mod_spec(mrows, tm), _mod_spec(mrows, tm),
                  _mod_spec(mrows, tm), _tok_spec(tm, DV_M), _tok_spec(tm, DQ_A)]
                 + [_const_spec(a.shape) for a in weights],
        out_specs=_tok_spec(tm, D_MODEL),
        compiler_params=_params(2),
        name="merge",
    )(x, sh, sc, g1, hm, ha, *weights)


def _ffn_body(alpha, tm, stride, halo, x_ref, sh_ref, sc_ref, g2_ref, cb0_ref, wup_ref, bup_ref,
              cw_ref, cbias_ref, wdn_ref, bdn_ref, lg_ref, lb_ref, o_ref, cs_ref, ubuf, act):
    @pl.when(pl.program_id(1) == 0)
    def _():
        cs_ref[...] = cb0_ref[...]

    x = x_ref[...]
    h = (_ln(x) * (1.0 + sc_ref[...]) + sh_ref[...]).astype(BF16)
    ys = []
    for half in range(2):
        cols = slice(half * D_FF, (half + 1) * D_FF)
        u = _dot(h, wup_ref[:, cols]) + bup_ref[:, cols]
        ubuf[pl.ds(0, halo), :] = cs_ref[:, cols]
        ubuf[pl.ds(halo, tm), :] = u
        cs_ref[:, cols] = ubuf[pl.ds(tm, halo), :]
        y = cbias_ref[:, cols] + u * cw_ref[CONV_W - 1:CONV_W, cols]
        for j in range(CONV_W - 1):
            y = y + ubuf[pl.ds(halo - (CONV_W - 1 - j) * stride, tm), :] * cw_ref[j:j + 1, cols]
        ys.append(y)
    act[...] = (jax.nn.gelu(ys[0]) * ys[1]).astype(BF16)
    f = _dot(act[...], wdn_ref[...]) + bdn_ref[...]
    o_ref[...] = _ln(alpha * x + g2_ref[...] * f) * lg_ref[...] + lb_ref[...]


def _ffn(x, sh, sc, g2, cb0, w, tm, stride, alpha):
    bsz, seq, _ = x.shape
    mrows = sh.shape[1]
    halo = cb0.shape[1]
    cs = pl.BlockSpec((None, halo, 2 * D_FF), lambda b, i: (b, 0, 0))
    weights = [w["wup"], w["bup"], w["cw"], w["cbias"], w["wdn"], w["bdn"], w["ln2_g"], w["ln2_b"]]
    return pl.pallas_call(
        functools.partial(_ffn_body, alpha, tm, stride, halo),
        out_shape=[jax.ShapeDtypeStruct((bsz, seq, D_MODEL), F32),
                   jax.ShapeDtypeStruct((bsz, halo, 2 * D_FF), F32)],
        grid=(bsz, seq // tm),
        in_specs=[_tok_spec(tm, D_MODEL), _mod_spec(mrows, tm), _mod_spec(mrows, tm),
                  _mod_spec(mrows, tm), cs] + [_const_spec(a.shape) for a in weights],
        out_specs=[_tok_spec(tm, D_MODEL), cs],
        scratch_shapes=[pltpu.VMEM((halo + tm, D_FF), F32), pltpu.VMEM((tm, D_FF), BF16)],
        compiler_params=_params(2),
        name="ffn",
    )(x, sh, sc, g2, cb0, *weights)


def _prep_weights(w_in, b_in, mlstm_norm_w, w_branch_m, w_branch_a, w_out, ln1_g, ln1_b,
                  w_up, b_up, conv_w, conv_b, w_down, b_down, ln2_g, ln2_b):
    o_gate = 2 * DQK_M + DV_M
    o_og = o_gate + 2 * NH_M
    o_qa = o_og + DV_M
    o_ka = o_qa + DQ_A
    o_va = o_ka + DKV_A
    o_gm = o_va + DKV_A
    row = lambda a: a.reshape(1, -1)
    swap = lambda a: a.reshape(a.shape[:-1] + (DKV_A // LANES, 2, HD_A))[..., ::-1, :].reshape(a.shape)
    gate_pad = LANES - 2 * NH_M
    w_k, b_k = w_in[:, o_ka:o_va], b_in[o_ka:o_va]
    w_v, b_v = w_in[:, o_va:o_gm], b_in[o_va:o_gm]
    return dict(
        wqk=w_in[:, :2 * DQK_M].astype(BF16), bqk=row(b_in[:2 * DQK_M]),
        wvm=w_in[:, 2 * DQK_M:o_gate].astype(BF16), bvm_row=row(b_in[2 * DQK_M:o_gate]),
        wvm_t=w_in[:, 2 * DQK_M:o_gate].T.astype(BF16), bvm_col=b_in[2 * DQK_M:o_gate].reshape(-1, 1),
        wg=jnp.pad(w_in[:, o_gate:o_og], ((0, 0), (0, gate_pad))).astype(BF16),
        bg=row(jnp.pad(b_in[o_gate:o_og], (0, gate_pad))),
        wqa=w_in[:, o_qa:o_ka].astype(BF16), bqa=row(b_in[o_qa:o_ka]),
        wkk=jnp.concatenate([w_k, swap(w_k)], axis=1).astype(BF16),
        bkk=row(jnp.concatenate([b_k, swap(b_k)])),
        wv=w_v.astype(BF16), bv_row=row(b_v), wv_t=w_v.T.astype(BF16), bv_col=b_v.reshape(-1, 1),
        wog=w_in[:, o_og:o_qa].astype(BF16), bog=row(b_in[o_og:o_qa]),
        wgg=w_in[:, o_gm:].astype(BF16), bgg=row(b_in[o_gm:]),
        norm_w=row(mlstm_norm_w), norm_w_col=mlstm_norm_w.reshape(-1, 1),
        wbm=w_branch_m.astype(BF16), wba=w_branch_a.astype(BF16),
        wo=w_out.astype(BF16), ln1_g=row(ln1_g), ln1_b=row(ln1_b),
        wup=w_up.astype(BF16), bup=row(b_up), cw=conv_w, cbias=row(conv_b),
        wdn=w_down.astype(BF16), bdn=row(b_down), ln2_g=row(ln2_g), ln2_b=row(ln2_b))


def kernel(x_prompt, x_sample, c_prompt, c_sample, state_mlstm_C, state_mlstm_n, state_mlstm_m,
           cache_k_win, cache_v_win, state_ffn_conv, w_ada, b_ada, w_in, b_in, mlstm_norm_w,
           attn_sinks, w_branch_m, w_branch_a, w_out, ln1_g, ln1_b, w_up, b_up, conv_w, conv_b,
           w_down, b_down, ln2_g, ln2_b):
    depth = w_in.shape[0]
    bp, lp, _ = x_prompt.shape
    bs, ls, _ = x_sample.shape
    assert cache_k_win.shape[2] == WINDOW
    alpha = (2 * depth) ** 0.25
    dt = x_prompt.dtype
    slopes = jnp.exp2(-8.0 * jnp.arange(1, NH_A + 1, dtype=F32) / NH_A)
    tm_p = 512
    ns = bs * ls

    yp = x_prompt
    ys = x_sample.reshape(1, ns, D_MODEL)
    new_p, new_s = [], []
    n_c = bp + bs
    c_rows = -(-n_c // SUBLANES) * SUBLANES
    c_all = jnp.concatenate([c_sample, c_prompt, jnp.zeros((c_rows - n_c, D_MODEL), dt)], axis=0)
    for l in range(depth):
        w = _prep_weights(w_in[l], b_in[l], mlstm_norm_w[l], w_branch_m[l], w_branch_a[l], w_out[l],
                          ln1_g[l], ln1_b[l], w_up[l], b_up[l], conv_w[l], conv_b[l], w_down[l],
                          b_down[l], ln2_g[l], ln2_b[l])
        prm = jnp.stack([slopes, attn_sinks[l].astype(F32)])
        mod = _ada(c_all, w_ada[l], b_ada[l])
        mod_s = mod[:bs].reshape(bs, N_MOD, D_MODEL)
        mod_p = mod[bs:bs + bp].reshape(bp, N_MOD, D_MODEL)
        mp_ = [mod_p[:, j:j + 1] for j in range(N_MOD)]
        ms_seq = [jnp.repeat(mod_s[:, j], ls, axis=0)[None] for j in range(3)]
        ms_pos = [jnp.tile(mod_s[:, j], (ls, 1))[None] for j in range(3, N_MOD)]

        qk, v, g, qa, kk, ka, vat = _inproj(yp, mp_[0], mp_[1], w, tm_p, True)
        hm, cp, np_, mp = _mlstm_prompt(qk, v, g, jnp.zeros((bp, NH_M, DHV_M, DHK_M), dt),
                                        jnp.zeros((bp, NH_M, DHK_M), dt), jnp.zeros((bp, 1, LANES), dt),
                                        w["norm_w_col"])
        ha = _swa_prompt(prm, qa, kk, vat)
        x1p = _merge(yp, mp_[0], mp_[1], mp_[2], hm, ha, w, tm_p, alpha)
        halo_p = SUBLANES
        yp, csp = _ffn(x1p, mp_[3], mp_[4], mp_[5], jnp.zeros((bp, halo_p, 2 * D_FF), dt),
                       w, tm_p, 1, alpha)
        p_k = ka[:, lp - WINDOW:].reshape(bp, WINDOW, NKV_A, HD_A)
        p_v = vat[:, :, lp - WINDOW:].reshape(bp, NKV_A, HD_A, WINDOW).transpose(0, 3, 1, 2)
        new_p.append((cp, np_, mp[:, 0, :NH_M], p_k, p_v, csp[:, halo_p - (CONV_W - 1):]))

        qk, v, g, qa, kn, vn = _inproj(ys, ms_seq[0], ms_seq[1], w, ns, False)
        per_seq = lambda a: a.reshape(bs, ls, a.shape[-1])
        m0 = jnp.pad(state_mlstm_m[l], ((0, 0), (0, LANES - NH_M)))[:, None, :]
        hm, cs_, ns_, ms = _mlstm_sample(per_seq(qk), per_seq(v), per_seq(g), state_mlstm_C[l],
                                         state_mlstm_n[l], m0, w["norm_w"], SEQS_PER_STEP)
        to_t = lambda a: a.transpose(0, 2, 3, 1).reshape(bs, DKV_A, WINDOW)
        from_t = lambda a: a.reshape(bs, NKV_A, HD_A, WINDOW).transpose(0, 3, 1, 2)
        ha, kct, vct = _swa_sample(prm, per_seq(qa), per_seq(kn), per_seq(vn),
                                   to_t(cache_k_win[l]), to_t(cache_v_win[l]), SEQS_PER_STEP)
        x1s = _merge(ys, ms_seq[0], ms_seq[1], ms_seq[2], hm.reshape(1, ns, DV_M),
                     ha.reshape(1, ns, DQ_A), w, ns, alpha)
        to_pos = lambda a: a.reshape(bs, -1, a.shape[-1]).transpose(1, 0, 2).reshape(1, -1, a.shape[-1])
        y_pos, css = _ffn(to_pos(x1s[0]), ms_pos[0], ms_pos[1], ms_pos[2], to_pos(state_ffn_conv[l]),
                          w, ns, bs, alpha)
        ys = y_pos.reshape(ls, bs, D_MODEL).transpose(1, 0, 2).reshape(1, ns, D_MODEL)
        conv_s = css.reshape(CONV_W - 1, bs, 2 * D_FF).transpose(1, 0, 2)
        new_s.append((cs_, ns_, ms[:, 0, :NH_M], from_t(kct), from_t(vct), conv_s))

    p_state = [jnp.stack(a) for a in zip(*new_p)]
    s_state = [jnp.stack(a) for a in zip(*new_s)]
    return (yp, ys.reshape(bs, ls, D_MODEL), *p_state, *s_state)
```

```python
import functools

import jax
import jax.numpy as jnp
from jax import lax
from jax.experimental import pallas as pl
from jax.experimental.pallas import tpu as pltpu

F32 = jnp.float32
BF16 = jnp.bfloat16

D_MODEL = 1024
NH_M, DHK_M, DHV_M = 4, 128, 256
DQK_M, DV_M = NH_M * DHK_M, NH_M * DHV_M
NH_A, NKV_A, HD_A = 16, 4, 64
GROUP_A = NH_A // NKV_A
WINDOW = 128
DQ_A, DKV_A = NH_A * HD_A, NKV_A * HD_A
D_FF = 2816
CONV_W = 3
N_MOD = 6
LN_EPS = 1e-5
CHUNK = 128
NEG = -1e30
LANES = 128
SUBLANES = 8
VMEM_LIMIT = 56 * 1024 * 1024
SEQS_PER_STEP = 8


def _ln(x):
    mu = jnp.mean(x, axis=-1, keepdims=True)
    xc = x - mu
    var = jnp.mean(xc * xc, axis=-1, keepdims=True)
    return xc * lax.rsqrt(var + LN_EPS)


def _dot(a, b):
    return jnp.dot(a, b, preferred_element_type=F32)


def _dot_nt(a, b):
    return lax.dot_general(a, b, (((1,), (1,)), ((), ())), preferred_element_type=F32)


def _dot_tn(a, b):
    return lax.dot_general(a, b, (((0,), (0,)), ((), ())), preferred_element_type=F32)


def _const_spec(shape):
    nd = len(shape)
    return pl.BlockSpec(shape, lambda *_: (0,) * nd, pipeline_mode=pl.Buffered(1))


def _params(n_grid):
    return pltpu.CompilerParams(dimension_semantics=("arbitrary",) * n_grid,
                                vmem_limit_bytes=VMEM_LIMIT)


def _first_step():
    return (pl.program_id(0) == 0) & (pl.program_id(1) == 0)


def _ada_body(c_ref, w_ref, b_ref, o_ref):
    c = c_ref[...]
    s = (c * jax.nn.sigmoid(c)).astype(BF16)
    o_ref[...] = _dot(s, w_ref[...].astype(BF16)) + b_ref[...]


def _ada(c, w_ada, b_ada):
    rows = c.shape[0]
    n_out = w_ada.shape[1]
    bn = 512
    return pl.pallas_call(
        _ada_body,
        out_shape=jax.ShapeDtypeStruct((rows, n_out), F32),
        grid=(n_out // bn,),
        in_specs=[pl.BlockSpec((rows, D_MODEL), lambda j: (0, 0)),
                  pl.BlockSpec((D_MODEL, bn), lambda j: (0, j)),
                  pl.BlockSpec((1, bn), lambda j: (0, j))],
        out_specs=pl.BlockSpec((rows, bn), lambda j: (0, j)),
        compiler_params=_params(1),
        name="ada",
    )(c, w_ada, b_ada.reshape(1, n_out))


def _inproj_body(prompt, x_ref, sh_ref, sc_ref, wqk_ref, bqk_ref, wvm_ref, bvm_ref, wg_ref, bg_ref,
                 wqa_ref, bqa_ref, wkk_ref, bkk_ref, wv_ref, bv_ref, qk_ref, v_ref, g_ref, qa_ref, *rest):
    h = (_ln(x_ref[...]) * (1.0 + sc_ref[...]) + sh_ref[...]).astype(BF16)

    def proj(w_ref, b_ref, lo, n):
        return _dot(h, w_ref[:, lo:lo + n]) + b_ref[:, lo:lo + n]

    qk_ref[:, :DQK_M] = (proj(wqk_ref, bqk_ref, 0, DQK_M) * DHK_M ** -0.5).astype(qk_ref.dtype)
    qk_ref[:, DQK_M:] = proj(wqk_ref, bqk_ref, DQK_M, DQK_M).astype(qk_ref.dtype)
    if prompt:
        v_ref[...] = (_dot_nt(wvm_ref[...], h) + bvm_ref[...]).astype(v_ref.dtype)
    else:
        v_ref[...] = proj(wvm_ref, bvm_ref, 0, DV_M).astype(v_ref.dtype)
    g_ref[...] = proj(wg_ref, bg_ref, 0, LANES)
    qa_ref[...] = (proj(wqa_ref, bqa_ref, 0, DQ_A) * HD_A ** -0.5).astype(qa_ref.dtype)
    kk = proj(wkk_ref, bkk_ref, 0, 2 * DKV_A)
    if prompt:
        kk_ref, ka_ref, vat_ref = rest
        kk_ref[...] = kk.astype(BF16)
        ka_ref[...] = kk[:, :DKV_A]
        vat_ref[...] = _dot_nt(wv_ref[...], h) + bv_ref[...]
    else:
        ka_ref, va_ref = rest
        ka_ref[...] = kk[:, :DKV_A]
        va_ref[...] = _dot(h, wv_ref[...]) + bv_ref[...]


def _tok_spec(tm, n):
    return pl.BlockSpec((None, tm, n), lambda b, i: (b, i, 0))


def _mod_spec(rows, tm):
    if rows == 1:
        return pl.BlockSpec((None, 1, D_MODEL), lambda b, i: (b, 0, 0))
    return pl.BlockSpec((None, tm, D_MODEL), lambda b, i: (b, i, 0))


def _inproj(x, sh, sc, w, tm, prompt):
    bsz, seq, _ = x.shape
    mrows = sh.shape[1]
    act = BF16 if prompt else F32
    t_spec = lambda n: pl.BlockSpec((None, n, tm), lambda b, i: (b, 0, i))
    outs = [((bsz, seq, 2 * DQK_M), act, _tok_spec(tm, 2 * DQK_M)),
            ((bsz, DV_M, seq), act, t_spec(DV_M)) if prompt else ((bsz, seq, DV_M), act, _tok_spec(tm, DV_M)),
            ((bsz, seq, LANES), F32, _tok_spec(tm, LANES)),
            ((bsz, seq, DQ_A), act, _tok_spec(tm, DQ_A))]
    if prompt:
        wvm, bvm, wv, bv = w["wvm_t"], w["bvm_col"], w["wv_t"], w["bv_col"]
        outs += [((bsz, seq, 2 * DKV_A), BF16, _tok_spec(tm, 2 * DKV_A)),
                 ((bsz, seq, DKV_A), F32, _tok_spec(tm, DKV_A)),
                 ((bsz, DKV_A, seq), F32, t_spec(DKV_A))]
    else:
        wvm, bvm, wv, bv = w["wvm"], w["bvm_row"], w["wv"], w["bv_row"]
        outs += [((bsz, seq, DKV_A), F32, _tok_spec(tm, DKV_A)),
                 ((bsz, seq, DKV_A), F32, _tok_spec(tm, DKV_A))]
    weights = [w["wqk"], w["bqk"], wvm, bvm, w["wg"], w["bg"], w["wqa"], w["bqa"], w["wkk"], w["bkk"],
               wv, bv]
    return pl.pallas_call(
        functools.partial(_inproj_body, prompt),
        out_shape=[jax.ShapeDtypeStruct(s, dt) for s, dt, _ in outs],
        grid=(bsz, seq // tm),
        in_specs=[_tok_spec(tm, D_MODEL), _mod_spec(mrows, tm), _mod_spec(mrows, tm)]
                 + [_const_spec(a.shape) for a in weights],
        out_specs=[spec for _, _, spec in outs],
        compiler_params=_params(2),
        name="inproj",
    )(x, sh, sc, *weights)


def _scan_rows(x, op, rows):
    row = lax.broadcasted_iota(jnp.int32, x.shape, 0)
    d = 1
    while d < rows:
        shifted = pltpu.roll(x, d, axis=0)
        x = jnp.where(row >= d, op(x, shifted), x)
        d *= 2
    return x


def _mlstm_sample_body(t_in, nb, qk_ref, v_ref, g_ref, c0_ref, n0_ref, m0_ref, nw_ref,
                       h_ref, c_ref, n_ref, m_ref, *pads):
    t = SUBLANES
    c_ref[...] = c0_ref[...]
    n_ref[...] = n0_ref[...]
    m_ref[...] = m0_ref[...]

    @pl.when(_first_step())
    def _():
        for p in pads:
            p[...] = jnp.zeros(p.shape, p.dtype)
    for src, dst in zip((qk_ref, v_ref, g_ref), pads):
        dst[:, pl.ds(0, t_in), :] = src[...]
    qk_src, v_src, g_src = pads

    row = lax.broadcasted_iota(jnp.int32, (t, LANES), 0)
    lane = lax.broadcasted_iota(jnp.int32, (t, LANES), 1)
    lane1 = lax.broadcasted_iota(jnp.int32, (1, LANES), 1)
    r2 = lax.broadcasted_iota(jnp.int32, (t, t), 0)
    c2 = lax.broadcasted_iota(jnp.int32, (t, t), 1)
    causal = c2 <= r2
    pad_gate = jnp.where(lane < NH_M, NEG, -NEG)

    seqs = []
    for s in range(nb):
        ga = _gate_algebra(jnp.where(row < t_in, g_src[s], pad_gate), m_ref[s], t)
        ga["gt"] = ga["gg"].T
        m_ref[s] = jnp.where(lane1 < NH_M, ga["m_new"], 0.0)
        seqs.append(ga)

    units = [(s, hd) for s in range(nb) for hd in range(NH_M)]
    nu = len(units)
    col = lambda name, s, hd: seqs[s][name][:, hd:hd + 1]
    q = [qk_src[s, :, hd * DHK_M:(hd + 1) * DHK_M].astype(BF16) for s, hd in units]
    k = [qk_src[s, :, DQK_M + hd * DHK_M:DQK_M + (hd + 1) * DHK_M].astype(BF16) for s, hd in units]
    v = [v_src[s, :, hd * DHV_M:(hd + 1) * DHV_M].astype(BF16) for s, hd in units]
    c_old = [c_ref[s, hd] for s, hd in units]
    n_old = [n_ref[s, hd:hd + 1, :] for s, hd in units]

    sqk = [_dot_nt(q[u], k[u]) for u in range(nu)]
    dexp = [jnp.exp(jnp.where(causal, col("eb", s, hd) + seqs[s]["gt"][hd:hd + 1, :], NEG))
            for s, hd in units]
    smat = [sqk[u] * dexp[u] for u in range(nu)]
    intra = [_dot(smat[u].astype(BF16), v[u]) for u in range(nu)]
    inter = [_dot_nt(q[u], c_old[u].astype(BF16)) for u in range(nu)]
    qn = [jnp.sum(q[u].astype(F32) * n_old[u].astype(BF16).astype(F32), axis=-1, keepdims=True)
          for u in range(nu)]
    den = [jnp.sum(smat[u], axis=-1, keepdims=True) + col("a_in", s, hd) * qn[u]
           for u, (s, hd) in enumerate(units)]
    hh = [(intra[u] + col("a_in", s, hd) * inter[u])
          / jnp.maximum(jnp.abs(den[u]), col("lowb", s, hd)) for u, (s, hd) in enumerate(units)]
    hn = [_ln(hh[u]) * nw_ref[:, hd * DHV_M:(hd + 1) * DHV_M] for u, (s, hd) in enumerate(units)]
    for u, (s, hd) in enumerate(units):
        h_ref[s, :, hd * DHV_M:(hd + 1) * DHV_M] = hn[u][:t_in]

    kw = [k[u].astype(F32) * col("ws", s, hd) for u, (s, hd) in enumerate(units)]
    upd = [_dot_tn(v[u], kw[u].astype(BF16)) for u in range(nu)]
    for u, (s, hd) in enumerate(units):
        dec = seqs[s]["decay"][:, hd:hd + 1]
        c_ref[s, hd] = dec * c_old[u] + upd[u]
        n_ref[s, hd:hd + 1, :] = dec * n_old[u] + jnp.sum(kw[u], axis=0, keepdims=True)


def _mlstm_sample(qk, v, g, c0, n0, m0, norm_w, nb):
    nseq, t_in, _ = qk.shape
    assert t_in <= SUBLANES
    blk = lambda n: pl.BlockSpec((nb, t_in, n), lambda o, c: (o, 0, 0))
    st_c = pl.BlockSpec((nb, NH_M, DHV_M, DHK_M), lambda o, c: (o, 0, 0, 0))
    st_n = pl.BlockSpec((nb, NH_M, DHK_M), lambda o, c: (o, 0, 0))
    st_m = pl.BlockSpec((nb, 1, LANES), lambda o, c: (o, 0, 0))
    return pl.pallas_call(
        functools.partial(_mlstm_sample_body, t_in, nb),
        out_shape=[jax.ShapeDtypeStruct((nseq, t_in, DV_M), F32),
                   jax.ShapeDtypeStruct((nseq, NH_M, DHV_M, DHK_M), F32),
                   jax.ShapeDtypeStruct((nseq, NH_M, DHK_M), F32),
                   jax.ShapeDtypeStruct((nseq, 1, LANES), F32)],
        grid=(nseq // nb, 1),
        in_specs=[blk(2 * DQK_M), blk(DV_M), blk(LANES), st_c, st_n, st_m,
                  pl.BlockSpec((1, DV_M), lambda o, c: (0, 0))],
        out_specs=[blk(DV_M), st_c, st_n, st_m],
        scratch_shapes=[pltpu.VMEM((nb, SUBLANES, 2 * DQK_M), F32), pltpu.VMEM((nb, SUBLANES, DV_M), F32),
                        pltpu.VMEM((nb, SUBLANES, LANES), F32)],
        compiler_params=_params(2),
        name="mlstm_sample",
    )(qk, v, g, c0, n0, m0, norm_w)


def _gate_algebra(g, m_prev, t):
    b = pltpu.roll(_scan_rows(jax.nn.log_sigmoid(g), jnp.add, t), LANES - NH_M, axis=1)
    gg = g - b
    gmx = _scan_rows(gg, jnp.maximum, t)
    a = b + m_prev
    mt = jnp.maximum(a, b + gmx)
    b_last, gmx_last = b[t - 1:t, :], gmx[t - 1:t, :]
    m_new = jnp.maximum(b_last + m_prev, b_last + gmx_last)
    return dict(gg=gg, eb=b - mt, a_in=jnp.exp(a - mt), lowb=jnp.exp(-mt),
                ws=jnp.exp(b_last + gg - m_new), decay=jnp.exp(b_last + m_prev - m_new), m_new=m_new)


def _mlstm_prompt_body(nb, qk_ref, vt_ref, g_ref, c0_ref, n0_ref, m0_ref, nwc_ref,
                       h_ref, c_ref, n_ref, m_ref, nwb):
    t = CHUNK

    @pl.when(pl.program_id(1) == 0)
    def _():
        c_ref[...] = c0_ref[...]
        n_ref[...] = n0_ref[...]
        m_ref[...] = m0_ref[...]

    @pl.when(_first_step())
    def _():
        for hd in range(NH_M):
            nwb[hd] = jnp.broadcast_to(nwc_ref[hd * DHV_M:(hd + 1) * DHV_M, :], (DHV_M, t))

    lane1 = lax.broadcasted_iota(jnp.int32, (1, LANES), 1)
    r2 = lax.broadcasted_iota(jnp.int32, (t, t), 0)
    c2 = lax.broadcasted_iota(jnp.int32, (t, t), 1)
    causal = r2 <= c2

    seqs = []
    for s in range(nb):
        ga = _gate_algebra(g_ref[s], m_ref[s], t)
        m_ref[s] = jnp.where(lane1 < NH_M, ga["m_new"], 0.0)
        for name in ("eb", "a_in", "lowb", "ws"):
            ga[name + "_t"] = ga[name].T
        seqs.append(ga)

    units = [(s, hd) for s in range(nb) for hd in range(NH_M)]
    nu = len(units)
    rowv = lambda name, s, hd: seqs[s][name + "_t"][hd:hd + 1, :]
    q = [qk_ref[s, :, hd * DHK_M:(hd + 1) * DHK_M] for s, hd in units]
    k = [qk_ref[s, :, DQK_M + hd * DHK_M:DQK_M + (hd + 1) * DHK_M] for s, hd in units]
    vt = [vt_ref[s, hd * DHV_M:(hd + 1) * DHV_M, :] for s, hd in units]
    c_old = [c_ref[s, hd] for s, hd in units]
    n_old = [n_ref[s, hd:hd + 1, :] for s, hd in units]

    skq = [_dot_nt(k[u], q[u]) for u in range(nu)]
    dexp = [jnp.exp(jnp.where(causal, seqs[s]["gg"][:, hd:hd + 1] + rowv("eb", s, hd), NEG))
            for s, hd in units]
    smat = [skq[u] * dexp[u] for u in range(nu)]
    intra = [_dot(vt[u], smat[u].astype(BF16)) for u in range(nu)]
    inter = [_dot_nt(c_old[u].astype(BF16), q[u]) for u in range(nu)]
    qn = [_dot_nt(jnp.broadcast_to(n_old[u], (SUBLANES, DHK_M)).astype(BF16), q[u])[0:1, :]
          for u in range(nu)]
    den = [jnp.sum(smat[u], axis=0, keepdims=True) + rowv("a_in", s, hd) * qn[u]
           for u, (s, hd) in enumerate(units)]
    inv = [1.0 / jnp.maximum(jnp.abs(den[u]), rowv("lowb", s, hd)) for u, (s, hd) in enumerate(units)]
    hh = [(intra[u] + rowv("a_in", s, hd) * inter[u]) * inv[u] for u, (s, hd) in enumerate(units)]
    mu = [jnp.mean(hh[u], axis=0, keepdims=True) for u in range(nu)]
    xc = [hh[u] - mu[u] for u in range(nu)]
    var = [jnp.mean(xc[u] * xc[u], axis=0, keepdims=True) for u in range(nu)]
    hn = [xc[u] * lax.rsqrt(var[u] + LN_EPS) * nwb[hd] for u, (s, hd) in enumerate(units)]
    for u, (s, hd) in enumerate(units):
        h_ref[s, :, hd * DHV_M:(hd + 1) * DHV_M] = hn[u].T

    vw = [(vt[u].astype(F32) * rowv("ws", s, hd)).astype(BF16) for u, (s, hd) in enumerate(units)]
    upd = [_dot(vw[u], k[u]) for u in range(nu)]
    nupd = [_dot(jnp.broadcast_to(rowv("ws", s, hd), (SUBLANES, t)).astype(BF16), k[u])[0:1, :]
            for u, (s, hd) in enumerate(units)]
    for u, (s, hd) in enumerate(units):
        dec = seqs[s]["decay"][:, hd:hd + 1]
        c_ref[s, hd] = dec * c_old[u] + upd[u]
        n_ref[s, hd:hd + 1, :] = dec * n_old[u] + nupd[u]


def _mlstm_prompt(qk, vt, g, c0, n0, m0, norm_w_col):
    nb, seq, _ = qk.shape
    tok = lambda n: pl.BlockSpec((nb, CHUNK, n), lambda o, c: (0, c, 0))
    st_c = pl.BlockSpec((nb, NH_M, DHV_M, DHK_M), lambda o, c: (0, 0, 0, 0))
    st_n = pl.BlockSpec((nb, NH_M, DHK_M), lambda o, c: (0, 0, 0))
    st_m = pl.BlockSpec((nb, 1, LANES), lambda o, c: (0, 0, 0))
    return pl.pallas_call(
        functools.partial(_mlstm_prompt_body, nb),
        out_shape=[jax.ShapeDtypeStruct((nb, seq, DV_M), F32),
                   jax.ShapeDtypeStruct((nb, NH_M, DHV_M, DHK_M), F32),
                   jax.ShapeDtypeStruct((nb, NH_M, DHK_M), F32),
                   jax.ShapeDtypeStruct((nb, 1, LANES), F32)],
        grid=(1, seq // CHUNK),
        in_specs=[tok(2 * DQK_M), pl.BlockSpec((nb, DV_M, CHUNK), lambda o, c: (0, 0, c)), tok(LANES),
                  st_c, st_n, st_m, pl.BlockSpec((DV_M, 1), lambda o, c: (0, 0))],
        out_specs=[tok(DV_M), st_c, st_n, st_m],
        scratch_shapes=[pltpu.VMEM((NH_M, DHV_M, CHUNK), F32)],
        compiler_params=_params(2),
        name="mlstm_prompt",
    )(qk, vt, g, c0, n0, m0, norm_w_col)


def _swa_prompt_body(prm_ref, q_ref, kc_ref, kp_ref, vc_ref, vp_ref, o_ref, tbl):
    row = lax.broadcasted_iota(jnp.int32, (WINDOW, WINDOW), 0)
    col = lax.broadcasted_iota(jnp.int32, (WINDOW, WINDOW), 1)
    tri = row <= col
    lo = col < HD_A

    @pl.when(_first_step())
    def _():
        dist = jnp.where(tri, col - row, col - row + WINDOW).astype(F32)
        for hd in range(NH_A):
            bias = prm_ref[0, hd] * dist
            tbl[0, hd] = bias
            tbl[1, hd] = bias + jnp.where(tri, 0.0, -NEG)

    which = jnp.where(pl.program_id(1) == 0, 1, 0)
    zb = jnp.zeros((WINDOW, LANES), BF16)

    def placed(k_ref, kv):
        c, par = kv // 2, kv % 2
        own = k_ref[:, c * LANES:(c + 1) * LANES]
        swp = k_ref[:, DKV_A + c * LANES:DKV_A + (c + 1) * LANES]
        if par == 0:
            return jnp.where(lo, own, zb), jnp.where(lo, zb, swp)
        return jnp.where(lo, swp, zb), jnp.where(lo, zb, own)

    st = []
    for kv in range(NKV_A):
        lhs = jnp.concatenate([*placed(kc_ref, kv), *placed(kp_ref, kv)], axis=0)
        qg = jnp.concatenate([q_ref[:, (2 * kv) * LANES:(2 * kv + 1) * LANES],
                              q_ref[:, (2 * kv + 1) * LANES:(2 * kv + 2) * LANES]], axis=0)
        st.append(_dot_nt(lhs, qg))
    tiles = []
    for hd in range(NH_A):
        kv, a, par = hd // GROUP_A, (hd % GROUP_A) // 2, hd % 2
        cols = slice(a * WINDOW, (a + 1) * WINDOW)
        tiles.append(jnp.where(tri, st[kv][par * WINDOW:(par + 1) * WINDOW, cols],
                               st[kv][(2 + par) * WINDOW:(3 + par) * WINDOW, cols]))
    sc = jnp.concatenate(tiles, axis=0).reshape(NH_A, WINDOW, WINDOW) - tbl[which]
    sink = jnp.concatenate([jnp.full((1, 1, WINDOW), prm_ref[1, hd], F32) for hd in range(NH_A)], axis=0)
    mx = jnp.maximum(jnp.max(sc, axis=1, keepdims=True), sink)
    p = jnp.exp(sc - mx)
    den = jnp.sum(p, axis=1, keepdims=True) + jnp.exp(sink - mx)
    pn = p * (1.0 / den)

    zero = jnp.zeros((WINDOW, WINDOW), F32)
    z64 = jnp.zeros((HD_A, WINDOW), BF16)
    outs = []
    for kv in range(NKV_A):
        cols = []
        for a in range(2):
            pe, po = pn[kv * GROUP_A + 2 * a], pn[kv * GROUP_A + 2 * a + 1]
            cols.append(jnp.concatenate([jnp.where(tri, pe, zero), jnp.where(tri, po, zero),
                                         jnp.where(tri, zero, pe), jnp.where(tri, zero, po)],
                                        axis=0).astype(BF16))
        pt = jnp.concatenate(cols, axis=1)
        vc = vc_ref[kv * HD_A:(kv + 1) * HD_A, :].astype(BF16)
        vp = vp_ref[kv * HD_A:(kv + 1) * HD_A, :].astype(BF16)
        vt = jnp.concatenate([jnp.concatenate([vc, z64], axis=0), jnp.concatenate([z64, vc], axis=0),
                              jnp.concatenate([vp, z64], axis=0), jnp.concatenate([z64, vp], axis=0)],
                             axis=1)
        ot = _dot(vt, pt)
        outs += [ot[:, :WINDOW].T, ot[:, WINDOW:].T]
    o_ref[...] = jnp.concatenate(outs, axis=1).astype(o_ref.dtype)


def _swa_prompt(prm, q, kk, vat):
    bsz, seq, _ = q.shape
    prev = lambda i: jnp.maximum(i - 1, 0)
    return pl.pallas_call(
        _swa_prompt_body,
        out_shape=jax.ShapeDtypeStruct((bsz, seq, DQ_A), BF16),
        grid=(bsz, seq // WINDOW),
        in_specs=[pl.BlockSpec(memory_space=pltpu.SMEM),
                  pl.BlockSpec((None, WINDOW, DQ_A), lambda b, i: (b, i, 0)),
                  pl.BlockSpec((None, WINDOW, 2 * DKV_A), lambda b, i: (b, i, 0)),
                  pl.BlockSpec((None, WINDOW, 2 * DKV_A), lambda b, i: (b, prev(i), 0)),
                  pl.BlockSpec((None, DKV_A, WINDOW), lambda b, i: (b, 0, i)),
                  pl.BlockSpec((None, DKV_A, WINDOW), lambda b, i: (b, 0, prev(i)))],
        out_specs=pl.BlockSpec((None, WINDOW, DQ_A), lambda b, i: (b, i, 0)),
        scratch_shapes=[pltpu.VMEM((2, NH_A, WINDOW, WINDOW), F32)],
        compiler_params=_params(2),
        name="swa_prompt",
    )(prm, q, kk, kk, vat, vat)


def _half_mask(shape, half):
    lane = lax.broadcasted_iota(jnp.int32, shape, 1)
    return lane < HD_A if half == 0 else lane >= HD_A


def _swa_sample_body(t_in, nb, prm_ref, q_ref, kn_ref, vn_ref, kct_ref, vct_ref,
                     o_ref, kco_ref, vco_ref, tbl, q8, kn_pad, vn_pad):
    tq = SUBLANES
    rows = NH_A * tq
    row = lax.broadcasted_iota(jnp.int32, (rows, WINDOW), 0)
    col = lax.broadcasted_iota(jnp.int32, (rows, WINDOW), 1)
    tri = col <= (row & (tq - 1))

    @pl.when(_first_step())
    def _():
        r8 = lax.broadcasted_iota(jnp.int32, (tq, WINDOW), 0)
        c8 = lax.broadcasted_iota(jnp.int32, (tq, WINDOW), 1)
        dist = jnp.where(c8 <= r8, r8 - c8, r8 - c8 + WINDOW).astype(F32)
        for hd in range(NH_A):
            tbl[pl.ds(hd * tq, tq), :] = prm_ref[0, hd] * dist
        for p in (q8, kn_pad, vn_pad):
            p[...] = jnp.zeros(p.shape, p.dtype)

    q8[:, pl.ds(0, t_in), :] = q_ref[...]
    kn_pad[:, pl.ds(0, t_in), :] = kn_ref[...]
    vn_pad[:, pl.ds(0, t_in), :] = vn_ref[...]
    sink_col = jnp.concatenate([jnp.full((tq, 1), prm_ref[1, hd], F32) for hd in range(NH_A)], axis=0)
    z8 = jnp.zeros((tq, LANES), F32)
    n_chunk = NKV_A // 2
    heads_per_chunk = NH_A // n_chunk

    def place(piece, src_half, dst_half):
        if src_half != dst_half:
            piece = pltpu.roll(piece, HD_A, axis=1)
        return jnp.where(_half_mask(piece.shape, dst_half), piece, z8)

    qexp = []
    for s in range(nb):
        per_c = []
        for c in range(n_chunk):
            pieces = []
            for hl in range(heads_per_chunk):
                hd = c * heads_per_chunk + hl
                pieces.append(place(q8[s, :, (hd // 2) * LANES:(hd // 2 + 1) * LANES],
                                    hd % 2, hl // GROUP_A))
            per_c.append(jnp.concatenate(pieces, axis=0).astype(BF16))
        qexp.append(per_c)
    csl = lambda c: slice(c * LANES, (c + 1) * LANES)
    s_prev = [[_dot(qexp[s][c], kct_ref[s, csl(c), :].astype(BF16)) for c in range(n_chunk)]
              for s in range(nb)]
    s_cur = [[_dot_nt(qexp[s][c], kn_pad[s, :, csl(c)].astype(BF16)) for c in range(n_chunk)]
             for s in range(nb)]
    sc = [jnp.where(tri, jnp.concatenate(s_cur[s], axis=0), jnp.concatenate(s_prev[s], axis=0)) - tbl[...]
          for s in range(nb)]
    mx = [jnp.maximum(jnp.max(sc[s], axis=-1, keepdims=True), sink_col) for s in range(nb)]
    p = [jnp.exp(sc[s] - mx[s]) for s in range(nb)]
    den = [jnp.sum(p[s], axis=-1, keepdims=True) + jnp.exp(sink_col - mx[s]) for s in range(nb)]
    pn = [p[s] * (1.0 / den[s]) for s in range(nb)]
    zero = jnp.zeros((rows, WINDOW), F32)
    pc = [jnp.where(tri, pn[s], zero).astype(BF16) for s in range(nb)]
    pp = [jnp.where(tri, zero, pn[s]).astype(BF16) for s in range(nb)]
    half_rows = heads_per_chunk * tq
    oc = [[_dot_nt(pp[s][c * half_rows:(c + 1) * half_rows], vct_ref[s, csl(c), :].astype(BF16))
           + _dot(pc[s][c * half_rows:(c + 1) * half_rows], vn_pad[s, :, csl(c)].astype(BF16))
           for c in range(n_chunk)] for s in range(nb)]
    for s in range(nb):
        chunks = []
        for pch in range(NH_A // 2):
            acc = None
            for hd in (2 * pch, 2 * pch + 1):
                c, hl = hd // heads_per_chunk, hd % heads_per_chunk
                piece = place(oc[s][c][hl * tq:(hl + 1) * tq, :], hl // GROUP_A, hd % 2)
                acc = piece if acc is None else acc + piece
            chunks.append(acc)
        o_ref[s] = jnp.concatenate(chunks, axis=1)[:t_in]

    lane = lax.broadcasted_iota(jnp.int32, (DKV_A, WINDOW), 1)
    for new_pad, old_ref, out_ref in ((kn_pad, kct_ref, kco_ref), (vn_pad, vct_ref, vco_ref)):
        for s in range(nb):
            merged = jnp.where(lane < t_in, new_pad[s].T, old_ref[s])
            out_ref[s] = pltpu.roll(merged, WINDOW - t_in, axis=1)


def _swa_sample(prm, q, kn, vn, kct, vct, nb):
    nseq, t_in, _ = q.shape
    assert t_in <= SUBLANES
    cur = lambda n: pl.BlockSpec((nb, t_in, n), lambda o, i: (o, 0, 0))
    win = pl.BlockSpec((nb, DKV_A, WINDOW), lambda o, i: (o, 0, 0))
    return pl.pallas_call(
        functools.partial(_swa_sample_body, t_in, nb),
        out_shape=[jax.ShapeDtypeStruct((nseq, t_in, DQ_A), F32),
                   jax.ShapeDtypeStruct((nseq, DKV_A, WINDOW), F32),
                   jax.ShapeDtypeStruct((nseq, DKV_A, WINDOW), F32)],
        grid=(nseq // nb, 1),
        in_specs=[pl.BlockSpec(memory_space=pltpu.SMEM), cur(DQ_A), cur(DKV_A), cur(DKV_A), win, win],
        out_specs=[cur(DQ_A), win, win],
        scratch_shapes=[pltpu.VMEM((NH_A * SUBLANES, WINDOW), F32),
                        pltpu.VMEM((nb, SUBLANES, DQ_A), F32),
                        pltpu.VMEM((nb, WINDOW, DKV_A), F32),
                        pltpu.VMEM((nb, WINDOW, DKV_A), F32)],
        compiler_params=_params(2),
        name="swa_sample",
    )(prm, q, kn, vn, kct, vct)


def _merge_body(alpha, x_ref, sh_ref, sc_ref, g1_ref, hm_ref, ha_ref, wog_ref, bog_ref,
                wgg_ref, bgg_ref, wbm_ref, wba_ref, wo_ref, lg_ref, lb_ref, o_ref):
    nsub = 2
    sub = x_ref.shape[0] // nsub
    rows = [pl.ds(r * sub, sub) for r in range(nsub)]
    mod = lambda ref, r: ref[...] if ref.shape[0] == 1 else ref[rows[r], :]

    def branch_gate(h, j):
        cols = slice(j * D_MODEL, (j + 1) * D_MODEL)
        return jax.nn.sigmoid(_dot(h, wgg_ref[:, cols]) + bgg_ref[:, cols])

    x = [x_ref[rows[r], :] for r in range(nsub)]
    h = [(_ln(x[r]) * (1.0 + mod(sc_ref, r)) + mod(sh_ref, r)).astype(BF16) for r in range(nsub)]
    og = [jax.nn.sigmoid(_dot(h[r], wog_ref[...]) + bog_ref[...]) for r in range(nsub)]
    hm = [(hm_ref[rows[r], :] * og[r]).astype(BF16) for r in range(nsub)]
    bm = [_dot(hm[r], wbm_ref[...]) for r in range(nsub)]
    gm = [branch_gate(h[r], 0) for r in range(nsub)]
    ba = [_dot(ha_ref[rows[r], :].astype(BF16), wba_ref[...]) for r in range(nsub)]
    ga = [branch_gate(h[r], 1) for r in range(nsub)]
    merged = [(gm[r] * bm[r] + ga[r] * ba[r]).astype(BF16) for r in range(nsub)]
    mo = [_dot(merged[r], wo_ref[...]) for r in range(nsub)]
    for r in range(nsub):
        o_ref[rows[r], :] = _ln(alpha * x[r] + mod(g1_ref, r) * mo[r]) * lg_ref[...] + lb_ref[...]


def _merge(x, sh, sc, g1, hm, ha, w, tm, alpha):
    bsz, seq, _ = x.shape
    mrows = sh.shape[1]
    weights = [w["wog"], w["bog"], w["wgg"], w["bgg"], w["wbm"], w["wba"], w["wo"],
               w["ln1_g"], w["ln1_b"]]
    return pl.pallas_call(
        functools.partial(_merge_body, alpha),
        out_shape=jax.ShapeDtypeStruct((bsz, seq, D_MODEL), F32),
        grid=(bsz, seq // tm),
        in_specs=[_tok_spec(tm, D_MODEL), _mod_spec(mrows, tm), _mod_spec(mrows, tm),
                  _mod_spec(mrows, tm), _tok_spec(tm, DV_M), _tok_spec(tm, DQ_A)]
                 + [_const_spec(a.shape) for a in weights],
        out_specs=_tok_spec(tm, D_MODEL),
        compiler_params=_params(2),
        name="merge",
    )(x, sh, sc, g1, hm, ha, *weights)


def _ffn_body(alpha, tm, stride, halo, x_ref, sh_ref, sc_ref, g2_ref, cb0_ref, wup_ref, bup_ref,
              cw_ref, cbias_ref, wdn_ref, bdn_ref, lg_ref, lb_ref, o_ref, cs_ref, ubuf, act):
    @pl.when(pl.program_id(1) == 0)
    def _():
        cs_ref[...] = cb0_ref[...]

    x = x_ref[...]
    h = (_ln(x) * (1.0 + sc_ref[...]) + sh_ref[...]).astype(BF16)
    ys = []
    for half in range(2):
        cols = slice(half * D_FF, (half + 1) * D_FF)
        u = _dot(h, wup_ref[:, cols]) + bup_ref[:, cols]
        ubuf[pl.ds(0, halo), :] = cs_ref[:, cols]
        ubuf[pl.ds(halo, tm), :] = u
        cs_ref[:, cols] = ubuf[pl.ds(tm, halo), :]
        y = cbias_ref[:, cols] + u * cw_ref[CONV_W - 1:CONV_W, cols]
        for j in range(CONV_W - 1):
            y = y + ubuf[pl.ds(halo - (CONV_W - 1 - j) * stride, tm), :] * cw_ref[j:j + 1, cols]
        ys.append(y)
    act[...] = (jax.nn.gelu(ys[0]) * ys[1]).astype(BF16)
    f = _dot(act[...], wdn_ref[...]) + bdn_ref[...]
    o_ref[...] = _ln(alpha * x + g2_ref[...] * f) * lg_ref[...] + lb_ref[...]


def _ffn(x, sh, sc, g2, cb0, w, tm, stride, alpha):
    bsz, seq, _ = x.shape
    mrows = sh.shape[1]
    halo = cb0.shape[1]
    cs = pl.BlockSpec((None, halo, 2 * D_FF), lambda b, i: (b, 0, 0))
    weights = [w["wup"], w["bup"], w["cw"], w["cbias"], w["wdn"], w["bdn"], w["ln2_g"], w["ln2_b"]]
    return pl.pallas_call(
        functools.partial(_ffn_body, alpha, tm, stride, halo),
        out_shape=[jax.ShapeDtypeStruct((bsz, seq, D_MODEL), F32),
                   jax.ShapeDtypeStruct((bsz, halo, 2 * D_FF), F32)],
        grid=(bsz, seq // tm),
        in_specs=[_tok_spec(tm, D_MODEL), _mod_spec(mrows, tm), _mod_spec(mrows, tm),
                  _mod_spec(mrows, tm), cs] + [_const_spec(a.shape) for a in weights],
        out_specs=[_tok_spec(tm, D_MODEL), cs],
        scratch_shapes=[pltpu.VMEM((halo + tm, D_FF), F32), pltpu.VMEM((tm, D_FF), BF16)],
        compiler_params=_params(2),
        name="ffn",
    )(x, sh, sc, g2, cb0, *weights)


_O_GATE = 2 * DQK_M + DV_M
_O_OG = _O_GATE + 2 * NH_M
_O_QA = _O_OG + DV_M
_O_KA = _O_QA + DQ_A
_O_VA = _O_KA + DKV_A
_O_GM = _O_VA + DKV_A
_PREP_ROWS = 128


def _swap_halves(a):
    parts = []
    for c in range(a.shape[-1] // LANES):
        parts += [a[..., c * LANES + HD_A:(c + 1) * LANES], a[..., c * LANES:c * LANES + HD_A]]
    return jnp.concatenate(parts, axis=-1)


def _split_w_in_body(w_ref, qk_ref, vm_ref, vmt_ref, g_ref, og_ref, qa_ref, kk_ref, v_ref, vt_ref, gg_ref):
    qk_ref[...] = w_ref[:, :2 * DQK_M].astype(BF16)
    vm = w_ref[:, 2 * DQK_M:_O_GATE]
    vm_ref[...] = vm.astype(BF16)
    vmt_ref[...] = vm.T.astype(BF16)
    pad = jnp.zeros((_PREP_ROWS, LANES - 2 * NH_M), F32)
    g_ref[...] = jnp.concatenate([w_ref[:, _O_GATE:_O_OG], pad], axis=1).astype(BF16)
    og_ref[...] = w_ref[:, _O_OG:_O_QA].astype(BF16)
    qa_ref[...] = w_ref[:, _O_QA:_O_KA].astype(BF16)
    k = w_ref[:, _O_KA:_O_VA]
    kk_ref[...] = jnp.concatenate([k, _swap_halves(k)], axis=1).astype(BF16)
    v = w_ref[:, _O_VA:_O_GM]
    v_ref[...] = v.astype(BF16)
    vt_ref[...] = v.T.astype(BF16)
    gg_ref[...] = w_ref[:, _O_GM:].astype(BF16)


def _split_w_in(w_in):
    d_in = w_in.shape[1]
    rows = lambda n: ((D_MODEL, n), pl.BlockSpec((_PREP_ROWS, n), lambda i: (i, 0)))
    cols = lambda n: ((n, D_MODEL), pl.BlockSpec((n, _PREP_ROWS), lambda i: (0, i)))
    outs = dict(wqk=rows(2 * DQK_M), wvm=rows(DV_M), wvm_t=cols(DV_M), wg=rows(LANES), wog=rows(DV_M),
                wqa=rows(DQ_A), wkk=rows(2 * DKV_A), wv=rows(DKV_A), wv_t=cols(DKV_A),
                wgg=rows(d_in - _O_GM))
    res = pl.pallas_call(
        _split_w_in_body,
        out_shape=[jax.ShapeDtypeStruct(s, BF16) for s, _ in outs.values()],
        grid=(D_MODEL // _PREP_ROWS,),
        in_specs=[pl.BlockSpec((_PREP_ROWS, d_in), lambda i: (i, 0))],
        out_specs=[spec for _, spec in outs.values()],
        compiler_params=_params(1),
        name="split_w_in",
    )(w_in)
    return dict(zip(outs.keys(), res))


def _prep_weights(w_in, b_in, mlstm_norm_w, w_branch_m, w_branch_a, w_out, ln1_g, ln1_b,
                  w_up, b_up, conv_w, conv_b, w_down, b_down, ln2_g, ln2_b):
    row = lambda a: a.reshape(1, -1)
    gate_pad = LANES - 2 * NH_M
    b_k, b_v = b_in[_O_KA:_O_VA], b_in[_O_VA:_O_GM]
    return dict(
        **_split_w_in(w_in),
        bqk=row(b_in[:2 * DQK_M]),
        bvm_row=row(b_in[2 * DQK_M:_O_GATE]), bvm_col=b_in[2 * DQK_M:_O_GATE].reshape(-1, 1),
        bg=row(jnp.pad(b_in[_O_GATE:_O_OG], (0, gate_pad))),
        bqa=row(b_in[_O_QA:_O_KA]),
        bkk=row(jnp.concatenate([b_k, _swap_halves(b_k)])),
        bv_row=row(b_v), bv_col=b_v.reshape(-1, 1),
        bog=row(b_in[_O_OG:_O_QA]), bgg=row(b_in[_O_GM:]),
        norm_w=row(mlstm_norm_w), norm_w_col=mlstm_norm_w.reshape(-1, 1),
        wbm=w_branch_m.astype(BF16), wba=w_branch_a.astype(BF16),
        wo=w_out.astype(BF16), ln1_g=row(ln1_g), ln1_b=row(ln1_b),
        wup=w_up.astype(BF16), bup=row(b_up), cw=conv_w, cbias=row(conv_b),
        wdn=w_down.astype(BF16), bdn=row(b_down), ln2_g=row(ln2_g), ln2_b=row(ln2_b))


def kernel(x_prompt, x_sample, c_prompt, c_sample, state_mlstm_C, state_mlstm_n, state_mlstm_m,
           cache_k_win, cache_v_win, state_ffn_conv, w_ada, b_ada, w_in, b_in, mlstm_norm_w,
           attn_sinks, w_branch_m, w_branch_a, w_out, ln1_g, ln1_b, w_up, b_up, conv_w, conv_b,
           w_down, b_down, ln2_g, ln2_b):
    depth = w_in.shape[0]
    bp, lp, _ = x_prompt.shape
    bs, ls, _ = x_sample.shape
    assert cache_k_win.shape[2] == WINDOW
    alpha = (2 * depth) ** 0.25
    dt = x_prompt.dtype
    slopes = jnp.exp2(-8.0 * jnp.arange(1, NH_A + 1, dtype=F32) / NH_A)
    tm_p = 512
    ns = bs * ls

    yp = x_prompt
    ys = x_sample.reshape(1, ns, D_MODEL)
    new_p, new_s = [], []
    n_c = bp + bs
    c_rows = -(-n_c // SUBLANES) * SUBLANES
    c_all = jnp.concatenate([c_sample, c_prompt, jnp.zeros((c_rows - n_c, D_MODEL), dt)], axis=0)
    for l in range(depth):
        w = _prep_weights(w_in[l], b_in[l], mlstm_norm_w[l], w_branch_m[l], w_branch_a[l], w_out[l],
                          ln1_g[l], ln1_b[l], w_up[l], b_up[l], conv_w[l], conv_b[l], w_down[l],
                          b_down[l], ln2_g[l], ln2_b[l])
        prm = jnp.stack([slopes, attn_sinks[l].astype(F32)])
        mod = _ada(c_all, w_ada[l], b_ada[l])
        mod_s = mod[:bs].reshape(bs, N_MOD, D_MODEL)
        mod_p = mod[bs:bs + bp].reshape(bp, N_MOD, D_MODEL)
        mp_ = [mod_p[:, j:j + 1] for j in range(N_MOD)]
        ms_seq = [jnp.repeat(mod_s[:, j], ls, axis=0)[None] for j in range(3)]
        ms_pos = [jnp.tile(mod_s[:, j], (ls, 1))[None] for j in range(3, N_MOD)]

        qk, v, g, qa, kk, ka, vat = _inproj(yp, mp_[0], mp_[1], w, tm_p, True)
        hm, cp, np_, mp = _mlstm_prompt(qk, v, g, jnp.zeros((bp, NH_M, DHV_M, DHK_M), dt),
                                        jnp.zeros((bp, NH_M, DHK_M), dt), jnp.zeros((bp, 1, LANES), dt),
                                        w["norm_w_col"])
        ha = _swa_prompt(prm, qa, kk, vat)
        x1p = _merge(yp, mp_[0], mp_[1], mp_[2], hm, ha, w, tm_p, alpha)
        halo_p = SUBLANES
        yp, csp = _ffn(x1p, mp_[3], mp_[4], mp_[5], jnp.zeros((bp, halo_p, 2 * D_FF), dt),
                       w, tm_p, 1, alpha)
        p_k = ka[:, lp - WINDOW:].reshape(bp, WINDOW, NKV_A, HD_A)
        p_v = vat[:, :, lp - WINDOW:].reshape(bp, NKV_A, HD_A, WINDOW).transpose(0, 3, 1, 2)
        new_p.append((cp, np_, mp[:, 0, :NH_M], p_k, p_v, csp[:, halo_p - (CONV_W - 1):]))

        qk, v, g, qa, kn, vn = _inproj(ys, ms_seq[0], ms_seq[1], w, ns, False)
        per_seq = lambda a: a.reshape(bs, ls, a.shape[-1])
        m0 = jnp.pad(state_mlstm_m[l], ((0, 0), (0, LANES - NH_M)))[:, None, :]
        hm, cs_, ns_, ms = _mlstm_sample(per_seq(qk), per_seq(v), per_seq(g), state_mlstm_C[l],
                                         state_mlstm_n[l], m0, w["norm_w"], SEQS_PER_STEP)
        to_t = lambda a: a.transpose(0, 2, 3, 1).reshape(bs, DKV_A, WINDOW)
        from_t = lambda a: a.reshape(bs, NKV_A, HD_A, WINDOW).transpose(0, 3, 1, 2)
        ha, kct, vct = _swa_sample(prm, per_seq(qa), per_seq(kn), per_seq(vn),
                                   to_t(cache_k_win[l]), to_t(cache_v_win[l]), SEQS_PER_STEP)
        x1s = _merge(ys, ms_seq[0], ms_seq[1], ms_seq[2], hm.reshape(1, ns, DV_M),
                     ha.reshape(1, ns, DQ_A), w, ns, alpha)
        to_pos = lambda a: a.reshape(bs, -1, a.shape[-1]).transpose(1, 0, 2).reshape(1, -1, a.shape[-1])
        y_pos, css = _ffn(to_pos(x1s[0]), ms_pos[0], ms_pos[1], ms_pos[2], to_pos(state_ffn_conv[l]),
                          w, ns, bs, alpha)
        ys = y_pos.reshape(ls, bs, D_MODEL).transpose(1, 0, 2).reshape(1, ns, D_MODEL)
        conv_s = css.reshape(CONV_W - 1, bs, 2 * D_FF).transpose(1, 0, 2)
        new_s.append((cs_, ns_, ms[:, 0, :NH_M], from_t(kct), from_t(vct), conv_s))

    p_state = [jnp.stack(a) for a in zip(*new_p)]
    s_state = [jnp.stack(a) for a in zip(*new_s)]
    return (yp, ys.reshape(bs, ls, D_MODEL), *p_state, *s_state)
```

```python
import functools

import jax
import jax.numpy as jnp
from jax import lax
from jax.experimental import pallas as pl
from jax.experimental.pallas import tpu as pltpu

F32 = jnp.float32
BF16 = jnp.bfloat16

D_MODEL = 1024
NH_M, DHK_M, DHV_M = 4, 128, 256
DQK_M, DV_M = NH_M * DHK_M, NH_M * DHV_M
NH_A, NKV_A, HD_A = 16, 4, 64
GROUP_A = NH_A // NKV_A
WINDOW = 128
DQ_A, DKV_A = NH_A * HD_A, NKV_A * HD_A
D_FF = 2816
CONV_W = 3
N_MOD = 6
LN_EPS = 1e-5
CHUNK = 128
NEG = -1e30
LANES = 128
SUBLANES = 8
VMEM_LIMIT = 56 * 1024 * 1024
SEQS_PER_STEP = 8


def _ln(x):
    mu = jnp.mean(x, axis=-1, keepdims=True)
    xc = x - mu
    var = jnp.mean(xc * xc, axis=-1, keepdims=True)
    return xc * lax.rsqrt(var + LN_EPS)


def _dot(a, b):
    return jnp.dot(a, b, preferred_element_type=F32)


def _dot_nt(a, b):
    return lax.dot_general(a, b, (((1,), (1,)), ((), ())), preferred_element_type=F32)


def _dot_tn(a, b):
    return lax.dot_general(a, b, (((0,), (0,)), ((), ())), preferred_element_type=F32)


def _const_spec(shape):
    nd = len(shape)
    return pl.BlockSpec(shape, lambda *_: (0,) * nd, pipeline_mode=pl.Buffered(1))


def _params(n_grid):
    return pltpu.CompilerParams(dimension_semantics=("arbitrary",) * n_grid,
                                vmem_limit_bytes=VMEM_LIMIT)


def _first_step():
    return (pl.program_id(0) == 0) & (pl.program_id(1) == 0)


def _ada_body(c_ref, w_ref, b_ref, o_ref):
    c = c_ref[...]
    s = (c * jax.nn.sigmoid(c)).astype(BF16)
    o_ref[...] = _dot(s, w_ref[...].astype(BF16)) + b_ref[...]


def _ada(c, w_ada, b_ada):
    rows = c.shape[0]
    n_out = w_ada.shape[1]
    bn = 512
    return pl.pallas_call(
        _ada_body,
        out_shape=jax.ShapeDtypeStruct((rows, n_out), F32),
        grid=(n_out // bn,),
        in_specs=[pl.BlockSpec((rows, D_MODEL), lambda j: (0, 0)),
                  pl.BlockSpec((D_MODEL, bn), lambda j: (0, j)),
                  pl.BlockSpec((1, bn), lambda j: (0, j))],
        out_specs=pl.BlockSpec((rows, bn), lambda j: (0, j)),
        compiler_params=_params(1),
        name="ada",
    )(c, w_ada, b_ada.reshape(1, n_out))


def _inproj_body(prompt, x_ref, sh_ref, sc_ref, wqk_ref, bqk_ref, wvm_ref, bvm_ref, wg_ref, bg_ref,
                 wqa_ref, bqa_ref, wkk_ref, bkk_ref, wv_ref, bv_ref, qk_ref, v_ref, g_ref, qa_ref, *rest):
    h = (_ln(x_ref[...]) * (1.0 + sc_ref[...]) + sh_ref[...]).astype(BF16)

    def proj(w_ref, b_ref, lo, n):
        return _dot(h, w_ref[:, lo:lo + n]) + b_ref[:, lo:lo + n]

    qk_ref[:, :DQK_M] = (proj(wqk_ref, bqk_ref, 0, DQK_M) * DHK_M ** -0.5).astype(qk_ref.dtype)
    qk_ref[:, DQK_M:] = proj(wqk_ref, bqk_ref, DQK_M, DQK_M).astype(qk_ref.dtype)
    if prompt:
        v_ref[...] = (_dot_nt(wvm_ref[...], h) + bvm_ref[...]).astype(v_ref.dtype)
    else:
        v_ref[...] = proj(wvm_ref, bvm_ref, 0, DV_M).astype(v_ref.dtype)
    g_ref[...] = proj(wg_ref, bg_ref, 0, LANES)
    qa_ref[...] = (proj(wqa_ref, bqa_ref, 0, DQ_A) * HD_A ** -0.5).astype(qa_ref.dtype)
    kk = proj(wkk_ref, bkk_ref, 0, 2 * DKV_A)
    if prompt:
        kk_ref, ka_ref, vat_ref = rest
        kk_ref[...] = kk.astype(BF16)
        ka_ref[...] = kk[:, :DKV_A]
        vat_ref[...] = _dot_nt(wv_ref[...], h) + bv_ref[...]
    else:
        ka_ref, va_ref = rest
        ka_ref[...] = kk[:, :DKV_A]
        va_ref[...] = _dot(h, wv_ref[...]) + bv_ref[...]


def _tok_spec(tm, n):
    return pl.BlockSpec((None, tm, n), lambda b, i: (b, i, 0))


def _mod_spec(rows, tm):
    if rows == 1:
        return pl.BlockSpec((None, 1, D_MODEL), lambda b, i: (b, 0, 0))
    return pl.BlockSpec((None, tm, D_MODEL), lambda b, i: (b, i, 0))


def _inproj(x, sh, sc, w, tm, prompt):
    bsz, seq, _ = x.shape
    mrows = sh.shape[1]
    act = BF16 if prompt else F32
    t_spec = lambda n: pl.BlockSpec((None, n, tm), lambda b, i: (b, 0, i))
    outs = [((bsz, seq, 2 * DQK_M), act, _tok_spec(tm, 2 * DQK_M)),
            ((bsz, DV_M, seq), act, t_spec(DV_M)) if prompt else ((bsz, seq, DV_M), act, _tok_spec(tm, DV_M)),
            ((bsz, seq, LANES), F32, _tok_spec(tm, LANES)),
            ((bsz, seq, DQ_A), act, _tok_spec(tm, DQ_A))]
    if prompt:
        wvm, bvm, wv, bv = w["wvm_t"], w["bvm_col"], w["wv_t"], w["bv_col"]
        outs += [((bsz, seq, 2 * DKV_A), BF16, _tok_spec(tm, 2 * DKV_A)),
                 ((bsz, seq, DKV_A), F32, _tok_spec(tm, DKV_A)),
                 ((bsz, DKV_A, seq), F32, t_spec(DKV_A))]
    else:
        wvm, bvm, wv, bv = w["wvm"], w["bvm_row"], w["wv"], w["bv_row"]
        outs += [((bsz, seq, DKV_A), F32, _tok_spec(tm, DKV_A)),
                 ((bsz, seq, DKV_A), F32, _tok_spec(tm, DKV_A))]
    weights = [w["wqk"], w["bqk"], wvm, bvm, w["wg"], w["bg"], w["wqa"], w["bqa"], w["wkk"], w["bkk"],
               wv, bv]
    return pl.pallas_call(
        functools.partial(_inproj_body, prompt),
        out_shape=[jax.ShapeDtypeStruct(s, dt) for s, dt, _ in outs],
        grid=(bsz, seq // tm),
        in_specs=[_tok_spec(tm, D_MODEL), _mod_spec(mrows, tm), _mod_spec(mrows, tm)]
                 + [_const_spec(a.shape) for a in weights],
        out_specs=[spec for _, _, spec in outs],
        compiler_params=_params(2),
        name="inproj",
    )(x, sh, sc, *weights)


def _scan_rows(x, op, rows):
    row = lax.broadcasted_iota(jnp.int32, x.shape, 0)
    d = 1
    while d < rows:
        shifted = pltpu.roll(x, d, axis=0)
        x = jnp.where(row >= d, op(x, shifted), x)
        d *= 2
    return x


def _mlstm_sample_body(t_in, nb, qk_ref, v_ref, g_ref, c0_ref, n0_ref, m0_ref, nw_ref,
                       h_ref, c_ref, n_ref, m_ref, *pads):
    t = SUBLANES
    c_ref[...] = c0_ref[...]
    n_ref[...] = n0_ref[...]
    m_ref[...] = m0_ref[...]

    @pl.when(_first_step())
    def _():
        for p in pads:
            p[...] = jnp.zeros(p.shape, p.dtype)
    for src, dst in zip((qk_ref, v_ref, g_ref), pads):
        dst[:, pl.ds(0, t_in), :] = src[...]
    qk_src, v_src, g_src = pads

    row = lax.broadcasted_iota(jnp.int32, (t, LANES), 0)
    lane = lax.broadcasted_iota(jnp.int32, (t, LANES), 1)
    lane1 = lax.broadcasted_iota(jnp.int32, (1, LANES), 1)
    r2 = lax.broadcasted_iota(jnp.int32, (t, t), 0)
    c2 = lax.broadcasted_iota(jnp.int32, (t, t), 1)
    causal = c2 <= r2
    pad_gate = jnp.where(lane < NH_M, NEG, -NEG)

    seqs = []
    for s in range(nb):
        ga = _gate_algebra(jnp.where(row < t_in, g_src[s], pad_gate), m_ref[s], t)
        ga["gt"] = ga["gg"].T
        m_ref[s] = jnp.where(lane1 < NH_M, ga["m_new"], 0.0)
        seqs.append(ga)

    units = [(s, hd) for s in range(nb) for hd in range(NH_M)]
    nu = len(units)
    col = lambda name, s, hd: seqs[s][name][:, hd:hd + 1]
    q = [qk_src[s, :, hd * DHK_M:(hd + 1) * DHK_M].astype(BF16) for s, hd in units]
    k = [qk_src[s, :, DQK_M + hd * DHK_M:DQK_M + (hd + 1) * DHK_M].astype(BF16) for s, hd in units]
    v = [v_src[s, :, hd * DHV_M:(hd + 1) * DHV_M].astype(BF16) for s, hd in units]
    c_old = [c_ref[s, hd] for s, hd in units]
    n_old = [n_ref[s, hd:hd + 1, :] for s, hd in units]

    sqk = [_dot_nt(q[u], k[u]) for u in range(nu)]
    dexp = [jnp.exp(jnp.where(causal, col("eb", s, hd) + seqs[s]["gt"][hd:hd + 1, :], NEG))
            for s, hd in units]
    smat = [sqk[u] * dexp[u] for u in range(nu)]
    intra = [_dot(smat[u].astype(BF16), v[u]) for u in range(nu)]
    inter = [_dot_nt(q[u], c_old[u].astype(BF16)) for u in range(nu)]
    qn = [jnp.sum(q[u].astype(F32) * n_old[u].astype(BF16).astype(F32), axis=-1, keepdims=True)
          for u in range(nu)]
    den = [jnp.sum(smat[u], axis=-1, keepdims=True) + col("a_in", s, hd) * qn[u]
           for u, (s, hd) in enumerate(units)]
    hh = [(intra[u] + col("a_in", s, hd) * inter[u])
          / jnp.maximum(jnp.abs(den[u]), col("lowb", s, hd)) for u, (s, hd) in enumerate(units)]
    hn = [_ln(hh[u]) * nw_ref[:, hd * DHV_M:(hd + 1) * DHV_M] for u, (s, hd) in enumerate(units)]
    for u, (s, hd) in enumerate(units):
        h_ref[s, :, hd * DHV_M:(hd + 1) * DHV_M] = hn[u][:t_in]

    kw = [k[u].astype(F32) * col("ws", s, hd) for u, (s, hd) in enumerate(units)]
    upd = [_dot_tn(v[u], kw[u].astype(BF16)) for u in range(nu)]
    for u, (s, hd) in enumerate(units):
        dec = seqs[s]["decay"][:, hd:hd + 1]
        c_ref[s, hd] = dec * c_old[u] + upd[u]
        n_ref[s, hd:hd + 1, :] = dec * n_old[u] + jnp.sum(kw[u], axis=0, keepdims=True)


def _mlstm_sample(qk, v, g, c0, n0, m0, norm_w, nb):
    nseq, t_in, _ = qk.shape
    assert t_in <= SUBLANES
    blk = lambda n: pl.BlockSpec((nb, t_in, n), lambda o, c: (o, 0, 0))
    st_c = pl.BlockSpec((nb, NH_M, DHV_M, DHK_M), lambda o, c: (o, 0, 0, 0))
    st_n = pl.BlockSpec((nb, NH_M, DHK_M), lambda o, c: (o, 0, 0))
    st_m = pl.BlockSpec((nb, 1, LANES), lambda o, c: (o, 0, 0))
    return pl.pallas_call(
        functools.partial(_mlstm_sample_body, t_in, nb),
        out_shape=[jax.ShapeDtypeStruct((nseq, t_in, DV_M), F32),
                   jax.ShapeDtypeStruct((nseq, NH_M, DHV_M, DHK_M), F32),
                   jax.ShapeDtypeStruct((nseq, NH_M, DHK_M), F32),
                   jax.ShapeDtypeStruct((nseq, 1, LANES), F32)],
        grid=(nseq // nb, 1),
        in_specs=[blk(2 * DQK_M), blk(DV_M), blk(LANES), st_c, st_n, st_m,
                  pl.BlockSpec((1, DV_M), lambda o, c: (0, 0))],
        out_specs=[blk(DV_M), st_c, st_n, st_m],
        scratch_shapes=[pltpu.VMEM((nb, SUBLANES, 2 * DQK_M), F32), pltpu.VMEM((nb, SUBLANES, DV_M), F32),
                        pltpu.VMEM((nb, SUBLANES, LANES), F32)],
        compiler_params=_params(2),
        name="mlstm_sample",
    )(qk, v, g, c0, n0, m0, norm_w)


def _gate_algebra(g, m_prev, t):
    b = pltpu.roll(_scan_rows(jax.nn.log_sigmoid(g), jnp.add, t), LANES - NH_M, axis=1)
    gg = g - b
    gmx = _scan_rows(gg, jnp.maximum, t)
    a = b + m_prev
    mt = jnp.maximum(a, b + gmx)
    b_last, gmx_last = b[t - 1:t, :], gmx[t - 1:t, :]
    m_new = jnp.maximum(b_last + m_prev, b_last + gmx_last)
    return dict(gg=gg, eb=b - mt, a_in=jnp.exp(a - mt), lowb=jnp.exp(-mt),
                ws=jnp.exp(b_last + gg - m_new), decay=jnp.exp(b_last + m_prev - m_new), m_new=m_new)


MLSTM_CHUNKS_PER_STEP = 4


def _mlstm_prompt_body(nb, qk_ref, vt_ref, g_ref, c0_ref, n0_ref, m0_ref, nwc_ref,
                       h_ref, c_ref, n_ref, m_ref, nwb):
    @pl.when(pl.program_id(1) == 0)
    def _():
        c_ref[...] = c0_ref[...]
        n_ref[...] = n0_ref[...]
        m_ref[...] = m0_ref[...]

    @pl.when(_first_step())
    def _():
        for hd in range(NH_M):
            nwb[hd] = jnp.broadcast_to(nwc_ref[hd * DHV_M:(hd + 1) * DHV_M, :], (DHV_M, CHUNK))

    for ci in range(MLSTM_CHUNKS_PER_STEP):
        _mlstm_prompt_chunk(nb, slice(ci * CHUNK, (ci + 1) * CHUNK), qk_ref, vt_ref, g_ref, nwb,
                            h_ref, c_ref, n_ref, m_ref)


def _mlstm_prompt_chunk(nb, rows, qk_ref, vt_ref, g_ref, nwb, h_ref, c_ref, n_ref, m_ref):
    t = CHUNK
    lane1 = lax.broadcasted_iota(jnp.int32, (1, LANES), 1)
    r2 = lax.broadcasted_iota(jnp.int32, (t, t), 0)
    c2 = lax.broadcasted_iota(jnp.int32, (t, t), 1)
    causal = r2 <= c2

    seqs = []
    for s in range(nb):
        ga = _gate_algebra(g_ref[s, rows, :], m_ref[s], t)
        m_ref[s] = jnp.where(lane1 < NH_M, ga["m_new"], 0.0)
        for name in ("eb", "a_in", "lowb", "ws"):
            ga[name + "_t"] = ga[name].T
        seqs.append(ga)

    units = [(s, hd) for s in range(nb) for hd in range(NH_M)]
    nu = len(units)
    rowv = lambda name, s, hd: seqs[s][name + "_t"][hd:hd + 1, :]
    q = [qk_ref[s, rows, hd * DHK_M:(hd + 1) * DHK_M] for s, hd in units]
    k = [qk_ref[s, rows, DQK_M + hd * DHK_M:DQK_M + (hd + 1) * DHK_M] for s, hd in units]
    vt = [vt_ref[s, hd * DHV_M:(hd + 1) * DHV_M, rows] for s, hd in units]
    c_old = [c_ref[s, hd] for s, hd in units]
    n_old = [n_ref[s, hd:hd + 1, :] for s, hd in units]

    skq = [_dot_nt(k[u], q[u]) for u in range(nu)]
    dexp = [jnp.exp(jnp.where(causal, seqs[s]["gg"][:, hd:hd + 1] + rowv("eb", s, hd), NEG))
            for s, hd in units]
    smat = [skq[u] * dexp[u] for u in range(nu)]
    intra = [_dot(vt[u], smat[u].astype(BF16)) for u in range(nu)]
    inter = [_dot_nt(c_old[u].astype(BF16), q[u]) for u in range(nu)]
    qn = [_dot_nt(jnp.broadcast_to(n_old[u], (SUBLANES, DHK_M)).astype(BF16), q[u])[0:1, :]
          for u in range(nu)]
    den = [jnp.sum(smat[u], axis=0, keepdims=True) + rowv("a_in", s, hd) * qn[u]
           for u, (s, hd) in enumerate(units)]
    inv = [1.0 / jnp.maximum(jnp.abs(den[u]), rowv("lowb", s, hd)) for u, (s, hd) in enumerate(units)]
    hh = [(intra[u] + rowv("a_in", s, hd) * inter[u]) * inv[u] for u, (s, hd) in enumerate(units)]
    mu = [jnp.mean(hh[u], axis=0, keepdims=True) for u in range(nu)]
    xc = [hh[u] - mu[u] for u in range(nu)]
    var = [jnp.mean(xc[u] * xc[u], axis=0, keepdims=True) for u in range(nu)]
    hn = [xc[u] * lax.rsqrt(var[u] + LN_EPS) * nwb[hd] for u, (s, hd) in enumerate(units)]
    for u, (s, hd) in enumerate(units):
        h_ref[s, rows, hd * DHV_M:(hd + 1) * DHV_M] = hn[u].T

    vw = [(vt[u].astype(F32) * rowv("ws", s, hd)).astype(BF16) for u, (s, hd) in enumerate(units)]
    upd = [_dot(vw[u], k[u]) for u in range(nu)]
    nupd = [_dot(jnp.broadcast_to(rowv("ws", s, hd), (SUBLANES, t)).astype(BF16), k[u])[0:1, :]
            for u, (s, hd) in enumerate(units)]
    for u, (s, hd) in enumerate(units):
        dec = seqs[s]["decay"][:, hd:hd + 1]
        c_ref[s, hd] = dec * c_old[u] + upd[u]
        n_ref[s, hd:hd + 1, :] = dec * n_old[u] + nupd[u]


def _mlstm_prompt(qk, vt, g, c0, n0, m0, norm_w_col):
    nb, seq, _ = qk.shape
    ts = MLSTM_CHUNKS_PER_STEP * CHUNK
    tok = lambda n: pl.BlockSpec((nb, ts, n), lambda o, c: (0, c, 0))
    st_c = pl.BlockSpec((nb, NH_M, DHV_M, DHK_M), lambda o, c: (0, 0, 0, 0))
    st_n = pl.BlockSpec((nb, NH_M, DHK_M), lambda o, c: (0, 0, 0))
    st_m = pl.BlockSpec((nb, 1, LANES), lambda o, c: (0, 0, 0))
    return pl.pallas_call(
        functools.partial(_mlstm_prompt_body, nb),
        out_shape=[jax.ShapeDtypeStruct((nb, seq, DV_M), F32),
                   jax.ShapeDtypeStruct((nb, NH_M, DHV_M, DHK_M), F32),
                   jax.ShapeDtypeStruct((nb, NH_M, DHK_M), F32),
                   jax.ShapeDtypeStruct((nb, 1, LANES), F32)],
        grid=(1, seq // ts),
        in_specs=[tok(2 * DQK_M), pl.BlockSpec((nb, DV_M, ts), lambda o, c: (0, 0, c)), tok(LANES),
                  st_c, st_n, st_m, pl.BlockSpec((DV_M, 1), lambda o, c: (0, 0))],
        out_specs=[tok(DV_M), st_c, st_n, st_m],
        scratch_shapes=[pltpu.VMEM((NH_M, DHV_M, CHUNK), F32)],
        compiler_params=_params(2),
        name="mlstm_prompt",
    )(qk, vt, g, c0, n0, m0, norm_w_col)


SWA_BLOCKS_PER_STEP = 8


def _swa_prompt_body(prm_ref, q_ref, kc_ref, kp_ref, vc_ref, vp_ref, o_ref, tbl):
    row = lax.broadcasted_iota(jnp.int32, (WINDOW, WINDOW), 0)
    col = lax.broadcasted_iota(jnp.int32, (WINDOW, WINDOW), 1)
    tri = row <= col

    @pl.when(_first_step())
    def _():
        dist = jnp.where(tri, col - row, col - row + WINDOW).astype(F32)
        for hd in range(NH_A):
            bias = prm_ref[0, hd] * dist
            tbl[0, hd] = bias
            tbl[1, hd] = bias + jnp.where(tri, 0.0, -NEG)

    for b in range(SWA_BLOCKS_PER_STEP):
        rows = pl.ds(b * WINDOW, WINDOW)
        which = jnp.where(pl.program_id(1) == 0, 1, 0) if b == 0 else 0
        k_prev = kp_ref if b == 0 else kc_ref.at[pl.ds((b - 1) * WINDOW, WINDOW), :]
        v_prev = vp_ref if b == 0 else vc_ref.at[:, pl.ds((b - 1) * WINDOW, WINDOW)]
        _swa_prompt_block(which, prm_ref, q_ref.at[rows, :], kc_ref.at[rows, :], k_prev,
                          vc_ref.at[:, rows], v_prev, o_ref.at[rows, :], tbl)


def _swa_prompt_block(which, prm_ref, q_ref, kc_ref, kp_ref, vc_ref, vp_ref, o_ref, tbl):
    row = lax.broadcasted_iota(jnp.int32, (WINDOW, WINDOW), 0)
    col = lax.broadcasted_iota(jnp.int32, (WINDOW, WINDOW), 1)
    tri = row <= col
    lo = col < HD_A
    zb = jnp.zeros((WINDOW, LANES), BF16)

    def placed(k_ref, kv):
        c, par = kv // 2, kv % 2
        own = k_ref[:, c * LANES:(c + 1) * LANES]
        swp = k_ref[:, DKV_A + c * LANES:DKV_A + (c + 1) * LANES]
        if par == 0:
            return jnp.where(lo, own, zb), jnp.where(lo, zb, swp)
        return jnp.where(lo, swp, zb), jnp.where(lo, zb, own)

    st = []
    for kv in range(NKV_A):
        lhs = jnp.concatenate([*placed(kc_ref, kv), *placed(kp_ref, kv)], axis=0)
        qg = jnp.concatenate([q_ref[:, (2 * kv) * LANES:(2 * kv + 1) * LANES],
                              q_ref[:, (2 * kv + 1) * LANES:(2 * kv + 2) * LANES]], axis=0)
        st.append(_dot_nt(lhs, qg))
    tiles = []
    for hd in range(NH_A):
        kv, a, par = hd // GROUP_A, (hd % GROUP_A) // 2, hd % 2
        cols = slice(a * WINDOW, (a + 1) * WINDOW)
        tiles.append(jnp.where(tri, st[kv][par * WINDOW:(par + 1) * WINDOW, cols],
                               st[kv][(2 + par) * WINDOW:(3 + par) * WINDOW, cols]))
    sc = jnp.concatenate(tiles, axis=0).reshape(NH_A, WINDOW, WINDOW) - tbl[which]
    sink = jnp.concatenate([jnp.full((1, 1, WINDOW), prm_ref[1, hd], F32) for hd in range(NH_A)], axis=0)
    mx = jnp.maximum(jnp.max(sc, axis=1, keepdims=True), sink)
    p = jnp.exp(sc - mx)
    den = jnp.sum(p, axis=1, keepdims=True) + jnp.exp(sink - mx)
    pn = p * (1.0 / den)

    zero = jnp.zeros((WINDOW, WINDOW), F32)
    z64 = jnp.zeros((HD_A, WINDOW), BF16)
    outs = []
    for kv in range(NKV_A):
        cols = []
        for a in range(2):
            pe, po = pn[kv * GROUP_A + 2 * a], pn[kv * GROUP_A + 2 * a + 1]
            cols.append(jnp.concatenate([jnp.where(tri, pe, zero), jnp.where(tri, po, zero),
                                         jnp.where(tri, zero, pe), jnp.where(tri, zero, po)],
                                        axis=0).astype(BF16))
        pt = jnp.concatenate(cols, axis=1)
        vc = vc_ref[kv * HD_A:(kv + 1) * HD_A, :].astype(BF16)
        vp = vp_ref[kv * HD_A:(kv + 1) * HD_A, :].astype(BF16)
        vt = jnp.concatenate([jnp.concatenate([vc, z64], axis=0), jnp.concatenate([z64, vc], axis=0),
                              jnp.concatenate([vp, z64], axis=0), jnp.concatenate([z64, vp], axis=0)],
                             axis=1)
        ot = _dot(vt, pt)
        outs += [ot[:, :WINDOW].T, ot[:, WINDOW:].T]
    o_ref[...] = jnp.concatenate(outs, axis=1).astype(o_ref.dtype)


def _swa_prompt(prm, q, kk, vat):
    bsz, seq, _ = q.shape
    tq = SWA_BLOCKS_PER_STEP * WINDOW
    prev = lambda i: jnp.maximum(i * SWA_BLOCKS_PER_STEP - 1, 0)
    return pl.pallas_call(
        _swa_prompt_body,
        out_shape=jax.ShapeDtypeStruct((bsz, seq, DQ_A), BF16),
        grid=(bsz, seq // tq),
        in_specs=[pl.BlockSpec(memory_space=pltpu.SMEM),
                  pl.BlockSpec((None, tq, DQ_A), lambda b, i: (b, i, 0)),
                  pl.BlockSpec((None, tq, 2 * DKV_A), lambda b, i: (b, i, 0)),
                  pl.BlockSpec((None, WINDOW, 2 * DKV_A), lambda b, i: (b, prev(i), 0)),
                  pl.BlockSpec((None, DKV_A, tq), lambda b, i: (b, 0, i)),
                  pl.BlockSpec((None, DKV_A, WINDOW), lambda b, i: (b, 0, prev(i)))],
        out_specs=pl.BlockSpec((None, tq, DQ_A), lambda b, i: (b, i, 0)),
        scratch_shapes=[pltpu.VMEM((2, NH_A, WINDOW, WINDOW), F32)],
        compiler_params=_params(2),
        name="swa_prompt",
    )(prm, q, kk, kk, vat, vat)


def _half_mask(shape, half):
    lane = lax.broadcasted_iota(jnp.int32, shape, 1)
    return lane < HD_A if half == 0 else lane >= HD_A


def _swa_sample_body(t_in, nb, prm_ref, q_ref, kn_ref, vn_ref, kct_ref, vct_ref,
                     o_ref, kco_ref, vco_ref, tbl, q8, kn_pad, vn_pad):
    tq = SUBLANES
    rows = NH_A * tq
    row = lax.broadcasted_iota(jnp.int32, (rows, WINDOW), 0)
    col = lax.broadcasted_iota(jnp.int32, (rows, WINDOW), 1)
    tri = col <= (row & (tq - 1))

    @pl.when(_first_step())
    def _():
        r8 = lax.broadcasted_iota(jnp.int32, (tq, WINDOW), 0)
        c8 = lax.broadcasted_iota(jnp.int32, (tq, WINDOW), 1)
        dist = jnp.where(c8 <= r8, r8 - c8, r8 - c8 + WINDOW).astype(F32)
        for hd in range(NH_A):
            tbl[pl.ds(hd * tq, tq), :] = prm_ref[0, hd] * dist
        for p in (q8, kn_pad, vn_pad):
            p[...] = jnp.zeros(p.shape, p.dtype)

    q8[:, pl.ds(0, t_in), :] = q_ref[...]
    kn_pad[:, pl.ds(0, t_in), :] = kn_ref[...]
    vn_pad[:, pl.ds(0, t_in), :] = vn_ref[...]
    sink_col = jnp.concatenate([jnp.full((tq, 1), prm_ref[1, hd], F32) for hd in range(NH_A)], axis=0)
    z8 = jnp.zeros((tq, LANES), F32)
    n_chunk = NKV_A // 2
    heads_per_chunk = NH_A // n_chunk

    def place(piece, src_half, dst_half):
        if src_half != dst_half:
            piece = pltpu.roll(piece, HD_A, axis=1)
        return jnp.where(_half_mask(piece.shape, dst_half), piece, z8)

    qexp = []
    for s in range(nb):
        per_c = []
        for c in range(n_chunk):
            pieces = []
            for hl in range(heads_per_chunk):
                hd = c * heads_per_chunk + hl
                pieces.append(place(q8[s, :, (hd // 2) * LANES:(hd // 2 + 1) * LANES],
                                    hd % 2, hl // GROUP_A))
            per_c.append(jnp.concatenate(pieces, axis=0).astype(BF16))
        qexp.append(per_c)
    csl = lambda c: slice(c * LANES, (c + 1) * LANES)
    s_prev = [[_dot(qexp[s][c], kct_ref[s, csl(c), :].astype(BF16)) for c in range(n_chunk)]
              for s in range(nb)]
    s_cur = [[_dot_nt(qexp[s][c], kn_pad[s, :, csl(c)].astype(BF16)) for c in range(n_chunk)]
             for s in range(nb)]
    sc = [jnp.where(tri, jnp.concatenate(s_cur[s], axis=0), jnp.concatenate(s_prev[s], axis=0)) - tbl[...]
          for s in range(nb)]
    mx = [jnp.maximum(jnp.max(sc[s], axis=-1, keepdims=True), sink_col) for s in range(nb)]
    p = [jnp.exp(sc[s] - mx[s]) for s in range(nb)]
    den = [jnp.sum(p[s], axis=-1, keepdims=True) + jnp.exp(sink_col - mx[s]) for s in range(nb)]
    pn = [p[s] * (1.0 / den[s]) for s in range(nb)]
    zero = jnp.zeros((rows, WINDOW), F32)
    pc = [jnp.where(tri, pn[s], zero).astype(BF16) for s in range(nb)]
    pp = [jnp.where(tri, zero, pn[s]).astype(BF16) for s in range(nb)]
    half_rows = heads_per_chunk * tq
    oc = [[_dot_nt(pp[s][c * half_rows:(c + 1) * half_rows], vct_ref[s, csl(c), :].astype(BF16))
           + _dot(pc[s][c * half_rows:(c + 1) * half_rows], vn_pad[s, :, csl(c)].astype(BF16))
           for c in range(n_chunk)] for s in range(nb)]
    for s in range(nb):
        chunks = []
        for pch in range(NH_A // 2):
            acc = None
            for hd in (2 * pch, 2 * pch + 1):
                c, hl = hd // heads_per_chunk, hd % heads_per_chunk
                piece = place(oc[s][c][hl * tq:(hl + 1) * tq, :], hl // GROUP_A, hd % 2)
                acc = piece if acc is None else acc + piece
            chunks.append(acc)
        o_ref[s] = jnp.concatenate(chunks, axis=1)[:t_in]

    lane = lax.broadcasted_iota(jnp.int32, (DKV_A, WINDOW), 1)
    for new_pad, old_ref, out_ref in ((kn_pad, kct_ref, kco_ref), (vn_pad, vct_ref, vco_ref)):
        for s in range(nb):
            merged = jnp.where(lane < t_in, new_pad[s].T, old_ref[s])
            out_ref[s] = pltpu.roll(merged, WINDOW - t_in, axis=1)


def _swa_sample(prm, q, kn, vn, kct, vct, nb):
    nseq, t_in, _ = q.shape
    assert t_in <= SUBLANES
    cur = lambda n: pl.BlockSpec((nb, t_in, n), lambda o, i: (o, 0, 0))
    win = pl.BlockSpec((nb, DKV_A, WINDOW), lambda o, i: (o, 0, 0))
    return pl.pallas_call(
        functools.partial(_swa_sample_body, t_in, nb),
        out_shape=[jax.ShapeDtypeStruct((nseq, t_in, DQ_A), F32),
                   jax.ShapeDtypeStruct((nseq, DKV_A, WINDOW), F32),
                   jax.ShapeDtypeStruct((nseq, DKV_A, WINDOW), F32)],
        grid=(nseq // nb, 1),
        in_specs=[pl.BlockSpec(memory_space=pltpu.SMEM), cur(DQ_A), cur(DKV_A), cur(DKV_A), win, win],
        out_specs=[cur(DQ_A), win, win],
        scratch_shapes=[pltpu.VMEM((NH_A * SUBLANES, WINDOW), F32),
                        pltpu.VMEM((nb, SUBLANES, DQ_A), F32),
                        pltpu.VMEM((nb, WINDOW, DKV_A), F32),
                        pltpu.VMEM((nb, WINDOW, DKV_A), F32)],
        compiler_params=_params(2),
        name="swa_sample",
    )(prm, q, kn, vn, kct, vct)


def _merge_body(alpha, x_ref, sh_ref, sc_ref, g1_ref, hm_ref, ha_ref, wog_ref, bog_ref,
                wgg_ref, bgg_ref, wbm_ref, wba_ref, wo_ref, lg_ref, lb_ref, o_ref):
    nsub = 2
    sub = x_ref.shape[0] // nsub
    rows = [pl.ds(r * sub, sub) for r in range(nsub)]
    mod = lambda ref, r: ref[...] if ref.shape[0] == 1 else ref[rows[r], :]

    def branch_gate(h, j):
        cols = slice(j * D_MODEL, (j + 1) * D_MODEL)
        return jax.nn.sigmoid(_dot(h, wgg_ref[:, cols]) + bgg_ref[:, cols])

    x = [x_ref[rows[r], :] for r in range(nsub)]
    h = [(_ln(x[r]) * (1.0 + mod(sc_ref, r)) + mod(sh_ref, r)).astype(BF16) for r in range(nsub)]
    og = [jax.nn.sigmoid(_dot(h[r], wog_ref[...]) + bog_ref[...]) for r in range(nsub)]
    hm = [(hm_ref[rows[r], :] * og[r]).astype(BF16) for r in range(nsub)]
    bm = [_dot(hm[r], wbm_ref[...]) for r in range(nsub)]
    gm = [branch_gate(h[r], 0) for r in range(nsub)]
    ba = [_dot(ha_ref[rows[r], :].astype(BF16), wba_ref[...]) for r in range(nsub)]
    ga = [branch_gate(h[r], 1) for r in range(nsub)]
    merged = [(gm[r] * bm[r] + ga[r] * ba[r]).astype(BF16) for r in range(nsub)]
    mo = [_dot(merged[r], wo_ref[...]) for r in range(nsub)]
    for r in range(nsub):
        o_ref[rows[r], :] = _ln(alpha * x[r] + mod(g1_ref, r) * mo[r]) * lg_ref[...] + lb_ref[...]


def _merge(x, sh, sc, g1, hm, ha, w, tm, alpha):
    bsz, seq, _ = x.shape
    mrows = sh.shape[1]
    weights = [w["wog"], w["bog"], w["wgg"], w["bgg"], w["wbm"], w["wba"], w["wo"],
               w["ln1_g"], w["ln1_b"]]
    return pl.pallas_call(
        functools.partial(_merge_body, alpha),
        out_shape=jax.ShapeDtypeStruct((bsz, seq, D_MODEL), F32),
        grid=(bsz, seq // tm),
        in_specs=[_tok_spec(tm, D_MODEL), _mod_spec(mrows, tm), _mod_spec(mrows, tm),
                  _mod_spec(mrows, tm), _tok_spec(tm, DV_M), _tok_spec(tm, DQ_A)]
                 + [_const_spec(a.shape) for a in weights],
        out_specs=_tok_spec(tm, D_MODEL),
        compiler_params=_params(2),
        name="merge",
    )(x, sh, sc, g1, hm, ha, *weights)


def _ffn_body(alpha, tm, stride, halo, x_ref, sh_ref, sc_ref, g2_ref, cb0_ref, wup_ref, bup_ref,
              cw_ref, cbias_ref, wdn_ref, bdn_ref, lg_ref, lb_ref, o_ref, cs_ref, ubuf, act):
    @pl.when(pl.program_id(1) == 0)
    def _():
        cs_ref[...] = cb0_ref[...]

    x = x_ref[...]
    h = (_ln(x) * (1.0 + sc_ref[...]) + sh_ref[...]).astype(BF16)
    ys = []
    for half in range(2):
        cols = slice(half * D_FF, (half + 1) * D_FF)
        u = _dot(h, wup_ref[:, cols]) + bup_ref[:, cols]
        ubuf[pl.ds(0, halo), :] = cs_ref[:, cols]
        ubuf[pl.ds(halo, tm), :] = u
        cs_ref[:, cols] = ubuf[pl.ds(tm, halo), :]
        y = cbias_ref[:, cols] + u * cw_ref[CONV_W - 1:CONV_W, cols]
        for j in range(CONV_W - 1):
            y = y + ubuf[pl.ds(halo - (CONV_W - 1 - j) * stride, tm), :] * cw_ref[j:j + 1, cols]
        ys.append(y)
    act[...] = (jax.nn.gelu(ys[0]) * ys[1]).astype(BF16)
    f = _dot(act[...], wdn_ref[...]) + bdn_ref[...]
    o_ref[...] = _ln(alpha * x + g2_ref[...] * f) * lg_ref[...] + lb_ref[...]


def _ffn(x, sh, sc, g2, cb0, w, tm, stride, alpha):
    bsz, seq, _ = x.shape
    mrows = sh.shape[1]
    halo = cb0.shape[1]
    cs = pl.BlockSpec((None, halo, 2 * D_FF), lambda b, i: (b, 0, 0))
    weights = [w["wup"], w["bup"], w["cw"], w["cbias"], w["wdn"], w["bdn"], w["ln2_g"], w["ln2_b"]]
    return pl.pallas_call(
        functools.partial(_ffn_body, alpha, tm, stride, halo),
        out_shape=[jax.ShapeDtypeStruct((bsz, seq, D_MODEL), F32),
                   jax.ShapeDtypeStruct((bsz, halo, 2 * D_FF), F32)],
        grid=(bsz, seq // tm),
        in_specs=[_tok_spec(tm, D_MODEL), _mod_spec(mrows, tm), _mod_spec(mrows, tm),
                  _mod_spec(mrows, tm), cs] + [_const_spec(a.shape) for a in weights],
        out_specs=[_tok_spec(tm, D_MODEL), cs],
        scratch_shapes=[pltpu.VMEM((halo + tm, D_FF), F32), pltpu.VMEM((tm, D_FF), BF16)],
        compiler_params=_params(2),
        name="ffn",
    )(x, sh, sc, g2, cb0, *weights)


_O_GATE = 2 * DQK_M + DV_M
_O_OG = _O_GATE + 2 * NH_M
_O_QA = _O_OG + DV_M
_O_KA = _O_QA + DQ_A
_O_VA = _O_KA + DKV_A
_O_GM = _O_VA + DKV_A
_PREP_ROWS = 128


def _swap_halves(a, axis):
    parts = []
    for c in range(a.shape[axis] // LANES):
        parts += [lax.slice_in_dim(a, c * LANES + HD_A, (c + 1) * LANES, axis=axis),
                  lax.slice_in_dim(a, c * LANES, c * LANES + HD_A, axis=axis)]
    return jnp.concatenate(parts, axis=axis)


def _split_w_in_body(wt_ref, qk_ref, vm_ref, vmt_ref, g_ref, og_ref, qa_ref, kk_ref, v_ref, vt_ref, gg_ref):
    piece = lambda lo, hi: wt_ref[lo:hi, :]
    qk_ref[...] = piece(0, 2 * DQK_M).T.astype(BF16)
    vm = piece(2 * DQK_M, _O_GATE)
    vmt_ref[...] = vm.astype(BF16)
    vm_ref[...] = vm.T.astype(BF16)
    pad = jnp.zeros((LANES - 2 * NH_M, _PREP_ROWS), F32)
    g_ref[...] = jnp.concatenate([piece(_O_GATE, _O_OG), pad], axis=0).T.astype(BF16)
    og_ref[...] = piece(_O_OG, _O_QA).T.astype(BF16)
    qa_ref[...] = piece(_O_QA, _O_KA).T.astype(BF16)
    k = piece(_O_KA, _O_VA)
    kk_ref[...] = jnp.concatenate([k, _swap_halves(k, 0)], axis=0).T.astype(BF16)
    v = piece(_O_VA, _O_GM)
    vt_ref[...] = v.astype(BF16)
    v_ref[...] = v.T.astype(BF16)
    gg_ref[...] = piece(_O_GM, wt_ref.shape[0]).T.astype(BF16)


def _split_w_in(w_in_t):
    d_in = w_in_t.shape[0]
    rows = lambda n: ((D_MODEL, n), pl.BlockSpec((_PREP_ROWS, n), lambda i: (i, 0)))
    cols = lambda n: ((n, D_MODEL), pl.BlockSpec((n, _PREP_ROWS), lambda i: (0, i)))
    outs = dict(wqk=rows(2 * DQK_M), wvm=rows(DV_M), wvm_t=cols(DV_M), wg=rows(LANES), wog=rows(DV_M),
                wqa=rows(DQ_A), wkk=rows(2 * DKV_A), wv=rows(DKV_A), wv_t=cols(DKV_A),
                wgg=rows(d_in - _O_GM))
    res = pl.pallas_call(
        _split_w_in_body,
        out_shape=[jax.ShapeDtypeStruct(s, BF16) for s, _ in outs.values()],
        grid=(D_MODEL // _PREP_ROWS,),
        in_specs=[pl.BlockSpec((d_in, _PREP_ROWS), lambda i: (0, i))],
        out_specs=[spec for _, spec in outs.values()],
        compiler_params=_params(1),
        name="split_w_in",
    )(w_in_t)
    return dict(zip(outs.keys(), res))


def _prep_weights(w_in, b_in, mlstm_norm_w, w_branch_m, w_branch_a, w_out, ln1_g, ln1_b,
                  w_up, b_up, conv_w, conv_b, w_down, b_down, ln2_g, ln2_b):
    row = lambda a: a.reshape(1, -1)
    gate_pad = LANES - 2 * NH_M
    b_k, b_v = b_in[_O_KA:_O_VA], b_in[_O_VA:_O_GM]
    return dict(
        **_split_w_in(w_in.T),
        bqk=row(b_in[:2 * DQK_M]),
        bvm_row=row(b_in[2 * DQK_M:_O_GATE]), bvm_col=b_in[2 * DQK_M:_O_GATE].reshape(-1, 1),
        bg=row(jnp.pad(b_in[_O_GATE:_O_OG], (0, gate_pad))),
        bqa=row(b_in[_O_QA:_O_KA]),
        bkk=row(jnp.concatenate([b_k, _swap_halves(b_k, 0)])),
        bv_row=row(b_v), bv_col=b_v.reshape(-1, 1),
        bog=row(b_in[_O_OG:_O_QA]), bgg=row(b_in[_O_GM:]),
        norm_w=row(mlstm_norm_w), norm_w_col=mlstm_norm_w.reshape(-1, 1),
        wbm=w_branch_m.astype(BF16), wba=w_branch_a.astype(BF16),
        wo=w_out.astype(BF16), ln1_g=row(ln1_g), ln1_b=row(ln1_b),
        wup=w_up.astype(BF16), bup=row(b_up), cw=conv_w, cbias=row(conv_b),
        wdn=w_down.astype(BF16), bdn=row(b_down), ln2_g=row(ln2_g), ln2_b=row(ln2_b))


def kernel(x_prompt, x_sample, c_prompt, c_sample, state_mlstm_C, state_mlstm_n, state_mlstm_m,
           cache_k_win, cache_v_win, state_ffn_conv, w_ada, b_ada, w_in, b_in, mlstm_norm_w,
           attn_sinks, w_branch_m, w_branch_a, w_out, ln1_g, ln1_b, w_up, b_up, conv_w, conv_b,
           w_down, b_down, ln2_g, ln2_b):
    depth = w_in.shape[0]
    bp, lp, _ = x_prompt.shape
    bs, ls, _ = x_sample.shape
    assert cache_k_win.shape[2] == WINDOW
    alpha = (2 * depth) ** 0.25
    dt = x_prompt.dtype
    slopes = jnp.exp2(-8.0 * jnp.arange(1, NH_A + 1, dtype=F32) / NH_A)
    tm_p = 512
    ns = bs * ls

    yp = x_prompt
    ys = x_sample.reshape(1, ns, D_MODEL)
    new_p, new_s = [], []
    n_c = bp + bs
    c_rows = -(-n_c // SUBLANES) * SUBLANES
    c_all = jnp.concatenate([c_sample, c_prompt, jnp.zeros((c_rows - n_c, D_MODEL), dt)], axis=0)
    for l in range(depth):
        w = _prep_weights(w_in[l], b_in[l], mlstm_norm_w[l], w_branch_m[l], w_branch_a[l], w_out[l],
                          ln1_g[l], ln1_b[l], w_up[l], b_up[l], conv_w[l], conv_b[l], w_down[l],
                          b_down[l], ln2_g[l], ln2_b[l])
        prm = jnp.stack([slopes, attn_sinks[l].astype(F32)])
        mod = _ada(c_all, w_ada[l], b_ada[l])
        mod_s = mod[:bs].reshape(bs, N_MOD, D_MODEL)
        mod_p = mod[bs:bs + bp].reshape(bp, N_MOD, D_MODEL)
        mp_ = [mod_p[:, j:j + 1] for j in range(N_MOD)]
        ms_seq = [jnp.repeat(mod_s[:, j], ls, axis=0)[None] for j in range(3)]
        ms_pos = [jnp.tile(mod_s[:, j], (ls, 1))[None] for j in range(3, N_MOD)]

        qk, v, g, qa, kk, ka, vat = _inproj(yp, mp_[0], mp_[1], w, tm_p, True)
        hm, cp, np_, mp = _mlstm_prompt(qk, v, g, jnp.zeros((bp, NH_M, DHV_M, DHK_M), dt),
                                        jnp.zeros((bp, NH_M, DHK_M), dt), jnp.zeros((bp, 1, LANES), dt),
                                        w["norm_w_col"])
        ha = _swa_prompt(prm, qa, kk, vat)
        x1p = _merge(yp, mp_[0], mp_[1], mp_[2], hm, ha, w, tm_p, alpha)
        halo_p = SUBLANES
        yp, csp = _ffn(x1p, mp_[3], mp_[4], mp_[5], jnp.zeros((bp, halo_p, 2 * D_FF), dt),
                       w, tm_p, 1, alpha)
        p_k = ka[:, lp - WINDOW:].reshape(bp, WINDOW, NKV_A, HD_A)
        p_v = vat[:, :, lp - WINDOW:].reshape(bp, NKV_A, HD_A, WINDOW).transpose(0, 3, 1, 2)
        new_p.append((cp, np_, mp[:, 0, :NH_M], p_k, p_v, csp[:, halo_p - (CONV_W - 1):]))

        qk, v, g, qa, kn, vn = _inproj(ys, ms_seq[0], ms_seq[1], w, ns, False)
        per_seq = lambda a: a.reshape(bs, ls, a.shape[-1])
        m0 = jnp.pad(state_mlstm_m[l], ((0, 0), (0, LANES - NH_M)))[:, None, :]
        hm, cs_, ns_, ms = _mlstm_sample(per_seq(qk), per_seq(v), per_seq(g), state_mlstm_C[l],
                                         state_mlstm_n[l], m0, w["norm_w"], SEQS_PER_STEP)
        to_t = lambda a: a.transpose(0, 2, 3, 1).reshape(bs, DKV_A, WINDOW)
        from_t = lambda a: a.reshape(bs, NKV_A, HD_A, WINDOW).transpose(0, 3, 1, 2)
        ha, kct, vct = _swa_sample(prm, per_seq(qa), per_seq(kn), per_seq(vn),
                                   to_t(cache_k_win[l]), to_t(cache_v_win[l]), SEQS_PER_STEP)
        x1s = _merge(ys, ms_seq[0], ms_seq[1], ms_seq[2], hm.reshape(1, ns, DV_M),
                     ha.reshape(1, ns, DQ_A), w, ns, alpha)
        to_pos = lambda a: a.reshape(bs, -1, a.shape[-1]).transpose(1, 0, 2).reshape(1, -1, a.shape[-1])
        y_pos, css = _ffn(to_pos(x1s[0]), ms_pos[0], ms_pos[1], ms_pos[2], to_pos(state_ffn_conv[l]),
                          w, ns, bs, alpha)
        ys = y_pos.reshape(ls, bs, D_MODEL).transpose(1, 0, 2).reshape(1, ns, D_MODEL)
        conv_s = css.reshape(CONV_W - 1, bs, 2 * D_FF).transpose(1, 0, 2)
        new_s.append((cs_, ns_, ms[:, 0, :NH_M], from_t(kct), from_t(vct), conv_s))

    p_state = [jnp.stack(a) for a in zip(*new_p)]
    s_state = [jnp.stack(a) for a in zip(*new_s)]
    return (yp, ys.reshape(bs, ls, D_MODEL), *p_state, *s_state)
```

```python
import functools

import jax
import jax.numpy as jnp
from jax import lax
from jax.experimental import pallas as pl
from jax.experimental.pallas import tpu as pltpu

F32 = jnp.float32
BF16 = jnp.bfloat16

D_MODEL = 1024
NH_M, DHK_M, DHV_M = 4, 128, 256
DQK_M, DV_M = NH_M * DHK_M, NH_M * DHV_M
NH_A, NKV_A, HD_A = 16, 4, 64
GROUP_A = NH_A // NKV_A
WINDOW = 128
DQ_A, DKV_A = NH_A * HD_A, NKV_A * HD_A
D_FF = 2816
CONV_W = 3
N_MOD = 6
LN_EPS = 1e-5
CHUNK = 128
NEG = -1e30
LANES = 128
SUBLANES = 8
VMEM_LIMIT = 56 * 1024 * 1024
SEQS_PER_STEP = 8


def _ln(x):
    mu = jnp.mean(x, axis=-1, keepdims=True)
    xc = x - mu
    var = jnp.mean(xc * xc, axis=-1, keepdims=True)
    return xc * lax.rsqrt(var + LN_EPS)


def _dot(a, b):
    return jnp.dot(a, b, preferred_element_type=F32)


def _dot_nt(a, b):
    return lax.dot_general(a, b, (((1,), (1,)), ((), ())), preferred_element_type=F32)


def _dot_tn(a, b):
    return lax.dot_general(a, b, (((0,), (0,)), ((), ())), preferred_element_type=F32)


def _const_spec(shape):
    nd = len(shape)
    return pl.BlockSpec(shape, lambda *_: (0,) * nd, pipeline_mode=pl.Buffered(1))


def _params(n_grid):
    return pltpu.CompilerParams(dimension_semantics=("arbitrary",) * n_grid,
                                vmem_limit_bytes=VMEM_LIMIT)


def _first_step():
    return (pl.program_id(0) == 0) & (pl.program_id(1) == 0)


def _ada_body(c_ref, w_ref, b_ref, o_ref):
    c = c_ref[...]
    s = (c * jax.nn.sigmoid(c)).astype(BF16)
    o_ref[...] = _dot(s, w_ref[...].astype(BF16)) + b_ref[...]


def _ada(c, w_ada, b_ada):
    rows = c.shape[0]
    n_out = w_ada.shape[1]
    bn = 1536
    assert n_out % bn == 0
    return pl.pallas_call(
        _ada_body,
        out_shape=jax.ShapeDtypeStruct((rows, n_out), F32),
        grid=(n_out // bn,),
        in_specs=[pl.BlockSpec((rows, D_MODEL), lambda j: (0, 0)),
                  pl.BlockSpec((D_MODEL, bn), lambda j: (0, j)),
                  pl.BlockSpec((1, bn), lambda j: (0, j))],
        out_specs=pl.BlockSpec((rows, bn), lambda j: (0, j)),
        compiler_params=_params(1),
        name="ada",
    )(c, w_ada, b_ada.reshape(1, n_out))


def _inproj_body(prompt, x_ref, sh_ref, sc_ref, wqk_ref, bqk_ref, wvm_ref, bvm_ref, wg_ref, bg_ref,
                 wqa_ref, bqa_ref, wkk_ref, bkk_ref, wv_ref, bv_ref, qk_ref, v_ref, g_ref, qa_ref, *rest):
    h = (_ln(x_ref[...]) * (1.0 + sc_ref[...]) + sh_ref[...]).astype(BF16)

    def proj(w_ref, b_ref, lo, n):
        return _dot(h, w_ref[:, lo:lo + n]) + b_ref[:, lo:lo + n]

    qk_ref[:, :DQK_M] = (proj(wqk_ref, bqk_ref, 0, DQK_M) * DHK_M ** -0.5).astype(qk_ref.dtype)
    qk_ref[:, DQK_M:] = proj(wqk_ref, bqk_ref, DQK_M, DQK_M).astype(qk_ref.dtype)
    if prompt:
        v_ref[...] = (_dot_nt(wvm_ref[...], h) + bvm_ref[...]).astype(v_ref.dtype)
    else:
        v_ref[...] = proj(wvm_ref, bvm_ref, 0, DV_M).astype(v_ref.dtype)
    g_ref[...] = proj(wg_ref, bg_ref, 0, LANES)
    qa_ref[...] = (proj(wqa_ref, bqa_ref, 0, DQ_A) * HD_A ** -0.5).astype(qa_ref.dtype)
    kk = proj(wkk_ref, bkk_ref, 0, 2 * DKV_A)
    if prompt:
        kk_ref, ka_ref, vat_ref = rest
        kk_ref[...] = kk.astype(BF16)
        ka_ref[...] = kk[:, :DKV_A]
        vat_ref[...] = _dot_nt(wv_ref[...], h) + bv_ref[...]
    else:
        ka_ref, va_ref = rest
        ka_ref[...] = kk[:, :DKV_A]
        va_ref[...] = _dot(h, wv_ref[...]) + bv_ref[...]


def _tok_spec(tm, n):
    return pl.BlockSpec((None, tm, n), lambda b, i: (b, i, 0))


def _mod_spec(rows, tm):
    if rows == 1:
        return pl.BlockSpec((None, 1, D_MODEL), lambda b, i: (b, 0, 0))
    return pl.BlockSpec((None, tm, D_MODEL), lambda b, i: (b, i, 0))


def _inproj(x, sh, sc, w, tm, prompt):
    bsz, seq, _ = x.shape
    mrows = sh.shape[1]
    act = BF16 if prompt else F32
    t_spec = lambda n: pl.BlockSpec((None, n, tm), lambda b, i: (b, 0, i))
    outs = [((bsz, seq, 2 * DQK_M), act, _tok_spec(tm, 2 * DQK_M)),
            ((bsz, DV_M, seq), act, t_spec(DV_M)) if prompt else ((bsz, seq, DV_M), act, _tok_spec(tm, DV_M)),
            ((bsz, seq, LANES), F32, _tok_spec(tm, LANES)),
            ((bsz, seq, DQ_A), act, _tok_spec(tm, DQ_A))]
    if prompt:
        wvm, bvm, wv, bv = w["wvm_t"], w["bvm_col"], w["wv_t"], w["bv_col"]
        outs += [((bsz, seq, 2 * DKV_A), BF16, _tok_spec(tm, 2 * DKV_A)),
                 ((bsz, seq, DKV_A), F32, _tok_spec(tm, DKV_A)),
                 ((bsz, DKV_A, seq), F32, t_spec(DKV_A))]
    else:
        wvm, bvm, wv, bv = w["wvm"], w["bvm_row"], w["wv"], w["bv_row"]
        outs += [((bsz, seq, DKV_A), F32, _tok_spec(tm, DKV_A)),
                 ((bsz, seq, DKV_A), F32, _tok_spec(tm, DKV_A))]
    weights = [w["wqk"], w["bqk"], wvm, bvm, w["wg"], w["bg"], w["wqa"], w["bqa"], w["wkk"], w["bkk"],
               wv, bv]
    return pl.pallas_call(
        functools.partial(_inproj_body, prompt),
        out_shape=[jax.ShapeDtypeStruct(s, dt) for s, dt, _ in outs],
        grid=(bsz, seq // tm),
        in_specs=[_tok_spec(tm, D_MODEL), _mod_spec(mrows, tm), _mod_spec(mrows, tm)]
                 + [_const_spec(a.shape) for a in weights],
        out_specs=[spec for _, _, spec in outs],
        compiler_params=_params(2),
        name="inproj",
    )(x, sh, sc, *weights)


def _scan_rows(x, op, rows):
    row = lax.broadcasted_iota(jnp.int32, x.shape, 0)
    d = 1
    while d < rows:
        shifted = pltpu.roll(x, d, axis=0)
        x = jnp.where(row >= d, op(x, shifted), x)
        d *= 2
    return x


def _mlstm_sample_body(t_in, nb, qk_ref, v_ref, g_ref, c0_ref, n0_ref, m0_ref, nw_ref,
                       h_ref, c_ref, n_ref, m_ref, *pads):
    t = SUBLANES
    c_ref[...] = c0_ref[...]
    n_ref[...] = n0_ref[...]
    m_ref[...] = m0_ref[...]

    @pl.when(_first_step())
    def _():
        for p in pads:
            p[...] = jnp.zeros(p.shape, p.dtype)
    for src, dst in zip((qk_ref, v_ref, g_ref), pads):
        dst[:, pl.ds(0, t_in), :] = src[...]
    qk_src, v_src, g_src = pads

    row = lax.broadcasted_iota(jnp.int32, (t, LANES), 0)
    lane = lax.broadcasted_iota(jnp.int32, (t, LANES), 1)
    lane1 = lax.broadcasted_iota(jnp.int32, (1, LANES), 1)
    r2 = lax.broadcasted_iota(jnp.int32, (t, t), 0)
    c2 = lax.broadcasted_iota(jnp.int32, (t, t), 1)
    causal = c2 <= r2
    pad_gate = jnp.where(lane < NH_M, NEG, -NEG)

    seqs = []
    for s in range(nb):
        ga = _gate_algebra(jnp.where(row < t_in, g_src[s], pad_gate), m_ref[s], t)
        ga["gt"] = ga["gg"].T
        m_ref[s] = jnp.where(lane1 < NH_M, ga["m_new"], 0.0)
        seqs.append(ga)

    units = [(s, hd) for s in range(nb) for hd in range(NH_M)]
    nu = len(units)
    col = lambda name, s, hd: seqs[s][name][:, hd:hd + 1]
    q = [qk_src[s, :, hd * DHK_M:(hd + 1) * DHK_M].astype(BF16) for s, hd in units]
    k = [qk_src[s, :, DQK_M + hd * DHK_M:DQK_M + (hd + 1) * DHK_M].astype(BF16) for s, hd in units]
    v = [v_src[s, :, hd * DHV_M:(hd + 1) * DHV_M].astype(BF16) for s, hd in units]
    c_old = [c_ref[s, hd] for s, hd in units]
    n_old = [n_ref[s, hd:hd + 1, :] for s, hd in units]

    sqk = [_dot_nt(q[u], k[u]) for u in range(nu)]
    dexp = [jnp.exp(jnp.where(causal, col("eb", s, hd) + seqs[s]["gt"][hd:hd + 1, :], NEG))
            for s, hd in units]
    smat = [sqk[u] * dexp[u] for u in range(nu)]
    intra = [_dot(smat[u].astype(BF16), v[u]) for u in range(nu)]
    inter = [_dot_nt(q[u], c_old[u].astype(BF16)) for u in range(nu)]
    qn = [jnp.sum(q[u].astype(F32) * n_old[u].astype(BF16).astype(F32), axis=-1, keepdims=True)
          for u in range(nu)]
    den = [jnp.sum(smat[u], axis=-1, keepdims=True) + col("a_in", s, hd) * qn[u]
           for u, (s, hd) in enumerate(units)]
    hh = [(intra[u] + col("a_in", s, hd) * inter[u])
          / jnp.maximum(jnp.abs(den[u]), col("lowb", s, hd)) for u, (s, hd) in enumerate(units)]
    hn = [_ln(hh[u]) * nw_ref[:, hd * DHV_M:(hd + 1) * DHV_M] for u, (s, hd) in enumerate(units)]
    for u, (s, hd) in enumerate(units):
        h_ref[s, :, hd * DHV_M:(hd + 1) * DHV_M] = hn[u][:t_in]

    kw = [k[u].astype(F32) * col("ws", s, hd) for u, (s, hd) in enumerate(units)]
    upd = [_dot_tn(v[u], kw[u].astype(BF16)) for u in range(nu)]
    for u, (s, hd) in enumerate(units):
        dec = seqs[s]["decay"][:, hd:hd + 1]
        c_ref[s, hd] = dec * c_old[u] + upd[u]
        n_ref[s, hd:hd + 1, :] = dec * n_old[u] + jnp.sum(kw[u], axis=0, keepdims=True)


def _mlstm_sample(qk, v, g, c0, n0, m0, norm_w, nb):
    nseq, t_in, _ = qk.shape
    assert t_in <= SUBLANES
    blk = lambda n: pl.BlockSpec((nb, t_in, n), lambda o, c: (o, 0, 0))
    st_c = pl.BlockSpec((nb, NH_M, DHV_M, DHK_M), lambda o, c: (o, 0, 0, 0))
    st_n = pl.BlockSpec((nb, NH_M, DHK_M), lambda o, c: (o, 0, 0))
    st_m = pl.BlockSpec((nb, 1, LANES), lambda o, c: (o, 0, 0))
    return pl.pallas_call(
        functools.partial(_mlstm_sample_body, t_in, nb),
        out_shape=[jax.ShapeDtypeStruct((nseq, t_in, DV_M), F32),
                   jax.ShapeDtypeStruct((nseq, NH_M, DHV_M, DHK_M), F32),
                   jax.ShapeDtypeStruct((nseq, NH_M, DHK_M), F32),
                   jax.ShapeDtypeStruct((nseq, 1, LANES), F32)],
        grid=(nseq // nb, 1),
        in_specs=[blk(2 * DQK_M), blk(DV_M), blk(LANES), st_c, st_n, st_m,
                  pl.BlockSpec((1, DV_M), lambda o, c: (0, 0))],
        out_specs=[blk(DV_M), st_c, st_n, st_m],
        scratch_shapes=[pltpu.VMEM((nb, SUBLANES, 2 * DQK_M), F32), pltpu.VMEM((nb, SUBLANES, DV_M), F32),
                        pltpu.VMEM((nb, SUBLANES, LANES), F32)],
        compiler_params=_params(2),
        name="mlstm_sample",
    )(qk, v, g, c0, n0, m0, norm_w)


def _gate_algebra(g, m_prev, t):
    b = pltpu.roll(_scan_rows(jax.nn.log_sigmoid(g), jnp.add, t), LANES - NH_M, axis=1)
    gg = g - b
    gmx = _scan_rows(gg, jnp.maximum, t)
    a = b + m_prev
    mt = jnp.maximum(a, b + gmx)
    b_last, gmx_last = b[t - 1:t, :], gmx[t - 1:t, :]
    m_new = jnp.maximum(b_last + m_prev, b_last + gmx_last)
    return dict(gg=gg, eb=b - mt, a_in=jnp.exp(a - mt), lowb=jnp.exp(-mt),
                ws=jnp.exp(b_last + gg - m_new), decay=jnp.exp(b_last + m_prev - m_new), m_new=m_new)


MLSTM_CHUNKS_PER_STEP = 4


def _mlstm_prompt_body(nb, qk_ref, vt_ref, g_ref, c0_ref, n0_ref, m0_ref, nwc_ref,
                       h_ref, c_ref, n_ref, m_ref, nwb):
    @pl.when(pl.program_id(1) == 0)
    def _():
        c_ref[...] = c0_ref[...]
        n_ref[...] = n0_ref[...]
        m_ref[...] = m0_ref[...]

    @pl.when(_first_step())
    def _():
        for hd in range(NH_M):
            nwb[hd] = jnp.broadcast_to(nwc_ref[hd * DHV_M:(hd + 1) * DHV_M, :], (DHV_M, CHUNK))

    for ci in range(MLSTM_CHUNKS_PER_STEP):
        _mlstm_prompt_chunk(nb, slice(ci * CHUNK, (ci + 1) * CHUNK), qk_ref, vt_ref, g_ref, nwb,
                            h_ref, c_ref, n_ref, m_ref)


def _mlstm_prompt_chunk(nb, rows, qk_ref, vt_ref, g_ref, nwb, h_ref, c_ref, n_ref, m_ref):
    t = CHUNK
    lane1 = lax.broadcasted_iota(jnp.int32, (1, LANES), 1)
    r2 = lax.broadcasted_iota(jnp.int32, (t, t), 0)
    c2 = lax.broadcasted_iota(jnp.int32, (t, t), 1)
    causal = r2 <= c2

    seqs = []
    for s in range(nb):
        ga = _gate_algebra(g_ref[s, rows, :], m_ref[s], t)
        m_ref[s] = jnp.where(lane1 < NH_M, ga["m_new"], 0.0)
        for name in ("eb", "a_in", "lowb", "ws"):
            ga[name + "_t"] = ga[name].T
        seqs.append(ga)

    units = [(s, hd) for s in range(nb) for hd in range(NH_M)]
    nu = len(units)
    rowv = lambda name, s, hd: seqs[s][name + "_t"][hd:hd + 1, :]
    q = [qk_ref[s, rows, hd * DHK_M:(hd + 1) * DHK_M] for s, hd in units]
    k = [qk_ref[s, rows, DQK_M + hd * DHK_M:DQK_M + (hd + 1) * DHK_M] for s, hd in units]
    vt = [vt_ref[s, hd * DHV_M:(hd + 1) * DHV_M, rows] for s, hd in units]
    c_old = [c_ref[s, hd] for s, hd in units]
    n_old = [n_ref[s, hd:hd + 1, :] for s, hd in units]

    skq = [_dot_nt(k[u], q[u]) for u in range(nu)]
    dexp = [jnp.exp(jnp.where(causal, seqs[s]["gg"][:, hd:hd + 1] + rowv("eb", s, hd), NEG))
            for s, hd in units]
    smat = [skq[u] * dexp[u] for u in range(nu)]
    intra = [_dot(vt[u], smat[u].astype(BF16)) for u in range(nu)]
    inter = [_dot_nt(c_old[u].astype(BF16), q[u]) for u in range(nu)]
    qn = [_dot_nt(jnp.broadcast_to(n_old[u], (SUBLANES, DHK_M)).astype(BF16), q[u])[0:1, :]
          for u in range(nu)]
    den = [jnp.sum(smat[u], axis=0, keepdims=True) + rowv("a_in", s, hd) * qn[u]
           for u, (s, hd) in enumerate(units)]
    inv = [1.0 / jnp.maximum(jnp.abs(den[u]), rowv("lowb", s, hd)) for u, (s, hd) in enumerate(units)]
    hh = [(intra[u] + rowv("a_in", s, hd) * inter[u]) * inv[u] for u, (s, hd) in enumerate(units)]
    mu = [jnp.mean(hh[u], axis=0, keepdims=True) for u in range(nu)]
    xc = [hh[u] - mu[u] for u in range(nu)]
    var = [jnp.mean(xc[u] * xc[u], axis=0, keepdims=True) for u in range(nu)]
    hn = [xc[u] * lax.rsqrt(var[u] + LN_EPS) * nwb[hd] for u, (s, hd) in enumerate(units)]
    for u, (s, hd) in enumerate(units):
        h_ref[s, rows, hd * DHV_M:(hd + 1) * DHV_M] = hn[u].T

    vw = [(vt[u].astype(F32) * rowv("ws", s, hd)).astype(BF16) for u, (s, hd) in enumerate(units)]
    upd = [_dot(vw[u], k[u]) for u in range(nu)]
    nupd = [_dot(jnp.broadcast_to(rowv("ws", s, hd), (SUBLANES, t)).astype(BF16), k[u])[0:1, :]
            for u, (s, hd) in enumerate(units)]
    for u, (s, hd) in enumerate(units):
        dec = seqs[s]["decay"][:, hd:hd + 1]
        c_ref[s, hd] = dec * c_old[u] + upd[u]
        n_ref[s, hd:hd + 1, :] = dec * n_old[u] + nupd[u]


def _mlstm_prompt(qk, vt, g, c0, n0, m0, norm_w_col):
    nb, seq, _ = qk.shape
    ts = MLSTM_CHUNKS_PER_STEP * CHUNK
    tok = lambda n: pl.BlockSpec((nb, ts, n), lambda o, c: (0, c, 0))
    st_c = pl.BlockSpec((nb, NH_M, DHV_M, DHK_M), lambda o, c: (0, 0, 0, 0))
    st_n = pl.BlockSpec((nb, NH_M, DHK_M), lambda o, c: (0, 0, 0))
    st_m = pl.BlockSpec((nb, 1, LANES), lambda o, c: (0, 0, 0))
    return pl.pallas_call(
        functools.partial(_mlstm_prompt_body, nb),
        out_shape=[jax.ShapeDtypeStruct((nb, seq, DV_M), F32),
                   jax.ShapeDtypeStruct((nb, NH_M, DHV_M, DHK_M), F32),
                   jax.ShapeDtypeStruct((nb, NH_M, DHK_M), F32),
                   jax.ShapeDtypeStruct((nb, 1, LANES), F32)],
        grid=(1, seq // ts),
        in_specs=[tok(2 * DQK_M), pl.BlockSpec((nb, DV_M, ts), lambda o, c: (0, 0, c)), tok(LANES),
                  st_c, st_n, st_m, pl.BlockSpec((DV_M, 1), lambda o, c: (0, 0))],
        out_specs=[tok(DV_M), st_c, st_n, st_m],
        scratch_shapes=[pltpu.VMEM((NH_M, DHV_M, CHUNK), F32)],
        compiler_params=_params(2),
        name="mlstm_prompt",
    )(qk, vt, g, c0, n0, m0, norm_w_col)


SWA_BLOCKS_PER_STEP = 8


def _swa_prompt_body(prm_ref, q_ref, kc_ref, kp_ref, vc_ref, vp_ref, o_ref, tbl):
    row = lax.broadcasted_iota(jnp.int32, (WINDOW, WINDOW), 0)
    col = lax.broadcasted_iota(jnp.int32, (WINDOW, WINDOW), 1)
    tri = row <= col

    @pl.when(_first_step())
    def _():
        dist = jnp.where(tri, col - row, col - row + WINDOW).astype(F32)
        for hd in range(NH_A):
            bias = prm_ref[0, hd] * dist
            tbl[0, hd] = bias
            tbl[1, hd] = bias + jnp.where(tri, 0.0, -NEG)

    for b in range(SWA_BLOCKS_PER_STEP):
        rows = pl.ds(b * WINDOW, WINDOW)
        which = jnp.where(pl.program_id(1) == 0, 1, 0) if b == 0 else 0
        k_prev = kp_ref if b == 0 else kc_ref.at[pl.ds((b - 1) * WINDOW, WINDOW), :]
        v_prev = vp_ref if b == 0 else vc_ref.at[:, pl.ds((b - 1) * WINDOW, WINDOW)]
        _swa_prompt_block(which, prm_ref, q_ref.at[rows, :], kc_ref.at[rows, :], k_prev,
                          vc_ref.at[:, rows], v_prev, o_ref.at[rows, :], tbl)


def _swa_prompt_block(which, prm_ref, q_ref, kc_ref, kp_ref, vc_ref, vp_ref, o_ref, tbl):
    row = lax.broadcasted_iota(jnp.int32, (WINDOW, WINDOW), 0)
    col = lax.broadcasted_iota(jnp.int32, (WINDOW, WINDOW), 1)
    tri = row <= col
    lo = col < HD_A
    zb = jnp.zeros((WINDOW, LANES), BF16)

    def placed(k_ref, kv):
        c, par = kv // 2, kv % 2
        own = k_ref[:, c * LANES:(c + 1) * LANES]
        swp = k_ref[:, DKV_A + c * LANES:DKV_A + (c + 1) * LANES]
        if par == 0:
            return jnp.where(lo, own, zb), jnp.where(lo, zb, swp)
        return jnp.where(lo, swp, zb), jnp.where(lo, zb, own)

    st = []
    for kv in range(NKV_A):
        lhs = jnp.concatenate([*placed(kc_ref, kv), *placed(kp_ref, kv)], axis=0)
        qg = jnp.concatenate([q_ref[:, (2 * kv) * LANES:(2 * kv + 1) * LANES],
                              q_ref[:, (2 * kv + 1) * LANES:(2 * kv + 2) * LANES]], axis=0)
        st.append(_dot_nt(lhs, qg))
    tiles = []
    for hd in range(NH_A):
        kv, a, par = hd // GROUP_A, (hd % GROUP_A) // 2, hd % 2
        cols = slice(a * WINDOW, (a + 1) * WINDOW)
        tiles.append(jnp.where(tri, st[kv][par * WINDOW:(par + 1) * WINDOW, cols],
                               st[kv][(2 + par) * WINDOW:(3 + par) * WINDOW, cols]))
    sc = jnp.concatenate(tiles, axis=0).reshape(NH_A, WINDOW, WINDOW) - tbl[which]
    sink = jnp.concatenate([jnp.full((1, 1, WINDOW), prm_ref[1, hd], F32) for hd in range(NH_A)], axis=0)
    mx = jnp.maximum(jnp.max(sc, axis=1, keepdims=True), sink)
    p = jnp.exp(sc - mx)
    den = jnp.sum(p, axis=1, keepdims=True) + jnp.exp(sink - mx)
    pn = p * (1.0 / den)

    zero = jnp.zeros((WINDOW, WINDOW), F32)
    z64 = jnp.zeros((HD_A, WINDOW), BF16)
    outs = []
    for kv in range(NKV_A):
        cols = []
        for a in range(2):
            pe, po = pn[kv * GROUP_A + 2 * a], pn[kv * GROUP_A + 2 * a + 1]
            cols.append(jnp.concatenate([jnp.where(tri, pe, zero), jnp.where(tri, po, zero),
                                         jnp.where(tri, zero, pe), jnp.where(tri, zero, po)],
                                        axis=0).astype(BF16))
        pt = jnp.concatenate(cols, axis=1)
        vc = vc_ref[kv * HD_A:(kv + 1) * HD_A, :].astype(BF16)
        vp = vp_ref[kv * HD_A:(kv + 1) * HD_A, :].astype(BF16)
        vt = jnp.concatenate([jnp.concatenate([vc, z64], axis=0), jnp.concatenate([z64, vc], axis=0),
                              jnp.concatenate([vp, z64], axis=0), jnp.concatenate([z64, vp], axis=0)],
                             axis=1)
        ot = _dot(vt, pt)
        outs += [ot[:, :WINDOW].T, ot[:, WINDOW:].T]
    o_ref[...] = jnp.concatenate(outs, axis=1).astype(o_ref.dtype)


def _swa_prompt(prm, q, kk, vat):
    bsz, seq, _ = q.shape
    tq = SWA_BLOCKS_PER_STEP * WINDOW
    prev = lambda i: jnp.maximum(i * SWA_BLOCKS_PER_STEP - 1, 0)
    return pl.pallas_call(
        _swa_prompt_body,
        out_shape=jax.ShapeDtypeStruct((bsz, seq, DQ_A), BF16),
        grid=(bsz, seq // tq),
        in_specs=[pl.BlockSpec(memory_space=pltpu.SMEM),
                  pl.BlockSpec((None, tq, DQ_A), lambda b, i: (b, i, 0)),
                  pl.BlockSpec((None, tq, 2 * DKV_A), lambda b, i: (b, i, 0)),
                  pl.BlockSpec((None, WINDOW, 2 * DKV_A), lambda b, i: (b, prev(i), 0)),
                  pl.BlockSpec((None, DKV_A, tq), lambda b, i: (b, 0, i)),
                  pl.BlockSpec((None, DKV_A, WINDOW), lambda b, i: (b, 0, prev(i)))],
        out_specs=pl.BlockSpec((None, tq, DQ_A), lambda b, i: (b, i, 0)),
        scratch_shapes=[pltpu.VMEM((2, NH_A, WINDOW, WINDOW), F32)],
        compiler_params=_params(2),
        name="swa_prompt",
    )(prm, q, kk, kk, vat, vat)


def _half_mask(shape, half):
    lane = lax.broadcasted_iota(jnp.int32, shape, 1)
    return lane < HD_A if half == 0 else lane >= HD_A


def _swa_sample_body(t_in, nb, prm_ref, q_ref, kn_ref, vn_ref, kct_ref, vct_ref,
                     o_ref, kco_ref, vco_ref, tbl, q8, kn_pad, vn_pad):
    tq = SUBLANES
    rows = NH_A * tq
    row = lax.broadcasted_iota(jnp.int32, (rows, WINDOW), 0)
    col = lax.broadcasted_iota(jnp.int32, (rows, WINDOW), 1)
    tri = col <= (row & (tq - 1))

    @pl.when(_first_step())
    def _():
        r8 = lax.broadcasted_iota(jnp.int32, (tq, WINDOW), 0)
        c8 = lax.broadcasted_iota(jnp.int32, (tq, WINDOW), 1)
        dist = jnp.where(c8 <= r8, r8 - c8, r8 - c8 + WINDOW).astype(F32)
        for hd in range(NH_A):
            tbl[pl.ds(hd * tq, tq), :] = prm_ref[0, hd] * dist
        for p in (q8, kn_pad, vn_pad):
            p[...] = jnp.zeros(p.shape, p.dtype)

    q8[:, pl.ds(0, t_in), :] = q_ref[...]
    kn_pad[:, pl.ds(0, t_in), :] = kn_ref[...]
    vn_pad[:, pl.ds(0, t_in), :] = vn_ref[...]
    sink_col = jnp.concatenate([jnp.full((tq, 1), prm_ref[1, hd], F32) for hd in range(NH_A)], axis=0)
    z8 = jnp.zeros((tq, LANES), F32)
    n_chunk = NKV_A // 2
    heads_per_chunk = NH_A // n_chunk

    def place(piece, src_half, dst_half):
        if src_half != dst_half:
            piece = pltpu.roll(piece, HD_A, axis=1)
        return jnp.where(_half_mask(piece.shape, dst_half), piece, z8)

    qexp = []
    for s in range(nb):
        per_c = []
        for c in range(n_chunk):
            pieces = []
            for hl in range(heads_per_chunk):
                hd = c * heads_per_chunk + hl
                pieces.append(place(q8[s, :, (hd // 2) * LANES:(hd // 2 + 1) * LANES],
                                    hd % 2, hl // GROUP_A))
            per_c.append(jnp.concatenate(pieces, axis=0).astype(BF16))
        qexp.append(per_c)
    csl = lambda c: slice(c * LANES, (c + 1) * LANES)
    s_prev = [[_dot(qexp[s][c], kct_ref[s, csl(c), :].astype(BF16)) for c in range(n_chunk)]
              for s in range(nb)]
    s_cur = [[_dot_nt(qexp[s][c], kn_pad[s, :, csl(c)].astype(BF16)) for c in range(n_chunk)]
             for s in range(nb)]
    sc = [jnp.where(tri, jnp.concatenate(s_cur[s], axis=0), jnp.concatenate(s_prev[s], axis=0)) - tbl[...]
          for s in range(nb)]
    mx = [jnp.maximum(jnp.max(sc[s], axis=-1, keepdims=True), sink_col) for s in range(nb)]
    p = [jnp.exp(sc[s] - mx[s]) for s in range(nb)]
    den = [jnp.sum(p[s], axis=-1, keepdims=True) + jnp.exp(sink_col - mx[s]) for s in range(nb)]
    pn = [p[s] * (1.0 / den[s]) for s in range(nb)]
    zero = jnp.zeros((rows, WINDOW), F32)
    pc = [jnp.where(tri, pn[s], zero).astype(BF16) for s in range(nb)]
    pp = [jnp.where(tri, zero, pn[s]).astype(BF16) for s in range(nb)]
    half_rows = heads_per_chunk * tq
    oc = [[_dot_nt(pp[s][c * half_rows:(c + 1) * half_rows], vct_ref[s, csl(c), :].astype(BF16))
           + _dot(pc[s][c * half_rows:(c + 1) * half_rows], vn_pad[s, :, csl(c)].astype(BF16))
           for c in range(n_chunk)] for s in range(nb)]
    for s in range(nb):
        chunks = []
        for pch in range(NH_A // 2):
            acc = None
            for hd in (2 * pch, 2 * pch + 1):
                c, hl = hd // heads_per_chunk, hd % heads_per_chunk
                piece = place(oc[s][c][hl * tq:(hl + 1) * tq, :], hl // GROUP_A, hd % 2)
                acc = piece if acc is None else acc + piece
            chunks.append(acc)
        o_ref[s] = jnp.concatenate(chunks, axis=1)[:t_in]

    lane = lax.broadcasted_iota(jnp.int32, (DKV_A, WINDOW), 1)
    for new_pad, old_ref, out_ref in ((kn_pad, kct_ref, kco_ref), (vn_pad, vct_ref, vco_ref)):
        for s in range(nb):
            merged = jnp.where(lane < t_in, new_pad[s].T, old_ref[s])
            out_ref[s] = pltpu.roll(merged, WINDOW - t_in, axis=1)


def _swa_sample(prm, q, kn, vn, kct, vct, nb):
    nseq, t_in, _ = q.shape
    assert t_in <= SUBLANES
    cur = lambda n: pl.BlockSpec((nb, t_in, n), lambda o, i: (o, 0, 0))
    win = pl.BlockSpec((nb, DKV_A, WINDOW), lambda o, i: (o, 0, 0))
    return pl.pallas_call(
        functools.partial(_swa_sample_body, t_in, nb),
        out_shape=[jax.ShapeDtypeStruct((nseq, t_in, DQ_A), F32),
                   jax.ShapeDtypeStruct((nseq, DKV_A, WINDOW), F32),
                   jax.ShapeDtypeStruct((nseq, DKV_A, WINDOW), F32)],
        grid=(nseq // nb, 1),
        in_specs=[pl.BlockSpec(memory_space=pltpu.SMEM), cur(DQ_A), cur(DKV_A), cur(DKV_A), win, win],
        out_specs=[cur(DQ_A), win, win],
        scratch_shapes=[pltpu.VMEM((NH_A * SUBLANES, WINDOW), F32),
                        pltpu.VMEM((nb, SUBLANES, DQ_A), F32),
                        pltpu.VMEM((nb, WINDOW, DKV_A), F32),
                        pltpu.VMEM((nb, WINDOW, DKV_A), F32)],
        compiler_params=_params(2),
        name="swa_sample",
    )(prm, q, kn, vn, kct, vct)


def _merge_body(alpha, x_ref, sh_ref, sc_ref, g1_ref, hm_ref, ha_ref, wog_ref, bog_ref,
                wgg_ref, bgg_ref, wbm_ref, wba_ref, wo_ref, lg_ref, lb_ref, o_ref):
    nsub = 2
    sub = x_ref.shape[0] // nsub
    rows = [pl.ds(r * sub, sub) for r in range(nsub)]
    mod = lambda ref, r: ref[...] if ref.shape[0] == 1 else ref[rows[r], :]

    def branch_gate(h, j):
        cols = slice(j * D_MODEL, (j + 1) * D_MODEL)
        return jax.nn.sigmoid(_dot(h, wgg_ref[:, cols]) + bgg_ref[:, cols])

    x = [x_ref[rows[r], :] for r in range(nsub)]
    h = [(_ln(x[r]) * (1.0 + mod(sc_ref, r)) + mod(sh_ref, r)).astype(BF16) for r in range(nsub)]
    og = [jax.nn.sigmoid(_dot(h[r], wog_ref[...]) + bog_ref[...]) for r in range(nsub)]
    hm = [(hm_ref[rows[r], :] * og[r]).astype(BF16) for r in range(nsub)]
    bm = [_dot(hm[r], wbm_ref[...]) for r in range(nsub)]
    gm = [branch_gate(h[r], 0) for r in range(nsub)]
    ba = [_dot(ha_ref[rows[r], :].astype(BF16), wba_ref[...]) for r in range(nsub)]
    ga = [branch_gate(h[r], 1) for r in range(nsub)]
    merged = [(gm[r] * bm[r] + ga[r] * ba[r]).astype(BF16) for r in range(nsub)]
    mo = [_dot(merged[r], wo_ref[...]) for r in range(nsub)]
    for r in range(nsub):
        o_ref[rows[r], :] = _ln(alpha * x[r] + mod(g1_ref, r) * mo[r]) * lg_ref[...] + lb_ref[...]


def _merge(x, sh, sc, g1, hm, ha, w, tm, alpha):
    bsz, seq, _ = x.shape
    mrows = sh.shape[1]
    weights = [w["wog"], w["bog"], w["wgg"], w["bgg"], w["wbm"], w["wba"], w["wo"],
               w["ln1_g"], w["ln1_b"]]
    return pl.pallas_call(
        functools.partial(_merge_body, alpha),
        out_shape=jax.ShapeDtypeStruct((bsz, seq, D_MODEL), F32),
        grid=(bsz, seq // tm),
        in_specs=[_tok_spec(tm, D_MODEL), _mod_spec(mrows, tm), _mod_spec(mrows, tm),
                  _mod_spec(mrows, tm), _tok_spec(tm, DV_M), _tok_spec(tm, DQ_A)]
                 + [_const_spec(a.shape) for a in weights],
        out_specs=_tok_spec(tm, D_MODEL),
        compiler_params=_params(2),
        name="merge",
    )(x, sh, sc, g1, hm, ha, *weights)


def _ffn_body(alpha, tm, stride, halo, x_ref, sh_ref, sc_ref, g2_ref, cb0_ref, wup_ref, bup_ref,
              cw_ref, cbias_ref, wdn_ref, bdn_ref, lg_ref, lb_ref, o_ref, cs_ref, ubuf, act):
    @pl.when(pl.program_id(1) == 0)
    def _():
        cs_ref[...] = cb0_ref[...]

    nsub = 2
    sub = tm // nsub
    rows = [pl.ds(r * sub, sub) for r in range(nsub)]
    mod = lambda ref, r: ref[...] if ref.shape[0] == 1 else ref[rows[r], :]
    x = [x_ref[rows[r], :] for r in range(nsub)]
    h = [(_ln(x[r]) * (1.0 + mod(sc_ref, r)) + mod(sh_ref, r)).astype(BF16) for r in range(nsub)]
    ys = [[None, None] for _ in range(nsub)]
    for half in range(2):
        cols = slice(half * D_FF, (half + 1) * D_FF)
        u = [_dot(h[r], wup_ref[:, cols]) + bup_ref[:, cols] for r in range(nsub)]
        ubuf[pl.ds(0, halo), :] = cs_ref[:, cols]
        for r in range(nsub):
            ubuf[pl.ds(halo + r * sub, sub), :] = u[r]
        cs_ref[:, cols] = ubuf[pl.ds(tm, halo), :]
        for r in range(nsub):
            y = cbias_ref[:, cols] + u[r] * cw_ref[CONV_W - 1:CONV_W, cols]
            for j in range(CONV_W - 1):
                tap = ubuf[pl.ds(halo + r * sub - (CONV_W - 1 - j) * stride, sub), :]
                y = y + tap * cw_ref[j:j + 1, cols]
            ys[r][half] = y
    for r in range(nsub):
        act[rows[r], :] = (jax.nn.gelu(ys[r][0]) * ys[r][1]).astype(BF16)
    f = [_dot(act[rows[r], :], wdn_ref[...]) + bdn_ref[...] for r in range(nsub)]
    for r in range(nsub):
        o_ref[rows[r], :] = _ln(alpha * x[r] + mod(g2_ref, r) * f[r]) * lg_ref[...] + lb_ref[...]


def _ffn(x, sh, sc, g2, cb0, w, tm, stride, alpha):
    bsz, seq, _ = x.shape
    mrows = sh.shape[1]
    halo = cb0.shape[1]
    cs = pl.BlockSpec((None, halo, 2 * D_FF), lambda b, i: (b, 0, 0))
    weights = [w["wup"], w["bup"], w["cw"], w["cbias"], w["wdn"], w["bdn"], w["ln2_g"], w["ln2_b"]]
    return pl.pallas_call(
        functools.partial(_ffn_body, alpha, tm, stride, halo),
        out_shape=[jax.ShapeDtypeStruct((bsz, seq, D_MODEL), F32),
                   jax.ShapeDtypeStruct((bsz, halo, 2 * D_FF), F32)],
        grid=(bsz, seq // tm),
        in_specs=[_tok_spec(tm, D_MODEL), _mod_spec(mrows, tm), _mod_spec(mrows, tm),
                  _mod_spec(mrows, tm), cs] + [_const_spec(a.shape) for a in weights],
        out_specs=[_tok_spec(tm, D_MODEL), cs],
        scratch_shapes=[pltpu.VMEM((halo + tm, D_FF), F32), pltpu.VMEM((tm, D_FF), BF16)],
        compiler_params=_params(2),
        name="ffn",
    )(x, sh, sc, g2, cb0, *weights)


_O_GATE = 2 * DQK_M + DV_M
_O_OG = _O_GATE + 2 * NH_M
_O_QA = _O_OG + DV_M
_O_KA = _O_QA + DQ_A
_O_VA = _O_KA + DKV_A
_O_GM = _O_VA + DKV_A
_PREP_ROWS = 128


def _swap_halves(a, axis):
    parts = []
    for c in range(a.shape[axis] // LANES):
        parts += [lax.slice_in_dim(a, c * LANES + HD_A, (c + 1) * LANES, axis=axis),
                  lax.slice_in_dim(a, c * LANES, c * LANES + HD_A, axis=axis)]
    return jnp.concatenate(parts, axis=axis)


def _split_w_in_body(wt_ref, qk_ref, vm_ref, vmt_ref, g_ref, og_ref, qa_ref, kk_ref, v_ref, vt_ref, gg_ref):
    piece = lambda lo, hi: wt_ref[lo:hi, :]
    qk_ref[...] = piece(0, 2 * DQK_M).T.astype(BF16)
    vm = piece(2 * DQK_M, _O_GATE)
    vmt_ref[...] = vm.astype(BF16)
    vm_ref[...] = vm.T.astype(BF16)
    pad = jnp.zeros((LANES - 2 * NH_M, _PREP_ROWS), F32)
    g_ref[...] = jnp.concatenate([piece(_O_GATE, _O_OG), pad], axis=0).T.astype(BF16)
    og_ref[...] = piece(_O_OG, _O_QA).T.astype(BF16)
    qa_ref[...] = piece(_O_QA, _O_KA).T.astype(BF16)
    k = piece(_O_KA, _O_VA)
    kk_ref[...] = jnp.concatenate([k, _swap_halves(k, 0)], axis=0).T.astype(BF16)
    v = piece(_O_VA, _O_GM)
    vt_ref[...] = v.astype(BF16)
    v_ref[...] = v.T.astype(BF16)
    gg_ref[...] = piece(_O_GM, wt_ref.shape[0]).T.astype(BF16)


def _split_w_in(w_in_t):
    d_in = w_in_t.shape[0]
    rows = lambda n: ((D_MODEL, n), pl.BlockSpec((_PREP_ROWS, n), lambda i: (i, 0)))
    cols = lambda n: ((n, D_MODEL), pl.BlockSpec((n, _PREP_ROWS), lambda i: (0, i)))
    outs = dict(wqk=rows(2 * DQK_M), wvm=rows(DV_M), wvm_t=cols(DV_M), wg=rows(LANES), wog=rows(DV_M),
                wqa=rows(DQ_A), wkk=rows(2 * DKV_A), wv=rows(DKV_A), wv_t=cols(DKV_A),
                wgg=rows(d_in - _O_GM))
    res = pl.pallas_call(
        _split_w_in_body,
        out_shape=[jax.ShapeDtypeStruct(s, BF16) for s, _ in outs.values()],
        grid=(D_MODEL // _PREP_ROWS,),
        in_specs=[pl.BlockSpec((d_in, _PREP_ROWS), lambda i: (0, i))],
        out_specs=[spec for _, spec in outs.values()],
        compiler_params=_params(1),
        name="split_w_in",
    )(w_in_t)
    return dict(zip(outs.keys(), res))


def _prep_weights(w_in, b_in, mlstm_norm_w, w_branch_m, w_branch_a, w_out, ln1_g, ln1_b,
                  w_up, b_up, conv_w, conv_b, w_down, b_down, ln2_g, ln2_b):
    row = lambda a: a.reshape(1, -1)
    gate_pad = LANES - 2 * NH_M
    b_k, b_v = b_in[_O_KA:_O_VA], b_in[_O_VA:_O_GM]
    return dict(
        **_split_w_in(w_in.T),
        bqk=row(b_in[:2 * DQK_M]),
        bvm_row=row(b_in[2 * DQK_M:_O_GATE]), bvm_col=b_in[2 * DQK_M:_O_GATE].reshape(-1, 1),
        bg=row(jnp.pad(b_in[_O_GATE:_O_OG], (0, gate_pad))),
        bqa=row(b_in[_O_QA:_O_KA]),
        bkk=row(jnp.concatenate([b_k, _swap_halves(b_k, 0)])),
        bv_row=row(b_v), bv_col=b_v.reshape(-1, 1),
        bog=row(b_in[_O_OG:_O_QA]), bgg=row(b_in[_O_GM:]),
        norm_w=row(mlstm_norm_w), norm_w_col=mlstm_norm_w.reshape(-1, 1),
        wbm=w_branch_m.astype(BF16), wba=w_branch_a.astype(BF16),
        wo=w_out.astype(BF16), ln1_g=row(ln1_g), ln1_b=row(ln1_b),
        wup=w_up.astype(BF16), bup=row(b_up), cw=conv_w, cbias=row(conv_b),
        wdn=w_down.astype(BF16), bdn=row(b_down), ln2_g=row(ln2_g), ln2_b=row(ln2_b))


def kernel(x_prompt, x_sample, c_prompt, c_sample, state_mlstm_C, state_mlstm_n, state_mlstm_m,
           cache_k_win, cache_v_win, state_ffn_conv, w_ada, b_ada, w_in, b_in, mlstm_norm_w,
           attn_sinks, w_branch_m, w_branch_a, w_out, ln1_g, ln1_b, w_up, b_up, conv_w, conv_b,
           w_down, b_down, ln2_g, ln2_b):
    depth = w_in.shape[0]
    bp, lp, _ = x_prompt.shape
    bs, ls, _ = x_sample.shape
    assert cache_k_win.shape[2] == WINDOW
    alpha = (2 * depth) ** 0.25
    dt = x_prompt.dtype
    slopes = jnp.exp2(-8.0 * jnp.arange(1, NH_A + 1, dtype=F32) / NH_A)
    tm_p = 512
    ns = bs * ls

    yp = x_prompt
    ys = x_sample.reshape(1, ns, D_MODEL)
    new_p, new_s = [], []
    n_c = bp + bs
    c_rows = -(-n_c // SUBLANES) * SUBLANES
    c_all = jnp.concatenate([c_sample, c_prompt, jnp.zeros((c_rows - n_c, D_MODEL), dt)], axis=0)
    for l in range(depth):
        w = _prep_weights(w_in[l], b_in[l], mlstm_norm_w[l], w_branch_m[l], w_branch_a[l], w_out[l],
                          ln1_g[l], ln1_b[l], w_up[l], b_up[l], conv_w[l], conv_b[l], w_down[l],
                          b_down[l], ln2_g[l], ln2_b[l])
        prm = jnp.stack([slopes, attn_sinks[l].astype(F32)])
        mod = _ada(c_all, w_ada[l], b_ada[l])
        mod_s = mod[:bs].reshape(bs, N_MOD, D_MODEL)
        mod_p = mod[bs:bs + bp].reshape(bp, N_MOD, D_MODEL)
        mp_ = [mod_p[:, j:j + 1] for j in range(N_MOD)]
        ms_seq = [jnp.repeat(mod_s[:, j], ls, axis=0)[None] for j in range(3)]
        ms_pos = [jnp.tile(mod_s[:, j], (ls, 1))[None] for j in range(3, N_MOD)]

        qk, v, g, qa, kk, ka, vat = _inproj(yp, mp_[0], mp_[1], w, tm_p, True)
        hm, cp, np_, mp = _mlstm_prompt(qk, v, g, jnp.zeros((bp, NH_M, DHV_M, DHK_M), dt),
                                        jnp.zeros((bp, NH_M, DHK_M), dt), jnp.zeros((bp, 1, LANES), dt),
                                        w["norm_w_col"])
        ha = _swa_prompt(prm, qa, kk, vat)
        x1p = _merge(yp, mp_[0], mp_[1], mp_[2], hm, ha, w, tm_p, alpha)
        halo_p = SUBLANES
        yp, csp = _ffn(x1p, mp_[3], mp_[4], mp_[5], jnp.zeros((bp, halo_p, 2 * D_FF), dt),
                       w, tm_p, 1, alpha)
        p_k = ka[:, lp - WINDOW:].reshape(bp, WINDOW, NKV_A, HD_A)
        p_v = vat[:, :, lp - WINDOW:].reshape(bp, NKV_A, HD_A, WINDOW).transpose(0, 3, 1, 2)
        new_p.append((cp, np_, mp[:, 0, :NH_M], p_k, p_v, csp[:, halo_p - (CONV_W - 1):]))

        qk, v, g, qa, kn, vn = _inproj(ys, ms_seq[0], ms_seq[1], w, ns, False)
        per_seq = lambda a: a.reshape(bs, ls, a.shape[-1])
        m0 = jnp.pad(state_mlstm_m[l], ((0, 0), (0, LANES - NH_M)))[:, None, :]
        hm, cs_, ns_, ms = _mlstm_sample(per_seq(qk), per_seq(v), per_seq(g), state_mlstm_C[l],
                                         state_mlstm_n[l], m0, w["norm_w"], SEQS_PER_STEP)
        to_t = lambda a: a.transpose(0, 2, 3, 1).reshape(bs, DKV_A, WINDOW)
        from_t = lambda a: a.reshape(bs, NKV_A, HD_A, WINDOW).transpose(0, 3, 1, 2)
        ha, kct, vct = _swa_sample(prm, per_seq(qa), per_seq(kn), per_seq(vn),
                                   to_t(cache_k_win[l]), to_t(cache_v_win[l]), SEQS_PER_STEP)
        x1s = _merge(ys, ms_seq[0], ms_seq[1], ms_seq[2], hm.reshape(1, ns, DV_M),
                     ha.reshape(1, ns, DQ_A), w, ns, alpha)
        to_pos = lambda a: a.reshape(bs, -1, a.shape[-1]).transpose(1, 0, 2).reshape(1, -1, a.shape[-1])
        y_pos, css = _ffn(to_pos(x1s[0]), ms_pos[0], ms_pos[1], ms_pos[2], to_pos(state_ffn_conv[l]),
                          w, ns, bs, alpha)
        ys = y_pos.reshape(ls, bs, D_MODEL).transpose(1, 0, 2).reshape(1, ns, D_MODEL)
        conv_s = css.reshape(CONV_W - 1, bs, 2 * D_FF).transpose(1, 0, 2)
        new_s.append((cs_, ns_, ms[:, 0, :NH_M], from_t(kct), from_t(vct), conv_s))

    p_state = [jnp.stack(a) for a in zip(*new_p)]
    s_state = [jnp.stack(a) for a in zip(*new_s)]
    return (yp, ys.reshape(bs, ls, D_MODEL), *p_state, *s_state)
```

```python
import functools

import jax
import jax.numpy as jnp
from jax import lax
from jax.experimental import pallas as pl
from jax.experimental.pallas import tpu as pltpu

F32 = jnp.float32
BF16 = jnp.bfloat16

D_MODEL = 1024
NH_M, DHK_M, DHV_M = 4, 128, 256
DQK_M, DV_M = NH_M * DHK_M, NH_M * DHV_M
NH_A, NKV_A, HD_A = 16, 4, 64
GROUP_A = NH_A // NKV_A
WINDOW = 128
DQ_A, DKV_A = NH_A * HD_A, NKV_A * HD_A
D_FF = 2816
CONV_W = 3
N_MOD = 6
LN_EPS = 1e-5
CHUNK = 128
NEG = -1e30
LANES = 128
SUBLANES = 8
VMEM_LIMIT = 56 * 1024 * 1024
SEQS_PER_STEP = 8


def _ln(x):
    mu = jnp.mean(x, axis=-1, keepdims=True)
    xc = x - mu
    var = jnp.mean(xc * xc, axis=-1, keepdims=True)
    return xc * lax.rsqrt(var + LN_EPS)


def _dot(a, b):
    return jnp.dot(a, b, preferred_element_type=F32)


def _dot_nt(a, b):
    return lax.dot_general(a, b, (((1,), (1,)), ((), ())), preferred_element_type=F32)


def _dot_tn(a, b):
    return lax.dot_general(a, b, (((0,), (0,)), ((), ())), preferred_element_type=F32)


def _const_spec(shape):
    nd = len(shape)
    return pl.BlockSpec(shape, lambda *_: (0,) * nd, pipeline_mode=pl.Buffered(1))


def _params(n_grid):
    return pltpu.CompilerParams(dimension_semantics=("arbitrary",) * n_grid,
                                vmem_limit_bytes=VMEM_LIMIT)


def _first_step():
    return (pl.program_id(0) == 0) & (pl.program_id(1) == 0)


def _ada_body(c_ref, w_ref, b_ref, o_ref):
    c = c_ref[...]
    s = (c * jax.nn.sigmoid(c)).astype(BF16)
    o_ref[...] = _dot(s, w_ref[...].astype(BF16)) + b_ref[...]


def _ada(c, w_ada, b_ada):
    rows = c.shape[0]
    n_out = w_ada.shape[1]
    bn = 1536
    assert n_out % bn == 0
    return pl.pallas_call(
        _ada_body,
        out_shape=jax.ShapeDtypeStruct((rows, n_out), F32),
        grid=(n_out // bn,),
        in_specs=[pl.BlockSpec((rows, D_MODEL), lambda j: (0, 0)),
                  pl.BlockSpec((D_MODEL, bn), lambda j: (0, j)),
                  pl.BlockSpec((1, bn), lambda j: (0, j))],
        out_specs=pl.BlockSpec((rows, bn), lambda j: (0, j)),
        compiler_params=_params(1),
        name="ada",
    )(c, w_ada, b_ada.reshape(1, n_out))


def _inproj_body(prompt, x_ref, sh_ref, sc_ref, wqk_ref, bqk_ref, wvm_ref, bvm_ref, wg_ref, bg_ref,
                 wqa_ref, bqa_ref, wk_ref, bk_ref, wv_ref, bv_ref, h_ref, qk_ref, v_ref, g_ref, qa_ref, *rest):
    h = (_ln(x_ref[...]) * (1.0 + sc_ref[...]) + sh_ref[...]).astype(BF16)
    h_ref[...] = h

    def proj(w_ref, b_ref, lo, n):
        return _dot(h, w_ref[:, lo:lo + n]) + b_ref[:, lo:lo + n]

    qk_ref[:, :DQK_M] = (proj(wqk_ref, bqk_ref, 0, DQK_M) * DHK_M ** -0.5).astype(qk_ref.dtype)
    qk_ref[:, DQK_M:] = proj(wqk_ref, bqk_ref, DQK_M, DQK_M).astype(qk_ref.dtype)
    if prompt:
        v_ref[...] = (_dot_nt(wvm_ref[...], h) + bvm_ref[...]).astype(v_ref.dtype)
    else:
        v_ref[...] = proj(wvm_ref, bvm_ref, 0, DV_M).astype(v_ref.dtype)
    g_ref[...] = proj(wg_ref, bg_ref, 0, LANES)
    qa_ref[...] = (proj(wqa_ref, bqa_ref, 0, DQ_A) * HD_A ** -0.5).astype(qa_ref.dtype)
    ka = proj(wk_ref, bk_ref, 0, DKV_A)
    if prompt:
        kb_ref, ka_ref, vat_ref = rest
        kb_ref[...] = ka.astype(BF16)
        ka_ref[...] = ka
        vat_ref[...] = _dot_nt(wv_ref[...], h) + bv_ref[...]
    else:
        ka_ref, va_ref = rest
        ka_ref[...] = ka
        va_ref[...] = _dot(h, wv_ref[...]) + bv_ref[...]


def _tok_spec(tm, n):
    return pl.BlockSpec((None, tm, n), lambda b, i: (b, i, 0))


def _mod_spec(rows, tm):
    if rows == 1:
        return pl.BlockSpec((None, 1, D_MODEL), lambda b, i: (b, 0, 0))
    return pl.BlockSpec((None, tm, D_MODEL), lambda b, i: (b, i, 0))


def _inproj(x, sh, sc, w, tm, prompt):
    bsz, seq, _ = x.shape
    mrows = sh.shape[1]
    act = BF16 if prompt else F32
    t_spec = lambda n: pl.BlockSpec((None, n, tm), lambda b, i: (b, 0, i))
    outs = [((bsz, seq, D_MODEL), BF16, _tok_spec(tm, D_MODEL)),
            ((bsz, seq, 2 * DQK_M), act, _tok_spec(tm, 2 * DQK_M)),
            ((bsz, DV_M, seq), act, t_spec(DV_M)) if prompt else ((bsz, seq, DV_M), act, _tok_spec(tm, DV_M)),
            ((bsz, seq, LANES), F32, _tok_spec(tm, LANES)),
            ((bsz, seq, DQ_A), act, _tok_spec(tm, DQ_A))]
    if prompt:
        wvm, bvm, wv, bv = w["wvm_t"], w["bvm_col"], w["wv_t"], w["bv_col"]
        outs += [((bsz, seq, DKV_A), BF16, _tok_spec(tm, DKV_A)),
                 ((bsz, seq, DKV_A), F32, _tok_spec(tm, DKV_A)),
                 ((bsz, DKV_A, seq), F32, t_spec(DKV_A))]
    else:
        wvm, bvm, wv, bv = w["wvm"], w["bvm_row"], w["wv"], w["bv_row"]
        outs += [((bsz, seq, DKV_A), F32, _tok_spec(tm, DKV_A)),
                 ((bsz, seq, DKV_A), F32, _tok_spec(tm, DKV_A))]
    weights = [w["wqk"], w["bqk"], wvm, bvm, w["wg"], w["bg"], w["wqa"], w["bqa"], w["wk"], w["bk"],
               wv, bv]
    return pl.pallas_call(
        functools.partial(_inproj_body, prompt),
        out_shape=[jax.ShapeDtypeStruct(s, dt) for s, dt, _ in outs],
        grid=(bsz, seq // tm),
        in_specs=[_tok_spec(tm, D_MODEL), _mod_spec(mrows, tm), _mod_spec(mrows, tm)]
                 + [_const_spec(a.shape) for a in weights],
        out_specs=[spec for _, _, spec in outs],
        compiler_params=_params(2),
        name="inproj",
    )(x, sh, sc, *weights)


def _scan_rows(x, op, rows):
    row = lax.broadcasted_iota(jnp.int32, x.shape, 0)
    d = 1
    while d < rows:
        shifted = pltpu.roll(x, d, axis=0)
        x = jnp.where(row >= d, op(x, shifted), x)
        d *= 2
    return x


def _mlstm_sample_body(t_in, nb, qk_ref, v_ref, g_ref, c0_ref, n0_ref, m0_ref, nw_ref,
                       h_ref, c_ref, n_ref, m_ref, *pads):
    t = SUBLANES
    c_ref[...] = c0_ref[...]
    n_ref[...] = n0_ref[...]
    m_ref[...] = m0_ref[...]

    @pl.when(_first_step())
    def _():
        for p in pads:
            p[...] = jnp.zeros(p.shape, p.dtype)
    for src, dst in zip((qk_ref, v_ref, g_ref), pads):
        dst[:, pl.ds(0, t_in), :] = src[...]
    qk_src, v_src, g_src = pads

    row = lax.broadcasted_iota(jnp.int32, (t, LANES), 0)
    lane = lax.broadcasted_iota(jnp.int32, (t, LANES), 1)
    lane1 = lax.broadcasted_iota(jnp.int32, (1, LANES), 1)
    r2 = lax.broadcasted_iota(jnp.int32, (t, t), 0)
    c2 = lax.broadcasted_iota(jnp.int32, (t, t), 1)
    causal = c2 <= r2
    pad_gate = jnp.where(lane < NH_M, NEG, -NEG)

    seqs = []
    for s in range(nb):
        ga = _gate_algebra(jnp.where(row < t_in, g_src[s], pad_gate), m_ref[s], t)
        ga["gt"] = ga["gg"].T
        m_ref[s] = jnp.where(lane1 < NH_M, ga["m_new"], 0.0)
        seqs.append(ga)

    units = [(s, hd) for s in range(nb) for hd in range(NH_M)]
    nu = len(units)
    col = lambda name, s, hd: seqs[s][name][:, hd:hd + 1]
    q = [qk_src[s, :, hd * DHK_M:(hd + 1) * DHK_M].astype(BF16) for s, hd in units]
    k = [qk_src[s, :, DQK_M + hd * DHK_M:DQK_M + (hd + 1) * DHK_M].astype(BF16) for s, hd in units]
    v = [v_src[s, :, hd * DHV_M:(hd + 1) * DHV_M].astype(BF16) for s, hd in units]
    c_old = [c_ref[s, hd] for s, hd in units]
    n_old = [n_ref[s, hd:hd + 1, :] for s, hd in units]

    sqk = [_dot_nt(q[u], k[u]) for u in range(nu)]
    dexp = [jnp.exp(jnp.where(causal, col("eb", s, hd) + seqs[s]["gt"][hd:hd + 1, :], NEG))
            for s, hd in units]
    smat = [sqk[u] * dexp[u] for u in range(nu)]
    intra = [_dot(smat[u].astype(BF16), v[u]) for u in range(nu)]
    inter = [_dot_nt(q[u], c_old[u].astype(BF16)) for u in range(nu)]
    qn = [jnp.sum(q[u].astype(F32) * n_old[u].astype(BF16).astype(F32), axis=-1, keepdims=True)
          for u in range(nu)]
    den = [jnp.sum(smat[u], axis=-1, keepdims=True) + col("a_in", s, hd) * qn[u]
           for u, (s, hd) in enumerate(units)]
    hh = [(intra[u] + col("a_in", s, hd) * inter[u])
          / jnp.maximum(jnp.abs(den[u]), col("lowb", s, hd)) for u, (s, hd) in enumerate(units)]
    hn = [_ln(hh[u]) * nw_ref[:, hd * DHV_M:(hd + 1) * DHV_M] for u, (s, hd) in enumerate(units)]
    for u, (s, hd) in enumerate(units):
        h_ref[s, :, hd * DHV_M:(hd + 1) * DHV_M] = hn[u][:t_in]

    kw = [k[u].astype(F32) * col("ws", s, hd) for u, (s, hd) in enumerate(units)]
    upd = [_dot_tn(v[u], kw[u].astype(BF16)) for u in range(nu)]
    for u, (s, hd) in enumerate(units):
        dec = seqs[s]["decay"][:, hd:hd + 1]
        c_ref[s, hd] = dec * c_old[u] + upd[u]
        n_ref[s, hd:hd + 1, :] = dec * n_old[u] + jnp.sum(kw[u], axis=0, keepdims=True)


def _mlstm_sample(qk, v, g, c0, n0, m0, norm_w, nb):
    nseq, t_in, _ = qk.shape
    assert t_in <= SUBLANES
    blk = lambda n: pl.BlockSpec((nb, t_in, n), lambda o, c: (o, 0, 0))
    st_c = pl.BlockSpec((nb, NH_M, DHV_M, DHK_M), lambda o, c: (o, 0, 0, 0))
    st_n = pl.BlockSpec((nb, NH_M, DHK_M), lambda o, c: (o, 0, 0))
    st_m = pl.BlockSpec((nb, 1, LANES), lambda o, c: (o, 0, 0))
    return pl.pallas_call(
        functools.partial(_mlstm_sample_body, t_in, nb),
        out_shape=[jax.ShapeDtypeStruct((nseq, t_in, DV_M), F32),
                   jax.ShapeDtypeStruct((nseq, NH_M, DHV_M, DHK_M), F32),
                   jax.ShapeDtypeStruct((nseq, NH_M, DHK_M), F32),
                   jax.ShapeDtypeStruct((nseq, 1, LANES), F32)],
        grid=(nseq // nb, 1),
        in_specs=[blk(2 * DQK_M), blk(DV_M), blk(LANES), st_c, st_n, st_m,
                  pl.BlockSpec((1, DV_M), lambda o, c: (0, 0))],
        out_specs=[blk(DV_M), st_c, st_n, st_m],
        scratch_shapes=[pltpu.VMEM((nb, SUBLANES, 2 * DQK_M), F32), pltpu.VMEM((nb, SUBLANES, DV_M), F32),
                        pltpu.VMEM((nb, SUBLANES, LANES), F32)],
        compiler_params=_params(2),
        name="mlstm_sample",
    )(qk, v, g, c0, n0, m0, norm_w)


def _gate_algebra(g, m_prev, t):
    b = pltpu.roll(_scan_rows(jax.nn.log_sigmoid(g), jnp.add, t), LANES - NH_M, axis=1)
    gg = g - b
    gmx = _scan_rows(gg, jnp.maximum, t)
    a = b + m_prev
    mt = jnp.maximum(a, b + gmx)
    b_last, gmx_last = b[t - 1:t, :], gmx[t - 1:t, :]
    m_new = jnp.maximum(b_last + m_prev, b_last + gmx_last)
    return dict(gg=gg, eb=b - mt, a_in=jnp.exp(a - mt), lowb=jnp.exp(-mt),
                ws=jnp.exp(b_last + gg - m_new), decay=jnp.exp(b_last + m_prev - m_new), m_new=m_new)


MLSTM_CHUNKS_PER_STEP = 4


def _mlstm_prompt_body(nb, qk_ref, vt_ref, g_ref, c0_ref, n0_ref, m0_ref, nwc_ref,
                       h_ref, c_ref, n_ref, m_ref, nwb):
    @pl.when(pl.program_id(1) == 0)
    def _():
        c_ref[...] = c0_ref[...]
        n_ref[...] = n0_ref[...]
        m_ref[...] = m0_ref[...]

    @pl.when(_first_step())
    def _():
        for hd in range(NH_M):
            nwb[hd] = jnp.broadcast_to(nwc_ref[hd * DHV_M:(hd + 1) * DHV_M, :], (DHV_M, CHUNK))

    for ci in range(MLSTM_CHUNKS_PER_STEP):
        _mlstm_prompt_chunk(nb, slice(ci * CHUNK, (ci + 1) * CHUNK), qk_ref, vt_ref, g_ref, nwb,
                            h_ref, c_ref, n_ref, m_ref)


def _mlstm_prompt_chunk(nb, rows, qk_ref, vt_ref, g_ref, nwb, h_ref, c_ref, n_ref, m_ref):
    t = CHUNK
    lane1 = lax.broadcasted_iota(jnp.int32, (1, LANES), 1)
    r2 = lax.broadcasted_iota(jnp.int32, (t, t), 0)
    c2 = lax.broadcasted_iota(jnp.int32, (t, t), 1)
    causal = r2 <= c2

    seqs = []
    for s in range(nb):
        ga = _gate_algebra(g_ref[s, rows, :], m_ref[s], t)
        m_ref[s] = jnp.where(lane1 < NH_M, ga["m_new"], 0.0)
        for name in ("eb", "a_in", "lowb", "ws"):
            ga[name + "_t"] = ga[name].T
        seqs.append(ga)

    units = [(s, hd) for s in range(nb) for hd in range(NH_M)]
    nu = len(units)
    rowv = lambda name, s, hd: seqs[s][name + "_t"][hd:hd + 1, :]
    q = [qk_ref[s, rows, hd * DHK_M:(hd + 1) * DHK_M] for s, hd in units]
    k = [qk_ref[s, rows, DQK_M + hd * DHK_M:DQK_M + (hd + 1) * DHK_M] for s, hd in units]
    vt = [vt_ref[s, hd * DHV_M:(hd + 1) * DHV_M, rows] for s, hd in units]
    c_old = [c_ref[s, hd] for s, hd in units]
    n_old = [n_ref[s, hd:hd + 1, :] for s, hd in units]

    skq = [_dot_nt(k[u], q[u]) for u in range(nu)]
    dexp = [jnp.exp(jnp.where(causal, seqs[s]["gg"][:, hd:hd + 1] + rowv("eb", s, hd), NEG))
            for s, hd in units]
    smat = [skq[u] * dexp[u] for u in range(nu)]
    intra = [_dot(vt[u], smat[u].astype(BF16)) for u in range(nu)]
    inter = [_dot_nt(c_old[u].astype(BF16), q[u]) for u in range(nu)]
    qn = [_dot_nt(jnp.broadcast_to(n_old[u], (SUBLANES, DHK_M)).astype(BF16), q[u])[0:1, :]
          for u in range(nu)]
    den = [jnp.sum(smat[u], axis=0, keepdims=True) + rowv("a_in", s, hd) * qn[u]
           for u, (s, hd) in enumerate(units)]
    inv = [1.0 / jnp.maximum(jnp.abs(den[u]), rowv("lowb", s, hd)) for u, (s, hd) in enumerate(units)]
    hh = [(intra[u] + rowv("a_in", s, hd) * inter[u]) * inv[u] for u, (s, hd) in enumerate(units)]
    mu = [jnp.mean(hh[u], axis=0, keepdims=True) for u in range(nu)]
    xc = [hh[u] - mu[u] for u in range(nu)]
    var = [jnp.mean(xc[u] * xc[u], axis=0, keepdims=True) for u in range(nu)]
    hn = [xc[u] * lax.rsqrt(var[u] + LN_EPS) * nwb[hd] for u, (s, hd) in enumerate(units)]
    for u, (s, hd) in enumerate(units):
        h_ref[s, rows, hd * DHV_M:(hd + 1) * DHV_M] = hn[u].T

    vw = [(vt[u].astype(F32) * rowv("ws", s, hd)).astype(BF16) for u, (s, hd) in enumerate(units)]
    upd = [_dot(vw[u], k[u]) for u in range(nu)]
    nupd = [_dot(jnp.broadcast_to(rowv("ws", s, hd), (SUBLANES, t)).astype(BF16), k[u])[0:1, :]
            for u, (s, hd) in enumerate(units)]
    for u, (s, hd) in enumerate(units):
        dec = seqs[s]["decay"][:, hd:hd + 1]
        c_ref[s, hd] = dec * c_old[u] + upd[u]
        n_ref[s, hd:hd + 1, :] = dec * n_old[u] + nupd[u]


def _mlstm_prompt(qk, vt, g, c0, n0, m0, norm_w_col):
    nb, seq, _ = qk.shape
    ts = MLSTM_CHUNKS_PER_STEP * CHUNK
    tok = lambda n: pl.BlockSpec((nb, ts, n), lambda o, c: (0, c, 0))
    st_c = pl.BlockSpec((nb, NH_M, DHV_M, DHK_M), lambda o, c: (0, 0, 0, 0))
    st_n = pl.BlockSpec((nb, NH_M, DHK_M), lambda o, c: (0, 0, 0))
    st_m = pl.BlockSpec((nb, 1, LANES), lambda o, c: (0, 0, 0))
    return pl.pallas_call(
        functools.partial(_mlstm_prompt_body, nb),
        out_shape=[jax.ShapeDtypeStruct((nb, seq, DV_M), F32),
                   jax.ShapeDtypeStruct((nb, NH_M, DHV_M, DHK_M), F32),
                   jax.ShapeDtypeStruct((nb, NH_M, DHK_M), F32),
                   jax.ShapeDtypeStruct((nb, 1, LANES), F32)],
        grid=(1, seq // ts),
        in_specs=[tok(2 * DQK_M), pl.BlockSpec((nb, DV_M, ts), lambda o, c: (0, 0, c)), tok(LANES),
                  st_c, st_n, st_m, pl.BlockSpec((DV_M, 1), lambda o, c: (0, 0))],
        out_specs=[tok(DV_M), st_c, st_n, st_m],
        scratch_shapes=[pltpu.VMEM((NH_M, DHV_M, CHUNK), F32)],
        compiler_params=_params(2),
        name="mlstm_prompt",
    )(qk, vt, g, c0, n0, m0, norm_w_col)


SWA_BLOCKS_PER_STEP = 8


def _swa_prompt_body(prm_ref, q_ref, kc_ref, kp_ref, vc_ref, vp_ref, o_ref, tbl):
    row = lax.broadcasted_iota(jnp.int32, (WINDOW, WINDOW), 0)
    col = lax.broadcasted_iota(jnp.int32, (WINDOW, WINDOW), 1)
    tri = row <= col

    @pl.when(_first_step())
    def _():
        dist = jnp.where(tri, col - row, col - row + WINDOW).astype(F32)
        for hd in range(NH_A):
            bias = prm_ref[0, hd] * dist
            tbl[0, hd] = bias
            tbl[1, hd] = bias + jnp.where(tri, 0.0, -NEG)

    for b in range(SWA_BLOCKS_PER_STEP):
        rows = pl.ds(b * WINDOW, WINDOW)
        which = jnp.where(pl.program_id(1) == 0, 1, 0) if b == 0 else 0
        k_prev = kp_ref if b == 0 else kc_ref.at[pl.ds((b - 1) * WINDOW, WINDOW), :]
        v_prev = vp_ref if b == 0 else vc_ref.at[:, pl.ds((b - 1) * WINDOW, WINDOW)]
        _swa_prompt_block(which, prm_ref, q_ref.at[rows, :], kc_ref.at[rows, :], k_prev,
                          vc_ref.at[:, rows], v_prev, o_ref.at[rows, :], tbl)


def _swa_prompt_block(which, prm_ref, q_ref, kc_ref, kp_ref, vc_ref, vp_ref, o_ref, tbl):
    row = lax.broadcasted_iota(jnp.int32, (WINDOW, WINDOW), 0)
    col = lax.broadcasted_iota(jnp.int32, (WINDOW, WINDOW), 1)
    tri = row <= col
    lo = col < HD_A
    zb = jnp.zeros((WINDOW, LANES), BF16)

    def placed(k_ref, kv):
        c, par = kv // 2, kv % 2
        own = k_ref[:, c * LANES:(c + 1) * LANES]
        swp = pltpu.roll(own, HD_A, axis=1)
        if par == 0:
            return jnp.where(lo, own, zb), jnp.where(lo, zb, swp)
        return jnp.where(lo, swp, zb), jnp.where(lo, zb, own)

    st = []
    for kv in range(NKV_A):
        lhs = jnp.concatenate([*placed(kc_ref, kv), *placed(kp_ref, kv)], axis=0)
        qg = jnp.concatenate([q_ref[:, (2 * kv) * LANES:(2 * kv + 1) * LANES],
                              q_ref[:, (2 * kv + 1) * LANES:(2 * kv + 2) * LANES]], axis=0)
        st.append(_dot_nt(lhs, qg))
    tiles = []
    for hd in range(NH_A):
        kv, a, par = hd // GROUP_A, (hd % GROUP_A) // 2, hd % 2
        cols = slice(a * WINDOW, (a + 1) * WINDOW)
        tiles.append(jnp.where(tri, st[kv][par * WINDOW:(par + 1) * WINDOW, cols],
                               st[kv][(2 + par) * WINDOW:(3 + par) * WINDOW, cols]))
    sc = jnp.concatenate(tiles, axis=0).reshape(NH_A, WINDOW, WINDOW) - tbl[which]
    sink = jnp.concatenate([jnp.full((1, 1, WINDOW), prm_ref[1, hd], F32) for hd in range(NH_A)], axis=0)
    mx = jnp.maximum(jnp.max(sc, axis=1, keepdims=True), sink)
    p = jnp.exp(sc - mx)
    den = jnp.sum(p, axis=1, keepdims=True) + jnp.exp(sink - mx)
    pn = p * (1.0 / den)

    zero = jnp.zeros((WINDOW, WINDOW), F32)
    z64 = jnp.zeros((HD_A, WINDOW), BF16)
    outs = []
    for kv in range(NKV_A):
        cols = []
        for a in range(2):
            pe, po = pn[kv * GROUP_A + 2 * a], pn[kv * GROUP_A + 2 * a + 1]
            cols.append(jnp.concatenate([jnp.where(tri, pe, zero), jnp.where(tri, po, zero),
                                         jnp.where(tri, zero, pe), jnp.where(tri, zero, po)],
                                        axis=0).astype(BF16))
        pt = jnp.concatenate(cols, axis=1)
        vc = vc_ref[kv * HD_A:(kv + 1) * HD_A, :].astype(BF16)
        vp = vp_ref[kv * HD_A:(kv + 1) * HD_A, :].astype(BF16)
        vt = jnp.concatenate([jnp.concatenate([vc, z64], axis=0), jnp.concatenate([z64, vc], axis=0),
                              jnp.concatenate([vp, z64], axis=0), jnp.concatenate([z64, vp], axis=0)],
                             axis=1)
        ot = _dot(vt, pt)
        outs += [ot[:, :WINDOW].T, ot[:, WINDOW:].T]
    o_ref[...] = jnp.concatenate(outs, axis=1).astype(o_ref.dtype)


def _swa_prompt(prm, q, kb, vat):
    bsz, seq, _ = q.shape
    tq = SWA_BLOCKS_PER_STEP * WINDOW
    prev = lambda i: jnp.maximum(i * SWA_BLOCKS_PER_STEP - 1, 0)
    return pl.pallas_call(
        _swa_prompt_body,
        out_shape=jax.ShapeDtypeStruct((bsz, seq, DQ_A), BF16),
        grid=(bsz, seq // tq),
        in_specs=[pl.BlockSpec(memory_space=pltpu.SMEM),
                  pl.BlockSpec((None, tq, DQ_A), lambda b, i: (b, i, 0)),
                  pl.BlockSpec((None, tq, DKV_A), lambda b, i: (b, i, 0)),
                  pl.BlockSpec((None, WINDOW, DKV_A), lambda b, i: (b, prev(i), 0)),
                  pl.BlockSpec((None, DKV_A, tq), lambda b, i: (b, 0, i)),
                  pl.BlockSpec((None, DKV_A, WINDOW), lambda b, i: (b, 0, prev(i)))],
        out_specs=pl.BlockSpec((None, tq, DQ_A), lambda b, i: (b, i, 0)),
        scratch_shapes=[pltpu.VMEM((2, NH_A, WINDOW, WINDOW), F32)],
        compiler_params=_params(2),
        name="swa_prompt",
    )(prm, q, kb, kb, vat, vat)


def _half_mask(shape, half):
    lane = lax.broadcasted_iota(jnp.int32, shape, 1)
    return lane < HD_A if half == 0 else lane >= HD_A


def _swa_sample_body(t_in, nb, prm_ref, q_ref, kn_ref, vn_ref, kct_ref, vct_ref,
                     o_ref, kco_ref, vco_ref, tbl, q8, kn_pad, vn_pad):
    tq = SUBLANES
    rows = NH_A * tq
    row = lax.broadcasted_iota(jnp.int32, (rows, WINDOW), 0)
    col = lax.broadcasted_iota(jnp.int32, (rows, WINDOW), 1)
    tri = col <= (row & (tq - 1))

    @pl.when(_first_step())
    def _():
        r8 = lax.broadcasted_iota(jnp.int32, (tq, WINDOW), 0)
        c8 = lax.broadcasted_iota(jnp.int32, (tq, WINDOW), 1)
        dist = jnp.where(c8 <= r8, r8 - c8, r8 - c8 + WINDOW).astype(F32)
        for hd in range(NH_A):
            tbl[pl.ds(hd * tq, tq), :] = prm_ref[0, hd] * dist
        for p in (q8, kn_pad, vn_pad):
            p[...] = jnp.zeros(p.shape, p.dtype)

    q8[:, pl.ds(0, t_in), :] = q_ref[...]
    kn_pad[:, pl.ds(0, t_in), :] = kn_ref[...]
    vn_pad[:, pl.ds(0, t_in), :] = vn_ref[...]
    sink_col = jnp.concatenate([jnp.full((tq, 1), prm_ref[1, hd], F32) for hd in range(NH_A)], axis=0)
    z8 = jnp.zeros((tq, LANES), F32)
    n_chunk = NKV_A // 2
    heads_per_chunk = NH_A // n_chunk

    def place(piece, src_half, dst_half):
        if src_half != dst_half:
            piece = pltpu.roll(piece, HD_A, axis=1)
        return jnp.where(_half_mask(piece.shape, dst_half), piece, z8)

    qexp = []
    for s in range(nb):
        per_c = []
        for c in range(n_chunk):
            pieces = []
            for hl in range(heads_per_chunk):
                hd = c * heads_per_chunk + hl
                pieces.append(place(q8[s, :, (hd // 2) * LANES:(hd // 2 + 1) * LANES],
                                    hd % 2, hl // GROUP_A))
            per_c.append(jnp.concatenate(pieces, axis=0).astype(BF16))
        qexp.append(per_c)
    csl = lambda c: slice(c * LANES, (c + 1) * LANES)
    s_prev = [[_dot(qexp[s][c], kct_ref[s, csl(c), :].astype(BF16)) for c in range(n_chunk)]
              for s in range(nb)]
    s_cur = [[_dot_nt(qexp[s][c], kn_pad[s, :, csl(c)].astype(BF16)) for c in range(n_chunk)]
             for s in range(nb)]
    sc = [jnp.where(tri, jnp.concatenate(s_cur[s], axis=0), jnp.concatenate(s_prev[s], axis=0)) - tbl[...]
          for s in range(nb)]
    mx = [jnp.maximum(jnp.max(sc[s], axis=-1, keepdims=True), sink_col) for s in range(nb)]
    p = [jnp.exp(sc[s] - mx[s]) for s in range(nb)]
    den = [jnp.sum(p[s], axis=-1, keepdims=True) + jnp.exp(sink_col - mx[s]) for s in range(nb)]
    pn = [p[s] * (1.0 / den[s]) for s in range(nb)]
    zero = jnp.zeros((rows, WINDOW), F32)
    pc = [jnp.where(tri, pn[s], zero).astype(BF16) for s in range(nb)]
    pp = [jnp.where(tri, zero, pn[s]).astype(BF16) for s in range(nb)]
    half_rows = heads_per_chunk * tq
    oc = [[_dot_nt(pp[s][c * half_rows:(c + 1) * half_rows], vct_ref[s, csl(c), :].astype(BF16))
           + _dot(pc[s][c * half_rows:(c + 1) * half_rows], vn_pad[s, :, csl(c)].astype(BF16))
           for c in range(n_chunk)] for s in range(nb)]
    for s in range(nb):
        chunks = []
        for pch in range(NH_A // 2):
            acc = None
            for hd in (2 * pch, 2 * pch + 1):
                c, hl = hd // heads_per_chunk, hd % heads_per_chunk
                piece = place(oc[s][c][hl * tq:(hl + 1) * tq, :], hl // GROUP_A, hd % 2)
                acc = piece if acc is None else acc + piece
            chunks.append(acc)
        o_ref[s] = jnp.concatenate(chunks, axis=1)[:t_in]

    lane = lax.broadcasted_iota(jnp.int32, (DKV_A, WINDOW), 1)
    for new_pad, old_ref, out_ref in ((kn_pad, kct_ref, kco_ref), (vn_pad, vct_ref, vco_ref)):
        for s in range(nb):
            merged = jnp.where(lane < t_in, new_pad[s].T, old_ref[s])
            out_ref[s] = pltpu.roll(merged, WINDOW - t_in, axis=1)


def _swa_sample(prm, q, kn, vn, kct, vct, nb):
    nseq, t_in, _ = q.shape
    assert t_in <= SUBLANES
    cur = lambda n: pl.BlockSpec((nb, t_in, n), lambda o, i: (o, 0, 0))
    win = pl.BlockSpec((nb, DKV_A, WINDOW), lambda o, i: (o, 0, 0))
    return pl.pallas_call(
        functools.partial(_swa_sample_body, t_in, nb),
        out_shape=[jax.ShapeDtypeStruct((nseq, t_in, DQ_A), F32),
                   jax.ShapeDtypeStruct((nseq, DKV_A, WINDOW), F32),
                   jax.ShapeDtypeStruct((nseq, DKV_A, WINDOW), F32)],
        grid=(nseq // nb, 1),
        in_specs=[pl.BlockSpec(memory_space=pltpu.SMEM), cur(DQ_A), cur(DKV_A), cur(DKV_A), win, win],
        out_specs=[cur(DQ_A), win, win],
        scratch_shapes=[pltpu.VMEM((NH_A * SUBLANES, WINDOW), F32),
                        pltpu.VMEM((nb, SUBLANES, DQ_A), F32),
                        pltpu.VMEM((nb, WINDOW, DKV_A), F32),
                        pltpu.VMEM((nb, WINDOW, DKV_A), F32)],
        compiler_params=_params(2),
        name="swa_sample",
    )(prm, q, kn, vn, kct, vct)


def _merge_body(alpha, x_ref, h_ref, g1_ref, hm_ref, ha_ref, wog_ref, bog_ref,
                wgg_ref, bgg_ref, wbm_ref, wba_ref, wo_ref, lg_ref, lb_ref, o_ref):
    nsub = 2
    sub = x_ref.shape[0] // nsub
    rows = [pl.ds(r * sub, sub) for r in range(nsub)]
    mod = lambda ref, r: ref[...] if ref.shape[0] == 1 else ref[rows[r], :]

    def branch_gate(h, j):
        cols = slice(j * D_MODEL, (j + 1) * D_MODEL)
        return jax.nn.sigmoid(_dot(h, wgg_ref[:, cols]) + bgg_ref[:, cols])

    x = [x_ref[rows[r], :] for r in range(nsub)]
    h = [h_ref[rows[r], :] for r in range(nsub)]
    og =[jax.nn.sigmoid(_dot(h[r], wog_ref[...]) + bog_ref[...]) for r in range(nsub)]
    hm = [(hm_ref[rows[r], :] * og[r]).astype(BF16) for r in range(nsub)]
    bm = [_dot(hm[r], wbm_ref[...]) for r in range(nsub)]
    gm = [branch_gate(h[r], 0) for r in range(nsub)]
    ba = [_dot(ha_ref[rows[r], :].astype(BF16), wba_ref[...]) for r in range(nsub)]
    ga = [branch_gate(h[r], 1) for r in range(nsub)]
    merged = [(gm[r] * bm[r] + ga[r] * ba[r]).astype(BF16) for r in range(nsub)]
    mo = [_dot(merged[r], wo_ref[...]) for r in range(nsub)]
    for r in range(nsub):
        o_ref[rows[r], :] = _ln(alpha * x[r] + mod(g1_ref, r) * mo[r]) * lg_ref[...] + lb_ref[...]


def _merge(x, h, g1, hm, ha, w, tm, alpha):
    bsz, seq, _ = x.shape
    mrows = g1.shape[1]
    weights = [w["wog"], w["bog"], w["wgg"], w["bgg"], w["wbm"], w["wba"], w["wo"],
               w["ln1_g"], w["ln1_b"]]
    return pl.pallas_call(
        functools.partial(_merge_body, alpha),
        out_shape=jax.ShapeDtypeStruct((bsz, seq, D_MODEL), F32),
        grid=(bsz, seq // tm),
        in_specs=[_tok_spec(tm, D_MODEL), _tok_spec(tm, D_MODEL), _mod_spec(mrows, tm),
                  _tok_spec(tm, DV_M), _tok_spec(tm, DQ_A)]
                 + [_const_spec(a.shape) for a in weights],
        out_specs=_tok_spec(tm, D_MODEL),
        compiler_params=_params(2),
        name="merge",
    )(x, h, g1, hm, ha, *weights)


def _ffn_body(alpha, tm, stride, halo, x_ref, sh_ref, sc_ref, g2_ref, cb0_ref, wup_ref, bup_ref,
              cw_ref, cbias_ref, wdn_ref, bdn_ref, lg_ref, lb_ref, o_ref, cs_ref, ubuf, act):
    @pl.when(pl.program_id(1) == 0)
    def _():
        cs_ref[...] = cb0_ref[...]

    nsub = 2
    sub = tm // nsub
    rows = [pl.ds(r * sub, sub) for r in range(nsub)]
    mod = lambda ref, r: ref[...] if ref.shape[0] == 1 else ref[rows[r], :]
    x = [x_ref[rows[r], :] for r in range(nsub)]
    h = [(_ln(x[r]) * (1.0 + mod(sc_ref, r)) + mod(sh_ref, r)).astype(BF16) for r in range(nsub)]
    ys = [[None, None] for _ in range(nsub)]
    for half in range(2):
        cols = slice(half * D_FF, (half + 1) * D_FF)
        u = [_dot(h[r], wup_ref[:, cols]) + bup_ref[:, cols] for r in range(nsub)]
        ubuf[pl.ds(0, halo), :] = cs_ref[:, cols]
        for r in range(nsub):
            ubuf[pl.ds(halo + r * sub, sub), :] = u[r]
        cs_ref[:, cols] = ubuf[pl.ds(tm, halo), :]
        for r in range(nsub):
            y = cbias_ref[:, cols] + u[r] * cw_ref[CONV_W - 1:CONV_W, cols]
            for j in range(CONV_W - 1):
                tap = ubuf[pl.ds(halo + r * sub - (CONV_W - 1 - j) * stride, sub), :]
                y = y + tap * cw_ref[j:j + 1, cols]
            ys[r][half] = y
    for r in range(nsub):
        act[rows[r], :] = (jax.nn.gelu(ys[r][0]) * ys[r][1]).astype(BF16)
    f = [_dot(act[rows[r], :], wdn_ref[...]) + bdn_ref[...] for r in range(nsub)]
    for r in range(nsub):
        o_ref[rows[r], :] = _ln(alpha * x[r] + mod(g2_ref, r) * f[r]) * lg_ref[...] + lb_ref[...]


def _ffn(x, sh, sc, g2, cb0, w, tm, stride, alpha):
    bsz, seq, _ = x.shape
    mrows = sh.shape[1]
    halo = cb0.shape[1]
    cs = pl.BlockSpec((None, halo, 2 * D_FF), lambda b, i: (b, 0, 0))
    weights = [w["wup"], w["bup"], w["cw"], w["cbias"], w["wdn"], w["bdn"], w["ln2_g"], w["ln2_b"]]
    return pl.pallas_call(
        functools.partial(_ffn_body, alpha, tm, stride, halo),
        out_shape=[jax.ShapeDtypeStruct((bsz, seq, D_MODEL), F32),
                   jax.ShapeDtypeStruct((bsz, halo, 2 * D_FF), F32)],
        grid=(bsz, seq // tm),
        in_specs=[_tok_spec(tm, D_MODEL), _mod_spec(mrows, tm), _mod_spec(mrows, tm),
                  _mod_spec(mrows, tm), cs] + [_const_spec(a.shape) for a in weights],
        out_specs=[_tok_spec(tm, D_MODEL), cs],
        scratch_shapes=[pltpu.VMEM((halo + tm, D_FF), F32), pltpu.VMEM((tm, D_FF), BF16)],
        compiler_params=_params(2),
        name="ffn",
    )(x, sh, sc, g2, cb0, *weights)


_O_GATE = 2 * DQK_M + DV_M
_O_OG = _O_GATE + 2 * NH_M
_O_QA = _O_OG + DV_M
_O_KA = _O_QA + DQ_A
_O_VA = _O_KA + DKV_A
_O_GM = _O_VA + DKV_A
_PREP_ROWS = 128


def _split_w_in_body(wt_ref, qk_ref, vm_ref, vmt_ref, g_ref, og_ref, qa_ref, k_ref, v_ref, vt_ref, gg_ref):
    piece = lambda lo, hi: wt_ref[lo:hi, :]
    qk_ref[...] = piece(0, 2 * DQK_M).T.astype(BF16)
    vm = piece(2 * DQK_M, _O_GATE)
    vmt_ref[...] = vm.astype(BF16)
    vm_ref[...] = vm.T.astype(BF16)
    pad = jnp.zeros((LANES - 2 * NH_M, _PREP_ROWS), F32)
    g_ref[...] = jnp.concatenate([piece(_O_GATE, _O_OG), pad], axis=0).T.astype(BF16)
    og_ref[...] = piece(_O_OG, _O_QA).T.astype(BF16)
    qa_ref[...] = piece(_O_QA, _O_KA).T.astype(BF16)
    k_ref[...] = piece(_O_KA, _O_VA).T.astype(BF16)
    v = piece(_O_VA, _O_GM)
    vt_ref[...] = v.astype(BF16)
    v_ref[...] = v.T.astype(BF16)
    gg_ref[...] = piece(_O_GM, wt_ref.shape[0]).T.astype(BF16)


def _split_w_in(w_in_t):
    d_in = w_in_t.shape[0]
    rows = lambda n: ((D_MODEL, n), pl.BlockSpec((_PREP_ROWS, n), lambda i: (i, 0)))
    cols = lambda n: ((n, D_MODEL), pl.BlockSpec((n, _PREP_ROWS), lambda i: (0, i)))
    outs = dict(wqk=rows(2 * DQK_M), wvm=rows(DV_M), wvm_t=cols(DV_M), wg=rows(LANES), wog=rows(DV_M),
                wqa=rows(DQ_A), wk=rows(DKV_A), wv=rows(DKV_A), wv_t=cols(DKV_A),
                wgg=rows(d_in - _O_GM))
    res = pl.pallas_call(
        _split_w_in_body,
        out_shape=[jax.ShapeDtypeStruct(s, BF16) for s, _ in outs.values()],
        grid=(D_MODEL // _PREP_ROWS,),
        in_specs=[pl.BlockSpec((d_in, _PREP_ROWS), lambda i: (0, i))],
        out_specs=[spec for _, spec in outs.values()],
        compiler_params=_params(1),
        name="split_w_in",
    )(w_in_t)
    return dict(zip(outs.keys(), res))


def _prep_weights(w_in, b_in, mlstm_norm_w, w_branch_m, w_branch_a, w_out, ln1_g, ln1_b,
                  w_up, b_up, conv_w, conv_b, w_down, b_down, ln2_g, ln2_b):
    row = lambda a: a.reshape(1, -1)
    gate_pad = LANES - 2 * NH_M
    b_k, b_v = b_in[_O_KA:_O_VA], b_in[_O_VA:_O_GM]
    return dict(
        **_split_w_in(w_in.T),
        bqk=row(b_in[:2 * DQK_M]),
        bvm_row=row(b_in[2 * DQK_M:_O_GATE]), bvm_col=b_in[2 * DQK_M:_O_GATE].reshape(-1, 1),
        bg=row(jnp.pad(b_in[_O_GATE:_O_OG], (0, gate_pad))),
        bqa=row(b_in[_O_QA:_O_KA]),
        bk=row(b_k),
        bv_row=row(b_v), bv_col=b_v.reshape(-1, 1),
        bog=row(b_in[_O_OG:_O_QA]), bgg=row(b_in[_O_GM:]),
        norm_w=row(mlstm_norm_w), norm_w_col=mlstm_norm_w.reshape(-1, 1),
        wbm=w_branch_m.astype(BF16), wba=w_branch_a.astype(BF16),
        wo=w_out.astype(BF16), ln1_g=row(ln1_g), ln1_b=row(ln1_b),
        wup=w_up.astype(BF16), bup=row(b_up), cw=conv_w, cbias=row(conv_b),
        wdn=w_down.astype(BF16), bdn=row(b_down), ln2_g=row(ln2_g), ln2_b=row(ln2_b))


def kernel(x_prompt, x_sample, c_prompt, c_sample, state_mlstm_C, state_mlstm_n, state_mlstm_m,
           cache_k_win, cache_v_win, state_ffn_conv, w_ada, b_ada, w_in, b_in, mlstm_norm_w,
           attn_sinks, w_branch_m, w_branch_a, w_out, ln1_g, ln1_b, w_up, b_up, conv_w, conv_b,
           w_down, b_down, ln2_g, ln2_b):
    depth = w_in.shape[0]
    bp, lp, _ = x_prompt.shape
    bs, ls, _ = x_sample.shape
    assert cache_k_win.shape[2] == WINDOW
    alpha = (2 * depth) ** 0.25
    dt = x_prompt.dtype
    slopes = jnp.exp2(-8.0 * jnp.arange(1, NH_A + 1, dtype=F32) / NH_A)
    tm_p = 512
    ns = bs * ls

    yp = x_prompt
    ys = x_sample.reshape(1, ns, D_MODEL)
    new_p, new_s = [], []
    n_c = bp + bs
    c_rows = -(-n_c // SUBLANES) * SUBLANES
    c_all = jnp.concatenate([c_sample, c_prompt, jnp.zeros((c_rows - n_c, D_MODEL), dt)], axis=0)
    for l in range(depth):
        w = _prep_weights(w_in[l], b_in[l], mlstm_norm_w[l], w_branch_m[l], w_branch_a[l], w_out[l],
                          ln1_g[l], ln1_b[l], w_up[l], b_up[l], conv_w[l], conv_b[l], w_down[l],
                          b_down[l], ln2_g[l], ln2_b[l])
        prm = jnp.stack([slopes, attn_sinks[l].astype(F32)])
        mod = _ada(c_all, w_ada[l], b_ada[l])
        mod_s = mod[:bs].reshape(bs, N_MOD, D_MODEL)
        mod_p = mod[bs:bs + bp].reshape(bp, N_MOD, D_MODEL)
        mp_ = [mod_p[:, j:j + 1] for j in range(N_MOD)]
        ms_seq = [jnp.repeat(mod_s[:, j], ls, axis=0)[None] for j in range(3)]
        ms_pos = [jnp.tile(mod_s[:, j], (ls, 1))[None] for j in range(3, N_MOD)]

        hp, qk, v, g, qa, kb, ka, vat = _inproj(yp, mp_[0], mp_[1], w, tm_p, True)
        hm, cp, np_, mp = _mlstm_prompt(qk, v, g, jnp.zeros((bp, NH_M, DHV_M, DHK_M), dt),
                                        jnp.zeros((bp, NH_M, DHK_M), dt), jnp.zeros((bp, 1, LANES), dt),
                                        w["norm_w_col"])
        ha = _swa_prompt(prm, qa, kb, vat)
        x1p = _merge(yp, hp, mp_[2], hm, ha, w, tm_p, alpha)
        halo_p = SUBLANES
        yp, csp = _ffn(x1p, mp_[3], mp_[4], mp_[5], jnp.zeros((bp, halo_p, 2 * D_FF), dt),
                       w, tm_p, 1, alpha)
        p_k = ka[:, lp - WINDOW:].reshape(bp, WINDOW, NKV_A, HD_A)
        p_v = vat[:, :, lp - WINDOW:].reshape(bp, NKV_A, HD_A, WINDOW).transpose(0, 3, 1, 2)
        new_p.append((cp, np_, mp[:, 0, :NH_M], p_k, p_v, csp[:, halo_p - (CONV_W - 1):]))

        hs, qk, v, g, qa, kn, vn = _inproj(ys, ms_seq[0], ms_seq[1], w, ns, False)
        per_seq = lambda a: a.reshape(bs, ls, a.shape[-1])
        m0 = jnp.pad(state_mlstm_m[l], ((0, 0), (0, LANES - NH_M)))[:, None, :]
        hm, cs_, ns_, ms = _mlstm_sample(per_seq(qk), per_seq(v), per_seq(g), state_mlstm_C[l],
                                         state_mlstm_n[l], m0, w["norm_w"], SEQS_PER_STEP)
        to_t = lambda a: a.transpose(0, 2, 3, 1).reshape(bs, DKV_A, WINDOW)
        from_t = lambda a: a.reshape(bs, NKV_A, HD_A, WINDOW).transpose(0, 3, 1, 2)
        ha, kct, vct = _swa_sample(prm, per_seq(qa), per_seq(kn), per_seq(vn),
                                   to_t(cache_k_win[l]), to_t(cache_v_win[l]), SEQS_PER_STEP)
        x1s = _merge(ys, hs, ms_seq[2], hm.reshape(1, ns, DV_M), ha.reshape(1, ns, DQ_A), w, ns, alpha)
        to_pos = lambda a: a.reshape(bs, -1, a.shape[-1]).transpose(1, 0, 2).reshape(1, -1, a.shape[-1])
        y_pos, css = _ffn(to_pos(x1s[0]), ms_pos[0], ms_pos[1], ms_pos[2], to_pos(state_ffn_conv[l]),
                          w, ns, bs, alpha)
        ys = y_pos.reshape(ls, bs, D_MODEL).transpose(1, 0, 2).reshape(1, ns, D_MODEL)
        conv_s = css.reshape(CONV_W - 1, bs, 2 * D_FF).transpose(1, 0, 2)
        new_s.append((cs_, ns_, ms[:, 0, :NH_M], from_t(kct), from_t(vct), conv_s))

    p_state = [jnp.stack(a) for a in zip(*new_p)]
    s_state = [jnp.stack(a) for a in zip(*new_s)]
    return (yp, ys.reshape(bs, ls, D_MODEL), *p_state, *s_state)
```

```python
import functools

import jax
import jax.numpy as jnp
from jax import lax
from jax.experimental import pallas as pl
from jax.experimental.pallas import tpu as pltpu

F32 = jnp.float32
BF16 = jnp.bfloat16

D_MODEL = 1024
NH_M, DHK_M, DHV_M = 4, 128, 256
DQK_M, DV_M = NH_M * DHK_M, NH_M * DHV_M
NH_A, NKV_A, HD_A = 16, 4, 64
GROUP_A = NH_A // NKV_A
WINDOW = 128
DQ_A, DKV_A = NH_A * HD_A, NKV_A * HD_A
D_FF = 2816
CONV_W = 3
N_MOD = 6
LN_EPS = 1e-5
CHUNK = 128
NEG = -1e30
LANES = 128
SUBLANES = 8
VMEM_LIMIT = 56 * 1024 * 1024
SEQS_PER_STEP = 8


def _ln(x):
    mu = jnp.mean(x, axis=-1, keepdims=True)
    xc = x - mu
    var = jnp.mean(xc * xc, axis=-1, keepdims=True)
    return xc * lax.rsqrt(var + LN_EPS)


def _dot(a, b):
    return jnp.dot(a, b, preferred_element_type=F32)


def _dot_nt(a, b):
    return lax.dot_general(a, b, (((1,), (1,)), ((), ())), preferred_element_type=F32)


def _dot_tn(a, b):
    return lax.dot_general(a, b, (((0,), (0,)), ((), ())), preferred_element_type=F32)


def _const_spec(shape):
    nd = len(shape)
    return pl.BlockSpec(shape, lambda *_: (0,) * nd, pipeline_mode=pl.Buffered(1))


def _params(n_grid):
    return pltpu.CompilerParams(dimension_semantics=("arbitrary",) * n_grid,
                                vmem_limit_bytes=VMEM_LIMIT)


def _first_step():
    return (pl.program_id(0) == 0) & (pl.program_id(1) == 0)


def _ada_body(c_ref, w_ref, b_ref, o_ref):
    c = c_ref[...]
    s = (c * jax.nn.sigmoid(c)).astype(BF16)
    o_ref[...] = _dot(s, w_ref[...].astype(BF16)) + b_ref[...]


def _ada(c, w_ada, b_ada):
    rows = c.shape[0]
    n_out = w_ada.shape[1]
    bn = 1536
    assert n_out % bn == 0
    return pl.pallas_call(
        _ada_body,
        out_shape=jax.ShapeDtypeStruct((rows, n_out), F32),
        grid=(n_out // bn,),
        in_specs=[pl.BlockSpec((rows, D_MODEL), lambda j: (0, 0)),
                  pl.BlockSpec((D_MODEL, bn), lambda j: (0, j)),
                  pl.BlockSpec((1, bn), lambda j: (0, j))],
        out_specs=pl.BlockSpec((rows, bn), lambda j: (0, j)),
        compiler_params=_params(1),
        name="ada",
    )(c, w_ada, b_ada.reshape(1, n_out))


def _inproj_body(prompt, x_ref, sh_ref, sc_ref, wqk_ref, bqk_ref, wvm_ref, bvm_ref, wg_ref, bg_ref,
                 wqa_ref, bqa_ref, wk_ref, bk_ref, wv_ref, bv_ref, h_ref, qk_ref, v_ref, g_ref, qa_ref, *rest):
    h = (_ln(x_ref[...]) * (1.0 + sc_ref[...]) + sh_ref[...]).astype(BF16)
    h_ref[...] = h

    def proj(w_ref, b_ref, lo, n):
        return _dot(h, w_ref[:, lo:lo + n]) + b_ref[:, lo:lo + n]

    qk_ref[:, :DQK_M] = (proj(wqk_ref, bqk_ref, 0, DQK_M) * DHK_M ** -0.5).astype(qk_ref.dtype)
    qk_ref[:, DQK_M:] = proj(wqk_ref, bqk_ref, DQK_M, DQK_M).astype(qk_ref.dtype)
    if prompt:
        v_ref[...] = (_dot_nt(wvm_ref[...], h) + bvm_ref[...]).astype(v_ref.dtype)
    else:
        v_ref[...] = proj(wvm_ref, bvm_ref, 0, DV_M).astype(v_ref.dtype)
    g_ref[...] = proj(wg_ref, bg_ref, 0, LANES)
    qa_ref[...] = (proj(wqa_ref, bqa_ref, 0, DQ_A) * HD_A ** -0.5).astype(qa_ref.dtype)
    ka = proj(wk_ref, bk_ref, 0, DKV_A)
    if prompt:
        kb_ref, ka_ref, vat_ref = rest
        kb_ref[...] = ka.astype(BF16)
        ka_ref[...] = ka
        vat_ref[...] = _dot_nt(wv_ref[...], h) + bv_ref[...]
    else:
        ka_ref, va_ref = rest
        ka_ref[...] = ka
        va_ref[...] = _dot(h, wv_ref[...]) + bv_ref[...]


def _tok_spec(tm, n):
    return pl.BlockSpec((None, tm, n), lambda b, i: (b, i, 0))


def _mod_spec(rows, tm):
    if rows == 1:
        return pl.BlockSpec((None, 1, D_MODEL), lambda b, i: (b, 0, 0))
    return pl.BlockSpec((None, tm, D_MODEL), lambda b, i: (b, i, 0))


def _inproj(x, sh, sc, w, tm, prompt):
    bsz, seq, _ = x.shape
    mrows = sh.shape[1]
    act = BF16 if prompt else F32
    t_spec = lambda n: pl.BlockSpec((None, n, tm), lambda b, i: (b, 0, i))
    outs = [((bsz, seq, D_MODEL), BF16, _tok_spec(tm, D_MODEL)),
            ((bsz, seq, 2 * DQK_M), act, _tok_spec(tm, 2 * DQK_M)),
            ((bsz, DV_M, seq), act, t_spec(DV_M)) if prompt else ((bsz, seq, DV_M), act, _tok_spec(tm, DV_M)),
            ((bsz, seq, LANES), F32, _tok_spec(tm, LANES)),
            ((bsz, seq, DQ_A), act, _tok_spec(tm, DQ_A))]
    if prompt:
        wvm, bvm, wv, bv = w["wvm_t"], w["bvm_col"], w["wv_t"], w["bv_col"]
        outs += [((bsz, seq, DKV_A), BF16, _tok_spec(tm, DKV_A)),
                 ((bsz, seq, DKV_A), F32, _tok_spec(tm, DKV_A)),
                 ((bsz, DKV_A, seq), F32, t_spec(DKV_A))]
    else:
        wvm, bvm, wv, bv = w["wvm"], w["bvm_row"], w["wv"], w["bv_row"]
        outs += [((bsz, seq, DKV_A), F32, _tok_spec(tm, DKV_A)),
                 ((bsz, seq, DKV_A), F32, _tok_spec(tm, DKV_A))]
    weights = [w["wqk"], w["bqk"], wvm, bvm, w["wg"], w["bg"], w["wqa"], w["bqa"], w["wk"], w["bk"],
               wv, bv]
    return pl.pallas_call(
        functools.partial(_inproj_body, prompt),
        out_shape=[jax.ShapeDtypeStruct(s, dt) for s, dt, _ in outs],
        grid=(bsz, seq // tm),
        in_specs=[_tok_spec(tm, D_MODEL), _mod_spec(mrows, tm), _mod_spec(mrows, tm)]
                 + [_const_spec(a.shape) for a in weights],
        out_specs=[spec for _, _, spec in outs],
        compiler_params=_params(2),
        name="inproj",
    )(x, sh, sc, *weights)


def _scan_rows(x, op, rows):
    row = lax.broadcasted_iota(jnp.int32, x.shape, 0)
    d = 1
    while d < rows:
        shifted = pltpu.roll(x, d, axis=0)
        x = jnp.where(row >= d, op(x, shifted), x)
        d *= 2
    return x


def _mlstm_sample_body(t_in, nb, qk_ref, v_ref, g_ref, c0_ref, n0_ref, m0_ref, nw_ref,
                       h_ref, c_ref, n_ref, m_ref, *pads):
    t = SUBLANES
    c_ref[...] = c0_ref[...]
    n_ref[...] = n0_ref[...]
    m_ref[...] = m0_ref[...]

    @pl.when(_first_step())
    def _():
        for p in pads:
            p[...] = jnp.zeros(p.shape, p.dtype)
    for src, dst in zip((qk_ref, v_ref, g_ref), pads):
        for s in range(nb):
            dst[s, pl.ds(0, t_in), :] = src[:, s, :]
    qk_src, v_src, g_src = pads

    row = lax.broadcasted_iota(jnp.int32, (t, LANES), 0)
    lane = lax.broadcasted_iota(jnp.int32, (t, LANES), 1)
    lane1 = lax.broadcasted_iota(jnp.int32, (1, LANES), 1)
    r2 = lax.broadcasted_iota(jnp.int32, (t, t), 0)
    c2 = lax.broadcasted_iota(jnp.int32, (t, t), 1)
    causal = c2 <= r2
    pad_gate = jnp.where(lane < NH_M, NEG, -NEG)

    seqs = []
    for s in range(nb):
        ga = _gate_algebra(jnp.where(row < t_in, g_src[s], pad_gate), m_ref[s], t)
        ga["gt"] = ga["gg"].T
        m_ref[s] = jnp.where(lane1 < NH_M, ga["m_new"], 0.0)
        seqs.append(ga)

    units = [(s, hd) for s in range(nb) for hd in range(NH_M)]
    nu = len(units)
    col = lambda name, s, hd: seqs[s][name][:, hd:hd + 1]
    q = [qk_src[s, :, hd * DHK_M:(hd + 1) * DHK_M].astype(BF16) for s, hd in units]
    k = [qk_src[s, :, DQK_M + hd * DHK_M:DQK_M + (hd + 1) * DHK_M].astype(BF16) for s, hd in units]
    v = [v_src[s, :, hd * DHV_M:(hd + 1) * DHV_M].astype(BF16) for s, hd in units]
    c_old = [c_ref[s, hd] for s, hd in units]
    n_old = [n_ref[s, hd:hd + 1, :] for s, hd in units]

    sqk = [_dot_nt(q[u], k[u]) for u in range(nu)]
    dexp = [jnp.exp(jnp.where(causal, col("eb", s, hd) + seqs[s]["gt"][hd:hd + 1, :], NEG))
            for s, hd in units]
    smat = [sqk[u] * dexp[u] for u in range(nu)]
    intra = [_dot(smat[u].astype(BF16), v[u]) for u in range(nu)]
    inter = [_dot_nt(q[u], c_old[u].astype(BF16)) for u in range(nu)]
    qn = [jnp.sum(q[u].astype(F32) * n_old[u].astype(BF16).astype(F32), axis=-1, keepdims=True)
          for u in range(nu)]
    den = [jnp.sum(smat[u], axis=-1, keepdims=True) + col("a_in", s, hd) * qn[u]
           for u, (s, hd) in enumerate(units)]
    hh = [(intra[u] + col("a_in", s, hd) * inter[u])
          / jnp.maximum(jnp.abs(den[u]), col("lowb", s, hd)) for u, (s, hd) in enumerate(units)]
    hn = [_ln(hh[u]) * nw_ref[:, hd * DHV_M:(hd + 1) * DHV_M] for u, (s, hd) in enumerate(units)]
    for u, (s, hd) in enumerate(units):
        h_ref[:, s, hd * DHV_M:(hd + 1) * DHV_M] = hn[u][:t_in]

    kw = [k[u].astype(F32) * col("ws", s, hd) for u, (s, hd) in enumerate(units)]
    upd = [_dot_tn(v[u], kw[u].astype(BF16)) for u in range(nu)]
    for u, (s, hd) in enumerate(units):
        dec = seqs[s]["decay"][:, hd:hd + 1]
        c_ref[s, hd] = dec * c_old[u] + upd[u]
        n_ref[s, hd:hd + 1, :] = dec * n_old[u] + jnp.sum(kw[u], axis=0, keepdims=True)


def _mlstm_sample(qk, v, g, c0, n0, m0, norm_w, nb):
    t_in, nseq, _ = qk.shape
    assert t_in <= SUBLANES
    blk = lambda n: pl.BlockSpec((t_in, nb, n), lambda o, c: (0, o, 0))
    st_c = pl.BlockSpec((nb, NH_M, DHV_M, DHK_M), lambda o, c: (o, 0, 0, 0))
    st_n = pl.BlockSpec((nb, NH_M, DHK_M), lambda o, c: (o, 0, 0))
    st_m = pl.BlockSpec((nb, 1, LANES), lambda o, c: (o, 0, 0))
    return pl.pallas_call(
        functools.partial(_mlstm_sample_body, t_in, nb),
        out_shape=[jax.ShapeDtypeStruct((t_in, nseq, DV_M), F32),
                   jax.ShapeDtypeStruct((nseq, NH_M, DHV_M, DHK_M), F32),
                   jax.ShapeDtypeStruct((nseq, NH_M, DHK_M), F32),
                   jax.ShapeDtypeStruct((nseq, 1, LANES), F32)],
        grid=(nseq // nb, 1),
        in_specs=[blk(2 * DQK_M), blk(DV_M), blk(LANES), st_c, st_n, st_m,
                  pl.BlockSpec((1, DV_M), lambda o, c: (0, 0))],
        out_specs=[blk(DV_M), st_c, st_n, st_m],
        scratch_shapes=[pltpu.VMEM((nb, SUBLANES, 2 * DQK_M), F32), pltpu.VMEM((nb, SUBLANES, DV_M), F32),
                        pltpu.VMEM((nb, SUBLANES, LANES), F32)],
        compiler_params=_params(2),
        name="mlstm_sample",
    )(qk, v, g, c0, n0, m0, norm_w)


def _gate_algebra(g, m_prev, t):
    b = pltpu.roll(_scan_rows(jax.nn.log_sigmoid(g), jnp.add, t), LANES - NH_M, axis=1)
    gg = g - b
    gmx = _scan_rows(gg, jnp.maximum, t)
    a = b + m_prev
    mt = jnp.maximum(a, b + gmx)
    b_last, gmx_last = b[t - 1:t, :], gmx[t - 1:t, :]
    m_new = jnp.maximum(b_last + m_prev, b_last + gmx_last)
    return dict(gg=gg, eb=b - mt, a_in=jnp.exp(a - mt), lowb=jnp.exp(-mt),
                ws=jnp.exp(b_last + gg - m_new), decay=jnp.exp(b_last + m_prev - m_new), m_new=m_new)


MLSTM_CHUNKS_PER_STEP = 4


def _mlstm_prompt_body(nb, qk_ref, vt_ref, g_ref, c0_ref, n0_ref, m0_ref, nwc_ref,
                       h_ref, c_ref, n_ref, m_ref, nwb):
    @pl.when(pl.program_id(1) == 0)
    def _():
        c_ref[...] = c0_ref[...]
        n_ref[...] = n0_ref[...]
        m_ref[...] = m0_ref[...]

    @pl.when(_first_step())
    def _():
        for hd in range(NH_M):
            nwb[hd] = jnp.broadcast_to(nwc_ref[hd * DHV_M:(hd + 1) * DHV_M, :], (DHV_M, CHUNK))

    for ci in range(MLSTM_CHUNKS_PER_STEP):
        _mlstm_prompt_chunk(nb, slice(ci * CHUNK, (ci + 1) * CHUNK), qk_ref, vt_ref, g_ref, nwb,
                            h_ref, c_ref, n_ref, m_ref)


def _mlstm_prompt_chunk(nb, rows, qk_ref, vt_ref, g_ref, nwb, h_ref, c_ref, n_ref, m_ref):
    t = CHUNK
    lane1 = lax.broadcasted_iota(jnp.int32, (1, LANES), 1)
    r2 = lax.broadcasted_iota(jnp.int32, (t, t), 0)
    c2 = lax.broadcasted_iota(jnp.int32, (t, t), 1)
    causal = r2 <= c2

    seqs = []
    for s in range(nb):
        ga = _gate_algebra(g_ref[s, rows, :], m_ref[s], t)
        m_ref[s] = jnp.where(lane1 < NH_M, ga["m_new"], 0.0)
        for name in ("eb", "a_in", "lowb", "ws"):
            ga[name + "_t"] = ga[name].T
        seqs.append(ga)

    units = [(s, hd) for s in range(nb) for hd in range(NH_M)]
    nu = len(units)
    rowv = lambda name, s, hd: seqs[s][name + "_t"][hd:hd + 1, :]
    q = [qk_ref[s, rows, hd * DHK_M:(hd + 1) * DHK_M] for s, hd in units]
    k = [qk_ref[s, rows, DQK_M + hd * DHK_M:DQK_M + (hd + 1) * DHK_M] for s, hd in units]
    vt = [vt_ref[s, hd * DHV_M:(hd + 1) * DHV_M, rows] for s, hd in units]
    c_old = [c_ref[s, hd] for s, hd in units]
    n_old = [n_ref[s, hd:hd + 1, :] for s, hd in units]

    skq = [_dot_nt(k[u], q[u]) for u in range(nu)]
    dexp = [jnp.exp(jnp.where(causal, seqs[s]["gg"][:, hd:hd + 1] + rowv("eb", s, hd), NEG))
            for s, hd in units]
    smat = [skq[u] * dexp[u] for u in range(nu)]
    intra = [_dot(vt[u], smat[u].astype(BF16)) for u in range(nu)]
    inter = [_dot_nt(c_old[u].astype(BF16), q[u]) for u in range(nu)]
    qn = [_dot_nt(jnp.broadcast_to(n_old[u], (SUBLANES, DHK_M)).astype(BF16), q[u])[0:1, :]
          for u in range(nu)]
    den = [jnp.sum(smat[u], axis=0, keepdims=True) + rowv("a_in", s, hd) * qn[u]
           for u, (s, hd) in enumerate(units)]
    inv = [1.0 / jnp.maximum(jnp.abs(den[u]), rowv("lowb", s, hd)) for u, (s, hd) in enumerate(units)]
    hh = [(intra[u] + rowv("a_in", s, hd) * inter[u]) * inv[u] for u, (s, hd) in enumerate(units)]
    mu = [jnp.mean(hh[u], axis=0, keepdims=True) for u in range(nu)]
    xc = [hh[u] - mu[u] for u in range(nu)]
    var = [jnp.mean(xc[u] * xc[u], axis=0, keepdims=True) for u in range(nu)]
    hn = [xc[u] * lax.rsqrt(var[u] + LN_EPS) * nwb[hd] for u, (s, hd) in enumerate(units)]
    for u, (s, hd) in enumerate(units):
        h_ref[s, rows, hd * DHV_M:(hd + 1) * DHV_M] = hn[u].T

    vw = [(vt[u].astype(F32) * rowv("ws", s, hd)).astype(BF16) for u, (s, hd) in enumerate(units)]
    upd = [_dot(vw[u], k[u]) for u in range(nu)]
    nupd = [_dot(jnp.broadcast_to(rowv("ws", s, hd), (SUBLANES, t)).astype(BF16), k[u])[0:1, :]
            for u, (s, hd) in enumerate(units)]
    for u, (s, hd) in enumerate(units):
        dec = seqs[s]["decay"][:, hd:hd + 1]
        c_ref[s, hd] = dec * c_old[u] + upd[u]
        n_ref[s, hd:hd + 1, :] = dec * n_old[u] + nupd[u]


def _mlstm_prompt(qk, vt, g, c0, n0, m0, norm_w_col):
    nb, seq, _ = qk.shape
    ts = MLSTM_CHUNKS_PER_STEP * CHUNK
    tok = lambda n: pl.BlockSpec((nb, ts, n), lambda o, c: (0, c, 0))
    st_c = pl.BlockSpec((nb, NH_M, DHV_M, DHK_M), lambda o, c: (0, 0, 0, 0))
    st_n = pl.BlockSpec((nb, NH_M, DHK_M), lambda o, c: (0, 0, 0))
    st_m = pl.BlockSpec((nb, 1, LANES), lambda o, c: (0, 0, 0))
    return pl.pallas_call(
        functools.partial(_mlstm_prompt_body, nb),
        out_shape=[jax.ShapeDtypeStruct((nb, seq, DV_M), F32),
                   jax.ShapeDtypeStruct((nb, NH_M, DHV_M, DHK_M), F32),
                   jax.ShapeDtypeStruct((nb, NH_M, DHK_M), F32),
                   jax.ShapeDtypeStruct((nb, 1, LANES), F32)],
        grid=(1, seq // ts),
        in_specs=[tok(2 * DQK_M), pl.BlockSpec((nb, DV_M, ts), lambda o, c: (0, 0, c)), tok(LANES),
                  st_c, st_n, st_m, pl.BlockSpec((DV_M, 1), lambda o, c: (0, 0))],
        out_specs=[tok(DV_M), st_c, st_n, st_m],
        scratch_shapes=[pltpu.VMEM((NH_M, DHV_M, CHUNK), F32)],
        compiler_params=_params(2),
        name="mlstm_prompt",
    )(qk, vt, g, c0, n0, m0, norm_w_col)


SWA_BLOCKS_PER_STEP = 16


def _swa_prompt_body(prm_ref, q_ref, kc_ref, kp_ref, vc_ref, vp_ref, o_ref, tbl):
    row = lax.broadcasted_iota(jnp.int32, (WINDOW, WINDOW), 0)
    col = lax.broadcasted_iota(jnp.int32, (WINDOW, WINDOW), 1)
    tri = row <= col

    @pl.when(_first_step())
    def _():
        dist = jnp.where(tri, col - row, col - row + WINDOW).astype(F32)
        for hd in range(NH_A):
            bias = prm_ref[0, hd] * dist
            tbl[0, hd] = bias
            tbl[1, hd] = bias + jnp.where(tri, 0.0, -NEG)

    for b in range(SWA_BLOCKS_PER_STEP):
        rows = pl.ds(b * WINDOW, WINDOW)
        which = jnp.where(pl.program_id(1) == 0, 1, 0) if b == 0 else 0
        k_prev = kp_ref if b == 0 else kc_ref.at[pl.ds((b - 1) * WINDOW, WINDOW), :]
        v_prev = vp_ref if b == 0 else vc_ref.at[:, pl.ds((b - 1) * WINDOW, WINDOW)]
        _swa_prompt_block(which, prm_ref, q_ref.at[rows, :], kc_ref.at[rows, :], k_prev,
                          vc_ref.at[:, rows], v_prev, o_ref.at[rows, :], tbl)


def _swa_prompt_block(which, prm_ref, q_ref, kc_ref, kp_ref, vc_ref, vp_ref, o_ref, tbl):
    row = lax.broadcasted_iota(jnp.int32, (WINDOW, WINDOW), 0)
    col = lax.broadcasted_iota(jnp.int32, (WINDOW, WINDOW), 1)
    tri = row <= col
    lo = col < HD_A
    zb = jnp.zeros((WINDOW, LANES), BF16)

    def placed(k_ref, kv):
        c, par = kv // 2, kv % 2
        own = k_ref[:, c * LANES:(c + 1) * LANES]
        swp = pltpu.roll(own, HD_A, axis=1)
        if par == 0:
            return jnp.where(lo, own, zb), jnp.where(lo, zb, swp)
        return jnp.where(lo, swp, zb), jnp.where(lo, zb, own)

    st = []
    for kv in range(NKV_A):
        lhs = jnp.concatenate([*placed(kc_ref, kv), *placed(kp_ref, kv)], axis=0)
        qg = jnp.concatenate([q_ref[:, (2 * kv) * LANES:(2 * kv + 1) * LANES],
                              q_ref[:, (2 * kv + 1) * LANES:(2 * kv + 2) * LANES]], axis=0)
        st.append(_dot_nt(lhs, qg))
    tiles = []
    for hd in range(NH_A):
        kv, a, par = hd // GROUP_A, (hd % GROUP_A) // 2, hd % 2
        cols = slice(a * WINDOW, (a + 1) * WINDOW)
        tiles.append(jnp.where(tri, st[kv][par * WINDOW:(par + 1) * WINDOW, cols],
                               st[kv][(2 + par) * WINDOW:(3 + par) * WINDOW, cols]))
    sc = jnp.concatenate(tiles, axis=0).reshape(NH_A, WINDOW, WINDOW) - tbl[which]
    sink = jnp.concatenate([jnp.full((1, 1, WINDOW), prm_ref[1, hd], F32) for hd in range(NH_A)], axis=0)
    mx = jnp.maximum(jnp.max(sc, axis=1, keepdims=True), sink)
    p = jnp.exp(sc - mx)
    den = jnp.sum(p, axis=1, keepdims=True) + jnp.exp(sink - mx)
    pn = p * (1.0 / den)

    zero = jnp.zeros((WINDOW, WINDOW), F32)
    z64 = jnp.zeros((HD_A, WINDOW), BF16)
    outs = []
    for kv in range(NKV_A):
        cols = []
        for a in range(2):
            pe, po = pn[kv * GROUP_A + 2 * a], pn[kv * GROUP_A + 2 * a + 1]
            cols.append(jnp.concatenate([jnp.where(tri, pe, zero), jnp.where(tri, po, zero),
                                         jnp.where(tri, zero, pe), jnp.where(tri, zero, po)],
                                        axis=0).astype(BF16))
        pt = jnp.concatenate(cols, axis=1)
        vc = vc_ref[kv * HD_A:(kv + 1) * HD_A, :].astype(BF16)
        vp = vp_ref[kv * HD_A:(kv + 1) * HD_A, :].astype(BF16)
        vt = jnp.concatenate([jnp.concatenate([vc, z64], axis=0), jnp.concatenate([z64, vc], axis=0),
                              jnp.concatenate([vp, z64], axis=0), jnp.concatenate([z64, vp], axis=0)],
                             axis=1)
        ot = _dot(vt, pt)
        outs += [ot[:, :WINDOW].T, ot[:, WINDOW:].T]
    o_ref[...] = jnp.concatenate(outs, axis=1).astype(o_ref.dtype)


def _swa_prompt(prm, q, kb, vat):
    bsz, seq, _ = q.shape
    tq = SWA_BLOCKS_PER_STEP * WINDOW
    prev = lambda i: jnp.maximum(i * SWA_BLOCKS_PER_STEP - 1, 0)
    return pl.pallas_call(
        _swa_prompt_body,
        out_shape=jax.ShapeDtypeStruct((bsz, seq, DQ_A), BF16),
        grid=(bsz, seq // tq),
        in_specs=[pl.BlockSpec(memory_space=pltpu.SMEM),
                  pl.BlockSpec((None, tq, DQ_A), lambda b, i: (b, i, 0)),
                  pl.BlockSpec((None, tq, DKV_A), lambda b, i: (b, i, 0)),
                  pl.BlockSpec((None, WINDOW, DKV_A), lambda b, i: (b, prev(i), 0)),
                  pl.BlockSpec((None, DKV_A, tq), lambda b, i: (b, 0, i)),
                  pl.BlockSpec((None, DKV_A, WINDOW), lambda b, i: (b, 0, prev(i)))],
        out_specs=pl.BlockSpec((None, tq, DQ_A), lambda b, i: (b, i, 0)),
        scratch_shapes=[pltpu.VMEM((2, NH_A, WINDOW, WINDOW), F32)],
        compiler_params=_params(2),
        name="swa_prompt",
    )(prm, q, kb, kb, vat, vat)


def _half_mask(shape, half):
    lane = lax.broadcasted_iota(jnp.int32, shape, 1)
    return lane < HD_A if half == 0 else lane >= HD_A


def _swa_sample_body(t_in, nb, prm_ref, q_ref, kn_ref, vn_ref, kct_ref, vct_ref,
                     o_ref, kco_ref, vco_ref, tbl, q8, kn_pad, vn_pad):
    tq = SUBLANES
    rows = NH_A * tq
    row = lax.broadcasted_iota(jnp.int32, (rows, WINDOW), 0)
    col = lax.broadcasted_iota(jnp.int32, (rows, WINDOW), 1)
    tri = col <= (row & (tq - 1))

    @pl.when(_first_step())
    def _():
        r8 = lax.broadcasted_iota(jnp.int32, (tq, WINDOW), 0)
        c8 = lax.broadcasted_iota(jnp.int32, (tq, WINDOW), 1)
        dist = jnp.where(c8 <= r8, r8 - c8, r8 - c8 + WINDOW).astype(F32)
        for hd in range(NH_A):
            tbl[pl.ds(hd * tq, tq), :] = prm_ref[0, hd] * dist
        for p in (q8, kn_pad, vn_pad):
            p[...] = jnp.zeros(p.shape, p.dtype)

    for s in range(nb):
        q8[s, pl.ds(0, t_in), :] = q_ref[:, s, :]
        kn_pad[s, pl.ds(0, t_in), :] = kn_ref[:, s, :]
        vn_pad[s, pl.ds(0, t_in), :] = vn_ref[:, s, :]
    sink_col = jnp.concatenate([jnp.full((tq, 1), prm_ref[1, hd], F32) for hd in range(NH_A)], axis=0)
    z8 = jnp.zeros((tq, LANES), F32)
    n_chunk = NKV_A // 2
    heads_per_chunk = NH_A // n_chunk

    def place(piece, src_half, dst_half):
        if src_half != dst_half:
            piece = pltpu.roll(piece, HD_A, axis=1)
        return jnp.where(_half_mask(piece.shape, dst_half), piece, z8)

    qexp = []
    for s in range(nb):
        per_c = []
        for c in range(n_chunk):
            pieces = []
            for hl in range(heads_per_chunk):
                hd = c * heads_per_chunk + hl
                pieces.append(place(q8[s, :, (hd // 2) * LANES:(hd // 2 + 1) * LANES],
                                    hd % 2, hl // GROUP_A))
            per_c.append(jnp.concatenate(pieces, axis=0).astype(BF16))
        qexp.append(per_c)
    csl = lambda c: slice(c * LANES, (c + 1) * LANES)
    s_prev = [[_dot(qexp[s][c], kct_ref[s, csl(c), :].astype(BF16)) for c in range(n_chunk)]
              for s in range(nb)]
    s_cur = [[_dot_nt(qexp[s][c], kn_pad[s, :, csl(c)].astype(BF16)) for c in range(n_chunk)]
             for s in range(nb)]
    sc = [jnp.where(tri, jnp.concatenate(s_cur[s], axis=0), jnp.concatenate(s_prev[s], axis=0)) - tbl[...]
          for s in range(nb)]
    mx = [jnp.maximum(jnp.max(sc[s], axis=-1, keepdims=True), sink_col) for s in range(nb)]
    p = [jnp.exp(sc[s] - mx[s]) for s in range(nb)]
    den = [jnp.sum(p[s], axis=-1, keepdims=True) + jnp.exp(sink_col - mx[s]) for s in range(nb)]
    pn = [p[s] * (1.0 / den[s]) for s in range(nb)]
    zero = jnp.zeros((rows, WINDOW), F32)
    pc = [jnp.where(tri, pn[s], zero).astype(BF16) for s in range(nb)]
    pp = [jnp.where(tri, zero, pn[s]).astype(BF16) for s in range(nb)]
    half_rows = heads_per_chunk * tq
    oc = [[_dot_nt(pp[s][c * half_rows:(c + 1) * half_rows], vct_ref[s, csl(c), :].astype(BF16))
           + _dot(pc[s][c * half_rows:(c + 1) * half_rows], vn_pad[s, :, csl(c)].astype(BF16))
           for c in range(n_chunk)] for s in range(nb)]
    for s in range(nb):
        chunks = []
        for pch in range(NH_A // 2):
            acc = None
            for hd in (2 * pch, 2 * pch + 1):
                c, hl = hd // heads_per_chunk, hd % heads_per_chunk
                piece = place(oc[s][c][hl * tq:(hl + 1) * tq, :], hl // GROUP_A, hd % 2)
                acc = piece if acc is None else acc + piece
            chunks.append(acc)
        o_ref[:, s, :] = jnp.concatenate(chunks, axis=1)[:t_in]

    lane = lax.broadcasted_iota(jnp.int32, (DKV_A, WINDOW), 1)
    for new_pad, old_ref, out_ref in ((kn_pad, kct_ref, kco_ref), (vn_pad, vct_ref, vco_ref)):
        for s in range(nb):
            merged = jnp.where(lane < t_in, new_pad[s].T, old_ref[s])
            out_ref[s] = pltpu.roll(merged, WINDOW - t_in, axis=1)


def _swa_sample(prm, q, kn, vn, kct, vct, nb):
    t_in, nseq, _ = q.shape
    assert t_in <= SUBLANES
    cur = lambda n: pl.BlockSpec((t_in, nb, n), lambda o, i: (0, o, 0))
    win = pl.BlockSpec((nb, DKV_A, WINDOW), lambda o, i: (o, 0, 0))
    return pl.pallas_call(
        functools.partial(_swa_sample_body, t_in, nb),
        out_shape=[jax.ShapeDtypeStruct((t_in, nseq, DQ_A), F32),
                   jax.ShapeDtypeStruct((nseq, DKV_A, WINDOW), F32),
                   jax.ShapeDtypeStruct((nseq, DKV_A, WINDOW), F32)],
        grid=(nseq // nb, 1),
        in_specs=[pl.BlockSpec(memory_space=pltpu.SMEM), cur(DQ_A), cur(DKV_A), cur(DKV_A), win, win],
        out_specs=[cur(DQ_A), win, win],
        scratch_shapes=[pltpu.VMEM((NH_A * SUBLANES, WINDOW), F32),
                        pltpu.VMEM((nb, SUBLANES, DQ_A), F32),
                        pltpu.VMEM((nb, WINDOW, DKV_A), F32),
                        pltpu.VMEM((nb, WINDOW, DKV_A), F32)],
        compiler_params=_params(2),
        name="swa_sample",
    )(prm, q, kn, vn, kct, vct)


def _merge_body(alpha, x_ref, h_ref, g1_ref, hm_ref, ha_ref, wog_ref, bog_ref,
                wgg_ref, bgg_ref, wbm_ref, wba_ref, wo_ref, lg_ref, lb_ref, o_ref):
    nsub = 2
    sub = x_ref.shape[0] // nsub
    rows = [pl.ds(r * sub, sub) for r in range(nsub)]
    mod = lambda ref, r: ref[...] if ref.shape[0] == 1 else ref[rows[r], :]

    def branch_gate(h, j):
        cols = slice(j * D_MODEL, (j + 1) * D_MODEL)
        return jax.nn.sigmoid(_dot(h, wgg_ref[:, cols]) + bgg_ref[:, cols])

    x = [x_ref[rows[r], :] for r in range(nsub)]
    h = [h_ref[rows[r], :] for r in range(nsub)]
    og =[jax.nn.sigmoid(_dot(h[r], wog_ref[...]) + bog_ref[...]) for r in range(nsub)]
    hm = [(hm_ref[rows[r], :] * og[r]).astype(BF16) for r in range(nsub)]
    bm = [_dot(hm[r], wbm_ref[...]) for r in range(nsub)]
    gm = [branch_gate(h[r], 0) for r in range(nsub)]
    ba = [_dot(ha_ref[rows[r], :].astype(BF16), wba_ref[...]) for r in range(nsub)]
    ga = [branch_gate(h[r], 1) for r in range(nsub)]
    merged = [(gm[r] * bm[r] + ga[r] * ba[r]).astype(BF16) for r in range(nsub)]
    mo = [_dot(merged[r], wo_ref[...]) for r in range(nsub)]
    for r in range(nsub):
        o_ref[rows[r], :] = _ln(alpha * x[r] + mod(g1_ref, r) * mo[r]) * lg_ref[...] + lb_ref[...]


def _merge(x, h, g1, hm, ha, w, tm, alpha):
    bsz, seq, _ = x.shape
    mrows = g1.shape[1]
    weights = [w["wog"], w["bog"], w["wgg"], w["bgg"], w["wbm"], w["wba"], w["wo"],
               w["ln1_g"], w["ln1_b"]]
    return pl.pallas_call(
        functools.partial(_merge_body, alpha),
        out_shape=jax.ShapeDtypeStruct((bsz, seq, D_MODEL), F32),
        grid=(bsz, seq // tm),
        in_specs=[_tok_spec(tm, D_MODEL), _tok_spec(tm, D_MODEL), _mod_spec(mrows, tm),
                  _tok_spec(tm, DV_M), _tok_spec(tm, DQ_A)]
                 + [_const_spec(a.shape) for a in weights],
        out_specs=_tok_spec(tm, D_MODEL),
        compiler_params=_params(2),
        name="merge",
    )(x, h, g1, hm, ha, *weights)


def _ffn_body(alpha, tm, stride, halo, x_ref, sh_ref, sc_ref, g2_ref, cb0_ref, wup_ref, bup_ref,
              cw_ref, cbias_ref, wdn_ref, bdn_ref, lg_ref, lb_ref, o_ref, cs_ref, ubuf, act):
    @pl.when(pl.program_id(1) == 0)
    def _():
        cs_ref[...] = cb0_ref[...]

    nsub = 2
    sub = tm // nsub
    rows = [pl.ds(r * sub, sub) for r in range(nsub)]
    mod = lambda ref, r: ref[...] if ref.shape[0] == 1 else ref[rows[r], :]
    x = [x_ref[rows[r], :] for r in range(nsub)]
    h = [(_ln(x[r]) * (1.0 + mod(sc_ref, r)) + mod(sh_ref, r)).astype(BF16) for r in range(nsub)]
    ys = [[None, None] for _ in range(nsub)]
    for half in range(2):
        cols = slice(half * D_FF, (half + 1) * D_FF)
        u = [_dot(h[r], wup_ref[:, cols]) + bup_ref[:, cols] for r in range(nsub)]
        ubuf[pl.ds(0, halo), :] = cs_ref[:, cols]
        for r in range(nsub):
            ubuf[pl.ds(halo + r * sub, sub), :] = u[r]
        cs_ref[:, cols] = ubuf[pl.ds(tm, halo), :]
        for r in range(nsub):
            y = cbias_ref[:, cols] + u[r] * cw_ref[CONV_W - 1:CONV_W, cols]
            for j in range(CONV_W - 1):
                tap = ubuf[pl.ds(halo + r * sub - (CONV_W - 1 - j) * stride, sub), :]
                y = y + tap * cw_ref[j:j + 1, cols]
            ys[r][half] = y
    for r in range(nsub):
        act[rows[r], :] = (jax.nn.gelu(ys[r][0]) * ys[r][1]).astype(BF16)
    f = [_dot(act[rows[r], :], wdn_ref[...]) + bdn_ref[...] for r in range(nsub)]
    for r in range(nsub):
        o_ref[rows[r], :] = _ln(alpha * x[r] + mod(g2_ref, r) * f[r]) * lg_ref[...] + lb_ref[...]


def _ffn(x, sh, sc, g2, cb0, w, tm, stride, alpha):
    bsz, seq, _ = x.shape
    mrows = sh.shape[1]
    halo = cb0.shape[1]
    cs = pl.BlockSpec((None, halo, 2 * D_FF), lambda b, i: (b, 0, 0))
    weights = [w["wup"], w["bup"], w["cw"], w["cbias"], w["wdn"], w["bdn"], w["ln2_g"], w["ln2_b"]]
    return pl.pallas_call(
        functools.partial(_ffn_body, alpha, tm, stride, halo),
        out_shape=[jax.ShapeDtypeStruct((bsz, seq, D_MODEL), F32),
                   jax.ShapeDtypeStruct((bsz, halo, 2 * D_FF), F32)],
        grid=(bsz, seq // tm),
        in_specs=[_tok_spec(tm, D_MODEL), _mod_spec(mrows, tm), _mod_spec(mrows, tm),
                  _mod_spec(mrows, tm), cs] + [_const_spec(a.shape) for a in weights],
        out_specs=[_tok_spec(tm, D_MODEL), cs],
        scratch_shapes=[pltpu.VMEM((halo + tm, D_FF), F32), pltpu.VMEM((tm, D_FF), BF16)],
        compiler_params=_params(2),
        name="ffn",
    )(x, sh, sc, g2, cb0, *weights)


_O_GATE = 2 * DQK_M + DV_M
_O_OG = _O_GATE + 2 * NH_M
_O_QA = _O_OG + DV_M
_O_KA = _O_QA + DQ_A
_O_VA = _O_KA + DKV_A
_O_GM = _O_VA + DKV_A
_PREP_ROWS = 128


def _split_w_in_body(wt_ref, qk_ref, vm_ref, vmt_ref, g_ref, og_ref, qa_ref, k_ref, v_ref, vt_ref, gg_ref):
    piece = lambda lo, hi: wt_ref[lo:hi, :]
    qk_ref[...] = piece(0, 2 * DQK_M).T.astype(BF16)
    vm = piece(2 * DQK_M, _O_GATE)
    vmt_ref[...] = vm.astype(BF16)
    vm_ref[...] = vm.T.astype(BF16)
    pad = jnp.zeros((LANES - 2 * NH_M, _PREP_ROWS), F32)
    g_ref[...] = jnp.concatenate([piece(_O_GATE, _O_OG), pad], axis=0).T.astype(BF16)
    og_ref[...] = piece(_O_OG, _O_QA).T.astype(BF16)
    qa_ref[...] = piece(_O_QA, _O_KA).T.astype(BF16)
    k_ref[...] = piece(_O_KA, _O_VA).T.astype(BF16)
    v = piece(_O_VA, _O_GM)
    vt_ref[...] = v.astype(BF16)
    v_ref[...] = v.T.astype(BF16)
    gg_ref[...] = piece(_O_GM, wt_ref.shape[0]).T.astype(BF16)


def _split_w_in(w_in_t):
    d_in = w_in_t.shape[0]
    rows = lambda n: ((D_MODEL, n), pl.BlockSpec((_PREP_ROWS, n), lambda i: (i, 0)))
    cols = lambda n: ((n, D_MODEL), pl.BlockSpec((n, _PREP_ROWS), lambda i: (0, i)))
    outs = dict(wqk=rows(2 * DQK_M), wvm=rows(DV_M), wvm_t=cols(DV_M), wg=rows(LANES), wog=rows(DV_M),
                wqa=rows(DQ_A), wk=rows(DKV_A), wv=rows(DKV_A), wv_t=cols(DKV_A),
                wgg=rows(d_in - _O_GM))
    res = pl.pallas_call(
        _split_w_in_body,
        out_shape=[jax.ShapeDtypeStruct(s, BF16) for s, _ in outs.values()],
        grid=(D_MODEL // _PREP_ROWS,),
        in_specs=[pl.BlockSpec((d_in, _PREP_ROWS), lambda i: (0, i))],
        out_specs=[spec for _, spec in outs.values()],
        compiler_params=_params(1),
        name="split_w_in",
    )(w_in_t)
    return dict(zip(outs.keys(), res))


def _prep_weights(w_in, b_in, mlstm_norm_w, w_branch_m, w_branch_a, w_out, ln1_g, ln1_b,
                  w_up, b_up, conv_w, conv_b, w_down, b_down, ln2_g, ln2_b):
    row = lambda a: a.reshape(1, -1)
    gate_pad = LANES - 2 * NH_M
    b_k, b_v = b_in[_O_KA:_O_VA], b_in[_O_VA:_O_GM]
    return dict(
        **_split_w_in(w_in.T),
        bqk=row(b_in[:2 * DQK_M]),
        bvm_row=row(b_in[2 * DQK_M:_O_GATE]), bvm_col=b_in[2 * DQK_M:_O_GATE].reshape(-1, 1),
        bg=row(jnp.pad(b_in[_O_GATE:_O_OG], (0, gate_pad))),
        bqa=row(b_in[_O_QA:_O_KA]),
        bk=row(b_k),
        bv_row=row(b_v), bv_col=b_v.reshape(-1, 1),
        bog=row(b_in[_O_OG:_O_QA]), bgg=row(b_in[_O_GM:]),
        norm_w=row(mlstm_norm_w), norm_w_col=mlstm_norm_w.reshape(-1, 1),
        wbm=w_branch_m.astype(BF16), wba=w_branch_a.astype(BF16),
        wo=w_out.astype(BF16), ln1_g=row(ln1_g), ln1_b=row(ln1_b),
        wup=w_up.astype(BF16), bup=row(b_up), cw=conv_w, cbias=row(conv_b),
        wdn=w_down.astype(BF16), bdn=row(b_down), ln2_g=row(ln2_g), ln2_b=row(ln2_b))


def kernel(x_prompt, x_sample, c_prompt, c_sample, state_mlstm_C, state_mlstm_n, state_mlstm_m,
           cache_k_win, cache_v_win, state_ffn_conv, w_ada, b_ada, w_in, b_in, mlstm_norm_w,
           attn_sinks, w_branch_m, w_branch_a, w_out, ln1_g, ln1_b, w_up, b_up, conv_w, conv_b,
           w_down, b_down, ln2_g, ln2_b):
    depth = w_in.shape[0]
    bp, lp, _ = x_prompt.shape
    bs, ls, _ = x_sample.shape
    assert cache_k_win.shape[2] == WINDOW
    alpha = (2 * depth) ** 0.25
    dt = x_prompt.dtype
    slopes = jnp.exp2(-8.0 * jnp.arange(1, NH_A + 1, dtype=F32) / NH_A)
    tm_p = 512
    ns = bs * ls

    yp = x_prompt
    to_pos = lambda a: a.transpose(1, 0, 2).reshape(1, -1, a.shape[-1])
    ys = to_pos(x_sample)
    new_p, new_s = [], []
    n_c = bp + bs
    c_rows = -(-n_c // SUBLANES) * SUBLANES
    c_all = jnp.concatenate([c_sample, c_prompt, jnp.zeros((c_rows - n_c, D_MODEL), dt)], axis=0)
    for l in range(depth):
        w = _prep_weights(w_in[l], b_in[l], mlstm_norm_w[l], w_branch_m[l], w_branch_a[l], w_out[l],
                          ln1_g[l], ln1_b[l], w_up[l], b_up[l], conv_w[l], conv_b[l], w_down[l],
                          b_down[l], ln2_g[l], ln2_b[l])
        prm = jnp.stack([slopes, attn_sinks[l].astype(F32)])
        mod = _ada(c_all, w_ada[l], b_ada[l])
        mod_s = mod[:bs].reshape(bs, N_MOD, D_MODEL)
        mod_p = mod[bs:bs + bp].reshape(bp, N_MOD, D_MODEL)
        mp_ = [mod_p[:, j:j + 1] for j in range(N_MOD)]
        ms_ = [jnp.tile(mod_s[:, j], (ls, 1))[None] for j in range(N_MOD)]

        hp, qk, v, g, qa, kb, ka, vat = _inproj(yp, mp_[0], mp_[1], w, tm_p, True)
        hm, cp, np_, mp = _mlstm_prompt(qk, v, g, jnp.zeros((bp, NH_M, DHV_M, DHK_M), dt),
                                        jnp.zeros((bp, NH_M, DHK_M), dt), jnp.zeros((bp, 1, LANES), dt),
                                        w["norm_w_col"])
        ha = _swa_prompt(prm, qa, kb, vat)
        x1p = _merge(yp, hp, mp_[2], hm, ha, w, tm_p, alpha)
        halo_p = SUBLANES
        yp, csp = _ffn(x1p, mp_[3], mp_[4], mp_[5], jnp.zeros((bp, halo_p, 2 * D_FF), dt),
                       w, tm_p, 1, alpha)
        p_k = ka[:, lp - WINDOW:].reshape(bp, WINDOW, NKV_A, HD_A)
        p_v = vat[:, :, lp - WINDOW:].reshape(bp, NKV_A, HD_A, WINDOW).transpose(0, 3, 1, 2)
        new_p.append((cp, np_, mp[:, 0, :NH_M], p_k, p_v, csp[:, halo_p - (CONV_W - 1):]))

        hs, qk, v, g, qa, kn, vn = _inproj(ys, ms_[0], ms_[1], w, ns, False)
        per_seq = lambda a: a.reshape(ls, bs, a.shape[-1])
        m0 = jnp.pad(state_mlstm_m[l], ((0, 0), (0, LANES - NH_M)))[:, None, :]
        hm, cs_, ns_, ms = _mlstm_sample(per_seq(qk), per_seq(v), per_seq(g), state_mlstm_C[l],
                                         state_mlstm_n[l], m0, w["norm_w"], SEQS_PER_STEP)
        to_t = lambda a: a.transpose(0, 2, 3, 1).reshape(bs, DKV_A, WINDOW)
        from_t = lambda a: a.reshape(bs, NKV_A, HD_A, WINDOW).transpose(0, 3, 1, 2)
        ha, kct, vct = _swa_sample(prm, per_seq(qa), per_seq(kn), per_seq(vn),
                                   to_t(cache_k_win[l]), to_t(cache_v_win[l]), SEQS_PER_STEP)
        x1s = _merge(ys, hs, ms_[2], hm.reshape(1, ns, DV_M), ha.reshape(1, ns, DQ_A), w, ns, alpha)
        ys, css = _ffn(x1s, ms_[3], ms_[4], ms_[5], to_pos(state_ffn_conv[l]), w, ns, bs, alpha)
        conv_s = css.reshape(CONV_W - 1, bs, 2 * D_FF).transpose(1, 0, 2)
        new_s.append((cs_, ns_, ms[:, 0, :NH_M], from_t(kct), from_t(vct), conv_s))

    p_state = [jnp.stack(a) for a in zip(*new_p)]
    s_state = [jnp.stack(a) for a in zip(*new_s)]
    y_sample = ys.reshape(ls, bs, D_MODEL).transpose(1, 0, 2)
    return (yp, y_sample, *p_state, *s_state)
```

```python
import functools

import jax
import jax.numpy as jnp
from jax import lax
from jax.experimental import pallas as pl
from jax.experimental.pallas import tpu as pltpu

F32 = jnp.float32
BF16 = jnp.bfloat16

D_MODEL = 1024
NH_M, DHK_M, DHV_M = 4, 128, 256
DQK_M, DV_M = NH_M * DHK_M, NH_M * DHV_M
NH_A, NKV_A, HD_A = 16, 4, 64
GROUP_A = NH_A // NKV_A
WINDOW = 128
DQ_A, DKV_A = NH_A * HD_A, NKV_A * HD_A
D_FF = 2816
CONV_W = 3
N_MOD = 6
LN_EPS = 1e-5
CHUNK = 128
NEG = -1e30
LANES = 128
SUBLANES = 8
VMEM_LIMIT = 56 * 1024 * 1024
SEQS_PER_STEP = 8


def _ln(x):
    mu = jnp.mean(x, axis=-1, keepdims=True)
    xc = x - mu
    var = jnp.mean(xc * xc, axis=-1, keepdims=True)
    return xc * lax.rsqrt(var + LN_EPS)


def _dot(a, b):
    return jnp.dot(a, b, preferred_element_type=F32)


def _dot_nt(a, b):
    return lax.dot_general(a, b, (((1,), (1,)), ((), ())), preferred_element_type=F32)


def _dot_tn(a, b):
    return lax.dot_general(a, b, (((0,), (0,)), ((), ())), preferred_element_type=F32)


def _const_spec(shape):
    nd = len(shape)
    return pl.BlockSpec(shape, lambda *_: (0,) * nd, pipeline_mode=pl.Buffered(1))


def _params(n_grid):
    return pltpu.CompilerParams(dimension_semantics=("arbitrary",) * n_grid,
                                vmem_limit_bytes=VMEM_LIMIT)


def _first_step():
    return (pl.program_id(0) == 0) & (pl.program_id(1) == 0)


def _ada_body(c_ref, w_ref, b_ref, o_ref):
    c = c_ref[...]
    s = (c * jax.nn.sigmoid(c)).astype(BF16)
    o_ref[...] = _dot(s, w_ref[...].astype(BF16)) + b_ref[...]


def _ada(c, w_ada, b_ada):
    rows = c.shape[0]
    n_out = w_ada.shape[1]
    bn = 1536
    assert n_out % bn == 0
    return pl.pallas_call(
        _ada_body,
        out_shape=jax.ShapeDtypeStruct((rows, n_out), F32),
        grid=(n_out // bn,),
        in_specs=[pl.BlockSpec((rows, D_MODEL), lambda j: (0, 0)),
                  pl.BlockSpec((D_MODEL, bn), lambda j: (0, j)),
                  pl.BlockSpec((1, bn), lambda j: (0, j))],
        out_specs=pl.BlockSpec((rows, bn), lambda j: (0, j)),
        compiler_params=_params(1),
        name="ada",
    )(c, w_ada, b_ada.reshape(1, n_out))


def _inproj_body(prompt, x_ref, sh_ref, sc_ref, wqk_ref, bqk_ref, wvm_ref, bvm_ref, wg_ref, bg_ref,
                 wqa_ref, bqa_ref, wk_ref, bk_ref, wv_ref, bv_ref, h_ref, qk_ref, v_ref, g_ref, qa_ref, *rest):
    h = (_ln(x_ref[...]) * (1.0 + sc_ref[...]) + sh_ref[...]).astype(BF16)
    h_ref[...] = h

    def proj(w_ref, b_ref, lo, n):
        return _dot(h, w_ref[:, lo:lo + n]) + b_ref[:, lo:lo + n]

    qk_ref[:, :DQK_M] = (proj(wqk_ref, bqk_ref, 0, DQK_M) * DHK_M ** -0.5).astype(qk_ref.dtype)
    qk_ref[:, DQK_M:] = proj(wqk_ref, bqk_ref, DQK_M, DQK_M).astype(qk_ref.dtype)
    if prompt:
        v_ref[...] = (_dot_nt(wvm_ref[...], h) + bvm_ref[...]).astype(v_ref.dtype)
    else:
        v_ref[...] = proj(wvm_ref, bvm_ref, 0, DV_M).astype(v_ref.dtype)
    g_ref[...] = proj(wg_ref, bg_ref, 0, LANES)
    qa_ref[...] = (proj(wqa_ref, bqa_ref, 0, DQ_A) * HD_A ** -0.5).astype(qa_ref.dtype)
    ka = proj(wk_ref, bk_ref, 0, DKV_A)
    if prompt:
        kb_ref, ka_ref, vat_ref = rest
        kb_ref[...] = ka.astype(BF16)
        ka_ref[...] = ka
        vat_ref[...] = _dot_nt(wv_ref[...], h) + bv_ref[...]
    else:
        ka_ref, va_ref = rest
        ka_ref[...] = ka
        va_ref[...] = _dot(h, wv_ref[...]) + bv_ref[...]


def _tok_spec(tm, n):
    return pl.BlockSpec((None, tm, n), lambda b, i: (b, i, 0))


def _mod_spec(rows, tm):
    if rows == 1:
        return pl.BlockSpec((None, 1, D_MODEL), lambda b, i: (b, 0, 0))
    return pl.BlockSpec((None, tm, D_MODEL), lambda b, i: (b, i, 0))


def _inproj(x, sh, sc, w, tm, prompt):
    bsz, seq, _ = x.shape
    mrows = sh.shape[1]
    act = BF16 if prompt else F32
    t_spec = lambda n: pl.BlockSpec((None, n, tm), lambda b, i: (b, 0, i))
    outs = [((bsz, seq, D_MODEL), BF16, _tok_spec(tm, D_MODEL)),
            ((bsz, seq, 2 * DQK_M), act, _tok_spec(tm, 2 * DQK_M)),
            ((bsz, DV_M, seq), act, t_spec(DV_M)) if prompt else ((bsz, seq, DV_M), act, _tok_spec(tm, DV_M)),
            ((bsz, seq, LANES), F32, _tok_spec(tm, LANES)),
            ((bsz, seq, DQ_A), act, _tok_spec(tm, DQ_A))]
    if prompt:
        wvm, bvm, wv, bv = w["wvm_t"], w["bvm_col"], w["wv_t"], w["bv_col"]
        outs += [((bsz, seq, DKV_A), BF16, _tok_spec(tm, DKV_A)),
                 ((bsz, seq, DKV_A), F32, _tok_spec(tm, DKV_A)),
                 ((bsz, DKV_A, seq), F32, t_spec(DKV_A))]
    else:
        wvm, bvm, wv, bv = w["wvm"], w["bvm_row"], w["wv"], w["bv_row"]
        outs += [((bsz, seq, DKV_A), F32, _tok_spec(tm, DKV_A)),
                 ((bsz, seq, DKV_A), F32, _tok_spec(tm, DKV_A))]
    weights = [w["wqk"], w["bqk"], wvm, bvm, w["wg"], w["bg"], w["wqa"], w["bqa"], w["wk"], w["bk"],
               wv, bv]
    return pl.pallas_call(
        functools.partial(_inproj_body, prompt),
        out_shape=[jax.ShapeDtypeStruct(s, dt) for s, dt, _ in outs],
        grid=(bsz, seq // tm),
        in_specs=[_tok_spec(tm, D_MODEL), _mod_spec(mrows, tm), _mod_spec(mrows, tm)]
                 + [_const_spec(a.shape) for a in weights],
        out_specs=[spec for _, _, spec in outs],
        compiler_params=_params(2),
        name="inproj",
    )(x, sh, sc, *weights)


def _scan_rows(x, op, rows):
    row = lax.broadcasted_iota(jnp.int32, x.shape, 0)
    d = 1
    while d < rows:
        shifted = pltpu.roll(x, d, axis=0)
        x = jnp.where(row >= d, op(x, shifted), x)
        d *= 2
    return x


def _mlstm_sample_body(t_in, nb, qk_ref, v_ref, g_ref, c0_ref, n0_ref, m0_ref, nw_ref,
                       h_ref, c_ref, n_ref, m_ref, *pads):
    t = SUBLANES
    c_ref[...] = c0_ref[...]
    n_ref[...] = n0_ref[...]
    m_ref[...] = m0_ref[...]

    @pl.when(_first_step())
    def _():
        for p in pads:
            p[...] = jnp.zeros(p.shape, p.dtype)
    for src, dst in zip((qk_ref, v_ref, g_ref), pads):
        for s in range(nb):
            dst[s, pl.ds(0, t_in), :] = src[:, s, :]
    qk_src, v_src, g_src = pads

    row = lax.broadcasted_iota(jnp.int32, (t, LANES), 0)
    lane = lax.broadcasted_iota(jnp.int32, (t, LANES), 1)
    lane1 = lax.broadcasted_iota(jnp.int32, (1, LANES), 1)
    r2 = lax.broadcasted_iota(jnp.int32, (t, t), 0)
    c2 = lax.broadcasted_iota(jnp.int32, (t, t), 1)
    causal = c2 <= r2
    pad_gate = jnp.where(lane < NH_M, NEG, -NEG)

    seqs = []
    for s in range(nb):
        ga = _gate_algebra(jnp.where(row < t_in, g_src[s], pad_gate), m_ref[s], t)
        ga["gt"] = ga["gg"].T
        m_ref[s] = jnp.where(lane1 < NH_M, ga["m_new"], 0.0)
        seqs.append(ga)

    units = [(s, hd) for s in range(nb) for hd in range(NH_M)]
    nu = len(units)
    col = lambda name, s, hd: seqs[s][name][:, hd:hd + 1]
    q = [qk_src[s, :, hd * DHK_M:(hd + 1) * DHK_M].astype(BF16) for s, hd in units]
    k = [qk_src[s, :, DQK_M + hd * DHK_M:DQK_M + (hd + 1) * DHK_M].astype(BF16) for s, hd in units]
    v = [v_src[s, :, hd * DHV_M:(hd + 1) * DHV_M].astype(BF16) for s, hd in units]
    c_old = [c_ref[s, hd] for s, hd in units]
    n_old = [n_ref[s, hd:hd + 1, :] for s, hd in units]

    sqk = [_dot_nt(q[u], k[u]) for u in range(nu)]
    dexp = [jnp.exp(jnp.where(causal, col("eb", s, hd) + seqs[s]["gt"][hd:hd + 1, :], NEG))
            for s, hd in units]
    smat = [sqk[u] * dexp[u] for u in range(nu)]
    intra = [_dot(smat[u].astype(BF16), v[u]) for u in range(nu)]
    inter = [_dot_nt(q[u], c_old[u].astype(BF16)) for u in range(nu)]
    qn = [jnp.sum(q[u].astype(F32) * n_old[u].astype(BF16).astype(F32), axis=-1, keepdims=True)
          for u in range(nu)]
    den = [jnp.sum(smat[u], axis=-1, keepdims=True) + col("a_in", s, hd) * qn[u]
           for u, (s, hd) in enumerate(units)]
    hh = [(intra[u] + col("a_in", s, hd) * inter[u])
          / jnp.maximum(jnp.abs(den[u]), col("lowb", s, hd)) for u, (s, hd) in enumerate(units)]
    hn = [_ln(hh[u]) * nw_ref[:, hd * DHV_M:(hd + 1) * DHV_M] for u, (s, hd) in enumerate(units)]
    for u, (s, hd) in enumerate(units):
        h_ref[:, s, hd * DHV_M:(hd + 1) * DHV_M] = hn[u][:t_in]

    kw = [k[u].astype(F32) * col("ws", s, hd) for u, (s, hd) in enumerate(units)]
    upd = [_dot_tn(v[u], kw[u].astype(BF16)) for u in range(nu)]
    for u, (s, hd) in enumerate(units):
        dec = seqs[s]["decay"][:, hd:hd + 1]
        c_ref[s, hd] = dec * c_old[u] + upd[u]
        n_ref[s, hd:hd + 1, :] = dec * n_old[u] + jnp.sum(kw[u], axis=0, keepdims=True)


def _mlstm_sample(qk, v, g, c0, n0, m0, norm_w, nb):
    t_in, nseq, _ = qk.shape
    assert t_in <= SUBLANES
    blk = lambda n: pl.BlockSpec((t_in, nb, n), lambda o, c: (0, o, 0))
    st_c = pl.BlockSpec((nb, NH_M, DHV_M, DHK_M), lambda o, c: (o, 0, 0, 0))
    st_n = pl.BlockSpec((nb, NH_M, DHK_M), lambda o, c: (o, 0, 0))
    st_m = pl.BlockSpec((nb, 1, LANES), lambda o, c: (o, 0, 0))
    return pl.pallas_call(
        functools.partial(_mlstm_sample_body, t_in, nb),
        out_shape=[jax.ShapeDtypeStruct((t_in, nseq, DV_M), F32),
                   jax.ShapeDtypeStruct((nseq, NH_M, DHV_M, DHK_M), F32),
                   jax.ShapeDtypeStruct((nseq, NH_M, DHK_M), F32),
                   jax.ShapeDtypeStruct((nseq, 1, LANES), F32)],
        grid=(nseq // nb, 1),
        in_specs=[blk(2 * DQK_M), blk(DV_M), blk(LANES), st_c, st_n, st_m,
                  pl.BlockSpec((1, DV_M), lambda o, c: (0, 0))],
        out_specs=[blk(DV_M), st_c, st_n, st_m],
        scratch_shapes=[pltpu.VMEM((nb, SUBLANES, 2 * DQK_M), F32), pltpu.VMEM((nb, SUBLANES, DV_M), F32),
                        pltpu.VMEM((nb, SUBLANES, LANES), F32)],
        compiler_params=_params(2),
        name="mlstm_sample",
    )(qk, v, g, c0, n0, m0, norm_w)


def _gate_algebra(g, m_prev, t):
    b = pltpu.roll(_scan_rows(jax.nn.log_sigmoid(g), jnp.add, t), LANES - NH_M, axis=1)
    gg = g - b
    gmx = _scan_rows(gg, jnp.maximum, t)
    a = b + m_prev
    mt = jnp.maximum(a, b + gmx)
    b_last, gmx_last = b[t - 1:t, :], gmx[t - 1:t, :]
    m_new = jnp.maximum(b_last + m_prev, b_last + gmx_last)
    return dict(gg=gg, eb=b - mt, a_in=jnp.exp(a - mt), lowb=jnp.exp(-mt),
                ws=jnp.exp(b_last + gg - m_new), decay=jnp.exp(b_last + m_prev - m_new), m_new=m_new)


MLSTM_CHUNKS_PER_STEP = 4


def _mlstm_prompt_body(nb, qk_ref, vt_ref, g_ref, c0_ref, n0_ref, m0_ref, nwc_ref,
                       h_ref, c_ref, n_ref, m_ref, nwb):
    @pl.when(pl.program_id(1) == 0)
    def _():
        c_ref[...] = c0_ref[...]
        n_ref[...] = n0_ref[...]
        m_ref[...] = m0_ref[...]

    @pl.when(_first_step())
    def _():
        for hd in range(NH_M):
            nwb[hd] = jnp.broadcast_to(nwc_ref[hd * DHV_M:(hd + 1) * DHV_M, :], (DHV_M, CHUNK))

    for ci in range(MLSTM_CHUNKS_PER_STEP):
        _mlstm_prompt_chunk(nb, slice(ci * CHUNK, (ci + 1) * CHUNK), qk_ref, vt_ref, g_ref, nwb,
                            h_ref, c_ref, n_ref, m_ref)


def _mlstm_prompt_chunk(nb, rows, qk_ref, vt_ref, g_ref, nwb, h_ref, c_ref, n_ref, m_ref):
    t = CHUNK
    lane1 = lax.broadcasted_iota(jnp.int32, (1, LANES), 1)
    r2 = lax.broadcasted_iota(jnp.int32, (t, t), 0)
    c2 = lax.broadcasted_iota(jnp.int32, (t, t), 1)
    causal = r2 <= c2

    seqs = []
    for s in range(nb):
        ga = _gate_algebra(g_ref[s, rows, :], m_ref[s], t)
        m_ref[s] = jnp.where(lane1 < NH_M, ga["m_new"], 0.0)
        for name in ("eb", "a_in", "lowb", "ws"):
            ga[name + "_t"] = ga[name].T
        seqs.append(ga)

    units = [(s, hd) for s in range(nb) for hd in range(NH_M)]
    nu = len(units)
    rowv = lambda name, s, hd: seqs[s][name + "_t"][hd:hd + 1, :]
    q = [qk_ref[s, rows, hd * DHK_M:(hd + 1) * DHK_M] for s, hd in units]
    k = [qk_ref[s, rows, DQK_M + hd * DHK_M:DQK_M + (hd + 1) * DHK_M] for s, hd in units]
    vt = [vt_ref[s, hd * DHV_M:(hd + 1) * DHV_M, rows] for s, hd in units]
    c_old = [c_ref[s, hd] for s, hd in units]
    n_old = [n_ref[s, hd:hd + 1, :] for s, hd in units]

    skq = [_dot_nt(k[u], q[u]) for u in range(nu)]
    dexp = [jnp.exp(jnp.where(causal, seqs[s]["gg"][:, hd:hd + 1] + rowv("eb", s, hd), NEG))
            for s, hd in units]
    smat = [skq[u] * dexp[u] for u in range(nu)]
    intra = [_dot(vt[u], smat[u].astype(BF16)) for u in range(nu)]
    inter = [_dot_nt(c_old[u].astype(BF16), q[u]) for u in range(nu)]
    qn = [_dot_nt(jnp.broadcast_to(n_old[u], (SUBLANES, DHK_M)).astype(BF16), q[u])[0:1, :]
          for u in range(nu)]
    den = [jnp.sum(smat[u], axis=0, keepdims=True) + rowv("a_in", s, hd) * qn[u]
           for u, (s, hd) in enumerate(units)]
    inv = [1.0 / jnp.maximum(jnp.abs(den[u]), rowv("lowb", s, hd)) for u, (s, hd) in enumerate(units)]
    hh = [(intra[u] + rowv("a_in", s, hd) * inter[u]) * inv[u] for u, (s, hd) in enumerate(units)]
    mu = [jnp.mean(hh[u], axis=0, keepdims=True) for u in range(nu)]
    xc = [hh[u] - mu[u] for u in range(nu)]
    var = [jnp.mean(xc[u] * xc[u], axis=0, keepdims=True) for u in range(nu)]
    hn = [xc[u] * lax.rsqrt(var[u] + LN_EPS) * nwb[hd] for u, (s, hd) in enumerate(units)]
    for u, (s, hd) in enumerate(units):
        h_ref[s, rows, hd * DHV_M:(hd + 1) * DHV_M] = hn[u].T

    vw = [(vt[u].astype(F32) * rowv("ws", s, hd)).astype(BF16) for u, (s, hd) in enumerate(units)]
    upd = [_dot(vw[u], k[u]) for u in range(nu)]
    nupd = [_dot(jnp.broadcast_to(rowv("ws", s, hd), (SUBLANES, t)).astype(BF16), k[u])[0:1, :]
            for u, (s, hd) in enumerate(units)]
    for u, (s, hd) in enumerate(units):
        dec = seqs[s]["decay"][:, hd:hd + 1]
        c_ref[s, hd] = dec * c_old[u] + upd[u]
        n_ref[s, hd:hd + 1, :] = dec * n_old[u] + nupd[u]


def _mlstm_prompt(qk, vt, g, c0, n0, m0, norm_w_col):
    nb, seq, _ = qk.shape
    ts = MLSTM_CHUNKS_PER_STEP * CHUNK
    tok = lambda n: pl.BlockSpec((nb, ts, n), lambda o, c: (0, c, 0))
    st_c = pl.BlockSpec((nb, NH_M, DHV_M, DHK_M), lambda o, c: (0, 0, 0, 0))
    st_n = pl.BlockSpec((nb, NH_M, DHK_M), lambda o, c: (0, 0, 0))
    st_m = pl.BlockSpec((nb, 1, LANES), lambda o, c: (0, 0, 0))
    return pl.pallas_call(
        functools.partial(_mlstm_prompt_body, nb),
        out_shape=[jax.ShapeDtypeStruct((nb, seq, DV_M), F32),
                   jax.ShapeDtypeStruct((nb, NH_M, DHV_M, DHK_M), F32),
                   jax.ShapeDtypeStruct((nb, NH_M, DHK_M), F32),
                   jax.ShapeDtypeStruct((nb, 1, LANES), F32)],
        grid=(1, seq // ts),
        in_specs=[tok(2 * DQK_M), pl.BlockSpec((nb, DV_M, ts), lambda o, c: (0, 0, c)), tok(LANES),
                  st_c, st_n, st_m, pl.BlockSpec((DV_M, 1), lambda o, c: (0, 0))],
        out_specs=[tok(DV_M), st_c, st_n, st_m],
        scratch_shapes=[pltpu.VMEM((NH_M, DHV_M, CHUNK), F32)],
        compiler_params=_params(2),
        name="mlstm_prompt",
    )(qk, vt, g, c0, n0, m0, norm_w_col)


SWA_BLOCKS_PER_STEP = 16


def _swa_prompt_body(prm_ref, q_ref, kc_ref, kp_ref, vc_ref, vp_ref, o_ref, tbl):
    row = lax.broadcasted_iota(jnp.int32, (WINDOW, WINDOW), 0)
    col = lax.broadcasted_iota(jnp.int32, (WINDOW, WINDOW), 1)
    tri = row <= col

    @pl.when(_first_step())
    def _():
        dist = jnp.where(tri, col - row, col - row + WINDOW).astype(F32)
        for hd in range(NH_A):
            bias = prm_ref[0, hd] * dist
            tbl[0, hd] = bias
            tbl[1, hd] = bias + jnp.where(tri, 0.0, -NEG)

    for b in range(SWA_BLOCKS_PER_STEP):
        rows = pl.ds(b * WINDOW, WINDOW)
        which = jnp.where(pl.program_id(1) == 0, 1, 0) if b == 0 else 0
        k_prev = kp_ref if b == 0 else kc_ref.at[pl.ds((b - 1) * WINDOW, WINDOW), :]
        v_prev = vp_ref if b == 0 else vc_ref.at[:, pl.ds((b - 1) * WINDOW, WINDOW)]
        _swa_prompt_block(which, prm_ref, q_ref.at[rows, :], kc_ref.at[rows, :], k_prev,
                          vc_ref.at[:, rows], v_prev, o_ref.at[rows, :], tbl)


def _swa_prompt_block(which, prm_ref, q_ref, kc_ref, kp_ref, vc_ref, vp_ref, o_ref, tbl):
    row = lax.broadcasted_iota(jnp.int32, (WINDOW, WINDOW), 0)
    col = lax.broadcasted_iota(jnp.int32, (WINDOW, WINDOW), 1)
    tri = row <= col
    lo = col < HD_A
    zb = jnp.zeros((WINDOW, LANES), BF16)

    def placed(k_ref, kv):
        c, par = kv // 2, kv % 2
        own = k_ref[:, c * LANES:(c + 1) * LANES]
        swp = pltpu.roll(own, HD_A, axis=1)
        if par == 0:
            return jnp.where(lo, own, zb), jnp.where(lo, zb, swp)
        return jnp.where(lo, swp, zb), jnp.where(lo, zb, own)

    st = []
    for kv in range(NKV_A):
        lhs = jnp.concatenate([*placed(kc_ref, kv), *placed(kp_ref, kv)], axis=0)
        qg = jnp.concatenate([q_ref[:, (2 * kv) * LANES:(2 * kv + 1) * LANES],
                              q_ref[:, (2 * kv + 1) * LANES:(2 * kv + 2) * LANES]], axis=0)
        st.append(_dot_nt(lhs, qg))
    tiles = []
    for hd in range(NH_A):
        kv, a, par = hd // GROUP_A, (hd % GROUP_A) // 2, hd % 2
        cols = slice(a * WINDOW, (a + 1) * WINDOW)
        tiles.append(jnp.where(tri, st[kv][par * WINDOW:(par + 1) * WINDOW, cols],
                               st[kv][(2 + par) * WINDOW:(3 + par) * WINDOW, cols]))
    sc = jnp.concatenate(tiles, axis=0).reshape(NH_A, WINDOW, WINDOW) - tbl[which]
    sink = jnp.concatenate([jnp.full((1, 1, WINDOW), prm_ref[1, hd], F32) for hd in range(NH_A)], axis=0)
    mx = jnp.maximum(jnp.max(sc, axis=1, keepdims=True), sink)
    p = jnp.exp(sc - mx)
    den = jnp.sum(p, axis=1, keepdims=True) + jnp.exp(sink - mx)
    pn = p * (1.0 / den)

    zero = jnp.zeros((WINDOW, WINDOW), F32)
    z64 = jnp.zeros((HD_A, WINDOW), BF16)
    outs = []
    for kv in range(NKV_A):
        cols = []
        for a in range(2):
            pe, po = pn[kv * GROUP_A + 2 * a], pn[kv * GROUP_A + 2 * a + 1]
            cols.append(jnp.concatenate([jnp.where(tri, pe, zero), jnp.where(tri, po, zero),
                                         jnp.where(tri, zero, pe), jnp.where(tri, zero, po)],
                                        axis=0).astype(BF16))
        pt = jnp.concatenate(cols, axis=1)
        vc = vc_ref[kv * HD_A:(kv + 1) * HD_A, :].astype(BF16)
        vp = vp_ref[kv * HD_A:(kv + 1) * HD_A, :].astype(BF16)
        vt = jnp.concatenate([jnp.concatenate([vc, z64], axis=0), jnp.concatenate([z64, vc], axis=0),
                              jnp.concatenate([vp, z64], axis=0), jnp.concatenate([z64, vp], axis=0)],
                             axis=1)
        ot = _dot(vt, pt)
        outs += [ot[:, :WINDOW].T, ot[:, WINDOW:].T]
    o_ref[...] = jnp.concatenate(outs, axis=1).astype(o_ref.dtype)


def _swa_prompt(prm, q, kb, vat):
    bsz, seq, _ = q.shape
    tq = SWA_BLOCKS_PER_STEP * WINDOW
    prev = lambda i: jnp.maximum(i * SWA_BLOCKS_PER_STEP - 1, 0)
    return pl.pallas_call(
        _swa_prompt_body,
        out_shape=jax.ShapeDtypeStruct((bsz, seq, DQ_A), BF16),
        grid=(bsz, seq // tq),
        in_specs=[pl.BlockSpec(memory_space=pltpu.SMEM),
                  pl.BlockSpec((None, tq, DQ_A), lambda b, i: (b, i, 0)),
                  pl.BlockSpec((None, tq, DKV_A), lambda b, i: (b, i, 0)),
                  pl.BlockSpec((None, WINDOW, DKV_A), lambda b, i: (b, prev(i), 0)),
                  pl.BlockSpec((None, DKV_A, tq), lambda b, i: (b, 0, i)),
                  pl.BlockSpec((None, DKV_A, WINDOW), lambda b, i: (b, 0, prev(i)))],
        out_specs=pl.BlockSpec((None, tq, DQ_A), lambda b, i: (b, i, 0)),
        scratch_shapes=[pltpu.VMEM((2, NH_A, WINDOW, WINDOW), F32)],
        compiler_params=_params(2),
        name="swa_prompt",
    )(prm, q, kb, kb, vat, vat)


def _half_mask(shape, half):
    lane = lax.broadcasted_iota(jnp.int32, shape, 1)
    return lane < HD_A if half == 0 else lane >= HD_A


def _swa_sample_body(t_in, nb, prm_ref, q_ref, kn_ref, vn_ref, kct_ref, vct_ref,
                     o_ref, kco_ref, vco_ref, tbl, q8, kn_pad, vn_pad):
    tq = SUBLANES
    rows = NH_A * tq
    row = lax.broadcasted_iota(jnp.int32, (rows, WINDOW), 0)
    col = lax.broadcasted_iota(jnp.int32, (rows, WINDOW), 1)
    tri = col <= (row & (tq - 1))

    @pl.when(_first_step())
    def _():
        r8 = lax.broadcasted_iota(jnp.int32, (tq, WINDOW), 0)
        c8 = lax.broadcasted_iota(jnp.int32, (tq, WINDOW), 1)
        dist = jnp.where(c8 <= r8, r8 - c8, r8 - c8 + WINDOW).astype(F32)
        for hd in range(NH_A):
            tbl[pl.ds(hd * tq, tq), :] = prm_ref[0, hd] * dist
        for p in (q8, kn_pad, vn_pad):
            p[...] = jnp.zeros(p.shape, p.dtype)

    for s in range(nb):
        q8[s, pl.ds(0, t_in), :] = q_ref[:, s, :]
        kn_pad[s, pl.ds(0, t_in), :] = kn_ref[:, s, :]
        vn_pad[s, pl.ds(0, t_in), :] = vn_ref[:, s, :]
    sink_col = jnp.concatenate([jnp.full((tq, 1), prm_ref[1, hd], F32) for hd in range(NH_A)], axis=0)
    z8 = jnp.zeros((tq, LANES), F32)
    n_chunk = NKV_A // 2
    heads_per_chunk = NH_A // n_chunk

    def place(piece, src_half, dst_half):
        if src_half != dst_half:
            piece = pltpu.roll(piece, HD_A, axis=1)
        return jnp.where(_half_mask(piece.shape, dst_half), piece, z8)

    qexp = []
    for s in range(nb):
        per_c = []
        for c in range(n_chunk):
            pieces = []
            for hl in range(heads_per_chunk):
                hd = c * heads_per_chunk + hl
                pieces.append(place(q8[s, :, (hd // 2) * LANES:(hd // 2 + 1) * LANES],
                                    hd % 2, hl // GROUP_A))
            per_c.append(jnp.concatenate(pieces, axis=0).astype(BF16))
        qexp.append(per_c)
    csl = lambda c: slice(c * LANES, (c + 1) * LANES)
    s_prev = [[_dot(qexp[s][c], kct_ref[s, csl(c), :].astype(BF16)) for c in range(n_chunk)]
              for s in range(nb)]
    s_cur = [[_dot_nt(qexp[s][c], kn_pad[s, :, csl(c)].astype(BF16)) for c in range(n_chunk)]
             for s in range(nb)]
    sc = [jnp.where(tri, jnp.concatenate(s_cur[s], axis=0), jnp.concatenate(s_prev[s], axis=0)) - tbl[...]
          for s in range(nb)]
    mx = [jnp.maximum(jnp.max(sc[s], axis=-1, keepdims=True), sink_col) for s in range(nb)]
    p = [jnp.exp(sc[s] - mx[s]) for s in range(nb)]
    den = [jnp.sum(p[s], axis=-1, keepdims=True) + jnp.exp(sink_col - mx[s]) for s in range(nb)]
    pn = [p[s] * (1.0 / den[s]) for s in range(nb)]
    zero = jnp.zeros((rows, WINDOW), F32)
    pc = [jnp.where(tri, pn[s], zero).astype(BF16) for s in range(nb)]
    pp = [jnp.where(tri, zero, pn[s]).astype(BF16) for s in range(nb)]
    half_rows = heads_per_chunk * tq
    oc = [[_dot_nt(pp[s][c * half_rows:(c + 1) * half_rows], vct_ref[s, csl(c), :].astype(BF16))
           + _dot(pc[s][c * half_rows:(c + 1) * half_rows], vn_pad[s, :, csl(c)].astype(BF16))
           for c in range(n_chunk)] for s in range(nb)]
    for s in range(nb):
        chunks = []
        for pch in range(NH_A // 2):
            acc = None
            for hd in (2 * pch, 2 * pch + 1):
                c, hl = hd // heads_per_chunk, hd % heads_per_chunk
                piece = place(oc[s][c][hl * tq:(hl + 1) * tq, :], hl // GROUP_A, hd % 2)
                acc = piece if acc is None else acc + piece
            chunks.append(acc)
        o_ref[:, s, :] = jnp.concatenate(chunks, axis=1)[:t_in]

    lane = lax.broadcasted_iota(jnp.int32, (DKV_A, WINDOW), 1)
    for new_pad, old_ref, out_ref in ((kn_pad, kct_ref, kco_ref), (vn_pad, vct_ref, vco_ref)):
        for s in range(nb):
            merged = jnp.where(lane < t_in, new_pad[s].T, old_ref[s])
            out_ref[s] = pltpu.roll(merged, WINDOW - t_in, axis=1)


def _swa_sample(prm, q, kn, vn, kct, vct, nb):
    t_in, nseq, _ = q.shape
    assert t_in <= SUBLANES
    cur = lambda n: pl.BlockSpec((t_in, nb, n), lambda o, i: (0, o, 0))
    win = pl.BlockSpec((nb, DKV_A, WINDOW), lambda o, i: (o, 0, 0))
    return pl.pallas_call(
        functools.partial(_swa_sample_body, t_in, nb),
        out_shape=[jax.ShapeDtypeStruct((t_in, nseq, DQ_A), F32),
                   jax.ShapeDtypeStruct((nseq, DKV_A, WINDOW), F32),
                   jax.ShapeDtypeStruct((nseq, DKV_A, WINDOW), F32)],
        grid=(nseq // nb, 1),
        in_specs=[pl.BlockSpec(memory_space=pltpu.SMEM), cur(DQ_A), cur(DKV_A), cur(DKV_A), win, win],
        out_specs=[cur(DQ_A), win, win],
        scratch_shapes=[pltpu.VMEM((NH_A * SUBLANES, WINDOW), F32),
                        pltpu.VMEM((nb, SUBLANES, DQ_A), F32),
                        pltpu.VMEM((nb, WINDOW, DKV_A), F32),
                        pltpu.VMEM((nb, WINDOW, DKV_A), F32)],
        compiler_params=_params(2),
        name="swa_sample",
    )(prm, q, kn, vn, kct, vct)


def _merge_body(alpha, x_ref, h_ref, g1_ref, hm_ref, ha_ref, wgt_ref, bgt_ref,
                wbm_ref, wba_ref, wo_ref, lg_ref, lb_ref, o_ref):
    nsub = 2
    sub = x_ref.shape[0] // nsub
    rows = [pl.ds(r * sub, sub) for r in range(nsub)]
    mod = lambda ref, r: ref[...] if ref.shape[0] == 1 else ref[rows[r], :]
    gate = lambda g, j: g[:, j * D_MODEL:(j + 1) * D_MODEL]

    x = [x_ref[rows[r], :] for r in range(nsub)]
    h = [h_ref[rows[r], :] for r in range(nsub)]
    g = [jax.nn.sigmoid(_dot(h[r], wgt_ref[...]) + bgt_ref[...]) for r in range(nsub)]
    hm = [(hm_ref[rows[r], :] * gate(g[r], 0)).astype(BF16) for r in range(nsub)]
    bm = [_dot(hm[r], wbm_ref[...]) for r in range(nsub)]
    ba = [_dot(ha_ref[rows[r], :].astype(BF16), wba_ref[...]) for r in range(nsub)]
    merged = [(gate(g[r], 1) * bm[r] + gate(g[r], 2) * ba[r]).astype(BF16) for r in range(nsub)]
    mo = [_dot(merged[r], wo_ref[...]) for r in range(nsub)]
    for r in range(nsub):
        o_ref[rows[r], :] = _ln(alpha * x[r] + mod(g1_ref, r) * mo[r]) * lg_ref[...] + lb_ref[...]


def _merge(x, h, g1, hm, ha, w, tm, alpha):
    bsz, seq, _ = x.shape
    mrows = g1.shape[1]
    weights = [w["wgates"], w["bgates"], w["wbm"], w["wba"], w["wo"],
               w["ln1_g"], w["ln1_b"]]
    return pl.pallas_call(
        functools.partial(_merge_body, alpha),
        out_shape=jax.ShapeDtypeStruct((bsz, seq, D_MODEL), F32),
        grid=(bsz, seq // tm),
        in_specs=[_tok_spec(tm, D_MODEL), _tok_spec(tm, D_MODEL), _mod_spec(mrows, tm),
                  _tok_spec(tm, DV_M), _tok_spec(tm, DQ_A)]
                 + [_const_spec(a.shape) for a in weights],
        out_specs=_tok_spec(tm, D_MODEL),
        compiler_params=_params(2),
        name="merge",
    )(x, h, g1, hm, ha, *weights)


def _ffn_body(alpha, tm, stride, halo, x_ref, sh_ref, sc_ref, g2_ref, cb0_ref, wup_ref, bup_ref,
              cw_ref, cbias_ref, wdn_ref, bdn_ref, lg_ref, lb_ref, o_ref, cs_ref, ubuf, act):
    @pl.when(pl.program_id(1) == 0)
    def _():
        cs_ref[...] = cb0_ref[...]

    nsub = 2
    sub = tm // nsub
    rows = [pl.ds(r * sub, sub) for r in range(nsub)]
    mod = lambda ref, r: ref[...] if ref.shape[0] == 1 else ref[rows[r], :]
    x = [x_ref[rows[r], :] for r in range(nsub)]
    h = [(_ln(x[r]) * (1.0 + mod(sc_ref, r)) + mod(sh_ref, r)).astype(BF16) for r in range(nsub)]
    ys = [[None, None] for _ in range(nsub)]
    for half in range(2):
        cols = slice(half * D_FF, (half + 1) * D_FF)
        u = [_dot(h[r], wup_ref[:, cols]) + bup_ref[:, cols] for r in range(nsub)]
        ubuf[pl.ds(0, halo), :] = cs_ref[:, cols]
        for r in range(nsub):
            ubuf[pl.ds(halo + r * sub, sub), :] = u[r]
        cs_ref[:, cols] = ubuf[pl.ds(tm, halo), :]
        for r in range(nsub):
            y = cbias_ref[:, cols] + u[r] * cw_ref[CONV_W - 1:CONV_W, cols]
            for j in range(CONV_W - 1):
                tap = ubuf[pl.ds(halo + r * sub - (CONV_W - 1 - j) * stride, sub), :]
                y = y + tap * cw_ref[j:j + 1, cols]
            ys[r][half] = y
    for r in range(nsub):
        act[rows[r], :] = (jax.nn.gelu(ys[r][0]) * ys[r][1]).astype(BF16)
    f = [_dot(act[rows[r], :], wdn_ref[...]) + bdn_ref[...] for r in range(nsub)]
    for r in range(nsub):
        o_ref[rows[r], :] = _ln(alpha * x[r] + mod(g2_ref, r) * f[r]) * lg_ref[...] + lb_ref[...]


def _ffn(x, sh, sc, g2, cb0, w, tm, stride, alpha):
    bsz, seq, _ = x.shape
    mrows = sh.shape[1]
    halo = cb0.shape[1]
    cs = pl.BlockSpec((None, halo, 2 * D_FF), lambda b, i: (b, 0, 0))
    weights = [w["wup"], w["bup"], w["cw"], w["cbias"], w["wdn"], w["bdn"], w["ln2_g"], w["ln2_b"]]
    return pl.pallas_call(
        functools.partial(_ffn_body, alpha, tm, stride, halo),
        out_shape=[jax.ShapeDtypeStruct((bsz, seq, D_MODEL), F32),
                   jax.ShapeDtypeStruct((bsz, halo, 2 * D_FF), F32)],
        grid=(bsz, seq // tm),
        in_specs=[_tok_spec(tm, D_MODEL), _mod_spec(mrows, tm), _mod_spec(mrows, tm),
                  _mod_spec(mrows, tm), cs] + [_const_spec(a.shape) for a in weights],
        out_specs=[_tok_spec(tm, D_MODEL), cs],
        scratch_shapes=[pltpu.VMEM((halo + tm, D_FF), F32), pltpu.VMEM((tm, D_FF), BF16)],
        compiler_params=_params(2),
        name="ffn",
    )(x, sh, sc, g2, cb0, *weights)


_O_GATE = 2 * DQK_M + DV_M
_O_OG = _O_GATE + 2 * NH_M
_O_QA = _O_OG + DV_M
_O_KA = _O_QA + DQ_A
_O_VA = _O_KA + DKV_A
_O_GM = _O_VA + DKV_A
_PREP_ROWS = 128


def _split_w_in_body(wt_ref, qk_ref, vm_ref, vmt_ref, g_ref, gates_ref, qa_ref, k_ref, v_ref, vt_ref):
    piece = lambda lo, hi: wt_ref[lo:hi, :]
    qk_ref[...] = piece(0, 2 * DQK_M).T.astype(BF16)
    vm = piece(2 * DQK_M, _O_GATE)
    vmt_ref[...] = vm.astype(BF16)
    vm_ref[...] = vm.T.astype(BF16)
    pad = jnp.zeros((LANES - 2 * NH_M, _PREP_ROWS), F32)
    g_ref[...] = jnp.concatenate([piece(_O_GATE, _O_OG), pad], axis=0).T.astype(BF16)
    gates_ref[:, :DV_M] = piece(_O_OG, _O_QA).T.astype(BF16)
    gates_ref[:, DV_M:] = piece(_O_GM, wt_ref.shape[0]).T.astype(BF16)
    qa_ref[...] = piece(_O_QA, _O_KA).T.astype(BF16)
    k_ref[...] = piece(_O_KA, _O_VA).T.astype(BF16)
    v = piece(_O_VA, _O_GM)
    vt_ref[...] = v.astype(BF16)
    v_ref[...] = v.T.astype(BF16)


def _split_w_in(w_in_t):
    d_in = w_in_t.shape[0]
    rows = lambda n: ((D_MODEL, n), pl.BlockSpec((_PREP_ROWS, n), lambda i: (i, 0)))
    cols = lambda n: ((n, D_MODEL), pl.BlockSpec((n, _PREP_ROWS), lambda i: (0, i)))
    outs = dict(wqk=rows(2 * DQK_M), wvm=rows(DV_M), wvm_t=cols(DV_M), wg=rows(LANES),
                wgates=rows(DV_M + d_in - _O_GM), wqa=rows(DQ_A), wk=rows(DKV_A), wv=rows(DKV_A),
                wv_t=cols(DKV_A))
    res = pl.pallas_call(
        _split_w_in_body,
        out_shape=[jax.ShapeDtypeStruct(s, BF16) for s, _ in outs.values()],
        grid=(D_MODEL // _PREP_ROWS,),
        in_specs=[pl.BlockSpec((d_in, _PREP_ROWS), lambda i: (0, i))],
        out_specs=[spec for _, spec in outs.values()],
        compiler_params=_params(1),
        name="split_w_in",
    )(w_in_t)
    return dict(zip(outs.keys(), res))


def _prep_weights(w_in, b_in, mlstm_norm_w, w_branch_m, w_branch_a, w_out, ln1_g, ln1_b,
                  w_up, b_up, conv_w, conv_b, w_down, b_down, ln2_g, ln2_b):
    row = lambda a: a.reshape(1, -1)
    gate_pad = LANES - 2 * NH_M
    b_k, b_v = b_in[_O_KA:_O_VA], b_in[_O_VA:_O_GM]
    return dict(
        **_split_w_in(w_in.T),
        bqk=row(b_in[:2 * DQK_M]),
        bvm_row=row(b_in[2 * DQK_M:_O_GATE]), bvm_col=b_in[2 * DQK_M:_O_GATE].reshape(-1, 1),
        bg=row(jnp.pad(b_in[_O_GATE:_O_OG], (0, gate_pad))),
        bqa=row(b_in[_O_QA:_O_KA]),
        bk=row(b_k),
        bv_row=row(b_v), bv_col=b_v.reshape(-1, 1),
        bgates=row(jnp.concatenate([b_in[_O_OG:_O_QA], b_in[_O_GM:]])),
        norm_w=row(mlstm_norm_w), norm_w_col=mlstm_norm_w.reshape(-1, 1),
        wbm=w_branch_m.astype(BF16), wba=w_branch_a.astype(BF16),
        wo=w_out.astype(BF16), ln1_g=row(ln1_g), ln1_b=row(ln1_b),
        wup=w_up.astype(BF16), bup=row(b_up), cw=conv_w, cbias=row(conv_b),
        wdn=w_down.astype(BF16), bdn=row(b_down), ln2_g=row(ln2_g), ln2_b=row(ln2_b))


def kernel(x_prompt, x_sample, c_prompt, c_sample, state_mlstm_C, state_mlstm_n, state_mlstm_m,
           cache_k_win, cache_v_win, state_ffn_conv, w_ada, b_ada, w_in, b_in, mlstm_norm_w,
           attn_sinks, w_branch_m, w_branch_a, w_out, ln1_g, ln1_b, w_up, b_up, conv_w, conv_b,
           w_down, b_down, ln2_g, ln2_b):
    depth = w_in.shape[0]
    bp, lp, _ = x_prompt.shape
    bs, ls, _ = x_sample.shape
    assert cache_k_win.shape[2] == WINDOW
    alpha = (2 * depth) ** 0.25
    dt = x_prompt.dtype
    slopes = jnp.exp2(-8.0 * jnp.arange(1, NH_A + 1, dtype=F32) / NH_A)
    tm_p = 512
    tm_in = 1024
    ns = bs * ls

    yp = x_prompt
    to_pos = lambda a: a.transpose(1, 0, 2).reshape(1, -1, a.shape[-1])
    ys = to_pos(x_sample)
    new_p, new_s = [], []
    n_c = bp + bs
    c_rows = -(-n_c // SUBLANES) * SUBLANES
    c_all = jnp.concatenate([c_sample, c_prompt, jnp.zeros((c_rows - n_c, D_MODEL), dt)], axis=0)
    for l in range(depth):
        w = _prep_weights(w_in[l], b_in[l], mlstm_norm_w[l], w_branch_m[l], w_branch_a[l], w_out[l],
                          ln1_g[l], ln1_b[l], w_up[l], b_up[l], conv_w[l], conv_b[l], w_down[l],
                          b_down[l], ln2_g[l], ln2_b[l])
        prm = jnp.stack([slopes, attn_sinks[l].astype(F32)])
        mod = _ada(c_all, w_ada[l], b_ada[l])
        mod_s = mod[:bs].reshape(bs, N_MOD, D_MODEL)
        mod_p = mod[bs:bs + bp].reshape(bp, N_MOD, D_MODEL)
        mp_ = [mod_p[:, j:j + 1] for j in range(N_MOD)]
        ms_ = [jnp.tile(mod_s[:, j], (ls, 1))[None] for j in range(N_MOD)]

        hp, qk, v, g, qa, kb, ka, vat = _inproj(yp, mp_[0], mp_[1], w, tm_in, True)
        hm, cp, np_, mp = _mlstm_prompt(qk, v, g, jnp.zeros((bp, NH_M, DHV_M, DHK_M), dt),
                                        jnp.zeros((bp, NH_M, DHK_M), dt), jnp.zeros((bp, 1, LANES), dt),
                                        w["norm_w_col"])
        ha = _swa_prompt(prm, qa, kb, vat)
        x1p = _merge(yp, hp, mp_[2], hm, ha, w, tm_p, alpha)
        halo_p = SUBLANES
        yp, csp = _ffn(x1p, mp_[3], mp_[4], mp_[5], jnp.zeros((bp, halo_p, 2 * D_FF), dt),
                       w, tm_p, 1, alpha)
        p_k = ka[:, lp - WINDOW:].reshape(bp, WINDOW, NKV_A, HD_A)
        p_v = vat[:, :, lp - WINDOW:].reshape(bp, NKV_A, HD_A, WINDOW).transpose(0, 3, 1, 2)
        new_p.append((cp, np_, mp[:, 0, :NH_M], p_k, p_v, csp[:, halo_p - (CONV_W - 1):]))

        hs, qk, v, g, qa, kn, vn = _inproj(ys, ms_[0], ms_[1], w, ns, False)
        per_seq = lambda a: a.reshape(ls, bs, a.shape[-1])
        m0 = jnp.pad(state_mlstm_m[l], ((0, 0), (0, LANES - NH_M)))[:, None, :]
        hm, cs_, ns_, ms = _mlstm_sample(per_seq(qk), per_seq(v), per_seq(g), state_mlstm_C[l],
                                         state_mlstm_n[l], m0, w["norm_w"], SEQS_PER_STEP)
        to_t = lambda a: a.transpose(0, 2, 3, 1).reshape(bs, DKV_A, WINDOW)
        from_t = lambda a: a.reshape(bs, NKV_A, HD_A, WINDOW).transpose(0, 3, 1, 2)
        ha, kct, vct = _swa_sample(prm, per_seq(qa), per_seq(kn), per_seq(vn),
                                   to_t(cache_k_win[l]), to_t(cache_v_win[l]), SEQS_PER_STEP)
        x1s = _merge(ys, hs, ms_[2], hm.reshape(1, ns, DV_M), ha.reshape(1, ns, DQ_A), w, ns, alpha)
        ys, css = _ffn(x1s, ms_[3], ms_[4], ms_[5], to_pos(state_ffn_conv[l]), w, ns, bs, alpha)
        conv_s = css.reshape(CONV_W - 1, bs, 2 * D_FF).transpose(1, 0, 2)
        new_s.append((cs_, ns_, ms[:, 0, :NH_M], from_t(kct), from_t(vct), conv_s))

    p_state = [jnp.stack(a) for a in zip(*new_p)]
    s_state = [jnp.stack(a) for a in zip(*new_s)]
    y_sample = ys.reshape(ls, bs, D_MODEL).transpose(1, 0, 2)
    return (yp, y_sample, *p_state, *s_state)
```

```python
import functools

import jax
import jax.numpy as jnp
from jax import lax
from jax.experimental import pallas as pl
from jax.experimental.pallas import tpu as pltpu

F32 = jnp.float32
BF16 = jnp.bfloat16

D_MODEL = 1024
NH_M, DHK_M, DHV_M = 4, 128, 256
DQK_M, DV_M = NH_M * DHK_M, NH_M * DHV_M
NH_A, NKV_A, HD_A = 16, 4, 64
GROUP_A = NH_A // NKV_A
WINDOW = 128
DQ_A, DKV_A = NH_A * HD_A, NKV_A * HD_A
D_FF = 2816
CONV_W = 3
N_MOD = 6
LN_EPS = 1e-5
CHUNK = 128
NEG = -1e30
LANES = 128
SUBLANES = 8
VMEM_LIMIT = 56 * 1024 * 1024
SEQS_PER_STEP = 8


def _ln(x):
    mu = jnp.mean(x, axis=-1, keepdims=True)
    xc = x - mu
    var = jnp.mean(xc * xc, axis=-1, keepdims=True)
    return xc * lax.rsqrt(var + LN_EPS)


def _dot(a, b):
    return jnp.dot(a, b, preferred_element_type=F32)


def _dot_nt(a, b):
    return lax.dot_general(a, b, (((1,), (1,)), ((), ())), preferred_element_type=F32)


def _dot_tn(a, b):
    return lax.dot_general(a, b, (((0,), (0,)), ((), ())), preferred_element_type=F32)


def _const_spec(shape):
    nd = len(shape)
    return pl.BlockSpec(shape, lambda *_: (0,) * nd, pipeline_mode=pl.Buffered(1))


def _params(n_grid):
    return pltpu.CompilerParams(dimension_semantics=("arbitrary",) * n_grid,
                                vmem_limit_bytes=VMEM_LIMIT)


def _first_step():
    return (pl.program_id(0) == 0) & (pl.program_id(1) == 0)


def _ada_body(c_ref, w_ref, b_ref, o_ref):
    c = c_ref[...]
    s = (c * jax.nn.sigmoid(c)).astype(BF16)
    o_ref[...] = _dot(s, w_ref[...].astype(BF16)) + b_ref[...]


def _ada(c, w_ada, b_ada):
    rows = c.shape[0]
    n_out = w_ada.shape[1]
    bn = 1536
    assert n_out % bn == 0
    return pl.pallas_call(
        _ada_body,
        out_shape=jax.ShapeDtypeStruct((rows, n_out), F32),
        grid=(n_out // bn,),
        in_specs=[pl.BlockSpec((rows, D_MODEL), lambda j: (0, 0)),
                  pl.BlockSpec((D_MODEL, bn), lambda j: (0, j)),
                  pl.BlockSpec((1, bn), lambda j: (0, j))],
        out_specs=pl.BlockSpec((rows, bn), lambda j: (0, j)),
        compiler_params=_params(1),
        name="ada",
    )(c, w_ada, b_ada.reshape(1, n_out))


def _inproj_body(prompt, x_ref, sh_ref, sc_ref, wqk_ref, bqk_ref, wvm_ref, bvm_ref, wg_ref, bg_ref,
                 wqa_ref, bqa_ref, wk_ref, bk_ref, wv_ref, bv_ref, h_ref, qk_ref, v_ref, g_ref, qa_ref, *rest):
    h = (_ln(x_ref[...]) * (1.0 + sc_ref[...]) + sh_ref[...]).astype(BF16)
    h_ref[...] = h

    def proj(w_ref, b_ref, lo, n):
        return _dot(h, w_ref[:, lo:lo + n]) + b_ref[:, lo:lo + n]

    qk_ref[:, :DQK_M] = (proj(wqk_ref, bqk_ref, 0, DQK_M) * DHK_M ** -0.5).astype(qk_ref.dtype)
    qk_ref[:, DQK_M:] = proj(wqk_ref, bqk_ref, DQK_M, DQK_M).astype(qk_ref.dtype)
    if prompt:
        v_ref[...] = (_dot_nt(wvm_ref[...], h) + bvm_ref[...]).astype(v_ref.dtype)
    else:
        v_ref[...] = proj(wvm_ref, bvm_ref, 0, DV_M).astype(v_ref.dtype)
    g_ref[...] = proj(wg_ref, bg_ref, 0, LANES)
    qa_ref[...] = (proj(wqa_ref, bqa_ref, 0, DQ_A) * HD_A ** -0.5).astype(qa_ref.dtype)
    ka = proj(wk_ref, bk_ref, 0, DKV_A)
    if prompt:
        kb_ref, ka_ref, vat_ref = rest
        kb_ref[...] = ka.astype(BF16)
        ka_ref[...] = ka
        vat_ref[...] = _dot_nt(wv_ref[...], h) + bv_ref[...]
    else:
        ka_ref, va_ref = rest
        ka_ref[...] = ka
        va_ref[...] = _dot(h, wv_ref[...]) + bv_ref[...]


def _tok_spec(tm, n):
    return pl.BlockSpec((None, tm, n), lambda b, i: (b, i, 0))


def _mod_spec(rows, tm):
    if rows == 1:
        return pl.BlockSpec((None, 1, D_MODEL), lambda b, i: (b, 0, 0))
    return pl.BlockSpec((None, tm, D_MODEL), lambda b, i: (b, i, 0))


def _inproj(x, sh, sc, w, tm, prompt):
    bsz, seq, _ = x.shape
    mrows = sh.shape[1]
    act = BF16 if prompt else F32
    t_spec = lambda n: pl.BlockSpec((None, n, tm), lambda b, i: (b, 0, i))
    outs = [((bsz, seq, D_MODEL), BF16, _tok_spec(tm, D_MODEL)),
            ((bsz, seq, 2 * DQK_M), act, _tok_spec(tm, 2 * DQK_M)),
            ((bsz, DV_M, seq), act, t_spec(DV_M)) if prompt else ((bsz, seq, DV_M), act, _tok_spec(tm, DV_M)),
            ((bsz, seq, LANES), F32, _tok_spec(tm, LANES)),
            ((bsz, seq, DQ_A), act, _tok_spec(tm, DQ_A))]
    if prompt:
        wvm, bvm, wv, bv = w["wvm_t"], w["bvm_col"], w["wv_t"], w["bv_col"]
        outs += [((bsz, seq, DKV_A), BF16, _tok_spec(tm, DKV_A)),
                 ((bsz, seq, DKV_A), F32, _tok_spec(tm, DKV_A)),
                 ((bsz, DKV_A, seq), F32, t_spec(DKV_A))]
    else:
        wvm, bvm, wv, bv = w["wvm"], w["bvm_row"], w["wv"], w["bv_row"]
        outs += [((bsz, seq, DKV_A), F32, _tok_spec(tm, DKV_A)),
                 ((bsz, seq, DKV_A), F32, _tok_spec(tm, DKV_A))]
    weights = [w["wqk"], w["bqk"], wvm, bvm, w["wg"], w["bg"], w["wqa"], w["bqa"], w["wk"], w["bk"],
               wv, bv]
    return pl.pallas_call(
        functools.partial(_inproj_body, prompt),
        out_shape=[jax.ShapeDtypeStruct(s, dt) for s, dt, _ in outs],
        grid=(bsz, seq // tm),
        in_specs=[_tok_spec(tm, D_MODEL), _mod_spec(mrows, tm), _mod_spec(mrows, tm)]
                 + [_const_spec(a.shape) for a in weights],
        out_specs=[spec for _, _, spec in outs],
        compiler_params=_params(2),
        name="inproj",
    )(x, sh, sc, *weights)


def _scan_rows(x, op, rows):
    row = lax.broadcasted_iota(jnp.int32, x.shape, 0)
    d = 1
    while d < rows:
        shifted = pltpu.roll(x, d, axis=0)
        x = jnp.where(row >= d, op(x, shifted), x)
        d *= 2
    return x


def _mlstm_sample_body(t_in, nb, qk_ref, v_ref, g_ref, c0_ref, n0_ref, m0_ref, nw_ref,
                       h_ref, c_ref, n_ref, m_ref, *pads):
    t = SUBLANES
    c_ref[...] = c0_ref[...]
    n_ref[...] = n0_ref[...]
    m_ref[...] = m0_ref[...]

    @pl.when(_first_step())
    def _():
        for p in pads:
            p[...] = jnp.zeros(p.shape, p.dtype)
    for src, dst in zip((qk_ref, v_ref, g_ref), pads):
        for s in range(nb):
            dst[s, pl.ds(0, t_in), :] = src[:, s, :]
    qk_src, v_src, g_src = pads

    row = lax.broadcasted_iota(jnp.int32, (t, LANES), 0)
    lane = lax.broadcasted_iota(jnp.int32, (t, LANES), 1)
    lane1 = lax.broadcasted_iota(jnp.int32, (1, LANES), 1)
    r2 = lax.broadcasted_iota(jnp.int32, (t, t), 0)
    c2 = lax.broadcasted_iota(jnp.int32, (t, t), 1)
    causal = c2 <= r2
    pad_gate = jnp.where(lane < NH_M, NEG, -NEG)

    seqs = []
    for s in range(nb):
        ga = _gate_algebra(jnp.where(row < t_in, g_src[s], pad_gate), m_ref[s], t)
        ga["gt"] = ga["gg"].T
        m_ref[s] = jnp.where(lane1 < NH_M, ga["m_new"], 0.0)
        seqs.append(ga)

    units = [(s, hd) for s in range(nb) for hd in range(NH_M)]
    nu = len(units)
    col = lambda name, s, hd: seqs[s][name][:, hd:hd + 1]
    q = [qk_src[s, :, hd * DHK_M:(hd + 1) * DHK_M].astype(BF16) for s, hd in units]
    k = [qk_src[s, :, DQK_M + hd * DHK_M:DQK_M + (hd + 1) * DHK_M].astype(BF16) for s, hd in units]
    v = [v_src[s, :, hd * DHV_M:(hd + 1) * DHV_M].astype(BF16) for s, hd in units]
    c_old = [c_ref[s, hd] for s, hd in units]
    n_old = [n_ref[s, hd:hd + 1, :] for s, hd in units]

    sqk = [_dot_nt(q[u], k[u]) for u in range(nu)]
    dexp = [jnp.exp(jnp.where(causal, col("eb", s, hd) + seqs[s]["gt"][hd:hd + 1, :], NEG))
            for s, hd in units]
    smat = [sqk[u] * dexp[u] for u in range(nu)]
    intra = [_dot(smat[u].astype(BF16), v[u]) for u in range(nu)]
    inter = [_dot_nt(q[u], c_old[u].astype(BF16)) for u in range(nu)]
    qn = [jnp.sum(q[u].astype(F32) * n_old[u].astype(BF16).astype(F32), axis=-1, keepdims=True)
          for u in range(nu)]
    den = [jnp.sum(smat[u], axis=-1, keepdims=True) + col("a_in", s, hd) * qn[u]
           for u, (s, hd) in enumerate(units)]
    hh = [(intra[u] + col("a_in", s, hd) * inter[u])
          / jnp.maximum(jnp.abs(den[u]), col("lowb", s, hd)) for u, (s, hd) in enumerate(units)]
    hn = [_ln(hh[u]) * nw_ref[:, hd * DHV_M:(hd + 1) * DHV_M] for u, (s, hd) in enumerate(units)]
    for u, (s, hd) in enumerate(units):
        h_ref[:, s, hd * DHV_M:(hd + 1) * DHV_M] = hn[u][:t_in]

    kw = [k[u].astype(F32) * col("ws", s, hd) for u, (s, hd) in enumerate(units)]
    upd = [_dot_tn(v[u], kw[u].astype(BF16)) for u in range(nu)]
    for u, (s, hd) in enumerate(units):
        dec = seqs[s]["decay"][:, hd:hd + 1]
        c_ref[s, hd] = dec * c_old[u] + upd[u]
        n_ref[s, hd:hd + 1, :] = dec * n_old[u] + jnp.sum(kw[u], axis=0, keepdims=True)


def _mlstm_sample(qk, v, g, c0, n0, m0, norm_w, nb):
    t_in, nseq, _ = qk.shape
    assert t_in <= SUBLANES
    blk = lambda n: pl.BlockSpec((t_in, nb, n), lambda o, c: (0, o, 0))
    st_c = pl.BlockSpec((nb, NH_M, DHV_M, DHK_M), lambda o, c: (o, 0, 0, 0))
    st_n = pl.BlockSpec((nb, NH_M, DHK_M), lambda o, c: (o, 0, 0))
    st_m = pl.BlockSpec((nb, 1, LANES), lambda o, c: (o, 0, 0))
    return pl.pallas_call(
        functools.partial(_mlstm_sample_body, t_in, nb),
        out_shape=[jax.ShapeDtypeStruct((t_in, nseq, DV_M), F32),
                   jax.ShapeDtypeStruct((nseq, NH_M, DHV_M, DHK_M), F32),
                   jax.ShapeDtypeStruct((nseq, NH_M, DHK_M), F32),
                   jax.ShapeDtypeStruct((nseq, 1, LANES), F32)],
        grid=(nseq // nb, 1),
        in_specs=[blk(2 * DQK_M), blk(DV_M), blk(LANES), st_c, st_n, st_m,
                  pl.BlockSpec((1, DV_M), lambda o, c: (0, 0))],
        out_specs=[blk(DV_M), st_c, st_n, st_m],
        scratch_shapes=[pltpu.VMEM((nb, SUBLANES, 2 * DQK_M), F32), pltpu.VMEM((nb, SUBLANES, DV_M), F32),
                        pltpu.VMEM((nb, SUBLANES, LANES), F32)],
        compiler_params=_params(2),
        name="mlstm_sample",
    )(qk, v, g, c0, n0, m0, norm_w)


def _gate_algebra(g, m_prev, t):
    b = pltpu.roll(_scan_rows(jax.nn.log_sigmoid(g), jnp.add, t), LANES - NH_M, axis=1)
    gg = g - b
    gmx = _scan_rows(gg, jnp.maximum, t)
    a = b + m_prev
    mt = jnp.maximum(a, b + gmx)
    b_last, gmx_last = b[t - 1:t, :], gmx[t - 1:t, :]
    m_new = jnp.maximum(b_last + m_prev, b_last + gmx_last)
    return dict(gg=gg, eb=b - mt, a_in=jnp.exp(a - mt), lowb=jnp.exp(-mt),
                ws=jnp.exp(b_last + gg - m_new), decay=jnp.exp(b_last + m_prev - m_new), m_new=m_new)


MLSTM_CHUNKS_PER_STEP = 4


def _mlstm_prompt_body(nb, qk_ref, vt_ref, g_ref, c0_ref, n0_ref, m0_ref, nwc_ref,
                       h_ref, c_ref, n_ref, m_ref, nwb):
    @pl.when(pl.program_id(1) == 0)
    def _():
        c_ref[...] = c0_ref[...]
        n_ref[...] = n0_ref[...]
        m_ref[...] = m0_ref[...]

    @pl.when(_first_step())
    def _():
        for hd in range(NH_M):
            nwb[hd] = jnp.broadcast_to(nwc_ref[hd * DHV_M:(hd + 1) * DHV_M, :], (DHV_M, CHUNK))

    for ci in range(MLSTM_CHUNKS_PER_STEP):
        _mlstm_prompt_chunk(nb, slice(ci * CHUNK, (ci + 1) * CHUNK), qk_ref, vt_ref, g_ref, nwb,
                            h_ref, c_ref, n_ref, m_ref)


def _mlstm_prompt_chunk(nb, rows, qk_ref, vt_ref, g_ref, nwb, h_ref, c_ref, n_ref, m_ref):
    t = CHUNK
    lane1 = lax.broadcasted_iota(jnp.int32, (1, LANES), 1)
    r2 = lax.broadcasted_iota(jnp.int32, (t, t), 0)
    c2 = lax.broadcasted_iota(jnp.int32, (t, t), 1)
    causal = r2 <= c2

    seqs = []
    for s in range(nb):
        ga = _gate_algebra(g_ref[s, rows, :], m_ref[s], t)
        m_ref[s] = jnp.where(lane1 < NH_M, ga["m_new"], 0.0)
        for name in ("eb", "a_in", "lowb"):
            ga[name + "_t"] = ga[name].T
        seqs.append(ga)

    units = [(s, hd) for s in range(nb) for hd in range(NH_M)]
    nu = len(units)
    rowv = lambda name, s, hd: seqs[s][name + "_t"][hd:hd + 1, :]
    q = [qk_ref[s, rows, hd * DHK_M:(hd + 1) * DHK_M] for s, hd in units]
    k = [qk_ref[s, rows, DQK_M + hd * DHK_M:DQK_M + (hd + 1) * DHK_M] for s, hd in units]
    vt = [vt_ref[s, hd * DHV_M:(hd + 1) * DHV_M, rows] for s, hd in units]
    c_old = [c_ref[s, hd] for s, hd in units]
    n_old = [n_ref[s, hd:hd + 1, :] for s, hd in units]

    skq = [_dot_nt(k[u], q[u]) for u in range(nu)]
    dexp = [jnp.exp(jnp.where(causal, seqs[s]["gg"][:, hd:hd + 1] + rowv("eb", s, hd), NEG))
            for s, hd in units]
    smat = [skq[u] * dexp[u] for u in range(nu)]
    intra = [_dot(vt[u], smat[u].astype(BF16)) for u in range(nu)]
    inter = [_dot_nt(c_old[u].astype(BF16), q[u]) for u in range(nu)]
    qn = [_dot_nt(jnp.broadcast_to(n_old[u], (SUBLANES, DHK_M)).astype(BF16), q[u])[0:1, :]
          for u in range(nu)]
    den = [jnp.sum(smat[u], axis=0, keepdims=True) + rowv("a_in", s, hd) * qn[u]
           for u, (s, hd) in enumerate(units)]
    inv = [1.0 / jnp.maximum(jnp.abs(den[u]), rowv("lowb", s, hd)) for u, (s, hd) in enumerate(units)]
    hh = [(intra[u] + rowv("a_in", s, hd) * inter[u]) * inv[u] for u, (s, hd) in enumerate(units)]
    mu = [jnp.mean(hh[u], axis=0, keepdims=True) for u in range(nu)]
    xc = [hh[u] - mu[u] for u in range(nu)]
    var = [jnp.mean(xc[u] * xc[u], axis=0, keepdims=True) for u in range(nu)]
    hn = [xc[u] * lax.rsqrt(var[u] + LN_EPS) * nwb[hd] for u, (s, hd) in enumerate(units)]
    for u, (s, hd) in enumerate(units):
        h_ref[s, rows, hd * DHV_M:(hd + 1) * DHV_M] = hn[u].T

    kw = [(k[u].astype(F32) * seqs[s]["ws"][:, hd:hd + 1]).astype(BF16) for u, (s, hd) in enumerate(units)]
    upd = [_dot(vt[u], kw[u]) for u in range(nu)]
    nupd = [_dot(jnp.ones((SUBLANES, t), BF16), kw[u])[0:1, :] for u in range(nu)]
    for u, (s, hd) in enumerate(units):
        dec = seqs[s]["decay"][:, hd:hd + 1]
        c_ref[s, hd] = dec * c_old[u] + upd[u]
        n_ref[s, hd:hd + 1, :] = dec * n_old[u] + nupd[u]


def _mlstm_prompt(qk, vt, g, c0, n0, m0, norm_w_col):
    nb, seq, _ = qk.shape
    ts = MLSTM_CHUNKS_PER_STEP * CHUNK
    tok = lambda n: pl.BlockSpec((nb, ts, n), lambda o, c: (0, c, 0))
    st_c = pl.BlockSpec((nb, NH_M, DHV_M, DHK_M), lambda o, c: (0, 0, 0, 0))
    st_n = pl.BlockSpec((nb, NH_M, DHK_M), lambda o, c: (0, 0, 0))
    st_m = pl.BlockSpec((nb, 1, LANES), lambda o, c: (0, 0, 0))
    return pl.pallas_call(
        functools.partial(_mlstm_prompt_body, nb),
        out_shape=[jax.ShapeDtypeStruct((nb, seq, DV_M), F32),
                   jax.ShapeDtypeStruct((nb, NH_M, DHV_M, DHK_M), F32),
                   jax.ShapeDtypeStruct((nb, NH_M, DHK_M), F32),
                   jax.ShapeDtypeStruct((nb, 1, LANES), F32)],
        grid=(1, seq // ts),
        in_specs=[tok(2 * DQK_M), pl.BlockSpec((nb, DV_M, ts), lambda o, c: (0, 0, c)), tok(LANES),
                  st_c, st_n, st_m, pl.BlockSpec((DV_M, 1), lambda o, c: (0, 0))],
        out_specs=[tok(DV_M), st_c, st_n, st_m],
        scratch_shapes=[pltpu.VMEM((NH_M, DHV_M, CHUNK), F32)],
        compiler_params=_params(2),
        name="mlstm_prompt",
    )(qk, vt, g, c0, n0, m0, norm_w_col)


SWA_BLOCKS_PER_STEP = 16


def _swa_prompt_body(prm_ref, q_ref, kc_ref, kp_ref, vc_ref, vp_ref, o_ref, tbl):
    row = lax.broadcasted_iota(jnp.int32, (WINDOW, WINDOW), 0)
    col = lax.broadcasted_iota(jnp.int32, (WINDOW, WINDOW), 1)
    tri = row <= col

    @pl.when(_first_step())
    def _():
        dist = jnp.where(tri, col - row, col - row + WINDOW).astype(F32)
        for hd in range(NH_A):
            bias = prm_ref[0, hd] * dist
            tbl[0, hd] = bias
            tbl[1, hd] = bias + jnp.where(tri, 0.0, -NEG)

    for b in range(SWA_BLOCKS_PER_STEP):
        rows = pl.ds(b * WINDOW, WINDOW)
        which = jnp.where(pl.program_id(1) == 0, 1, 0) if b == 0 else 0
        k_prev = kp_ref if b == 0 else kc_ref.at[pl.ds((b - 1) * WINDOW, WINDOW), :]
        v_prev = vp_ref if b == 0 else vc_ref.at[:, pl.ds((b - 1) * WINDOW, WINDOW)]
        _swa_prompt_block(which, prm_ref, q_ref.at[rows, :], kc_ref.at[rows, :], k_prev,
                          vc_ref.at[:, rows], v_prev, o_ref.at[rows, :], tbl)


def _swa_prompt_block(which, prm_ref, q_ref, kc_ref, kp_ref, vc_ref, vp_ref, o_ref, tbl):
    row = lax.broadcasted_iota(jnp.int32, (WINDOW, WINDOW), 0)
    col = lax.broadcasted_iota(jnp.int32, (WINDOW, WINDOW), 1)
    tri = row <= col
    lo = col < HD_A
    zb = jnp.zeros((WINDOW, LANES), BF16)

    def placed(k_ref, kv):
        c, par = kv // 2, kv % 2
        own = k_ref[:, c * LANES:(c + 1) * LANES]
        swp = pltpu.roll(own, HD_A, axis=1)
        if par == 0:
            return jnp.where(lo, own, zb), jnp.where(lo, zb, swp)
        return jnp.where(lo, swp, zb), jnp.where(lo, zb, own)

    st = []
    for kv in range(NKV_A):
        lhs = jnp.concatenate([*placed(kc_ref, kv), *placed(kp_ref, kv)], axis=0)
        qg = jnp.concatenate([q_ref[:, (2 * kv) * LANES:(2 * kv + 1) * LANES],
                              q_ref[:, (2 * kv + 1) * LANES:(2 * kv + 2) * LANES]], axis=0)
        st.append(_dot_nt(lhs, qg))
    tiles = []
    for hd in range(NH_A):
        kv, a, par = hd // GROUP_A, (hd % GROUP_A) // 2, hd % 2
        cols = slice(a * WINDOW, (a + 1) * WINDOW)
        tiles.append(jnp.where(tri, st[kv][par * WINDOW:(par + 1) * WINDOW, cols],
                               st[kv][(2 + par) * WINDOW:(3 + par) * WINDOW, cols]))
    sc = jnp.concatenate(tiles, axis=0).reshape(NH_A, WINDOW, WINDOW) - tbl[which]
    sink = jnp.concatenate([jnp.full((1, 1, WINDOW), prm_ref[1, hd], F32) for hd in range(NH_A)], axis=0)
    mx = jnp.maximum(jnp.max(sc, axis=1, keepdims=True), sink)
    p = jnp.exp(sc - mx)
    den = jnp.sum(p, axis=1, keepdims=True) + jnp.exp(sink - mx)
    pn = p * (1.0 / den)

    zero = jnp.zeros((WINDOW, WINDOW), F32)
    z64 = jnp.zeros((HD_A, WINDOW), BF16)
    outs = []
    for kv in range(NKV_A):
        cols = []
        for a in range(2):
            pe, po = pn[kv * GROUP_A + 2 * a], pn[kv * GROUP_A + 2 * a + 1]
            cols.append(jnp.concatenate([jnp.where(tri, pe, zero), jnp.where(tri, po, zero),
                                         jnp.where(tri, zero, pe), jnp.where(tri, zero, po)],
                                        axis=0).astype(BF16))
        pt = jnp.concatenate(cols, axis=1)
        vc = vc_ref[kv * HD_A:(kv + 1) * HD_A, :].astype(BF16)
        vp = vp_ref[kv * HD_A:(kv + 1) * HD_A, :].astype(BF16)
        vt = jnp.concatenate([jnp.concatenate([vc, z64], axis=0), jnp.concatenate([z64, vc], axis=0),
                              jnp.concatenate([vp, z64], axis=0), jnp.concatenate([z64, vp], axis=0)],
                             axis=1)
        ot = _dot(vt, pt)
        outs += [ot[:, :WINDOW].T, ot[:, WINDOW:].T]
    o_ref[...] = jnp.concatenate(outs, axis=1).astype(o_ref.dtype)


def _swa_prompt(prm, q, kb, vat):
    bsz, seq, _ = q.shape
    tq = SWA_BLOCKS_PER_STEP * WINDOW
    prev = lambda i: jnp.maximum(i * SWA_BLOCKS_PER_STEP - 1, 0)
    return pl.pallas_call(
        _swa_prompt_body,
        out_shape=jax.ShapeDtypeStruct((bsz, seq, DQ_A), BF16),
        grid=(bsz, seq // tq),
        in_specs=[pl.BlockSpec(memory_space=pltpu.SMEM),
                  pl.BlockSpec((None, tq, DQ_A), lambda b, i: (b, i, 0)),
                  pl.BlockSpec((None, tq, DKV_A), lambda b, i: (b, i, 0)),
                  pl.BlockSpec((None, WINDOW, DKV_A), lambda b, i: (b, prev(i), 0)),
                  pl.BlockSpec((None, DKV_A, tq), lambda b, i: (b, 0, i)),
                  pl.BlockSpec((None, DKV_A, WINDOW), lambda b, i: (b, 0, prev(i)))],
        out_specs=pl.BlockSpec((None, tq, DQ_A), lambda b, i: (b, i, 0)),
        scratch_shapes=[pltpu.VMEM((2, NH_A, WINDOW, WINDOW), F32)],
        compiler_params=_params(2),
        name="swa_prompt",
    )(prm, q, kb, kb, vat, vat)


def _half_mask(shape, half):
    lane = lax.broadcasted_iota(jnp.int32, shape, 1)
    return lane < HD_A if half == 0 else lane >= HD_A


def _swa_sample_body(t_in, nb, prm_ref, q_ref, kn_ref, vn_ref, kct_ref, vct_ref,
                     o_ref, kco_ref, vco_ref, tbl, q8, kn_pad, vn_pad):
    tq = SUBLANES
    rows = NH_A * tq
    row = lax.broadcasted_iota(jnp.int32, (rows, WINDOW), 0)
    col = lax.broadcasted_iota(jnp.int32, (rows, WINDOW), 1)
    tri = col <= (row & (tq - 1))

    @pl.when(_first_step())
    def _():
        r8 = lax.broadcasted_iota(jnp.int32, (tq, WINDOW), 0)
        c8 = lax.broadcasted_iota(jnp.int32, (tq, WINDOW), 1)
        dist = jnp.where(c8 <= r8, r8 - c8, r8 - c8 + WINDOW).astype(F32)
        for hd in range(NH_A):
            tbl[pl.ds(hd * tq, tq), :] = prm_ref[0, hd] * dist
        for p in (q8, kn_pad, vn_pad):
            p[...] = jnp.zeros(p.shape, p.dtype)

    for s in range(nb):
        q8[s, pl.ds(0, t_in), :] = q_ref[:, s, :]
        kn_pad[s, pl.ds(0, t_in), :] = kn_ref[:, s, :]
        vn_pad[s, pl.ds(0, t_in), :] = vn_ref[:, s, :]
    sink_col = jnp.concatenate([jnp.full((tq, 1), prm_ref[1, hd], F32) for hd in range(NH_A)], axis=0)
    z8 = jnp.zeros((tq, LANES), F32)
    n_chunk = NKV_A // 2
    heads_per_chunk = NH_A // n_chunk

    def place(piece, src_half, dst_half):
        if src_half != dst_half:
            piece = pltpu.roll(piece, HD_A, axis=1)
        return jnp.where(_half_mask(piece.shape, dst_half), piece, z8)

    qexp = []
    for s in range(nb):
        per_c = []
        for c in range(n_chunk):
            pieces = []
            for hl in range(heads_per_chunk):
                hd = c * heads_per_chunk + hl
                pieces.append(place(q8[s, :, (hd // 2) * LANES:(hd // 2 + 1) * LANES],
                                    hd % 2, hl // GROUP_A))
            per_c.append(jnp.concatenate(pieces, axis=0).astype(BF16))
        qexp.append(per_c)
    csl = lambda c: slice(c * LANES, (c + 1) * LANES)
    s_prev = [[_dot(qexp[s][c], kct_ref[s, csl(c), :].astype(BF16)) for c in range(n_chunk)]
              for s in range(nb)]
    s_cur = [[_dot_nt(qexp[s][c], kn_pad[s, :, csl(c)].astype(BF16)) for c in range(n_chunk)]
             for s in range(nb)]
    sc = [jnp.where(tri, jnp.concatenate(s_cur[s], axis=0), jnp.concatenate(s_prev[s], axis=0)) - tbl[...]
          for s in range(nb)]
    mx = [jnp.maximum(jnp.max(sc[s], axis=-1, keepdims=True), sink_col) for s in range(nb)]
    p = [jnp.exp(sc[s] - mx[s]) for s in range(nb)]
    den = [jnp.sum(p[s], axis=-1, keepdims=True) + jnp.exp(sink_col - mx[s]) for s in range(nb)]
    pn = [p[s] * (1.0 / den[s]) for s in range(nb)]
    zero = jnp.zeros((rows, WINDOW), F32)
    pc = [jnp.where(tri, pn[s], zero).astype(BF16) for s in range(nb)]
    pp = [jnp.where(tri, zero, pn[s]).astype(BF16) for s in range(nb)]
    half_rows = heads_per_chunk * tq
    oc = [[_dot_nt(pp[s][c * half_rows:(c + 1) * half_rows], vct_ref[s, csl(c), :].astype(BF16))
           + _dot(pc[s][c * half_rows:(c + 1) * half_rows], vn_pad[s, :, csl(c)].astype(BF16))
           for c in range(n_chunk)] for s in range(nb)]
    for s in range(nb):
        chunks = []
        for pch in range(NH_A // 2):
            acc = None
            for hd in (2 * pch, 2 * pch + 1):
                c, hl = hd // heads_per_chunk, hd % heads_per_chunk
                piece = place(oc[s][c][hl * tq:(hl + 1) * tq, :], hl // GROUP_A, hd % 2)
                acc = piece if acc is None else acc + piece
            chunks.append(acc)
        o_ref[:, s, :] = jnp.concatenate(chunks, axis=1)[:t_in]

    lane = lax.broadcasted_iota(jnp.int32, (DKV_A, WINDOW), 1)
    for new_pad, old_ref, out_ref in ((kn_pad, kct_ref, kco_ref), (vn_pad, vct_ref, vco_ref)):
        for s in range(nb):
            merged = jnp.where(lane < t_in, new_pad[s].T, old_ref[s])
            out_ref[s] = pltpu.roll(merged, WINDOW - t_in, axis=1)


def _swa_sample(prm, q, kn, vn, kct, vct, nb):
    t_in, nseq, _ = q.shape
    assert t_in <= SUBLANES
    cur = lambda n: pl.BlockSpec((t_in, nb, n), lambda o, i: (0, o, 0))
    win = pl.BlockSpec((nb, DKV_A, WINDOW), lambda o, i: (o, 0, 0))
    return pl.pallas_call(
        functools.partial(_swa_sample_body, t_in, nb),
        out_shape=[jax.ShapeDtypeStruct((t_in, nseq, DQ_A), F32),
                   jax.ShapeDtypeStruct((nseq, DKV_A, WINDOW), F32),
                   jax.ShapeDtypeStruct((nseq, DKV_A, WINDOW), F32)],
        grid=(nseq // nb, 1),
        in_specs=[pl.BlockSpec(memory_space=pltpu.SMEM), cur(DQ_A), cur(DKV_A), cur(DKV_A), win, win],
        out_specs=[cur(DQ_A), win, win],
        scratch_shapes=[pltpu.VMEM((NH_A * SUBLANES, WINDOW), F32),
                        pltpu.VMEM((nb, SUBLANES, DQ_A), F32),
                        pltpu.VMEM((nb, WINDOW, DKV_A), F32),
                        pltpu.VMEM((nb, WINDOW, DKV_A), F32)],
        compiler_params=_params(2),
        name="swa_sample",
    )(prm, q, kn, vn, kct, vct)


def _merge_body(alpha, x_ref, h_ref, g1_ref, hm_ref, ha_ref, wgt_ref, bgt_ref,
                wbm_ref, wba_ref, wo_ref, lg_ref, lb_ref, o_ref):
    nsub = 2
    sub = x_ref.shape[0] // nsub
    rows = [pl.ds(r * sub, sub) for r in range(nsub)]
    mod = lambda ref, r: ref[...] if ref.shape[0] == 1 else ref[rows[r], :]
    gate = lambda g, j: g[:, j * D_MODEL:(j + 1) * D_MODEL]

    x = [x_ref[rows[r], :] for r in range(nsub)]
    h = [h_ref[rows[r], :] for r in range(nsub)]
    g = [jax.nn.sigmoid(_dot(h[r], wgt_ref[...]) + bgt_ref[...]) for r in range(nsub)]
    hm = [(hm_ref[rows[r], :] * gate(g[r], 0)).astype(BF16) for r in range(nsub)]
    bm = [_dot(hm[r], wbm_ref[...]) for r in range(nsub)]
    ba = [_dot(ha_ref[rows[r], :].astype(BF16), wba_ref[...]) for r in range(nsub)]
    merged = [(gate(g[r], 1) * bm[r] + gate(g[r], 2) * ba[r]).astype(BF16) for r in range(nsub)]
    mo = [_dot(merged[r], wo_ref[...]) for r in range(nsub)]
    for r in range(nsub):
        o_ref[rows[r], :] = _ln(alpha * x[r] + mod(g1_ref, r) * mo[r]) * lg_ref[...] + lb_ref[...]


def _merge(x, h, g1, hm, ha, w, tm, alpha):
    bsz, seq, _ = x.shape
    mrows = g1.shape[1]
    weights = [w["wgates"], w["bgates"], w["wbm"], w["wba"], w["wo"],
               w["ln1_g"], w["ln1_b"]]
    return pl.pallas_call(
        functools.partial(_merge_body, alpha),
        out_shape=jax.ShapeDtypeStruct((bsz, seq, D_MODEL), F32),
        grid=(bsz, seq // tm),
        in_specs=[_tok_spec(tm, D_MODEL), _tok_spec(tm, D_MODEL), _mod_spec(mrows, tm),
                  _tok_spec(tm, DV_M), _tok_spec(tm, DQ_A)]
                 + [_const_spec(a.shape) for a in weights],
        out_specs=_tok_spec(tm, D_MODEL),
        compiler_params=_params(2),
        name="merge",
    )(x, h, g1, hm, ha, *weights)


def _ffn_body(alpha, tm, stride, halo, x_ref, sh_ref, sc_ref, g2_ref, cb0_ref, wup_ref, bup_ref,
              cw_ref, cbias_ref, wdn_ref, bdn_ref, lg_ref, lb_ref, o_ref, cs_ref, ubuf, act):
    @pl.when(pl.program_id(1) == 0)
    def _():
        cs_ref[...] = cb0_ref[...]

    nsub = 2
    sub = tm // nsub
    rows = [pl.ds(r * sub, sub) for r in range(nsub)]
    mod = lambda ref, r: ref[...] if ref.shape[0] == 1 else ref[rows[r], :]
    x = [x_ref[rows[r], :] for r in range(nsub)]
    h = [(_ln(x[r]) * (1.0 + mod(sc_ref, r)) + mod(sh_ref, r)).astype(BF16) for r in range(nsub)]
    ys = [[None, None] for _ in range(nsub)]
    for half in range(2):
        cols = slice(half * D_FF, (half + 1) * D_FF)
        u = [_dot(h[r], wup_ref[:, cols]) + bup_ref[:, cols] for r in range(nsub)]
        ubuf[pl.ds(0, halo), :] = cs_ref[:, cols]
        for r in range(nsub):
            ubuf[pl.ds(halo + r * sub, sub), :] = u[r]
        cs_ref[:, cols] = ubuf[pl.ds(tm, halo), :]
        for r in range(nsub):
            y = cbias_ref[:, cols] + u[r] * cw_ref[CONV_W - 1:CONV_W, cols]
            for j in range(CONV_W - 1):
                tap = ubuf[pl.ds(halo + r * sub - (CONV_W - 1 - j) * stride, sub), :]
                y = y + tap * cw_ref[j:j + 1, cols]
            ys[r][half] = y
    for r in range(nsub):
        act[rows[r], :] = (jax.nn.gelu(ys[r][0]) * ys[r][1]).astype(BF16)
    f = [_dot(act[rows[r], :], wdn_ref[...]) + bdn_ref[...] for r in range(nsub)]
    for r in range(nsub):
        o_ref[rows[r], :] = _ln(alpha * x[r] + mod(g2_ref, r) * f[r]) * lg_ref[...] + lb_ref[...]


def _ffn(x, sh, sc, g2, cb0, w, tm, stride, alpha):
    bsz, seq, _ = x.shape
    mrows = sh.shape[1]
    halo = cb0.shape[1]
    cs = pl.BlockSpec((None, halo, 2 * D_FF), lambda b, i: (b, 0, 0))
    weights = [w["wup"], w["bup"], w["cw"], w["cbias"], w["wdn"], w["bdn"], w["ln2_g"], w["ln2_b"]]
    return pl.pallas_call(
        functools.partial(_ffn_body, alpha, tm, stride, halo),
        out_shape=[jax.ShapeDtypeStruct((bsz, seq, D_MODEL), F32),
                   jax.ShapeDtypeStruct((bsz, halo, 2 * D_FF), F32)],
        grid=(bsz, seq // tm),
        in_specs=[_tok_spec(tm, D_MODEL), _mod_spec(mrows, tm), _mod_spec(mrows, tm),
                  _mod_spec(mrows, tm), cs] + [_const_spec(a.shape) for a in weights],
        out_specs=[_tok_spec(tm, D_MODEL), cs],
        scratch_shapes=[pltpu.VMEM((halo + tm, D_FF), F32), pltpu.VMEM((tm, D_FF), BF16)],
        compiler_params=_params(2),
        name="ffn",
    )(x, sh, sc, g2, cb0, *weights)


_O_GATE = 2 * DQK_M + DV_M
_O_OG = _O_GATE + 2 * NH_M
_O_QA = _O_OG + DV_M
_O_KA = _O_QA + DQ_A
_O_VA = _O_KA + DKV_A
_O_GM = _O_VA + DKV_A
_PREP_ROWS = 128


def _split_w_in_body(wt_ref, qk_ref, vm_ref, vmt_ref, g_ref, gates_ref, qa_ref, k_ref, v_ref, vt_ref):
    piece = lambda lo, hi: wt_ref[lo:hi, :]
    qk_ref[...] = piece(0, 2 * DQK_M).T.astype(BF16)
    vm = piece(2 * DQK_M, _O_GATE)
    vmt_ref[...] = vm.astype(BF16)
    vm_ref[...] = vm.T.astype(BF16)
    pad = jnp.zeros((LANES - 2 * NH_M, _PREP_ROWS), F32)
    g_ref[...] = jnp.concatenate([piece(_O_GATE, _O_OG), pad], axis=0).T.astype(BF16)
    gates_ref[:, :DV_M] = piece(_O_OG, _O_QA).T.astype(BF16)
    gates_ref[:, DV_M:] = piece(_O_GM, wt_ref.shape[0]).T.astype(BF16)
    qa_ref[...] = piece(_O_QA, _O_KA).T.astype(BF16)
    k_ref[...] = piece(_O_KA, _O_VA).T.astype(BF16)
    v = piece(_O_VA, _O_GM)
    vt_ref[...] = v.astype(BF16)
    v_ref[...] = v.T.astype(BF16)


def _split_w_in(w_in_t):
    d_in = w_in_t.shape[0]
    rows = lambda n: ((D_MODEL, n), pl.BlockSpec((_PREP_ROWS, n), lambda i: (i, 0)))
    cols = lambda n: ((n, D_MODEL), pl.BlockSpec((n, _PREP_ROWS), lambda i: (0, i)))
    outs = dict(wqk=rows(2 * DQK_M), wvm=rows(DV_M), wvm_t=cols(DV_M), wg=rows(LANES),
                wgates=rows(DV_M + d_in - _O_GM), wqa=rows(DQ_A), wk=rows(DKV_A), wv=rows(DKV_A),
                wv_t=cols(DKV_A))
    res = pl.pallas_call(
        _split_w_in_body,
        out_shape=[jax.ShapeDtypeStruct(s, BF16) for s, _ in outs.values()],
        grid=(D_MODEL // _PREP_ROWS,),
        in_specs=[pl.BlockSpec((d_in, _PREP_ROWS), lambda i: (0, i))],
        out_specs=[spec for _, spec in outs.values()],
        compiler_params=_params(1),
        name="split_w_in",
    )(w_in_t)
    return dict(zip(outs.keys(), res))


def _prep_weights(w_in, b_in, mlstm_norm_w, w_branch_m, w_branch_a, w_out, ln1_g, ln1_b,
                  w_up, b_up, conv_w, conv_b, w_down, b_down, ln2_g, ln2_b):
    row = lambda a: a.reshape(1, -1)
    gate_pad = LANES - 2 * NH_M
    b_k, b_v = b_in[_O_KA:_O_VA], b_in[_O_VA:_O_GM]
    return dict(
        **_split_w_in(w_in.T),
        bqk=row(b_in[:2 * DQK_M]),
        bvm_row=row(b_in[2 * DQK_M:_O_GATE]), bvm_col=b_in[2 * DQK_M:_O_GATE].reshape(-1, 1),
        bg=row(jnp.pad(b_in[_O_GATE:_O_OG], (0, gate_pad))),
        bqa=row(b_in[_O_QA:_O_KA]),
        bk=row(b_k),
        bv_row=row(b_v), bv_col=b_v.reshape(-1, 1),
        bgates=row(jnp.concatenate([b_in[_O_OG:_O_QA], b_in[_O_GM:]])),
        norm_w=row(mlstm_norm_w), norm_w_col=mlstm_norm_w.reshape(-1, 1),
        wbm=w_branch_m.astype(BF16), wba=w_branch_a.astype(BF16),
        wo=w_out.astype(BF16), ln1_g=row(ln1_g), ln1_b=row(ln1_b),
        wup=w_up.astype(BF16), bup=row(b_up), cw=conv_w, cbias=row(conv_b),
        wdn=w_down.astype(BF16), bdn=row(b_down), ln2_g=row(ln2_g), ln2_b=row(ln2_b))


def kernel(x_prompt, x_sample, c_prompt, c_sample, state_mlstm_C, state_mlstm_n, state_mlstm_m,
           cache_k_win, cache_v_win, state_ffn_conv, w_ada, b_ada, w_in, b_in, mlstm_norm_w,
           attn_sinks, w_branch_m, w_branch_a, w_out, ln1_g, ln1_b, w_up, b_up, conv_w, conv_b,
           w_down, b_down, ln2_g, ln2_b):
    depth = w_in.shape[0]
    bp, lp, _ = x_prompt.shape
    bs, ls, _ = x_sample.shape
    assert cache_k_win.shape[2] == WINDOW
    alpha = (2 * depth) ** 0.25
    dt = x_prompt.dtype
    slopes = jnp.exp2(-8.0 * jnp.arange(1, NH_A + 1, dtype=F32) / NH_A)
    tm_p = 512
    tm_in = 1024
    ns = bs * ls

    yp = x_prompt
    to_pos = lambda a: a.transpose(1, 0, 2).reshape(1, -1, a.shape[-1])
    ys = to_pos(x_sample)
    new_p, new_s = [], []
    n_c = bp + bs
    c_rows = -(-n_c // SUBLANES) * SUBLANES
    c_all = jnp.concatenate([c_sample, c_prompt, jnp.zeros((c_rows - n_c, D_MODEL), dt)], axis=0)
    for l in range(depth):
        w = _prep_weights(w_in[l], b_in[l], mlstm_norm_w[l], w_branch_m[l], w_branch_a[l], w_out[l],
                          ln1_g[l], ln1_b[l], w_up[l], b_up[l], conv_w[l], conv_b[l], w_down[l],
                          b_down[l], ln2_g[l], ln2_b[l])
        prm = jnp.stack([slopes, attn_sinks[l].astype(F32)])
        mod = _ada(c_all, w_ada[l], b_ada[l])
        mod_s = mod[:bs].reshape(bs, N_MOD, D_MODEL)
        mod_p = mod[bs:bs + bp].reshape(bp, N_MOD, D_MODEL)
        mp_ = [mod_p[:, j:j + 1] for j in range(N_MOD)]
        ms_ = [jnp.tile(mod_s[:, j], (ls, 1))[None] for j in range(N_MOD)]

        hp, qk, v, g, qa, kb, ka, vat = _inproj(yp, mp_[0], mp_[1], w, tm_in, True)
        hm, cp, np_, mp = _mlstm_prompt(qk, v, g, jnp.zeros((bp, NH_M, DHV_M, DHK_M), dt),
                                        jnp.zeros((bp, NH_M, DHK_M), dt), jnp.zeros((bp, 1, LANES), dt),
                                        w["norm_w_col"])
        ha = _swa_prompt(prm, qa, kb, vat)
        x1p = _merge(yp, hp, mp_[2], hm, ha, w, tm_p, alpha)
        halo_p = SUBLANES
        yp, csp = _ffn(x1p, mp_[3], mp_[4], mp_[5], jnp.zeros((bp, halo_p, 2 * D_FF), dt),
                       w, tm_p, 1, alpha)
        p_k = ka[:, lp - WINDOW:].reshape(bp, WINDOW, NKV_A, HD_A)
        p_v = vat[:, :, lp - WINDOW:].reshape(bp, NKV_A, HD_A, WINDOW).transpose(0, 3, 1, 2)
        new_p.append((cp, np_, mp[:, 0, :NH_M], p_k, p_v, csp[:, halo_p - (CONV_W - 1):]))

        hs, qk, v, g, qa, kn, vn = _inproj(ys, ms_[0], ms_[1], w, ns, False)
        per_seq = lambda a: a.reshape(ls, bs, a.shape[-1])
        m0 = jnp.pad(state_mlstm_m[l], ((0, 0), (0, LANES - NH_M)))[:, None, :]
        hm, cs_, ns_, ms = _mlstm_sample(per_seq(qk), per_seq(v), per_seq(g), state_mlstm_C[l],
                                         state_mlstm_n[l], m0, w["norm_w"], SEQS_PER_STEP)
        to_t = lambda a: a.transpose(0, 2, 3, 1).reshape(bs, DKV_A, WINDOW)
        from_t = lambda a: a.reshape(bs, NKV_A, HD_A, WINDOW).transpose(0, 3, 1, 2)
        ha, kct, vct = _swa_sample(prm, per_seq(qa), per_seq(kn), per_seq(vn),
                                   to_t(cache_k_win[l]), to_t(cache_v_win[l]), SEQS_PER_STEP)
        x1s = _merge(ys, hs, ms_[2], hm.reshape(1, ns, DV_M), ha.reshape(1, ns, DQ_A), w, ns, alpha)
        ys, css = _ffn(x1s, ms_[3], ms_[4], ms_[5], to_pos(state_ffn_conv[l]), w, ns, bs, alpha)
        conv_s = css.reshape(CONV_W - 1, bs, 2 * D_FF).transpose(1, 0, 2)
        new_s.append((cs_, ns_, ms[:, 0, :NH_M], from_t(kct), from_t(vct), conv_s))

    p_state = [jnp.stack(a) for a in zip(*new_p)]
    s_state = [jnp.stack(a) for a in zip(*new_s)]
    y_sample = ys.reshape(ls, bs, D_MODEL).transpose(1, 0, 2)
    return (yp, y_sample, *p_state, *s_state)
```

```python
import functools

import jax
import jax.numpy as jnp
from jax import lax
from jax.experimental import pallas as pl
from jax.experimental.pallas import tpu as pltpu

F32 = jnp.float32
BF16 = jnp.bfloat16

D_MODEL = 1024
NH_M, DHK_M, DHV_M = 4, 128, 256
DQK_M, DV_M = NH_M * DHK_M, NH_M * DHV_M
NH_A, NKV_A, HD_A = 16, 4, 64
GROUP_A = NH_A // NKV_A
WINDOW = 128
DQ_A, DKV_A = NH_A * HD_A, NKV_A * HD_A
D_FF = 2816
CONV_W = 3
N_MOD = 6
LN_EPS = 1e-5
CHUNK = 128
NEG = -1e30
LANES = 128
SUBLANES = 8
VMEM_LIMIT = 56 * 1024 * 1024
SEQS_PER_STEP = 8
SWA_SEQS_PER_STEP = 16


def _ln(x):
    mu = jnp.mean(x, axis=-1, keepdims=True)
    xc = x - mu
    var = jnp.mean(xc * xc, axis=-1, keepdims=True)
    return xc * lax.rsqrt(var + LN_EPS)


def _dot(a, b):
    return jnp.dot(a, b, preferred_element_type=F32)


def _dot_nt(a, b):
    return lax.dot_general(a, b, (((1,), (1,)), ((), ())), preferred_element_type=F32)


def _dot_tn(a, b):
    return lax.dot_general(a, b, (((0,), (0,)), ((), ())), preferred_element_type=F32)


def _const_spec(shape):
    nd = len(shape)
    return pl.BlockSpec(shape, lambda *_: (0,) * nd, pipeline_mode=pl.Buffered(1))


def _params(n_grid):
    return pltpu.CompilerParams(dimension_semantics=("arbitrary",) * n_grid,
                                vmem_limit_bytes=VMEM_LIMIT)


def _first_step():
    return (pl.program_id(0) == 0) & (pl.program_id(1) == 0)


def _ada_body(c_ref, w_ref, b_ref, o_ref):
    c = c_ref[...]
    s = (c * jax.nn.sigmoid(c)).astype(BF16)
    o_ref[...] = _dot(s, w_ref[...].astype(BF16)) + b_ref[...]


def _ada(c, w_ada, b_ada):
    rows = c.shape[0]
    n_out = w_ada.shape[1]
    bn = 3072
    assert n_out % bn == 0
    return pl.pallas_call(
        _ada_body,
        out_shape=jax.ShapeDtypeStruct((rows, n_out), F32),
        grid=(n_out // bn,),
        in_specs=[pl.BlockSpec((rows, D_MODEL), lambda j: (0, 0)),
                  pl.BlockSpec((D_MODEL, bn), lambda j: (0, j)),
                  pl.BlockSpec((1, bn), lambda j: (0, j))],
        out_specs=pl.BlockSpec((rows, bn), lambda j: (0, j)),
        compiler_params=_params(1),
        name="ada",
    )(c, w_ada, b_ada.reshape(1, n_out))


def _inproj_body(prompt, x_ref, sh_ref, sc_ref, wqk_ref, bqk_ref, wvm_ref, bvm_ref, wg_ref, bg_ref,
                 wqa_ref, bqa_ref, wk_ref, bk_ref, wv_ref, bv_ref, h_ref, qk_ref, v_ref, g_ref, qa_ref, *rest):
    h = (_ln(x_ref[...]) * (1.0 + sc_ref[...]) + sh_ref[...]).astype(BF16)
    h_ref[...] = h

    def proj(w_ref, b_ref, lo, n):
        return _dot(h, w_ref[:, lo:lo + n]) + b_ref[:, lo:lo + n]

    qk_ref[:, :DQK_M] = (proj(wqk_ref, bqk_ref, 0, DQK_M) * DHK_M ** -0.5).astype(qk_ref.dtype)
    qk_ref[:, DQK_M:] = proj(wqk_ref, bqk_ref, DQK_M, DQK_M).astype(qk_ref.dtype)
    if prompt:
        v_ref[...] = (_dot_nt(wvm_ref[...], h) + bvm_ref[...]).astype(v_ref.dtype)
    else:
        v_ref[...] = proj(wvm_ref, bvm_ref, 0, DV_M).astype(v_ref.dtype)
    g_ref[...] = proj(wg_ref, bg_ref, 0, LANES)
    qa_ref[...] = (proj(wqa_ref, bqa_ref, 0, DQ_A) * HD_A ** -0.5).astype(qa_ref.dtype)
    ka = proj(wk_ref, bk_ref, 0, DKV_A)
    if prompt:
        kb_ref, ka_ref, vat_ref = rest
        kb_ref[...] = ka.astype(BF16)
        ka_ref[...] = ka
        vat_ref[...] = _dot_nt(wv_ref[...], h) + bv_ref[...]
    else:
        ka_ref, va_ref = rest
        ka_ref[...] = ka
        va_ref[...] = _dot(h, wv_ref[...]) + bv_ref[...]


def _tok_spec(tm, n):
    return pl.BlockSpec((None, tm, n), lambda b, i: (b, i, 0))


def _mod_spec(rows, tm):
    if rows == 1:
        return pl.BlockSpec((None, 1, D_MODEL), lambda b, i: (b, 0, 0))
    return pl.BlockSpec((None, tm, D_MODEL), lambda b, i: (b, i, 0))


def _inproj(x, sh, sc, w, tm, prompt):
    bsz, seq, _ = x.shape
    mrows = sh.shape[1]
    act = BF16 if prompt else F32
    t_spec = lambda n: pl.BlockSpec((None, n, tm), lambda b, i: (b, 0, i))
    outs = [((bsz, seq, D_MODEL), BF16, _tok_spec(tm, D_MODEL)),
            ((bsz, seq, 2 * DQK_M), act, _tok_spec(tm, 2 * DQK_M)),
            ((bsz, DV_M, seq), act, t_spec(DV_M)) if prompt else ((bsz, seq, DV_M), act, _tok_spec(tm, DV_M)),
            ((bsz, seq, LANES), F32, _tok_spec(tm, LANES)),
            ((bsz, seq, DQ_A), act, _tok_spec(tm, DQ_A))]
    if prompt:
        wvm, bvm, wv, bv = w["wvm_t"], w["bvm_col"], w["wv_t"], w["bv_col"]
        outs += [((bsz, seq, DKV_A), BF16, _tok_spec(tm, DKV_A)),
                 ((bsz, seq, DKV_A), F32, _tok_spec(tm, DKV_A)),
                 ((bsz, DKV_A, seq), F32, t_spec(DKV_A))]
    else:
        wvm, bvm, wv, bv = w["wvm"], w["bvm_row"], w["wv"], w["bv_row"]
        outs += [((bsz, seq, DKV_A), F32, _tok_spec(tm, DKV_A)),
                 ((bsz, seq, DKV_A), F32, _tok_spec(tm, DKV_A))]
    weights = [w["wqk"], w["bqk"], wvm, bvm, w["wg"], w["bg"], w["wqa"], w["bqa"], w["wk"], w["bk"],
               wv, bv]
    return pl.pallas_call(
        functools.partial(_inproj_body, prompt),
        out_shape=[jax.ShapeDtypeStruct(s, dt) for s, dt, _ in outs],
        grid=(bsz, seq // tm),
        in_specs=[_tok_spec(tm, D_MODEL), _mod_spec(mrows, tm), _mod_spec(mrows, tm)]
                 + [_const_spec(a.shape) for a in weights],
        out_specs=[spec for _, _, spec in outs],
        compiler_params=_params(2),
        name="inproj",
    )(x, sh, sc, *weights)


def _scan_rows(x, op, rows):
    row = lax.broadcasted_iota(jnp.int32, x.shape, 0)
    d = 1
    while d < rows:
        shifted = pltpu.roll(x, d, axis=0)
        x = jnp.where(row >= d, op(x, shifted), x)
        d *= 2
    return x


def _mlstm_sample_body(t_in, nb, qk_ref, v_ref, g_ref, c0_ref, n0_ref, m0_ref, nw_ref,
                       h_ref, c_ref, n_ref, m_ref, *pads):
    t = SUBLANES
    c_ref[...] = c0_ref[...]
    n_ref[...] = n0_ref[...]
    m_ref[...] = m0_ref[...]

    @pl.when(_first_step())
    def _():
        for p in pads:
            p[...] = jnp.zeros(p.shape, p.dtype)
    for src, dst in zip((qk_ref, v_ref, g_ref), pads):
        for s in range(nb):
            dst[s, pl.ds(0, t_in), :] = src[:, s, :]
    qk_src, v_src, g_src = pads

    row = lax.broadcasted_iota(jnp.int32, (t, LANES), 0)
    lane = lax.broadcasted_iota(jnp.int32, (t, LANES), 1)
    lane1 = lax.broadcasted_iota(jnp.int32, (1, LANES), 1)
    r2 = lax.broadcasted_iota(jnp.int32, (t, t), 0)
    c2 = lax.broadcasted_iota(jnp.int32, (t, t), 1)
    causal = c2 <= r2
    pad_gate = jnp.where(lane < NH_M, NEG, -NEG)

    seqs = []
    for s in range(nb):
        ga = _gate_algebra(jnp.where(row < t_in, g_src[s], pad_gate), m_ref[s], t)
        ga["gt"] = ga["gg"].T
        m_ref[s] = jnp.where(lane1 < NH_M, ga["m_new"], 0.0)
        seqs.append(ga)

    units = [(s, hd) for s in range(nb) for hd in range(NH_M)]
    nu = len(units)
    col = lambda name, s, hd: seqs[s][name][:, hd:hd + 1]
    q = [qk_src[s, :, hd * DHK_M:(hd + 1) * DHK_M].astype(BF16) for s, hd in units]
    k = [qk_src[s, :, DQK_M + hd * DHK_M:DQK_M + (hd + 1) * DHK_M].astype(BF16) for s, hd in units]
    v = [v_src[s, :, hd * DHV_M:(hd + 1) * DHV_M].astype(BF16) for s, hd in units]
    c_old = [c_ref[s, hd] for s, hd in units]
    n_old = [n_ref[s, hd:hd + 1, :] for s, hd in units]

    sqk = [_dot_nt(q[u], k[u]) for u in range(nu)]
    dexp = [jnp.exp(jnp.where(causal, col("eb", s, hd) + seqs[s]["gt"][hd:hd + 1, :], NEG))
            for s, hd in units]
    smat = [sqk[u] * dexp[u] for u in range(nu)]
    intra = [_dot(smat[u].astype(BF16), v[u]) for u in range(nu)]
    inter = [_dot_nt(q[u], c_old[u].astype(BF16)) for u in range(nu)]
    qn = [jnp.sum(q[u].astype(F32) * n_old[u].astype(BF16).astype(F32), axis=-1, keepdims=True)
          for u in range(nu)]
    den = [jnp.sum(smat[u], axis=-1, keepdims=True) + col("a_in", s, hd) * qn[u]
           for u, (s, hd) in enumerate(units)]
    hh = [(intra[u] + col("a_in", s, hd) * inter[u])
          / jnp.maximum(jnp.abs(den[u]), col("lowb", s, hd)) for u, (s, hd) in enumerate(units)]
    hn = [_ln(hh[u]) * nw_ref[:, hd * DHV_M:(hd + 1) * DHV_M] for u, (s, hd) in enumerate(units)]
    for u, (s, hd) in enumerate(units):
        h_ref[:, s, hd * DHV_M:(hd + 1) * DHV_M] = hn[u][:t_in]

    kw = [k[u].astype(F32) * col("ws", s, hd) for u, (s, hd) in enumerate(units)]
    upd = [_dot_tn(v[u], kw[u].astype(BF16)) for u in range(nu)]
    for u, (s, hd) in enumerate(units):
        dec = seqs[s]["decay"][:, hd:hd + 1]
        c_ref[s, hd] = dec * c_old[u] + upd[u]
        n_ref[s, hd:hd + 1, :] = dec * n_old[u] + jnp.sum(kw[u], axis=0, keepdims=True)


def _mlstm_sample(qk, v, g, c0, n0, m0, norm_w, nb):
    t_in, nseq, _ = qk.shape
    assert t_in <= SUBLANES
    blk = lambda n: pl.BlockSpec((t_in, nb, n), lambda o, c: (0, o, 0))
    st_c = pl.BlockSpec((nb, NH_M, DHV_M, DHK_M), lambda o, c: (o, 0, 0, 0))
    st_n = pl.BlockSpec((nb, NH_M, DHK_M), lambda o, c: (o, 0, 0))
    st_m = pl.BlockSpec((nb, 1, LANES), lambda o, c: (o, 0, 0))
    return pl.pallas_call(
        functools.partial(_mlstm_sample_body, t_in, nb),
        out_shape=[jax.ShapeDtypeStruct((t_in, nseq, DV_M), F32),
                   jax.ShapeDtypeStruct((nseq, NH_M, DHV_M, DHK_M), F32),
                   jax.ShapeDtypeStruct((nseq, NH_M, DHK_M), F32),
                   jax.ShapeDtypeStruct((nseq, 1, LANES), F32)],
        grid=(nseq // nb, 1),
        in_specs=[blk(2 * DQK_M), blk(DV_M), blk(LANES), st_c, st_n, st_m,
                  pl.BlockSpec((1, DV_M), lambda o, c: (0, 0))],
        out_specs=[blk(DV_M), st_c, st_n, st_m],
        scratch_shapes=[pltpu.VMEM((nb, SUBLANES, 2 * DQK_M), F32), pltpu.VMEM((nb, SUBLANES, DV_M), F32),
                        pltpu.VMEM((nb, SUBLANES, LANES), F32)],
        compiler_params=_params(2),
        name="mlstm_sample",
    )(qk, v, g, c0, n0, m0, norm_w)


def _gate_algebra(g, m_prev, t):
    b = pltpu.roll(_scan_rows(jax.nn.log_sigmoid(g), jnp.add, t), LANES - NH_M, axis=1)
    gg = g - b
    gmx = _scan_rows(gg, jnp.maximum, t)
    a = b + m_prev
    mt = jnp.maximum(a, b + gmx)
    b_last, gmx_last = b[t - 1:t, :], gmx[t - 1:t, :]
    m_new = jnp.maximum(b_last + m_prev, b_last + gmx_last)
    return dict(gg=gg, eb=b - mt, a_in=jnp.exp(a - mt), lowb=jnp.exp(-mt),
                ws=jnp.exp(b_last + gg - m_new), decay=jnp.exp(b_last + m_prev - m_new), m_new=m_new)


MLSTM_CHUNKS_PER_STEP = 4


def _mlstm_prompt_body(nb, qk_ref, vt_ref, g_ref, c0_ref, n0_ref, m0_ref, nwc_ref,
                       h_ref, c_ref, n_ref, m_ref, nwb):
    @pl.when(pl.program_id(1) == 0)
    def _():
        c_ref[...] = c0_ref[...]
        n_ref[...] = n0_ref[...]
        m_ref[...] = m0_ref[...]

    @pl.when(_first_step())
    def _():
        for hd in range(NH_M):
            nwb[hd] = jnp.broadcast_to(nwc_ref[hd * DHV_M:(hd + 1) * DHV_M, :], (DHV_M, CHUNK))

    for ci in range(MLSTM_CHUNKS_PER_STEP):
        _mlstm_prompt_chunk(nb, slice(ci * CHUNK, (ci + 1) * CHUNK), qk_ref, vt_ref, g_ref, nwb,
                            h_ref, c_ref, n_ref, m_ref)


def _mlstm_prompt_chunk(nb, rows, qk_ref, vt_ref, g_ref, nwb, h_ref, c_ref, n_ref, m_ref):
    t = CHUNK
    lane1 = lax.broadcasted_iota(jnp.int32, (1, LANES), 1)
    r2 = lax.broadcasted_iota(jnp.int32, (t, t), 0)
    c2 = lax.broadcasted_iota(jnp.int32, (t, t), 1)
    causal = r2 <= c2

    seqs = []
    for s in range(nb):
        ga = _gate_algebra(g_ref[s, rows, :], m_ref[s], t)
        m_ref[s] = jnp.where(lane1 < NH_M, ga["m_new"], 0.0)
        for name in ("eb", "a_in", "lowb"):
            ga[name + "_t"] = ga[name].T
        seqs.append(ga)

    units = [(s, hd) for s in range(nb) for hd in range(NH_M)]
    nu = len(units)
    rowv = lambda name, s, hd: seqs[s][name + "_t"][hd:hd + 1, :]
    q = [qk_ref[s, rows, hd * DHK_M:(hd + 1) * DHK_M] for s, hd in units]
    k = [qk_ref[s, rows, DQK_M + hd * DHK_M:DQK_M + (hd + 1) * DHK_M] for s, hd in units]
    vt = [vt_ref[s, hd * DHV_M:(hd + 1) * DHV_M, rows] for s, hd in units]
    c_old = [c_ref[s, hd] for s, hd in units]
    n_old = [n_ref[s, hd:hd + 1, :] for s, hd in units]

    skq = [_dot_nt(k[u], q[u]) for u in range(nu)]
    dexp = [jnp.exp(jnp.where(causal, seqs[s]["gg"][:, hd:hd + 1] + rowv("eb", s, hd), NEG))
            for s, hd in units]
    smat = [skq[u] * dexp[u] for u in range(nu)]
    intra = [_dot(vt[u], smat[u].astype(BF16)) for u in range(nu)]
    inter = [_dot_nt(c_old[u].astype(BF16), q[u]) for u in range(nu)]
    qn = [_dot_nt(jnp.broadcast_to(n_old[u], (SUBLANES, DHK_M)).astype(BF16), q[u])[0:1, :]
          for u in range(nu)]
    den = [jnp.sum(smat[u], axis=0, keepdims=True) + rowv("a_in", s, hd) * qn[u]
           for u, (s, hd) in enumerate(units)]
    inv = [1.0 / jnp.maximum(jnp.abs(den[u]), rowv("lowb", s, hd)) for u, (s, hd) in enumerate(units)]
    hh = [(intra[u] + rowv("a_in", s, hd) * inter[u]) * inv[u] for u, (s, hd) in enumerate(units)]
    mu = [jnp.mean(hh[u], axis=0, keepdims=True) for u in range(nu)]
    xc = [hh[u] - mu[u] for u in range(nu)]
    var = [jnp.mean(xc[u] * xc[u], axis=0, keepdims=True) for u in range(nu)]
    hn = [xc[u] * lax.rsqrt(var[u] + LN_EPS) * nwb[hd] for u, (s, hd) in enumerate(units)]
    for u, (s, hd) in enumerate(units):
        h_ref[s, rows, hd * DHV_M:(hd + 1) * DHV_M] = hn[u].T

    kw = [(k[u].astype(F32) * seqs[s]["ws"][:, hd:hd + 1]).astype(BF16) for u, (s, hd) in enumerate(units)]
    upd = [_dot(vt[u], kw[u]) for u in range(nu)]
    nupd = [_dot(jnp.ones((SUBLANES, t), BF16), kw[u])[0:1, :] for u in range(nu)]
    for u, (s, hd) in enumerate(units):
        dec = seqs[s]["decay"][:, hd:hd + 1]
        c_ref[s, hd] = dec * c_old[u] + upd[u]
        n_ref[s, hd:hd + 1, :] = dec * n_old[u] + nupd[u]


def _mlstm_prompt(qk, vt, g, c0, n0, m0, norm_w_col):
    nb, seq, _ = qk.shape
    ts = MLSTM_CHUNKS_PER_STEP * CHUNK
    tok = lambda n: pl.BlockSpec((nb, ts, n), lambda o, c: (0, c, 0))
    st_c = pl.BlockSpec((nb, NH_M, DHV_M, DHK_M), lambda o, c: (0, 0, 0, 0))
    st_n = pl.BlockSpec((nb, NH_M, DHK_M), lambda o, c: (0, 0, 0))
    st_m = pl.BlockSpec((nb, 1, LANES), lambda o, c: (0, 0, 0))
    return pl.pallas_call(
        functools.partial(_mlstm_prompt_body, nb),
        out_shape=[jax.ShapeDtypeStruct((nb, seq, DV_M), F32),
                   jax.ShapeDtypeStruct((nb, NH_M, DHV_M, DHK_M), F32),
                   jax.ShapeDtypeStruct((nb, NH_M, DHK_M), F32),
                   jax.ShapeDtypeStruct((nb, 1, LANES), F32)],
        grid=(1, seq // ts),
        in_specs=[tok(2 * DQK_M), pl.BlockSpec((nb, DV_M, ts), lambda o, c: (0, 0, c)), tok(LANES),
                  st_c, st_n, st_m, pl.BlockSpec((DV_M, 1), lambda o, c: (0, 0))],
        out_specs=[tok(DV_M), st_c, st_n, st_m],
        scratch_shapes=[pltpu.VMEM((NH_M, DHV_M, CHUNK), F32)],
        compiler_params=_params(2),
        name="mlstm_prompt",
    )(qk, vt, g, c0, n0, m0, norm_w_col)


SWA_BLOCKS_PER_STEP = 16


def _swa_prompt_body(prm_ref, q_ref, kc_ref, kp_ref, vc_ref, vp_ref, o_ref, tbl):
    row = lax.broadcasted_iota(jnp.int32, (WINDOW, WINDOW), 0)
    col = lax.broadcasted_iota(jnp.int32, (WINDOW, WINDOW), 1)
    tri = row <= col

    @pl.when(_first_step())
    def _():
        dist = jnp.where(tri, col - row, col - row + WINDOW).astype(F32)
        for hd in range(NH_A):
            bias = prm_ref[0, hd] * dist
            tbl[0, hd] = bias
            tbl[1, hd] = bias + jnp.where(tri, 0.0, -NEG)

    for b in range(SWA_BLOCKS_PER_STEP):
        rows = pl.ds(b * WINDOW, WINDOW)
        which = jnp.where(pl.program_id(1) == 0, 1, 0) if b == 0 else 0
        k_prev = kp_ref if b == 0 else kc_ref.at[pl.ds((b - 1) * WINDOW, WINDOW), :]
        v_prev = vp_ref if b == 0 else vc_ref.at[:, pl.ds((b - 1) * WINDOW, WINDOW)]
        _swa_prompt_block(which, prm_ref, q_ref.at[rows, :], kc_ref.at[rows, :], k_prev,
                          vc_ref.at[:, rows], v_prev, o_ref.at[rows, :], tbl)


def _swa_prompt_block(which, prm_ref, q_ref, kc_ref, kp_ref, vc_ref, vp_ref, o_ref, tbl):
    row = lax.broadcasted_iota(jnp.int32, (WINDOW, WINDOW), 0)
    col = lax.broadcasted_iota(jnp.int32, (WINDOW, WINDOW), 1)
    tri = row <= col
    lo = col < HD_A
    zb = jnp.zeros((WINDOW, LANES), BF16)

    def placed(k_ref, kv):
        c, par = kv // 2, kv % 2
        own = k_ref[:, c * LANES:(c + 1) * LANES]
        swp = pltpu.roll(own, HD_A, axis=1)
        if par == 0:
            return jnp.where(lo, own, zb), jnp.where(lo, zb, swp)
        return jnp.where(lo, swp, zb), jnp.where(lo, zb, own)

    st = []
    for kv in range(NKV_A):
        lhs = jnp.concatenate([*placed(kc_ref, kv), *placed(kp_ref, kv)], axis=0)
        qg = jnp.concatenate([q_ref[:, (2 * kv) * LANES:(2 * kv + 1) * LANES],
                              q_ref[:, (2 * kv + 1) * LANES:(2 * kv + 2) * LANES]], axis=0)
        st.append(_dot_nt(lhs, qg))
    tiles = []
    for hd in range(NH_A):
        kv, a, par = hd // GROUP_A, (hd % GROUP_A) // 2, hd % 2
        cols = slice(a * WINDOW, (a + 1) * WINDOW)
        tiles.append(jnp.where(tri, st[kv][par * WINDOW:(par + 1) * WINDOW, cols],
                               st[kv][(2 + par) * WINDOW:(3 + par) * WINDOW, cols]))
    sc = jnp.concatenate(tiles, axis=0).reshape(NH_A, WINDOW, WINDOW) - tbl[which]
    sink = jnp.concatenate([jnp.full((1, 1, WINDOW), prm_ref[1, hd], F32) for hd in range(NH_A)], axis=0)
    mx = jnp.maximum(jnp.max(sc, axis=1, keepdims=True), sink)
    p = jnp.exp(sc - mx)
    den = jnp.sum(p, axis=1, keepdims=True) + jnp.exp(sink - mx)
    pn = p * (1.0 / den)

    zero = jnp.zeros((WINDOW, WINDOW), F32)
    z64 = jnp.zeros((HD_A, WINDOW), BF16)
    outs = []
    for kv in range(NKV_A):
        cols = []
        for a in range(2):
            pe, po = pn[kv * GROUP_A + 2 * a], pn[kv * GROUP_A + 2 * a + 1]
            cols.append(jnp.concatenate([jnp.where(tri, pe, zero), jnp.where(tri, po, zero),
                                         jnp.where(tri, zero, pe), jnp.where(tri, zero, po)],
                                        axis=0).astype(BF16))
        pt = jnp.concatenate(cols, axis=1)
        vc = vc_ref[kv * HD_A:(kv + 1) * HD_A, :].astype(BF16)
        vp = vp_ref[kv * HD_A:(kv + 1) * HD_A, :].astype(BF16)
        vt = jnp.concatenate([jnp.concatenate([vc, z64], axis=0), jnp.concatenate([z64, vc], axis=0),
                              jnp.concatenate([vp, z64], axis=0), jnp.concatenate([z64, vp], axis=0)],
                             axis=1)
        ot = _dot(vt, pt)
        outs += [ot[:, :WINDOW].T, ot[:, WINDOW:].T]
    o_ref[...] = jnp.concatenate(outs, axis=1).astype(o_ref.dtype)


def _swa_prompt(prm, q, kb, vat):
    bsz, seq, _ = q.shape
    tq = SWA_BLOCKS_PER_STEP * WINDOW
    prev = lambda i: jnp.maximum(i * SWA_BLOCKS_PER_STEP - 1, 0)
    return pl.pallas_call(
        _swa_prompt_body,
        out_shape=jax.ShapeDtypeStruct((bsz, seq, DQ_A), BF16),
        grid=(bsz, seq // tq),
        in_specs=[pl.BlockSpec(memory_space=pltpu.SMEM),
                  pl.BlockSpec((None, tq, DQ_A), lambda b, i: (b, i, 0)),
                  pl.BlockSpec((None, tq, DKV_A), lambda b, i: (b, i, 0)),
                  pl.BlockSpec((None, WINDOW, DKV_A), lambda b, i: (b, prev(i), 0)),
                  pl.BlockSpec((None, DKV_A, tq), lambda b, i: (b, 0, i)),
                  pl.BlockSpec((None, DKV_A, WINDOW), lambda b, i: (b, 0, prev(i)))],
        out_specs=pl.BlockSpec((None, tq, DQ_A), lambda b, i: (b, i, 0)),
        scratch_shapes=[pltpu.VMEM((2, NH_A, WINDOW, WINDOW), F32)],
        compiler_params=_params(2),
        name="swa_prompt",
    )(prm, q, kb, kb, vat, vat)


def _half_mask(shape, half):
    lane = lax.broadcasted_iota(jnp.int32, shape, 1)
    return lane < HD_A if half == 0 else lane >= HD_A


def _swa_sample_body(t_in, nb, prm_ref, q_ref, kn_ref, vn_ref, kct_ref, vct_ref,
                     o_ref, kco_ref, vco_ref, tbl, q8, kn_pad, vn_pad):
    tq = SUBLANES
    rows = NH_A * tq
    row = lax.broadcasted_iota(jnp.int32, (rows, WINDOW), 0)
    col = lax.broadcasted_iota(jnp.int32, (rows, WINDOW), 1)
    tri = col <= (row & (tq - 1))

    @pl.when(_first_step())
    def _():
        r8 = lax.broadcasted_iota(jnp.int32, (tq, WINDOW), 0)
        c8 = lax.broadcasted_iota(jnp.int32, (tq, WINDOW), 1)
        dist = jnp.where(c8 <= r8, r8 - c8, r8 - c8 + WINDOW).astype(F32)
        for hd in range(NH_A):
            tbl[pl.ds(hd * tq, tq), :] = prm_ref[0, hd] * dist
        for p in (q8, kn_pad, vn_pad):
            p[...] = jnp.zeros(p.shape, p.dtype)

    for s in range(nb):
        q8[s, pl.ds(0, t_in), :] = q_ref[:, s, :]
        kn_pad[s, pl.ds(0, t_in), :] = kn_ref[:, s, :]
        vn_pad[s, pl.ds(0, t_in), :] = vn_ref[:, s, :]
    sink_col = jnp.concatenate([jnp.full((tq, 1), prm_ref[1, hd], F32) for hd in range(NH_A)], axis=0)
    z8 = jnp.zeros((tq, LANES), F32)
    n_chunk = NKV_A // 2
    heads_per_chunk = NH_A // n_chunk

    def place(piece, src_half, dst_half):
        if src_half != dst_half:
            piece = pltpu.roll(piece, HD_A, axis=1)
        return jnp.where(_half_mask(piece.shape, dst_half), piece, z8)

    qexp = []
    for s in range(nb):
        per_c = []
        for c in range(n_chunk):
            pieces = []
            for hl in range(heads_per_chunk):
                hd = c * heads_per_chunk + hl
                pieces.append(place(q8[s, :, (hd // 2) * LANES:(hd // 2 + 1) * LANES],
                                    hd % 2, hl // GROUP_A))
            per_c.append(jnp.concatenate(pieces, axis=0).astype(BF16))
        qexp.append(per_c)
    csl = lambda c: slice(c * LANES, (c + 1) * LANES)
    s_prev = [[_dot(qexp[s][c], kct_ref[s, csl(c), :].astype(BF16)) for c in range(n_chunk)]
              for s in range(nb)]
    s_cur = [[_dot_nt(qexp[s][c], kn_pad[s, :, csl(c)].astype(BF16)) for c in range(n_chunk)]
             for s in range(nb)]
    sc = [jnp.where(tri, jnp.concatenate(s_cur[s], axis=0), jnp.concatenate(s_prev[s], axis=0)) - tbl[...]
          for s in range(nb)]
    mx = [jnp.maximum(jnp.max(sc[s], axis=-1, keepdims=True), sink_col) for s in range(nb)]
    p = [jnp.exp(sc[s] - mx[s]) for s in range(nb)]
    den = [jnp.sum(p[s], axis=-1, keepdims=True) + jnp.exp(sink_col - mx[s]) for s in range(nb)]
    pn = [p[s] * (1.0 / den[s]) for s in range(nb)]
    zero = jnp.zeros((rows, WINDOW), F32)
    pc = [jnp.where(tri, pn[s], zero).astype(BF16) for s in range(nb)]
    pp = [jnp.where(tri, zero, pn[s]).astype(BF16) for s in range(nb)]
    half_rows = heads_per_chunk * tq
    oc = [[_dot_nt(pp[s][c * half_rows:(c + 1) * half_rows], vct_ref[s, csl(c), :].astype(BF16))
           + _dot(pc[s][c * half_rows:(c + 1) * half_rows], vn_pad[s, :, csl(c)].astype(BF16))
           for c in range(n_chunk)] for s in range(nb)]
    for s in range(nb):
        chunks = []
        for pch in range(NH_A // 2):
            acc = None
            for hd in (2 * pch, 2 * pch + 1):
                c, hl = hd // heads_per_chunk, hd % heads_per_chunk
                piece = place(oc[s][c][hl * tq:(hl + 1) * tq, :], hl // GROUP_A, hd % 2)
                acc = piece if acc is None else acc + piece
            chunks.append(acc)
        o_ref[:, s, :] = jnp.concatenate(chunks, axis=1)[:t_in]

    lane = lax.broadcasted_iota(jnp.int32, (DKV_A, WINDOW), 1)
    for new_pad, old_ref, out_ref in ((kn_pad, kct_ref, kco_ref), (vn_pad, vct_ref, vco_ref)):
        for s in range(nb):
            merged = jnp.where(lane < t_in, new_pad[s].T, old_ref[s])
            out_ref[s] = pltpu.roll(merged, WINDOW - t_in, axis=1)


def _swa_sample(prm, q, kn, vn, kct, vct, nb):
    t_in, nseq, _ = q.shape
    assert t_in <= SUBLANES
    cur = lambda n: pl.BlockSpec((t_in, nb, n), lambda o, i: (0, o, 0))
    win = pl.BlockSpec((nb, DKV_A, WINDOW), lambda o, i: (o, 0, 0))
    return pl.pallas_call(
        functools.partial(_swa_sample_body, t_in, nb),
        out_shape=[jax.ShapeDtypeStruct((t_in, nseq, DQ_A), F32),
                   jax.ShapeDtypeStruct((nseq, DKV_A, WINDOW), F32),
                   jax.ShapeDtypeStruct((nseq, DKV_A, WINDOW), F32)],
        grid=(nseq // nb, 1),
        in_specs=[pl.BlockSpec(memory_space=pltpu.SMEM), cur(DQ_A), cur(DKV_A), cur(DKV_A), win, win],
        out_specs=[cur(DQ_A), win, win],
        scratch_shapes=[pltpu.VMEM((NH_A * SUBLANES, WINDOW), F32),
                        pltpu.VMEM((nb, SUBLANES, DQ_A), F32),
                        pltpu.VMEM((nb, WINDOW, DKV_A), F32),
                        pltpu.VMEM((nb, WINDOW, DKV_A), F32)],
        compiler_params=_params(2),
        name="swa_sample",
    )(prm, q, kn, vn, kct, vct)


def _merge_body(alpha, x_ref, h_ref, g1_ref, hm_ref, ha_ref, wgt_ref, bgt_ref,
                wbm_ref, wba_ref, wo_ref, lg_ref, lb_ref, o_ref):
    nsub = 2
    sub = x_ref.shape[0] // nsub
    rows = [pl.ds(r * sub, sub) for r in range(nsub)]
    mod = lambda ref, r: ref[...] if ref.shape[0] == 1 else ref[rows[r], :]
    gate = lambda g, j: g[:, j * D_MODEL:(j + 1) * D_MODEL]

    x = [x_ref[rows[r], :] for r in range(nsub)]
    h = [h_ref[rows[r], :] for r in range(nsub)]
    g = [jax.nn.sigmoid(_dot(h[r], wgt_ref[...]) + bgt_ref[...]) for r in range(nsub)]
    hm = [(hm_ref[rows[r], :] * gate(g[r], 0)).astype(BF16) for r in range(nsub)]
    bm = [_dot(hm[r], wbm_ref[...]) for r in range(nsub)]
    ba = [_dot(ha_ref[rows[r], :].astype(BF16), wba_ref[...]) for r in range(nsub)]
    merged = [(gate(g[r], 1) * bm[r] + gate(g[r], 2) * ba[r]).astype(BF16) for r in range(nsub)]
    mo = [_dot(merged[r], wo_ref[...]) for r in range(nsub)]
    for r in range(nsub):
        o_ref[rows[r], :] = _ln(alpha * x[r] + mod(g1_ref, r) * mo[r]) * lg_ref[...] + lb_ref[...]


def _merge(x, h, g1, hm, ha, w, tm, alpha):
    bsz, seq, _ = x.shape
    mrows = g1.shape[1]
    weights = [w["wgates"], w["bgates"], w["wbm"], w["wba"], w["wo"],
               w["ln1_g"], w["ln1_b"]]
    return pl.pallas_call(
        functools.partial(_merge_body, alpha),
        out_shape=jax.ShapeDtypeStruct((bsz, seq, D_MODEL), F32),
        grid=(bsz, seq // tm),
        in_specs=[_tok_spec(tm, D_MODEL), _tok_spec(tm, D_MODEL), _mod_spec(mrows, tm),
                  _tok_spec(tm, DV_M), _tok_spec(tm, DQ_A)]
                 + [_const_spec(a.shape) for a in weights],
        out_specs=_tok_spec(tm, D_MODEL),
        compiler_params=_params(2),
        name="merge",
    )(x, h, g1, hm, ha, *weights)


def _ffn_body(alpha, tm, stride, halo, x_ref, sh_ref, sc_ref, g2_ref, cb0_ref, wup_ref, bup_ref,
              cw_ref, cbias_ref, wdn_ref, bdn_ref, lg_ref, lb_ref, o_ref, cs_ref, ubuf, act):
    @pl.when(pl.program_id(1) == 0)
    def _():
        cs_ref[...] = cb0_ref[...]

    nsub = 2
    sub = tm // nsub
    rows = [pl.ds(r * sub, sub) for r in range(nsub)]
    mod = lambda ref, r: ref[...] if ref.shape[0] == 1 else ref[rows[r], :]
    x = [x_ref[rows[r], :] for r in range(nsub)]
    h = [(_ln(x[r]) * (1.0 + mod(sc_ref, r)) + mod(sh_ref, r)).astype(BF16) for r in range(nsub)]
    ys = [[None, None] for _ in range(nsub)]
    for half in range(2):
        cols = slice(half * D_FF, (half + 1) * D_FF)
        u = [_dot(h[r], wup_ref[:, cols]) + bup_ref[:, cols] for r in range(nsub)]
        ubuf[pl.ds(0, halo), :] = cs_ref[:, cols]
        for r in range(nsub):
            ubuf[pl.ds(halo + r * sub, sub), :] = u[r]
        cs_ref[:, cols] = ubuf[pl.ds(tm, halo), :]
        for r in range(nsub):
            y = cbias_ref[:, cols] + u[r] * cw_ref[CONV_W - 1:CONV_W, cols]
            for j in range(CONV_W - 1):
                tap = ubuf[pl.ds(halo + r * sub - (CONV_W - 1 - j) * stride, sub), :]
                y = y + tap * cw_ref[j:j + 1, cols]
            ys[r][half] = y
    for r in range(nsub):
        act[rows[r], :] = (jax.nn.gelu(ys[r][0]) * ys[r][1]).astype(BF16)
    f = [_dot(act[rows[r], :], wdn_ref[...]) + bdn_ref[...] for r in range(nsub)]
    for r in range(nsub):
        o_ref[rows[r], :] = _ln(alpha * x[r] + mod(g2_ref, r) * f[r]) * lg_ref[...] + lb_ref[...]


def _ffn(x, sh, sc, g2, cb0, w, tm, stride, alpha):
    bsz, seq, _ = x.shape
    mrows = sh.shape[1]
    halo = cb0.shape[1]
    cs = pl.BlockSpec((None, halo, 2 * D_FF), lambda b, i: (b, 0, 0))
    weights = [w["wup"], w["bup"], w["cw"], w["cbias"], w["wdn"], w["bdn"], w["ln2_g"], w["ln2_b"]]
    return pl.pallas_call(
        functools.partial(_ffn_body, alpha, tm, stride, halo),
        out_shape=[jax.ShapeDtypeStruct((bsz, seq, D_MODEL), F32),
                   jax.ShapeDtypeStruct((bsz, halo, 2 * D_FF), F32)],
        grid=(bsz, seq // tm),
        in_specs=[_tok_spec(tm, D_MODEL), _mod_spec(mrows, tm), _mod_spec(mrows, tm),
                  _mod_spec(mrows, tm), cs] + [_const_spec(a.shape) for a in weights],
        out_specs=[_tok_spec(tm, D_MODEL), cs],
        scratch_shapes=[pltpu.VMEM((halo + tm, D_FF), F32), pltpu.VMEM((tm, D_FF), BF16)],
        compiler_params=_params(2),
        name="ffn",
    )(x, sh, sc, g2, cb0, *weights)


_O_GATE = 2 * DQK_M + DV_M
_O_OG = _O_GATE + 2 * NH_M
_O_QA = _O_OG + DV_M
_O_KA = _O_QA + DQ_A
_O_VA = _O_KA + DKV_A
_O_GM = _O_VA + DKV_A
_PREP_ROWS = 256


def _split_w_in_body(wt_ref, qk_ref, vm_ref, vmt_ref, g_ref, gates_ref, qa_ref, k_ref, v_ref, vt_ref):
    piece = lambda lo, hi: wt_ref[lo:hi, :]
    qk_ref[...] = piece(0, 2 * DQK_M).T.astype(BF16)
    vm = piece(2 * DQK_M, _O_GATE)
    vmt_ref[...] = vm.astype(BF16)
    vm_ref[...] = vm.T.astype(BF16)
    pad = jnp.zeros((LANES - 2 * NH_M, _PREP_ROWS), F32)
    g_ref[...] = jnp.concatenate([piece(_O_GATE, _O_OG), pad], axis=0).T.astype(BF16)
    gates_ref[:, :DV_M] = piece(_O_OG, _O_QA).T.astype(BF16)
    gates_ref[:, DV_M:] = piece(_O_GM, wt_ref.shape[0]).T.astype(BF16)
    qa_ref[...] = piece(_O_QA, _O_KA).T.astype(BF16)
    k_ref[...] = piece(_O_KA, _O_VA).T.astype(BF16)
    v = piece(_O_VA, _O_GM)
    vt_ref[...] = v.astype(BF16)
    v_ref[...] = v.T.astype(BF16)


def _split_w_in(w_in_t):
    d_in = w_in_t.shape[0]
    rows = lambda n: ((D_MODEL, n), pl.BlockSpec((_PREP_ROWS, n), lambda i: (i, 0)))
    cols = lambda n: ((n, D_MODEL), pl.BlockSpec((n, _PREP_ROWS), lambda i: (0, i)))
    outs = dict(wqk=rows(2 * DQK_M), wvm=rows(DV_M), wvm_t=cols(DV_M), wg=rows(LANES),
                wgates=rows(DV_M + d_in - _O_GM), wqa=rows(DQ_A), wk=rows(DKV_A), wv=rows(DKV_A),
                wv_t=cols(DKV_A))
    res = pl.pallas_call(
        _split_w_in_body,
        out_shape=[jax.ShapeDtypeStruct(s, BF16) for s, _ in outs.values()],
        grid=(D_MODEL // _PREP_ROWS,),
        in_specs=[pl.BlockSpec((d_in, _PREP_ROWS), lambda i: (0, i))],
        out_specs=[spec for _, spec in outs.values()],
        compiler_params=_params(1),
        name="split_w_in",
    )(w_in_t)
    return dict(zip(outs.keys(), res))


def _prep_weights(w_in, b_in, mlstm_norm_w, w_branch_m, w_branch_a, w_out, ln1_g, ln1_b,
                  w_up, b_up, conv_w, conv_b, w_down, b_down, ln2_g, ln2_b):
    row = lambda a: a.reshape(1, -1)
    gate_pad = LANES - 2 * NH_M
    b_k, b_v = b_in[_O_KA:_O_VA], b_in[_O_VA:_O_GM]
    return dict(
        **_split_w_in(w_in.T),
        bqk=row(b_in[:2 * DQK_M]),
        bvm_row=row(b_in[2 * DQK_M:_O_GATE]), bvm_col=b_in[2 * DQK_M:_O_GATE].reshape(-1, 1),
        bg=row(jnp.pad(b_in[_O_GATE:_O_OG], (0, gate_pad))),
        bqa=row(b_in[_O_QA:_O_KA]),
        bk=row(b_k),
        bv_row=row(b_v), bv_col=b_v.reshape(-1, 1),
        bgates=row(jnp.concatenate([b_in[_O_OG:_O_QA], b_in[_O_GM:]])),
        norm_w=row(mlstm_norm_w), norm_w_col=mlstm_norm_w.reshape(-1, 1),
        wbm=w_branch_m.astype(BF16), wba=w_branch_a.astype(BF16),
        wo=w_out.astype(BF16), ln1_g=row(ln1_g), ln1_b=row(ln1_b),
        wup=w_up.astype(BF16), bup=row(b_up), cw=conv_w, cbias=row(conv_b),
        wdn=w_down.astype(BF16), bdn=row(b_down), ln2_g=row(ln2_g), ln2_b=row(ln2_b))


def kernel(x_prompt, x_sample, c_prompt, c_sample, state_mlstm_C, state_mlstm_n, state_mlstm_m,
           cache_k_win, cache_v_win, state_ffn_conv, w_ada, b_ada, w_in, b_in, mlstm_norm_w,
           attn_sinks, w_branch_m, w_branch_a, w_out, ln1_g, ln1_b, w_up, b_up, conv_w, conv_b,
           w_down, b_down, ln2_g, ln2_b):
    depth = w_in.shape[0]
    bp, lp, _ = x_prompt.shape
    bs, ls, _ = x_sample.shape
    assert cache_k_win.shape[2] == WINDOW
    alpha = (2 * depth) ** 0.25
    dt = x_prompt.dtype
    slopes = jnp.exp2(-8.0 * jnp.arange(1, NH_A + 1, dtype=F32) / NH_A)
    tm_p = 512
    tm_in = 1024
    ns = bs * ls

    yp = x_prompt
    to_pos = lambda a: a.transpose(1, 0, 2).reshape(1, -1, a.shape[-1])
    ys = to_pos(x_sample)
    new_p, new_s = [], []
    n_c = bp + bs
    c_rows = -(-n_c // SUBLANES) * SUBLANES
    c_all = jnp.concatenate([c_sample, c_prompt, jnp.zeros((c_rows - n_c, D_MODEL), dt)], axis=0)
    for l in range(depth):
        w = _prep_weights(w_in[l], b_in[l], mlstm_norm_w[l], w_branch_m[l], w_branch_a[l], w_out[l],
                          ln1_g[l], ln1_b[l], w_up[l], b_up[l], conv_w[l], conv_b[l], w_down[l],
                          b_down[l], ln2_g[l], ln2_b[l])
        prm = jnp.stack([slopes, attn_sinks[l].astype(F32)])
        mod = _ada(c_all, w_ada[l], b_ada[l])
        mod_s = mod[:bs].reshape(bs, N_MOD, D_MODEL)
        mod_p = mod[bs:bs + bp].reshape(bp, N_MOD, D_MODEL)
        mp_ = [mod_p[:, j:j + 1] for j in range(N_MOD)]
        ms_ = [jnp.tile(mod_s[:, j], (ls, 1))[None] for j in range(N_MOD)]

        hp, qk, v, g, qa, kb, ka, vat = _inproj(yp, mp_[0], mp_[1], w, tm_in, True)
        hm, cp, np_, mp = _mlstm_prompt(qk, v, g, jnp.zeros((bp, NH_M, DHV_M, DHK_M), dt),
                                        jnp.zeros((bp, NH_M, DHK_M), dt), jnp.zeros((bp, 1, LANES), dt),
                                        w["norm_w_col"])
        ha = _swa_prompt(prm, qa, kb, vat)
        x1p = _merge(yp, hp, mp_[2], hm, ha, w, tm_p, alpha)
        halo_p = SUBLANES
        yp, csp = _ffn(x1p, mp_[3], mp_[4], mp_[5], jnp.zeros((bp, halo_p, 2 * D_FF), dt),
                       w, tm_p, 1, alpha)
        p_k = ka[:, lp - WINDOW:].reshape(bp, WINDOW, NKV_A, HD_A)
        p_v = vat[:, :, lp - WINDOW:].reshape(bp, NKV_A, HD_A, WINDOW).transpose(0, 3, 1, 2)
        new_p.append((cp, np_, mp[:, 0, :NH_M], p_k, p_v, csp[:, halo_p - (CONV_W - 1):]))

        hs, qk, v, g, qa, kn, vn = _inproj(ys, ms_[0], ms_[1], w, ns, False)
        per_seq = lambda a: a.reshape(ls, bs, a.shape[-1])
        m0 = jnp.pad(state_mlstm_m[l], ((0, 0), (0, LANES - NH_M)))[:, None, :]
        hm, cs_, ns_, ms = _mlstm_sample(per_seq(qk), per_seq(v), per_seq(g), state_mlstm_C[l],
                                         state_mlstm_n[l], m0, w["norm_w"], SEQS_PER_STEP)
        to_t = lambda a: a.transpose(0, 2, 3, 1).reshape(bs, DKV_A, WINDOW)
        from_t = lambda a: a.reshape(bs, NKV_A, HD_A, WINDOW).transpose(0, 3, 1, 2)
        ha, kct, vct = _swa_sample(prm, per_seq(qa), per_seq(kn), per_seq(vn),
                                   to_t(cache_k_win[l]), to_t(cache_v_win[l]), SWA_SEQS_PER_STEP)
        x1s = _merge(ys, hs, ms_[2], hm.reshape(1, ns, DV_M), ha.reshape(1, ns, DQ_A), w, ns, alpha)
        ys, css = _ffn(x1s, ms_[3], ms_[4], ms_[5], to_pos(state_ffn_conv[l]), w, ns, bs, alpha)
        conv_s = css.reshape(CONV_W - 1, bs, 2 * D_FF).transpose(1, 0, 2)
        new_s.append((cs_, ns_, ms[:, 0, :NH_M], from_t(kct), from_t(vct), conv_s))

    p_state = [jnp.stack(a) for a in zip(*new_p)]
    s_state = [jnp.stack(a) for a in zip(*new_s)]
    y_sample = ys.reshape(ls, bs, D_MODEL).transpose(1, 0, 2)
    return (yp, y_sample, *p_state, *s_state)
```

```python
import functools

import jax
import jax.numpy as jnp
from jax import lax
from jax.experimental import pallas as pl
from jax.experimental.pallas import tpu as pltpu

F32 = jnp.float32
BF16 = jnp.bfloat16

D_MODEL = 1024
NH_M, DHK_M, DHV_M = 4, 128, 256
DQK_M, DV_M = NH_M * DHK_M, NH_M * DHV_M
NH_A, NKV_A, HD_A = 16, 4, 64
GROUP_A = NH_A // NKV_A
WINDOW = 128
DQ_A, DKV_A = NH_A * HD_A, NKV_A * HD_A
D_FF = 2816
CONV_W = 3
N_MOD = 6
LN_EPS = 1e-5
CHUNK = 128
NEG = -1e30
LOG2E = 1.4426950408889634
LANES = 128
SUBLANES = 8
VMEM_LIMIT = 56 * 1024 * 1024
SEQS_PER_STEP = 8
SWA_SEQS_PER_STEP = 8


def _ln(x):
    mu = jnp.mean(x, axis=-1, keepdims=True)
    xc = x - mu
    var = jnp.mean(xc * xc, axis=-1, keepdims=True)
    return xc * lax.rsqrt(var + LN_EPS)


def _dot(a, b):
    return jnp.dot(a, b, preferred_element_type=F32)


def _dot_nt(a, b):
    return lax.dot_general(a, b, (((1,), (1,)), ((), ())), preferred_element_type=F32)


def _dot_tn(a, b):
    return lax.dot_general(a, b, (((0,), (0,)), ((), ())), preferred_element_type=F32)


def _const_spec(shape):
    nd = len(shape)
    return pl.BlockSpec(shape, lambda *_: (0,) * nd, pipeline_mode=pl.Buffered(1))


def _params(n_grid):
    return pltpu.CompilerParams(dimension_semantics=("arbitrary",) * n_grid,
                                vmem_limit_bytes=VMEM_LIMIT)


def _first_step():
    return (pl.program_id(0) == 0) & (pl.program_id(1) == 0)


def _ada_body(c_ref, w_ref, b_ref, o_ref):
    c = c_ref[...]
    s = (c * jax.nn.sigmoid(c)).astype(BF16)
    o_ref[...] = _dot(s, w_ref[...].astype(BF16)) + b_ref[...]


def _ada(c, w_ada, b_ada):
    rows = c.shape[0]
    n_out = w_ada.shape[1]
    bn = 3072
    assert n_out % bn == 0
    return pl.pallas_call(
        _ada_body,
        out_shape=jax.ShapeDtypeStruct((rows, n_out), F32),
        grid=(n_out // bn,),
        in_specs=[pl.BlockSpec((rows, D_MODEL), lambda j: (0, 0)),
                  pl.BlockSpec((D_MODEL, bn), lambda j: (0, j)),
                  pl.BlockSpec((1, bn), lambda j: (0, j))],
        out_specs=pl.BlockSpec((rows, bn), lambda j: (0, j)),
        compiler_params=_params(1),
        name="ada",
    )(c, w_ada, b_ada.reshape(1, n_out))


def _inproj_body(prompt, x_ref, sh_ref, sc_ref, wqk_ref, bqk_ref, wvm_ref, bvm_ref, wg_ref, bg_ref,
                 wqa_ref, bqa_ref, wk_ref, bk_ref, wv_ref, bv_ref, h_ref, qk_ref, v_ref, g_ref, qa_ref, *rest):
    h = (_ln(x_ref[...]) * (1.0 + sc_ref[...]) + sh_ref[...]).astype(BF16)
    h_ref[...] = h

    def proj(w_ref, b_ref, lo, n):
        return _dot(h, w_ref[:, lo:lo + n]) + b_ref[:, lo:lo + n]

    qk_ref[:, :DQK_M] = (proj(wqk_ref, bqk_ref, 0, DQK_M) * DHK_M ** -0.5).astype(qk_ref.dtype)
    qk_ref[:, DQK_M:] = proj(wqk_ref, bqk_ref, DQK_M, DQK_M).astype(qk_ref.dtype)
    if prompt:
        v_ref[...] = (_dot_nt(wvm_ref[...], h) + bvm_ref[...]).astype(v_ref.dtype)
    else:
        v_ref[...] = proj(wvm_ref, bvm_ref, 0, DV_M).astype(v_ref.dtype)
    g_ref[...] = proj(wg_ref, bg_ref, 0, LANES)
    qa_ref[...] = (proj(wqa_ref, bqa_ref, 0, DQ_A) * (HD_A ** -0.5 * LOG2E)).astype(qa_ref.dtype)
    ka = proj(wk_ref, bk_ref, 0, DKV_A)
    if prompt:
        kb_ref, ka_ref, vat_ref = rest
        kb_ref[...] = ka.astype(BF16)
        ka_ref[...] = ka
        vat_ref[...] = _dot_nt(wv_ref[...], h) + bv_ref[...]
    else:
        ka_ref, va_ref = rest
        ka_ref[...] = ka
        va_ref[...] = _dot(h, wv_ref[...]) + bv_ref[...]


def _tok_spec(tm, n):
    return pl.BlockSpec((None, tm, n), lambda b, i: (b, i, 0))


def _mod_spec(rows, tm):
    if rows == 1:
        return pl.BlockSpec((None, 1, D_MODEL), lambda b, i: (b, 0, 0))
    return pl.BlockSpec((None, tm, D_MODEL), lambda b, i: (b, i, 0))


def _inproj(x, sh, sc, w, tm, prompt):
    bsz, seq, _ = x.shape
    mrows = sh.shape[1]
    act = BF16 if prompt else F32
    t_spec = lambda n: pl.BlockSpec((None, n, tm), lambda b, i: (b, 0, i))
    outs = [((bsz, seq, D_MODEL), BF16, _tok_spec(tm, D_MODEL)),
            ((bsz, seq, 2 * DQK_M), act, _tok_spec(tm, 2 * DQK_M)),
            ((bsz, DV_M, seq), act, t_spec(DV_M)) if prompt else ((bsz, seq, DV_M), act, _tok_spec(tm, DV_M)),
            ((bsz, seq, LANES), F32, _tok_spec(tm, LANES)),
            ((bsz, seq, DQ_A), act, _tok_spec(tm, DQ_A))]
    if prompt:
        wvm, bvm, wv, bv = w["wvm_t"], w["bvm_col"], w["wv_t"], w["bv_col"]
        outs += [((bsz, seq, DKV_A), BF16, _tok_spec(tm, DKV_A)),
                 ((bsz, seq, DKV_A), F32, _tok_spec(tm, DKV_A)),
                 ((bsz, DKV_A, seq), F32, t_spec(DKV_A))]
    else:
        wvm, bvm, wv, bv = w["wvm"], w["bvm_row"], w["wv"], w["bv_row"]
        outs += [((bsz, seq, DKV_A), F32, _tok_spec(tm, DKV_A)),
                 ((bsz, seq, DKV_A), F32, _tok_spec(tm, DKV_A))]
    weights = [w["wqk"], w["bqk"], wvm, bvm, w["wg"], w["bg"], w["wqa"], w["bqa"], w["wk"], w["bk"],
               wv, bv]
    return pl.pallas_call(
        functools.partial(_inproj_body, prompt),
        out_shape=[jax.ShapeDtypeStruct(s, dt) for s, dt, _ in outs],
        grid=(bsz, seq // tm),
        in_specs=[_tok_spec(tm, D_MODEL), _mod_spec(mrows, tm), _mod_spec(mrows, tm)]
                 + [_const_spec(a.shape) for a in weights],
        out_specs=[spec for _, _, spec in outs],
        compiler_params=_params(2),
        name="inproj",
    )(x, sh, sc, *weights)


def _scan_rows(x, op, rows):
    row = lax.broadcasted_iota(jnp.int32, x.shape, 0)
    d = 1
    while d < rows:
        shifted = pltpu.roll(x, d, axis=0)
        x = jnp.where(row >= d, op(x, shifted), x)
        d *= 2
    return x


def _mlstm_sample_body(t_in, nb, qk_ref, v_ref, g_ref, c0_ref, n0_ref, m0_ref, nw_ref,
                       h_ref, c_ref, n_ref, m_ref, *pads):
    t = SUBLANES
    c_ref[...] = c0_ref[...]
    n_ref[...] = n0_ref[...]
    m_ref[...] = m0_ref[...]

    @pl.when(_first_step())
    def _():
        for p in pads:
            p[...] = jnp.zeros(p.shape, p.dtype)
    for src, dst in zip((qk_ref, v_ref, g_ref), pads):
        for s in range(nb):
            dst[s, pl.ds(0, t_in), :] = src[:, s, :]
    qk_src, v_src, g_src = pads

    row = lax.broadcasted_iota(jnp.int32, (t, LANES), 0)
    lane = lax.broadcasted_iota(jnp.int32, (t, LANES), 1)
    lane1 = lax.broadcasted_iota(jnp.int32, (1, LANES), 1)
    r2 = lax.broadcasted_iota(jnp.int32, (t, t), 0)
    c2 = lax.broadcasted_iota(jnp.int32, (t, t), 1)
    causal = c2 <= r2
    pad_gate = jnp.where(lane < NH_M, NEG, -NEG)

    seqs = []
    for s in range(nb):
        ga = _gate_algebra(jnp.where(row < t_in, g_src[s], pad_gate), m_ref[s], t)
        ga["gt"] = ga["gg"].T
        m_ref[s] = jnp.where(lane1 < NH_M, ga["m_new"], 0.0)
        seqs.append(ga)

    units = [(s, hd) for s in range(nb) for hd in range(NH_M)]
    nu = len(units)
    col = lambda name, s, hd: seqs[s][name][:, hd:hd + 1]
    q = [qk_src[s, :, hd * DHK_M:(hd + 1) * DHK_M].astype(BF16) for s, hd in units]
    k = [qk_src[s, :, DQK_M + hd * DHK_M:DQK_M + (hd + 1) * DHK_M].astype(BF16) for s, hd in units]
    v = [v_src[s, :, hd * DHV_M:(hd + 1) * DHV_M].astype(BF16) for s, hd in units]
    c_old = [c_ref[s, hd] for s, hd in units]
    n_old = [n_ref[s, hd:hd + 1, :] for s, hd in units]

    sqk = [_dot_nt(q[u], k[u]) for u in range(nu)]
    dexp = [jnp.exp(jnp.where(causal, col("eb", s, hd) + seqs[s]["gt"][hd:hd + 1, :], NEG))
            for s, hd in units]
    smat = [sqk[u] * dexp[u] for u in range(nu)]
    intra = [_dot(smat[u].astype(BF16), v[u]) for u in range(nu)]
    inter = [_dot_nt(q[u], c_old[u].astype(BF16)) for u in range(nu)]
    qn = [jnp.sum(q[u].astype(F32) * n_old[u].astype(BF16).astype(F32), axis=-1, keepdims=True)
          for u in range(nu)]
    den = [jnp.sum(smat[u], axis=-1, keepdims=True) + col("a_in", s, hd) * qn[u]
           for u, (s, hd) in enumerate(units)]
    hh = [(intra[u] + col("a_in", s, hd) * inter[u])
          / jnp.maximum(jnp.abs(den[u]), col("lowb", s, hd)) for u, (s, hd) in enumerate(units)]
    hn = [_ln(hh[u]) * nw_ref[:, hd * DHV_M:(hd + 1) * DHV_M] for u, (s, hd) in enumerate(units)]
    for u, (s, hd) in enumerate(units):
        h_ref[:, s, hd * DHV_M:(hd + 1) * DHV_M] = hn[u][:t_in]

    kw = [k[u].astype(F32) * col("ws", s, hd) for u, (s, hd) in enumerate(units)]
    upd = [_dot_tn(v[u], kw[u].astype(BF16)) for u in range(nu)]
    for u, (s, hd) in enumerate(units):
        dec = seqs[s]["decay"][:, hd:hd + 1]
        c_ref[s, hd] = dec * c_old[u] + upd[u]
        n_ref[s, hd:hd + 1, :] = dec * n_old[u] + jnp.sum(kw[u], axis=0, keepdims=True)


def _mlstm_sample(qk, v, g, c0, n0, m0, norm_w, nb):
    t_in, nseq, _ = qk.shape
    assert t_in <= SUBLANES
    blk = lambda n: pl.BlockSpec((t_in, nb, n), lambda o, c: (0, o, 0))
    st_c = pl.BlockSpec((nb, NH_M, DHV_M, DHK_M), lambda o, c: (o, 0, 0, 0))
    st_n = pl.BlockSpec((nb, NH_M, DHK_M), lambda o, c: (o, 0, 0))
    st_m = pl.BlockSpec((nb, 1, LANES), lambda o, c: (o, 0, 0))
    return pl.pallas_call(
        functools.partial(_mlstm_sample_body, t_in, nb),
        out_shape=[jax.ShapeDtypeStruct((t_in, nseq, DV_M), F32),
                   jax.ShapeDtypeStruct((nseq, NH_M, DHV_M, DHK_M), F32),
                   jax.ShapeDtypeStruct((nseq, NH_M, DHK_M), F32),
                   jax.ShapeDtypeStruct((nseq, 1, LANES), F32)],
        grid=(nseq // nb, 1),
        in_specs=[blk(2 * DQK_M), blk(DV_M), blk(LANES), st_c, st_n, st_m,
                  pl.BlockSpec((1, DV_M), lambda o, c: (0, 0))],
        out_specs=[blk(DV_M), st_c, st_n, st_m],
        scratch_shapes=[pltpu.VMEM((nb, SUBLANES, 2 * DQK_M), F32), pltpu.VMEM((nb, SUBLANES, DV_M), F32),
                        pltpu.VMEM((nb, SUBLANES, LANES), F32)],
        compiler_params=_params(2),
        name="mlstm_sample",
    )(qk, v, g, c0, n0, m0, norm_w)


def _gate_algebra(g, m_prev, t):
    b = pltpu.roll(_scan_rows(jax.nn.log_sigmoid(g), jnp.add, t), LANES - NH_M, axis=1)
    gg = g - b
    gmx = _scan_rows(gg, jnp.maximum, t)
    a = b + m_prev
    mt = jnp.maximum(a, b + gmx)
    b_last, gmx_last = b[t - 1:t, :], gmx[t - 1:t, :]
    m_new = jnp.maximum(b_last + m_prev, b_last + gmx_last)
    return dict(gg=gg, eb=b - mt, a_in=jnp.exp(a - mt), lowb=jnp.exp(-mt),
                ws=jnp.exp(b_last + gg - m_new), decay=jnp.exp(b_last + m_prev - m_new), m_new=m_new)


MLSTM_CHUNKS_PER_STEP = 4


def _mlstm_prompt_body(nb, qk_ref, vt_ref, g_ref, c0_ref, n0_ref, m0_ref, nwc_ref,
                       h_ref, c_ref, n_ref, m_ref, nwb):
    @pl.when(pl.program_id(1) == 0)
    def _():
        c_ref[...] = c0_ref[...]
        n_ref[...] = n0_ref[...]
        m_ref[...] = m0_ref[...]

    @pl.when(_first_step())
    def _():
        for hd in range(NH_M):
            nwb[hd] = jnp.broadcast_to(nwc_ref[hd * DHV_M:(hd + 1) * DHV_M, :], (DHV_M, CHUNK))

    for ci in range(MLSTM_CHUNKS_PER_STEP):
        _mlstm_prompt_chunk(nb, slice(ci * CHUNK, (ci + 1) * CHUNK), qk_ref, vt_ref, g_ref, nwb,
                            h_ref, c_ref, n_ref, m_ref)


def _mlstm_prompt_chunk(nb, rows, qk_ref, vt_ref, g_ref, nwb, h_ref, c_ref, n_ref, m_ref):
    t = CHUNK
    lane1 = lax.broadcasted_iota(jnp.int32, (1, LANES), 1)
    r2 = lax.broadcasted_iota(jnp.int32, (t, t), 0)
    c2 = lax.broadcasted_iota(jnp.int32, (t, t), 1)
    causal = r2 <= c2

    seqs = []
    for s in range(nb):
        ga = _gate_algebra(g_ref[s, rows, :], m_ref[s], t)
        m_ref[s] = jnp.where(lane1 < NH_M, ga["m_new"], 0.0)
        for name in ("eb", "a_in", "lowb"):
            ga[name + "_t"] = ga[name].T
        seqs.append(ga)

    units = [(s, hd) for s in range(nb) for hd in range(NH_M)]
    nu = len(units)
    rowv = lambda name, s, hd: seqs[s][name + "_t"][hd:hd + 1, :]
    q = [qk_ref[s, rows, hd * DHK_M:(hd + 1) * DHK_M] for s, hd in units]
    k = [qk_ref[s, rows, DQK_M + hd * DHK_M:DQK_M + (hd + 1) * DHK_M] for s, hd in units]
    vt = [vt_ref[s, hd * DHV_M:(hd + 1) * DHV_M, rows] for s, hd in units]
    c_old = [c_ref[s, hd] for s, hd in units]
    n_old = [n_ref[s, hd:hd + 1, :] for s, hd in units]

    skq = [_dot_nt(k[u], q[u]) for u in range(nu)]
    dexp = [jnp.exp(jnp.where(causal, seqs[s]["gg"][:, hd:hd + 1] + rowv("eb", s, hd), NEG))
            for s, hd in units]
    smat = [skq[u] * dexp[u] for u in range(nu)]
    intra = [_dot(vt[u], smat[u].astype(BF16)) for u in range(nu)]
    inter = [_dot_nt(c_old[u].astype(BF16), q[u]) for u in range(nu)]
    qn = [_dot_nt(jnp.broadcast_to(n_old[u], (SUBLANES, DHK_M)).astype(BF16), q[u])[0:1, :]
          for u in range(nu)]
    den = [jnp.sum(smat[u], axis=0, keepdims=True) + rowv("a_in", s, hd) * qn[u]
           for u, (s, hd) in enumerate(units)]
    inv = [1.0 / jnp.maximum(jnp.abs(den[u]), rowv("lowb", s, hd)) for u, (s, hd) in enumerate(units)]
    hh = [(intra[u] + rowv("a_in", s, hd) * inter[u]) * inv[u] for u, (s, hd) in enumerate(units)]
    mu = [jnp.mean(hh[u], axis=0, keepdims=True) for u in range(nu)]
    xc = [hh[u] - mu[u] for u in range(nu)]
    var = [jnp.mean(xc[u] * xc[u], axis=0, keepdims=True) for u in range(nu)]
    hn = [xc[u] * lax.rsqrt(var[u] + LN_EPS) * nwb[hd] for u, (s, hd) in enumerate(units)]
    for u, (s, hd) in enumerate(units):
        h_ref[s, rows, hd * DHV_M:(hd + 1) * DHV_M] = hn[u].T

    kw = [(k[u].astype(F32) * seqs[s]["ws"][:, hd:hd + 1]).astype(BF16) for u, (s, hd) in enumerate(units)]
    upd = [_dot(vt[u], kw[u]) for u in range(nu)]
    nupd = [_dot(jnp.ones((SUBLANES, t), BF16), kw[u])[0:1, :] for u in range(nu)]
    for u, (s, hd) in enumerate(units):
        dec = seqs[s]["decay"][:, hd:hd + 1]
        c_ref[s, hd] = dec * c_old[u] + upd[u]
        n_ref[s, hd:hd + 1, :] = dec * n_old[u] + nupd[u]


def _mlstm_prompt(qk, vt, g, c0, n0, m0, norm_w_col):
    nb, seq, _ = qk.shape
    ts = MLSTM_CHUNKS_PER_STEP * CHUNK
    tok = lambda n: pl.BlockSpec((nb, ts, n), lambda o, c: (0, c, 0))
    st_c = pl.BlockSpec((nb, NH_M, DHV_M, DHK_M), lambda o, c: (0, 0, 0, 0))
    st_n = pl.BlockSpec((nb, NH_M, DHK_M), lambda o, c: (0, 0, 0))
    st_m = pl.BlockSpec((nb, 1, LANES), lambda o, c: (0, 0, 0))
    return pl.pallas_call(
        functools.partial(_mlstm_prompt_body, nb),
        out_shape=[jax.ShapeDtypeStruct((nb, seq, DV_M), F32),
                   jax.ShapeDtypeStruct((nb, NH_M, DHV_M, DHK_M), F32),
                   jax.ShapeDtypeStruct((nb, NH_M, DHK_M), F32),
                   jax.ShapeDtypeStruct((nb, 1, LANES), F32)],
        grid=(1, seq // ts),
        in_specs=[tok(2 * DQK_M), pl.BlockSpec((nb, DV_M, ts), lambda o, c: (0, 0, c)), tok(LANES),
                  st_c, st_n, st_m, pl.BlockSpec((DV_M, 1), lambda o, c: (0, 0))],
        out_specs=[tok(DV_M), st_c, st_n, st_m],
        scratch_shapes=[pltpu.VMEM((NH_M, DHV_M, CHUNK), F32)],
        compiler_params=_params(2),
        name="mlstm_prompt",
    )(qk, vt, g, c0, n0, m0, norm_w_col)


SWA_BLOCKS_PER_STEP = 16


def _swa_prompt_body(prm_ref, q_ref, kc_ref, kp_ref, vc_ref, vp_ref, o_ref, tbl):
    row = lax.broadcasted_iota(jnp.int32, (WINDOW, WINDOW), 0)
    col = lax.broadcasted_iota(jnp.int32, (WINDOW, WINDOW), 1)
    tri = row <= col

    @pl.when(_first_step())
    def _():
        dist = jnp.where(tri, col - row, col - row + WINDOW).astype(F32)
        for hd in range(NH_A):
            bias = prm_ref[0, hd] * dist
            tbl[0, hd] = bias
            tbl[1, hd] = bias + jnp.where(tri, 0.0, -NEG)

    for b in range(SWA_BLOCKS_PER_STEP):
        rows = pl.ds(b * WINDOW, WINDOW)
        which = jnp.where(pl.program_id(1) == 0, 1, 0) if b == 0 else 0
        k_prev = kp_ref if b == 0 else kc_ref.at[pl.ds((b - 1) * WINDOW, WINDOW), :]
        v_prev = vp_ref if b == 0 else vc_ref.at[:, pl.ds((b - 1) * WINDOW, WINDOW)]
        _swa_prompt_block(which, prm_ref, q_ref.at[rows, :], kc_ref.at[rows, :], k_prev,
                          vc_ref.at[:, rows], v_prev, o_ref.at[rows, :], tbl)


def _swa_prompt_block(which, prm_ref, q_ref, kc_ref, kp_ref, vc_ref, vp_ref, o_ref, tbl):
    row = lax.broadcasted_iota(jnp.int32, (WINDOW, WINDOW), 0)
    col = lax.broadcasted_iota(jnp.int32, (WINDOW, WINDOW), 1)
    tri = row <= col
    lo = col < HD_A
    zb = jnp.zeros((WINDOW, LANES), BF16)

    def placed(k_ref, kv):
        c, par = kv // 2, kv % 2
        own = k_ref[:, c * LANES:(c + 1) * LANES]
        swp = pltpu.roll(own, HD_A, axis=1)
        if par == 0:
            return jnp.where(lo, own, zb), jnp.where(lo, zb, swp)
        return jnp.where(lo, swp, zb), jnp.where(lo, zb, own)

    st = []
    for kv in range(NKV_A):
        lhs = jnp.concatenate([*placed(kc_ref, kv), *placed(kp_ref, kv)], axis=0)
        qg = jnp.concatenate([q_ref[:, (2 * kv) * LANES:(2 * kv + 1) * LANES],
                              q_ref[:, (2 * kv + 1) * LANES:(2 * kv + 2) * LANES]], axis=0)
        st.append(_dot_nt(lhs, qg))
    tiles = []
    for hd in range(NH_A):
        kv, a, par = hd // GROUP_A, (hd % GROUP_A) // 2, hd % 2
        cols = slice(a * WINDOW, (a + 1) * WINDOW)
        tiles.append(jnp.where(tri, st[kv][par * WINDOW:(par + 1) * WINDOW, cols],
                               st[kv][(2 + par) * WINDOW:(3 + par) * WINDOW, cols]))
    sc = jnp.concatenate(tiles, axis=0).reshape(NH_A, WINDOW, WINDOW) - tbl[which]
    sink = jnp.concatenate([jnp.full((1, 1, WINDOW), prm_ref[1, hd], F32) for hd in range(NH_A)], axis=0)
    mx = jnp.maximum(jnp.max(sc, axis=1, keepdims=True), sink)
    p = jnp.exp2(sc - mx)
    den = jnp.sum(p, axis=1, keepdims=True) + jnp.exp2(sink - mx)
    pn = p * (1.0 / den)

    zero = jnp.zeros((WINDOW, WINDOW), F32)
    z64 = jnp.zeros((HD_A, WINDOW), BF16)
    outs = []
    for kv in range(NKV_A):
        cols = []
        for a in range(2):
            pe, po = pn[kv * GROUP_A + 2 * a], pn[kv * GROUP_A + 2 * a + 1]
            cols.append(jnp.concatenate([jnp.where(tri, pe, zero), jnp.where(tri, po, zero),
                                         jnp.where(tri, zero, pe), jnp.where(tri, zero, po)],
                                        axis=0).astype(BF16))
        pt = jnp.concatenate(cols, axis=1)
        vc = vc_ref[kv * HD_A:(kv + 1) * HD_A, :].astype(BF16)
        vp = vp_ref[kv * HD_A:(kv + 1) * HD_A, :].astype(BF16)
        vt = jnp.concatenate([jnp.concatenate([vc, z64], axis=0), jnp.concatenate([z64, vc], axis=0),
                              jnp.concatenate([vp, z64], axis=0), jnp.concatenate([z64, vp], axis=0)],
                             axis=1)
        ot = _dot(vt, pt)
        outs += [ot[:, :WINDOW].T, ot[:, WINDOW:].T]
    o_ref[...] = jnp.concatenate(outs, axis=1).astype(o_ref.dtype)


def _swa_prompt(prm, q, kb, vat):
    bsz, seq, _ = q.shape
    tq = SWA_BLOCKS_PER_STEP * WINDOW
    prev = lambda i: jnp.maximum(i * SWA_BLOCKS_PER_STEP - 1, 0)
    return pl.pallas_call(
        _swa_prompt_body,
        out_shape=jax.ShapeDtypeStruct((bsz, seq, DQ_A), BF16),
        grid=(bsz, seq // tq),
        in_specs=[pl.BlockSpec(memory_space=pltpu.SMEM),
                  pl.BlockSpec((None, tq, DQ_A), lambda b, i: (b, i, 0)),
                  pl.BlockSpec((None, tq, DKV_A), lambda b, i: (b, i, 0)),
                  pl.BlockSpec((None, WINDOW, DKV_A), lambda b, i: (b, prev(i), 0)),
                  pl.BlockSpec((None, DKV_A, tq), lambda b, i: (b, 0, i)),
                  pl.BlockSpec((None, DKV_A, WINDOW), lambda b, i: (b, 0, prev(i)))],
        out_specs=pl.BlockSpec((None, tq, DQ_A), lambda b, i: (b, i, 0)),
        scratch_shapes=[pltpu.VMEM((2, NH_A, WINDOW, WINDOW), F32)],
        compiler_params=_params(2),
        name="swa_prompt",
    )(prm, q, kb, kb, vat, vat)


def _half_mask(shape, half):
    lane = lax.broadcasted_iota(jnp.int32, shape, 1)
    return lane < HD_A if half == 0 else lane >= HD_A


def _swa_sample_body(t_in, nb, prm_ref, q_ref, kn_ref, vn_ref, kct_ref, vct_ref,
                     o_ref, kco_ref, vco_ref, tbl, q8, kn_pad, vn_pad):
    tq = SUBLANES
    rows = NH_A * tq
    row = lax.broadcasted_iota(jnp.int32, (rows, WINDOW), 0)
    col = lax.broadcasted_iota(jnp.int32, (rows, WINDOW), 1)
    tri = col <= (row & (tq - 1))

    @pl.when(_first_step())
    def _():
        r8 = lax.broadcasted_iota(jnp.int32, (tq, WINDOW), 0)
        c8 = lax.broadcasted_iota(jnp.int32, (tq, WINDOW), 1)
        dist = jnp.where(c8 <= r8, r8 - c8, r8 - c8 + WINDOW).astype(F32)
        for hd in range(NH_A):
            tbl[pl.ds(hd * tq, tq), :] = prm_ref[0, hd] * dist
        for p in (q8, kn_pad, vn_pad):
            p[...] = jnp.zeros(p.shape, p.dtype)

    for s in range(nb):
        q8[s, pl.ds(0, t_in), :] = q_ref[:, s, :]
        kn_pad[s, pl.ds(0, t_in), :] = kn_ref[:, s, :]
        vn_pad[s, pl.ds(0, t_in), :] = vn_ref[:, s, :]
    sink_col = jnp.concatenate([jnp.full((tq, 1), prm_ref[1, hd], F32) for hd in range(NH_A)], axis=0)
    z8 = jnp.zeros((tq, LANES), F32)
    n_chunk = NKV_A // 2
    heads_per_chunk = NH_A // n_chunk

    def place(piece, src_half, dst_half):
        if src_half != dst_half:
            piece = pltpu.roll(piece, HD_A, axis=1)
        return jnp.where(_half_mask(piece.shape, dst_half), piece, z8)

    qexp = []
    for s in range(nb):
        per_c = []
        for c in range(n_chunk):
            pieces = []
            for hl in range(heads_per_chunk):
                hd = c * heads_per_chunk + hl
                pieces.append(place(q8[s, :, (hd // 2) * LANES:(hd // 2 + 1) * LANES],
                                    hd % 2, hl // GROUP_A))
            per_c.append(jnp.concatenate(pieces, axis=0).astype(BF16))
        qexp.append(per_c)
    csl = lambda c: slice(c * LANES, (c + 1) * LANES)
    s_prev = [[_dot(qexp[s][c], kct_ref[s, csl(c), :].astype(BF16)) for c in range(n_chunk)]
              for s in range(nb)]
    s_cur = [[_dot_nt(qexp[s][c], kn_pad[s, :, csl(c)].astype(BF16)) for c in range(n_chunk)]
             for s in range(nb)]
    sc = [jnp.where(tri, jnp.concatenate(s_cur[s], axis=0), jnp.concatenate(s_prev[s], axis=0)) - tbl[...]
          for s in range(nb)]
    mx = [jnp.maximum(jnp.max(sc[s], axis=-1, keepdims=True), sink_col) for s in range(nb)]
    p = [jnp.exp2(sc[s] - mx[s]) for s in range(nb)]
    den = [jnp.sum(p[s], axis=-1, keepdims=True) + jnp.exp2(sink_col - mx[s]) for s in range(nb)]
    pn = [p[s] * (1.0 / den[s]) for s in range(nb)]
    zero = jnp.zeros((rows, WINDOW), F32)
    pc = [jnp.where(tri, pn[s], zero).astype(BF16) for s in range(nb)]
    pp = [jnp.where(tri, zero, pn[s]).astype(BF16) for s in range(nb)]
    half_rows = heads_per_chunk * tq
    oc = [[_dot_nt(pp[s][c * half_rows:(c + 1) * half_rows], vct_ref[s, csl(c), :].astype(BF16))
           + _dot(pc[s][c * half_rows:(c + 1) * half_rows], vn_pad[s, :, csl(c)].astype(BF16))
           for c in range(n_chunk)] for s in range(nb)]
    for s in range(nb):
        chunks = []
        for pch in range(NH_A // 2):
            acc = None
            for hd in (2 * pch, 2 * pch + 1):
                c, hl = hd // heads_per_chunk, hd % heads_per_chunk
                piece = place(oc[s][c][hl * tq:(hl + 1) * tq, :], hl // GROUP_A, hd % 2)
                acc = piece if acc is None else acc + piece
            chunks.append(acc)
        o_ref[:, s, :] = jnp.concatenate(chunks, axis=1)[:t_in]

    lane = lax.broadcasted_iota(jnp.int32, (DKV_A, WINDOW), 1)
    for new_pad, old_ref, out_ref in ((kn_pad, kct_ref, kco_ref), (vn_pad, vct_ref, vco_ref)):
        for s in range(nb):
            merged = jnp.where(lane < t_in, new_pad[s].T, old_ref[s])
            out_ref[s] = pltpu.roll(merged, WINDOW - t_in, axis=1)


def _swa_sample(prm, q, kn, vn, kct, vct, nb):
    t_in, nseq, _ = q.shape
    assert t_in <= SUBLANES
    cur = lambda n: pl.BlockSpec((t_in, nb, n), lambda o, i: (0, o, 0))
    win = pl.BlockSpec((nb, DKV_A, WINDOW), lambda o, i: (o, 0, 0))
    return pl.pallas_call(
        functools.partial(_swa_sample_body, t_in, nb),
        out_shape=[jax.ShapeDtypeStruct((t_in, nseq, DQ_A), F32),
                   jax.ShapeDtypeStruct((nseq, DKV_A, WINDOW), F32),
                   jax.ShapeDtypeStruct((nseq, DKV_A, WINDOW), F32)],
        grid=(nseq // nb, 1),
        in_specs=[pl.BlockSpec(memory_space=pltpu.SMEM), cur(DQ_A), cur(DKV_A), cur(DKV_A), win, win],
        out_specs=[cur(DQ_A), win, win],
        scratch_shapes=[pltpu.VMEM((NH_A * SUBLANES, WINDOW), F32),
                        pltpu.VMEM((nb, SUBLANES, DQ_A), F32),
                        pltpu.VMEM((nb, WINDOW, DKV_A), F32),
                        pltpu.VMEM((nb, WINDOW, DKV_A), F32)],
        compiler_params=_params(2),
        name="swa_sample",
    )(prm, q, kn, vn, kct, vct)


def _merge_body(alpha, x_ref, h_ref, g1_ref, hm_ref, ha_ref, wgt_ref, bgt_ref,
                wbm_ref, wba_ref, wo_ref, lg_ref, lb_ref, o_ref):
    nsub = 2
    sub = x_ref.shape[0] // nsub
    rows = [pl.ds(r * sub, sub) for r in range(nsub)]
    mod = lambda ref, r: ref[...] if ref.shape[0] == 1 else ref[rows[r], :]
    gate = lambda g, j: g[:, j * D_MODEL:(j + 1) * D_MODEL]

    x = [x_ref[rows[r], :] for r in range(nsub)]
    h = [h_ref[rows[r], :] for r in range(nsub)]
    g = [jax.nn.sigmoid(_dot(h[r], wgt_ref[...]) + bgt_ref[...]) for r in range(nsub)]
    hm = [(hm_ref[rows[r], :] * gate(g[r], 0)).astype(BF16) for r in range(nsub)]
    bm = [_dot(hm[r], wbm_ref[...]) for r in range(nsub)]
    ba = [_dot(ha_ref[rows[r], :].astype(BF16), wba_ref[...]) for r in range(nsub)]
    merged = [(gate(g[r], 1) * bm[r] + gate(g[r], 2) * ba[r]).astype(BF16) for r in range(nsub)]
    mo = [_dot(merged[r], wo_ref[...]) for r in range(nsub)]
    for r in range(nsub):
        o_ref[rows[r], :] = _ln(alpha * x[r] + mod(g1_ref, r) * mo[r]) * lg_ref[...] + lb_ref[...]


def _merge(x, h, g1, hm, ha, w, tm, alpha):
    bsz, seq, _ = x.shape
    mrows = g1.shape[1]
    weights = [w["wgates"], w["bgates"], w["wbm"], w["wba"], w["wo"],
               w["ln1_g"], w["ln1_b"]]
    return pl.pallas_call(
        functools.partial(_merge_body, alpha),
        out_shape=jax.ShapeDtypeStruct((bsz, seq, D_MODEL), F32),
        grid=(bsz, seq // tm),
        in_specs=[_tok_spec(tm, D_MODEL), _tok_spec(tm, D_MODEL), _mod_spec(mrows, tm),
                  _tok_spec(tm, DV_M), _tok_spec(tm, DQ_A)]
                 + [_const_spec(a.shape) for a in weights],
        out_specs=_tok_spec(tm, D_MODEL),
        compiler_params=_params(2),
        name="merge",
    )(x, h, g1, hm, ha, *weights)


def _ffn_body(alpha, tm, stride, halo, x_ref, sh_ref, sc_ref, g2_ref, cb0_ref, wup_ref, bup_ref,
              cw_ref, cbias_ref, wdn_ref, bdn_ref, lg_ref, lb_ref, o_ref, cs_ref, ubuf, act):
    @pl.when(pl.program_id(1) == 0)
    def _():
        cs_ref[...] = cb0_ref[...]

    nsub = 2
    sub = tm // nsub
    rows = [pl.ds(r * sub, sub) for r in range(nsub)]
    mod = lambda ref, r: ref[...] if ref.shape[0] == 1 else ref[rows[r], :]
    x = [x_ref[rows[r], :] for r in range(nsub)]
    h = [(_ln(x[r]) * (1.0 + mod(sc_ref, r)) + mod(sh_ref, r)).astype(BF16) for r in range(nsub)]
    ys = [[None, None] for _ in range(nsub)]
    for half in range(2):
        cols = slice(half * D_FF, (half + 1) * D_FF)
        u = [_dot(h[r], wup_ref[:, cols]) + bup_ref[:, cols] for r in range(nsub)]
        ubuf[pl.ds(0, halo), :] = cs_ref[:, cols]
        for r in range(nsub):
            ubuf[pl.ds(halo + r * sub, sub), :] = u[r]
        cs_ref[:, cols] = ubuf[pl.ds(tm, halo), :]
        for r in range(nsub):
            y = cbias_ref[:, cols] + u[r] * cw_ref[CONV_W - 1:CONV_W, cols]
            for j in range(CONV_W - 1):
                tap = ubuf[pl.ds(halo + r * sub - (CONV_W - 1 - j) * stride, sub), :]
                y = y + tap * cw_ref[j:j + 1, cols]
            ys[r][half] = y
    for r in range(nsub):
        act[rows[r], :] = (jax.nn.gelu(ys[r][0]) * ys[r][1]).astype(BF16)
    f = [_dot(act[rows[r], :], wdn_ref[...]) + bdn_ref[...] for r in range(nsub)]
    for r in range(nsub):
        o_ref[rows[r], :] = _ln(alpha * x[r] + mod(g2_ref, r) * f[r]) * lg_ref[...] + lb_ref[...]


def _ffn(x, sh, sc, g2, cb0, w, tm, stride, alpha):
    bsz, seq, _ = x.shape
    mrows = sh.shape[1]
    halo = cb0.shape[1]
    cs = pl.BlockSpec((None, halo, 2 * D_FF), lambda b, i: (b, 0, 0))
    weights = [w["wup"], w["bup"], w["cw"], w["cbias"], w["wdn"], w["bdn"], w["ln2_g"], w["ln2_b"]]
    return pl.pallas_call(
        functools.partial(_ffn_body, alpha, tm, stride, halo),
        out_shape=[jax.ShapeDtypeStruct((bsz, seq, D_MODEL), F32),
                   jax.ShapeDtypeStruct((bsz, halo, 2 * D_FF), F32)],
        grid=(bsz, seq // tm),
        in_specs=[_tok_spec(tm, D_MODEL), _mod_spec(mrows, tm), _mod_spec(mrows, tm),
                  _mod_spec(mrows, tm), cs] + [_const_spec(a.shape) for a in weights],
        out_specs=[_tok_spec(tm, D_MODEL), cs],
        scratch_shapes=[pltpu.VMEM((halo + tm, D_FF), F32), pltpu.VMEM((tm, D_FF), BF16)],
        compiler_params=_params(2),
        name="ffn",
    )(x, sh, sc, g2, cb0, *weights)


_O_GATE = 2 * DQK_M + DV_M
_O_OG = _O_GATE + 2 * NH_M
_O_QA = _O_OG + DV_M
_O_KA = _O_QA + DQ_A
_O_VA = _O_KA + DKV_A
_O_GM = _O_VA + DKV_A
_PREP_ROWS = 256


def _split_w_in_body(wt_ref, qk_ref, vm_ref, vmt_ref, g_ref, gates_ref, qa_ref, k_ref, v_ref, vt_ref):
    piece = lambda lo, hi: wt_ref[lo:hi, :]
    qk_ref[...] = piece(0, 2 * DQK_M).T.astype(BF16)
    vm = piece(2 * DQK_M, _O_GATE)
    vmt_ref[...] = vm.astype(BF16)
    vm_ref[...] = vm.T.astype(BF16)
    pad = jnp.zeros((LANES - 2 * NH_M, _PREP_ROWS), F32)
    g_ref[...] = jnp.concatenate([piece(_O_GATE, _O_OG), pad], axis=0).T.astype(BF16)
    gates_ref[:, :DV_M] = piece(_O_OG, _O_QA).T.astype(BF16)
    gates_ref[:, DV_M:] = piece(_O_GM, wt_ref.shape[0]).T.astype(BF16)
    qa_ref[...] = piece(_O_QA, _O_KA).T.astype(BF16)
    k_ref[...] = piece(_O_KA, _O_VA).T.astype(BF16)
    v = piece(_O_VA, _O_GM)
    vt_ref[...] = v.astype(BF16)
    v_ref[...] = v.T.astype(BF16)


def _split_w_in(w_in_t):
    d_in = w_in_t.shape[0]
    rows = lambda n: ((D_MODEL, n), pl.BlockSpec((_PREP_ROWS, n), lambda i: (i, 0)))
    cols = lambda n: ((n, D_MODEL), pl.BlockSpec((n, _PREP_ROWS), lambda i: (0, i)))
    outs = dict(wqk=rows(2 * DQK_M), wvm=rows(DV_M), wvm_t=cols(DV_M), wg=rows(LANES),
                wgates=rows(DV_M + d_in - _O_GM), wqa=rows(DQ_A), wk=rows(DKV_A), wv=rows(DKV_A),
                wv_t=cols(DKV_A))
    res = pl.pallas_call(
        _split_w_in_body,
        out_shape=[jax.ShapeDtypeStruct(s, BF16) for s, _ in outs.values()],
        grid=(D_MODEL // _PREP_ROWS,),
        in_specs=[pl.BlockSpec((d_in, _PREP_ROWS), lambda i: (0, i))],
        out_specs=[spec for _, spec in outs.values()],
        compiler_params=_params(1),
        name="split_w_in",
    )(w_in_t)
    return dict(zip(outs.keys(), res))


def _prep_weights(w_in, b_in, mlstm_norm_w, w_branch_m, w_branch_a, w_out, ln1_g, ln1_b,
                  w_up, b_up, conv_w, conv_b, w_down, b_down, ln2_g, ln2_b):
    row = lambda a: a.reshape(1, -1)
    gate_pad = LANES - 2 * NH_M
    b_k, b_v = b_in[_O_KA:_O_VA], b_in[_O_VA:_O_GM]
    return dict(
        **_split_w_in(w_in.T),
        bqk=row(b_in[:2 * DQK_M]),
        bvm_row=row(b_in[2 * DQK_M:_O_GATE]), bvm_col=b_in[2 * DQK_M:_O_GATE].reshape(-1, 1),
        bg=row(jnp.pad(b_in[_O_GATE:_O_OG], (0, gate_pad))),
        bqa=row(b_in[_O_QA:_O_KA]),
        bk=row(b_k),
        bv_row=row(b_v), bv_col=b_v.reshape(-1, 1),
        bgates=row(jnp.concatenate([b_in[_O_OG:_O_QA], b_in[_O_GM:]])),
        norm_w=row(mlstm_norm_w), norm_w_col=mlstm_norm_w.reshape(-1, 1),
        wbm=w_branch_m.astype(BF16), wba=w_branch_a.astype(BF16),
        wo=w_out.astype(BF16), ln1_g=row(ln1_g), ln1_b=row(ln1_b),
        wup=w_up.astype(BF16), bup=row(b_up), cw=conv_w, cbias=row(conv_b),
        wdn=w_down.astype(BF16), bdn=row(b_down), ln2_g=row(ln2_g), ln2_b=row(ln2_b))


def kernel(x_prompt, x_sample, c_prompt, c_sample, state_mlstm_C, state_mlstm_n, state_mlstm_m,
           cache_k_win, cache_v_win, state_ffn_conv, w_ada, b_ada, w_in, b_in, mlstm_norm_w,
           attn_sinks, w_branch_m, w_branch_a, w_out, ln1_g, ln1_b, w_up, b_up, conv_w, conv_b,
           w_down, b_down, ln2_g, ln2_b):
    depth = w_in.shape[0]
    bp, lp, _ = x_prompt.shape
    bs, ls, _ = x_sample.shape
    assert cache_k_win.shape[2] == WINDOW
    alpha = (2 * depth) ** 0.25
    dt = x_prompt.dtype
    slopes = jnp.exp2(-8.0 * jnp.arange(1, NH_A + 1, dtype=F32) / NH_A)
    tm_p = 512
    tm_in = 1024
    ns = bs * ls

    yp = x_prompt
    to_pos = lambda a: a.transpose(1, 0, 2).reshape(1, -1, a.shape[-1])
    ys = to_pos(x_sample)
    new_p, new_s = [], []
    n_c = bp + bs
    c_rows = -(-n_c // SUBLANES) * SUBLANES
    c_all = jnp.concatenate([c_sample, c_prompt, jnp.zeros((c_rows - n_c, D_MODEL), dt)], axis=0)
    for l in range(depth):
        w = _prep_weights(w_in[l], b_in[l], mlstm_norm_w[l], w_branch_m[l], w_branch_a[l], w_out[l],
                          ln1_g[l], ln1_b[l], w_up[l], b_up[l], conv_w[l], conv_b[l], w_down[l],
                          b_down[l], ln2_g[l], ln2_b[l])
        prm = jnp.stack([slopes, attn_sinks[l].astype(F32)]) * LOG2E
        mod = _ada(c_all, w_ada[l], b_ada[l])
        mod_s = mod[:bs].reshape(bs, N_MOD, D_MODEL)
        mod_p = mod[bs:bs + bp].reshape(bp, N_MOD, D_MODEL)
        mp_ = [mod_p[:, j:j + 1] for j in range(N_MOD)]
        ms_ = [jnp.tile(mod_s[:, j], (ls, 1))[None] for j in range(N_MOD)]

        hp, qk, v, g, qa, kb, ka, vat = _inproj(yp, mp_[0], mp_[1], w, tm_in, True)
        hm, cp, np_, mp = _mlstm_prompt(qk, v, g, jnp.zeros((bp, NH_M, DHV_M, DHK_M), dt),
                                        jnp.zeros((bp, NH_M, DHK_M), dt), jnp.zeros((bp, 1, LANES), dt),
                                        w["norm_w_col"])
        ha = _swa_prompt(prm, qa, kb, vat)
        x1p = _merge(yp, hp, mp_[2], hm, ha, w, tm_p, alpha)
        halo_p = SUBLANES
        yp, csp = _ffn(x1p, mp_[3], mp_[4], mp_[5], jnp.zeros((bp, halo_p, 2 * D_FF), dt),
                       w, tm_p, 1, alpha)
        p_k = ka[:, lp - WINDOW:].reshape(bp, WINDOW, NKV_A, HD_A)
        p_v = vat[:, :, lp - WINDOW:].reshape(bp, NKV_A, HD_A, WINDOW).transpose(0, 3, 1, 2)
        new_p.append((cp, np_, mp[:, 0, :NH_M], p_k, p_v, csp[:, halo_p - (CONV_W - 1):]))

        hs, qk, v, g, qa, kn, vn = _inproj(ys, ms_[0], ms_[1], w, ns, False)
        per_seq = lambda a: a.reshape(ls, bs, a.shape[-1])
        m0 = jnp.pad(state_mlstm_m[l], ((0, 0), (0, LANES - NH_M)))[:, None, :]
        hm, cs_, ns_, ms = _mlstm_sample(per_seq(qk), per_seq(v), per_seq(g), state_mlstm_C[l],
                                         state_mlstm_n[l], m0, w["norm_w"], SEQS_PER_STEP)
        to_t = lambda a: a.transpose(0, 2, 3, 1).reshape(bs, DKV_A, WINDOW)
        from_t = lambda a: a.reshape(bs, NKV_A, HD_A, WINDOW).transpose(0, 3, 1, 2)
        ha, kct, vct = _swa_sample(prm, per_seq(qa), per_seq(kn), per_seq(vn),
                                   to_t(cache_k_win[l]), to_t(cache_v_win[l]), SWA_SEQS_PER_STEP)
        x1s = _merge(ys, hs, ms_[2], hm.reshape(1, ns, DV_M), ha.reshape(1, ns, DQ_A), w, ns, alpha)
        ys, css = _ffn(x1s, ms_[3], ms_[4], ms_[5], to_pos(state_ffn_conv[l]), w, ns, bs, alpha)
        conv_s = css.reshape(CONV_W - 1, bs, 2 * D_FF).transpose(1, 0, 2)
        new_s.append((cs_, ns_, ms[:, 0, :NH_M], from_t(kct), from_t(vct), conv_s))

    p_state = [jnp.stack(a) for a in zip(*new_p)]
    s_state = [jnp.stack(a) for a in zip(*new_s)]
    y_sample = ys.reshape(ls, bs, D_MODEL).transpose(1, 0, 2)
    return (yp, y_sample, *p_state, *s_state)
```

```python
import functools

import jax
import jax.numpy as jnp
from jax import lax
from jax.experimental import pallas as pl
from jax.experimental.pallas import tpu as pltpu

F32 = jnp.float32
BF16 = jnp.bfloat16

D_MODEL = 1024
NH_M, DHK_M, DHV_M = 4, 128, 256
DQK_M, DV_M = NH_M * DHK_M, NH_M * DHV_M
NH_A, NKV_A, HD_A = 16, 4, 64
GROUP_A = NH_A // NKV_A
WINDOW = 128
DQ_A, DKV_A = NH_A * HD_A, NKV_A * HD_A
D_FF = 2816
CONV_W = 3
N_MOD = 6
LN_EPS = 1e-5
CHUNK = 128
NEG = -1e30
LOG2E = 1.4426950408889634
LANES = 128
SUBLANES = 8
VMEM_LIMIT = 56 * 1024 * 1024
SEQS_PER_STEP = 16
SWA_SEQS_PER_STEP = 8


def _ln(x):
    mu = jnp.mean(x, axis=-1, keepdims=True)
    xc = x - mu
    var = jnp.mean(xc * xc, axis=-1, keepdims=True)
    return xc * lax.rsqrt(var + LN_EPS)


def _dot(a, b):
    return jnp.dot(a, b, preferred_element_type=F32)


def _dot_nt(a, b):
    return lax.dot_general(a, b, (((1,), (1,)), ((), ())), preferred_element_type=F32)


def _dot_tn(a, b):
    return lax.dot_general(a, b, (((0,), (0,)), ((), ())), preferred_element_type=F32)


def _const_spec(shape):
    nd = len(shape)
    return pl.BlockSpec(shape, lambda *_: (0,) * nd, pipeline_mode=pl.Buffered(1))


def _params(n_grid):
    return pltpu.CompilerParams(dimension_semantics=("arbitrary",) * n_grid,
                                vmem_limit_bytes=VMEM_LIMIT)


def _first_step():
    return (pl.program_id(0) == 0) & (pl.program_id(1) == 0)


def _ada_body(c_ref, w_ref, b_ref, o_ref):
    c = c_ref[...]
    s = (c * jax.nn.sigmoid(c)).astype(BF16)
    o_ref[...] = _dot(s, w_ref[...].astype(BF16)) + b_ref[...]


def _ada(c, w_ada, b_ada):
    rows = c.shape[0]
    n_out = w_ada.shape[1]
    bn = 3072
    assert n_out % bn == 0
    return pl.pallas_call(
        _ada_body,
        out_shape=jax.ShapeDtypeStruct((rows, n_out), F32),
        grid=(n_out // bn,),
        in_specs=[pl.BlockSpec((rows, D_MODEL), lambda j: (0, 0)),
                  pl.BlockSpec((D_MODEL, bn), lambda j: (0, j)),
                  pl.BlockSpec((1, bn), lambda j: (0, j))],
        out_specs=pl.BlockSpec((rows, bn), lambda j: (0, j)),
        compiler_params=_params(1),
        name="ada",
    )(c, w_ada, b_ada.reshape(1, n_out))


def _inproj_body(prompt, x_ref, sh_ref, sc_ref, wqk_ref, bqk_ref, wvm_ref, bvm_ref, wg_ref, bg_ref,
                 wqa_ref, bqa_ref, wk_ref, bk_ref, wv_ref, bv_ref, h_ref, qk_ref, v_ref, g_ref, qa_ref, *rest):
    h = (_ln(x_ref[...]) * (1.0 + sc_ref[...]) + sh_ref[...]).astype(BF16)
    h_ref[...] = h

    def proj(w_ref, b_ref, lo, n):
        return _dot(h, w_ref[:, lo:lo + n]) + b_ref[:, lo:lo + n]

    qk_ref[:, :DQK_M] = (proj(wqk_ref, bqk_ref, 0, DQK_M) * DHK_M ** -0.5).astype(qk_ref.dtype)
    qk_ref[:, DQK_M:] = proj(wqk_ref, bqk_ref, DQK_M, DQK_M).astype(qk_ref.dtype)
    if prompt:
        v_ref[...] = (_dot_nt(wvm_ref[...], h) + bvm_ref[...]).astype(v_ref.dtype)
    else:
        v_ref[...] = proj(wvm_ref, bvm_ref, 0, DV_M).astype(v_ref.dtype)
    g_ref[...] = proj(wg_ref, bg_ref, 0, LANES)
    qa_ref[...] = (proj(wqa_ref, bqa_ref, 0, DQ_A) * (HD_A ** -0.5 * LOG2E)).astype(qa_ref.dtype)
    ka = proj(wk_ref, bk_ref, 0, DKV_A)
    if prompt:
        kb_ref, ka_ref, vat_ref = rest
        kb_ref[...] = ka.astype(BF16)
        ka_ref[...] = ka
        vat_ref[...] = _dot_nt(wv_ref[...], h) + bv_ref[...]
    else:
        ka_ref, va_ref = rest
        ka_ref[...] = ka
        va_ref[...] = _dot(h, wv_ref[...]) + bv_ref[...]


def _tok_spec(tm, n):
    return pl.BlockSpec((None, tm, n), lambda b, i: (b, i, 0))


def _mod_spec(rows, tm):
    if rows == 1:
        return pl.BlockSpec((None, 1, D_MODEL), lambda b, i: (b, 0, 0))
    return pl.BlockSpec((None, tm, D_MODEL), lambda b, i: (b, i, 0))


def _inproj(x, sh, sc, w, tm, prompt):
    bsz, seq, _ = x.shape
    mrows = sh.shape[1]
    act = BF16 if prompt else F32
    t_spec = lambda n: pl.BlockSpec((None, n, tm), lambda b, i: (b, 0, i))
    outs = [((bsz, seq, D_MODEL), BF16, _tok_spec(tm, D_MODEL)),
            ((bsz, seq, 2 * DQK_M), act, _tok_spec(tm, 2 * DQK_M)),
            ((bsz, DV_M, seq), act, t_spec(DV_M)) if prompt else ((bsz, seq, DV_M), act, _tok_spec(tm, DV_M)),
            ((bsz, seq, LANES), F32, _tok_spec(tm, LANES)),
            ((bsz, seq, DQ_A), act, _tok_spec(tm, DQ_A))]
    if prompt:
        wvm, bvm, wv, bv = w["wvm_t"], w["bvm_col"], w["wv_t"], w["bv_col"]
        outs += [((bsz, seq, DKV_A), BF16, _tok_spec(tm, DKV_A)),
                 ((bsz, seq, DKV_A), F32, _tok_spec(tm, DKV_A)),
                 ((bsz, DKV_A, seq), F32, t_spec(DKV_A))]
    else:
        wvm, bvm, wv, bv = w["wvm"], w["bvm_row"], w["wv"], w["bv_row"]
        outs += [((bsz, seq, DKV_A), F32, _tok_spec(tm, DKV_A)),
                 ((bsz, seq, DKV_A), F32, _tok_spec(tm, DKV_A))]
    weights = [w["wqk"], w["bqk"], wvm, bvm, w["wg"], w["bg"], w["wqa"], w["bqa"], w["wk"], w["bk"],
               wv, bv]
    return pl.pallas_call(
        functools.partial(_inproj_body, prompt),
        out_shape=[jax.ShapeDtypeStruct(s, dt) for s, dt, _ in outs],
        grid=(bsz, seq // tm),
        in_specs=[_tok_spec(tm, D_MODEL), _mod_spec(mrows, tm), _mod_spec(mrows, tm)]
                 + [_const_spec(a.shape) for a in weights],
        out_specs=[spec for _, _, spec in outs],
        compiler_params=_params(2),
        name="inproj",
    )(x, sh, sc, *weights)


def _scan_rows(x, op, rows):
    row = lax.broadcasted_iota(jnp.int32, x.shape, 0)
    d = 1
    while d < rows:
        shifted = pltpu.roll(x, d, axis=0)
        x = jnp.where(row >= d, op(x, shifted), x)
        d *= 2
    return x


def _mlstm_sample_body(t_in, nb, qk_ref, v_ref, g_ref, c0_ref, n0_ref, m0_ref, nw_ref,
                       h_ref, c_ref, n_ref, m_ref, *pads):
    t = SUBLANES
    c_ref[...] = c0_ref[...]
    n_ref[...] = n0_ref[...]
    m_ref[...] = m0_ref[...]

    @pl.when(_first_step())
    def _():
        for p in pads:
            p[...] = jnp.zeros(p.shape, p.dtype)
    for src, dst in zip((qk_ref, v_ref, g_ref), pads):
        for s in range(nb):
            dst[s, pl.ds(0, t_in), :] = src[:, s, :]
    qk_src, v_src, g_src = pads

    row = lax.broadcasted_iota(jnp.int32, (t, LANES), 0)
    lane = lax.broadcasted_iota(jnp.int32, (t, LANES), 1)
    lane1 = lax.broadcasted_iota(jnp.int32, (1, LANES), 1)
    r2 = lax.broadcasted_iota(jnp.int32, (t, t), 0)
    c2 = lax.broadcasted_iota(jnp.int32, (t, t), 1)
    causal = c2 <= r2
    pad_gate = jnp.where(lane < NH_M, NEG, -NEG)

    seqs = []
    for s in range(nb):
        ga = _gate_algebra(jnp.where(row < t_in, g_src[s], pad_gate), m_ref[s], t)
        ga["gt"] = ga["gg"].T
        m_ref[s] = jnp.where(lane1 < NH_M, ga["m_new"], 0.0)
        seqs.append(ga)

    units = [(s, hd) for s in range(nb) for hd in range(NH_M)]
    nu = len(units)
    col = lambda name, s, hd: seqs[s][name][:, hd:hd + 1]
    q = [qk_src[s, :, hd * DHK_M:(hd + 1) * DHK_M].astype(BF16) for s, hd in units]
    k = [qk_src[s, :, DQK_M + hd * DHK_M:DQK_M + (hd + 1) * DHK_M].astype(BF16) for s, hd in units]
    v = [v_src[s, :, hd * DHV_M:(hd + 1) * DHV_M].astype(BF16) for s, hd in units]
    c_old = [c_ref[s, hd] for s, hd in units]
    n_old = [n_ref[s, hd:hd + 1, :] for s, hd in units]

    sqk = [_dot_nt(q[u], k[u]) for u in range(nu)]
    dexp = [jnp.exp(jnp.where(causal, col("eb", s, hd) + seqs[s]["gt"][hd:hd + 1, :], NEG))
            for s, hd in units]
    smat = [sqk[u] * dexp[u] for u in range(nu)]
    intra = [_dot(smat[u].astype(BF16), v[u]) for u in range(nu)]
    inter = [_dot_nt(q[u], c_old[u].astype(BF16)) for u in range(nu)]
    qn = [jnp.sum(q[u].astype(F32) * n_old[u].astype(BF16).astype(F32), axis=-1, keepdims=True)
          for u in range(nu)]
    den = [jnp.sum(smat[u], axis=-1, keepdims=True) + col("a_in", s, hd) * qn[u]
           for u, (s, hd) in enumerate(units)]
    hh = [(intra[u] + col("a_in", s, hd) * inter[u])
          / jnp.maximum(jnp.abs(den[u]), col("lowb", s, hd)) for u, (s, hd) in enumerate(units)]
    hn = [_ln(hh[u]) * nw_ref[:, hd * DHV_M:(hd + 1) * DHV_M] for u, (s, hd) in enumerate(units)]
    for u, (s, hd) in enumerate(units):
        h_ref[:, s, hd * DHV_M:(hd + 1) * DHV_M] = hn[u][:t_in]

    kw = [k[u].astype(F32) * col("ws", s, hd) for u, (s, hd) in enumerate(units)]
    upd = [_dot_tn(v[u], kw[u].astype(BF16)) for u in range(nu)]
    for u, (s, hd) in enumerate(units):
        dec = seqs[s]["decay"][:, hd:hd + 1]
        c_ref[s, hd] = dec * c_old[u] + upd[u]
        n_ref[s, hd:hd + 1, :] = dec * n_old[u] + jnp.sum(kw[u], axis=0, keepdims=True)


def _mlstm_sample(qk, v, g, c0, n0, m0, norm_w, nb):
    t_in, nseq, _ = qk.shape
    assert t_in <= SUBLANES
    blk = lambda n: pl.BlockSpec((t_in, nb, n), lambda o, c: (0, o, 0))
    st_c = pl.BlockSpec((nb, NH_M, DHV_M, DHK_M), lambda o, c: (o, 0, 0, 0))
    st_n = pl.BlockSpec((nb, NH_M, DHK_M), lambda o, c: (o, 0, 0))
    st_m = pl.BlockSpec((nb, 1, LANES), lambda o, c: (o, 0, 0))
    return pl.pallas_call(
        functools.partial(_mlstm_sample_body, t_in, nb),
        out_shape=[jax.ShapeDtypeStruct((t_in, nseq, DV_M), F32),
                   jax.ShapeDtypeStruct((nseq, NH_M, DHV_M, DHK_M), F32),
                   jax.ShapeDtypeStruct((nseq, NH_M, DHK_M), F32),
                   jax.ShapeDtypeStruct((nseq, 1, LANES), F32)],
        grid=(nseq // nb, 1),
        in_specs=[blk(2 * DQK_M), blk(DV_M), blk(LANES), st_c, st_n, st_m,
                  pl.BlockSpec((1, DV_M), lambda o, c: (0, 0))],
        out_specs=[blk(DV_M), st_c, st_n, st_m],
        scratch_shapes=[pltpu.VMEM((nb, SUBLANES, 2 * DQK_M), F32), pltpu.VMEM((nb, SUBLANES, DV_M), F32),
                        pltpu.VMEM((nb, SUBLANES, LANES), F32)],
        compiler_params=_params(2),
        name="mlstm_sample",
    )(qk, v, g, c0, n0, m0, norm_w)


def _gate_algebra(g, m_prev, t):
    b = pltpu.roll(_scan_rows(jax.nn.log_sigmoid(g), jnp.add, t), LANES - NH_M, axis=1)
    gg = g - b
    gmx = _scan_rows(gg, jnp.maximum, t)
    a = b + m_prev
    mt = jnp.maximum(a, b + gmx)
    b_last, gmx_last = b[t - 1:t, :], gmx[t - 1:t, :]
    m_new = jnp.maximum(b_last + m_prev, b_last + gmx_last)
    return dict(gg=gg, eb=b - mt, a_in=jnp.exp(a - mt), lowb=jnp.exp(-mt),
                ws=jnp.exp(b_last + gg - m_new), decay=jnp.exp(b_last + m_prev - m_new), m_new=m_new)


MLSTM_CHUNKS_PER_STEP = 4


def _mlstm_prompt_body(nb, qk_ref, vt_ref, g_ref, c0_ref, n0_ref, m0_ref, nwc_ref,
                       h_ref, c_ref, n_ref, m_ref, nwb):
    @pl.when(pl.program_id(1) == 0)
    def _():
        c_ref[...] = c0_ref[...]
        n_ref[...] = n0_ref[...]
        m_ref[...] = m0_ref[...]

    @pl.when(_first_step())
    def _():
        for hd in range(NH_M):
            nwb[hd] = jnp.broadcast_to(nwc_ref[hd * DHV_M:(hd + 1) * DHV_M, :], (DHV_M, CHUNK))

    for ci in range(MLSTM_CHUNKS_PER_STEP):
        _mlstm_prompt_chunk(nb, slice(ci * CHUNK, (ci + 1) * CHUNK), qk_ref, vt_ref, g_ref, nwb,
                            h_ref, c_ref, n_ref, m_ref)


def _mlstm_prompt_chunk(nb, rows, qk_ref, vt_ref, g_ref, nwb, h_ref, c_ref, n_ref, m_ref):
    t = CHUNK
    lane1 = lax.broadcasted_iota(jnp.int32, (1, LANES), 1)
    r2 = lax.broadcasted_iota(jnp.int32, (t, t), 0)
    c2 = lax.broadcasted_iota(jnp.int32, (t, t), 1)
    causal = r2 <= c2

    seqs = []
    for s in range(nb):
        ga = _gate_algebra(g_ref[s, rows, :], m_ref[s], t)
        m_ref[s] = jnp.where(lane1 < NH_M, ga["m_new"], 0.0)
        for name in ("eb", "a_in", "lowb"):
            ga[name + "_t"] = ga[name].T
        seqs.append(ga)

    units = [(s, hd) for s in range(nb) for hd in range(NH_M)]
    nu = len(units)
    rowv = lambda name, s, hd: seqs[s][name + "_t"][hd:hd + 1, :]
    q = [qk_ref[s, rows, hd * DHK_M:(hd + 1) * DHK_M] for s, hd in units]
    k = [qk_ref[s, rows, DQK_M + hd * DHK_M:DQK_M + (hd + 1) * DHK_M] for s, hd in units]
    vt = [vt_ref[s, hd * DHV_M:(hd + 1) * DHV_M, rows] for s, hd in units]
    c_old = [c_ref[s, hd] for s, hd in units]
    n_old = [n_ref[s, hd:hd + 1, :] for s, hd in units]

    skq = [_dot_nt(k[u], q[u]) for u in range(nu)]
    dexp = [jnp.exp(jnp.where(causal, seqs[s]["gg"][:, hd:hd + 1] + rowv("eb", s, hd), NEG))
            for s, hd in units]
    smat = [skq[u] * dexp[u] for u in range(nu)]
    intra = [_dot(vt[u], smat[u].astype(BF16)) for u in range(nu)]
    inter = [_dot_nt(c_old[u].astype(BF16), q[u]) for u in range(nu)]
    qn = [_dot_nt(jnp.broadcast_to(n_old[u], (SUBLANES, DHK_M)).astype(BF16), q[u])[0:1, :]
          for u in range(nu)]
    den = [jnp.sum(smat[u], axis=0, keepdims=True) + rowv("a_in", s, hd) * qn[u]
           for u, (s, hd) in enumerate(units)]
    inv = [1.0 / jnp.maximum(jnp.abs(den[u]), rowv("lowb", s, hd)) for u, (s, hd) in enumerate(units)]
    hh = [(intra[u] + rowv("a_in", s, hd) * inter[u]) * inv[u] for u, (s, hd) in enumerate(units)]
    mu = [jnp.mean(hh[u], axis=0, keepdims=True) for u in range(nu)]
    xc = [hh[u] - mu[u] for u in range(nu)]
    var = [jnp.mean(xc[u] * xc[u], axis=0, keepdims=True) for u in range(nu)]
    hn = [xc[u] * lax.rsqrt(var[u] + LN_EPS) * nwb[hd] for u, (s, hd) in enumerate(units)]
    for u, (s, hd) in enumerate(units):
        h_ref[s, rows, hd * DHV_M:(hd + 1) * DHV_M] = hn[u].T

    kw = [(k[u].astype(F32) * seqs[s]["ws"][:, hd:hd + 1]).astype(BF16) for u, (s, hd) in enumerate(units)]
    upd = [_dot(vt[u], kw[u]) for u in range(nu)]
    nupd = [_dot(jnp.ones((SUBLANES, t), BF16), kw[u])[0:1, :] for u in range(nu)]
    for u, (s, hd) in enumerate(units):
        dec = seqs[s]["decay"][:, hd:hd + 1]
        c_ref[s, hd] = dec * c_old[u] + upd[u]
        n_ref[s, hd:hd + 1, :] = dec * n_old[u] + nupd[u]


def _mlstm_prompt(qk, vt, g, c0, n0, m0, norm_w_col):
    nb, seq, _ = qk.shape
    ts = MLSTM_CHUNKS_PER_STEP * CHUNK
    tok = lambda n: pl.BlockSpec((nb, ts, n), lambda o, c: (0, c, 0))
    st_c = pl.BlockSpec((nb, NH_M, DHV_M, DHK_M), lambda o, c: (0, 0, 0, 0))
    st_n = pl.BlockSpec((nb, NH_M, DHK_M), lambda o, c: (0, 0, 0))
    st_m = pl.BlockSpec((nb, 1, LANES), lambda o, c: (0, 0, 0))
    return pl.pallas_call(
        functools.partial(_mlstm_prompt_body, nb),
        out_shape=[jax.ShapeDtypeStruct((nb, seq, DV_M), F32),
                   jax.ShapeDtypeStruct((nb, NH_M, DHV_M, DHK_M), F32),
                   jax.ShapeDtypeStruct((nb, NH_M, DHK_M), F32),
                   jax.ShapeDtypeStruct((nb, 1, LANES), F32)],
        grid=(1, seq // ts),
        in_specs=[tok(2 * DQK_M), pl.BlockSpec((nb, DV_M, ts), lambda o, c: (0, 0, c)), tok(LANES),
                  st_c, st_n, st_m, pl.BlockSpec((DV_M, 1), lambda o, c: (0, 0))],
        out_specs=[tok(DV_M), st_c, st_n, st_m],
        scratch_shapes=[pltpu.VMEM((NH_M, DHV_M, CHUNK), F32)],
        compiler_params=_params(2),
        name="mlstm_prompt",
    )(qk, vt, g, c0, n0, m0, norm_w_col)


SWA_BLOCKS_PER_STEP = 16


def _swa_prompt_body(prm_ref, q_ref, kc_ref, kp_ref, vc_ref, vp_ref, o_ref, tbl):
    row = lax.broadcasted_iota(jnp.int32, (WINDOW, WINDOW), 0)
    col = lax.broadcasted_iota(jnp.int32, (WINDOW, WINDOW), 1)
    tri = row <= col

    @pl.when(_first_step())
    def _():
        dist = jnp.where(tri, col - row, col - row + WINDOW).astype(F32)
        for hd in range(NH_A):
            bias = prm_ref[0, hd] * dist
            tbl[0, hd] = bias
            tbl[1, hd] = bias + jnp.where(tri, 0.0, -NEG)

    for b in range(SWA_BLOCKS_PER_STEP):
        rows = pl.ds(b * WINDOW, WINDOW)
        which = jnp.where(pl.program_id(1) == 0, 1, 0) if b == 0 else 0
        k_prev = kp_ref if b == 0 else kc_ref.at[pl.ds((b - 1) * WINDOW, WINDOW), :]
        v_prev = vp_ref if b == 0 else vc_ref.at[:, pl.ds((b - 1) * WINDOW, WINDOW)]
        _swa_prompt_block(which, prm_ref, q_ref.at[rows, :], kc_ref.at[rows, :], k_prev,
                          vc_ref.at[:, rows], v_prev, o_ref.at[rows, :], tbl)


def _swa_prompt_block(which, prm_ref, q_ref, kc_ref, kp_ref, vc_ref, vp_ref, o_ref, tbl):
    row = lax.broadcasted_iota(jnp.int32, (WINDOW, WINDOW), 0)
    col = lax.broadcasted_iota(jnp.int32, (WINDOW, WINDOW), 1)
    tri = row <= col
    lo = col < HD_A
    zb = jnp.zeros((WINDOW, LANES), BF16)

    def placed(k_ref, kv):
        c, par = kv // 2, kv % 2
        own = k_ref[:, c * LANES:(c + 1) * LANES]
        swp = pltpu.roll(own, HD_A, axis=1)
        if par == 0:
            return jnp.where(lo, own, zb), jnp.where(lo, zb, swp)
        return jnp.where(lo, swp, zb), jnp.where(lo, zb, own)

    st = []
    for kv in range(NKV_A):
        lhs = jnp.concatenate([*placed(kc_ref, kv), *placed(kp_ref, kv)], axis=0)
        qg = jnp.concatenate([q_ref[:, (2 * kv) * LANES:(2 * kv + 1) * LANES],
                              q_ref[:, (2 * kv + 1) * LANES:(2 * kv + 2) * LANES]], axis=0)
        st.append(_dot_nt(lhs, qg))
    tiles = []
    for hd in range(NH_A):
        kv, a, par = hd // GROUP_A, (hd % GROUP_A) // 2, hd % 2
        cols = slice(a * WINDOW, (a + 1) * WINDOW)
        tiles.append(jnp.where(tri, st[kv][par * WINDOW:(par + 1) * WINDOW, cols],
                               st[kv][(2 + par) * WINDOW:(3 + par) * WINDOW, cols]))
    sc = jnp.concatenate(tiles, axis=0).reshape(NH_A, WINDOW, WINDOW) - tbl[which]
    sink = jnp.concatenate([jnp.full((1, 1, WINDOW), prm_ref[1, hd], F32) for hd in range(NH_A)], axis=0)
    mx = jnp.maximum(jnp.max(sc, axis=1, keepdims=True), sink)
    p = jnp.exp2(sc - mx)
    den = jnp.sum(p, axis=1, keepdims=True) + jnp.exp2(sink - mx)
    pn = p * (1.0 / den)

    zero = jnp.zeros((WINDOW, WINDOW), F32)
    z64 = jnp.zeros((HD_A, WINDOW), BF16)
    outs = []
    for kv in range(NKV_A):
        cols = []
        for a in range(2):
            pe, po = pn[kv * GROUP_A + 2 * a], pn[kv * GROUP_A + 2 * a + 1]
            cols.append(jnp.concatenate([jnp.where(tri, pe, zero), jnp.where(tri, po, zero),
                                         jnp.where(tri, zero, pe), jnp.where(tri, zero, po)],
                                        axis=0).astype(BF16))
        pt = jnp.concatenate(cols, axis=1)
        vc = vc_ref[kv * HD_A:(kv + 1) * HD_A, :].astype(BF16)
        vp = vp_ref[kv * HD_A:(kv + 1) * HD_A, :].astype(BF16)
        vt = jnp.concatenate([jnp.concatenate([vc, z64], axis=0), jnp.concatenate([z64, vc], axis=0),
                              jnp.concatenate([vp, z64], axis=0), jnp.concatenate([z64, vp], axis=0)],
                             axis=1)
        ot = _dot(vt, pt)
        outs += [ot[:, :WINDOW].T, ot[:, WINDOW:].T]
    o_ref[...] = jnp.concatenate(outs, axis=1).astype(o_ref.dtype)


def _swa_prompt(prm, q, kb, vat):
    bsz, seq, _ = q.shape
    tq = SWA_BLOCKS_PER_STEP * WINDOW
    prev = lambda i: jnp.maximum(i * SWA_BLOCKS_PER_STEP - 1, 0)
    return pl.pallas_call(
        _swa_prompt_body,
        out_shape=jax.ShapeDtypeStruct((bsz, seq, DQ_A), BF16),
        grid=(bsz, seq // tq),
        in_specs=[pl.BlockSpec(memory_space=pltpu.SMEM),
                  pl.BlockSpec((None, tq, DQ_A), lambda b, i: (b, i, 0)),
                  pl.BlockSpec((None, tq, DKV_A), lambda b, i: (b, i, 0)),
                  pl.BlockSpec((None, WINDOW, DKV_A), lambda b, i: (b, prev(i), 0)),
                  pl.BlockSpec((None, DKV_A, tq), lambda b, i: (b, 0, i)),
                  pl.BlockSpec((None, DKV_A, WINDOW), lambda b, i: (b, 0, prev(i)))],
        out_specs=pl.BlockSpec((None, tq, DQ_A), lambda b, i: (b, i, 0)),
        scratch_shapes=[pltpu.VMEM((2, NH_A, WINDOW, WINDOW), F32)],
        compiler_params=_params(2),
        name="swa_prompt",
    )(prm, q, kb, kb, vat, vat)


def _half_mask(shape, half):
    lane = lax.broadcasted_iota(jnp.int32, shape, 1)
    return lane < HD_A if half == 0 else lane >= HD_A


def _swa_sample_body(t_in, nb, prm_ref, q_ref, kn_ref, vn_ref, kct_ref, vct_ref,
                     o_ref, kco_ref, vco_ref, tbl, q8, kn_pad, vn_pad):
    tq = SUBLANES
    rows = NH_A * tq
    row = lax.broadcasted_iota(jnp.int32, (rows, WINDOW), 0)
    col = lax.broadcasted_iota(jnp.int32, (rows, WINDOW), 1)
    tri = col <= (row & (tq - 1))

    @pl.when(_first_step())
    def _():
        r8 = lax.broadcasted_iota(jnp.int32, (tq, WINDOW), 0)
        c8 = lax.broadcasted_iota(jnp.int32, (tq, WINDOW), 1)
        dist = jnp.where(c8 <= r8, r8 - c8, r8 - c8 + WINDOW).astype(F32)
        for hd in range(NH_A):
            tbl[pl.ds(hd * tq, tq), :] = prm_ref[0, hd] * dist
        for p in (q8, kn_pad, vn_pad):
            p[...] = jnp.zeros(p.shape, p.dtype)

    for s in range(nb):
        q8[s, pl.ds(0, t_in), :] = q_ref[:, s, :]
        kn_pad[s, pl.ds(0, t_in), :] = kn_ref[:, s, :]
        vn_pad[s, pl.ds(0, t_in), :] = vn_ref[:, s, :]
    sink_col = jnp.concatenate([jnp.full((tq, 1), prm_ref[1, hd], F32) for hd in range(NH_A)], axis=0)
    z8 = jnp.zeros((tq, LANES), F32)
    n_chunk = NKV_A // 2
    heads_per_chunk = NH_A // n_chunk

    def place(piece, src_half, dst_half):
        if src_half != dst_half:
            piece = pltpu.roll(piece, HD_A, axis=1)
        return jnp.where(_half_mask(piece.shape, dst_half), piece, z8)

    qexp = []
    for s in range(nb):
        per_c = []
        for c in range(n_chunk):
            pieces = []
            for hl in range(heads_per_chunk):
                hd = c * heads_per_chunk + hl
                pieces.append(place(q8[s, :, (hd // 2) * LANES:(hd // 2 + 1) * LANES],
                                    hd % 2, hl // GROUP_A))
            per_c.append(jnp.concatenate(pieces, axis=0).astype(BF16))
        qexp.append(per_c)
    csl = lambda c: slice(c * LANES, (c + 1) * LANES)
    s_prev = [[_dot(qexp[s][c], kct_ref[s, csl(c), :].astype(BF16)) for c in range(n_chunk)]
              for s in range(nb)]
    s_cur = [[_dot_nt(qexp[s][c], kn_pad[s, :, csl(c)].astype(BF16)) for c in range(n_chunk)]
             for s in range(nb)]
    sc = [jnp.where(tri, jnp.concatenate(s_cur[s], axis=0), jnp.concatenate(s_prev[s], axis=0)) - tbl[...]
          for s in range(nb)]
    mx = [jnp.maximum(jnp.max(sc[s], axis=-1, keepdims=True), sink_col) for s in range(nb)]
    p = [jnp.exp2(sc[s] - mx[s]) for s in range(nb)]
    den = [jnp.sum(p[s], axis=-1, keepdims=True) + jnp.exp2(sink_col - mx[s]) for s in range(nb)]
    pn = [p[s] * (1.0 / den[s]) for s in range(nb)]
    zero = jnp.zeros((rows, WINDOW), F32)
    pc = [jnp.where(tri, pn[s], zero).astype(BF16) for s in range(nb)]
    pp = [jnp.where(tri, zero, pn[s]).astype(BF16) for s in range(nb)]
    half_rows = heads_per_chunk * tq
    oc = [[_dot_nt(pp[s][c * half_rows:(c + 1) * half_rows], vct_ref[s, csl(c), :].astype(BF16))
           + _dot(pc[s][c * half_rows:(c + 1) * half_rows], vn_pad[s, :, csl(c)].astype(BF16))
           for c in range(n_chunk)] for s in range(nb)]
    for s in range(nb):
        chunks = []
        for pch in range(NH_A // 2):
            acc = None
            for hd in (2 * pch, 2 * pch + 1):
                c, hl = hd // heads_per_chunk, hd % heads_per_chunk
                piece = place(oc[s][c][hl * tq:(hl + 1) * tq, :], hl // GROUP_A, hd % 2)
                acc = piece if acc is None else acc + piece
            chunks.append(acc)
        o_ref[:, s, :] = jnp.concatenate(chunks, axis=1)[:t_in]

    lane = lax.broadcasted_iota(jnp.int32, (DKV_A, WINDOW), 1)
    for new_pad, old_ref, out_ref in ((kn_pad, kct_ref, kco_ref), (vn_pad, vct_ref, vco_ref)):
        for s in range(nb):
            merged = jnp.where(lane < t_in, new_pad[s].T, old_ref[s])
            out_ref[s] = pltpu.roll(merged, WINDOW - t_in, axis=1)


def _swa_sample(prm, q, kn, vn, kct, vct, nb):
    t_in, nseq, _ = q.shape
    assert t_in <= SUBLANES
    cur = lambda n: pl.BlockSpec((t_in, nb, n), lambda o, i: (0, o, 0))
    win = pl.BlockSpec((nb, DKV_A, WINDOW), lambda o, i: (o, 0, 0))
    return pl.pallas_call(
        functools.partial(_swa_sample_body, t_in, nb),
        out_shape=[jax.ShapeDtypeStruct((t_in, nseq, DQ_A), F32),
                   jax.ShapeDtypeStruct((nseq, DKV_A, WINDOW), F32),
                   jax.ShapeDtypeStruct((nseq, DKV_A, WINDOW), F32)],
        grid=(nseq // nb, 1),
        in_specs=[pl.BlockSpec(memory_space=pltpu.SMEM), cur(DQ_A), cur(DKV_A), cur(DKV_A), win, win],
        out_specs=[cur(DQ_A), win, win],
        scratch_shapes=[pltpu.VMEM((NH_A * SUBLANES, WINDOW), F32),
                        pltpu.VMEM((nb, SUBLANES, DQ_A), F32),
                        pltpu.VMEM((nb, WINDOW, DKV_A), F32),
                        pltpu.VMEM((nb, WINDOW, DKV_A), F32)],
        compiler_params=_params(2),
        name="swa_sample",
    )(prm, q, kn, vn, kct, vct)


def _merge_body(alpha, x_ref, h_ref, g1_ref, hm_ref, ha_ref, wgt_ref, bgt_ref,
                wbm_ref, wba_ref, wo_ref, lg_ref, lb_ref, o_ref):
    nsub = 2
    sub = x_ref.shape[0] // nsub
    rows = [pl.ds(r * sub, sub) for r in range(nsub)]
    mod = lambda ref, r: ref[...] if ref.shape[0] == 1 else ref[rows[r], :]
    gate = lambda g, j: g[:, j * D_MODEL:(j + 1) * D_MODEL]

    x = [x_ref[rows[r], :] for r in range(nsub)]
    h = [h_ref[rows[r], :] for r in range(nsub)]
    g = [jax.nn.sigmoid(_dot(h[r], wgt_ref[...]) + bgt_ref[...]) for r in range(nsub)]
    hm = [(hm_ref[rows[r], :] * gate(g[r], 0)).astype(BF16) for r in range(nsub)]
    bm = [_dot(hm[r], wbm_ref[...]) for r in range(nsub)]
    ba = [_dot(ha_ref[rows[r], :].astype(BF16), wba_ref[...]) for r in range(nsub)]
    merged = [(gate(g[r], 1) * bm[r] + gate(g[r], 2) * ba[r]).astype(BF16) for r in range(nsub)]
    mo = [_dot(merged[r], wo_ref[...]) for r in range(nsub)]
    for r in range(nsub):
        o_ref[rows[r], :] = _ln(alpha * x[r] + mod(g1_ref, r) * mo[r]) * lg_ref[...] + lb_ref[...]


def _merge(x, h, g1, hm, ha, w, tm, alpha):
    bsz, seq, _ = x.shape
    mrows = g1.shape[1]
    weights = [w["wgates"], w["bgates"], w["wbm"], w["wba"], w["wo"],
               w["ln1_g"], w["ln1_b"]]
    return pl.pallas_call(
        functools.partial(_merge_body, alpha),
        out_shape=jax.ShapeDtypeStruct((bsz, seq, D_MODEL), F32),
        grid=(bsz, seq // tm),
        in_specs=[_tok_spec(tm, D_MODEL), _tok_spec(tm, D_MODEL), _mod_spec(mrows, tm),
                  _tok_spec(tm, DV_M), _tok_spec(tm, DQ_A)]
                 + [_const_spec(a.shape) for a in weights],
        out_specs=_tok_spec(tm, D_MODEL),
        compiler_params=_params(2),
        name="merge",
    )(x, h, g1, hm, ha, *weights)


def _ffn_body(alpha, tm, stride, halo, x_ref, sh_ref, sc_ref, g2_ref, cb0_ref, wup_ref, bup_ref,
              cw_ref, cbias_ref, wdn_ref, bdn_ref, lg_ref, lb_ref, o_ref, cs_ref, ubuf, act):
    @pl.when(pl.program_id(1) == 0)
    def _():
        cs_ref[...] = cb0_ref[...]

    nsub = 2
    sub = tm // nsub
    rows = [pl.ds(r * sub, sub) for r in range(nsub)]
    mod = lambda ref, r: ref[...] if ref.shape[0] == 1 else ref[rows[r], :]
    x = [x_ref[rows[r], :] for r in range(nsub)]
    h = [(_ln(x[r]) * (1.0 + mod(sc_ref, r)) + mod(sh_ref, r)).astype(BF16) for r in range(nsub)]
    ys = [[None, None] for _ in range(nsub)]
    for half in range(2):
        cols = slice(half * D_FF, (half + 1) * D_FF)
        u = [_dot(h[r], wup_ref[:, cols]) + bup_ref[:, cols] for r in range(nsub)]
        ubuf[pl.ds(0, halo), :] = cs_ref[:, cols]
        for r in range(nsub):
            ubuf[pl.ds(halo + r * sub, sub), :] = u[r]
        cs_ref[:, cols] = ubuf[pl.ds(tm, halo), :]
        for r in range(nsub):
            y = cbias_ref[:, cols] + u[r] * cw_ref[CONV_W - 1:CONV_W, cols]
            for j in range(CONV_W - 1):
                tap = ubuf[pl.ds(halo + r * sub - (CONV_W - 1 - j) * stride, sub), :]
                y = y + tap * cw_ref[j:j + 1, cols]
            ys[r][half] = y
    for r in range(nsub):
        act[rows[r], :] = (jax.nn.gelu(ys[r][0]) * ys[r][1]).astype(BF16)
    f = [_dot(act[rows[r], :], wdn_ref[...]) + bdn_ref[...] for r in range(nsub)]
    for r in range(nsub):
        o_ref[rows[r], :] = _ln(alpha * x[r] + mod(g2_ref, r) * f[r]) * lg_ref[...] + lb_ref[...]


def _ffn(x, sh, sc, g2, cb0, w, tm, stride, alpha):
    bsz, seq, _ = x.shape
    mrows = sh.shape[1]
    halo = cb0.shape[1]
    cs = pl.BlockSpec((None, halo, 2 * D_FF), lambda b, i: (b, 0, 0))
    weights = [w["wup"], w["bup"], w["cw"], w["cbias"], w["wdn"], w["bdn"], w["ln2_g"], w["ln2_b"]]
    return pl.pallas_call(
        functools.partial(_ffn_body, alpha, tm, stride, halo),
        out_shape=[jax.ShapeDtypeStruct((bsz, seq, D_MODEL), F32),
                   jax.ShapeDtypeStruct((bsz, halo, 2 * D_FF), F32)],
        grid=(bsz, seq // tm),
        in_specs=[_tok_spec(tm, D_MODEL), _mod_spec(mrows, tm), _mod_spec(mrows, tm),
                  _mod_spec(mrows, tm), cs] + [_const_spec(a.shape) for a in weights],
        out_specs=[_tok_spec(tm, D_MODEL), cs],
        scratch_shapes=[pltpu.VMEM((halo + tm, D_FF), F32), pltpu.VMEM((tm, D_FF), BF16)],
        compiler_params=_params(2),
        name="ffn",
    )(x, sh, sc, g2, cb0, *weights)


_O_GATE = 2 * DQK_M + DV_M
_O_OG = _O_GATE + 2 * NH_M
_O_QA = _O_OG + DV_M
_O_KA = _O_QA + DQ_A
_O_VA = _O_KA + DKV_A
_O_GM = _O_VA + DKV_A
_PREP_ROWS = 256


def _split_w_in_body(wt_ref, qk_ref, vm_ref, vmt_ref, g_ref, gates_ref, qa_ref, k_ref, v_ref, vt_ref):
    piece = lambda lo, hi: wt_ref[lo:hi, :]
    qk_ref[...] = piece(0, 2 * DQK_M).T.astype(BF16)
    vm = piece(2 * DQK_M, _O_GATE)
    vmt_ref[...] = vm.astype(BF16)
    vm_ref[...] = vm.T.astype(BF16)
    pad = jnp.zeros((LANES - 2 * NH_M, _PREP_ROWS), F32)
    g_ref[...] = jnp.concatenate([piece(_O_GATE, _O_OG), pad], axis=0).T.astype(BF16)
    gates_ref[:, :DV_M] = piece(_O_OG, _O_QA).T.astype(BF16)
    gates_ref[:, DV_M:] = piece(_O_GM, wt_ref.shape[0]).T.astype(BF16)
    qa_ref[...] = piece(_O_QA, _O_KA).T.astype(BF16)
    k_ref[...] = piece(_O_KA, _O_VA).T.astype(BF16)
    v = piece(_O_VA, _O_GM)
    vt_ref[...] = v.astype(BF16)
    v_ref[...] = v.T.astype(BF16)


def _split_w_in(w_in_t):
    d_in = w_in_t.shape[0]
    rows = lambda n: ((D_MODEL, n), pl.BlockSpec((_PREP_ROWS, n), lambda i: (i, 0)))
    cols = lambda n: ((n, D_MODEL), pl.BlockSpec((n, _PREP_ROWS), lambda i: (0, i)))
    outs = dict(wqk=rows(2 * DQK_M), wvm=rows(DV_M), wvm_t=cols(DV_M), wg=rows(LANES),
                wgates=rows(DV_M + d_in - _O_GM), wqa=rows(DQ_A), wk=rows(DKV_A), wv=rows(DKV_A),
                wv_t=cols(DKV_A))
    res = pl.pallas_call(
        _split_w_in_body,
        out_shape=[jax.ShapeDtypeStruct(s, BF16) for s, _ in outs.values()],
        grid=(D_MODEL // _PREP_ROWS,),
        in_specs=[pl.BlockSpec((d_in, _PREP_ROWS), lambda i: (0, i))],
        out_specs=[spec for _, spec in outs.values()],
        compiler_params=_params(1),
        name="split_w_in",
    )(w_in_t)
    return dict(zip(outs.keys(), res))


def _prep_weights(w_in, b_in, mlstm_norm_w, w_branch_m, w_branch_a, w_out, ln1_g, ln1_b,
                  w_up, b_up, conv_w, conv_b, w_down, b_down, ln2_g, ln2_b):
    row = lambda a: a.reshape(1, -1)
    gate_pad = LANES - 2 * NH_M
    b_k, b_v = b_in[_O_KA:_O_VA], b_in[_O_VA:_O_GM]
    return dict(
        **_split_w_in(w_in.T),
        bqk=row(b_in[:2 * DQK_M]),
        bvm_row=row(b_in[2 * DQK_M:_O_GATE]), bvm_col=b_in[2 * DQK_M:_O_GATE].reshape(-1, 1),
        bg=row(jnp.pad(b_in[_O_GATE:_O_OG], (0, gate_pad))),
        bqa=row(b_in[_O_QA:_O_KA]),
        bk=row(b_k),
        bv_row=row(b_v), bv_col=b_v.reshape(-1, 1),
        bgates=row(jnp.concatenate([b_in[_O_OG:_O_QA], b_in[_O_GM:]])),
        norm_w=row(mlstm_norm_w), norm_w_col=mlstm_norm_w.reshape(-1, 1),
        wbm=w_branch_m.astype(BF16), wba=w_branch_a.astype(BF16),
        wo=w_out.astype(BF16), ln1_g=row(ln1_g), ln1_b=row(ln1_b),
        wup=w_up.astype(BF16), bup=row(b_up), cw=conv_w, cbias=row(conv_b),
        wdn=w_down.astype(BF16), bdn=row(b_down), ln2_g=row(ln2_g), ln2_b=row(ln2_b))


def kernel(x_prompt, x_sample, c_prompt, c_sample, state_mlstm_C, state_mlstm_n, state_mlstm_m,
           cache_k_win, cache_v_win, state_ffn_conv, w_ada, b_ada, w_in, b_in, mlstm_norm_w,
           attn_sinks, w_branch_m, w_branch_a, w_out, ln1_g, ln1_b, w_up, b_up, conv_w, conv_b,
           w_down, b_down, ln2_g, ln2_b):
    depth = w_in.shape[0]
    bp, lp, _ = x_prompt.shape
    bs, ls, _ = x_sample.shape
    assert cache_k_win.shape[2] == WINDOW
    alpha = (2 * depth) ** 0.25
    dt = x_prompt.dtype
    slopes = jnp.exp2(-8.0 * jnp.arange(1, NH_A + 1, dtype=F32) / NH_A)
    tm_p = 512
    tm_in = 1024
    ns = bs * ls

    yp = x_prompt
    to_pos = lambda a: a.transpose(1, 0, 2).reshape(1, -1, a.shape[-1])
    ys = to_pos(x_sample)
    new_p, new_s = [], []
    n_c = bp + bs
    c_rows = -(-n_c // SUBLANES) * SUBLANES
    c_all = jnp.concatenate([c_sample, c_prompt, jnp.zeros((c_rows - n_c, D_MODEL), dt)], axis=0)
    for l in range(depth):
        w = _prep_weights(w_in[l], b_in[l], mlstm_norm_w[l], w_branch_m[l], w_branch_a[l], w_out[l],
                          ln1_g[l], ln1_b[l], w_up[l], b_up[l], conv_w[l], conv_b[l], w_down[l],
                          b_down[l], ln2_g[l], ln2_b[l])
        prm = jnp.stack([slopes, attn_sinks[l].astype(F32)]) * LOG2E
        mod = _ada(c_all, w_ada[l], b_ada[l])
        mod_s = mod[:bs].reshape(bs, N_MOD, D_MODEL)
        mod_p = mod[bs:bs + bp].reshape(bp, N_MOD, D_MODEL)
        mp_ = [mod_p[:, j:j + 1] for j in range(N_MOD)]
        ms_ = [jnp.tile(mod_s[:, j], (ls, 1))[None] for j in range(N_MOD)]

        hp, qk, v, g, qa, kb, ka, vat = _inproj(yp, mp_[0], mp_[1], w, tm_in, True)
        hm, cp, np_, mp = _mlstm_prompt(qk, v, g, jnp.zeros((bp, NH_M, DHV_M, DHK_M), dt),
                                        jnp.zeros((bp, NH_M, DHK_M), dt), jnp.zeros((bp, 1, LANES), dt),
                                        w["norm_w_col"])
        ha = _swa_prompt(prm, qa, kb, vat)
        x1p = _merge(yp, hp, mp_[2], hm, ha, w, tm_p, alpha)
        halo_p = SUBLANES
        yp, csp = _ffn(x1p, mp_[3], mp_[4], mp_[5], jnp.zeros((bp, halo_p, 2 * D_FF), dt),
                       w, tm_p, 1, alpha)
        p_k = ka[:, lp - WINDOW:].reshape(bp, WINDOW, NKV_A, HD_A)
        p_v = vat[:, :, lp - WINDOW:].reshape(bp, NKV_A, HD_A, WINDOW).transpose(0, 3, 1, 2)
        new_p.append((cp, np_, mp[:, 0, :NH_M], p_k, p_v, csp[:, halo_p - (CONV_W - 1):]))

        hs, qk, v, g, qa, kn, vn = _inproj(ys, ms_[0], ms_[1], w, ns, False)
        per_seq = lambda a: a.reshape(ls, bs, a.shape[-1])
        m0 = jnp.pad(state_mlstm_m[l], ((0, 0), (0, LANES - NH_M)))[:, None, :]
        hm, cs_, ns_, ms = _mlstm_sample(per_seq(qk), per_seq(v), per_seq(g), state_mlstm_C[l],
                                         state_mlstm_n[l], m0, w["norm_w"], SEQS_PER_STEP)
        to_t = lambda a: a.transpose(0, 2, 3, 1).reshape(bs, DKV_A, WINDOW)
        from_t = lambda a: a.reshape(bs, NKV_A, HD_A, WINDOW).transpose(0, 3, 1, 2)
        ha, kct, vct = _swa_sample(prm, per_seq(qa), per_seq(kn), per_seq(vn),
                                   to_t(cache_k_win[l]), to_t(cache_v_win[l]), SWA_SEQS_PER_STEP)
        x1s = _merge(ys, hs, ms_[2], hm.reshape(1, ns, DV_M), ha.reshape(1, ns, DQ_A), w, ns, alpha)
        ys, css = _ffn(x1s, ms_[3], ms_[4], ms_[5], to_pos(state_ffn_conv[l]), w, ns, bs, alpha)
        conv_s = css.reshape(CONV_W - 1, bs, 2 * D_FF).transpose(1, 0, 2)
        new_s.append((cs_, ns_, ms[:, 0, :NH_M], from_t(kct), from_t(vct), conv_s))

    p_state = [jnp.stack(a) for a in zip(*new_p)]
    s_state = [jnp.stack(a) for a in zip(*new_s)]
    y_sample = ys.reshape(ls, bs, D_MODEL).transpose(1, 0, 2)
    return (yp, y_sample, *p_state, *s_state)
```

```python
import functools

import jax
import jax.numpy as jnp
from jax import lax
from jax.experimental import pallas as pl
from jax.experimental.pallas import tpu as pltpu

F32 = jnp.float32
BF16 = jnp.bfloat16

D_MODEL = 1024
NH_M, DHK_M, DHV_M = 4, 128, 256
DQK_M, DV_M = NH_M * DHK_M, NH_M * DHV_M
NH_A, NKV_A, HD_A = 16, 4, 64
GROUP_A = NH_A // NKV_A
WINDOW = 128
DQ_A, DKV_A = NH_A * HD_A, NKV_A * HD_A
D_FF = 2816
CONV_W = 3
N_MOD = 6
LN_EPS = 1e-5
CHUNK = 128
NEG = -1e30
LOG2E = 1.4426950408889634
LANES = 128
SUBLANES = 8
VMEM_LIMIT = 56 * 1024 * 1024
SEQS_PER_STEP = 16
SWA_SEQS_PER_STEP = 8


def _ln(x):
    mu = jnp.mean(x, axis=-1, keepdims=True)
    xc = x - mu
    var = jnp.mean(xc * xc, axis=-1, keepdims=True)
    return xc * lax.rsqrt(var + LN_EPS)


def _dot(a, b):
    return jnp.dot(a, b, preferred_element_type=F32)


def _dot_nt(a, b):
    return lax.dot_general(a, b, (((1,), (1,)), ((), ())), preferred_element_type=F32)


def _dot_tn(a, b):
    return lax.dot_general(a, b, (((0,), (0,)), ((), ())), preferred_element_type=F32)


def _const_spec(shape):
    nd = len(shape)
    return pl.BlockSpec(shape, lambda *_: (0,) * nd, pipeline_mode=pl.Buffered(1))


def _params(n_grid):
    return pltpu.CompilerParams(dimension_semantics=("arbitrary",) * n_grid,
                                vmem_limit_bytes=VMEM_LIMIT)


def _first_step():
    return (pl.program_id(0) == 0) & (pl.program_id(1) == 0)


def _ada_body(c_ref, w_ref, b_ref, o_ref):
    c = c_ref[...]
    s = (c * jax.nn.sigmoid(c)).astype(BF16)
    o_ref[...] = _dot(s, w_ref[...].astype(BF16)) + b_ref[...]


def _ada(c, w_ada, b_ada):
    rows = c.shape[0]
    n_out = w_ada.shape[1]
    bn = 3072
    assert n_out % bn == 0
    return pl.pallas_call(
        _ada_body,
        out_shape=jax.ShapeDtypeStruct((rows, n_out), F32),
        grid=(n_out // bn,),
        in_specs=[pl.BlockSpec((rows, D_MODEL), lambda j: (0, 0)),
                  pl.BlockSpec((D_MODEL, bn), lambda j: (0, j)),
                  pl.BlockSpec((1, bn), lambda j: (0, j))],
        out_specs=pl.BlockSpec((rows, bn), lambda j: (0, j)),
        compiler_params=_params(1),
        name="ada",
    )(c, w_ada, b_ada.reshape(1, n_out))


def _inproj_body(prompt, x_ref, sh_ref, sc_ref, wqk_ref, bqk_ref, wvm_ref, bvm_ref, wg_ref, bg_ref,
                 wqa_ref, bqa_ref, wk_ref, bk_ref, wv_ref, bv_ref, h_ref, qk_ref, v_ref, g_ref, qa_ref, *rest):
    h = (_ln(x_ref[...]) * (1.0 + sc_ref[...]) + sh_ref[...]).astype(BF16)
    h_ref[...] = h

    def proj(w_ref, b_ref, lo, n):
        return _dot(h, w_ref[:, lo:lo + n]) + b_ref[:, lo:lo + n]

    qk_ref[:, :DQK_M] = (proj(wqk_ref, bqk_ref, 0, DQK_M) * DHK_M ** -0.5).astype(qk_ref.dtype)
    qk_ref[:, DQK_M:] = proj(wqk_ref, bqk_ref, DQK_M, DQK_M).astype(qk_ref.dtype)
    if prompt:
        v_ref[...] = (_dot_nt(wvm_ref[...], h) + bvm_ref[...]).astype(v_ref.dtype)
    else:
        v_ref[...] = proj(wvm_ref, bvm_ref, 0, DV_M).astype(v_ref.dtype)
    g_ref[...] = proj(wg_ref, bg_ref, 0, LANES)
    qa_ref[...] = (proj(wqa_ref, bqa_ref, 0, DQ_A) * (HD_A ** -0.5 * LOG2E)).astype(qa_ref.dtype)
    ka = proj(wk_ref, bk_ref, 0, DKV_A)
    if prompt:
        kb_ref, ka_ref, vat_ref = rest
        kb_ref[...] = ka.astype(BF16)
        ka_ref[...] = ka
        vat_ref[...] = _dot_nt(wv_ref[...], h) + bv_ref[...]
    else:
        ka_ref, va_ref = rest
        ka_ref[...] = ka
        va_ref[...] = _dot(h, wv_ref[...]) + bv_ref[...]


def _tok_spec(tm, n):
    return pl.BlockSpec((None, tm, n), lambda b, i: (b, i, 0))


def _mod_spec(rows, tm):
    if rows == 1:
        return pl.BlockSpec((None, 1, D_MODEL), lambda b, i: (b, 0, 0))
    return pl.BlockSpec((None, tm, D_MODEL), lambda b, i: (b, i, 0))


def _inproj(x, sh, sc, w, tm, prompt):
    bsz, seq, _ = x.shape
    mrows = sh.shape[1]
    act = BF16 if prompt else F32
    t_spec = lambda n: pl.BlockSpec((None, n, tm), lambda b, i: (b, 0, i))
    outs = [((bsz, seq, D_MODEL), BF16, _tok_spec(tm, D_MODEL)),
            ((bsz, seq, 2 * DQK_M), act, _tok_spec(tm, 2 * DQK_M)),
            ((bsz, DV_M, seq), act, t_spec(DV_M)) if prompt else ((bsz, seq, DV_M), act, _tok_spec(tm, DV_M)),
            ((bsz, seq, LANES), F32, _tok_spec(tm, LANES)),
            ((bsz, seq, DQ_A), act, _tok_spec(tm, DQ_A))]
    if prompt:
        wvm, bvm, wv, bv = w["wvm_t"], w["bvm_col"], w["wv_t"], w["bv_col"]
        outs += [((bsz, seq, DKV_A), BF16, _tok_spec(tm, DKV_A)),
                 ((bsz, seq, DKV_A), F32, _tok_spec(tm, DKV_A)),
                 ((bsz, DKV_A, seq), F32, t_spec(DKV_A))]
    else:
        wvm, bvm, wv, bv = w["wvm"], w["bvm_row"], w["wv"], w["bv_row"]
        outs += [((bsz, seq, DKV_A), F32, _tok_spec(tm, DKV_A)),
                 ((bsz, seq, DKV_A), F32, _tok_spec(tm, DKV_A))]
    weights = [w["wqk"], w["bqk"], wvm, bvm, w["wg"], w["bg"], w["wqa"], w["bqa"], w["wk"], w["bk"],
               wv, bv]
    return pl.pallas_call(
        functools.partial(_inproj_body, prompt),
        out_shape=[jax.ShapeDtypeStruct(s, dt) for s, dt, _ in outs],
        grid=(bsz, seq // tm),
        in_specs=[_tok_spec(tm, D_MODEL), _mod_spec(mrows, tm), _mod_spec(mrows, tm)]
                 + [_const_spec(a.shape) for a in weights],
        out_specs=[spec for _, _, spec in outs],
        compiler_params=_params(2),
        name="inproj",
    )(x, sh, sc, *weights)


def _scan_rows(x, op, rows):
    row = lax.broadcasted_iota(jnp.int32, x.shape, 0)
    d = 1
    while d < rows:
        shifted = pltpu.roll(x, d, axis=0)
        x = jnp.where(row >= d, op(x, shifted), x)
        d *= 2
    return x


def _mlstm_sample_body(t_in, nb, qk_ref, v_ref, g_ref, c0_ref, n0_ref, m0_ref, nw_ref,
                       h_ref, c_ref, n_ref, m_ref, *pads):
    t = SUBLANES
    c_ref[...] = c0_ref[...]
    n_ref[...] = n0_ref[...]
    m_ref[...] = m0_ref[...]

    @pl.when(_first_step())
    def _():
        for p in pads:
            p[...] = jnp.zeros(p.shape, p.dtype)
    for src, dst in zip((qk_ref, v_ref, g_ref), pads):
        for s in range(nb):
            dst[s, pl.ds(0, t_in), :] = src[:, s, :]
    qk_src, v_src, g_src = pads

    row = lax.broadcasted_iota(jnp.int32, (t, LANES), 0)
    lane = lax.broadcasted_iota(jnp.int32, (t, LANES), 1)
    lane1 = lax.broadcasted_iota(jnp.int32, (1, LANES), 1)
    r2 = lax.broadcasted_iota(jnp.int32, (t, t), 0)
    c2 = lax.broadcasted_iota(jnp.int32, (t, t), 1)
    causal = c2 <= r2
    pad_gate = jnp.where(lane < NH_M, NEG, -NEG)

    seqs = []
    for s in range(nb):
        ga = _gate_algebra(jnp.where(row < t_in, g_src[s], pad_gate), m_ref[s], t)
        ga["gt"] = ga["gg"].T
        m_ref[s] = jnp.where(lane1 < NH_M, ga["m_new"], 0.0)
        seqs.append(ga)

    units = [(s, hd) for s in range(nb) for hd in range(NH_M)]
    nu = len(units)
    col = lambda name, s, hd: seqs[s][name][:, hd:hd + 1]
    q = [qk_src[s, :, hd * DHK_M:(hd + 1) * DHK_M].astype(BF16) for s, hd in units]
    k = [qk_src[s, :, DQK_M + hd * DHK_M:DQK_M + (hd + 1) * DHK_M].astype(BF16) for s, hd in units]
    v = [v_src[s, :, hd * DHV_M:(hd + 1) * DHV_M].astype(BF16) for s, hd in units]
    c_old = [c_ref[s, hd] for s, hd in units]
    n_old = [n_ref[s, hd:hd + 1, :] for s, hd in units]

    sqk = [_dot_nt(q[u], k[u]) for u in range(nu)]
    dexp = [jnp.exp(jnp.where(causal, col("eb", s, hd) + seqs[s]["gt"][hd:hd + 1, :], NEG))
            for s, hd in units]
    smat = [sqk[u] * dexp[u] for u in range(nu)]
    intra = [_dot(smat[u].astype(BF16), v[u]) for u in range(nu)]
    inter = [_dot_nt(q[u], c_old[u].astype(BF16)) for u in range(nu)]
    qn = [jnp.sum(q[u].astype(F32) * n_old[u].astype(BF16).astype(F32), axis=-1, keepdims=True)
          for u in range(nu)]
    den = [jnp.sum(smat[u], axis=-1, keepdims=True) + col("a_in", s, hd) * qn[u]
           for u, (s, hd) in enumerate(units)]
    hh = [(intra[u] + col("a_in", s, hd) * inter[u])
          / jnp.maximum(jnp.abs(den[u]), col("lowb", s, hd)) for u, (s, hd) in enumerate(units)]
    hn = [_ln(hh[u]) * nw_ref[:, hd * DHV_M:(hd + 1) * DHV_M] for u, (s, hd) in enumerate(units)]
    for u, (s, hd) in enumerate(units):
        h_ref[:, s, hd * DHV_M:(hd + 1) * DHV_M] = hn[u][:t_in]

    kw = [k[u].astype(F32) * col("ws", s, hd) for u, (s, hd) in enumerate(units)]
    upd = [_dot_tn(v[u], kw[u].astype(BF16)) for u in range(nu)]
    for u, (s, hd) in enumerate(units):
        dec = seqs[s]["decay"][:, hd:hd + 1]
        c_ref[s, hd] = dec * c_old[u] + upd[u]
        n_ref[s, hd:hd + 1, :] = dec * n_old[u] + jnp.sum(kw[u], axis=0, keepdims=True)


def _mlstm_sample(qk, v, g, c0, n0, m0, norm_w, nb):
    t_in, nseq, _ = qk.shape
    assert t_in <= SUBLANES
    blk = lambda n: pl.BlockSpec((t_in, nb, n), lambda o, c: (0, o, 0))
    st_c = pl.BlockSpec((nb, NH_M, DHV_M, DHK_M), lambda o, c: (o, 0, 0, 0))
    st_n = pl.BlockSpec((nb, NH_M, DHK_M), lambda o, c: (o, 0, 0))
    st_m = pl.BlockSpec((nb, 1, LANES), lambda o, c: (o, 0, 0))
    return pl.pallas_call(
        functools.partial(_mlstm_sample_body, t_in, nb),
        out_shape=[jax.ShapeDtypeStruct((t_in, nseq, DV_M), F32),
                   jax.ShapeDtypeStruct((nseq, NH_M, DHV_M, DHK_M), F32),
                   jax.ShapeDtypeStruct((nseq, NH_M, DHK_M), F32),
                   jax.ShapeDtypeStruct((nseq, 1, LANES), F32)],
        grid=(nseq // nb, 1),
        in_specs=[blk(2 * DQK_M), blk(DV_M), blk(LANES), st_c, st_n, st_m,
                  pl.BlockSpec((1, DV_M), lambda o, c: (0, 0))],
        out_specs=[blk(DV_M), st_c, st_n, st_m],
        scratch_shapes=[pltpu.VMEM((nb, SUBLANES, 2 * DQK_M), F32), pltpu.VMEM((nb, SUBLANES, DV_M), F32),
                        pltpu.VMEM((nb, SUBLANES, LANES), F32)],
        compiler_params=_params(2),
        name="mlstm_sample",
    )(qk, v, g, c0, n0, m0, norm_w)


def _gate_algebra(g, m_prev, t):
    b = pltpu.roll(_scan_rows(jax.nn.log_sigmoid(g), jnp.add, t), LANES - NH_M, axis=1)
    gg = g - b
    gmx = _scan_rows(gg, jnp.maximum, t)
    a = b + m_prev
    mt = jnp.maximum(a, b + gmx)
    b_last, gmx_last = b[t - 1:t, :], gmx[t - 1:t, :]
    m_new = jnp.maximum(b_last + m_prev, b_last + gmx_last)
    return dict(gg=gg, eb=b - mt, a_in=jnp.exp(a - mt), lowb=jnp.exp(-mt),
                ws=jnp.exp(b_last + gg - m_new), decay=jnp.exp(b_last + m_prev - m_new), m_new=m_new)


MLSTM_CHUNKS_PER_STEP = 4


def _mlstm_prompt_body(nb, qk_ref, vt_ref, g_ref, nwc_ref, h_ref, c_ref, n_ref, m_ref, nwb):
    @pl.when(pl.program_id(1) == 0)
    def _():
        c_ref[...] = jnp.zeros(c_ref.shape, F32)
        n_ref[...] = jnp.zeros(n_ref.shape, F32)
        m_ref[...] = jnp.zeros(m_ref.shape, F32)

    @pl.when(_first_step())
    def _():
        for hd in range(NH_M):
            nwb[hd] = jnp.broadcast_to(nwc_ref[hd * DHV_M:(hd + 1) * DHV_M, :], (DHV_M, CHUNK))

    for ci in range(MLSTM_CHUNKS_PER_STEP):
        _mlstm_prompt_chunk(nb, slice(ci * CHUNK, (ci + 1) * CHUNK), qk_ref, vt_ref, g_ref, nwb,
                            h_ref, c_ref, n_ref, m_ref)


def _mlstm_prompt_chunk(nb, rows, qk_ref, vt_ref, g_ref, nwb, h_ref, c_ref, n_ref, m_ref):
    t = CHUNK
    lane1 = lax.broadcasted_iota(jnp.int32, (1, LANES), 1)
    r2 = lax.broadcasted_iota(jnp.int32, (t, t), 0)
    c2 = lax.broadcasted_iota(jnp.int32, (t, t), 1)
    causal = r2 <= c2

    seqs = []
    for s in range(nb):
        ga = _gate_algebra(g_ref[s, rows, :], m_ref[s], t)
        m_ref[s] = jnp.where(lane1 < NH_M, ga["m_new"], 0.0)
        for name in ("eb", "a_in", "lowb"):
            ga[name + "_t"] = ga[name].T
        seqs.append(ga)

    units = [(s, hd) for s in range(nb) for hd in range(NH_M)]
    nu = len(units)
    rowv = lambda name, s, hd: seqs[s][name + "_t"][hd:hd + 1, :]
    q = [qk_ref[s, rows, hd * DHK_M:(hd + 1) * DHK_M] for s, hd in units]
    k = [qk_ref[s, rows, DQK_M + hd * DHK_M:DQK_M + (hd + 1) * DHK_M] for s, hd in units]
    vt = [vt_ref[s, hd * DHV_M:(hd + 1) * DHV_M, rows] for s, hd in units]
    c_old = [c_ref[s, hd] for s, hd in units]
    n_old = [n_ref[s, hd:hd + 1, :] for s, hd in units]

    skq = [_dot_nt(k[u], q[u]) for u in range(nu)]
    dexp = [jnp.exp(jnp.where(causal, seqs[s]["gg"][:, hd:hd + 1] + rowv("eb", s, hd), NEG))
            for s, hd in units]
    smat = [skq[u] * dexp[u] for u in range(nu)]
    intra = [_dot(vt[u], smat[u].astype(BF16)) for u in range(nu)]
    inter = [_dot_nt(c_old[u].astype(BF16), q[u]) for u in range(nu)]
    qn = [_dot_nt(jnp.broadcast_to(n_old[u], (SUBLANES, DHK_M)).astype(BF16), q[u])[0:1, :]
          for u in range(nu)]
    den = [jnp.sum(smat[u], axis=0, keepdims=True) + rowv("a_in", s, hd) * qn[u]
           for u, (s, hd) in enumerate(units)]
    inv = [1.0 / jnp.maximum(jnp.abs(den[u]), rowv("lowb", s, hd)) for u, (s, hd) in enumerate(units)]
    hh = [(intra[u] + rowv("a_in", s, hd) * inter[u]) * inv[u] for u, (s, hd) in enumerate(units)]
    mu = [jnp.mean(hh[u], axis=0, keepdims=True) for u in range(nu)]
    xc = [hh[u] - mu[u] for u in range(nu)]
    var = [jnp.mean(xc[u] * xc[u], axis=0, keepdims=True) for u in range(nu)]
    hn = [xc[u] * lax.rsqrt(var[u] + LN_EPS) * nwb[hd] for u, (s, hd) in enumerate(units)]
    for u, (s, hd) in enumerate(units):
        h_ref[s, rows, hd * DHV_M:(hd + 1) * DHV_M] = hn[u].T

    kw = [(k[u].astype(F32) * seqs[s]["ws"][:, hd:hd + 1]).astype(BF16) for u, (s, hd) in enumerate(units)]
    upd = [_dot(vt[u], kw[u]) for u in range(nu)]
    nupd = [_dot(jnp.ones((SUBLANES, t), BF16), kw[u])[0:1, :] for u in range(nu)]
    for u, (s, hd) in enumerate(units):
        dec = seqs[s]["decay"][:, hd:hd + 1]
        c_ref[s, hd] = dec * c_old[u] + upd[u]
        n_ref[s, hd:hd + 1, :] = dec * n_old[u] + nupd[u]


def _mlstm_prompt(qk, vt, g, norm_w_col):
    nb, seq, _ = qk.shape
    ts = MLSTM_CHUNKS_PER_STEP * CHUNK
    tok = lambda n: pl.BlockSpec((nb, ts, n), lambda o, c: (0, c, 0))
    st_c = pl.BlockSpec((nb, NH_M, DHV_M, DHK_M), lambda o, c: (0, 0, 0, 0))
    st_n = pl.BlockSpec((nb, NH_M, DHK_M), lambda o, c: (0, 0, 0))
    st_m = pl.BlockSpec((nb, 1, LANES), lambda o, c: (0, 0, 0))
    return pl.pallas_call(
        functools.partial(_mlstm_prompt_body, nb),
        out_shape=[jax.ShapeDtypeStruct((nb, seq, DV_M), F32),
                   jax.ShapeDtypeStruct((nb, NH_M, DHV_M, DHK_M), F32),
                   jax.ShapeDtypeStruct((nb, NH_M, DHK_M), F32),
                   jax.ShapeDtypeStruct((nb, 1, LANES), F32)],
        grid=(1, seq // ts),
        in_specs=[tok(2 * DQK_M), pl.BlockSpec((nb, DV_M, ts), lambda o, c: (0, 0, c)), tok(LANES),
                  pl.BlockSpec((DV_M, 1), lambda o, c: (0, 0))],
        out_specs=[tok(DV_M), st_c, st_n, st_m],
        scratch_shapes=[pltpu.VMEM((NH_M, DHV_M, CHUNK), F32)],
        compiler_params=_params(2),
        name="mlstm_prompt",
    )(qk, vt, g, norm_w_col)


SWA_BLOCKS_PER_STEP = 16


def _swa_prompt_body(prm_ref, q_ref, kc_ref, kp_ref, vc_ref, vp_ref, o_ref, tbl):
    row = lax.broadcasted_iota(jnp.int32, (WINDOW, WINDOW), 0)
    col = lax.broadcasted_iota(jnp.int32, (WINDOW, WINDOW), 1)
    tri = row <= col

    @pl.when(_first_step())
    def _():
        dist = jnp.where(tri, col - row, col - row + WINDOW).astype(F32)
        for hd in range(NH_A):
            bias = prm_ref[0, hd] * dist
            tbl[0, hd] = bias
            tbl[1, hd] = bias + jnp.where(tri, 0.0, -NEG)

    for b in range(SWA_BLOCKS_PER_STEP):
        rows = pl.ds(b * WINDOW, WINDOW)
        which = jnp.where(pl.program_id(1) == 0, 1, 0) if b == 0 else 0
        k_prev = kp_ref if b == 0 else kc_ref.at[pl.ds((b - 1) * WINDOW, WINDOW), :]
        v_prev = vp_ref if b == 0 else vc_ref.at[:, pl.ds((b - 1) * WINDOW, WINDOW)]
        _swa_prompt_block(which, prm_ref, q_ref.at[rows, :], kc_ref.at[rows, :], k_prev,
                          vc_ref.at[:, rows], v_prev, o_ref.at[rows, :], tbl)


def _swa_prompt_block(which, prm_ref, q_ref, kc_ref, kp_ref, vc_ref, vp_ref, o_ref, tbl):
    row = lax.broadcasted_iota(jnp.int32, (WINDOW, WINDOW), 0)
    col = lax.broadcasted_iota(jnp.int32, (WINDOW, WINDOW), 1)
    tri = row <= col
    lo = col < HD_A
    zb = jnp.zeros((WINDOW, LANES), BF16)

    def placed(k_ref, kv):
        c, par = kv // 2, kv % 2
        own = k_ref[:, c * LANES:(c + 1) * LANES]
        swp = pltpu.roll(own, HD_A, axis=1)
        if par == 0:
            return jnp.where(lo, own, zb), jnp.where(lo, zb, swp)
        return jnp.where(lo, swp, zb), jnp.where(lo, zb, own)

    st = []
    for kv in range(NKV_A):
        lhs = jnp.concatenate([*placed(kc_ref, kv), *placed(kp_ref, kv)], axis=0)
        qg = jnp.concatenate([q_ref[:, (2 * kv) * LANES:(2 * kv + 1) * LANES],
                              q_ref[:, (2 * kv + 1) * LANES:(2 * kv + 2) * LANES]], axis=0)
        st.append(_dot_nt(lhs, qg))
    tiles = []
    for hd in range(NH_A):
        kv, a, par = hd // GROUP_A, (hd % GROUP_A) // 2, hd % 2
        cols = slice(a * WINDOW, (a + 1) * WINDOW)
        tiles.append(jnp.where(tri, st[kv][par * WINDOW:(par + 1) * WINDOW, cols],
                               st[kv][(2 + par) * WINDOW:(3 + par) * WINDOW, cols]))
    sc = jnp.concatenate(tiles, axis=0).reshape(NH_A, WINDOW, WINDOW) - tbl[which]
    sink = jnp.concatenate([jnp.full((1, 1, WINDOW), prm_ref[1, hd], F32) for hd in range(NH_A)], axis=0)
    mx = jnp.maximum(jnp.max(sc, axis=1, keepdims=True), sink)
    p = jnp.exp2(sc - mx)
    den = jnp.sum(p, axis=1, keepdims=True) + jnp.exp2(sink - mx)
    pn = p * (1.0 / den)

    zero = jnp.zeros((WINDOW, WINDOW), F32)
    z64 = jnp.zeros((HD_A, WINDOW), BF16)
    outs = []
    for kv in range(NKV_A):
        cols = []
        for a in range(2):
            pe, po = pn[kv * GROUP_A + 2 * a], pn[kv * GROUP_A + 2 * a + 1]
            cols.append(jnp.concatenate([jnp.where(tri, pe, zero), jnp.where(tri, po, zero),
                                         jnp.where(tri, zero, pe), jnp.where(tri, zero, po)],
                                        axis=0).astype(BF16))
        pt = jnp.concatenate(cols, axis=1)
        vc = vc_ref[kv * HD_A:(kv + 1) * HD_A, :].astype(BF16)
        vp = vp_ref[kv * HD_A:(kv + 1) * HD_A, :].astype(BF16)
        vt = jnp.concatenate([jnp.concatenate([vc, z64], axis=0), jnp.concatenate([z64, vc], axis=0),
                              jnp.concatenate([vp, z64], axis=0), jnp.concatenate([z64, vp], axis=0)],
                             axis=1)
        ot = _dot(vt, pt)
        outs += [ot[:, :WINDOW].T, ot[:, WINDOW:].T]
    o_ref[...] = jnp.concatenate(outs, axis=1).astype(o_ref.dtype)


def _swa_prompt(prm, q, kb, vat):
    bsz, seq, _ = q.shape
    tq = SWA_BLOCKS_PER_STEP * WINDOW
    prev = lambda i: jnp.maximum(i * SWA_BLOCKS_PER_STEP - 1, 0)
    return pl.pallas_call(
        _swa_prompt_body,
        out_shape=jax.ShapeDtypeStruct((bsz, seq, DQ_A), BF16),
        grid=(bsz, seq // tq),
        in_specs=[pl.BlockSpec(memory_space=pltpu.SMEM),
                  pl.BlockSpec((None, tq, DQ_A), lambda b, i: (b, i, 0)),
                  pl.BlockSpec((None, tq, DKV_A), lambda b, i: (b, i, 0)),
                  pl.BlockSpec((None, WINDOW, DKV_A), lambda b, i: (b, prev(i), 0)),
                  pl.BlockSpec((None, DKV_A, tq), lambda b, i: (b, 0, i)),
                  pl.BlockSpec((None, DKV_A, WINDOW), lambda b, i: (b, 0, prev(i)))],
        out_specs=pl.BlockSpec((None, tq, DQ_A), lambda b, i: (b, i, 0)),
        scratch_shapes=[pltpu.VMEM((2, NH_A, WINDOW, WINDOW), F32)],
        compiler_params=_params(2),
        name="swa_prompt",
    )(prm, q, kb, kb, vat, vat)


def _half_mask(shape, half):
    lane = lax.broadcasted_iota(jnp.int32, shape, 1)
    return lane < HD_A if half == 0 else lane >= HD_A


def _swa_sample_body(t_in, nb, prm_ref, q_ref, kn_ref, vn_ref, kct_ref, vct_ref,
                     o_ref, kco_ref, vco_ref, tbl, q8, kn_pad, vn_pad):
    tq = SUBLANES
    rows = NH_A * tq
    row = lax.broadcasted_iota(jnp.int32, (rows, WINDOW), 0)
    col = lax.broadcasted_iota(jnp.int32, (rows, WINDOW), 1)
    tri = col <= (row & (tq - 1))

    @pl.when(_first_step())
    def _():
        r8 = lax.broadcasted_iota(jnp.int32, (tq, WINDOW), 0)
        c8 = lax.broadcasted_iota(jnp.int32, (tq, WINDOW), 1)
        dist = jnp.where(c8 <= r8, r8 - c8, r8 - c8 + WINDOW).astype(F32)
        for hd in range(NH_A):
            tbl[pl.ds(hd * tq, tq), :] = prm_ref[0, hd] * dist
        for p in (q8, kn_pad, vn_pad):
            p[...] = jnp.zeros(p.shape, p.dtype)

    for s in range(nb):
        q8[s, pl.ds(0, t_in), :] = q_ref[:, s, :]
        kn_pad[s, pl.ds(0, t_in), :] = kn_ref[:, s, :]
        vn_pad[s, pl.ds(0, t_in), :] = vn_ref[:, s, :]
    sink_col = jnp.concatenate([jnp.full((tq, 1), prm_ref[1, hd], F32) for hd in range(NH_A)], axis=0)
    z8 = jnp.zeros((tq, LANES), F32)
    n_chunk = NKV_A // 2
    heads_per_chunk = NH_A // n_chunk

    def place(piece, src_half, dst_half):
        if src_half != dst_half:
            piece = pltpu.roll(piece, HD_A, axis=1)
        return jnp.where(_half_mask(piece.shape, dst_half), piece, z8)

    qexp = []
    for s in range(nb):
        per_c = []
        for c in range(n_chunk):
            pieces = []
            for hl in range(heads_per_chunk):
                hd = c * heads_per_chunk + hl
                pieces.append(place(q8[s, :, (hd // 2) * LANES:(hd // 2 + 1) * LANES],
                                    hd % 2, hl // GROUP_A))
            per_c.append(jnp.concatenate(pieces, axis=0).astype(BF16))
        qexp.append(per_c)
    csl = lambda c: slice(c * LANES, (c + 1) * LANES)
    s_prev = [[_dot(qexp[s][c], kct_ref[s, csl(c), :].astype(BF16)) for c in range(n_chunk)]
              for s in range(nb)]
    s_cur = [[_dot_nt(qexp[s][c], kn_pad[s, :, csl(c)].astype(BF16)) for c in range(n_chunk)]
             for s in range(nb)]
    sc = [jnp.where(tri, jnp.concatenate(s_cur[s], axis=0), jnp.concatenate(s_prev[s], axis=0)) - tbl[...]
          for s in range(nb)]
    mx = [jnp.maximum(jnp.max(sc[s], axis=-1, keepdims=True), sink_col) for s in range(nb)]
    p = [jnp.exp2(sc[s] - mx[s]) for s in range(nb)]
    den = [jnp.sum(p[s], axis=-1, keepdims=True) + jnp.exp2(sink_col - mx[s]) for s in range(nb)]
    pn = [p[s] * (1.0 / den[s]) for s in range(nb)]
    zero = jnp.zeros((rows, WINDOW), F32)
    pc = [jnp.where(tri, pn[s], zero).astype(BF16) for s in range(nb)]
    pp = [jnp.where(tri, zero, pn[s]).astype(BF16) for s in range(nb)]
    half_rows = heads_per_chunk * tq
    oc = [[_dot_nt(pp[s][c * half_rows:(c + 1) * half_rows], vct_ref[s, csl(c), :].astype(BF16))
           + _dot(pc[s][c * half_rows:(c + 1) * half_rows], vn_pad[s, :, csl(c)].astype(BF16))
           for c in range(n_chunk)] for s in range(nb)]
    for s in range(nb):
        chunks = []
        for pch in range(NH_A // 2):
            acc = None
            for hd in (2 * pch, 2 * pch + 1):
                c, hl = hd // heads_per_chunk, hd % heads_per_chunk
                piece = place(oc[s][c][hl * tq:(hl + 1) * tq, :], hl // GROUP_A, hd % 2)
                acc = piece if acc is None else acc + piece
            chunks.append(acc)
        o_ref[:, s, :] = jnp.concatenate(chunks, axis=1)[:t_in]

    lane = lax.broadcasted_iota(jnp.int32, (DKV_A, WINDOW), 1)
    for new_pad, old_ref, out_ref in ((kn_pad, kct_ref, kco_ref), (vn_pad, vct_ref, vco_ref)):
        for s in range(nb):
            merged = jnp.where(lane < t_in, new_pad[s].T, old_ref[s])
            out_ref[s] = pltpu.roll(merged, WINDOW - t_in, axis=1)


def _swa_sample(prm, q, kn, vn, kct, vct, nb):
    t_in, nseq, _ = q.shape
    assert t_in <= SUBLANES
    cur = lambda n: pl.BlockSpec((t_in, nb, n), lambda o, i: (0, o, 0))
    win = pl.BlockSpec((nb, DKV_A, WINDOW), lambda o, i: (o, 0, 0))
    return pl.pallas_call(
        functools.partial(_swa_sample_body, t_in, nb),
        out_shape=[jax.ShapeDtypeStruct((t_in, nseq, DQ_A), F32),
                   jax.ShapeDtypeStruct((nseq, DKV_A, WINDOW), F32),
                   jax.ShapeDtypeStruct((nseq, DKV_A, WINDOW), F32)],
        grid=(nseq // nb, 1),
        in_specs=[pl.BlockSpec(memory_space=pltpu.SMEM), cur(DQ_A), cur(DKV_A), cur(DKV_A), win, win],
        out_specs=[cur(DQ_A), win, win],
        scratch_shapes=[pltpu.VMEM((NH_A * SUBLANES, WINDOW), F32),
                        pltpu.VMEM((nb, SUBLANES, DQ_A), F32),
                        pltpu.VMEM((nb, WINDOW, DKV_A), F32),
                        pltpu.VMEM((nb, WINDOW, DKV_A), F32)],
        compiler_params=_params(2),
        name="swa_sample",
    )(prm, q, kn, vn, kct, vct)


def _merge_body(alpha, x_ref, h_ref, g1_ref, hm_ref, ha_ref, wgt_ref, bgt_ref,
                wbm_ref, wba_ref, wo_ref, lg_ref, lb_ref, o_ref):
    nsub = 2
    sub = x_ref.shape[0] // nsub
    rows = [pl.ds(r * sub, sub) for r in range(nsub)]
    mod = lambda ref, r: ref[...] if ref.shape[0] == 1 else ref[rows[r], :]
    gate = lambda g, j: g[:, j * D_MODEL:(j + 1) * D_MODEL]

    x = [x_ref[rows[r], :] for r in range(nsub)]
    h = [h_ref[rows[r], :] for r in range(nsub)]
    g = [jax.nn.sigmoid(_dot(h[r], wgt_ref[...]) + bgt_ref[...]) for r in range(nsub)]
    hm = [(hm_ref[rows[r], :] * gate(g[r], 0)).astype(BF16) for r in range(nsub)]
    bm = [_dot(hm[r], wbm_ref[...]) for r in range(nsub)]
    ba = [_dot(ha_ref[rows[r], :].astype(BF16), wba_ref[...]) for r in range(nsub)]
    merged = [(gate(g[r], 1) * bm[r] + gate(g[r], 2) * ba[r]).astype(BF16) for r in range(nsub)]
    mo = [_dot(merged[r], wo_ref[...]) for r in range(nsub)]
    for r in range(nsub):
        o_ref[rows[r], :] = _ln(alpha * x[r] + mod(g1_ref, r) * mo[r]) * lg_ref[...] + lb_ref[...]


def _merge(x, h, g1, hm, ha, w, tm, alpha):
    bsz, seq, _ = x.shape
    mrows = g1.shape[1]
    weights = [w["wgates"], w["bgates"], w["wbm"], w["wba"], w["wo"],
               w["ln1_g"], w["ln1_b"]]
    return pl.pallas_call(
        functools.partial(_merge_body, alpha),
        out_shape=jax.ShapeDtypeStruct((bsz, seq, D_MODEL), F32),
        grid=(bsz, seq // tm),
        in_specs=[_tok_spec(tm, D_MODEL), _tok_spec(tm, D_MODEL), _mod_spec(mrows, tm),
                  _tok_spec(tm, DV_M), _tok_spec(tm, DQ_A)]
                 + [_const_spec(a.shape) for a in weights],
        out_specs=_tok_spec(tm, D_MODEL),
        compiler_params=_params(2),
        name="merge",
    )(x, h, g1, hm, ha, *weights)


def _ffn_body(alpha, tm, stride, halo, x_ref, sh_ref, sc_ref, g2_ref, cb0_ref, wup_ref, bup_ref,
              cw_ref, cbias_ref, wdn_ref, bdn_ref, lg_ref, lb_ref, o_ref, cs_ref, ubuf, act):
    @pl.when(pl.program_id(1) == 0)
    def _():
        cs_ref[...] = cb0_ref[...]

    nsub = 2
    sub = tm // nsub
    rows = [pl.ds(r * sub, sub) for r in range(nsub)]
    mod = lambda ref, r: ref[...] if ref.shape[0] == 1 else ref[rows[r], :]
    x = [x_ref[rows[r], :] for r in range(nsub)]
    h = [(_ln(x[r]) * (1.0 + mod(sc_ref, r)) + mod(sh_ref, r)).astype(BF16) for r in range(nsub)]
    ys = [[None, None] for _ in range(nsub)]
    for half in range(2):
        cols = slice(half * D_FF, (half + 1) * D_FF)
        u = [_dot(h[r], wup_ref[:, cols]) + bup_ref[:, cols] for r in range(nsub)]
        ubuf[pl.ds(0, halo), :] = cs_ref[:, cols]
        for r in range(nsub):
            ubuf[pl.ds(halo + r * sub, sub), :] = u[r]
        cs_ref[:, cols] = ubuf[pl.ds(tm, halo), :]
        for r in range(nsub):
            y = cbias_ref[:, cols] + u[r] * cw_ref[CONV_W - 1:CONV_W, cols]
            for j in range(CONV_W - 1):
                tap = ubuf[pl.ds(halo + r * sub - (CONV_W - 1 - j) * stride, sub), :]
                y = y + tap * cw_ref[j:j + 1, cols]
            ys[r][half] = y
    for r in range(nsub):
        act[rows[r], :] = (jax.nn.gelu(ys[r][0]) * ys[r][1]).astype(BF16)
    f = [_dot(act[rows[r], :], wdn_ref[...]) + bdn_ref[...] for r in range(nsub)]
    for r in range(nsub):
        o_ref[rows[r], :] = _ln(alpha * x[r] + mod(g2_ref, r) * f[r]) * lg_ref[...] + lb_ref[...]


def _ffn(x, sh, sc, g2, cb0, w, tm, stride, alpha):
    bsz, seq, _ = x.shape
    mrows = sh.shape[1]
    halo = cb0.shape[1]
    cs = pl.BlockSpec((None, halo, 2 * D_FF), lambda b, i: (b, 0, 0))
    weights = [w["wup"], w["bup"], w["cw"], w["cbias"], w["wdn"], w["bdn"], w["ln2_g"], w["ln2_b"]]
    return pl.pallas_call(
        functools.partial(_ffn_body, alpha, tm, stride, halo),
        out_shape=[jax.ShapeDtypeStruct((bsz, seq, D_MODEL), F32),
                   jax.ShapeDtypeStruct((bsz, halo, 2 * D_FF), F32)],
        grid=(bsz, seq // tm),
        in_specs=[_tok_spec(tm, D_MODEL), _mod_spec(mrows, tm), _mod_spec(mrows, tm),
                  _mod_spec(mrows, tm), cs] + [_const_spec(a.shape) for a in weights],
        out_specs=[_tok_spec(tm, D_MODEL), cs],
        scratch_shapes=[pltpu.VMEM((halo + tm, D_FF), F32), pltpu.VMEM((tm, D_FF), BF16)],
        compiler_params=_params(2),
        name="ffn",
    )(x, sh, sc, g2, cb0, *weights)


_O_GATE = 2 * DQK_M + DV_M
_O_OG = _O_GATE + 2 * NH_M
_O_QA = _O_OG + DV_M
_O_KA = _O_QA + DQ_A
_O_VA = _O_KA + DKV_A
_O_GM = _O_VA + DKV_A
_PREP_ROWS = 256


def _split_w_in_body(wt_ref, qk_ref, vm_ref, vmt_ref, g_ref, gates_ref, qa_ref, k_ref, v_ref, vt_ref):
    piece = lambda lo, hi: wt_ref[lo:hi, :]
    qk_ref[...] = piece(0, 2 * DQK_M).T.astype(BF16)
    vm = piece(2 * DQK_M, _O_GATE)
    vmt_ref[...] = vm.astype(BF16)
    vm_ref[...] = vm.T.astype(BF16)
    pad = jnp.zeros((LANES - 2 * NH_M, _PREP_ROWS), F32)
    g_ref[...] = jnp.concatenate([piece(_O_GATE, _O_OG), pad], axis=0).T.astype(BF16)
    gates_ref[:, :DV_M] = piece(_O_OG, _O_QA).T.astype(BF16)
    gates_ref[:, DV_M:] = piece(_O_GM, wt_ref.shape[0]).T.astype(BF16)
    qa_ref[...] = piece(_O_QA, _O_KA).T.astype(BF16)
    k_ref[...] = piece(_O_KA, _O_VA).T.astype(BF16)
    v = piece(_O_VA, _O_GM)
    vt_ref[...] = v.astype(BF16)
    v_ref[...] = v.T.astype(BF16)


def _split_w_in(w_in_t):
    d_in = w_in_t.shape[0]
    rows = lambda n: ((D_MODEL, n), pl.BlockSpec((_PREP_ROWS, n), lambda i: (i, 0)))
    cols = lambda n: ((n, D_MODEL), pl.BlockSpec((n, _PREP_ROWS), lambda i: (0, i)))
    outs = dict(wqk=rows(2 * DQK_M), wvm=rows(DV_M), wvm_t=cols(DV_M), wg=rows(LANES),
                wgates=rows(DV_M + d_in - _O_GM), wqa=rows(DQ_A), wk=rows(DKV_A), wv=rows(DKV_A),
                wv_t=cols(DKV_A))
    res = pl.pallas_call(
        _split_w_in_body,
        out_shape=[jax.ShapeDtypeStruct(s, BF16) for s, _ in outs.values()],
        grid=(D_MODEL // _PREP_ROWS,),
        in_specs=[pl.BlockSpec((d_in, _PREP_ROWS), lambda i: (0, i))],
        out_specs=[spec for _, spec in outs.values()],
        compiler_params=_params(1),
        name="split_w_in",
    )(w_in_t)
    return dict(zip(outs.keys(), res))


def _prep_weights(w_in, b_in, mlstm_norm_w, w_branch_m, w_branch_a, w_out, ln1_g, ln1_b,
                  w_up, b_up, conv_w, conv_b, w_down, b_down, ln2_g, ln2_b):
    row = lambda a: a.reshape(1, -1)
    gate_pad = LANES - 2 * NH_M
    b_k, b_v = b_in[_O_KA:_O_VA], b_in[_O_VA:_O_GM]
    return dict(
        **_split_w_in(w_in.T),
        bqk=row(b_in[:2 * DQK_M]),
        bvm_row=row(b_in[2 * DQK_M:_O_GATE]), bvm_col=b_in[2 * DQK_M:_O_GATE].reshape(-1, 1),
        bg=row(jnp.pad(b_in[_O_GATE:_O_OG], (0, gate_pad))),
        bqa=row(b_in[_O_QA:_O_KA]),
        bk=row(b_k),
        bv_row=row(b_v), bv_col=b_v.reshape(-1, 1),
        bgates=row(jnp.concatenate([b_in[_O_OG:_O_QA], b_in[_O_GM:]])),
        norm_w=row(mlstm_norm_w), norm_w_col=mlstm_norm_w.reshape(-1, 1),
        wbm=w_branch_m.astype(BF16), wba=w_branch_a.astype(BF16),
        wo=w_out.astype(BF16), ln1_g=row(ln1_g), ln1_b=row(ln1_b),
        wup=w_up.astype(BF16), bup=row(b_up), cw=conv_w, cbias=row(conv_b),
        wdn=w_down.astype(BF16), bdn=row(b_down), ln2_g=row(ln2_g), ln2_b=row(ln2_b))


def kernel(x_prompt, x_sample, c_prompt, c_sample, state_mlstm_C, state_mlstm_n, state_mlstm_m,
           cache_k_win, cache_v_win, state_ffn_conv, w_ada, b_ada, w_in, b_in, mlstm_norm_w,
           attn_sinks, w_branch_m, w_branch_a, w_out, ln1_g, ln1_b, w_up, b_up, conv_w, conv_b,
           w_down, b_down, ln2_g, ln2_b):
    depth = w_in.shape[0]
    bp, lp, _ = x_prompt.shape
    bs, ls, _ = x_sample.shape
    assert cache_k_win.shape[2] == WINDOW
    alpha = (2 * depth) ** 0.25
    dt = x_prompt.dtype
    slopes = jnp.exp2(-8.0 * jnp.arange(1, NH_A + 1, dtype=F32) / NH_A)
    tm_p = 512
    tm_in = 1024
    ns = bs * ls

    yp = x_prompt
    to_pos = lambda a: a.transpose(1, 0, 2).reshape(1, -1, a.shape[-1])
    ys = to_pos(x_sample)
    new_p, new_s = [], []
    n_c = bp + bs
    c_rows = -(-n_c // SUBLANES) * SUBLANES
    c_all = jnp.concatenate([c_sample, c_prompt, jnp.zeros((c_rows - n_c, D_MODEL), dt)], axis=0)
    for l in range(depth):
        w = _prep_weights(w_in[l], b_in[l], mlstm_norm_w[l], w_branch_m[l], w_branch_a[l], w_out[l],
                          ln1_g[l], ln1_b[l], w_up[l], b_up[l], conv_w[l], conv_b[l], w_down[l],
                          b_down[l], ln2_g[l], ln2_b[l])
        prm = jnp.stack([slopes, attn_sinks[l].astype(F32)]) * LOG2E
        mod = _ada(c_all, w_ada[l], b_ada[l])
        mod_s = mod[:bs].reshape(bs, N_MOD, D_MODEL)
        mod_p = mod[bs:bs + bp].reshape(bp, N_MOD, D_MODEL)
        mp_ = [mod_p[:, j:j + 1] for j in range(N_MOD)]
        ms_ = [jnp.tile(mod_s[:, j], (ls, 1))[None] for j in range(N_MOD)]

        hp, qk, v, g, qa, kb, ka, vat = _inproj(yp, mp_[0], mp_[1], w, tm_in, True)
        hm, cp, np_, mp = _mlstm_prompt(qk, v, g, w["norm_w_col"])
        ha = _swa_prompt(prm, qa, kb, vat)
        x1p = _merge(yp, hp, mp_[2], hm, ha, w, tm_p, alpha)
        halo_p = SUBLANES
        yp, csp = _ffn(x1p, mp_[3], mp_[4], mp_[5], jnp.zeros((bp, halo_p, 2 * D_FF), dt),
                       w, tm_p, 1, alpha)
        p_k = ka[:, lp - WINDOW:].reshape(bp, WINDOW, NKV_A, HD_A)
        p_v = vat[:, :, lp - WINDOW:].reshape(bp, NKV_A, HD_A, WINDOW).transpose(0, 3, 1, 2)
        new_p.append((cp, np_, mp[:, 0, :NH_M], p_k, p_v, csp[:, halo_p - (CONV_W - 1):]))

        hs, qk, v, g, qa, kn, vn = _inproj(ys, ms_[0], ms_[1], w, ns, False)
        per_seq = lambda a: a.reshape(ls, bs, a.shape[-1])
        m0 = jnp.pad(state_mlstm_m[l], ((0, 0), (0, LANES - NH_M)))[:, None, :]
        hm, cs_, ns_, ms = _mlstm_sample(per_seq(qk), per_seq(v), per_seq(g), state_mlstm_C[l],
                                         state_mlstm_n[l], m0, w["norm_w"], SEQS_PER_STEP)
        to_t = lambda a: a.transpose(0, 2, 3, 1).reshape(bs, DKV_A, WINDOW)
        from_t = lambda a: a.reshape(bs, NKV_A, HD_A, WINDOW).transpose(0, 3, 1, 2)
        ha, kct, vct = _swa_sample(prm, per_seq(qa), per_seq(kn), per_seq(vn),
                                   to_t(cache_k_win[l]), to_t(cache_v_win[l]), SWA_SEQS_PER_STEP)
        x1s = _merge(ys, hs, ms_[2], hm.reshape(1, ns, DV_M), ha.reshape(1, ns, DQ_A), w, ns, alpha)
        ys, css = _ffn(x1s, ms_[3], ms_[4], ms_[5], to_pos(state_ffn_conv[l]), w, ns, bs, alpha)
        conv_s = css.reshape(CONV_W - 1, bs, 2 * D_FF).transpose(1, 0, 2)
        new_s.append((cs_, ns_, ms[:, 0, :NH_M], from_t(kct), from_t(vct), conv_s))

    p_state = [jnp.stack(a) for a in zip(*new_p)]
    s_state = [jnp.stack(a) for a in zip(*new_s)]
    y_sample = ys.reshape(ls, bs, D_MODEL).transpose(1, 0, 2)
    return (yp, y_sample, *p_state, *s_state)
```

```python
import functools

import jax
import jax.numpy as jnp
from jax import lax
from jax.experimental import pallas as pl
from jax.experimental.pallas import tpu as pltpu

F32 = jnp.float32
BF16 = jnp.bfloat16

D_MODEL = 1024
NH_M, DHK_M, DHV_M = 4, 128, 256
DQK_M, DV_M = NH_M * DHK_M, NH_M * DHV_M
NH_A, NKV_A, HD_A = 16, 4, 64
GROUP_A = NH_A // NKV_A
WINDOW = 128
DQ_A, DKV_A = NH_A * HD_A, NKV_A * HD_A
D_FF = 2816
CONV_W = 3
N_MOD = 6
LN_EPS = 1e-5
CHUNK = 128
NEG = -1e30
LOG2E = 1.4426950408889634
LANES = 128
SUBLANES = 8
VMEM_LIMIT = 56 * 1024 * 1024
SEQS_PER_STEP = 16
SWA_SEQS_PER_STEP = 8


def _ln(x):
    mu = jnp.mean(x, axis=-1, keepdims=True)
    xc = x - mu
    var = jnp.mean(xc * xc, axis=-1, keepdims=True)
    return xc * lax.rsqrt(var + LN_EPS)


def _dot(a, b):
    return jnp.dot(a, b, preferred_element_type=F32)


def _dot_nt(a, b):
    return lax.dot_general(a, b, (((1,), (1,)), ((), ())), preferred_element_type=F32)


def _dot_tn(a, b):
    return lax.dot_general(a, b, (((0,), (0,)), ((), ())), preferred_element_type=F32)


def _const_spec(shape):
    nd = len(shape)
    return pl.BlockSpec(shape, lambda *_: (0,) * nd, pipeline_mode=pl.Buffered(1))


def _params(n_grid):
    return pltpu.CompilerParams(dimension_semantics=("arbitrary",) * n_grid,
                                vmem_limit_bytes=VMEM_LIMIT)


def _first_step():
    return (pl.program_id(0) == 0) & (pl.program_id(1) == 0)


def _ada_body(c_ref, w_ref, b_ref, o_ref):
    c = c_ref[...]
    s = (c * jax.nn.sigmoid(c)).astype(BF16)
    o_ref[...] = _dot(s, w_ref[...].astype(BF16)) + b_ref[...]


def _ada(c, w_ada, b_ada):
    rows = c.shape[0]
    n_out = w_ada.shape[1]
    bn = 3072
    assert n_out % bn == 0
    return pl.pallas_call(
        _ada_body,
        out_shape=jax.ShapeDtypeStruct((rows, n_out), F32),
        grid=(n_out // bn,),
        in_specs=[pl.BlockSpec((rows, D_MODEL), lambda j: (0, 0)),
                  pl.BlockSpec((D_MODEL, bn), lambda j: (0, j)),
                  pl.BlockSpec((1, bn), lambda j: (0, j))],
        out_specs=pl.BlockSpec((rows, bn), lambda j: (0, j)),
        compiler_params=_params(1),
        name="ada",
    )(c, w_ada, b_ada.reshape(1, n_out))


def _inproj_body(prompt, x_ref, sh_ref, sc_ref, wqk_ref, bqk_ref, wvm_ref, bvm_ref, wg_ref, bg_ref,
                 wqa_ref, bqa_ref, wk_ref, bk_ref, wv_ref, bv_ref, h_ref, qk_ref, v_ref, g_ref, qa_ref, *rest):
    tm = x_ref.shape[0]
    h = (_ln(x_ref[...]) * (1.0 + _mod_rows(sc_ref, 0, tm)) + _mod_rows(sh_ref, 0, tm)).astype(BF16)
    h_ref[...] = h

    def proj(w_ref, b_ref, lo, n):
        return _dot(h, w_ref[:, lo:lo + n]) + b_ref[:, lo:lo + n]

    qk_ref[:, :DQK_M] = (proj(wqk_ref, bqk_ref, 0, DQK_M) * DHK_M ** -0.5).astype(qk_ref.dtype)
    qk_ref[:, DQK_M:] = proj(wqk_ref, bqk_ref, DQK_M, DQK_M).astype(qk_ref.dtype)
    if prompt:
        v_ref[...] = (_dot_nt(wvm_ref[...], h) + bvm_ref[...]).astype(v_ref.dtype)
    else:
        v_ref[...] = proj(wvm_ref, bvm_ref, 0, DV_M).astype(v_ref.dtype)
    g_ref[...] = proj(wg_ref, bg_ref, 0, LANES)
    qa_ref[...] = (proj(wqa_ref, bqa_ref, 0, DQ_A) * (HD_A ** -0.5 * LOG2E)).astype(qa_ref.dtype)
    ka = proj(wk_ref, bk_ref, 0, DKV_A)
    if prompt:
        kb_ref, ka_ref, vat_ref = rest
        kb_ref[...] = ka.astype(BF16)
        ka_ref[...] = ka
        vat_ref[...] = _dot_nt(wv_ref[...], h) + bv_ref[...]
    else:
        ka_ref, va_ref = rest
        ka_ref[...] = ka
        va_ref[...] = _dot(h, wv_ref[...]) + bv_ref[...]


def _tok_spec(tm, n):
    return pl.BlockSpec((None, tm, n), lambda b, i: (b, i, 0))


def _mod_operand(m):
    if isinstance(m, tuple):
        arr, j, n = m
        return arr, pl.BlockSpec((n, D_MODEL), lambda b, i: (0, j))
    return m, pl.BlockSpec((None, 1, D_MODEL), lambda b, i: (b, 0, 0))


def _mod_rows(ref, start, n):
    period = ref.shape[0]
    if period == 1:
        return ref[...]
    assert start % period == 0 and n % period == 0
    return jnp.concatenate([ref[...]] * (n // period), axis=0)


def _inproj(x, sh, sc, w, tm, prompt):
    bsz, seq, _ = x.shape
    (sh, sh_spec), (sc, sc_spec) = _mod_operand(sh), _mod_operand(sc)
    act = BF16 if prompt else F32
    t_spec = lambda n: pl.BlockSpec((None, n, tm), lambda b, i: (b, 0, i))
    outs = [((bsz, seq, D_MODEL), BF16, _tok_spec(tm, D_MODEL)),
            ((bsz, seq, 2 * DQK_M), act, _tok_spec(tm, 2 * DQK_M)),
            ((bsz, DV_M, seq), act, t_spec(DV_M)) if prompt else ((bsz, seq, DV_M), act, _tok_spec(tm, DV_M)),
            ((bsz, seq, LANES), F32, _tok_spec(tm, LANES)),
            ((bsz, seq, DQ_A), act, _tok_spec(tm, DQ_A))]
    if prompt:
        wvm, bvm, wv, bv = w["wvm_t"], w["bvm_col"], w["wv_t"], w["bv_col"]
        outs += [((bsz, seq, DKV_A), BF16, _tok_spec(tm, DKV_A)),
                 ((bsz, seq, DKV_A), F32, _tok_spec(tm, DKV_A)),
                 ((bsz, DKV_A, seq), F32, t_spec(DKV_A))]
    else:
        wvm, bvm, wv, bv = w["wvm"], w["bvm_row"], w["wv"], w["bv_row"]
        outs += [((bsz, seq, DKV_A), F32, _tok_spec(tm, DKV_A)),
                 ((bsz, seq, DKV_A), F32, _tok_spec(tm, DKV_A))]
    weights = [w["wqk"], w["bqk"], wvm, bvm, w["wg"], w["bg"], w["wqa"], w["bqa"], w["wk"], w["bk"],
               wv, bv]
    return pl.pallas_call(
        functools.partial(_inproj_body, prompt),
        out_shape=[jax.ShapeDtypeStruct(s, dt) for s, dt, _ in outs],
        grid=(bsz, seq // tm),
        in_specs=[_tok_spec(tm, D_MODEL), sh_spec, sc_spec] + [_const_spec(a.shape) for a in weights],
        out_specs=[spec for _, _, spec in outs],
        compiler_params=_params(2),
        name="inproj",
    )(x, sh, sc, *weights)


def _scan_rows(x, op, rows):
    row = lax.broadcasted_iota(jnp.int32, x.shape, 0)
    d = 1
    while d < rows:
        shifted = pltpu.roll(x, d, axis=0)
        x = jnp.where(row >= d, op(x, shifted), x)
        d *= 2
    return x


def _mlstm_sample_body(t_in, nb, qk_ref, v_ref, g_ref, c0_ref, n0_ref, m0_ref, nw_ref,
                       h_ref, c_ref, n_ref, m_ref, *pads):
    t = SUBLANES
    c_ref[...] = c0_ref[...]
    n_ref[...] = n0_ref[...]
    m_ref[...] = m0_ref[...]

    @pl.when(_first_step())
    def _():
        for p in pads:
            p[...] = jnp.zeros(p.shape, p.dtype)
    for src, dst in zip((qk_ref, v_ref, g_ref), pads):
        for s in range(nb):
            dst[s, pl.ds(0, t_in), :] = src[:, s, :]
    qk_src, v_src, g_src = pads

    row = lax.broadcasted_iota(jnp.int32, (t, LANES), 0)
    lane = lax.broadcasted_iota(jnp.int32, (t, LANES), 1)
    lane1 = lax.broadcasted_iota(jnp.int32, (1, LANES), 1)
    r2 = lax.broadcasted_iota(jnp.int32, (t, t), 0)
    c2 = lax.broadcasted_iota(jnp.int32, (t, t), 1)
    causal = c2 <= r2
    pad_gate = jnp.where(lane < NH_M, NEG, -NEG)

    seqs = []
    for s in range(nb):
        ga = _gate_algebra(jnp.where(row < t_in, g_src[s], pad_gate), m_ref[s], t)
        ga["gt"] = ga["gg"].T
        m_ref[s] = jnp.where(lane1 < NH_M, ga["m_new"], 0.0)
        seqs.append(ga)

    units = [(s, hd) for s in range(nb) for hd in range(NH_M)]
    nu = len(units)
    col = lambda name, s, hd: seqs[s][name][:, hd:hd + 1]
    q = [qk_src[s, :, hd * DHK_M:(hd + 1) * DHK_M].astype(BF16) for s, hd in units]
    k = [qk_src[s, :, DQK_M + hd * DHK_M:DQK_M + (hd + 1) * DHK_M].astype(BF16) for s, hd in units]
    v = [v_src[s, :, hd * DHV_M:(hd + 1) * DHV_M].astype(BF16) for s, hd in units]
    c_old = [c_ref[s, hd] for s, hd in units]
    n_old = [n_ref[s, hd:hd + 1, :] for s, hd in units]

    sqk = [_dot_nt(q[u], k[u]) for u in range(nu)]
    dexp = [jnp.exp(jnp.where(causal, col("eb", s, hd) + seqs[s]["gt"][hd:hd + 1, :], NEG))
            for s, hd in units]
    smat = [sqk[u] * dexp[u] for u in range(nu)]
    intra = [_dot(smat[u].astype(BF16), v[u]) for u in range(nu)]
    inter = [_dot_nt(q[u], c_old[u].astype(BF16)) for u in range(nu)]
    qn = [jnp.sum(q[u].astype(F32) * n_old[u].astype(BF16).astype(F32), axis=-1, keepdims=True)
          for u in range(nu)]
    den = [jnp.sum(smat[u], axis=-1, keepdims=True) + col("a_in", s, hd) * qn[u]
           for u, (s, hd) in enumerate(units)]
    hh = [(intra[u] + col("a_in", s, hd) * inter[u])
          / jnp.maximum(jnp.abs(den[u]), col("lowb", s, hd)) for u, (s, hd) in enumerate(units)]
    hn = [_ln(hh[u]) * nw_ref[:, hd * DHV_M:(hd + 1) * DHV_M] for u, (s, hd) in enumerate(units)]
    for u, (s, hd) in enumerate(units):
        h_ref[:, s, hd * DHV_M:(hd + 1) * DHV_M] = hn[u][:t_in]

    kw = [k[u].astype(F32) * col("ws", s, hd) for u, (s, hd) in enumerate(units)]
    upd = [_dot_tn(v[u], kw[u].astype(BF16)) for u in range(nu)]
    for u, (s, hd) in enumerate(units):
        dec = seqs[s]["decay"][:, hd:hd + 1]
        c_ref[s, hd] = dec * c_old[u] + upd[u]
        n_ref[s, hd:hd + 1, :] = dec * n_old[u] + jnp.sum(kw[u], axis=0, keepdims=True)


def _mlstm_sample(qk, v, g, c0, n0, m0, norm_w, nb):
    t_in, nseq, _ = qk.shape
    assert t_in <= SUBLANES
    blk = lambda n: pl.BlockSpec((t_in, nb, n), lambda o, c: (0, o, 0))
    st_c = pl.BlockSpec((nb, NH_M, DHV_M, DHK_M), lambda o, c: (o, 0, 0, 0))
    st_n = pl.BlockSpec((nb, NH_M, DHK_M), lambda o, c: (o, 0, 0))
    st_m = pl.BlockSpec((nb, 1, LANES), lambda o, c: (o, 0, 0))
    return pl.pallas_call(
        functools.partial(_mlstm_sample_body, t_in, nb),
        out_shape=[jax.ShapeDtypeStruct((t_in, nseq, DV_M), F32),
                   jax.ShapeDtypeStruct((nseq, NH_M, DHV_M, DHK_M), F32),
                   jax.ShapeDtypeStruct((nseq, NH_M, DHK_M), F32),
                   jax.ShapeDtypeStruct((nseq, 1, LANES), F32)],
        grid=(nseq // nb, 1),
        in_specs=[blk(2 * DQK_M), blk(DV_M), blk(LANES), st_c, st_n, st_m,
                  pl.BlockSpec((1, DV_M), lambda o, c: (0, 0))],
        out_specs=[blk(DV_M), st_c, st_n, st_m],
        scratch_shapes=[pltpu.VMEM((nb, SUBLANES, 2 * DQK_M), F32), pltpu.VMEM((nb, SUBLANES, DV_M), F32),
                        pltpu.VMEM((nb, SUBLANES, LANES), F32)],
        compiler_params=_params(2),
        name="mlstm_sample",
    )(qk, v, g, c0, n0, m0, norm_w)


def _gate_algebra(g, m_prev, t):
    b = pltpu.roll(_scan_rows(jax.nn.log_sigmoid(g), jnp.add, t), LANES - NH_M, axis=1)
    gg = g - b
    gmx = _scan_rows(gg, jnp.maximum, t)
    a = b + m_prev
    mt = jnp.maximum(a, b + gmx)
    b_last, gmx_last = b[t - 1:t, :], gmx[t - 1:t, :]
    m_new = jnp.maximum(b_last + m_prev, b_last + gmx_last)
    return dict(gg=gg, eb=b - mt, a_in=jnp.exp(a - mt), lowb=jnp.exp(-mt),
                ws=jnp.exp(b_last + gg - m_new), decay=jnp.exp(b_last + m_prev - m_new), m_new=m_new)


MLSTM_CHUNKS_PER_STEP = 4


def _mlstm_prompt_body(nb, qk_ref, vt_ref, g_ref, nwc_ref, h_ref, c_ref, n_ref, m_ref, nwb):
    @pl.when(pl.program_id(1) == 0)
    def _():
        c_ref[...] = jnp.zeros(c_ref.shape, F32)
        n_ref[...] = jnp.zeros(n_ref.shape, F32)
        m_ref[...] = jnp.zeros(m_ref.shape, F32)

    @pl.when(_first_step())
    def _():
        for hd in range(NH_M):
            nwb[hd] = jnp.broadcast_to(nwc_ref[hd * DHV_M:(hd + 1) * DHV_M, :], (DHV_M, CHUNK))

    for ci in range(MLSTM_CHUNKS_PER_STEP):
        _mlstm_prompt_chunk(nb, slice(ci * CHUNK, (ci + 1) * CHUNK), qk_ref, vt_ref, g_ref, nwb,
                            h_ref, c_ref, n_ref, m_ref)


def _mlstm_prompt_chunk(nb, rows, qk_ref, vt_ref, g_ref, nwb, h_ref, c_ref, n_ref, m_ref):
    t = CHUNK
    lane1 = lax.broadcasted_iota(jnp.int32, (1, LANES), 1)
    r2 = lax.broadcasted_iota(jnp.int32, (t, t), 0)
    c2 = lax.broadcasted_iota(jnp.int32, (t, t), 1)
    causal = r2 <= c2

    seqs = []
    for s in range(nb):
        ga = _gate_algebra(g_ref[s, rows, :], m_ref[s], t)
        m_ref[s] = jnp.where(lane1 < NH_M, ga["m_new"], 0.0)
        for name in ("eb", "a_in", "lowb"):
            ga[name + "_t"] = ga[name].T
        seqs.append(ga)

    units = [(s, hd) for s in range(nb) for hd in range(NH_M)]
    nu = len(units)
    rowv = lambda name, s, hd: seqs[s][name + "_t"][hd:hd + 1, :]
    q = [qk_ref[s, rows, hd * DHK_M:(hd + 1) * DHK_M] for s, hd in units]
    k = [qk_ref[s, rows, DQK_M + hd * DHK_M:DQK_M + (hd + 1) * DHK_M] for s, hd in units]
    vt = [vt_ref[s, hd * DHV_M:(hd + 1) * DHV_M, rows] for s, hd in units]
    c_old = [c_ref[s, hd] for s, hd in units]
    n_old = [n_ref[s, hd:hd + 1, :] for s, hd in units]

    skq = [_dot_nt(k[u], q[u]) for u in range(nu)]
    dexp = [jnp.exp(jnp.where(causal, seqs[s]["gg"][:, hd:hd + 1] + rowv("eb", s, hd), NEG))
            for s, hd in units]
    smat = [skq[u] * dexp[u] for u in range(nu)]
    intra = [_dot(vt[u], smat[u].astype(BF16)) for u in range(nu)]
    inter = [_dot_nt(c_old[u].astype(BF16), q[u]) for u in range(nu)]
    qn = [_dot_nt(jnp.broadcast_to(n_old[u], (SUBLANES, DHK_M)).astype(BF16), q[u])[0:1, :]
          for u in range(nu)]
    den = [jnp.sum(smat[u], axis=0, keepdims=True) + rowv("a_in", s, hd) * qn[u]
           for u, (s, hd) in enumerate(units)]
    inv = [1.0 / jnp.maximum(jnp.abs(den[u]), rowv("lowb", s, hd)) for u, (s, hd) in enumerate(units)]
    hh = [(intra[u] + rowv("a_in", s, hd) * inter[u]) * inv[u] for u, (s, hd) in enumerate(units)]
    mu = [jnp.mean(hh[u], axis=0, keepdims=True) for u in range(nu)]
    xc = [hh[u] - mu[u] for u in range(nu)]
    var = [jnp.mean(xc[u] * xc[u], axis=0, keepdims=True) for u in range(nu)]
    hn = [xc[u] * lax.rsqrt(var[u] + LN_EPS) * nwb[hd] for u, (s, hd) in enumerate(units)]
    for u, (s, hd) in enumerate(units):
        h_ref[s, rows, hd * DHV_M:(hd + 1) * DHV_M] = hn[u].T

    kw = [(k[u].astype(F32) * seqs[s]["ws"][:, hd:hd + 1]).astype(BF16) for u, (s, hd) in enumerate(units)]
    upd = [_dot(vt[u], kw[u]) for u in range(nu)]
    nupd = [_dot(jnp.ones((SUBLANES, t), BF16), kw[u])[0:1, :] for u in range(nu)]
    for u, (s, hd) in enumerate(units):
        dec = seqs[s]["decay"][:, hd:hd + 1]
        c_ref[s, hd] = dec * c_old[u] + upd[u]
        n_ref[s, hd:hd + 1, :] = dec * n_old[u] + nupd[u]


def _mlstm_prompt(qk, vt, g, norm_w_col):
    nb, seq, _ = qk.shape
    ts = MLSTM_CHUNKS_PER_STEP * CHUNK
    tok = lambda n: pl.BlockSpec((nb, ts, n), lambda o, c: (0, c, 0))
    st_c = pl.BlockSpec((nb, NH_M, DHV_M, DHK_M), lambda o, c: (0, 0, 0, 0))
    st_n = pl.BlockSpec((nb, NH_M, DHK_M), lambda o, c: (0, 0, 0))
    st_m = pl.BlockSpec((nb, 1, LANES), lambda o, c: (0, 0, 0))
    return pl.pallas_call(
        functools.partial(_mlstm_prompt_body, nb),
        out_shape=[jax.ShapeDtypeStruct((nb, seq, DV_M), F32),
                   jax.ShapeDtypeStruct((nb, NH_M, DHV_M, DHK_M), F32),
                   jax.ShapeDtypeStruct((nb, NH_M, DHK_M), F32),
                   jax.ShapeDtypeStruct((nb, 1, LANES), F32)],
        grid=(1, seq // ts),
        in_specs=[tok(2 * DQK_M), pl.BlockSpec((nb, DV_M, ts), lambda o, c: (0, 0, c)), tok(LANES),
                  pl.BlockSpec((DV_M, 1), lambda o, c: (0, 0))],
        out_specs=[tok(DV_M), st_c, st_n, st_m],
        scratch_shapes=[pltpu.VMEM((NH_M, DHV_M, CHUNK), F32)],
        compiler_params=_params(2),
        name="mlstm_prompt",
    )(qk, vt, g, norm_w_col)


SWA_BLOCKS_PER_STEP = 16


def _swa_prompt_body(prm_ref, q_ref, kc_ref, kp_ref, vc_ref, vp_ref, o_ref, tbl):
    row = lax.broadcasted_iota(jnp.int32, (WINDOW, WINDOW), 0)
    col = lax.broadcasted_iota(jnp.int32, (WINDOW, WINDOW), 1)
    tri = row <= col

    @pl.when(_first_step())
    def _():
        dist = jnp.where(tri, col - row, col - row + WINDOW).astype(F32)
        for hd in range(NH_A):
            bias = prm_ref[0, hd] * dist
            tbl[0, hd] = bias
            tbl[1, hd] = bias + jnp.where(tri, 0.0, -NEG)

    for b in range(SWA_BLOCKS_PER_STEP):
        rows = pl.ds(b * WINDOW, WINDOW)
        which = jnp.where(pl.program_id(1) == 0, 1, 0) if b == 0 else 0
        k_prev = kp_ref if b == 0 else kc_ref.at[pl.ds((b - 1) * WINDOW, WINDOW), :]
        v_prev = vp_ref if b == 0 else vc_ref.at[:, pl.ds((b - 1) * WINDOW, WINDOW)]
        _swa_prompt_block(which, prm_ref, q_ref.at[rows, :], kc_ref.at[rows, :], k_prev,
                          vc_ref.at[:, rows], v_prev, o_ref.at[rows, :], tbl)


def _swa_prompt_block(which, prm_ref, q_ref, kc_ref, kp_ref, vc_ref, vp_ref, o_ref, tbl):
    row = lax.broadcasted_iota(jnp.int32, (WINDOW, WINDOW), 0)
    col = lax.broadcasted_iota(jnp.int32, (WINDOW, WINDOW), 1)
    tri = row <= col
    lo = col < HD_A
    zb = jnp.zeros((WINDOW, LANES), BF16)

    def placed(k_ref, kv):
        c, par = kv // 2, kv % 2
        own = k_ref[:, c * LANES:(c + 1) * LANES]
        swp = pltpu.roll(own, HD_A, axis=1)
        if par == 0:
            return jnp.where(lo, own, zb), jnp.where(lo, zb, swp)
        return jnp.where(lo, swp, zb), jnp.where(lo, zb, own)

    st = []
    for kv in range(NKV_A):
        lhs = jnp.concatenate([*placed(kc_ref, kv), *placed(kp_ref, kv)], axis=0)
        qg = jnp.concatenate([q_ref[:, (2 * kv) * LANES:(2 * kv + 1) * LANES],
                              q_ref[:, (2 * kv + 1) * LANES:(2 * kv + 2) * LANES]], axis=0)
        st.append(_dot_nt(lhs, qg))
    tiles = []
    for hd in range(NH_A):
        kv, a, par = hd // GROUP_A, (hd % GROUP_A) // 2, hd % 2
        cols = slice(a * WINDOW, (a + 1) * WINDOW)
        tiles.append(jnp.where(tri, st[kv][par * WINDOW:(par + 1) * WINDOW, cols],
                               st[kv][(2 + par) * WINDOW:(3 + par) * WINDOW, cols]))
    sc = jnp.concatenate(tiles, axis=0).reshape(NH_A, WINDOW, WINDOW) - tbl[which]
    sink = jnp.concatenate([jnp.full((1, 1, WINDOW), prm_ref[1, hd], F32) for hd in range(NH_A)], axis=0)
    mx = jnp.maximum(jnp.max(sc, axis=1, keepdims=True), sink)
    p = jnp.exp2(sc - mx)
    den = jnp.sum(p, axis=1, keepdims=True) + jnp.exp2(sink - mx)
    pn = p * (1.0 / den)

    zero = jnp.zeros((WINDOW, WINDOW), F32)
    z64 = jnp.zeros((HD_A, WINDOW), BF16)
    outs = []
    for kv in range(NKV_A):
        cols = []
        for a in range(2):
            pe, po = pn[kv * GROUP_A + 2 * a], pn[kv * GROUP_A + 2 * a + 1]
            cols.append(jnp.concatenate([jnp.where(tri, pe, zero), jnp.where(tri, po, zero),
                                         jnp.where(tri, zero, pe), jnp.where(tri, zero, po)],
                                        axis=0).astype(BF16))
        pt = jnp.concatenate(cols, axis=1)
        vc = vc_ref[kv * HD_A:(kv + 1) * HD_A, :].astype(BF16)
        vp = vp_ref[kv * HD_A:(kv + 1) * HD_A, :].astype(BF16)
        vt = jnp.concatenate([jnp.concatenate([vc, z64], axis=0), jnp.concatenate([z64, vc], axis=0),
                              jnp.concatenate([vp, z64], axis=0), jnp.concatenate([z64, vp], axis=0)],
                             axis=1)
        ot = _dot(vt, pt)
        outs += [ot[:, :WINDOW].T, ot[:, WINDOW:].T]
    o_ref[...] = jnp.concatenate(outs, axis=1).astype(o_ref.dtype)


def _swa_prompt(prm, q, kb, vat):
    bsz, seq, _ = q.shape
    tq = SWA_BLOCKS_PER_STEP * WINDOW
    prev = lambda i: jnp.maximum(i * SWA_BLOCKS_PER_STEP - 1, 0)
    return pl.pallas_call(
        _swa_prompt_body,
        out_shape=jax.ShapeDtypeStruct((bsz, seq, DQ_A), BF16),
        grid=(bsz, seq // tq),
        in_specs=[pl.BlockSpec(memory_space=pltpu.SMEM),
                  pl.BlockSpec((None, tq, DQ_A), lambda b, i: (b, i, 0)),
                  pl.BlockSpec((None, tq, DKV_A), lambda b, i: (b, i, 0)),
                  pl.BlockSpec((None, WINDOW, DKV_A), lambda b, i: (b, prev(i), 0)),
                  pl.BlockSpec((None, DKV_A, tq), lambda b, i: (b, 0, i)),
                  pl.BlockSpec((None, DKV_A, WINDOW), lambda b, i: (b, 0, prev(i)))],
        out_specs=pl.BlockSpec((None, tq, DQ_A), lambda b, i: (b, i, 0)),
        scratch_shapes=[pltpu.VMEM((2, NH_A, WINDOW, WINDOW), F32)],
        compiler_params=_params(2),
        name="swa_prompt",
    )(prm, q, kb, kb, vat, vat)


def _half_mask(shape, half):
    lane = lax.broadcasted_iota(jnp.int32, shape, 1)
    return lane < HD_A if half == 0 else lane >= HD_A


def _swa_sample_body(t_in, nb, prm_ref, q_ref, kn_ref, vn_ref, kct_ref, vct_ref,
                     o_ref, kco_ref, vco_ref, tbl, q8, kn_pad, vn_pad):
    tq = SUBLANES
    rows = NH_A * tq
    row = lax.broadcasted_iota(jnp.int32, (rows, WINDOW), 0)
    col = lax.broadcasted_iota(jnp.int32, (rows, WINDOW), 1)
    tri = col <= (row & (tq - 1))

    @pl.when(_first_step())
    def _():
        r8 = lax.broadcasted_iota(jnp.int32, (tq, WINDOW), 0)
        c8 = lax.broadcasted_iota(jnp.int32, (tq, WINDOW), 1)
        dist = jnp.where(c8 <= r8, r8 - c8, r8 - c8 + WINDOW).astype(F32)
        for hd in range(NH_A):
            tbl[pl.ds(hd * tq, tq), :] = prm_ref[0, hd] * dist
        for p in (q8, kn_pad, vn_pad):
            p[...] = jnp.zeros(p.shape, p.dtype)

    for s in range(nb):
        q8[s, pl.ds(0, t_in), :] = q_ref[:, s, :]
        kn_pad[s, pl.ds(0, t_in), :] = kn_ref[:, s, :]
        vn_pad[s, pl.ds(0, t_in), :] = vn_ref[:, s, :]
    sink_col = jnp.concatenate([jnp.full((tq, 1), prm_ref[1, hd], F32) for hd in range(NH_A)], axis=0)
    z8 = jnp.zeros((tq, LANES), F32)
    n_chunk = NKV_A // 2
    heads_per_chunk = NH_A // n_chunk

    def place(piece, src_half, dst_half):
        if src_half != dst_half:
            piece = pltpu.roll(piece, HD_A, axis=1)
        return jnp.where(_half_mask(piece.shape, dst_half), piece, z8)

    qexp = []
    for s in range(nb):
        per_c = []
        for c in range(n_chunk):
            pieces = []
            for hl in range(heads_per_chunk):
                hd = c * heads_per_chunk + hl
                pieces.append(place(q8[s, :, (hd // 2) * LANES:(hd // 2 + 1) * LANES],
                                    hd % 2, hl // GROUP_A))
            per_c.append(jnp.concatenate(pieces, axis=0).astype(BF16))
        qexp.append(per_c)
    csl = lambda c: slice(c * LANES, (c + 1) * LANES)
    s_prev = [[_dot(qexp[s][c], kct_ref[s, csl(c), :].astype(BF16)) for c in range(n_chunk)]
              for s in range(nb)]
    s_cur = [[_dot_nt(qexp[s][c], kn_pad[s, :, csl(c)].astype(BF16)) for c in range(n_chunk)]
             for s in range(nb)]
    sc = [jnp.where(tri, jnp.concatenate(s_cur[s], axis=0), jnp.concatenate(s_prev[s], axis=0)) - tbl[...]
          for s in range(nb)]
    mx = [jnp.maximum(jnp.max(sc[s], axis=-1, keepdims=True), sink_col) for s in range(nb)]
    p = [jnp.exp2(sc[s] - mx[s]) for s in range(nb)]
    den = [jnp.sum(p[s], axis=-1, keepdims=True) + jnp.exp2(sink_col - mx[s]) for s in range(nb)]
    pn = [p[s] * (1.0 / den[s]) for s in range(nb)]
    zero = jnp.zeros((rows, WINDOW), F32)
    pc = [jnp.where(tri, pn[s], zero).astype(BF16) for s in range(nb)]
    pp = [jnp.where(tri, zero, pn[s]).astype(BF16) for s in range(nb)]
    half_rows = heads_per_chunk * tq
    oc = [[_dot_nt(pp[s][c * half_rows:(c + 1) * half_rows], vct_ref[s, csl(c), :].astype(BF16))
           + _dot(pc[s][c * half_rows:(c + 1) * half_rows], vn_pad[s, :, csl(c)].astype(BF16))
           for c in range(n_chunk)] for s in range(nb)]
    for s in range(nb):
        chunks = []
        for pch in range(NH_A // 2):
            acc = None
            for hd in (2 * pch, 2 * pch + 1):
                c, hl = hd // heads_per_chunk, hd % heads_per_chunk
                piece = place(oc[s][c][hl * tq:(hl + 1) * tq, :], hl // GROUP_A, hd % 2)
                acc = piece if acc is None else acc + piece
            chunks.append(acc)
        o_ref[:, s, :] = jnp.concatenate(chunks, axis=1)[:t_in]

    lane = lax.broadcasted_iota(jnp.int32, (DKV_A, WINDOW), 1)
    for new_pad, old_ref, out_ref in ((kn_pad, kct_ref, kco_ref), (vn_pad, vct_ref, vco_ref)):
        for s in range(nb):
            merged = jnp.where(lane < t_in, new_pad[s].T, old_ref[s])
            out_ref[s] = pltpu.roll(merged, WINDOW - t_in, axis=1)


def _swa_sample(prm, q, kn, vn, kct, vct, nb):
    t_in, nseq, _ = q.shape
    assert t_in <= SUBLANES
    cur = lambda n: pl.BlockSpec((t_in, nb, n), lambda o, i: (0, o, 0))
    win = pl.BlockSpec((nb, DKV_A, WINDOW), lambda o, i: (o, 0, 0))
    return pl.pallas_call(
        functools.partial(_swa_sample_body, t_in, nb),
        out_shape=[jax.ShapeDtypeStruct((t_in, nseq, DQ_A), F32),
                   jax.ShapeDtypeStruct((nseq, DKV_A, WINDOW), F32),
                   jax.ShapeDtypeStruct((nseq, DKV_A, WINDOW), F32)],
        grid=(nseq // nb, 1),
        in_specs=[pl.BlockSpec(memory_space=pltpu.SMEM), cur(DQ_A), cur(DKV_A), cur(DKV_A), win, win],
        out_specs=[cur(DQ_A), win, win],
        scratch_shapes=[pltpu.VMEM((NH_A * SUBLANES, WINDOW), F32),
                        pltpu.VMEM((nb, SUBLANES, DQ_A), F32),
                        pltpu.VMEM((nb, WINDOW, DKV_A), F32),
                        pltpu.VMEM((nb, WINDOW, DKV_A), F32)],
        compiler_params=_params(2),
        name="swa_sample",
    )(prm, q, kn, vn, kct, vct)


def _merge_body(alpha, x_ref, h_ref, g1_ref, hm_ref, ha_ref, wgt_ref, bgt_ref,
                wbm_ref, wba_ref, wo_ref, lg_ref, lb_ref, o_ref):
    nsub = 2
    sub = x_ref.shape[0] // nsub
    rows = [pl.ds(r * sub, sub) for r in range(nsub)]
    mod = lambda ref, r: _mod_rows(ref, r * sub, sub)
    gate = lambda g, j: g[:, j * D_MODEL:(j + 1) * D_MODEL]

    x = [x_ref[rows[r], :] for r in range(nsub)]
    h = [h_ref[rows[r], :] for r in range(nsub)]
    g = [jax.nn.sigmoid(_dot(h[r], wgt_ref[...]) + bgt_ref[...]) for r in range(nsub)]
    hm = [(hm_ref[rows[r], :] * gate(g[r], 0)).astype(BF16) for r in range(nsub)]
    bm = [_dot(hm[r], wbm_ref[...]) for r in range(nsub)]
    ba = [_dot(ha_ref[rows[r], :].astype(BF16), wba_ref[...]) for r in range(nsub)]
    merged = [(gate(g[r], 1) * bm[r] + gate(g[r], 2) * ba[r]).astype(BF16) for r in range(nsub)]
    mo = [_dot(merged[r], wo_ref[...]) for r in range(nsub)]
    for r in range(nsub):
        o_ref[rows[r], :] = _ln(alpha * x[r] + mod(g1_ref, r) * mo[r]) * lg_ref[...] + lb_ref[...]


def _merge(x, h, g1, hm, ha, w, tm, alpha):
    bsz, seq, _ = x.shape
    g1, g1_spec = _mod_operand(g1)
    weights = [w["wgates"], w["bgates"], w["wbm"], w["wba"], w["wo"],
               w["ln1_g"], w["ln1_b"]]
    return pl.pallas_call(
        functools.partial(_merge_body, alpha),
        out_shape=jax.ShapeDtypeStruct((bsz, seq, D_MODEL), F32),
        grid=(bsz, seq // tm),
        in_specs=[_tok_spec(tm, D_MODEL), _tok_spec(tm, D_MODEL), g1_spec,
                  _tok_spec(tm, DV_M), _tok_spec(tm, DQ_A)]
                 + [_const_spec(a.shape) for a in weights],
        out_specs=_tok_spec(tm, D_MODEL),
        compiler_params=_params(2),
        name="merge",
    )(x, h, g1, hm, ha, *weights)


def _ffn_body(alpha, tm, stride, halo, x_ref, sh_ref, sc_ref, g2_ref, cb0_ref, wup_ref, bup_ref,
              cw_ref, cbias_ref, wdn_ref, bdn_ref, lg_ref, lb_ref, o_ref, cs_ref, ubuf, act):
    @pl.when(pl.program_id(1) == 0)
    def _():
        cs_ref[...] = cb0_ref[...]

    nsub = 2
    sub = tm // nsub
    rows = [pl.ds(r * sub, sub) for r in range(nsub)]
    mod = lambda ref, r: _mod_rows(ref, r * sub, sub)
    x = [x_ref[rows[r], :] for r in range(nsub)]
    h = [(_ln(x[r]) * (1.0 + mod(sc_ref, r)) + mod(sh_ref, r)).astype(BF16) for r in range(nsub)]
    ys = [[None, None] for _ in range(nsub)]
    for half in range(2):
        cols = slice(half * D_FF, (half + 1) * D_FF)
        u = [_dot(h[r], wup_ref[:, cols]) + bup_ref[:, cols] for r in range(nsub)]
        ubuf[pl.ds(0, halo), :] = cs_ref[:, cols]
        for r in range(nsub):
            ubuf[pl.ds(halo + r * sub, sub), :] = u[r]
        cs_ref[:, cols] = ubuf[pl.ds(tm, halo), :]
        for r in range(nsub):
            y = cbias_ref[:, cols] + u[r] * cw_ref[CONV_W - 1:CONV_W, cols]
            for j in range(CONV_W - 1):
                tap = ubuf[pl.ds(halo + r * sub - (CONV_W - 1 - j) * stride, sub), :]
                y = y + tap * cw_ref[j:j + 1, cols]
            ys[r][half] = y
    for r in range(nsub):
        act[rows[r], :] = (jax.nn.gelu(ys[r][0]) * ys[r][1]).astype(BF16)
    f = [_dot(act[rows[r], :], wdn_ref[...]) + bdn_ref[...] for r in range(nsub)]
    for r in range(nsub):
        o_ref[rows[r], :] = _ln(alpha * x[r] + mod(g2_ref, r) * f[r]) * lg_ref[...] + lb_ref[...]


def _ffn(x, sh, sc, g2, cb0, w, tm, stride, alpha):
    bsz, seq, _ = x.shape
    (sh, sh_spec), (sc, sc_spec), (g2, g2_spec) = _mod_operand(sh), _mod_operand(sc), _mod_operand(g2)
    halo = cb0.shape[1]
    cs = pl.BlockSpec((None, halo, 2 * D_FF), lambda b, i: (b, 0, 0))
    weights = [w["wup"], w["bup"], w["cw"], w["cbias"], w["wdn"], w["bdn"], w["ln2_g"], w["ln2_b"]]
    return pl.pallas_call(
        functools.partial(_ffn_body, alpha, tm, stride, halo),
        out_shape=[jax.ShapeDtypeStruct((bsz, seq, D_MODEL), F32),
                   jax.ShapeDtypeStruct((bsz, halo, 2 * D_FF), F32)],
        grid=(bsz, seq // tm),
        in_specs=[_tok_spec(tm, D_MODEL), sh_spec, sc_spec, g2_spec, cs]
                 + [_const_spec(a.shape) for a in weights],
        out_specs=[_tok_spec(tm, D_MODEL), cs],
        scratch_shapes=[pltpu.VMEM((halo + tm, D_FF), F32), pltpu.VMEM((tm, D_FF), BF16)],
        compiler_params=_params(2),
        name="ffn",
    )(x, sh, sc, g2, cb0, *weights)


_O_GATE = 2 * DQK_M + DV_M
_O_OG = _O_GATE + 2 * NH_M
_O_QA = _O_OG + DV_M
_O_KA = _O_QA + DQ_A
_O_VA = _O_KA + DKV_A
_O_GM = _O_VA + DKV_A
_PREP_ROWS = 256


def _split_w_in_body(wt_ref, qk_ref, vm_ref, vmt_ref, g_ref, gates_ref, qa_ref, k_ref, v_ref, vt_ref):
    piece = lambda lo, hi: wt_ref[lo:hi, :]
    qk_ref[...] = piece(0, 2 * DQK_M).T.astype(BF16)
    vm = piece(2 * DQK_M, _O_GATE)
    vmt_ref[...] = vm.astype(BF16)
    vm_ref[...] = vm.T.astype(BF16)
    pad = jnp.zeros((LANES - 2 * NH_M, _PREP_ROWS), F32)
    g_ref[...] = jnp.concatenate([piece(_O_GATE, _O_OG), pad], axis=0).T.astype(BF16)
    gates_ref[:, :DV_M] = piece(_O_OG, _O_QA).T.astype(BF16)
    gates_ref[:, DV_M:] = piece(_O_GM, wt_ref.shape[0]).T.astype(BF16)
    qa_ref[...] = piece(_O_QA, _O_KA).T.astype(BF16)
    k_ref[...] = piece(_O_KA, _O_VA).T.astype(BF16)
    v = piece(_O_VA, _O_GM)
    vt_ref[...] = v.astype(BF16)
    v_ref[...] = v.T.astype(BF16)


def _split_w_in(w_in_t):
    d_in = w_in_t.shape[0]
    rows = lambda n: ((D_MODEL, n), pl.BlockSpec((_PREP_ROWS, n), lambda i: (i, 0)))
    cols = lambda n: ((n, D_MODEL), pl.BlockSpec((n, _PREP_ROWS), lambda i: (0, i)))
    outs = dict(wqk=rows(2 * DQK_M), wvm=rows(DV_M), wvm_t=cols(DV_M), wg=rows(LANES),
                wgates=rows(DV_M + d_in - _O_GM), wqa=rows(DQ_A), wk=rows(DKV_A), wv=rows(DKV_A),
                wv_t=cols(DKV_A))
    res = pl.pallas_call(
        _split_w_in_body,
        out_shape=[jax.ShapeDtypeStruct(s, BF16) for s, _ in outs.values()],
        grid=(D_MODEL // _PREP_ROWS,),
        in_specs=[pl.BlockSpec((d_in, _PREP_ROWS), lambda i: (0, i))],
        out_specs=[spec for _, spec in outs.values()],
        compiler_params=_params(1),
        name="split_w_in",
    )(w_in_t)
    return dict(zip(outs.keys(), res))


def _prep_weights(w_in, b_in, mlstm_norm_w, w_branch_m, w_branch_a, w_out, ln1_g, ln1_b,
                  w_up, b_up, conv_w, conv_b, w_down, b_down, ln2_g, ln2_b):
    row = lambda a: a.reshape(1, -1)
    gate_pad = LANES - 2 * NH_M
    b_k, b_v = b_in[_O_KA:_O_VA], b_in[_O_VA:_O_GM]
    return dict(
        **_split_w_in(w_in.T),
        bqk=row(b_in[:2 * DQK_M]),
        bvm_row=row(b_in[2 * DQK_M:_O_GATE]), bvm_col=b_in[2 * DQK_M:_O_GATE].reshape(-1, 1),
        bg=row(jnp.pad(b_in[_O_GATE:_O_OG], (0, gate_pad))),
        bqa=row(b_in[_O_QA:_O_KA]),
        bk=row(b_k),
        bv_row=row(b_v), bv_col=b_v.reshape(-1, 1),
        bgates=row(jnp.concatenate([b_in[_O_OG:_O_QA], b_in[_O_GM:]])),
        norm_w=row(mlstm_norm_w), norm_w_col=mlstm_norm_w.reshape(-1, 1),
        wbm=w_branch_m.astype(BF16), wba=w_branch_a.astype(BF16),
        wo=w_out.astype(BF16), ln1_g=row(ln1_g), ln1_b=row(ln1_b),
        wup=w_up.astype(BF16), bup=row(b_up), cw=conv_w, cbias=row(conv_b),
        wdn=w_down.astype(BF16), bdn=row(b_down), ln2_g=row(ln2_g), ln2_b=row(ln2_b))


def kernel(x_prompt, x_sample, c_prompt, c_sample, state_mlstm_C, state_mlstm_n, state_mlstm_m,
           cache_k_win, cache_v_win, state_ffn_conv, w_ada, b_ada, w_in, b_in, mlstm_norm_w,
           attn_sinks, w_branch_m, w_branch_a, w_out, ln1_g, ln1_b, w_up, b_up, conv_w, conv_b,
           w_down, b_down, ln2_g, ln2_b):
    depth = w_in.shape[0]
    bp, lp, _ = x_prompt.shape
    bs, ls, _ = x_sample.shape
    assert cache_k_win.shape[2] == WINDOW
    alpha = (2 * depth) ** 0.25
    dt = x_prompt.dtype
    slopes = jnp.exp2(-8.0 * jnp.arange(1, NH_A + 1, dtype=F32) / NH_A)
    tm_p = 512
    tm_in = 1024
    ns = bs * ls

    yp = x_prompt
    to_pos = lambda a: a.transpose(1, 0, 2).reshape(1, -1, a.shape[-1])
    ys = to_pos(x_sample)
    new_p, new_s = [], []
    n_c = bp + bs
    c_rows = -(-n_c // SUBLANES) * SUBLANES
    c_all = jnp.concatenate([c_sample, c_prompt, jnp.zeros((c_rows - n_c, D_MODEL), dt)], axis=0)
    for l in range(depth):
        w = _prep_weights(w_in[l], b_in[l], mlstm_norm_w[l], w_branch_m[l], w_branch_a[l], w_out[l],
                          ln1_g[l], ln1_b[l], w_up[l], b_up[l], conv_w[l], conv_b[l], w_down[l],
                          b_down[l], ln2_g[l], ln2_b[l])
        prm = jnp.stack([slopes, attn_sinks[l].astype(F32)]) * LOG2E
        mod = _ada(c_all, w_ada[l], b_ada[l])
        mod_p = mod[bs:bs + bp].reshape(bp, N_MOD, D_MODEL)
        mp_ = [mod_p[:, j:j + 1] for j in range(N_MOD)]
        ms_ = [(mod, j, bs) for j in range(N_MOD)]

        hp, qk, v, g, qa, kb, ka, vat = _inproj(yp, mp_[0], mp_[1], w, tm_in, True)
        hm, cp, np_, mp = _mlstm_prompt(qk, v, g, w["norm_w_col"])
        ha = _swa_prompt(prm, qa, kb, vat)
        x1p = _merge(yp, hp, mp_[2], hm, ha, w, tm_p, alpha)
        halo_p = SUBLANES
        yp, csp = _ffn(x1p, mp_[3], mp_[4], mp_[5], jnp.zeros((bp, halo_p, 2 * D_FF), dt),
                       w, tm_p, 1, alpha)
        p_k = ka[:, lp - WINDOW:].reshape(bp, WINDOW, NKV_A, HD_A)
        p_v = vat[:, :, lp - WINDOW:].reshape(bp, NKV_A, HD_A, WINDOW).transpose(0, 3, 1, 2)
        new_p.append((cp, np_, mp[:, 0, :NH_M], p_k, p_v, csp[:, halo_p - (CONV_W - 1):]))

        hs, qk, v, g, qa, kn, vn = _inproj(ys, ms_[0], ms_[1], w, ns, False)
        per_seq = lambda a: a.reshape(ls, bs, a.shape[-1])
        m0 = jnp.pad(state_mlstm_m[l], ((0, 0), (0, LANES - NH_M)))[:, None, :]
        hm, cs_, ns_, ms = _mlstm_sample(per_seq(qk), per_seq(v), per_seq(g), state_mlstm_C[l],
                                         state_mlstm_n[l], m0, w["norm_w"], SEQS_PER_STEP)
        to_t = lambda a: a.transpose(0, 2, 3, 1).reshape(bs, DKV_A, WINDOW)
        from_t = lambda a: a.reshape(bs, NKV_A, HD_A, WINDOW).transpose(0, 3, 1, 2)
        ha, kct, vct = _swa_sample(prm, per_seq(qa), per_seq(kn), per_seq(vn),
                                   to_t(cache_k_win[l]), to_t(cache_v_win[l]), SWA_SEQS_PER_STEP)
        x1s = _merge(ys, hs, ms_[2], hm.reshape(1, ns, DV_M), ha.reshape(1, ns, DQ_A), w, ns, alpha)
        ys, css = _ffn(x1s, ms_[3], ms_[4], ms_[5], to_pos(state_ffn_conv[l]), w, ns, bs, alpha)
        conv_s = css.reshape(CONV_W - 1, bs, 2 * D_FF).transpose(1, 0, 2)
        new_s.append((cs_, ns_, ms[:, 0, :NH_M], from_t(kct), from_t(vct), conv_s))

    p_state = [jnp.stack(a) for a in zip(*new_p)]
    s_state = [jnp.stack(a) for a in zip(*new_s)]
    y_sample = ys.reshape(ls, bs, D_MODEL).transpose(1, 0, 2)
    return (yp, y_sample, *p_state, *s_state)
```

```python
import functools

import jax
import jax.numpy as jnp
from jax import lax
from jax.experimental import pallas as pl
from jax.experimental.pallas import tpu as pltpu

F32 = jnp.float32
BF16 = jnp.bfloat16

D_MODEL = 1024
NH_M, DHK_M, DHV_M = 4, 128, 256
DQK_M, DV_M = NH_M * DHK_M, NH_M * DHV_M
NH_A, NKV_A, HD_A = 16, 4, 64
GROUP_A = NH_A // NKV_A
WINDOW = 128
DQ_A, DKV_A = NH_A * HD_A, NKV_A * HD_A
D_FF = 2816
CONV_W = 3
N_MOD = 6
LN_EPS = 1e-5
CHUNK = 128
NEG = -1e30
LOG2E = 1.4426950408889634
LANES = 128
SUBLANES = 8
VMEM_LIMIT = 56 * 1024 * 1024
SEQS_PER_STEP = 16
SWA_SEQS_PER_STEP = 8


def _ln(x):
    mu = jnp.mean(x, axis=-1, keepdims=True)
    xc = x - mu
    var = jnp.mean(xc * xc, axis=-1, keepdims=True)
    return xc * lax.rsqrt(var + LN_EPS)


def _dot(a, b):
    return jnp.dot(a, b, preferred_element_type=F32)


def _dot_nt(a, b):
    return lax.dot_general(a, b, (((1,), (1,)), ((), ())), preferred_element_type=F32)


def _dot_tn(a, b):
    return lax.dot_general(a, b, (((0,), (0,)), ((), ())), preferred_element_type=F32)


def _const_spec(shape):
    nd = len(shape)
    return pl.BlockSpec(shape, lambda *_: (0,) * nd, pipeline_mode=pl.Buffered(1))


def _params(n_grid):
    return pltpu.CompilerParams(dimension_semantics=("arbitrary",) * n_grid,
                                vmem_limit_bytes=VMEM_LIMIT)


def _first_step():
    return (pl.program_id(0) == 0) & (pl.program_id(1) == 0)


def _ada_body(c_ref, w_ref, b_ref, o_ref):
    c = c_ref[...]
    s = (c * jax.nn.sigmoid(c)).astype(BF16)
    o_ref[...] = _dot(s, w_ref[...].astype(BF16)) + b_ref[...]


def _ada(c, w_ada, b_ada):
    rows = c.shape[0]
    n_out = w_ada.shape[1]
    bn = 3072
    assert n_out % bn == 0
    return pl.pallas_call(
        _ada_body,
        out_shape=jax.ShapeDtypeStruct((rows, n_out), F32),
        grid=(n_out // bn,),
        in_specs=[pl.BlockSpec((rows, D_MODEL), lambda j: (0, 0)),
                  pl.BlockSpec((D_MODEL, bn), lambda j: (0, j)),
                  pl.BlockSpec((1, bn), lambda j: (0, j))],
        out_specs=pl.BlockSpec((rows, bn), lambda j: (0, j)),
        compiler_params=_params(1),
        name="ada",
    )(c, w_ada, b_ada.reshape(1, n_out))


def _inproj_body(prompt, x_ref, sh_ref, sc_ref, wqk_ref, bqk_ref, wvm_ref, bvm_ref, wg_ref, bg_ref,
                 wqa_ref, bqa_ref, wk_ref, bk_ref, wv_ref, bv_ref, h_ref, qk_ref, v_ref, g_ref, qa_ref, *rest):
    tm = x_ref.shape[0]
    h = (_ln(x_ref[...]) * (1.0 + _mod_rows(sc_ref, 0, tm)) + _mod_rows(sh_ref, 0, tm)).astype(BF16)
    h_ref[...] = h

    def proj(w_ref, b_ref, lo, n):
        return _dot(h, w_ref[:, lo:lo + n]) + b_ref[:, lo:lo + n]

    qk_ref[:, :DQK_M] = (proj(wqk_ref, bqk_ref, 0, DQK_M) * DHK_M ** -0.5).astype(qk_ref.dtype)
    qk_ref[:, DQK_M:] = proj(wqk_ref, bqk_ref, DQK_M, DQK_M).astype(qk_ref.dtype)
    if prompt:
        v_ref[...] = (_dot_nt(wvm_ref[...], h) + bvm_ref[...]).astype(v_ref.dtype)
    else:
        v_ref[...] = proj(wvm_ref, bvm_ref, 0, DV_M).astype(v_ref.dtype)
    g_ref[...] = proj(wg_ref, bg_ref, 0, LANES)
    qa_ref[...] = (proj(wqa_ref, bqa_ref, 0, DQ_A) * (HD_A ** -0.5 * LOG2E)).astype(qa_ref.dtype)
    ka = proj(wk_ref, bk_ref, 0, DKV_A)
    if prompt:
        kb_ref, ka_ref, vat_ref = rest
        kb_ref[...] = ka.astype(BF16)
        ka_ref[...] = ka
        vat_ref[...] = _dot_nt(wv_ref[...], h) + bv_ref[...]
    else:
        ka_ref, va_ref = rest
        ka_ref[...] = ka
        va_ref[...] = _dot(h, wv_ref[...]) + bv_ref[...]


def _tok_spec(tm, n):
    return pl.BlockSpec((None, tm, n), lambda b, i: (b, i, 0))


def _mod_operand(m):
    if isinstance(m, tuple):
        arr, j, n = m
        return arr, pl.BlockSpec((n, D_MODEL), lambda b, i: (0, j))
    return m, pl.BlockSpec((None, 1, D_MODEL), lambda b, i: (b, 0, 0))


def _mod_rows(ref, start, n):
    period = ref.shape[0]
    if period == 1:
        return ref[...]
    assert start % period == 0 and n % period == 0
    return jnp.concatenate([ref[...]] * (n // period), axis=0)


def _inproj(x, sh, sc, w, tm, prompt):
    bsz, seq, _ = x.shape
    (sh, sh_spec), (sc, sc_spec) = _mod_operand(sh), _mod_operand(sc)
    act = BF16 if prompt else F32
    t_spec = lambda n: pl.BlockSpec((None, n, tm), lambda b, i: (b, 0, i))
    outs = [((bsz, seq, D_MODEL), BF16, _tok_spec(tm, D_MODEL)),
            ((bsz, seq, 2 * DQK_M), act, _tok_spec(tm, 2 * DQK_M)),
            ((bsz, DV_M, seq), act, t_spec(DV_M)) if prompt else ((bsz, seq, DV_M), act, _tok_spec(tm, DV_M)),
            ((bsz, seq, LANES), F32, _tok_spec(tm, LANES)),
            ((bsz, seq, DQ_A), act, _tok_spec(tm, DQ_A))]
    if prompt:
        wvm, bvm, wv, bv = w["wvm_t"], w["bvm_col"], w["wv_t"], w["bv_col"]
        outs += [((bsz, seq, DKV_A), BF16, _tok_spec(tm, DKV_A)),
                 ((bsz, seq, DKV_A), F32, _tok_spec(tm, DKV_A)),
                 ((bsz, DKV_A, seq), F32, t_spec(DKV_A))]
    else:
        wvm, bvm, wv, bv = w["wvm"], w["bvm_row"], w["wv"], w["bv_row"]
        outs += [((bsz, seq, DKV_A), F32, _tok_spec(tm, DKV_A)),
                 ((bsz, seq, DKV_A), F32, _tok_spec(tm, DKV_A))]
    weights = [w["wqk"], w["bqk"], wvm, bvm, w["wg"], w["bg"], w["wqa"], w["bqa"], w["wk"], w["bk"],
               wv, bv]
    return pl.pallas_call(
        functools.partial(_inproj_body, prompt),
        out_shape=[jax.ShapeDtypeStruct(s, dt) for s, dt, _ in outs],
        grid=(bsz, seq // tm),
        in_specs=[_tok_spec(tm, D_MODEL), sh_spec, sc_spec] + [_const_spec(a.shape) for a in weights],
        out_specs=[spec for _, _, spec in outs],
        compiler_params=_params(2),
        name="inproj",
    )(x, sh, sc, *weights)


def _scan_rows(x, op, rows):
    row = lax.broadcasted_iota(jnp.int32, x.shape, 0)
    d = 1
    while d < rows:
        shifted = pltpu.roll(x, d, axis=0)
        x = jnp.where(row >= d, op(x, shifted), x)
        d *= 2
    return x


def _mlstm_sample_body(t_in, nb, qk_ref, v_ref, g_ref, c0_ref, n0_ref, m0_ref, nw_ref,
                       h_ref, c_ref, n_ref, m_ref, *pads):
    t = SUBLANES
    c_ref[...] = c0_ref[...]
    n_ref[...] = n0_ref[...]
    m_ref[...] = m0_ref[...]

    @pl.when(_first_step())
    def _():
        for p in pads:
            p[...] = jnp.zeros(p.shape, p.dtype)
    for src, dst in zip((qk_ref, v_ref, g_ref), pads):
        for s in range(nb):
            dst[s, pl.ds(0, t_in), :] = src[:, s, :]
    qk_src, v_src, g_src = pads

    row = lax.broadcasted_iota(jnp.int32, (t, LANES), 0)
    lane = lax.broadcasted_iota(jnp.int32, (t, LANES), 1)
    lane1 = lax.broadcasted_iota(jnp.int32, (1, LANES), 1)
    r2 = lax.broadcasted_iota(jnp.int32, (t, t), 0)
    c2 = lax.broadcasted_iota(jnp.int32, (t, t), 1)
    causal = c2 <= r2
    pad_gate = jnp.where(lane < NH_M, NEG, -NEG)

    seqs = []
    for s in range(nb):
        ga = _gate_algebra(jnp.where(row < t_in, g_src[s], pad_gate), m_ref[s], t)
        ga["gt"] = ga["gg"].T
        m_ref[s] = jnp.where(lane1 < NH_M, ga["m_new"], 0.0)
        seqs.append(ga)

    units = [(s, hd) for s in range(nb) for hd in range(NH_M)]
    nu = len(units)
    col = lambda name, s, hd: seqs[s][name][:, hd:hd + 1]
    q = [qk_src[s, :, hd * DHK_M:(hd + 1) * DHK_M].astype(BF16) for s, hd in units]
    k = [qk_src[s, :, DQK_M + hd * DHK_M:DQK_M + (hd + 1) * DHK_M].astype(BF16) for s, hd in units]
    v = [v_src[s, :, hd * DHV_M:(hd + 1) * DHV_M].astype(BF16) for s, hd in units]
    c_old = [c_ref[s, hd] for s, hd in units]
    n_old = [n_ref[s, hd:hd + 1, :] for s, hd in units]

    sqk = [_dot_nt(q[u], k[u]) for u in range(nu)]
    dexp = [jnp.exp(jnp.where(causal, col("eb", s, hd) + seqs[s]["gt"][hd:hd + 1, :], NEG))
            for s, hd in units]
    smat = [sqk[u] * dexp[u] for u in range(nu)]
    intra = [_dot(smat[u].astype(BF16), v[u]) for u in range(nu)]
    inter = [_dot_nt(q[u], c_old[u].astype(BF16)) for u in range(nu)]
    qn = [jnp.sum(q[u].astype(F32) * n_old[u].astype(BF16).astype(F32), axis=-1, keepdims=True)
          for u in range(nu)]
    den = [jnp.sum(smat[u], axis=-1, keepdims=True) + col("a_in", s, hd) * qn[u]
           for u, (s, hd) in enumerate(units)]
    hh = [(intra[u] + col("a_in", s, hd) * inter[u])
          / jnp.maximum(jnp.abs(den[u]), col("lowb", s, hd)) for u, (s, hd) in enumerate(units)]
    hn = [_ln(hh[u]) * nw_ref[:, hd * DHV_M:(hd + 1) * DHV_M] for u, (s, hd) in enumerate(units)]
    for u, (s, hd) in enumerate(units):
        h_ref[:, s, hd * DHV_M:(hd + 1) * DHV_M] = hn[u][:t_in]

    kw = [k[u].astype(F32) * col("ws", s, hd) for u, (s, hd) in enumerate(units)]
    upd = [_dot_tn(v[u], kw[u].astype(BF16)) for u in range(nu)]
    for u, (s, hd) in enumerate(units):
        dec = seqs[s]["decay"][:, hd:hd + 1]
        c_ref[s, hd] = dec * c_old[u] + upd[u]
        n_ref[s, hd:hd + 1, :] = dec * n_old[u] + jnp.sum(kw[u], axis=0, keepdims=True)


def _mlstm_sample(qk, v, g, c0, n0, m0, norm_w, nb):
    t_in, nseq, _ = qk.shape
    assert t_in <= SUBLANES
    blk = lambda n: pl.BlockSpec((t_in, nb, n), lambda o, c: (0, o, 0))
    st_c = pl.BlockSpec((nb, NH_M, DHV_M, DHK_M), lambda o, c: (o, 0, 0, 0))
    st_n = pl.BlockSpec((nb, NH_M, DHK_M), lambda o, c: (o, 0, 0))
    st_m = pl.BlockSpec((nb, 1, LANES), lambda o, c: (o, 0, 0))
    return pl.pallas_call(
        functools.partial(_mlstm_sample_body, t_in, nb),
        out_shape=[jax.ShapeDtypeStruct((t_in, nseq, DV_M), F32),
                   jax.ShapeDtypeStruct((nseq, NH_M, DHV_M, DHK_M), F32),
                   jax.ShapeDtypeStruct((nseq, NH_M, DHK_M), F32),
                   jax.ShapeDtypeStruct((nseq, 1, LANES), F32)],
        grid=(nseq // nb, 1),
        in_specs=[blk(2 * DQK_M), blk(DV_M), blk(LANES), st_c, st_n, st_m,
                  pl.BlockSpec((1, DV_M), lambda o, c: (0, 0))],
        out_specs=[blk(DV_M), st_c, st_n, st_m],
        scratch_shapes=[pltpu.VMEM((nb, SUBLANES, 2 * DQK_M), F32), pltpu.VMEM((nb, SUBLANES, DV_M), F32),
                        pltpu.VMEM((nb, SUBLANES, LANES), F32)],
        compiler_params=_params(2),
        name="mlstm_sample",
    )(qk, v, g, c0, n0, m0, norm_w)


def _gate_algebra(g, m_prev, t):
    b = pltpu.roll(_scan_rows(jax.nn.log_sigmoid(g), jnp.add, t), LANES - NH_M, axis=1)
    gg = g - b
    gmx = _scan_rows(gg, jnp.maximum, t)
    a = b + m_prev
    mt = jnp.maximum(a, b + gmx)
    b_last, gmx_last = b[t - 1:t, :], gmx[t - 1:t, :]
    m_new = jnp.maximum(b_last + m_prev, b_last + gmx_last)
    return dict(gg=gg, eb=b - mt, a_in=jnp.exp(a - mt), lowb=jnp.exp(-mt),
                ws=jnp.exp(b_last + gg - m_new), decay=jnp.exp(b_last + m_prev - m_new), m_new=m_new)


MLSTM_CHUNKS_PER_STEP = 4


def _mlstm_prompt_body(nb, qk_ref, vt_ref, g_ref, nwc_ref, h_ref, c_ref, n_ref, m_ref, nwb):
    @pl.when(pl.program_id(1) == 0)
    def _():
        c_ref[...] = jnp.zeros(c_ref.shape, F32)
        n_ref[...] = jnp.zeros(n_ref.shape, F32)
        m_ref[...] = jnp.zeros(m_ref.shape, F32)

    @pl.when(_first_step())
    def _():
        for hd in range(NH_M):
            nwb[hd] = jnp.broadcast_to(nwc_ref[hd * DHV_M:(hd + 1) * DHV_M, :], (DHV_M, CHUNK))

    for ci in range(MLSTM_CHUNKS_PER_STEP):
        _mlstm_prompt_chunk(nb, slice(ci * CHUNK, (ci + 1) * CHUNK), qk_ref, vt_ref, g_ref, nwb,
                            h_ref, c_ref, n_ref, m_ref)


def _mlstm_prompt_chunk(nb, rows, qk_ref, vt_ref, g_ref, nwb, h_ref, c_ref, n_ref, m_ref):
    t = CHUNK
    lane1 = lax.broadcasted_iota(jnp.int32, (1, LANES), 1)
    r2 = lax.broadcasted_iota(jnp.int32, (t, t), 0)
    c2 = lax.broadcasted_iota(jnp.int32, (t, t), 1)
    causal = r2 <= c2

    seqs = []
    for s in range(nb):
        ga = _gate_algebra(g_ref[s, rows, :], m_ref[s], t)
        m_ref[s] = jnp.where(lane1 < NH_M, ga["m_new"], 0.0)
        for name in ("eb", "a_in", "lowb"):
            ga[name + "_t"] = ga[name].T
        seqs.append(ga)

    units = [(s, hd) for s in range(nb) for hd in range(NH_M)]
    nu = len(units)
    rowv = lambda name, s, hd: seqs[s][name + "_t"][hd:hd + 1, :]
    q = [qk_ref[s, rows, hd * DHK_M:(hd + 1) * DHK_M] for s, hd in units]
    k = [qk_ref[s, rows, DQK_M + hd * DHK_M:DQK_M + (hd + 1) * DHK_M] for s, hd in units]
    vt = [vt_ref[s, hd * DHV_M:(hd + 1) * DHV_M, rows] for s, hd in units]
    c_old = [c_ref[s, hd] for s, hd in units]
    n_old = [n_ref[s, hd:hd + 1, :] for s, hd in units]

    skq = [_dot_nt(k[u], q[u]) for u in range(nu)]
    dexp = [jnp.exp(jnp.where(causal, seqs[s]["gg"][:, hd:hd + 1] + rowv("eb", s, hd), NEG))
            for s, hd in units]
    smat = [skq[u] * dexp[u] for u in range(nu)]
    intra = [_dot(vt[u], smat[u].astype(BF16)) for u in range(nu)]
    inter = [_dot_nt(c_old[u].astype(BF16), q[u]) for u in range(nu)]
    qn = [_dot_nt(jnp.broadcast_to(n_old[u], (SUBLANES, DHK_M)).astype(BF16), q[u])[0:1, :]
          for u in range(nu)]
    den = [jnp.sum(smat[u], axis=0, keepdims=True) + rowv("a_in", s, hd) * qn[u]
           for u, (s, hd) in enumerate(units)]
    inv = [1.0 / jnp.maximum(jnp.abs(den[u]), rowv("lowb", s, hd)) for u, (s, hd) in enumerate(units)]
    hh = [(intra[u] + rowv("a_in", s, hd) * inter[u]) * inv[u] for u, (s, hd) in enumerate(units)]
    mu = [jnp.mean(hh[u], axis=0, keepdims=True) for u in range(nu)]
    xc = [hh[u] - mu[u] for u in range(nu)]
    var = [jnp.mean(xc[u] * xc[u], axis=0, keepdims=True) for u in range(nu)]
    hn = [xc[u] * lax.rsqrt(var[u] + LN_EPS) * nwb[hd] for u, (s, hd) in enumerate(units)]
    for u, (s, hd) in enumerate(units):
        h_ref[s, rows, hd * DHV_M:(hd + 1) * DHV_M] = hn[u].T

    kw = [(k[u].astype(F32) * seqs[s]["ws"][:, hd:hd + 1]).astype(BF16) for u, (s, hd) in enumerate(units)]
    upd = [_dot(vt[u], kw[u]) for u in range(nu)]
    nupd = [_dot(jnp.ones((SUBLANES, t), BF16), kw[u])[0:1, :] for u in range(nu)]
    for u, (s, hd) in enumerate(units):
        dec = seqs[s]["decay"][:, hd:hd + 1]
        c_ref[s, hd] = dec * c_old[u] + upd[u]
        n_ref[s, hd:hd + 1, :] = dec * n_old[u] + nupd[u]


def _mlstm_prompt(qk, vt, g, norm_w_col):
    nb, seq, _ = qk.shape
    ts = MLSTM_CHUNKS_PER_STEP * CHUNK
    tok = lambda n: pl.BlockSpec((nb, ts, n), lambda o, c: (0, c, 0))
    st_c = pl.BlockSpec((nb, NH_M, DHV_M, DHK_M), lambda o, c: (0, 0, 0, 0))
    st_n = pl.BlockSpec((nb, NH_M, DHK_M), lambda o, c: (0, 0, 0))
    st_m = pl.BlockSpec((nb, 1, LANES), lambda o, c: (0, 0, 0))
    return pl.pallas_call(
        functools.partial(_mlstm_prompt_body, nb),
        out_shape=[jax.ShapeDtypeStruct((nb, seq, DV_M), F32),
                   jax.ShapeDtypeStruct((nb, NH_M, DHV_M, DHK_M), F32),
                   jax.ShapeDtypeStruct((nb, NH_M, DHK_M), F32),
                   jax.ShapeDtypeStruct((nb, 1, LANES), F32)],
        grid=(1, seq // ts),
        in_specs=[tok(2 * DQK_M), pl.BlockSpec((nb, DV_M, ts), lambda o, c: (0, 0, c)), tok(LANES),
                  pl.BlockSpec((DV_M, 1), lambda o, c: (0, 0))],
        out_specs=[tok(DV_M), st_c, st_n, st_m],
        scratch_shapes=[pltpu.VMEM((NH_M, DHV_M, CHUNK), F32)],
        compiler_params=_params(2),
        name="mlstm_prompt",
    )(qk, vt, g, norm_w_col)


SWA_BLOCKS_PER_STEP = 16


def _swa_prompt_body(prm_ref, q_ref, kc_ref, kp_ref, vc_ref, vp_ref, o_ref, tbl):
    row = lax.broadcasted_iota(jnp.int32, (WINDOW, WINDOW), 0)
    col = lax.broadcasted_iota(jnp.int32, (WINDOW, WINDOW), 1)
    tri = row <= col

    @pl.when(_first_step())
    def _():
        dist = jnp.where(tri, col - row, col - row + WINDOW).astype(F32)
        for hd in range(NH_A):
            bias = prm_ref[0, hd] * dist
            tbl[0, hd] = bias
            tbl[1, hd] = bias + jnp.where(tri, 0.0, -NEG)

    for b in range(SWA_BLOCKS_PER_STEP):
        rows = pl.ds(b * WINDOW, WINDOW)
        which = jnp.where(pl.program_id(1) == 0, 1, 0) if b == 0 else 0
        k_prev = kp_ref if b == 0 else kc_ref.at[pl.ds((b - 1) * WINDOW, WINDOW), :]
        v_prev = vp_ref if b == 0 else vc_ref.at[:, pl.ds((b - 1) * WINDOW, WINDOW)]
        _swa_prompt_block(which, prm_ref, q_ref.at[rows, :], kc_ref.at[rows, :], k_prev,
                          vc_ref.at[:, rows], v_prev, o_ref.at[rows, :], tbl)


def _swa_prompt_block(which, prm_ref, q_ref, kc_ref, kp_ref, vc_ref, vp_ref, o_ref, tbl):
    row = lax.broadcasted_iota(jnp.int32, (WINDOW, WINDOW), 0)
    col = lax.broadcasted_iota(jnp.int32, (WINDOW, WINDOW), 1)
    tri = row <= col
    lo = col < HD_A
    zb = jnp.zeros((WINDOW, LANES), BF16)

    def placed(k_ref, kv):
        c, par = kv // 2, kv % 2
        own = k_ref[:, c * LANES:(c + 1) * LANES]
        swp = pltpu.roll(own, HD_A, axis=1)
        if par == 0:
            return jnp.where(lo, own, zb), jnp.where(lo, zb, swp)
        return jnp.where(lo, swp, zb), jnp.where(lo, zb, own)

    st = []
    for kv in range(NKV_A):
        lhs = jnp.concatenate([*placed(kc_ref, kv), *placed(kp_ref, kv)], axis=0)
        qg = jnp.concatenate([q_ref[:, (2 * kv) * LANES:(2 * kv + 1) * LANES],
                              q_ref[:, (2 * kv + 1) * LANES:(2 * kv + 2) * LANES]], axis=0)
        st.append(_dot_nt(lhs, qg))
    tiles = []
    for hd in range(NH_A):
        kv, a, par = hd // GROUP_A, (hd % GROUP_A) // 2, hd % 2
        cols = slice(a * WINDOW, (a + 1) * WINDOW)
        tiles.append(jnp.where(tri, st[kv][par * WINDOW:(par + 1) * WINDOW, cols],
                               st[kv][(2 + par) * WINDOW:(3 + par) * WINDOW, cols]))
    sc = jnp.concatenate(tiles, axis=0).reshape(NH_A, WINDOW, WINDOW) - tbl[which]
    sink = jnp.concatenate([jnp.full((1, 1, WINDOW), prm_ref[1, hd], F32) for hd in range(NH_A)], axis=0)
    mx = jnp.maximum(jnp.max(sc, axis=1, keepdims=True), sink)
    p = jnp.exp2(sc - mx)
    den = jnp.sum(p, axis=1, keepdims=True) + jnp.exp2(sink - mx)
    pn = p * (1.0 / den)

    zero = jnp.zeros((WINDOW, WINDOW), F32)
    z64 = jnp.zeros((HD_A, WINDOW), BF16)
    outs = []
    for kv in range(NKV_A):
        cols = []
        for a in range(2):
            pe, po = pn[kv * GROUP_A + 2 * a], pn[kv * GROUP_A + 2 * a + 1]
            cols.append(jnp.concatenate([jnp.where(tri, pe, zero), jnp.where(tri, po, zero),
                                         jnp.where(tri, zero, pe), jnp.where(tri, zero, po)],
                                        axis=0).astype(BF16))
        pt = jnp.concatenate(cols, axis=1)
        vc = vc_ref[kv * HD_A:(kv + 1) * HD_A, :].astype(BF16)
        vp = vp_ref[kv * HD_A:(kv + 1) * HD_A, :].astype(BF16)
        vt = jnp.concatenate([jnp.concatenate([vc, z64], axis=0), jnp.concatenate([z64, vc], axis=0),
                              jnp.concatenate([vp, z64], axis=0), jnp.concatenate([z64, vp], axis=0)],
                             axis=1)
        ot = _dot(vt, pt)
        outs += [ot[:, :WINDOW].T, ot[:, WINDOW:].T]
    o_ref[...] = jnp.concatenate(outs, axis=1).astype(o_ref.dtype)


def _swa_prompt(prm, q, kb, vat):
    bsz, seq, _ = q.shape
    tq = SWA_BLOCKS_PER_STEP * WINDOW
    prev = lambda i: jnp.maximum(i * SWA_BLOCKS_PER_STEP - 1, 0)
    return pl.pallas_call(
        _swa_prompt_body,
        out_shape=jax.ShapeDtypeStruct((bsz, seq, DQ_A), BF16),
        grid=(bsz, seq // tq),
        in_specs=[pl.BlockSpec(memory_space=pltpu.SMEM),
                  pl.BlockSpec((None, tq, DQ_A), lambda b, i: (b, i, 0)),
                  pl.BlockSpec((None, tq, DKV_A), lambda b, i: (b, i, 0)),
                  pl.BlockSpec((None, WINDOW, DKV_A), lambda b, i: (b, prev(i), 0)),
                  pl.BlockSpec((None, DKV_A, tq), lambda b, i: (b, 0, i)),
                  pl.BlockSpec((None, DKV_A, WINDOW), lambda b, i: (b, 0, prev(i)))],
        out_specs=pl.BlockSpec((None, tq, DQ_A), lambda b, i: (b, i, 0)),
        scratch_shapes=[pltpu.VMEM((2, NH_A, WINDOW, WINDOW), F32)],
        compiler_params=_params(2),
        name="swa_prompt",
    )(prm, q, kb, kb, vat, vat)


def _half_mask(shape, half):
    lane = lax.broadcasted_iota(jnp.int32, shape, 1)
    return lane < HD_A if half == 0 else lane >= HD_A


def _swa_sample_body(t_in, nb, prm_ref, q_ref, kn_ref, vn_ref, kct_ref, vct_ref,
                     o_ref, kco_ref, vco_ref, tbl, q8, kn_pad, vn_pad):
    tq = SUBLANES
    rows = NH_A * tq
    row = lax.broadcasted_iota(jnp.int32, (rows, WINDOW), 0)
    col = lax.broadcasted_iota(jnp.int32, (rows, WINDOW), 1)
    tri = col <= (row & (tq - 1))

    @pl.when(_first_step())
    def _():
        r8 = lax.broadcasted_iota(jnp.int32, (tq, WINDOW), 0)
        c8 = lax.broadcasted_iota(jnp.int32, (tq, WINDOW), 1)
        dist = jnp.where(c8 <= r8, r8 - c8, r8 - c8 + WINDOW).astype(F32)
        for hd in range(NH_A):
            tbl[pl.ds(hd * tq, tq), :] = prm_ref[0, hd] * dist
        for p in (q8, kn_pad, vn_pad):
            p[...] = jnp.zeros(p.shape, p.dtype)

    for s in range(nb):
        q8[s, pl.ds(0, t_in), :] = q_ref[:, s, :]
        kn_pad[s, pl.ds(0, t_in), :] = kn_ref[:, s, :]
        vn_pad[s, pl.ds(0, t_in), :] = vn_ref[:, s, :]
    sink_col = jnp.concatenate([jnp.full((tq, 1), prm_ref[1, hd], F32) for hd in range(NH_A)], axis=0)
    z8 = jnp.zeros((tq, LANES), F32)
    n_chunk = NKV_A // 2
    heads_per_chunk = NH_A // n_chunk

    def place(piece, src_half, dst_half):
        if src_half != dst_half:
            piece = pltpu.roll(piece, HD_A, axis=1)
        return jnp.where(_half_mask(piece.shape, dst_half), piece, z8)

    qexp = []
    for s in range(nb):
        per_c = []
        for c in range(n_chunk):
            pieces = []
            for hl in range(heads_per_chunk):
                hd = c * heads_per_chunk + hl
                pieces.append(place(q8[s, :, (hd // 2) * LANES:(hd // 2 + 1) * LANES],
                                    hd % 2, hl // GROUP_A))
            per_c.append(jnp.concatenate(pieces, axis=0).astype(BF16))
        qexp.append(per_c)
    csl = lambda c: slice(c * LANES, (c + 1) * LANES)
    s_prev = [[_dot(qexp[s][c], kct_ref[s, csl(c), :].astype(BF16)) for c in range(n_chunk)]
              for s in range(nb)]
    s_cur = [[_dot_nt(qexp[s][c], kn_pad[s, :, csl(c)].astype(BF16)) for c in range(n_chunk)]
             for s in range(nb)]
    sc = [jnp.where(tri, jnp.concatenate(s_cur[s], axis=0), jnp.concatenate(s_prev[s], axis=0)) - tbl[...]
          for s in range(nb)]
    mx = [jnp.maximum(jnp.max(sc[s], axis=-1, keepdims=True), sink_col) for s in range(nb)]
    p = [jnp.exp2(sc[s] - mx[s]) for s in range(nb)]
    den = [jnp.sum(p[s], axis=-1, keepdims=True) + jnp.exp2(sink_col - mx[s]) for s in range(nb)]
    pn = [p[s] * (1.0 / den[s]) for s in range(nb)]
    zero = jnp.zeros((rows, WINDOW), F32)
    pc = [jnp.where(tri, pn[s], zero).astype(BF16) for s in range(nb)]
    pp = [jnp.where(tri, zero, pn[s]).astype(BF16) for s in range(nb)]
    half_rows = heads_per_chunk * tq
    oc = [[_dot_nt(pp[s][c * half_rows:(c + 1) * half_rows], vct_ref[s, csl(c), :].astype(BF16))
           + _dot(pc[s][c * half_rows:(c + 1) * half_rows], vn_pad[s, :, csl(c)].astype(BF16))
           for c in range(n_chunk)] for s in range(nb)]
    for s in range(nb):
        chunks = []
        for pch in range(NH_A // 2):
            acc = None
            for hd in (2 * pch, 2 * pch + 1):
                c, hl = hd // heads_per_chunk, hd % heads_per_chunk
                piece = place(oc[s][c][hl * tq:(hl + 1) * tq, :], hl // GROUP_A, hd % 2)
                acc = piece if acc is None else acc + piece
            chunks.append(acc)
        o_ref[:, s, :] = jnp.concatenate(chunks, axis=1)[:t_in]

    lane = lax.broadcasted_iota(jnp.int32, (DKV_A, WINDOW), 1)
    for new_pad, old_ref, out_ref in ((kn_pad, kct_ref, kco_ref), (vn_pad, vct_ref, vco_ref)):
        for s in range(nb):
            merged = jnp.where(lane < t_in, new_pad[s].T, old_ref[s])
            out_ref[s] = pltpu.roll(merged, WINDOW - t_in, axis=1)


def _swa_sample(prm, q, kn, vn, kct, vct, nb):
    t_in, nseq, _ = q.shape
    assert t_in <= SUBLANES
    cur = lambda n: pl.BlockSpec((t_in, nb, n), lambda o, i: (0, o, 0))
    win = pl.BlockSpec((nb, DKV_A, WINDOW), lambda o, i: (o, 0, 0))
    return pl.pallas_call(
        functools.partial(_swa_sample_body, t_in, nb),
        out_shape=[jax.ShapeDtypeStruct((t_in, nseq, DQ_A), F32),
                   jax.ShapeDtypeStruct((nseq, DKV_A, WINDOW), F32),
                   jax.ShapeDtypeStruct((nseq, DKV_A, WINDOW), F32)],
        grid=(nseq // nb, 1),
        in_specs=[pl.BlockSpec(memory_space=pltpu.SMEM), cur(DQ_A), cur(DKV_A), cur(DKV_A), win, win],
        out_specs=[cur(DQ_A), win, win],
        scratch_shapes=[pltpu.VMEM((NH_A * SUBLANES, WINDOW), F32),
                        pltpu.VMEM((nb, SUBLANES, DQ_A), F32),
                        pltpu.VMEM((nb, WINDOW, DKV_A), F32),
                        pltpu.VMEM((nb, WINDOW, DKV_A), F32)],
        compiler_params=_params(2),
        name="swa_sample",
    )(prm, q, kn, vn, kct, vct)


def _merge_body(alpha, x_ref, h_ref, g1_ref, hm_ref, ha_ref, wgt_ref, bgt_ref,
                wbm32_ref, wba32_ref, wo32_ref, lg_ref, lb_ref, o_ref, wsq):
    @pl.when(_first_step())
    def _():
        for j, ref in enumerate((wbm32_ref, wba32_ref, wo32_ref)):
            wsq[j] = ref[...].astype(BF16)

    wbm_ref, wba_ref, wo_ref = wsq.at[0], wsq.at[1], wsq.at[2]
    nsub = 2
    sub = x_ref.shape[0] // nsub
    rows = [pl.ds(r * sub, sub) for r in range(nsub)]
    mod = lambda ref, r: _mod_rows(ref, r * sub, sub)
    gate = lambda g, j: g[:, j * D_MODEL:(j + 1) * D_MODEL]

    x = [x_ref[rows[r], :] for r in range(nsub)]
    h = [h_ref[rows[r], :] for r in range(nsub)]
    g = [jax.nn.sigmoid(_dot(h[r], wgt_ref[...]) + bgt_ref[...]) for r in range(nsub)]
    hm = [(hm_ref[rows[r], :] * gate(g[r], 0)).astype(BF16) for r in range(nsub)]
    bm = [_dot(hm[r], wbm_ref[...]) for r in range(nsub)]
    ba = [_dot(ha_ref[rows[r], :].astype(BF16), wba_ref[...]) for r in range(nsub)]
    merged = [(gate(g[r], 1) * bm[r] + gate(g[r], 2) * ba[r]).astype(BF16) for r in range(nsub)]
    mo = [_dot(merged[r], wo_ref[...]) for r in range(nsub)]
    for r in range(nsub):
        o_ref[rows[r], :] = _ln(alpha * x[r] + mod(g1_ref, r) * mo[r]) * lg_ref[...] + lb_ref[...]


def _merge(x, h, g1, hm, ha, w, tm, alpha):
    bsz, seq, _ = x.shape
    g1, g1_spec = _mod_operand(g1)
    weights = [w["wgates"], w["bgates"], w["wbm"], w["wba"], w["wo"],
               w["ln1_g"], w["ln1_b"]]
    return pl.pallas_call(
        functools.partial(_merge_body, alpha),
        out_shape=jax.ShapeDtypeStruct((bsz, seq, D_MODEL), F32),
        grid=(bsz, seq // tm),
        in_specs=[_tok_spec(tm, D_MODEL), _tok_spec(tm, D_MODEL), g1_spec,
                  _tok_spec(tm, DV_M), _tok_spec(tm, DQ_A)]
                 + [_const_spec(a.shape) for a in weights],
        out_specs=_tok_spec(tm, D_MODEL),
        scratch_shapes=[pltpu.VMEM((3, D_MODEL, D_MODEL), BF16)],
        compiler_params=_params(2),
        name="merge",
    )(x, h, g1, hm, ha, *weights)


def _ffn_body(alpha, tm, stride, halo, x_ref, sh_ref, sc_ref, g2_ref, cb0_ref, wup_ref, bup_ref,
              cw_ref, cbias_ref, wdn_ref, bdn_ref, lg_ref, lb_ref, o_ref, cs_ref, ubuf, act):
    @pl.when(pl.program_id(1) == 0)
    def _():
        cs_ref[...] = cb0_ref[...]

    nsub = 2
    sub = tm // nsub
    rows = [pl.ds(r * sub, sub) for r in range(nsub)]
    mod = lambda ref, r: _mod_rows(ref, r * sub, sub)
    x = [x_ref[rows[r], :] for r in range(nsub)]
    h = [(_ln(x[r]) * (1.0 + mod(sc_ref, r)) + mod(sh_ref, r)).astype(BF16) for r in range(nsub)]
    ys = [[None, None] for _ in range(nsub)]
    for half in range(2):
        cols = slice(half * D_FF, (half + 1) * D_FF)
        u = [_dot(h[r], wup_ref[:, cols]) + bup_ref[:, cols] for r in range(nsub)]
        ubuf[pl.ds(0, halo), :] = cs_ref[:, cols]
        for r in range(nsub):
            ubuf[pl.ds(halo + r * sub, sub), :] = u[r]
        cs_ref[:, cols] = ubuf[pl.ds(tm, halo), :]
        for r in range(nsub):
            y = cbias_ref[:, cols] + u[r] * cw_ref[CONV_W - 1:CONV_W, cols]
            for j in range(CONV_W - 1):
                tap = ubuf[pl.ds(halo + r * sub - (CONV_W - 1 - j) * stride, sub), :]
                y = y + tap * cw_ref[j:j + 1, cols]
            ys[r][half] = y
    for r in range(nsub):
        act[rows[r], :] = (jax.nn.gelu(ys[r][0]) * ys[r][1]).astype(BF16)
    f = [_dot(act[rows[r], :], wdn_ref[...]) + bdn_ref[...] for r in range(nsub)]
    for r in range(nsub):
        o_ref[rows[r], :] = _ln(alpha * x[r] + mod(g2_ref, r) * f[r]) * lg_ref[...] + lb_ref[...]


def _ffn(x, sh, sc, g2, cb0, w, tm, stride, alpha):
    bsz, seq, _ = x.shape
    (sh, sh_spec), (sc, sc_spec), (g2, g2_spec) = _mod_operand(sh), _mod_operand(sc), _mod_operand(g2)
    halo = cb0.shape[1]
    cs = pl.BlockSpec((None, halo, 2 * D_FF), lambda b, i: (b, 0, 0))
    weights = [w["wup"], w["bup"], w["cw"], w["cbias"], w["wdn"], w["bdn"], w["ln2_g"], w["ln2_b"]]
    return pl.pallas_call(
        functools.partial(_ffn_body, alpha, tm, stride, halo),
        out_shape=[jax.ShapeDtypeStruct((bsz, seq, D_MODEL), F32),
                   jax.ShapeDtypeStruct((bsz, halo, 2 * D_FF), F32)],
        grid=(bsz, seq // tm),
        in_specs=[_tok_spec(tm, D_MODEL), sh_spec, sc_spec, g2_spec, cs]
                 + [_const_spec(a.shape) for a in weights],
        out_specs=[_tok_spec(tm, D_MODEL), cs],
        scratch_shapes=[pltpu.VMEM((halo + tm, D_FF), F32), pltpu.VMEM((tm, D_FF), BF16)],
        compiler_params=_params(2),
        name="ffn",
    )(x, sh, sc, g2, cb0, *weights)


_O_GATE = 2 * DQK_M + DV_M
_O_OG = _O_GATE + 2 * NH_M
_O_QA = _O_OG + DV_M
_O_KA = _O_QA + DQ_A
_O_VA = _O_KA + DKV_A
_O_GM = _O_VA + DKV_A
_PREP_ROWS = 256


def _split_w_in_body(wt_ref, qk_ref, vm_ref, vmt_ref, g_ref, gates_ref, qa_ref, k_ref, v_ref, vt_ref):
    piece = lambda lo, hi: wt_ref[lo:hi, :]
    qk_ref[...] = piece(0, 2 * DQK_M).T.astype(BF16)
    vm = piece(2 * DQK_M, _O_GATE)
    vmt_ref[...] = vm.astype(BF16)
    vm_ref[...] = vm.T.astype(BF16)
    pad = jnp.zeros((LANES - 2 * NH_M, _PREP_ROWS), F32)
    g_ref[...] = jnp.concatenate([piece(_O_GATE, _O_OG), pad], axis=0).T.astype(BF16)
    gates_ref[:, :DV_M] = piece(_O_OG, _O_QA).T.astype(BF16)
    gates_ref[:, DV_M:] = piece(_O_GM, wt_ref.shape[0]).T.astype(BF16)
    qa_ref[...] = piece(_O_QA, _O_KA).T.astype(BF16)
    k_ref[...] = piece(_O_KA, _O_VA).T.astype(BF16)
    v = piece(_O_VA, _O_GM)
    vt_ref[...] = v.astype(BF16)
    v_ref[...] = v.T.astype(BF16)


def _split_w_in(w_in_t):
    d_in = w_in_t.shape[0]
    rows = lambda n: ((D_MODEL, n), pl.BlockSpec((_PREP_ROWS, n), lambda i: (i, 0)))
    cols = lambda n: ((n, D_MODEL), pl.BlockSpec((n, _PREP_ROWS), lambda i: (0, i)))
    outs = dict(wqk=rows(2 * DQK_M), wvm=rows(DV_M), wvm_t=cols(DV_M), wg=rows(LANES),
                wgates=rows(DV_M + d_in - _O_GM), wqa=rows(DQ_A), wk=rows(DKV_A), wv=rows(DKV_A),
                wv_t=cols(DKV_A))
    res = pl.pallas_call(
        _split_w_in_body,
        out_shape=[jax.ShapeDtypeStruct(s, BF16) for s, _ in outs.values()],
        grid=(D_MODEL // _PREP_ROWS,),
        in_specs=[pl.BlockSpec((d_in, _PREP_ROWS), lambda i: (0, i))],
        out_specs=[spec for _, spec in outs.values()],
        compiler_params=_params(1),
        name="split_w_in",
    )(w_in_t)
    return dict(zip(outs.keys(), res))


def _prep_weights(w_in, b_in, mlstm_norm_w, w_branch_m, w_branch_a, w_out, ln1_g, ln1_b,
                  w_up, b_up, conv_w, conv_b, w_down, b_down, ln2_g, ln2_b):
    row = lambda a: a.reshape(1, -1)
    gate_pad = LANES - 2 * NH_M
    b_k, b_v = b_in[_O_KA:_O_VA], b_in[_O_VA:_O_GM]
    return dict(
        **_split_w_in(w_in.T),
        bqk=row(b_in[:2 * DQK_M]),
        bvm_row=row(b_in[2 * DQK_M:_O_GATE]), bvm_col=b_in[2 * DQK_M:_O_GATE].reshape(-1, 1),
        bg=row(jnp.pad(b_in[_O_GATE:_O_OG], (0, gate_pad))),
        bqa=row(b_in[_O_QA:_O_KA]),
        bk=row(b_k),
        bv_row=row(b_v), bv_col=b_v.reshape(-1, 1),
        bgates=row(jnp.concatenate([b_in[_O_OG:_O_QA], b_in[_O_GM:]])),
        norm_w=row(mlstm_norm_w), norm_w_col=mlstm_norm_w.reshape(-1, 1),
        wbm=w_branch_m, wba=w_branch_a, wo=w_out, ln1_g=row(ln1_g), ln1_b=row(ln1_b),
        wup=w_up.astype(BF16), bup=row(b_up), cw=conv_w, cbias=row(conv_b),
        wdn=w_down.astype(BF16), bdn=row(b_down), ln2_g=row(ln2_g), ln2_b=row(ln2_b))


def kernel(x_prompt, x_sample, c_prompt, c_sample, state_mlstm_C, state_mlstm_n, state_mlstm_m,
           cache_k_win, cache_v_win, state_ffn_conv, w_ada, b_ada, w_in, b_in, mlstm_norm_w,
           attn_sinks, w_branch_m, w_branch_a, w_out, ln1_g, ln1_b, w_up, b_up, conv_w, conv_b,
           w_down, b_down, ln2_g, ln2_b):
    depth = w_in.shape[0]
    bp, lp, _ = x_prompt.shape
    bs, ls, _ = x_sample.shape
    assert cache_k_win.shape[2] == WINDOW
    alpha = (2 * depth) ** 0.25
    dt = x_prompt.dtype
    slopes = jnp.exp2(-8.0 * jnp.arange(1, NH_A + 1, dtype=F32) / NH_A)
    tm_p = 512
    tm_in = 1024
    ns = bs * ls

    yp = x_prompt
    to_pos = lambda a: a.transpose(1, 0, 2).reshape(1, -1, a.shape[-1])
    ys = to_pos(x_sample)
    new_p, new_s = [], []
    n_c = bp + bs
    c_rows = -(-n_c // SUBLANES) * SUBLANES
    c_all = jnp.concatenate([c_sample, c_prompt, jnp.zeros((c_rows - n_c, D_MODEL), dt)], axis=0)
    for l in range(depth):
        w = _prep_weights(w_in[l], b_in[l], mlstm_norm_w[l], w_branch_m[l], w_branch_a[l], w_out[l],
                          ln1_g[l], ln1_b[l], w_up[l], b_up[l], conv_w[l], conv_b[l], w_down[l],
                          b_down[l], ln2_g[l], ln2_b[l])
        prm = jnp.stack([slopes, attn_sinks[l].astype(F32)]) * LOG2E
        mod = _ada(c_all, w_ada[l], b_ada[l])
        mod_p = mod[bs:bs + bp].reshape(bp, N_MOD, D_MODEL)
        mp_ = [mod_p[:, j:j + 1] for j in range(N_MOD)]
        ms_ = [(mod, j, bs) for j in range(N_MOD)]

        hp, qk, v, g, qa, kb, ka, vat = _inproj(yp, mp_[0], mp_[1], w, tm_in, True)
        hm, cp, np_, mp = _mlstm_prompt(qk, v, g, w["norm_w_col"])
        ha = _swa_prompt(prm, qa, kb, vat)
        x1p = _merge(yp, hp, mp_[2], hm, ha, w, tm_p, alpha)
        halo_p = SUBLANES
        yp, csp = _ffn(x1p, mp_[3], mp_[4], mp_[5], jnp.zeros((bp, halo_p, 2 * D_FF), dt),
                       w, tm_p, 1, alpha)
        p_k = ka[:, lp - WINDOW:].reshape(bp, WINDOW, NKV_A, HD_A)
        p_v = vat[:, :, lp - WINDOW:].reshape(bp, NKV_A, HD_A, WINDOW).transpose(0, 3, 1, 2)
        new_p.append((cp, np_, mp[:, 0, :NH_M], p_k, p_v, csp[:, halo_p - (CONV_W - 1):]))

        hs, qk, v, g, qa, kn, vn = _inproj(ys, ms_[0], ms_[1], w, ns, False)
        per_seq = lambda a: a.reshape(ls, bs, a.shape[-1])
        m0 = jnp.pad(state_mlstm_m[l], ((0, 0), (0, LANES - NH_M)))[:, None, :]
        hm, cs_, ns_, ms = _mlstm_sample(per_seq(qk), per_seq(v), per_seq(g), state_mlstm_C[l],
                                         state_mlstm_n[l], m0, w["norm_w"], SEQS_PER_STEP)
        to_t = lambda a: a.transpose(0, 2, 3, 1).reshape(bs, DKV_A, WINDOW)
        from_t = lambda a: a.reshape(bs, NKV_A, HD_A, WINDOW).transpose(0, 3, 1, 2)
        ha, kct, vct = _swa_sample(prm, per_seq(qa), per_seq(kn), per_seq(vn),
                                   to_t(cache_k_win[l]), to_t(cache_v_win[l]), SWA_SEQS_PER_STEP)
        x1s = _merge(ys, hs, ms_[2], hm.reshape(1, ns, DV_M), ha.reshape(1, ns, DQ_A), w, ns, alpha)
        ys, css = _ffn(x1s, ms_[3], ms_[4], ms_[5], to_pos(state_ffn_conv[l]), w, ns, bs, alpha)
        conv_s = css.reshape(CONV_W - 1, bs, 2 * D_FF).transpose(1, 0, 2)
        new_s.append((cs_, ns_, ms[:, 0, :NH_M], from_t(kct), from_t(vct), conv_s))

    p_state = [jnp.stack(a) for a in zip(*new_p)]
    s_state = [jnp.stack(a) for a in zip(*new_s)]
    y_sample = ys.reshape(ls, bs, D_MODEL).transpose(1, 0, 2)
    return (yp, y_sample, *p_state, *s_state)
```

```python
import functools

import jax
import jax.numpy as jnp
from jax import lax
from jax.experimental import pallas as pl
from jax.experimental.pallas import tpu as pltpu

F32 = jnp.float32
BF16 = jnp.bfloat16

D_MODEL = 1024
NH_M, DHK_M, DHV_M = 4, 128, 256
DQK_M, DV_M = NH_M * DHK_M, NH_M * DHV_M
NH_A, NKV_A, HD_A = 16, 4, 64
GROUP_A = NH_A // NKV_A
WINDOW = 128
DQ_A, DKV_A = NH_A * HD_A, NKV_A * HD_A
D_FF = 2816
CONV_W = 3
N_MOD = 6
LN_EPS = 1e-5
CHUNK = 128
NEG = -1e30
LOG2E = 1.4426950408889634
LANES = 128
SUBLANES = 8
VMEM_LIMIT = 56 * 1024 * 1024
SEQS_PER_STEP = 16
SWA_SEQS_PER_STEP = 8


def _ln(x):
    mu = jnp.mean(x, axis=-1, keepdims=True)
    xc = x - mu
    var = jnp.mean(xc * xc, axis=-1, keepdims=True)
    return xc * lax.rsqrt(var + LN_EPS)


def _dot(a, b):
    return jnp.dot(a, b, preferred_element_type=F32)


def _dot_nt(a, b):
    return lax.dot_general(a, b, (((1,), (1,)), ((), ())), preferred_element_type=F32)


def _dot_tn(a, b):
    return lax.dot_general(a, b, (((0,), (0,)), ((), ())), preferred_element_type=F32)


def _const_spec(shape):
    nd = len(shape)
    return pl.BlockSpec(shape, lambda *_: (0,) * nd, pipeline_mode=pl.Buffered(1))


def _params(n_grid):
    return pltpu.CompilerParams(dimension_semantics=("arbitrary",) * n_grid,
                                vmem_limit_bytes=VMEM_LIMIT)


def _first_step():
    return (pl.program_id(0) == 0) & (pl.program_id(1) == 0)


def _ada_body(c_ref, w_ref, b_ref, o_ref):
    c = c_ref[...]
    s = (c * jax.nn.sigmoid(c)).astype(BF16)
    o_ref[...] = _dot(s, w_ref[...].astype(BF16)) + b_ref[...]


def _ada(c, w_ada, b_ada):
    rows = c.shape[0]
    n_out = w_ada.shape[1]
    bn = 3072
    assert n_out % bn == 0
    return pl.pallas_call(
        _ada_body,
        out_shape=jax.ShapeDtypeStruct((rows, n_out), F32),
        grid=(n_out // bn,),
        in_specs=[pl.BlockSpec((rows, D_MODEL), lambda j: (0, 0)),
                  pl.BlockSpec((D_MODEL, bn), lambda j: (0, j)),
                  pl.BlockSpec((1, bn), lambda j: (0, j))],
        out_specs=pl.BlockSpec((rows, bn), lambda j: (0, j)),
        compiler_params=_params(1),
        name="ada",
    )(c, w_ada, b_ada.reshape(1, n_out))


def _inproj_body(prompt, x_ref, sh_ref, sc_ref, wqk_ref, bqk_ref, wvm_ref, bvm_ref, wg_ref, bg_ref,
                 wqa_ref, bqa_ref, wk_ref, bk_ref, wv_ref, bv_ref, h_ref, qk_ref, v_ref, g_ref, qa_ref, *rest):
    tm = x_ref.shape[0]
    h = (_ln(x_ref[...]) * (1.0 + _mod_rows(sc_ref, 0, tm)) + _mod_rows(sh_ref, 0, tm)).astype(BF16)
    h_ref[...] = h

    def proj(w_ref, b_ref, lo, n):
        return _dot(h, w_ref[:, lo:lo + n]) + b_ref[:, lo:lo + n]

    qk_ref[:, :DQK_M] = (proj(wqk_ref, bqk_ref, 0, DQK_M) * DHK_M ** -0.5).astype(qk_ref.dtype)
    qk_ref[:, DQK_M:] = proj(wqk_ref, bqk_ref, DQK_M, DQK_M).astype(qk_ref.dtype)
    if prompt:
        v_ref[...] = (_dot_nt(wvm_ref[...], h) + bvm_ref[...]).astype(v_ref.dtype)
    else:
        v_ref[...] = proj(wvm_ref, bvm_ref, 0, DV_M).astype(v_ref.dtype)
    g_ref[...] = proj(wg_ref, bg_ref, 0, LANES)
    qa_ref[...] = (proj(wqa_ref, bqa_ref, 0, DQ_A) * (HD_A ** -0.5 * LOG2E)).astype(qa_ref.dtype)
    ka = proj(wk_ref, bk_ref, 0, DKV_A)
    if prompt:
        kb_ref, ka_ref, vat_ref = rest
        kb_ref[...] = ka.astype(BF16)
        ka_ref[...] = ka
        vat_ref[...] = _dot_nt(wv_ref[...], h) + bv_ref[...]
    else:
        ka_ref, va_ref = rest
        ka_ref[...] = ka
        va_ref[...] = _dot(h, wv_ref[...]) + bv_ref[...]


def _tok_spec(tm, n):
    return pl.BlockSpec((None, tm, n), lambda b, i: (b, i, 0))


def _mod_operand(m):
    if isinstance(m, tuple):
        arr, j, n = m
        return arr, pl.BlockSpec((n, D_MODEL), lambda b, i: (0, j))
    return m, pl.BlockSpec((None, 1, D_MODEL), lambda b, i: (b, 0, 0))


def _mod_rows(ref, start, n):
    period = ref.shape[0]
    if period == 1:
        return ref[...]
    assert start % period == 0 and n % period == 0
    return jnp.concatenate([ref[...]] * (n // period), axis=0)


def _inproj(x, sh, sc, w, tm, prompt):
    bsz, seq, _ = x.shape
    (sh, sh_spec), (sc, sc_spec) = _mod_operand(sh), _mod_operand(sc)
    act = BF16 if prompt else F32
    t_spec = lambda n: pl.BlockSpec((None, n, tm), lambda b, i: (b, 0, i))
    outs = [((bsz, seq, D_MODEL), BF16, _tok_spec(tm, D_MODEL)),
            ((bsz, seq, 2 * DQK_M), act, _tok_spec(tm, 2 * DQK_M)),
            ((bsz, DV_M, seq), act, t_spec(DV_M)) if prompt else ((bsz, seq, DV_M), act, _tok_spec(tm, DV_M)),
            ((bsz, seq, LANES), F32, _tok_spec(tm, LANES)),
            ((bsz, seq, DQ_A), act, _tok_spec(tm, DQ_A))]
    if prompt:
        wvm, bvm, wv, bv = w["wvm_t"], w["bvm_col"], w["wv_t"], w["bv_col"]
        outs += [((bsz, seq, DKV_A), BF16, _tok_spec(tm, DKV_A)),
                 ((bsz, seq, DKV_A), F32, _tok_spec(tm, DKV_A)),
                 ((bsz, DKV_A, seq), F32, t_spec(DKV_A))]
    else:
        wvm, bvm, wv, bv = w["wvm"], w["bvm_row"], w["wv"], w["bv_row"]
        outs += [((bsz, seq, DKV_A), F32, _tok_spec(tm, DKV_A)),
                 ((bsz, seq, DKV_A), F32, _tok_spec(tm, DKV_A))]
    weights = [w["wqk"], w["bqk"], wvm, bvm, w["wg"], w["bg"], w["wqa"], w["bqa"], w["wk"], w["bk"],
               wv, bv]
    return pl.pallas_call(
        functools.partial(_inproj_body, prompt),
        out_shape=[jax.ShapeDtypeStruct(s, dt) for s, dt, _ in outs],
        grid=(bsz, seq // tm),
        in_specs=[_tok_spec(tm, D_MODEL), sh_spec, sc_spec] + [_const_spec(a.shape) for a in weights],
        out_specs=[spec for _, _, spec in outs],
        compiler_params=_params(2),
        name="inproj",
    )(x, sh, sc, *weights)


def _scan_rows(x, op, rows):
    row = lax.broadcasted_iota(jnp.int32, x.shape, 0)
    d = 1
    while d < rows:
        shifted = pltpu.roll(x, d, axis=0)
        x = jnp.where(row >= d, op(x, shifted), x)
        d *= 2
    return x


def _mlstm_sample_body(t_in, nb, qk_ref, v_ref, g_ref, c0_ref, n0_ref, m0_ref, nw_ref,
                       h_ref, c_ref, n_ref, m_ref, *pads):
    t = SUBLANES
    c_ref[...] = c0_ref[...]
    n_ref[...] = n0_ref[...]
    m_ref[...] = m0_ref[...]

    @pl.when(_first_step())
    def _():
        for p in pads:
            p[...] = jnp.zeros(p.shape, p.dtype)
    for src, dst in zip((qk_ref, v_ref, g_ref), pads):
        for s in range(nb):
            dst[s, pl.ds(0, t_in), :] = src[:, s, :]
    qk_src, v_src, g_src = pads

    row = lax.broadcasted_iota(jnp.int32, (t, LANES), 0)
    lane = lax.broadcasted_iota(jnp.int32, (t, LANES), 1)
    lane1 = lax.broadcasted_iota(jnp.int32, (1, LANES), 1)
    r2 = lax.broadcasted_iota(jnp.int32, (t, t), 0)
    c2 = lax.broadcasted_iota(jnp.int32, (t, t), 1)
    causal = c2 <= r2
    pad_gate = jnp.where(lane < NH_M, NEG, -NEG)

    seqs = []
    for s in range(nb):
        ga = _gate_algebra(jnp.where(row < t_in, g_src[s], pad_gate), m_ref[s], t)
        ga["gt"] = ga["gg"].T
        m_ref[s] = jnp.where(lane1 < NH_M, ga["m_new"], 0.0)
        seqs.append(ga)

    units = [(s, hd) for s in range(nb) for hd in range(NH_M)]
    nu = len(units)
    col = lambda name, s, hd: seqs[s][name][:, hd:hd + 1]
    q = [qk_src[s, :, hd * DHK_M:(hd + 1) * DHK_M].astype(BF16) for s, hd in units]
    k = [qk_src[s, :, DQK_M + hd * DHK_M:DQK_M + (hd + 1) * DHK_M].astype(BF16) for s, hd in units]
    v = [v_src[s, :, hd * DHV_M:(hd + 1) * DHV_M].astype(BF16) for s, hd in units]
    c_old = [c_ref[s, hd] for s, hd in units]
    n_old = [n_ref[s, hd:hd + 1, :] for s, hd in units]

    sqk = [_dot_nt(q[u], k[u]) for u in range(nu)]
    dexp = [jnp.exp(jnp.where(causal, col("eb", s, hd) + seqs[s]["gt"][hd:hd + 1, :], NEG))
            for s, hd in units]
    smat = [sqk[u] * dexp[u] for u in range(nu)]
    intra = [_dot(smat[u].astype(BF16), v[u]) for u in range(nu)]
    inter = [_dot_nt(q[u], c_old[u].astype(BF16)) for u in range(nu)]
    qn = [jnp.sum(q[u].astype(F32) * n_old[u].astype(BF16).astype(F32), axis=-1, keepdims=True)
          for u in range(nu)]
    den = [jnp.sum(smat[u], axis=-1, keepdims=True) + col("a_in", s, hd) * qn[u]
           for u, (s, hd) in enumerate(units)]
    hh = [(intra[u] + col("a_in", s, hd) * inter[u])
          / jnp.maximum(jnp.abs(den[u]), col("lowb", s, hd)) for u, (s, hd) in enumerate(units)]
    hn = [_ln(hh[u]) * nw_ref[:, hd * DHV_M:(hd + 1) * DHV_M] for u, (s, hd) in enumerate(units)]
    for u, (s, hd) in enumerate(units):
        h_ref[:, s, hd * DHV_M:(hd + 1) * DHV_M] = hn[u][:t_in]

    kw = [k[u].astype(F32) * col("ws", s, hd) for u, (s, hd) in enumerate(units)]
    upd = [_dot_tn(v[u], kw[u].astype(BF16)) for u in range(nu)]
    for u, (s, hd) in enumerate(units):
        dec = seqs[s]["decay"][:, hd:hd + 1]
        c_ref[s, hd] = dec * c_old[u] + upd[u]
        n_ref[s, hd:hd + 1, :] = dec * n_old[u] + jnp.sum(kw[u], axis=0, keepdims=True)


def _mlstm_sample(qk, v, g, c0, n0, m0, norm_w, nb):
    t_in, nseq, _ = qk.shape
    assert t_in <= SUBLANES
    blk = lambda n: pl.BlockSpec((t_in, nb, n), lambda o, c: (0, o, 0))
    st_c = pl.BlockSpec((nb, NH_M, DHV_M, DHK_M), lambda o, c: (o, 0, 0, 0))
    st_n = pl.BlockSpec((nb, NH_M, DHK_M), lambda o, c: (o, 0, 0))
    st_m = pl.BlockSpec((nb, 1, LANES), lambda o, c: (o, 0, 0))
    return pl.pallas_call(
        functools.partial(_mlstm_sample_body, t_in, nb),
        out_shape=[jax.ShapeDtypeStruct((t_in, nseq, DV_M), F32),
                   jax.ShapeDtypeStruct((nseq, NH_M, DHV_M, DHK_M), F32),
                   jax.ShapeDtypeStruct((nseq, NH_M, DHK_M), F32),
                   jax.ShapeDtypeStruct((nseq, 1, LANES), F32)],
        grid=(nseq // nb, 1),
        in_specs=[blk(2 * DQK_M), blk(DV_M), blk(LANES), st_c, st_n, st_m,
                  pl.BlockSpec((1, DV_M), lambda o, c: (0, 0))],
        out_specs=[blk(DV_M), st_c, st_n, st_m],
        scratch_shapes=[pltpu.VMEM((nb, SUBLANES, 2 * DQK_M), F32), pltpu.VMEM((nb, SUBLANES, DV_M), F32),
                        pltpu.VMEM((nb, SUBLANES, LANES), F32)],
        compiler_params=_params(2),
        name="mlstm_sample",
    )(qk, v, g, c0, n0, m0, norm_w)


def _gate_algebra(g, m_prev, t):
    b = pltpu.roll(_scan_rows(jax.nn.log_sigmoid(g), jnp.add, t), LANES - NH_M, axis=1)
    gg = g - b
    gmx = _scan_rows(gg, jnp.maximum, t)
    a = b + m_prev
    mt = jnp.maximum(a, b + gmx)
    b_last, gmx_last = b[t - 1:t, :], gmx[t - 1:t, :]
    m_new = jnp.maximum(b_last + m_prev, b_last + gmx_last)
    return dict(gg=gg, eb=b - mt, a_in=jnp.exp(a - mt), lowb=jnp.exp(-mt),
                ws=jnp.exp(b_last + gg - m_new), decay=jnp.exp(b_last + m_prev - m_new), m_new=m_new)


MLSTM_CHUNKS_PER_STEP = 4


def _mlstm_prompt_body(nb, qk_ref, vt_ref, g_ref, nwc_ref, h_ref, c_ref, n_ref, m_ref, nwb):
    @pl.when(pl.program_id(1) == 0)
    def _():
        c_ref[...] = jnp.zeros(c_ref.shape, F32)
        n_ref[...] = jnp.zeros(n_ref.shape, F32)
        m_ref[...] = jnp.zeros(m_ref.shape, F32)

    @pl.when(_first_step())
    def _():
        for hd in range(NH_M):
            nwb[hd] = jnp.broadcast_to(nwc_ref[hd * DHV_M:(hd + 1) * DHV_M, :], (DHV_M, CHUNK))

    for ci in range(MLSTM_CHUNKS_PER_STEP):
        _mlstm_prompt_chunk(nb, slice(ci * CHUNK, (ci + 1) * CHUNK), qk_ref, vt_ref, g_ref, nwb,
                            h_ref, c_ref, n_ref, m_ref)


def _mlstm_prompt_chunk(nb, rows, qk_ref, vt_ref, g_ref, nwb, h_ref, c_ref, n_ref, m_ref):
    t = CHUNK
    lane1 = lax.broadcasted_iota(jnp.int32, (1, LANES), 1)
    r2 = lax.broadcasted_iota(jnp.int32, (t, t), 0)
    c2 = lax.broadcasted_iota(jnp.int32, (t, t), 1)
    causal = r2 <= c2

    seqs = []
    for s in range(nb):
        ga = _gate_algebra(g_ref[s, rows, :], m_ref[s], t)
        m_ref[s] = jnp.where(lane1 < NH_M, ga["m_new"], 0.0)
        for name in ("eb", "a_in", "lowb"):
            ga[name + "_t"] = ga[name].T
        seqs.append(ga)

    units = [(s, hd) for s in range(nb) for hd in range(NH_M)]
    nu = len(units)
    rowv = lambda name, s, hd: seqs[s][name + "_t"][hd:hd + 1, :]
    q = [qk_ref[s, rows, hd * DHK_M:(hd + 1) * DHK_M] for s, hd in units]
    k = [qk_ref[s, rows, DQK_M + hd * DHK_M:DQK_M + (hd + 1) * DHK_M] for s, hd in units]
    vt = [vt_ref[s, hd * DHV_M:(hd + 1) * DHV_M, rows] for s, hd in units]
    c_old = [c_ref[s, hd] for s, hd in units]
    n_old = [n_ref[s, hd:hd + 1, :] for s, hd in units]

    skq = [_dot_nt(k[u], q[u]) for u in range(nu)]
    dexp = [jnp.exp(jnp.where(causal, seqs[s]["gg"][:, hd:hd + 1] + rowv("eb", s, hd), NEG))
            for s, hd in units]
    smat = [skq[u] * dexp[u] for u in range(nu)]
    intra = [_dot(vt[u], smat[u].astype(BF16)) for u in range(nu)]
    inter = [_dot_nt(c_old[u].astype(BF16), q[u]) for u in range(nu)]
    qn = [_dot_nt(jnp.broadcast_to(n_old[u], (SUBLANES, DHK_M)).astype(BF16), q[u])[0:1, :]
          for u in range(nu)]
    den = [jnp.sum(smat[u], axis=0, keepdims=True) + rowv("a_in", s, hd) * qn[u]
           for u, (s, hd) in enumerate(units)]
    inv = [1.0 / jnp.maximum(jnp.abs(den[u]), rowv("lowb", s, hd)) for u, (s, hd) in enumerate(units)]
    hh = [(intra[u] + rowv("a_in", s, hd) * inter[u]) * inv[u] for u, (s, hd) in enumerate(units)]
    mu = [jnp.mean(hh[u], axis=0, keepdims=True) for u in range(nu)]
    xc = [hh[u] - mu[u] for u in range(nu)]
    var = [jnp.mean(xc[u] * xc[u], axis=0, keepdims=True) for u in range(nu)]
    hn = [xc[u] * lax.rsqrt(var[u] + LN_EPS) * nwb[hd] for u, (s, hd) in enumerate(units)]
    for u, (s, hd) in enumerate(units):
        h_ref[s, rows, hd * DHV_M:(hd + 1) * DHV_M] = hn[u].T

    kw = [(k[u].astype(F32) * seqs[s]["ws"][:, hd:hd + 1]).astype(BF16) for u, (s, hd) in enumerate(units)]
    upd = [_dot(vt[u], kw[u]) for u in range(nu)]
    nupd = [_dot(jnp.ones((SUBLANES, t), BF16), kw[u])[0:1, :] for u in range(nu)]
    for u, (s, hd) in enumerate(units):
        dec = seqs[s]["decay"][:, hd:hd + 1]
        c_ref[s, hd] = dec * c_old[u] + upd[u]
        n_ref[s, hd:hd + 1, :] = dec * n_old[u] + nupd[u]


def _mlstm_prompt(qk, vt, g, norm_w_col):
    nb, seq, _ = qk.shape
    ts = MLSTM_CHUNKS_PER_STEP * CHUNK
    tok = lambda n: pl.BlockSpec((nb, ts, n), lambda o, c: (0, c, 0))
    st_c = pl.BlockSpec((nb, NH_M, DHV_M, DHK_M), lambda o, c: (0, 0, 0, 0))
    st_n = pl.BlockSpec((nb, NH_M, DHK_M), lambda o, c: (0, 0, 0))
    st_m = pl.BlockSpec((nb, 1, LANES), lambda o, c: (0, 0, 0))
    return pl.pallas_call(
        functools.partial(_mlstm_prompt_body, nb),
        out_shape=[jax.ShapeDtypeStruct((nb, seq, DV_M), F32),
                   jax.ShapeDtypeStruct((nb, NH_M, DHV_M, DHK_M), F32),
                   jax.ShapeDtypeStruct((nb, NH_M, DHK_M), F32),
                   jax.ShapeDtypeStruct((nb, 1, LANES), F32)],
        grid=(1, seq // ts),
        in_specs=[tok(2 * DQK_M), pl.BlockSpec((nb, DV_M, ts), lambda o, c: (0, 0, c)), tok(LANES),
                  pl.BlockSpec((DV_M, 1), lambda o, c: (0, 0))],
        out_specs=[tok(DV_M), st_c, st_n, st_m],
        scratch_shapes=[pltpu.VMEM((NH_M, DHV_M, CHUNK), F32)],
        compiler_params=_params(2),
        name="mlstm_prompt",
    )(qk, vt, g, norm_w_col)


SWA_BLOCKS_PER_STEP = 16


def _swa_prompt_body(prm_ref, q_ref, kc_ref, kp_ref, vc_ref, vp_ref, o_ref, tbl):
    row = lax.broadcasted_iota(jnp.int32, (WINDOW, WINDOW), 0)
    col = lax.broadcasted_iota(jnp.int32, (WINDOW, WINDOW), 1)
    tri = row <= col

    @pl.when(_first_step())
    def _():
        dist = jnp.where(tri, col - row, col - row + WINDOW).astype(F32)
        for hd in range(NH_A):
            bias = prm_ref[0, hd] * dist
            tbl[0, hd] = bias
            tbl[1, hd] = bias + jnp.where(tri, 0.0, -NEG)

    for b in range(SWA_BLOCKS_PER_STEP):
        rows = pl.ds(b * WINDOW, WINDOW)
        which = jnp.where(pl.program_id(1) == 0, 1, 0) if b == 0 else 0
        k_prev = kp_ref if b == 0 else kc_ref.at[pl.ds((b - 1) * WINDOW, WINDOW), :]
        v_prev = vp_ref if b == 0 else vc_ref.at[:, pl.ds((b - 1) * WINDOW, WINDOW)]
        _swa_prompt_block(which, prm_ref, q_ref.at[rows, :], kc_ref.at[rows, :], k_prev,
                          vc_ref.at[:, rows], v_prev, o_ref.at[rows, :], tbl)


def _swa_prompt_block(which, prm_ref, q_ref, kc_ref, kp_ref, vc_ref, vp_ref, o_ref, tbl):
    row = lax.broadcasted_iota(jnp.int32, (WINDOW, WINDOW), 0)
    col = lax.broadcasted_iota(jnp.int32, (WINDOW, WINDOW), 1)
    tri = row <= col
    lo = col < HD_A
    zb = jnp.zeros((WINDOW, LANES), BF16)

    def placed(k_ref, kv):
        c, par = kv // 2, kv % 2
        own = k_ref[:, c * LANES:(c + 1) * LANES]
        swp = pltpu.roll(own, HD_A, axis=1)
        if par == 0:
            return jnp.where(lo, own, zb), jnp.where(lo, zb, swp)
        return jnp.where(lo, swp, zb), jnp.where(lo, zb, own)

    st = []
    for kv in range(NKV_A):
        lhs = jnp.concatenate([*placed(kc_ref, kv), *placed(kp_ref, kv)], axis=0)
        qg = jnp.concatenate([q_ref[:, (2 * kv) * LANES:(2 * kv + 1) * LANES],
                              q_ref[:, (2 * kv + 1) * LANES:(2 * kv + 2) * LANES]], axis=0)
        st.append(_dot_nt(lhs, qg))
    tiles = []
    for hd in range(NH_A):
        kv, a, par = hd // GROUP_A, (hd % GROUP_A) // 2, hd % 2
        cols = slice(a * WINDOW, (a + 1) * WINDOW)
        tiles.append(jnp.where(tri, st[kv][par * WINDOW:(par + 1) * WINDOW, cols],
                               st[kv][(2 + par) * WINDOW:(3 + par) * WINDOW, cols]))
    sc = jnp.concatenate(tiles, axis=0).reshape(NH_A, WINDOW, WINDOW) - tbl[which]
    sink = jnp.concatenate([jnp.full((1, 1, WINDOW), prm_ref[1, hd], F32) for hd in range(NH_A)], axis=0)
    mx = jnp.maximum(jnp.max(sc, axis=1, keepdims=True), sink)
    p = jnp.exp2(sc - mx)
    den = jnp.sum(p, axis=1, keepdims=True) + jnp.exp2(sink - mx)
    pn = p * (1.0 / den)

    zero = jnp.zeros((WINDOW, WINDOW), F32)
    z64 = jnp.zeros((HD_A, WINDOW), BF16)
    outs = []
    for kv in range(NKV_A):
        cols = []
        for a in range(2):
            pe, po = pn[kv * GROUP_A + 2 * a], pn[kv * GROUP_A + 2 * a + 1]
            cols.append(jnp.concatenate([jnp.where(tri, pe, zero), jnp.where(tri, po, zero),
                                         jnp.where(tri, zero, pe), jnp.where(tri, zero, po)],
                                        axis=0).astype(BF16))
        pt = jnp.concatenate(cols, axis=1)
        vc = vc_ref[kv * HD_A:(kv + 1) * HD_A, :].astype(BF16)
        vp = vp_ref[kv * HD_A:(kv + 1) * HD_A, :].astype(BF16)
        vt = jnp.concatenate([jnp.concatenate([vc, z64], axis=0), jnp.concatenate([z64, vc], axis=0),
                              jnp.concatenate([vp, z64], axis=0), jnp.concatenate([z64, vp], axis=0)],
                             axis=1)
        ot = _dot(vt, pt)
        outs += [ot[:, :WINDOW].T, ot[:, WINDOW:].T]
    o_ref[...] = jnp.concatenate(outs, axis=1).astype(o_ref.dtype)


def _swa_prompt(prm, q, kb, vat):
    bsz, seq, _ = q.shape
    tq = SWA_BLOCKS_PER_STEP * WINDOW
    prev = lambda i: jnp.maximum(i * SWA_BLOCKS_PER_STEP - 1, 0)
    return pl.pallas_call(
        _swa_prompt_body,
        out_shape=jax.ShapeDtypeStruct((bsz, seq, DQ_A), BF16),
        grid=(bsz, seq // tq),
        in_specs=[pl.BlockSpec(memory_space=pltpu.SMEM),
                  pl.BlockSpec((None, tq, DQ_A), lambda b, i: (b, i, 0)),
                  pl.BlockSpec((None, tq, DKV_A), lambda b, i: (b, i, 0)),
                  pl.BlockSpec((None, WINDOW, DKV_A), lambda b, i: (b, prev(i), 0)),
                  pl.BlockSpec((None, DKV_A, tq), lambda b, i: (b, 0, i)),
                  pl.BlockSpec((None, DKV_A, WINDOW), lambda b, i: (b, 0, prev(i)))],
        out_specs=pl.BlockSpec((None, tq, DQ_A), lambda b, i: (b, i, 0)),
        scratch_shapes=[pltpu.VMEM((2, NH_A, WINDOW, WINDOW), F32)],
        compiler_params=_params(2),
        name="swa_prompt",
    )(prm, q, kb, kb, vat, vat)


def _half_mask(shape, half):
    lane = lax.broadcasted_iota(jnp.int32, shape, 1)
    return lane < HD_A if half == 0 else lane >= HD_A


def _swa_sample_body(t_in, nb, prm_ref, q_ref, kn_ref, vn_ref, kct_ref, vct_ref,
                     o_ref, kco_ref, vco_ref, tbl, q8, kn_pad, vn_pad):
    tq = SUBLANES
    rows = NH_A * tq
    row = lax.broadcasted_iota(jnp.int32, (rows, WINDOW), 0)
    col = lax.broadcasted_iota(jnp.int32, (rows, WINDOW), 1)
    tri = col <= (row & (tq - 1))

    @pl.when(_first_step())
    def _():
        r8 = lax.broadcasted_iota(jnp.int32, (tq, WINDOW), 0)
        c8 = lax.broadcasted_iota(jnp.int32, (tq, WINDOW), 1)
        dist = jnp.where(c8 <= r8, r8 - c8, r8 - c8 + WINDOW).astype(F32)
        for hd in range(NH_A):
            tbl[pl.ds(hd * tq, tq), :] = prm_ref[0, hd] * dist
        for p in (q8, kn_pad, vn_pad):
            p[...] = jnp.zeros(p.shape, p.dtype)

    for s in range(nb):
        q8[s, pl.ds(0, t_in), :] = q_ref[:, s, :]
        kn_pad[s, pl.ds(0, t_in), :] = kn_ref[:, s, :]
        vn_pad[s, pl.ds(0, t_in), :] = vn_ref[:, s, :]
    sink_col = jnp.concatenate([jnp.full((tq, 1), prm_ref[1, hd], F32) for hd in range(NH_A)], axis=0)
    z8 = jnp.zeros((tq, LANES), F32)
    n_chunk = NKV_A // 2
    heads_per_chunk = NH_A // n_chunk

    def place(piece, src_half, dst_half):
        if src_half != dst_half:
            piece = pltpu.roll(piece, HD_A, axis=1)
        return jnp.where(_half_mask(piece.shape, dst_half), piece, z8)

    qexp = []
    for s in range(nb):
        per_c = []
        for c in range(n_chunk):
            pieces = []
            for hl in range(heads_per_chunk):
                hd = c * heads_per_chunk + hl
                pieces.append(place(q8[s, :, (hd // 2) * LANES:(hd // 2 + 1) * LANES],
                                    hd % 2, hl // GROUP_A))
            per_c.append(jnp.concatenate(pieces, axis=0).astype(BF16))
        qexp.append(per_c)
    csl = lambda c: slice(c * LANES, (c + 1) * LANES)
    s_prev = [[_dot(qexp[s][c], kct_ref[s, csl(c), :].astype(BF16)) for c in range(n_chunk)]
              for s in range(nb)]
    s_cur = [[_dot_nt(qexp[s][c], kn_pad[s, :, csl(c)].astype(BF16)) for c in range(n_chunk)]
             for s in range(nb)]
    sc = [jnp.where(tri, jnp.concatenate(s_cur[s], axis=0), jnp.concatenate(s_prev[s], axis=0)) - tbl[...]
          for s in range(nb)]
    mx = [jnp.maximum(jnp.max(sc[s], axis=-1, keepdims=True), sink_col) for s in range(nb)]
    p = [jnp.exp2(sc[s] - mx[s]) for s in range(nb)]
    den = [jnp.sum(p[s], axis=-1, keepdims=True) + jnp.exp2(sink_col - mx[s]) for s in range(nb)]
    pn = [p[s] * (1.0 / den[s]) for s in range(nb)]
    zero = jnp.zeros((rows, WINDOW), F32)
    pc = [jnp.where(tri, pn[s], zero).astype(BF16) for s in range(nb)]
    pp = [jnp.where(tri, zero, pn[s]).astype(BF16) for s in range(nb)]
    half_rows = heads_per_chunk * tq
    oc = [[_dot_nt(pp[s][c * half_rows:(c + 1) * half_rows], vct_ref[s, csl(c), :].astype(BF16))
           + _dot(pc[s][c * half_rows:(c + 1) * half_rows], vn_pad[s, :, csl(c)].astype(BF16))
           for c in range(n_chunk)] for s in range(nb)]
    for s in range(nb):
        chunks = []
        for pch in range(NH_A // 2):
            acc = None
            for hd in (2 * pch, 2 * pch + 1):
                c, hl = hd // heads_per_chunk, hd % heads_per_chunk
                piece = place(oc[s][c][hl * tq:(hl + 1) * tq, :], hl // GROUP_A, hd % 2)
                acc = piece if acc is None else acc + piece
            chunks.append(acc)
        o_ref[:, s, :] = jnp.concatenate(chunks, axis=1)[:t_in]

    lane = lax.broadcasted_iota(jnp.int32, (DKV_A, WINDOW), 1)
    for new_pad, old_ref, out_ref in ((kn_pad, kct_ref, kco_ref), (vn_pad, vct_ref, vco_ref)):
        for s in range(nb):
            merged = jnp.where(lane < t_in, new_pad[s].T, old_ref[s])
            out_ref[s] = pltpu.roll(merged, WINDOW - t_in, axis=1)


def _swa_sample(prm, q, kn, vn, kct, vct, nb):
    t_in, nseq, _ = q.shape
    assert t_in <= SUBLANES
    cur = lambda n: pl.BlockSpec((t_in, nb, n), lambda o, i: (0, o, 0))
    win = pl.BlockSpec((nb, DKV_A, WINDOW), lambda o, i: (o, 0, 0))
    return pl.pallas_call(
        functools.partial(_swa_sample_body, t_in, nb),
        out_shape=[jax.ShapeDtypeStruct((t_in, nseq, DQ_A), F32),
                   jax.ShapeDtypeStruct((nseq, DKV_A, WINDOW), F32),
                   jax.ShapeDtypeStruct((nseq, DKV_A, WINDOW), F32)],
        grid=(nseq // nb, 1),
        in_specs=[pl.BlockSpec(memory_space=pltpu.SMEM), cur(DQ_A), cur(DKV_A), cur(DKV_A), win, win],
        out_specs=[cur(DQ_A), win, win],
        scratch_shapes=[pltpu.VMEM((NH_A * SUBLANES, WINDOW), F32),
                        pltpu.VMEM((nb, SUBLANES, DQ_A), F32),
                        pltpu.VMEM((nb, WINDOW, DKV_A), F32),
                        pltpu.VMEM((nb, WINDOW, DKV_A), F32)],
        compiler_params=_params(2),
        name="swa_sample",
    )(prm, q, kn, vn, kct, vct)


def _merge_body(alpha, x_ref, h_ref, g1_ref, hm_ref, ha_ref, wgt_ref, bgt_ref,
                wbm_ref, wba_ref, wo_ref, lg_ref, lb_ref, o_ref):
    nsub = 2
    sub = x_ref.shape[0] // nsub
    rows = [pl.ds(r * sub, sub) for r in range(nsub)]
    mod = lambda ref, r: _mod_rows(ref, r * sub, sub)
    gate = lambda g, j: g[:, j * D_MODEL:(j + 1) * D_MODEL]

    x = [x_ref[rows[r], :] for r in range(nsub)]
    h = [h_ref[rows[r], :] for r in range(nsub)]
    g = [jax.nn.sigmoid(_dot(h[r], wgt_ref[...]) + bgt_ref[...]) for r in range(nsub)]
    hm = [(hm_ref[rows[r], :] * gate(g[r], 0)).astype(BF16) for r in range(nsub)]
    bm = [_dot(hm[r], wbm_ref[...]) for r in range(nsub)]
    ba = [_dot(ha_ref[rows[r], :].astype(BF16), wba_ref[...]) for r in range(nsub)]
    merged = [(gate(g[r], 1) * bm[r] + gate(g[r], 2) * ba[r]).astype(BF16) for r in range(nsub)]
    mo = [_dot(merged[r], wo_ref[...]) for r in range(nsub)]
    for r in range(nsub):
        o_ref[rows[r], :] = _ln(alpha * x[r] + mod(g1_ref, r) * mo[r]) * lg_ref[...] + lb_ref[...]


def _merge(x, h, g1, hm, ha, w, tm, alpha):
    bsz, seq, _ = x.shape
    g1, g1_spec = _mod_operand(g1)
    weights = [w["wgates"], w["bgates"], w["wbm"], w["wba"], w["wo"],
               w["ln1_g"], w["ln1_b"]]
    return pl.pallas_call(
        functools.partial(_merge_body, alpha),
        out_shape=jax.ShapeDtypeStruct((bsz, seq, D_MODEL), F32),
        grid=(bsz, seq // tm),
        in_specs=[_tok_spec(tm, D_MODEL), _tok_spec(tm, D_MODEL), g1_spec,
                  _tok_spec(tm, DV_M), _tok_spec(tm, DQ_A)]
                 + [_const_spec(a.shape) for a in weights],
        out_specs=_tok_spec(tm, D_MODEL),
        compiler_params=_params(2),
        name="merge",
    )(x, h, g1, hm, ha, *weights)


def _ffn_body(alpha, tm, stride, halo, x_ref, sh_ref, sc_ref, g2_ref, cb0_ref, wup_ref, bup_ref,
              cw_ref, cbias_ref, wdn_ref, bdn_ref, lg_ref, lb_ref, o_ref, cs_ref, ubuf, act):
    @pl.when(pl.program_id(1) == 0)
    def _():
        cs_ref[...] = cb0_ref[...]

    nsub = 2
    sub = tm // nsub
    rows = [pl.ds(r * sub, sub) for r in range(nsub)]
    mod = lambda ref, r: _mod_rows(ref, r * sub, sub)
    x = [x_ref[rows[r], :] for r in range(nsub)]
    h = [(_ln(x[r]) * (1.0 + mod(sc_ref, r)) + mod(sh_ref, r)).astype(BF16) for r in range(nsub)]
    ys = [[None, None] for _ in range(nsub)]
    for half in range(2):
        cols = slice(half * D_FF, (half + 1) * D_FF)
        u = [_dot(h[r], wup_ref[:, cols]) + bup_ref[:, cols] for r in range(nsub)]
        ubuf[pl.ds(0, halo), :] = cs_ref[:, cols]
        for r in range(nsub):
            ubuf[pl.ds(halo + r * sub, sub), :] = u[r]
        cs_ref[:, cols] = ubuf[pl.ds(tm, halo), :]
        for r in range(nsub):
            y = cbias_ref[:, cols] + u[r] * cw_ref[CONV_W - 1:CONV_W, cols]
            for j in range(CONV_W - 1):
                tap = ubuf[pl.ds(halo + r * sub - (CONV_W - 1 - j) * stride, sub), :]
                y = y + tap * cw_ref[j:j + 1, cols]
            ys[r][half] = y
    for r in range(nsub):
        act[rows[r], :] = (jax.nn.gelu(ys[r][0]) * ys[r][1]).astype(BF16)
    f = [_dot(act[rows[r], :], wdn_ref[...]) + bdn_ref[...] for r in range(nsub)]
    for r in range(nsub):
        o_ref[rows[r], :] = _ln(alpha * x[r] + mod(g2_ref, r) * f[r]) * lg_ref[...] + lb_ref[...]


def _ffn(x, sh, sc, g2, cb0, w, tm, stride, alpha):
    bsz, seq, _ = x.shape
    (sh, sh_spec), (sc, sc_spec), (g2, g2_spec) = _mod_operand(sh), _mod_operand(sc), _mod_operand(g2)
    halo = cb0.shape[1]
    cs = pl.BlockSpec((None, halo, 2 * D_FF), lambda b, i: (b, 0, 0))
    weights = [w["wup"], w["bup"], w["cw"], w["cbias"], w["wdn"], w["bdn"], w["ln2_g"], w["ln2_b"]]
    return pl.pallas_call(
        functools.partial(_ffn_body, alpha, tm, stride, halo),
        out_shape=[jax.ShapeDtypeStruct((bsz, seq, D_MODEL), F32),
                   jax.ShapeDtypeStruct((bsz, halo, 2 * D_FF), F32)],
        grid=(bsz, seq // tm),
        in_specs=[_tok_spec(tm, D_MODEL), sh_spec, sc_spec, g2_spec, cs]
                 + [_const_spec(a.shape) for a in weights],
        out_specs=[_tok_spec(tm, D_MODEL), cs],
        scratch_shapes=[pltpu.VMEM((halo + tm, D_FF), F32), pltpu.VMEM((tm, D_FF), BF16)],
        compiler_params=_params(2),
        name="ffn",
    )(x, sh, sc, g2, cb0, *weights)


_O_GATE = 2 * DQK_M + DV_M
_O_OG = _O_GATE + 2 * NH_M
_O_QA = _O_OG + DV_M
_O_KA = _O_QA + DQ_A
_O_VA = _O_KA + DKV_A
_O_GM = _O_VA + DKV_A
_PREP_ROWS = 256


def _split_w_in_body(wt_ref, wbm32_ref, wba32_ref, wo32_ref, qk_ref, vm_ref, vmt_ref, g_ref, gates_ref,
                     qa_ref, k_ref, v_ref, vt_ref, wbm_ref, wba_ref, wo_ref):
    for src, dst in ((wbm32_ref, wbm_ref), (wba32_ref, wba_ref), (wo32_ref, wo_ref)):
        dst[...] = src[...].astype(BF16)
    piece = lambda lo, hi: wt_ref[lo:hi, :]
    qk_ref[...] = piece(0, 2 * DQK_M).T.astype(BF16)
    vm = piece(2 * DQK_M, _O_GATE)
    vmt_ref[...] = vm.astype(BF16)
    vm_ref[...] = vm.T.astype(BF16)
    pad = jnp.zeros((LANES - 2 * NH_M, _PREP_ROWS), F32)
    g_ref[...] = jnp.concatenate([piece(_O_GATE, _O_OG), pad], axis=0).T.astype(BF16)
    gates_ref[:, :DV_M] = piece(_O_OG, _O_QA).T.astype(BF16)
    gates_ref[:, DV_M:] = piece(_O_GM, wt_ref.shape[0]).T.astype(BF16)
    qa_ref[...] = piece(_O_QA, _O_KA).T.astype(BF16)
    k_ref[...] = piece(_O_KA, _O_VA).T.astype(BF16)
    v = piece(_O_VA, _O_GM)
    vt_ref[...] = v.astype(BF16)
    v_ref[...] = v.T.astype(BF16)


def _split_w_in(w_in_t, w_branch_m, w_branch_a, w_out):
    d_in = w_in_t.shape[0]
    rows = lambda n: ((D_MODEL, n), pl.BlockSpec((_PREP_ROWS, n), lambda i: (i, 0)))
    cols = lambda n: ((n, D_MODEL), pl.BlockSpec((n, _PREP_ROWS), lambda i: (0, i)))
    outs = dict(wqk=rows(2 * DQK_M), wvm=rows(DV_M), wvm_t=cols(DV_M), wg=rows(LANES),
                wgates=rows(DV_M + d_in - _O_GM), wqa=rows(DQ_A), wk=rows(DKV_A), wv=rows(DKV_A),
                wv_t=cols(DKV_A), wbm=rows(D_MODEL), wba=rows(D_MODEL), wo=rows(D_MODEL))
    res = pl.pallas_call(
        _split_w_in_body,
        out_shape=[jax.ShapeDtypeStruct(s, BF16) for s, _ in outs.values()],
        grid=(D_MODEL // _PREP_ROWS,),
        in_specs=[pl.BlockSpec((d_in, _PREP_ROWS), lambda i: (0, i))] + [rows(D_MODEL)[1]] * 3,
        out_specs=[spec for _, spec in outs.values()],
        compiler_params=_params(1),
        name="split_w_in",
    )(w_in_t, w_branch_m, w_branch_a, w_out)
    return dict(zip(outs.keys(), res))


def _prep_weights(w_in, b_in, mlstm_norm_w, w_branch_m, w_branch_a, w_out, ln1_g, ln1_b,
                  w_up, b_up, conv_w, conv_b, w_down, b_down, ln2_g, ln2_b):
    row = lambda a: a.reshape(1, -1)
    gate_pad = LANES - 2 * NH_M
    b_k, b_v = b_in[_O_KA:_O_VA], b_in[_O_VA:_O_GM]
    return dict(
        **_split_w_in(w_in.T, w_branch_m, w_branch_a, w_out),
        bqk=row(b_in[:2 * DQK_M]),
        bvm_row=row(b_in[2 * DQK_M:_O_GATE]), bvm_col=b_in[2 * DQK_M:_O_GATE].reshape(-1, 1),
        bg=row(jnp.pad(b_in[_O_GATE:_O_OG], (0, gate_pad))),
        bqa=row(b_in[_O_QA:_O_KA]),
        bk=row(b_k),
        bv_row=row(b_v), bv_col=b_v.reshape(-1, 1),
        bgates=row(jnp.concatenate([b_in[_O_OG:_O_QA], b_in[_O_GM:]])),
        norm_w=row(mlstm_norm_w), norm_w_col=mlstm_norm_w.reshape(-1, 1),
        ln1_g=row(ln1_g), ln1_b=row(ln1_b),
        wup=w_up.astype(BF16), bup=row(b_up), cw=conv_w, cbias=row(conv_b),
        wdn=w_down.astype(BF16), bdn=row(b_down), ln2_g=row(ln2_g), ln2_b=row(ln2_b))


def kernel(x_prompt, x_sample, c_prompt, c_sample, state_mlstm_C, state_mlstm_n, state_mlstm_m,
           cache_k_win, cache_v_win, state_ffn_conv, w_ada, b_ada, w_in, b_in, mlstm_norm_w,
           attn_sinks, w_branch_m, w_branch_a, w_out, ln1_g, ln1_b, w_up, b_up, conv_w, conv_b,
           w_down, b_down, ln2_g, ln2_b):
    depth = w_in.shape[0]
    bp, lp, _ = x_prompt.shape
    bs, ls, _ = x_sample.shape
    assert cache_k_win.shape[2] == WINDOW
    alpha = (2 * depth) ** 0.25
    dt = x_prompt.dtype
    slopes = jnp.exp2(-8.0 * jnp.arange(1, NH_A + 1, dtype=F32) / NH_A)
    tm_p = 512
    tm_in = 1024
    ns = bs * ls

    yp = x_prompt
    to_pos = lambda a: a.transpose(1, 0, 2).reshape(1, -1, a.shape[-1])
    ys = to_pos(x_sample)
    new_p, new_s = [], []
    n_c = bp + bs
    c_rows = -(-n_c // SUBLANES) * SUBLANES
    c_all = jnp.concatenate([c_sample, c_prompt, jnp.zeros((c_rows - n_c, D_MODEL), dt)], axis=0)
    for l in range(depth):
        w = _prep_weights(w_in[l], b_in[l], mlstm_norm_w[l], w_branch_m[l], w_branch_a[l], w_out[l],
                          ln1_g[l], ln1_b[l], w_up[l], b_up[l], conv_w[l], conv_b[l], w_down[l],
                          b_down[l], ln2_g[l], ln2_b[l])
        prm = jnp.stack([slopes, attn_sinks[l].astype(F32)]) * LOG2E
        mod = _ada(c_all, w_ada[l], b_ada[l])
        mod_p = mod[bs:bs + bp].reshape(bp, N_MOD, D_MODEL)
        mp_ = [mod_p[:, j:j + 1] for j in range(N_MOD)]
        ms_ = [(mod, j, bs) for j in range(N_MOD)]

        hp, qk, v, g, qa, kb, ka, vat = _inproj(yp, mp_[0], mp_[1], w, tm_in, True)
        hm, cp, np_, mp = _mlstm_prompt(qk, v, g, w["norm_w_col"])
        ha = _swa_prompt(prm, qa, kb, vat)
        x1p = _merge(yp, hp, mp_[2], hm, ha, w, tm_p, alpha)
        halo_p = SUBLANES
        yp, csp = _ffn(x1p, mp_[3], mp_[4], mp_[5], jnp.zeros((bp, halo_p, 2 * D_FF), dt),
                       w, tm_p, 1, alpha)
        p_k = ka[:, lp - WINDOW:].reshape(bp, WINDOW, NKV_A, HD_A)
        p_v = vat[:, :, lp - WINDOW:].reshape(bp, NKV_A, HD_A, WINDOW).transpose(0, 3, 1, 2)
        new_p.append((cp, np_, mp[:, 0, :NH_M], p_k, p_v, csp[:, halo_p - (CONV_W - 1):]))

        hs, qk, v, g, qa, kn, vn = _inproj(ys, ms_[0], ms_[1], w, ns, False)
        per_seq = lambda a: a.reshape(ls, bs, a.shape[-1])
        m0 = jnp.pad(state_mlstm_m[l], ((0, 0), (0, LANES - NH_M)))[:, None, :]
        hm, cs_, ns_, ms = _mlstm_sample(per_seq(qk), per_seq(v), per_seq(g), state_mlstm_C[l],
                                         state_mlstm_n[l], m0, w["norm_w"], SEQS_PER_STEP)
        to_t = lambda a: a.transpose(0, 2, 3, 1).reshape(bs, DKV_A, WINDOW)
        from_t = lambda a: a.reshape(bs, NKV_A, HD_A, WINDOW).transpose(0, 3, 1, 2)
        ha, kct, vct = _swa_sample(prm, per_seq(qa), per_seq(kn), per_seq(vn),
                                   to_t(cache_k_win[l]), to_t(cache_v_win[l]), SWA_SEQS_PER_STEP)
        x1s = _merge(ys, hs, ms_[2], hm.reshape(1, ns, DV_M), ha.reshape(1, ns, DQ_A), w, ns, alpha)
        ys, css = _ffn(x1s, ms_[3], ms_[4], ms_[5], to_pos(state_ffn_conv[l]), w, ns, bs, alpha)
        conv_s = css.reshape(CONV_W - 1, bs, 2 * D_FF).transpose(1, 0, 2)
        new_s.append((cs_, ns_, ms[:, 0, :NH_M], from_t(kct), from_t(vct), conv_s))

    p_state = [jnp.stack(a) for a in zip(*new_p)]
    s_state = [jnp.stack(a) for a in zip(*new_s)]
    y_sample = ys.reshape(ls, bs, D_MODEL).transpose(1, 0, 2)
    return (yp, y_sample, *p_state, *s_state)
```
